```python
import math
import jax, jax.numpy as jnp
from jax import lax
import numpy as np

D_MODEL = 1024
BATCH = 2
SEQ = 8192
DEPTH = 1
DEC_BATCH = 128
DEC_SEQ = 1
PAST_LEN = 8192
PAGE_SIZE = 128

SSM_WIDTH = D_MODEL
SSM_GROUP = 16
SSM_GROUPS = SSM_WIDTH // SSM_GROUP
SSM_STATE = 64
SSM_CHUNK = 128
N_HEADS = 16
HEAD_DIM = D_MODEL // N_HEADS
N_KV_HEADS = 4
KV_GROUP = N_HEADS // N_KV_HEADS
ATTN_WIDTH = N_HEADS * HEAD_DIM
KV_WIDTH = N_KV_HEADS * HEAD_DIM
WINDOW = 128
ATTN_BLOCK = WINDOW
N_BUCKETS = 32
MAX_DISTANCE = WINDOW
NEG_INF = -1e30
LN_EPS = 1e-5
DEEPNORM_ALPHA = (2 * DEPTH) ** 0.25
DEEPNORM_BETA = (8 * DEPTH) ** -0.25
SPLITS = (SSM_WIDTH, SSM_WIDTH, ATTN_WIDTH, KV_WIDTH, KV_WIDTH, ATTN_WIDTH, D_MODEL, D_MODEL)
D_IN = SSM_WIDTH * 2 + ATTN_WIDTH * 2 + KV_WIDTH * 2 + D_MODEL * 2

kernel_name = "hybrid_s5_swa_gated_decoder_step"


def layer_norm(x, g, b):
    xf = x.astype(jnp.float32)
    mu = xf.mean(-1, keepdims=True)
    var = jnp.square(xf - mu).mean(-1, keepdims=True)
    return ((xf - mu) * lax.rsqrt(var + LN_EPS) * g.astype(jnp.float32) + b.astype(jnp.float32)).astype(x.dtype)


def rel_bucket(dist):
    max_exact = N_BUCKETS // 2
    df = jnp.maximum(dist, 1).astype(jnp.float32)
    large = max_exact + (jnp.log(df / max_exact) / math.log(MAX_DISTANCE / max_exact)
                         * (N_BUCKETS - max_exact)).astype(jnp.int32)
    large = jnp.minimum(large, N_BUCKETS - 1)
    return jnp.where(dist < max_exact, dist, large)


def rel_bias_from_dist(dist, rel_bias):
    b = rel_bias.astype(jnp.float32)[rel_bucket(jnp.clip(dist, 0, WINDOW))]
    return jnp.transpose(b, (2, 0, 1)).reshape(N_KV_HEADS, KV_GROUP, *dist.shape)


def sink_softmax(s, sinks):
    sk = sinks.astype(jnp.float32).reshape(N_KV_HEADS, KV_GROUP, 1)
    m = jnp.maximum(s.max(-1), sk)
    e = jnp.exp(s - m[..., None])
    return e / (e.sum(-1, keepdims=True) + jnp.exp(sk - m)[..., None])


def swa_prompt(q, k, v, sinks, rel_bias):
    b, l = q.shape[:2]
    nb = l // ATTN_BLOCK
    qb = q.reshape(b, nb, ATTN_BLOCK, N_KV_HEADS, KV_GROUP, HEAD_DIM)

    def band(t):
        tb = t.reshape(b, nb, ATTN_BLOCK, N_KV_HEADS, HEAD_DIM)
        prev = jnp.concatenate([jnp.zeros_like(tb[:, :1]), tb[:, :-1]], axis=1)
        return jnp.concatenate([prev, tb], axis=2)

    kk, vv = band(k), band(v)
    s = jnp.einsum('bnqkgd,bnskd->bnkgqs', qb, kk, preferred_element_type=jnp.float32) * (HEAD_DIM ** -0.5)
    qi = jnp.arange(ATTN_BLOCK)[:, None]
    kj = jnp.arange(2 * ATTN_BLOCK)[None, :]
    dist = qi + ATTN_BLOCK - kj
    key_pos = jnp.arange(nb)[:, None, None] * ATTN_BLOCK + kj[None] - ATTN_BLOCK
    mask = ((dist >= 0) & (dist <= WINDOW))[None] & (key_pos >= 0)
    s = s + rel_bias_from_dist(dist, rel_bias)
    s = jnp.where(mask[None, :, None, None], s, NEG_INF)
    p = sink_softmax(s, sinks)
    o = jnp.einsum('bnkgqs,bnskd->bnqkgd', p.astype(vv.dtype), vv)
    return o.reshape(b, l, ATTN_WIDTH)


def swa_sample(q, k, v, k_buf, v_buf, sinks, rel_bias):
    b, t = q.shape[:2]
    qg = q.reshape(b, t, N_KV_HEADS, KV_GROUP, HEAD_DIM)
    kk = jnp.concatenate([k_buf.astype(k.dtype), k], axis=1)
    vv = jnp.concatenate([v_buf.astype(v.dtype), v], axis=1)
    s = jnp.einsum('btkgd,bskd->bkgts', qg, kk, preferred_element_type=jnp.float32) * (HEAD_DIM ** -0.5)
    dist = jnp.arange(t)[:, None] + WINDOW - jnp.arange(WINDOW + t)[None, :]
    mask = (dist >= 0) & (dist <= WINDOW)
    s = s + rel_bias_from_dist(dist, rel_bias)
    s = jnp.where(mask, s, NEG_INF)
    p = sink_softmax(s, sinks)
    o = jnp.einsum('bkgts,bskd->btkgd', p.astype(vv.dtype), vv)
    return o.reshape(b, t, ATTN_WIDTH), kk[:, -WINDOW:], vv[:, -WINDOW:]


def ssm_discretise(lam_re, lam_im, log_delta, b_re, b_im):
    lr = lam_re.astype(jnp.float32)
    li = lam_im.astype(jnp.float32)
    dt = jnp.exp(log_delta.astype(jnp.float32))[:, None]
    mag = jnp.exp(lr * dt)
    ar, ai = mag * jnp.cos(li * dt), mag * jnp.sin(li * dt)
    den = lr * lr + li * li
    nr = ar - 1.0
    cr = (nr * lr + ai * li) / den
    ci = (ai * lr - nr * li) / den
    br, bi = b_re.astype(jnp.float32), b_im.astype(jnp.float32)
    bbr = cr[..., None] * br - ci[..., None] * bi
    bbi = cr[..., None] * bi + ci[..., None] * br
    return ar, ai, bbr, bbi


def complex_combine(e1, e2):
    a1r, a1i, b1r, b1i = e1
    a2r, a2i, b2r, b2i = e2
    return (a2r * a1r - a2i * a1i,
            a2r * a1i + a2i * a1r,
            a2r * b1r - a2i * b1i + b2r,
            a2r * b1i + a2i * b1r + b2i)


def ssm_segment(u, hr, hi, ar, ai, bbr, bbi, c_re, c_im):
    xr = jnp.einsum('btgc,gpc->btgp', u, bbr)
    xi = jnp.einsum('btgc,gpc->btgp', u, bbi)
    xr = xr.at[:, 0].add(ar * hr - ai * hi)
    xi = xi.at[:, 0].add(ar * hi + ai * hr)
    shp = xr.shape
    _, _, sr, si = lax.associative_scan(
        complex_combine, (jnp.broadcast_to(ar, shp), jnp.broadcast_to(ai, shp), xr, xi), axis=1)
    y = (jnp.einsum('btgp,gcp->btgc', sr, c_re.astype(jnp.float32))
         - jnp.einsum('btgp,gcp->btgc', si, c_im.astype(jnp.float32)))
    return y, sr[:, -1], si[:, -1]


def ssm_mixer(u, h0r, h0i, lam_re, lam_im, log_delta, b_re, b_im, c_re, c_im, d_skip):
    b, l, _ = u.shape
    uf = u.astype(jnp.float32)
    ar, ai, bbr, bbi = ssm_discretise(lam_re, lam_im, log_delta, b_re, b_im)
    chunk = SSM_CHUNK if l % SSM_CHUNK == 0 else l
    nc = l // chunk
    xs = uf.reshape(b, nc, chunk, SSM_GROUPS, SSM_GROUP).transpose(1, 0, 2, 3, 4)

    def step(carry, uc):
        y, hr, hi = ssm_segment(uc, carry[0], carry[1], ar, ai, bbr, bbi, c_re, c_im)
        return (hr, hi), y

    (hr, hi), ys = lax.scan(step, (h0r.astype(jnp.float32), h0i.astype(jnp.float32)), xs)
    y = ys.transpose(1, 0, 2, 3, 4).reshape(b, l, SSM_WIDTH)
    return y + d_skip.astype(jnp.float32) * uf, hr, hi


def decoder_layer(x, c, h0r, h0i, k_buf, v_buf, lp, rel_bias):
    (w_ada, b_ada, w_in, lam_re, lam_im, log_delta, b_re, b_im, c_re, c_im, d_skip,
     w_glu, b_glu, sinks, w_branch_s, w_branch_a, w_out, ln_g, ln_b) = lp
    b, l, _ = x.shape
    mod = jax.nn.silu(c) @ w_ada + b_ada
    shift, scale, gate = jnp.split(mod, 3, axis=-1)
    h = x * (1.0 + scale[:, None]) + shift[:, None]
    proj = h @ w_in
    points = [int(p) for p in np.cumsum(SPLITS)[:-1]]
    u_s, z_s, q, k, v, z_a, g_s, g_a = jnp.split(proj, points, axis=-1)
    y_s, hr, hi = ssm_mixer(u_s, h0r, h0i, lam_re, lam_im, log_delta, b_re, b_im, c_re, c_im, d_skip)
    y_s = jax.nn.gelu(y_s)
    y_s = y_s * jax.nn.sigmoid(y_s @ w_glu.astype(jnp.float32) + b_glu.astype(jnp.float32))
    y_s = (y_s * jax.nn.silu(z_s.astype(jnp.float32))).astype(x.dtype)
    b_s = y_s @ w_branch_s
    k = k.reshape(b, l, N_KV_HEADS, HEAD_DIM)
    v = v.reshape(b, l, N_KV_HEADS, HEAD_DIM)
    if k_buf is None:
        o_a = swa_prompt(q, k, v, sinks, rel_bias)
        new_k, new_v = k[:, -WINDOW:], v[:, -WINDOW:]
    else:
        o_a, new_k, new_v = swa_sample(q, k, v, k_buf, v_buf, sinks, rel_bias)
    b_a = (o_a * jax.nn.silu(z_a)) @ w_branch_a
    m = jax.nn.sigmoid(g_s) * b_s + jax.nn.sigmoid(g_a) * b_a
    out = m @ w_out
    y = layer_norm(DEEPNORM_ALPHA * x + gate[:, None] * out, ln_g, ln_b)
    return y, hr, hi, new_k, new_v


def setup_inputs(seed: int = 0) -> dict:
    key = jax.random.key(seed)
    ks = iter(jax.random.split(key, 40))
    nrm = lambda shape, s: jax.random.normal(next(ks), shape, jnp.float32) * s
    inputs = {}
    inputs['x_prompt'] = nrm((BATCH, SEQ, D_MODEL), 1.0)
    inputs['x_sample'] = nrm((DEC_BATCH, DEC_SEQ, D_MODEL), 1.0)
    inputs['c_prompt'] = nrm((BATCH, D_MODEL), 1.0)
    inputs['c_sample'] = nrm((DEC_BATCH, D_MODEL), 1.0)
    inputs['state_ssm_re'] = nrm((DEPTH, DEC_BATCH, SSM_GROUPS, SSM_STATE), 0.3)
    inputs['state_ssm_im'] = nrm((DEPTH, DEC_BATCH, SSM_GROUPS, SSM_STATE), 0.3)
    inputs['cache_swa_k'] = nrm((DEPTH, DEC_BATCH, WINDOW, N_KV_HEADS, HEAD_DIM), 1.0)
    inputs['cache_swa_v'] = nrm((DEPTH, DEC_BATCH, WINDOW, N_KV_HEADS, HEAD_DIM), 1.0)
    inputs['w_ada'] = nrm((DEPTH, D_MODEL, 3 * D_MODEL), 0.5 * D_MODEL ** -0.5)
    inputs['b_ada'] = nrm((DEPTH, 3 * D_MODEL), 0.02)
    inputs['w_in'] = nrm((DEPTH, D_MODEL, D_IN), D_MODEL ** -0.5)
    inputs['ssm_lambda_re'] = -0.5 + nrm((DEPTH, SSM_GROUPS, SSM_STATE), 0.01)
    inputs['ssm_lambda_im'] = (jnp.pi * jnp.arange(SSM_STATE, dtype=jnp.float32))[None, None] + nrm((DEPTH, SSM_GROUPS, SSM_STATE), 0.01)
    inputs['ssm_log_delta'] = jax.random.uniform(next(ks), (DEPTH, SSM_GROUPS), jnp.float32, math.log(1e-3), math.log(1e-1))
    inputs['ssm_b_re'] = nrm((DEPTH, SSM_GROUPS, SSM_STATE, SSM_GROUP), (2 * SSM_GROUP) ** -0.5)
    inputs['ssm_b_im'] = nrm((DEPTH, SSM_GROUPS, SSM_STATE, SSM_GROUP), (2 * SSM_GROUP) ** -0.5)
    inputs['ssm_c_re'] = nrm((DEPTH, SSM_GROUPS, SSM_GROUP, SSM_STATE), SSM_STATE ** -0.5)
    inputs['ssm_c_im'] = nrm((DEPTH, SSM_GROUPS, SSM_GROUP, SSM_STATE), SSM_STATE ** -0.5)
    inputs['ssm_d'] = nrm((DEPTH, SSM_WIDTH), 1.0)
    inputs['w_glu'] = nrm((DEPTH, SSM_WIDTH, SSM_WIDTH), SSM_WIDTH ** -0.5)
    inputs['b_glu'] = nrm((DEPTH, SSM_WIDTH), 0.02)
    inputs['attn_sinks'] = nrm((DEPTH, N_HEADS), 0.5)
    inputs['rel_bias'] = nrm((N_BUCKETS, N_HEADS), 0.1)
    inputs['w_branch_s'] = nrm((DEPTH, SSM_WIDTH, D_MODEL), SSM_WIDTH ** -0.5)
    inputs['w_branch_a'] = nrm((DEPTH, ATTN_WIDTH, D_MODEL), ATTN_WIDTH ** -0.5)
    inputs['w_out'] = nrm((DEPTH, D_MODEL, D_MODEL), DEEPNORM_BETA * D_MODEL ** -0.5)
    inputs['ln_g'] = 1.0 + nrm((DEPTH, D_MODEL), 0.02)
    inputs['ln_b'] = nrm((DEPTH, D_MODEL), 0.02)
    return inputs


def reference(x_prompt, x_sample, c_prompt, c_sample, state_ssm_re, state_ssm_im, cache_swa_k, cache_swa_v,
              w_ada, b_ada, w_in, ssm_lambda_re, ssm_lambda_im, ssm_log_delta, ssm_b_re, ssm_b_im,
              ssm_c_re, ssm_c_im, ssm_d, w_glu, b_glu, attn_sinks, rel_bias, w_branch_s, w_branch_a,
              w_out, ln_g, ln_b):
    yp, ys = x_prompt, x_sample
    p_hr, p_hi, p_k, p_v = [], [], [], []
    s_hr, s_hi, s_k, s_v = [], [], [], []
    zeros_state = jnp.zeros((x_prompt.shape[0], SSM_GROUPS, SSM_STATE), jnp.float32)
    for i in range(DEPTH):
        lp = (w_ada[i], b_ada[i], w_in[i], ssm_lambda_re[i], ssm_lambda_im[i], ssm_log_delta[i],
              ssm_b_re[i], ssm_b_im[i], ssm_c_re[i], ssm_c_im[i], ssm_d[i], w_glu[i], b_glu[i],
              attn_sinks[i], w_branch_s[i], w_branch_a[i], w_out[i], ln_g[i], ln_b[i])
        yp, hr, hi, nk, nv = decoder_layer(yp, c_prompt, zeros_state, zeros_state, None, None, lp, rel_bias)
        p_hr.append(hr); p_hi.append(hi); p_k.append(nk); p_v.append(nv)
        ys, hr, hi, nk, nv = decoder_layer(ys, c_sample, state_ssm_re[i], state_ssm_im[i],
                                           cache_swa_k[i], cache_swa_v[i], lp, rel_bias)
        s_hr.append(hr); s_hi.append(hi); s_k.append(nk); s_v.append(nv)
    return (yp, ys,
            jnp.stack(p_hr), jnp.stack(p_hi), jnp.stack(p_k), jnp.stack(p_v),
            jnp.stack(s_hr), jnp.stack(s_hi), jnp.stack(s_k), jnp.stack(s_v))
```

```python
import functools
import math

import numpy as np
import jax
import jax.numpy as jnp
from jax import lax
from jax.experimental import pallas as pl
from jax.experimental.pallas import tpu as pltpu

F32 = jnp.float32
BF16 = jnp.bfloat16

D_MODEL = 1024
SSM_GROUPS = 64
SSM_GROUP = 16
SSM_STATE = 64
N_HEADS = 16
HEAD_DIM = 64
N_KV_HEADS = 4
KV_GROUP = 4
KV_WIDTH = N_KV_HEADS * HEAD_DIM
WINDOW = 128
N_BUCKETS = 32
NEG_INF = -1e30
LN_EPS = 1e-5
DEPTH = 1
DEEPNORM_ALPHA = (2 * DEPTH) ** 0.25
D_IN = 6656
CHUNK = 16
CW = CHUNK * SSM_GROUP
SW = 2 * SSM_STATE

COL_U, COL_ZS, COL_Q, COL_ZA, COL_GS, COL_GA = 0, 1, 2, 3, 4, 5
COL_K, COL_V = 24, 25

VMEM_LIMIT = 56 * 1024 * 1024


def _head_perm():
    return np.arange(N_HEADS * HEAD_DIM).reshape(N_KV_HEADS, KV_GROUP, HEAD_DIM).transpose(1, 0, 2).reshape(-1)


def _sigmoid(x):
    return 1.0 / (1.0 + jnp.exp(-x))


def _silu(x):
    return x * _sigmoid(x)


def _gelu_tanh(x):
    return 0.5 * x * (1.0 + jnp.tanh(math.sqrt(2.0 / math.pi) * (x + 0.044715 * (x * x * x))))


def _dot(a, b):
    return jnp.dot(a, b, preferred_element_type=F32)


def _dot_nt(a, b):
    return lax.dot_general(a, b, (((1,), (1,)), ((), ())), preferred_element_type=F32)


def _ada_body(c_ref, w_ref, b_ref, o_ref):
    c = c_ref[...]
    sc = _silu(c).astype(BF16)
    o_ref[...] = _dot(sc, w_ref[...].astype(BF16)) + b_ref[...]


def _ada(c_all, w_ada, b_ada):
    rows = c_all.shape[0]
    bn = 512
    return pl.pallas_call(
        _ada_body,
        grid=(3 * D_MODEL // bn,),
        in_specs=[pl.BlockSpec((rows, D_MODEL), lambda j: (0, 0)),
                  pl.BlockSpec((D_MODEL, bn), lambda j: (0, j)),
                  pl.BlockSpec((1, bn), lambda j: (0, j))],
        out_specs=pl.BlockSpec((rows, bn), lambda j: (0, j)),
        out_shape=jax.ShapeDtypeStruct((rows, 3 * D_MODEL), F32),
        name="ada",
    )(c_all, w_ada, b_ada.reshape(1, -1))


PROJ_CB = 512


def _proj_body(x_ref, shift_ref, scale_ref, w_ref, o_ref):
    h = x_ref[...] * (1.0 + scale_ref[...]) + shift_ref[...]
    hb = h.astype(BF16)
    for j in range(D_IN // PROJ_CB):
        sl = slice(j * PROJ_CB, (j + 1) * PROJ_CB)
        o_ref[:, sl] = _dot(hb, w_ref[:, sl]).astype(BF16)


def _proj(x2d, shift_arr, scale_arr, shift_spec, scale_spec, w_in_b, tm):
    rows = x2d.shape[0]
    return pl.pallas_call(
        _proj_body,
        grid=(rows // tm,),
        in_specs=[pl.BlockSpec((tm, D_MODEL), lambda i: (i, 0)),
                  shift_spec, scale_spec,
                  pl.BlockSpec((D_MODEL, D_IN), lambda i: (0, 0), pipeline_mode=pl.Buffered(1))],
        out_specs=pl.BlockSpec((tm, D_IN), lambda i: (i, 0)),
        out_shape=jax.ShapeDtypeStruct((rows, D_IN), BF16),
        compiler_params=pltpu.CompilerParams(vmem_limit_bytes=VMEM_LIMIT),
        name="proj",
    )(x2d, shift_arr, scale_arr, w_in_b)


def _cmul(c1, c2, x):
    return c1 * x + c2 * pltpu.roll(x, SSM_STATE, 1)


def _ssm_body(z_ref, rhs_ref, cpow_ref, coef_ref, y_ref, hfin_ref, s_scr, hp_scr, *, n_batch, nk):
    out = _dot(z_ref[0], rhs_ref[0])
    s_scr[...] = out[:, CW:]
    coef = coef_ref[0]
    row = lax.broadcasted_iota(jnp.int32, (8, SW), 0)

    for b in range(n_batch):
        def body(r, h, b=b):
            base = pl.multiple_of(b * nk + r * 8, 8)
            blk = s_scr[pl.ds(base, 8), :]
            for i, d in enumerate((1, 2, 4)):
                sh = jnp.where(row >= d, pltpu.roll(blk, d, 0), 0.0)
                blk = blk + _cmul(coef[2 * i:2 * i + 1], coef[2 * i + 1:2 * i + 2], sh)
            hb = jnp.broadcast_to(h, (8, SW))
            blk = blk + _cmul(coef[8:16], coef[16:24], hb)
            hp_scr[pl.ds(base, 8), :] = jnp.where(row == 0, hb, pltpu.roll(blk, 1, 0))
            return blk[7:8, :]

        hfin = lax.fori_loop(0, nk // 8, body, jnp.zeros((1, SW), F32))
        hfin_ref[0, b:b + 1, :] = hfin

    y_ref[0] = out[:, :CW] + _dot(hp_scr[...].astype(BF16), cpow_ref[0])


def _ssm_prompt(z, rhs, cpow, coef, n_batch):
    g, rows, _ = z.shape
    nk = rows // n_batch
    return pl.pallas_call(
        functools.partial(_ssm_body, n_batch=n_batch, nk=nk),
        grid=(g,),
        in_specs=[pl.BlockSpec((1, rows, CW), lambda i: (i, 0, 0)),
                  pl.BlockSpec((1, CW, CW + SW), lambda i: (i, 0, 0)),
                  pl.BlockSpec((1, SW, CW), lambda i: (i, 0, 0)),
                  pl.BlockSpec((1, 24, SW), lambda i: (i, 0, 0))],
        out_specs=[pl.BlockSpec((1, rows, CW), lambda i: (i, 0, 0)),
                   pl.BlockSpec((1, n_batch, SW), lambda i: (i, 0, 0))],
        out_shape=[jax.ShapeDtypeStruct((g, rows, CW), F32),
                   jax.ShapeDtypeStruct((g, n_batch, SW), F32)],
        scratch_shapes=[pltpu.VMEM((rows, SW), F32), pltpu.VMEM((rows, SW), F32)],
        name="ssm_prompt",
    )(z, rhs, cpow, coef)


def _post(ys, zs, o_a, za, gs, ga, x, gate, wglu_ref, bglu_ref, wbs_ref, wba_ref, wout_ref, lng_ref, lnb_ref):
    ys = _gelu_tanh(ys)
    glu = ys * _sigmoid(_dot(ys.astype(BF16), wglu_ref[...]) + bglu_ref[...])
    t = (glu * _silu(zs.astype(F32))).astype(BF16)
    b_s = _dot(t, wbs_ref[...])
    oz = (o_a * _silu(za.astype(F32))).astype(BF16)
    b_a = _dot(oz, wba_ref[...])
    m = _sigmoid(gs.astype(F32)) * b_s + _sigmoid(ga.astype(F32)) * b_a
    out = _dot(m.astype(BF16), wout_ref[...])
    r = DEEPNORM_ALPHA * x + gate * out
    mu = jnp.mean(r, axis=-1, keepdims=True)
    rc = r - mu
    var = jnp.mean(rc * rc, axis=-1, keepdims=True)
    return rc * lax.rsqrt(var + LN_EPS) * lng_ref[...] + lnb_ref[...]


TB = 256
NQB = TB // WINDOW


def _final_body(sinks_ref, x_ref, yssm_ref, zs_ref, q_ref, za_ref, gs_ref, ga_ref,
                kc_ref, kp_ref, vc_ref, vp_ref, gate_ref, bias_ref,
                wglu_ref, bglu_ref, wbs_ref, wba_ref, wout_ref, lng_ref, lnb_ref,
                y_ref, o_scr):
    is_first = pl.program_id(1) == 0
    qi = lax.broadcasted_iota(jnp.int32, (KV_GROUP * WINDOW, 2 * WINDOW), 0) % WINDOW
    kj = lax.broadcasted_iota(jnp.int32, (KV_GROUP * WINDOW, 2 * WINDOW), 1)
    dist = qi + WINDOW - kj
    band = (dist >= 0) & (dist <= WINDOW)
    band_first = band & ((kj >= WINDOW) | jnp.logical_not(is_first))

    for blk in range(NQB):
        rows = slice(blk * WINDOW, (blk + 1) * WINDOW)
        if blk == 0:
            kp, vp, mask = kp_ref[...], vp_ref[...], band_first
        else:
            prev = slice((blk - 1) * WINDOW, blk * WINDOW)
            kp, vp, mask = kc_ref[prev, :], vc_ref[prev, :], band
        kcat = jnp.concatenate([kp, kc_ref[rows, :]], axis=0)
        vcat = jnp.concatenate([vp, vc_ref[rows, :]], axis=0)
        qb = q_ref[rows, :]
        for kh in range(N_KV_HEADS):
            hs = slice(kh * HEAD_DIM, (kh + 1) * HEAD_DIM)
            qs = jnp.concatenate(
                [qb[:, j * KV_WIDTH + kh * HEAD_DIM: j * KV_WIDTH + (kh + 1) * HEAD_DIM] for j in range(KV_GROUP)],
                axis=0)
            s = _dot_nt(qs, kcat[:, hs]) * (HEAD_DIM ** -0.5) + bias_ref[kh]
            s = jnp.where(mask, s, NEG_INF)
            sink = jnp.concatenate(
                [jnp.full((WINDOW, 1), sinks_ref[kh * KV_GROUP + j], F32) for j in range(KV_GROUP)], axis=0)
            m = jnp.maximum(jnp.max(s, axis=-1, keepdims=True), sink)
            e = jnp.exp(s - m)
            den = jnp.sum(e, axis=-1, keepdims=True) + jnp.exp(sink - m)
            o = _dot(e.astype(BF16), vcat[:, hs]) / den
            for j in range(KV_GROUP):
                c0 = j * KV_WIDTH + kh * HEAD_DIM
                o_scr[rows, c0:c0 + HEAD_DIM] = o[j * WINDOW:(j + 1) * WINDOW, :]

    y_ref[...] = _post(yssm_ref[...], zs_ref[...], o_scr[...], za_ref[...], gs_ref[...], ga_ref[...],
                       x_ref[...], gate_ref[...], wglu_ref, bglu_ref, wbs_ref, wba_ref, wout_ref,
                       lng_ref, lnb_ref)


def _const_spec(shape):
    nd = len(shape)
    return pl.BlockSpec(shape, lambda *_: (0,) * nd, pipeline_mode=pl.Buffered(1))


def _final_prompt(sinks, x2d, yssm, proj, modp, bias_tbl, wglu, bglu, wbs, wba, wout, lng, lnb, n_batch, seq):
    steps = seq // TB
    blk_per_step = TB // WINDOW

    def row(b, i):
        return b * steps + i

    def pcol(c):
        return pl.BlockSpec((TB, D_MODEL), lambda b, i, c=c: (row(b, i), c))

    def kv_cur(c):
        return pl.BlockSpec((TB, KV_WIDTH), lambda b, i, c=c: (row(b, i), c))

    def kv_prev(c):
        return pl.BlockSpec(
            (WINDOW, KV_WIDTH),
            lambda b, i, c=c: (jnp.maximum(row(b, i) * blk_per_step - 1, b * steps * blk_per_step), c))

    return pl.pallas_call(
        _final_body,
        grid=(n_batch, steps),
        in_specs=[pl.BlockSpec(memory_space=pltpu.SMEM),
                  pl.BlockSpec((TB, D_MODEL), lambda b, i: (row(b, i), 0)),
                  pl.BlockSpec((TB, D_MODEL), lambda b, i: (row(b, i), 0)),
                  pcol(COL_ZS), pcol(COL_Q), pcol(COL_ZA), pcol(COL_GS), pcol(COL_GA),
                  kv_cur(COL_K), kv_prev(COL_K), kv_cur(COL_V), kv_prev(COL_V),
                  pl.BlockSpec((None, 1, D_MODEL), lambda b, i: (b * 3 + 2, 0, 0)),
                  _const_spec(bias_tbl.shape),
                  _const_spec(wglu.shape), _const_spec(bglu.shape), _const_spec(wbs.shape),
                  _const_spec(wba.shape), _const_spec(wout.shape), _const_spec(lng.shape), _const_spec(lnb.shape)],
        out_specs=pl.BlockSpec((TB, D_MODEL), lambda b, i: (row(b, i), 0)),
        out_shape=jax.ShapeDtypeStruct((n_batch * seq, D_MODEL), F32),
        scratch_shapes=[pltpu.VMEM((TB, D_MODEL), F32)],
        compiler_params=pltpu.CompilerParams(vmem_limit_bytes=VMEM_LIMIT),
        name="final_prompt",
    )(sinks, x2d, yssm, proj, proj, proj, proj, proj, proj, proj, proj, proj, modp, bias_tbl,
      wglu, bglu, wbs, wba, wout, lng, lnb)


GT = 8


def _sample_ssm_body(u_ref, h0r_ref, h0i_ref, ar_ref, ai_ref, bre_ref, bim_ref, cblk_ref, d_ref,
                     y_ref, hr_ref, hi_ref):
    ub = u_ref[...]
    ys = []
    for a in range(GT):
        ua = ub[:, a * 128:(a + 1) * 128]
        st = slice(a * 512, (a + 1) * 512)
        xr = _dot(ua, bre_ref[a])
        xi = _dot(ua, bim_ref[a])
        h0r = h0r_ref[:, st]
        h0i = h0i_ref[:, st]
        ar = ar_ref[:, st]
        ai = ai_ref[:, st]
        hr = ar * h0r - ai * h0i + xr
        hi = ar * h0i + ai * h0r + xi
        hr_ref[:, st] = hr
        hi_ref[:, st] = hi
        hcat = jnp.concatenate([hr, hi], axis=1).astype(BF16)
        ys.append(_dot(hcat, cblk_ref[a]))
    y_ref[...] = jnp.concatenate(ys, axis=1) + d_ref[...] * ub.astype(F32)


def _sample_ssm(proj_s, h0r, h0i, ar, ai, bre, bim, cblk, d_row):
    rows = proj_s.shape[0]
    ns = SSM_GROUPS * SSM_STATE
    full = lambda shape: pl.BlockSpec(shape, lambda i: (0,) * len(shape))
    return pl.pallas_call(
        _sample_ssm_body,
        grid=(1,),
        in_specs=[pl.BlockSpec((rows, D_MODEL), lambda i: (0, COL_U)),
                  full((rows, ns)), full((rows, ns)), full((1, ns)), full((1, ns)),
                  full(bre.shape), full(bim.shape), full(cblk.shape), full((1, D_MODEL))],
        out_specs=[full((rows, D_MODEL)), full((rows, ns)), full((rows, ns))],
        out_shape=[jax.ShapeDtypeStruct((rows, D_MODEL), F32),
                   jax.ShapeDtypeStruct((rows, ns), F32),
                   jax.ShapeDtypeStruct((rows, ns), F32)],
        compiler_params=pltpu.CompilerParams(vmem_limit_bytes=VMEM_LIMIT),
        name="sample_ssm",
    )(proj_s, h0r, h0i, ar, ai, bre, bim, cblk, d_row)


SB = 16
KROWS = WINDOW + 8


def _sample_attn_body(q_ref, kn_ref, vn_ref, ck_ref, cv_ref, bias_ref, sink_ref, seg_ref, segt_ref,
                      o_ref, nk_ref, nv_ref):
    qf = q_ref[...].astype(F32)
    knf = kn_ref[...].astype(F32)
    vnf = vn_ref[...].astype(F32)
    rowk = lax.broadcasted_iota(jnp.int32, (KROWS, 128), 0)
    roww = lax.broadcasted_iota(jnp.int32, (WINDOW, KV_WIDTH), 0)
    sink = sink_ref[...]
    for b in range(SB):
        kb = ck_ref[b]
        vb = cv_ref[b]
        qrow = qf[b:b + 1]
        kn = jnp.broadcast_to(knf[b:b + 1], (8, KV_WIDTH))
        vn = jnp.broadcast_to(vnf[b:b + 1], (8, KV_WIDTH))
        kall = jnp.concatenate([kb, kn], axis=0)
        vall = jnp.concatenate([vb, vn], axis=0)
        prod = jnp.concatenate([kall * qrow[:, j * KV_WIDTH:(j + 1) * KV_WIDTH] for j in range(KV_GROUP)], axis=1)
        s = _dot(prod.astype(BF16), seg_ref[...]) * (HEAD_DIM ** -0.5) + bias_ref[...]
        s = jnp.where(rowk <= WINDOW, s, NEG_INF)
        m = jnp.maximum(jnp.max(s, axis=0, keepdims=True), sink)
        e = jnp.exp(s - m)
        den = jnp.sum(e, axis=0, keepdims=True) + jnp.exp(sink - m)
        p = e / den
        pexp = _dot(p.astype(BF16), segt_ref[...])
        v4 = jnp.concatenate([vall] * KV_GROUP, axis=1)
        o_ref[b:b + 1, :] = jnp.sum(pexp * v4, axis=0, keepdims=True)
        last = roww == WINDOW - 1
        nk_ref[b] = jnp.where(last, jnp.broadcast_to(knf[b:b + 1], (WINDOW, KV_WIDTH)), pltpu.roll(kb, WINDOW - 1, 0))
        nv_ref[b] = jnp.where(last, jnp.broadcast_to(vnf[b:b + 1], (WINDOW, KV_WIDTH)), pltpu.roll(vb, WINDOW - 1, 0))


def _sample_attn(proj_s, ck, cv, bias_s, sink_row, seg, segt):
    rows = proj_s.shape[0]
    cache_spec = pl.BlockSpec((SB, WINDOW, KV_WIDTH), lambda i: (i, 0, 0))
    return pl.pallas_call(
        _sample_attn_body,
        grid=(rows // SB,),
        in_specs=[pl.BlockSpec((SB, D_MODEL), lambda i: (i, COL_Q)),
                  pl.BlockSpec((SB, KV_WIDTH), lambda i: (i, COL_K)),
                  pl.BlockSpec((SB, KV_WIDTH), lambda i: (i, COL_V)),
                  cache_spec, cache_spec,
                  pl.BlockSpec(bias_s.shape, lambda i: (0, 0)),
                  pl.BlockSpec(sink_row.shape, lambda i: (0, 0)),
                  pl.BlockSpec(seg.shape, lambda i: (0, 0)),
                  pl.BlockSpec(segt.shape, lambda i: (0, 0))],
        out_specs=[pl.BlockSpec((SB, D_MODEL), lambda i: (i, 0)), cache_spec, cache_spec],
        out_shape=[jax.ShapeDtypeStruct((rows, D_MODEL), F32),
                   jax.ShapeDtypeStruct(ck.shape, F32),
                   jax.ShapeDtypeStruct(cv.shape, F32)],
        compiler_params=pltpu.CompilerParams(vmem_limit_bytes=VMEM_LIMIT),
        name="sample_attn",
    )(proj_s, proj_s, proj_s, ck, cv, bias_s, sink_row, seg, segt)


def _sample_final_body(x_ref, yssm_ref, zs_ref, o_ref, za_ref, gs_ref, ga_ref, gate_ref,
                       wglu_ref, bglu_ref, wbs_ref, wba_ref, wout_ref, lng_ref, lnb_ref, y_ref):
    y_ref[...] = _post(yssm_ref[...], zs_ref[...], o_ref[...], za_ref[...], gs_ref[...], ga_ref[...],
                       x_ref[...], gate_ref[...], wglu_ref, bglu_ref, wbs_ref, wba_ref, wout_ref,
                       lng_ref, lnb_ref)


def _final_sample(x2d, yssm, o_a, proj_s, mods, wglu, bglu, wbs, wba, wout, lng, lnb):
    rows = x2d.shape[0]
    full = lambda shape: pl.BlockSpec(shape, lambda i: (0,) * len(shape))
    pcol = lambda c: pl.BlockSpec((rows, D_MODEL), lambda i, c=c: (0, c))
    return pl.pallas_call(
        _sample_final_body,
        grid=(1,),
        in_specs=[full((rows, D_MODEL)), full((rows, D_MODEL)), pcol(COL_ZS), full((rows, D_MODEL)),
                  pcol(COL_ZA), pcol(COL_GS), pcol(COL_GA),
                  pl.BlockSpec((rows, D_MODEL), lambda i: (0, 2)),
                  full(wglu.shape), full(bglu.shape), full(wbs.shape), full(wba.shape), full(wout.shape),
                  full(lng.shape), full(lnb.shape)],
        out_specs=full((rows, D_MODEL)),
        out_shape=jax.ShapeDtypeStruct((rows, D_MODEL), F32),
        compiler_params=pltpu.CompilerParams(vmem_limit_bytes=VMEM_LIMIT),
        name="final_sample",
    )(x2d, yssm, proj_s, o_a, proj_s, proj_s, proj_s, mods, wglu, bglu, wbs, wba, wout, lng, lnb)


def _rel_bucket(dist):
    max_exact = N_BUCKETS // 2
    df = jnp.maximum(dist, 1).astype(F32)
    large = max_exact + (jnp.log(df / max_exact) / math.log(WINDOW / max_exact)
                         * (N_BUCKETS - max_exact)).astype(jnp.int32)
    large = jnp.minimum(large, N_BUCKETS - 1)
    return jnp.where(dist < max_exact, dist, large)


def _ssm_params(lam_re, lam_im, log_delta, b_re, b_im, c_re, c_im, d_skip):
    hp = lax.Precision.HIGHEST
    g, p, c = SSM_GROUPS, SSM_STATE, SSM_GROUP
    lr, li = lam_re.astype(F32), lam_im.astype(F32)
    dt = jnp.exp(log_delta.astype(F32))[:, None]
    mag = jnp.exp(lr * dt)
    ar, ai = mag * jnp.cos(li * dt), mag * jnp.sin(li * dt)
    den = lr * lr + li * li
    nr = ar - 1.0
    cr_ = (nr * lr + ai * li) / den
    ci_ = (ai * lr - nr * li) / den
    bbr = cr_[..., None] * b_re - ci_[..., None] * b_im
    bbi = cr_[..., None] * b_im + ci_[..., None] * b_re

    def powers(taus):
        t = jnp.asarray(taus, F32)[:, None, None]
        m = jnp.exp(lr * dt * t)
        return m * jnp.cos(li * dt * t), m * jnp.sin(li * dt * t)

    pr, pi = powers(np.arange(CHUNK + 1))
    wr = pr[:CHUNK, :, :, None] * bbr - pi[:CHUNK, :, :, None] * bbi
    wi = pr[:CHUNK, :, :, None] * bbi + pi[:CHUNK, :, :, None] * bbr
    kt = (jnp.einsum('gcp,tgpd->tgcd', c_re, wr, precision=hp)
          - jnp.einsum('gcp,tgpd->tgcd', c_im, wi, precision=hp))
    tt = np.arange(CHUNK)[:, None]
    ss = np.arange(CHUNK)[None, :]
    lag = np.clip(tt - ss, 0, CHUNK - 1)
    causal = jnp.asarray((tt >= ss), F32)
    m5 = kt[lag] * causal[:, :, None, None, None]
    m5 = m5 + (jnp.eye(CHUNK, dtype=F32)[:, :, None, None, None]
               * (d_skip.reshape(g, c)[None, None, :, :, None] * jnp.eye(c, dtype=F32)[None, None, None]))
    mt = jnp.transpose(m5, (2, 1, 4, 0, 3)).reshape(g, CW, CW)
    rev = np.arange(CHUNK)[::-1]
    bpr = jnp.transpose(wr[rev], (1, 0, 3, 2)).reshape(g, CW, p)
    bpi = jnp.transpose(wi[rev], (1, 0, 3, 2)).reshape(g, CW, p)
    rhs = jnp.concatenate([mt, bpr, bpi], axis=2).astype(BF16)
    p1r, p1i = pr[1:], pi[1:]
    cpr = c_re[None] * jnp.transpose(p1r, (0, 1, 2))[:, :, None, :] - c_im[None] * p1i[:, :, None, :]
    cpi = -(c_re[None] * p1i[:, :, None, :] + c_im[None] * p1r[:, :, None, :])
    cpow = jnp.concatenate([jnp.transpose(cpr, (1, 3, 0, 2)).reshape(g, p, CW),
                            jnp.transpose(cpi, (1, 3, 0, 2)).reshape(g, p, CW)], axis=1).astype(BF16)
    qr, qi = powers(CHUNK * np.arange(1, 9))
    c1 = jnp.concatenate([qr, qr], axis=2)
    c2 = jnp.concatenate([-qi, qi], axis=2)
    log_rows = jnp.stack([c1[0], c2[0], c1[1], c2[1], c1[3], c2[3], c1[7], c2[7]], axis=1)
    coef = jnp.concatenate([log_rows, jnp.transpose(c1, (1, 0, 2)), jnp.transpose(c2, (1, 0, 2))], axis=1)

    eye8 = jnp.eye(GT, dtype=F32)
    bre = jnp.einsum('agpc,gh->agchp', bbr.reshape(GT, GT, p, c), eye8).reshape(GT, GT * c, GT * p).astype(BF16)
    bim = jnp.einsum('agpc,gh->agchp', bbi.reshape(GT, GT, p, c), eye8).reshape(GT, GT * c, GT * p).astype(BF16)
    cre = jnp.einsum('agcp,gh->agphc', c_re.reshape(GT, GT, c, p), eye8).reshape(GT, GT * p, GT * c)
    cim = jnp.einsum('agcp,gh->agphc', c_im.reshape(GT, GT, c, p), eye8).reshape(GT, GT * p, GT * c)
    cblk = jnp.concatenate([cre, -cim], axis=1).astype(BF16)
    return rhs, cpow, coef, ar.reshape(1, -1), ai.reshape(1, -1), bre, bim, cblk


def _attn_params(rel_bias, sinks):
    rb = rel_bias.astype(F32)
    qi = jnp.arange(WINDOW)[:, None]
    kj = jnp.arange(2 * WINDOW)[None, :]
    bk = _rel_bucket(jnp.clip(qi + WINDOW - kj, 0, WINDOW))
    bq = rb[bk]
    bias_tbl = jnp.transpose(bq.reshape(WINDOW, 2 * WINDOW, N_KV_HEADS, KV_GROUP), (2, 3, 0, 1)) \
        .reshape(N_KV_HEADS, KV_GROUP * WINDOW, 2 * WINDOW)
    ds = jnp.clip(WINDOW - jnp.arange(KROWS), 0, WINDOW)
    bs = rb[_rel_bucket(ds)]
    bs = jnp.transpose(bs.reshape(KROWS, N_KV_HEADS, KV_GROUP), (0, 2, 1)).reshape(KROWS, N_HEADS)
    bias_s = jnp.pad(bs, ((0, 0), (0, 128 - N_HEADS)))
    sk = jnp.transpose(sinks.astype(F32).reshape(N_KV_HEADS, KV_GROUP), (1, 0)).reshape(1, N_HEADS)
    sink_row = jnp.pad(sk, ((0, 0), (0, 128 - N_HEADS)))
    seg_np = np.zeros((D_MODEL, 128), np.float32)
    for j in range(KV_GROUP):
        for kh in range(N_KV_HEADS):
            r0 = j * KV_WIDTH + kh * HEAD_DIM
            seg_np[r0:r0 + HEAD_DIM, j * N_KV_HEADS + kh] = 1.0
    seg = jnp.asarray(seg_np, BF16)
    segt = jnp.asarray(seg_np.T, BF16)
    return bias_tbl, bias_s, sink_row, seg, segt


def kernel(x_prompt, x_sample, c_prompt, c_sample, state_ssm_re, state_ssm_im, cache_swa_k, cache_swa_v,
           w_ada, b_ada, w_in, ssm_lambda_re, ssm_lambda_im, ssm_log_delta, ssm_b_re, ssm_b_im,
           ssm_c_re, ssm_c_im, ssm_d, w_glu, b_glu, attn_sinks, rel_bias, w_branch_s, w_branch_a,
           w_out, ln_g, ln_b):
    assert w_ada.shape[0] == 1, "single-layer trunk"
    n_batch, seq, _ = x_prompt.shape
    n_dec = x_sample.shape[0]
    perm = _head_perm()

    wi = w_in[0]
    w_in_b = jnp.concatenate([wi[:, 0:1024], wi[:, 1024:2048], wi[:, 2048:3072][:, perm],
                              wi[:, 3584:4608][:, perm], wi[:, 4608:5632], wi[:, 5632:6656],
                              wi[:, 3072:3328], wi[:, 3328:3584]], axis=1).astype(BF16)
    wglu = w_glu[0].astype(BF16)
    bglu = b_glu[0].reshape(1, -1).astype(F32)
    wbs = w_branch_s[0].astype(BF16)
    wba = w_branch_a[0][perm, :].astype(BF16)
    wout = w_out[0].astype(BF16)
    lng = ln_g[0].reshape(1, -1).astype(F32)
    lnb = ln_b[0].reshape(1, -1).astype(F32)
    rhs, cpow, coef, ar_row, ai_row, bre, bim, cblk = _ssm_params(
        ssm_lambda_re[0], ssm_lambda_im[0], ssm_log_delta[0], ssm_b_re[0], ssm_b_im[0],
        ssm_c_re[0], ssm_c_im[0], ssm_d[0])
    bias_tbl, bias_s, sink_row, seg, segt = _attn_params(rel_bias, attn_sinks[0])

    c_all = jnp.concatenate([c_prompt, jnp.zeros((8 - n_batch, D_MODEL), F32), c_sample], axis=0)
    mod = _ada(c_all, w_ada[0], b_ada[0])
    modp = mod[:n_batch].reshape(n_batch * 3, 1, D_MODEL)
    mods = mod[8:8 + n_dec]

    xp = x_prompt.reshape(n_batch * seq, D_MODEL)
    tm = 512
    steps_per_b = seq // tm
    proj_p = _proj(
        xp, modp, modp,
        pl.BlockSpec((None, 1, D_MODEL), lambda i: ((i // steps_per_b) * 3 + 0, 0, 0)),
        pl.BlockSpec((None, 1, D_MODEL), lambda i: ((i // steps_per_b) * 3 + 1, 0, 0)),
        w_in_b, tm)
    nk_all = n_batch * seq // CHUNK
    z = jnp.transpose(proj_p[:, :D_MODEL].reshape(nk_all, CHUNK, SSM_GROUPS, SSM_GROUP), (2, 0, 1, 3)) \
        .reshape(SSM_GROUPS, nk_all, CW)
    yz, hfin = _ssm_prompt(z, rhs, cpow, coef, n_batch)
    yssm = jnp.transpose(yz.reshape(SSM_GROUPS, nk_all, CHUNK, SSM_GROUP), (1, 2, 0, 3)) \
        .reshape(n_batch * seq, D_MODEL)
    yp = _final_prompt(attn_sinks[0].astype(F32), xp, yssm, proj_p, modp, bias_tbl,
                       wglu, bglu, wbs, wba, wout, lng, lnb, n_batch, seq)
    yp = yp.reshape(n_batch, seq, D_MODEL)
    p_hr = jnp.transpose(hfin[:, :, :SSM_STATE], (1, 0, 2))[None]
    p_hi = jnp.transpose(hfin[:, :, SSM_STATE:], (1, 0, 2))[None]
    kv_tail = proj_p.reshape(n_batch, seq, D_IN)[:, seq - WINDOW:, COL_K * KV_WIDTH:].astype(F32)
    p_k = kv_tail[:, :, :KV_WIDTH].reshape(1, n_batch, WINDOW, N_KV_HEADS, HEAD_DIM)
    p_v = kv_tail[:, :, KV_WIDTH:].reshape(1, n_batch, WINDOW, N_KV_HEADS, HEAD_DIM)

    xs = x_sample.reshape(n_dec, D_MODEL)
    proj_s = _proj(xs, mods, mods,
                   pl.BlockSpec((n_dec, D_MODEL), lambda i: (0, 0)),
                   pl.BlockSpec((n_dec, D_MODEL), lambda i: (0, 1)),
                   w_in_b, n_dec)
    ns = SSM_GROUPS * SSM_STATE
    ys_s, s_hr, s_hi = _sample_ssm(proj_s, state_ssm_re[0].reshape(n_dec, ns), state_ssm_im[0].reshape(n_dec, ns),
                                   ar_row, ai_row, bre, bim, cblk, ssm_d[0].reshape(1, -1).astype(F32))
    o_s, nk, nv = _sample_attn(proj_s, cache_swa_k[0].reshape(n_dec, WINDOW, KV_WIDTH),
                               cache_swa_v[0].reshape(n_dec, WINDOW, KV_WIDTH), bias_s, sink_row, seg, segt)
    ysmp = _final_sample(xs, ys_s, o_s, proj_s, mods, wglu, bglu, wbs, wba, wout, lng, lnb)

    return (yp, ysmp.reshape(n_dec, 1, D_MODEL),
            p_hr, p_hi, p_k, p_v,
            s_hr.reshape(1, n_dec, SSM_GROUPS, SSM_STATE), s_hi.reshape(1, n_dec, SSM_GROUPS, SSM_STATE),
            nk.reshape(1, n_dec, WINDOW, N_KV_HEADS, HEAD_DIM), nv.reshape(1, n_dec, WINDOW, N_KV_HEADS, HEAD_DIM))
```

```python
import functools
import math

import numpy as np
import jax
import jax.numpy as jnp
from jax import lax
from jax.experimental import pallas as pl
from jax.experimental.pallas import tpu as pltpu

F32 = jnp.float32
BF16 = jnp.bfloat16

D_MODEL = 1024
SSM_GROUPS = 64
SSM_GROUP = 16
SSM_STATE = 64
N_HEADS = 16
HEAD_DIM = 64
N_KV_HEADS = 4
KV_GROUP = 4
KV_WIDTH = N_KV_HEADS * HEAD_DIM
WINDOW = 128
N_BUCKETS = 32
NEG_INF = -1e30
LN_EPS = 1e-5
DEPTH = 1
DEEPNORM_ALPHA = (2 * DEPTH) ** 0.25
D_IN = 6656
CHUNK = 16
CW = CHUNK * SSM_GROUP
SW = 2 * SSM_STATE

COL_U, COL_ZS, COL_Q, COL_ZA, COL_GS, COL_GA = 0, 1, 2, 3, 4, 5
COL_K, COL_V = 24, 25

VMEM_LIMIT = 56 * 1024 * 1024


def _head_perm():
    return np.arange(N_HEADS * HEAD_DIM).reshape(N_KV_HEADS, KV_GROUP, HEAD_DIM).transpose(1, 0, 2).reshape(-1)


def _sigmoid(x):
    return 1.0 / (1.0 + jnp.exp(-x))


def _silu(x):
    return x * _sigmoid(x)


def _gelu_tanh(x):
    return 0.5 * x * (1.0 + jnp.tanh(math.sqrt(2.0 / math.pi) * (x + 0.044715 * (x * x * x))))


def _dot(a, b):
    return jnp.dot(a, b, preferred_element_type=F32)


def _dot_nt(a, b):
    return lax.dot_general(a, b, (((1,), (1,)), ((), ())), preferred_element_type=F32)


def _ada_body(c_ref, w_ref, b_ref, o_ref):
    c = c_ref[...]
    sc = _silu(c).astype(BF16)
    o_ref[...] = _dot(sc, w_ref[...].astype(BF16)) + b_ref[...]


def _ada(c_all, w_ada, b_ada):
    rows = c_all.shape[0]
    bn = 512
    return pl.pallas_call(
        _ada_body,
        grid=(3 * D_MODEL // bn,),
        in_specs=[pl.BlockSpec((rows, D_MODEL), lambda j: (0, 0)),
                  pl.BlockSpec((D_MODEL, bn), lambda j: (0, j)),
                  pl.BlockSpec((1, bn), lambda j: (0, j))],
        out_specs=pl.BlockSpec((rows, bn), lambda j: (0, j)),
        out_shape=jax.ShapeDtypeStruct((rows, 3 * D_MODEL), F32),
        name="ada",
    )(c_all, w_ada, b_ada.reshape(1, -1))


PROJ_CB = 512


def _proj_body(x_ref, shift_ref, scale_ref, w_ref, o_ref):
    h = x_ref[...] * (1.0 + scale_ref[...]) + shift_ref[...]
    hb = h.astype(BF16)
    for j in range(D_IN // PROJ_CB):
        sl = slice(j * PROJ_CB, (j + 1) * PROJ_CB)
        o_ref[:, sl] = _dot(hb, w_ref[:, sl]).astype(BF16)


def _proj(x2d, shift_arr, scale_arr, shift_spec, scale_spec, w_in_b, tm):
    rows = x2d.shape[0]
    return pl.pallas_call(
        _proj_body,
        grid=(rows // tm,),
        in_specs=[pl.BlockSpec((tm, D_MODEL), lambda i: (i, 0)),
                  shift_spec, scale_spec,
                  pl.BlockSpec((D_MODEL, D_IN), lambda i: (0, 0), pipeline_mode=pl.Buffered(1))],
        out_specs=pl.BlockSpec((tm, D_IN), lambda i: (i, 0)),
        out_shape=jax.ShapeDtypeStruct((rows, D_IN), BF16),
        compiler_params=pltpu.CompilerParams(vmem_limit_bytes=VMEM_LIMIT),
        name="proj",
    )(x2d, shift_arr, scale_arr, w_in_b)


NKS = 128
GC = 16


def _cmul(c1, c2, x):
    return c1 * x + c2 * pltpu.roll(x, SSM_STATE, x.ndim - 1)


def _ssm_body(z_ref, rhs_ref, cpow_ref, coef_ref, y_ref, hfin_ref, s_scr, hp_scr, carry_scr):
    @pl.when(pl.program_id(1) == 0)
    def _():
        carry_scr[...] = jnp.zeros_like(carry_scr)

    def local(g, c):
        out = _dot(z_ref[g], rhs_ref[g])
        y_ref[g] = out[:, :CW]
        s_scr[g] = out[:, CW:]
        return c

    lax.fori_loop(0, SSM_GROUPS, local, 0)

    row = lax.broadcasted_iota(jnp.int32, (GC, 8, SW), 1)
    for gc in range(SSM_GROUPS // GC):
        gs = slice(gc * GC, (gc + 1) * GC)
        coef = coef_ref[gs]
        h = carry_scr[gs]
        for r in range(NKS // 8):
            rs = slice(r * 8, (r + 1) * 8)
            blk = s_scr[gs, rs, :]
            for i, d in enumerate((1, 2, 4)):
                sh = jnp.where(row >= d, pltpu.roll(blk, d, 1), 0.0)
                blk = blk + _cmul(coef[:, 2 * i:2 * i + 1], coef[:, 2 * i + 1:2 * i + 2], sh)
            blk = blk + _cmul(coef[:, 8:16], coef[:, 16:24], h)
            hp_scr[gs, rs, :] = jnp.where(row == 0, h, pltpu.roll(blk, 1, 1))
            h = jnp.broadcast_to(blk[:, 7:8, :], (GC, 8, SW))
        carry_scr[gs] = h
    hfin_ref[...] = carry_scr[...]

    def carried(g, c):
        y_ref[g] += _dot(hp_scr[g].astype(BF16), cpow_ref[g])
        return c

    lax.fori_loop(0, SSM_GROUPS, carried, 0)


def _ssm_prompt(z, rhs, cpow, coef, n_batch):
    g, rows, _ = z.shape
    steps = rows // n_batch // NKS
    return pl.pallas_call(
        _ssm_body,
        grid=(n_batch, steps),
        in_specs=[pl.BlockSpec((g, NKS, CW), lambda b, i: (0, b * steps + i, 0)),
                  _const_spec(rhs.shape), _const_spec(cpow.shape), _const_spec(coef.shape)],
        out_specs=[pl.BlockSpec((g, NKS, CW), lambda b, i: (0, b * steps + i, 0)),
                   pl.BlockSpec((None, g, 8, SW), lambda b, i: (b, 0, 0, 0))],
        out_shape=[jax.ShapeDtypeStruct((g, rows, CW), F32),
                   jax.ShapeDtypeStruct((n_batch, g, 8, SW), F32)],
        scratch_shapes=[pltpu.VMEM((g, NKS, SW), F32), pltpu.VMEM((g, NKS, SW), F32),
                        pltpu.VMEM((g, 8, SW), F32)],
        compiler_params=pltpu.CompilerParams(dimension_semantics=("arbitrary", "arbitrary"),
                                             vmem_limit_bytes=VMEM_LIMIT),
        name="ssm_prompt",
    )(z, rhs, cpow, coef)


def _post(ys, zs, o_a, za, gs, ga, x, gate, wglu_ref, bglu_ref, wbs_ref, wba_ref, wout_ref, lng_ref, lnb_ref):
    ys = _gelu_tanh(ys)
    glu = ys * _sigmoid(_dot(ys.astype(BF16), wglu_ref[...]) + bglu_ref[...])
    t = (glu * _silu(zs.astype(F32))).astype(BF16)
    b_s = _dot(t, wbs_ref[...])
    oz = (o_a * _silu(za.astype(F32))).astype(BF16)
    b_a = _dot(oz, wba_ref[...])
    m = _sigmoid(gs.astype(F32)) * b_s + _sigmoid(ga.astype(F32)) * b_a
    out = _dot(m.astype(BF16), wout_ref[...])
    r = DEEPNORM_ALPHA * x + gate * out
    mu = jnp.mean(r, axis=-1, keepdims=True)
    rc = r - mu
    var = jnp.mean(rc * rc, axis=-1, keepdims=True)
    return rc * lax.rsqrt(var + LN_EPS) * lng_ref[...] + lnb_ref[...]


TB = 256
NQB = TB // WINDOW


def _final_body(sinks_ref, x_ref, yssm_ref, zs_ref, q_ref, za_ref, gs_ref, ga_ref,
                kc_ref, kp_ref, vc_ref, vp_ref, gate_ref, bias_ref,
                wglu_ref, bglu_ref, wbs_ref, wba_ref, wout_ref, lng_ref, lnb_ref,
                y_ref, o_scr):
    is_first = pl.program_id(1) == 0
    qi = lax.broadcasted_iota(jnp.int32, (KV_GROUP * WINDOW, 2 * WINDOW), 0) % WINDOW
    kj = lax.broadcasted_iota(jnp.int32, (KV_GROUP * WINDOW, 2 * WINDOW), 1)
    dist = qi + WINDOW - kj
    band = (dist >= 0) & (dist <= WINDOW)
    band_first = band & ((kj >= WINDOW) | jnp.logical_not(is_first))

    for blk in range(NQB):
        rows = slice(blk * WINDOW, (blk + 1) * WINDOW)
        if blk == 0:
            kp, vp, mask = kp_ref[...], vp_ref[...], band_first
        else:
            prev = slice((blk - 1) * WINDOW, blk * WINDOW)
            kp, vp, mask = kc_ref[prev, :], vc_ref[prev, :], band
        kcat = jnp.concatenate([kp, kc_ref[rows, :]], axis=0)
        vcat = jnp.concatenate([vp, vc_ref[rows, :]], axis=0)
        qb = q_ref[rows, :]
        for kh in range(N_KV_HEADS):
            hs = slice(kh * HEAD_DIM, (kh + 1) * HEAD_DIM)
            qs = jnp.concatenate(
                [qb[:, j * KV_WIDTH + kh * HEAD_DIM: j * KV_WIDTH + (kh + 1) * HEAD_DIM] for j in range(KV_GROUP)],
                axis=0)
            s = _dot_nt(qs, kcat[:, hs]) * (HEAD_DIM ** -0.5) + bias_ref[kh]
            s = jnp.where(mask, s, NEG_INF)
            sink = jnp.concatenate(
                [jnp.full((WINDOW, 1), sinks_ref[kh * KV_GROUP + j], F32) for j in range(KV_GROUP)], axis=0)
            m = jnp.maximum(jnp.max(s, axis=-1, keepdims=True), sink)
            e = jnp.exp(s - m)
            den = jnp.sum(e, axis=-1, keepdims=True) + jnp.exp(sink - m)
            o = _dot(e.astype(BF16), vcat[:, hs]) / den
            for j in range(KV_GROUP):
                c0 = j * KV_WIDTH + kh * HEAD_DIM
                o_scr[rows, c0:c0 + HEAD_DIM] = o[j * WINDOW:(j + 1) * WINDOW, :]

    y_ref[...] = _post(yssm_ref[...], zs_ref[...], o_scr[...], za_ref[...], gs_ref[...], ga_ref[...],
                       x_ref[...], gate_ref[...], wglu_ref, bglu_ref, wbs_ref, wba_ref, wout_ref,
                       lng_ref, lnb_ref)


def _const_spec(shape):
    nd = len(shape)
    return pl.BlockSpec(shape, lambda *_: (0,) * nd, pipeline_mode=pl.Buffered(1))


def _final_prompt(sinks, x2d, yssm, proj, modp, bias_tbl, wglu, bglu, wbs, wba, wout, lng, lnb, n_batch, seq):
    steps = seq // TB
    blk_per_step = TB // WINDOW

    def row(b, i):
        return b * steps + i

    def pcol(c):
        return pl.BlockSpec((TB, D_MODEL), lambda b, i, c=c: (row(b, i), c))

    def kv_cur(c):
        return pl.BlockSpec((TB, KV_WIDTH), lambda b, i, c=c: (row(b, i), c))

    def kv_prev(c):
        return pl.BlockSpec(
            (WINDOW, KV_WIDTH),
            lambda b, i, c=c: (jnp.maximum(row(b, i) * blk_per_step - 1, b * steps * blk_per_step), c))

    return pl.pallas_call(
        _final_body,
        grid=(n_batch, steps),
        in_specs=[pl.BlockSpec(memory_space=pltpu.SMEM),
                  pl.BlockSpec((TB, D_MODEL), lambda b, i: (row(b, i), 0)),
                  pl.BlockSpec((TB, D_MODEL), lambda b, i: (row(b, i), 0)),
                  pcol(COL_ZS), pcol(COL_Q), pcol(COL_ZA), pcol(COL_GS), pcol(COL_GA),
                  kv_cur(COL_K), kv_prev(COL_K), kv_cur(COL_V), kv_prev(COL_V),
                  pl.BlockSpec((None, 1, D_MODEL), lambda b, i: (b * 3 + 2, 0, 0)),
                  _const_spec(bias_tbl.shape),
                  _const_spec(wglu.shape), _const_spec(bglu.shape), _const_spec(wbs.shape),
                  _const_spec(wba.shape), _const_spec(wout.shape), _const_spec(lng.shape), _const_spec(lnb.shape)],
        out_specs=pl.BlockSpec((TB, D_MODEL), lambda b, i: (row(b, i), 0)),
        out_shape=jax.ShapeDtypeStruct((n_batch * seq, D_MODEL), F32),
        scratch_shapes=[pltpu.VMEM((TB, D_MODEL), F32)],
        compiler_params=pltpu.CompilerParams(vmem_limit_bytes=VMEM_LIMIT),
        name="final_prompt",
    )(sinks, x2d, yssm, proj, proj, proj, proj, proj, proj, proj, proj, proj, modp, bias_tbl,
      wglu, bglu, wbs, wba, wout, lng, lnb)


GT = 8


def _sample_ssm_body(u_ref, h0r_ref, h0i_ref, ar_ref, ai_ref, bre_ref, bim_ref, cblk_ref, d_ref,
                     y_ref, hr_ref, hi_ref):
    ub = u_ref[...]
    ys = []
    for a in range(GT):
        ua = ub[:, a * 128:(a + 1) * 128]
        st = slice(a * 512, (a + 1) * 512)
        xr = _dot(ua, bre_ref[a])
        xi = _dot(ua, bim_ref[a])
        h0r = h0r_ref[:, st]
        h0i = h0i_ref[:, st]
        ar = ar_ref[:, st]
        ai = ai_ref[:, st]
        hr = ar * h0r - ai * h0i + xr
        hi = ar * h0i + ai * h0r + xi
        hr_ref[:, st] = hr
        hi_ref[:, st] = hi
        hcat = jnp.concatenate([hr, hi], axis=1).astype(BF16)
        ys.append(_dot(hcat, cblk_ref[a]))
    y_ref[...] = jnp.concatenate(ys, axis=1) + d_ref[...] * ub.astype(F32)


def _sample_ssm(proj_s, h0r, h0i, ar, ai, bre, bim, cblk, d_row):
    rows = proj_s.shape[0]
    ns = SSM_GROUPS * SSM_STATE
    full = lambda shape: pl.BlockSpec(shape, lambda i: (0,) * len(shape))
    return pl.pallas_call(
        _sample_ssm_body,
        grid=(1,),
        in_specs=[pl.BlockSpec((rows, D_MODEL), lambda i: (0, COL_U)),
                  full((rows, ns)), full((rows, ns)), full((1, ns)), full((1, ns)),
                  full(bre.shape), full(bim.shape), full(cblk.shape), full((1, D_MODEL))],
        out_specs=[full((rows, D_MODEL)), full((rows, ns)), full((rows, ns))],
        out_shape=[jax.ShapeDtypeStruct((rows, D_MODEL), F32),
                   jax.ShapeDtypeStruct((rows, ns), F32),
                   jax.ShapeDtypeStruct((rows, ns), F32)],
        compiler_params=pltpu.CompilerParams(vmem_limit_bytes=VMEM_LIMIT),
        name="sample_ssm",
    )(proj_s, h0r, h0i, ar, ai, bre, bim, cblk, d_row)


SB = 16
KROWS = WINDOW + 8


def _sample_attn_body(q_ref, kn_ref, vn_ref, ck_ref, cv_ref, bias_ref, sink_ref, seg_ref, segt_ref,
                      o_ref, nk_ref, nv_ref):
    qf = q_ref[...].astype(F32)
    knf = kn_ref[...].astype(F32)
    vnf = vn_ref[...].astype(F32)
    rowk = lax.broadcasted_iota(jnp.int32, (KROWS, 128), 0)
    roww = lax.broadcasted_iota(jnp.int32, (WINDOW, KV_WIDTH), 0)
    sink = sink_ref[...]
    for b in range(SB):
        kb = ck_ref[b]
        vb = cv_ref[b]
        qrow = qf[b:b + 1]
        kn = jnp.broadcast_to(knf[b:b + 1], (8, KV_WIDTH))
        vn = jnp.broadcast_to(vnf[b:b + 1], (8, KV_WIDTH))
        kall = jnp.concatenate([kb, kn], axis=0)
        vall = jnp.concatenate([vb, vn], axis=0)
        prod = jnp.concatenate([kall * qrow[:, j * KV_WIDTH:(j + 1) * KV_WIDTH] for j in range(KV_GROUP)], axis=1)
        s = _dot(prod.astype(BF16), seg_ref[...]) * (HEAD_DIM ** -0.5) + bias_ref[...]
        s = jnp.where(rowk <= WINDOW, s, NEG_INF)
        m = jnp.maximum(jnp.max(s, axis=0, keepdims=True), sink)
        e = jnp.exp(s - m)
        den = jnp.sum(e, axis=0, keepdims=True) + jnp.exp(sink - m)
        p = e / den
        pexp = _dot(p.astype(BF16), segt_ref[...])
        v4 = jnp.concatenate([vall] * KV_GROUP, axis=1)
        o_ref[b:b + 1, :] = jnp.sum(pexp * v4, axis=0, keepdims=True)
        last = roww == WINDOW - 1
        nk_ref[b] = jnp.where(last, jnp.broadcast_to(knf[b:b + 1], (WINDOW, KV_WIDTH)), pltpu.roll(kb, WINDOW - 1, 0))
        nv_ref[b] = jnp.where(last, jnp.broadcast_to(vnf[b:b + 1], (WINDOW, KV_WIDTH)), pltpu.roll(vb, WINDOW - 1, 0))


def _sample_attn(proj_s, ck, cv, bias_s, sink_row, seg, segt):
    rows = proj_s.shape[0]
    cache_spec = pl.BlockSpec((SB, WINDOW, KV_WIDTH), lambda i: (i, 0, 0))
    return pl.pallas_call(
        _sample_attn_body,
        grid=(rows // SB,),
        in_specs=[pl.BlockSpec((SB, D_MODEL), lambda i: (i, COL_Q)),
                  pl.BlockSpec((SB, KV_WIDTH), lambda i: (i, COL_K)),
                  pl.BlockSpec((SB, KV_WIDTH), lambda i: (i, COL_V)),
                  cache_spec, cache_spec,
                  pl.BlockSpec(bias_s.shape, lambda i: (0, 0)),
                  pl.BlockSpec(sink_row.shape, lambda i: (0, 0)),
                  pl.BlockSpec(seg.shape, lambda i: (0, 0)),
                  pl.BlockSpec(segt.shape, lambda i: (0, 0))],
        out_specs=[pl.BlockSpec((SB, D_MODEL), lambda i: (i, 0)), cache_spec, cache_spec],
        out_shape=[jax.ShapeDtypeStruct((rows, D_MODEL), F32),
                   jax.ShapeDtypeStruct(ck.shape, F32),
                   jax.ShapeDtypeStruct(cv.shape, F32)],
        compiler_params=pltpu.CompilerParams(vmem_limit_bytes=VMEM_LIMIT),
        name="sample_attn",
    )(proj_s, proj_s, proj_s, ck, cv, bias_s, sink_row, seg, segt)


def _sample_final_body(x_ref, yssm_ref, zs_ref, o_ref, za_ref, gs_ref, ga_ref, gate_ref,
                       wglu_ref, bglu_ref, wbs_ref, wba_ref, wout_ref, lng_ref, lnb_ref, y_ref):
    y_ref[...] = _post(yssm_ref[...], zs_ref[...], o_ref[...], za_ref[...], gs_ref[...], ga_ref[...],
                       x_ref[...], gate_ref[...], wglu_ref, bglu_ref, wbs_ref, wba_ref, wout_ref,
                       lng_ref, lnb_ref)


def _final_sample(x2d, yssm, o_a, proj_s, mods, wglu, bglu, wbs, wba, wout, lng, lnb):
    rows = x2d.shape[0]
    full = lambda shape: pl.BlockSpec(shape, lambda i: (0,) * len(shape))
    pcol = lambda c: pl.BlockSpec((rows, D_MODEL), lambda i, c=c: (0, c))
    return pl.pallas_call(
        _sample_final_body,
        grid=(1,),
        in_specs=[full((rows, D_MODEL)), full((rows, D_MODEL)), pcol(COL_ZS), full((rows, D_MODEL)),
                  pcol(COL_ZA), pcol(COL_GS), pcol(COL_GA),
                  pl.BlockSpec((rows, D_MODEL), lambda i: (0, 2)),
                  full(wglu.shape), full(bglu.shape), full(wbs.shape), full(wba.shape), full(wout.shape),
                  full(lng.shape), full(lnb.shape)],
        out_specs=full((rows, D_MODEL)),
        out_shape=jax.ShapeDtypeStruct((rows, D_MODEL), F32),
        compiler_params=pltpu.CompilerParams(vmem_limit_bytes=VMEM_LIMIT),
        name="final_sample",
    )(x2d, yssm, proj_s, o_a, proj_s, proj_s, proj_s, mods, wglu, bglu, wbs, wba, wout, lng, lnb)


def _rel_bucket(dist):
    max_exact = N_BUCKETS // 2
    df = jnp.maximum(dist, 1).astype(F32)
    large = max_exact + (jnp.log(df / max_exact) / math.log(WINDOW / max_exact)
                         * (N_BUCKETS - max_exact)).astype(jnp.int32)
    large = jnp.minimum(large, N_BUCKETS - 1)
    return jnp.where(dist < max_exact, dist, large)


def _ssm_params(lam_re, lam_im, log_delta, b_re, b_im, c_re, c_im, d_skip):
    hp = lax.Precision.HIGHEST
    g, p, c = SSM_GROUPS, SSM_STATE, SSM_GROUP
    lr, li = lam_re.astype(F32), lam_im.astype(F32)
    dt = jnp.exp(log_delta.astype(F32))[:, None]
    mag = jnp.exp(lr * dt)
    ar, ai = mag * jnp.cos(li * dt), mag * jnp.sin(li * dt)
    den = lr * lr + li * li
    nr = ar - 1.0
    cr_ = (nr * lr + ai * li) / den
    ci_ = (ai * lr - nr * li) / den
    bbr = cr_[..., None] * b_re - ci_[..., None] * b_im
    bbi = cr_[..., None] * b_im + ci_[..., None] * b_re

    def powers(taus):
        t = jnp.asarray(taus, F32)[:, None, None]
        m = jnp.exp(lr * dt * t)
        return m * jnp.cos(li * dt * t), m * jnp.sin(li * dt * t)

    pr, pi = powers(np.arange(CHUNK + 1))
    wr = pr[:CHUNK, :, :, None] * bbr - pi[:CHUNK, :, :, None] * bbi
    wi = pr[:CHUNK, :, :, None] * bbi + pi[:CHUNK, :, :, None] * bbr
    kt = (jnp.einsum('gcp,tgpd->tgcd', c_re, wr, precision=hp)
          - jnp.einsum('gcp,tgpd->tgcd', c_im, wi, precision=hp))
    tt = np.arange(CHUNK)[:, None]
    ss = np.arange(CHUNK)[None, :]
    lag = np.clip(tt - ss, 0, CHUNK - 1)
    causal = jnp.asarray((tt >= ss), F32)
    m5 = kt[lag] * causal[:, :, None, None, None]
    m5 = m5 + (jnp.eye(CHUNK, dtype=F32)[:, :, None, None, None]
               * (d_skip.reshape(g, c)[None, None, :, :, None] * jnp.eye(c, dtype=F32)[None, None, None]))
    mt = jnp.transpose(m5, (2, 1, 4, 0, 3)).reshape(g, CW, CW)
    rev = np.arange(CHUNK)[::-1]
    bpr = jnp.transpose(wr[rev], (1, 0, 3, 2)).reshape(g, CW, p)
    bpi = jnp.transpose(wi[rev], (1, 0, 3, 2)).reshape(g, CW, p)
    rhs = jnp.concatenate([mt, bpr, bpi], axis=2).astype(BF16)
    p1r, p1i = pr[1:], pi[1:]
    cpr = c_re[None] * jnp.transpose(p1r, (0, 1, 2))[:, :, None, :] - c_im[None] * p1i[:, :, None, :]
    cpi = -(c_re[None] * p1i[:, :, None, :] + c_im[None] * p1r[:, :, None, :])
    cpow = jnp.concatenate([jnp.transpose(cpr, (1, 3, 0, 2)).reshape(g, p, CW),
                            jnp.transpose(cpi, (1, 3, 0, 2)).reshape(g, p, CW)], axis=1).astype(BF16)
    qr, qi = powers(CHUNK * np.arange(1, 9))
    c1 = jnp.concatenate([qr, qr], axis=2)
    c2 = jnp.concatenate([-qi, qi], axis=2)
    log_rows = jnp.stack([c1[0], c2[0], c1[1], c2[1], c1[3], c2[3], c1[7], c2[7]], axis=1)
    coef = jnp.concatenate([log_rows, jnp.transpose(c1, (1, 0, 2)), jnp.transpose(c2, (1, 0, 2))], axis=1)

    eye8 = jnp.eye(GT, dtype=F32)
    bre = jnp.einsum('agpc,gh->agchp', bbr.reshape(GT, GT, p, c), eye8).reshape(GT, GT * c, GT * p).astype(BF16)
    bim = jnp.einsum('agpc,gh->agchp', bbi.reshape(GT, GT, p, c), eye8).reshape(GT, GT * c, GT * p).astype(BF16)
    cre = jnp.einsum('agcp,gh->agphc', c_re.reshape(GT, GT, c, p), eye8).reshape(GT, GT * p, GT * c)
    cim = jnp.einsum('agcp,gh->agphc', c_im.reshape(GT, GT, c, p), eye8).reshape(GT, GT * p, GT * c)
    cblk = jnp.concatenate([cre, -cim], axis=1).astype(BF16)
    return rhs, cpow, coef, ar.reshape(1, -1), ai.reshape(1, -1), bre, bim, cblk


def _attn_params(rel_bias, sinks):
    rb = rel_bias.astype(F32)
    qi = jnp.arange(WINDOW)[:, None]
    kj = jnp.arange(2 * WINDOW)[None, :]
    bk = _rel_bucket(jnp.clip(qi + WINDOW - kj, 0, WINDOW))
    bq = rb[bk]
    bias_tbl = jnp.transpose(bq.reshape(WINDOW, 2 * WINDOW, N_KV_HEADS, KV_GROUP), (2, 3, 0, 1)) \
        .reshape(N_KV_HEADS, KV_GROUP * WINDOW, 2 * WINDOW)
    ds = jnp.clip(WINDOW - jnp.arange(KROWS), 0, WINDOW)
    bs = rb[_rel_bucket(ds)]
    bs = jnp.transpose(bs.reshape(KROWS, N_KV_HEADS, KV_GROUP), (0, 2, 1)).reshape(KROWS, N_HEADS)
    bias_s = jnp.pad(bs, ((0, 0), (0, 128 - N_HEADS)))
    sk = jnp.transpose(sinks.astype(F32).reshape(N_KV_HEADS, KV_GROUP), (1, 0)).reshape(1, N_HEADS)
    sink_row = jnp.pad(sk, ((0, 0), (0, 128 - N_HEADS)))
    seg_np = np.zeros((D_MODEL, 128), np.float32)
    for j in range(KV_GROUP):
        for kh in range(N_KV_HEADS):
            r0 = j * KV_WIDTH + kh * HEAD_DIM
            seg_np[r0:r0 + HEAD_DIM, j * N_KV_HEADS + kh] = 1.0
    seg = jnp.asarray(seg_np, BF16)
    segt = jnp.asarray(seg_np.T, BF16)
    return bias_tbl, bias_s, sink_row, seg, segt


def kernel(x_prompt, x_sample, c_prompt, c_sample, state_ssm_re, state_ssm_im, cache_swa_k, cache_swa_v,
           w_ada, b_ada, w_in, ssm_lambda_re, ssm_lambda_im, ssm_log_delta, ssm_b_re, ssm_b_im,
           ssm_c_re, ssm_c_im, ssm_d, w_glu, b_glu, attn_sinks, rel_bias, w_branch_s, w_branch_a,
           w_out, ln_g, ln_b):
    assert w_ada.shape[0] == 1, "single-layer trunk"
    n_batch, seq, _ = x_prompt.shape
    n_dec = x_sample.shape[0]
    perm = _head_perm()

    wi = w_in[0]
    w_in_b = jnp.concatenate([wi[:, 0:1024], wi[:, 1024:2048], wi[:, 2048:3072][:, perm],
                              wi[:, 3584:4608][:, perm], wi[:, 4608:5632], wi[:, 5632:6656],
                              wi[:, 3072:3328], wi[:, 3328:3584]], axis=1).astype(BF16)
    wglu = w_glu[0].astype(BF16)
    bglu = b_glu[0].reshape(1, -1).astype(F32)
    wbs = w_branch_s[0].astype(BF16)
    wba = w_branch_a[0][perm, :].astype(BF16)
    wout = w_out[0].astype(BF16)
    lng = ln_g[0].reshape(1, -1).astype(F32)
    lnb = ln_b[0].reshape(1, -1).astype(F32)
    rhs, cpow, coef, ar_row, ai_row, bre, bim, cblk = _ssm_params(
        ssm_lambda_re[0], ssm_lambda_im[0], ssm_log_delta[0], ssm_b_re[0], ssm_b_im[0],
        ssm_c_re[0], ssm_c_im[0], ssm_d[0])
    bias_tbl, bias_s, sink_row, seg, segt = _attn_params(rel_bias, attn_sinks[0])

    c_all = jnp.concatenate([c_prompt, jnp.zeros((8 - n_batch, D_MODEL), F32), c_sample], axis=0)
    mod = _ada(c_all, w_ada[0], b_ada[0])
    modp = mod[:n_batch].reshape(n_batch * 3, 1, D_MODEL)
    mods = mod[8:8 + n_dec]

    xp = x_prompt.reshape(n_batch * seq, D_MODEL)
    tm = 512
    steps_per_b = seq // tm
    proj_p = _proj(
        xp, modp, modp,
        pl.BlockSpec((None, 1, D_MODEL), lambda i: ((i // steps_per_b) * 3 + 0, 0, 0)),
        pl.BlockSpec((None, 1, D_MODEL), lambda i: ((i // steps_per_b) * 3 + 1, 0, 0)),
        w_in_b, tm)
    nk_all = n_batch * seq // CHUNK
    z = jnp.transpose(proj_p[:, :D_MODEL].reshape(nk_all, CHUNK, SSM_GROUPS, SSM_GROUP), (2, 0, 1, 3)) \
        .reshape(SSM_GROUPS, nk_all, CW)
    yz, hfin = _ssm_prompt(z, rhs, cpow, coef, n_batch)
    yssm = jnp.transpose(yz.reshape(SSM_GROUPS, nk_all, CHUNK, SSM_GROUP), (1, 2, 0, 3)) \
        .reshape(n_batch * seq, D_MODEL)
    yp = _final_prompt(attn_sinks[0].astype(F32), xp, yssm, proj_p, modp, bias_tbl,
                       wglu, bglu, wbs, wba, wout, lng, lnb, n_batch, seq)
    yp = yp.reshape(n_batch, seq, D_MODEL)
    p_hr = hfin[:, :, 0, :SSM_STATE][None]
    p_hi = hfin[:, :, 0, SSM_STATE:][None]
    kv_tail = proj_p.reshape(n_batch, seq, D_IN)[:, seq - WINDOW:, COL_K * KV_WIDTH:].astype(F32)
    p_k = kv_tail[:, :, :KV_WIDTH].reshape(1, n_batch, WINDOW, N_KV_HEADS, HEAD_DIM)
    p_v = kv_tail[:, :, KV_WIDTH:].reshape(1, n_batch, WINDOW, N_KV_HEADS, HEAD_DIM)

    xs = x_sample.reshape(n_dec, D_MODEL)
    proj_s = _proj(xs, mods, mods,
                   pl.BlockSpec((n_dec, D_MODEL), lambda i: (0, 0)),
                   pl.BlockSpec((n_dec, D_MODEL), lambda i: (0, 1)),
                   w_in_b, n_dec)
    ns = SSM_GROUPS * SSM_STATE
    ys_s, s_hr, s_hi = _sample_ssm(proj_s, state_ssm_re[0].reshape(n_dec, ns), state_ssm_im[0].reshape(n_dec, ns),
                                   ar_row, ai_row, bre, bim, cblk, ssm_d[0].reshape(1, -1).astype(F32))
    o_s, nk, nv = _sample_attn(proj_s, cache_swa_k[0].reshape(n_dec, WINDOW, KV_WIDTH),
                               cache_swa_v[0].reshape(n_dec, WINDOW, KV_WIDTH), bias_s, sink_row, seg, segt)
    ysmp = _final_sample(xs, ys_s, o_s, proj_s, mods, wglu, bglu, wbs, wba, wout, lng, lnb)

    return (yp, ysmp.reshape(n_dec, 1, D_MODEL),
            p_hr, p_hi, p_k, p_v,
            s_hr.reshape(1, n_dec, SSM_GROUPS, SSM_STATE), s_hi.reshape(1, n_dec, SSM_GROUPS, SSM_STATE),
            nk.reshape(1, n_dec, WINDOW, N_KV_HEADS, HEAD_DIM), nv.reshape(1, n_dec, WINDOW, N_KV_HEADS, HEAD_DIM))
```

```python
import functools
import math

import numpy as np
import jax
import jax.numpy as jnp
from jax import lax
from jax.experimental import pallas as pl
from jax.experimental.pallas import tpu as pltpu

F32 = jnp.float32
BF16 = jnp.bfloat16

D_MODEL = 1024
SSM_GROUPS = 64
SSM_GROUP = 16
SSM_STATE = 64
N_HEADS = 16
HEAD_DIM = 64
N_KV_HEADS = 4
KV_GROUP = 4
KV_WIDTH = N_KV_HEADS * HEAD_DIM
WINDOW = 128
N_BUCKETS = 32
NEG_INF = -1e30
LN_EPS = 1e-5
DEPTH = 1
DEEPNORM_ALPHA = (2 * DEPTH) ** 0.25
D_IN = 6656
CHUNK = 16
CW = CHUNK * SSM_GROUP
SW = 2 * SSM_STATE

COL_U, COL_ZS, COL_Q, COL_ZA, COL_GS, COL_GA = 0, 1, 2, 3, 4, 5
COL_K, COL_V = 24, 25

VMEM_LIMIT = 56 * 1024 * 1024


def _head_perm():
    return np.arange(N_HEADS * HEAD_DIM).reshape(N_KV_HEADS, KV_GROUP, HEAD_DIM).transpose(1, 0, 2).reshape(-1)


def _sigmoid(x):
    return 1.0 / (1.0 + jnp.exp(-x))


def _silu(x):
    return x * _sigmoid(x)


def _gelu_tanh(x):
    return 0.5 * x * (1.0 + jnp.tanh(math.sqrt(2.0 / math.pi) * (x + 0.044715 * (x * x * x))))


def _dot(a, b):
    return jnp.dot(a, b, preferred_element_type=F32)


def _dot_nt(a, b):
    return lax.dot_general(a, b, (((1,), (1,)), ((), ())), preferred_element_type=F32)


def _ada_body(c_ref, w_ref, b_ref, o_ref):
    c = c_ref[...]
    sc = _silu(c).astype(BF16)
    o_ref[...] = _dot(sc, w_ref[...].astype(BF16)) + b_ref[...]


def _ada(c_all, w_ada, b_ada):
    rows = c_all.shape[0]
    bn = 512
    return pl.pallas_call(
        _ada_body,
        grid=(3 * D_MODEL // bn,),
        in_specs=[pl.BlockSpec((rows, D_MODEL), lambda j: (0, 0)),
                  pl.BlockSpec((D_MODEL, bn), lambda j: (0, j)),
                  pl.BlockSpec((1, bn), lambda j: (0, j))],
        out_specs=pl.BlockSpec((rows, bn), lambda j: (0, j)),
        out_shape=jax.ShapeDtypeStruct((rows, 3 * D_MODEL), F32),
        name="ada",
    )(c_all, w_ada, b_ada.reshape(1, -1))


PROJ_CB = 512


def _proj_body(x_ref, shift_ref, scale_ref, w_ref, o_ref, *, n_out):
    h = x_ref[...] * (1.0 + scale_ref[...]) + shift_ref[...]
    hb = h.astype(BF16)
    for j in range(n_out // PROJ_CB):
        sl = slice(j * PROJ_CB, (j + 1) * PROJ_CB)
        o_ref[:, sl] = _dot(hb, w_ref[:, sl]).astype(BF16)


def _proj(x2d, shift_arr, scale_arr, shift_spec, scale_spec, w_in_b, tm, n_out):
    rows = x2d.shape[0]
    return pl.pallas_call(
        functools.partial(_proj_body, n_out=n_out),
        grid=(rows // tm,),
        in_specs=[pl.BlockSpec((tm, D_MODEL), lambda i: (i, 0)),
                  shift_spec, scale_spec,
                  pl.BlockSpec((D_MODEL, n_out), lambda i: (0, 0), pipeline_mode=pl.Buffered(1))],
        out_specs=pl.BlockSpec((tm, n_out), lambda i: (i, 0)),
        out_shape=jax.ShapeDtypeStruct((rows, n_out), BF16),
        compiler_params=pltpu.CompilerParams(vmem_limit_bytes=VMEM_LIMIT),
        name="proj",
    )(x2d, shift_arr, scale_arr, w_in_b)


NKS = 128
GC = 16


def _cmul(c1, c2, x):
    return c1 * x + c2 * pltpu.roll(x, SSM_STATE, x.ndim - 1)


def _ssm_body(z_ref, rhs_ref, cpow_ref, coef_ref, y_ref, hfin_ref, s_scr, hp_scr, carry_scr):
    @pl.when(pl.program_id(1) == 0)
    def _():
        carry_scr[...] = jnp.zeros_like(carry_scr)

    def local(g, c):
        out = _dot(z_ref[g], rhs_ref[g])
        y_ref[g] = out[:, :CW]
        s_scr[g] = out[:, CW:]
        return c

    lax.fori_loop(0, SSM_GROUPS, local, 0)

    row = lax.broadcasted_iota(jnp.int32, (GC, 8, SW), 1)
    for gc in range(SSM_GROUPS // GC):
        gs = slice(gc * GC, (gc + 1) * GC)
        coef = coef_ref[gs]
        h = carry_scr[gs]
        for r in range(NKS // 8):
            rs = slice(r * 8, (r + 1) * 8)
            blk = s_scr[gs, rs, :]
            for i, d in enumerate((1, 2, 4)):
                sh = jnp.where(row >= d, pltpu.roll(blk, d, 1), 0.0)
                blk = blk + _cmul(coef[:, 2 * i:2 * i + 1], coef[:, 2 * i + 1:2 * i + 2], sh)
            blk = blk + _cmul(coef[:, 8:16], coef[:, 16:24], h)
            hp_scr[gs, rs, :] = jnp.where(row == 0, h, pltpu.roll(blk, 1, 1))
            h = jnp.broadcast_to(blk[:, 7:8, :], (GC, 8, SW))
        carry_scr[gs] = h
    hfin_ref[...] = carry_scr[...]

    def carried(g, c):
        y_ref[g] += _dot(hp_scr[g].astype(BF16), cpow_ref[g])
        return c

    lax.fori_loop(0, SSM_GROUPS, carried, 0)


def _ssm_prompt(z, rhs, cpow, coef, n_batch):
    g, rows, _ = z.shape
    steps = rows // n_batch // NKS
    return pl.pallas_call(
        _ssm_body,
        grid=(n_batch, steps),
        in_specs=[pl.BlockSpec((g, NKS, CW), lambda b, i: (0, b * steps + i, 0)),
                  _const_spec(rhs.shape), _const_spec(cpow.shape), _const_spec(coef.shape)],
        out_specs=[pl.BlockSpec((g, NKS, CW), lambda b, i: (0, b * steps + i, 0)),
                   pl.BlockSpec((None, g, 8, SW), lambda b, i: (b, 0, 0, 0))],
        out_shape=[jax.ShapeDtypeStruct((g, rows, CW), F32),
                   jax.ShapeDtypeStruct((n_batch, g, 8, SW), F32)],
        scratch_shapes=[pltpu.VMEM((g, NKS, SW), F32), pltpu.VMEM((g, NKS, SW), F32),
                        pltpu.VMEM((g, 8, SW), F32)],
        compiler_params=pltpu.CompilerParams(dimension_semantics=("arbitrary", "arbitrary"),
                                             vmem_limit_bytes=VMEM_LIMIT),
        name="ssm_prompt",
    )(z, rhs, cpow, coef)


def _post(ys, zs, o_a, za, gs, ga, x, gate, wglu_ref, bglu_ref, wbs_ref, wba_ref, wout_ref, lng_ref, lnb_ref):
    ys = _gelu_tanh(ys)
    glu = ys * _sigmoid(_dot(ys.astype(BF16), wglu_ref[...]) + bglu_ref[...])
    t = (glu * _silu(zs.astype(F32))).astype(BF16)
    b_s = _dot(t, wbs_ref[...])
    oz = (o_a * _silu(za.astype(F32))).astype(BF16)
    b_a = _dot(oz, wba_ref[...])
    m = _sigmoid(gs.astype(F32)) * b_s + _sigmoid(ga.astype(F32)) * b_a
    out = _dot(m.astype(BF16), wout_ref[...])
    r = DEEPNORM_ALPHA * x + gate * out
    mu = jnp.mean(r, axis=-1, keepdims=True)
    rc = r - mu
    var = jnp.mean(rc * rc, axis=-1, keepdims=True)
    return rc * lax.rsqrt(var + LN_EPS) * lng_ref[...] + lnb_ref[...]


TB = 256
NQB = TB // WINDOW
W_MAIN = 5 * D_MODEL + KV_WIDTH
ONES_ROWS = 16


def _const_spec(shape):
    nd = len(shape)
    return pl.BlockSpec(shape, lambda *_: (0,) * nd, pipeline_mode=pl.Buffered(1))


def _main_body(x_ref, yssm_ref, shift_ref, scale_ref, gate_ref, wmain_ref, wvt_ref, biast_ref, sink_ref, eye_ref,
               wglu_ref, bglu_ref, wbs_ref, wba_ref, wout_ref, lng_ref, lnb_ref,
               y_ref, ktail_ref, vtail_ref,
               zs_s, q_s, za_s, gs_s, ga_s, k_s, vt_s, ot_s, o_s):
    i = pl.program_id(1)
    x = x_ref[...]
    hb = (x * (1.0 + scale_ref[...]) + shift_ref[...]).astype(BF16)

    for dst, c0 in ((zs_s, 0), (q_s, D_MODEL), (za_s, 2 * D_MODEL), (gs_s, 3 * D_MODEL), (ga_s, 4 * D_MODEL)):
        for c in range(2):
            dst[:, c * 512:(c + 1) * 512] = _dot(hb, wmain_ref[:, c0 + c * 512:c0 + (c + 1) * 512]).astype(BF16)

    @pl.when(i == 0)
    def _():
        k_s[0:WINDOW, :] = jnp.zeros((WINDOW, KV_WIDTH), BF16)
        vt_s[:, 0:WINDOW] = jnp.zeros((KV_WIDTH, WINDOW), BF16)

    k_s[WINDOW:WINDOW + TB, :] = _dot(hb, wmain_ref[:, 5 * D_MODEL:W_MAIN]).astype(BF16)
    vt_s[:, WINDOW:WINDOW + TB] = _dot_nt(wvt_ref[...], hb).astype(BF16)

    first = jnp.where(i == 0, 1, 0)
    lane = lax.broadcasted_iota(jnp.int32, (2 * WINDOW, 128), 1)
    ones = jnp.ones((ONES_ROWS, 2 * WINDOW), BF16)
    for blk in range(NQB):
        rows = slice(blk * WINDOW, (blk + 1) * WINDOW)
        kcat = k_s[blk * WINDOW:blk * WINDOW + 2 * WINDOW, :]
        vtc = vt_s[:, blk * WINDOW:blk * WINDOW + 2 * WINDOW]
        sel = first if blk == 0 else 0
        for kh in range(N_KV_HEADS):
            pair, odd = kh // 2, kh % 2
            kcm = jnp.where((lane >= HEAD_DIM) == bool(odd), kcat[:, pair * 128:(pair + 1) * 128], jnp.zeros((), BF16))
            qp = jnp.concatenate(
                [q_s[rows, j * KV_WIDTH + pair * 128:j * KV_WIDTH + (pair + 1) * 128] for j in range(KV_GROUP)],
                axis=0)
            st = _dot_nt(kcm, qp) + biast_ref[sel, kh]
            sink = sink_ref[kh]
            m = jnp.maximum(jnp.max(st, axis=0, keepdims=True), sink)
            e = jnp.exp(st - m).astype(BF16)
            lhs = jnp.concatenate([vtc[kh * HEAD_DIM:(kh + 1) * HEAD_DIM, :], ones], axis=0)
            r = _dot(lhs, e)
            den = r[HEAD_DIM:HEAD_DIM + 1, :] + jnp.exp(sink - m)
            ot = r[:HEAD_DIM, :] / den
            for j in range(KV_GROUP):
                ot_s[blk, j, kh * HEAD_DIM:(kh + 1) * HEAD_DIM, :] = ot[:, j * WINDOW:(j + 1) * WINDOW]
        for j in range(KV_GROUP):
            o_s[rows, j * KV_WIDTH:(j + 1) * KV_WIDTH] = _dot_nt(eye_ref[...], ot_s[blk, j].astype(BF16))

    @pl.when(i == pl.num_programs(1) - 1)
    def _():
        ktail_ref[...] = k_s[TB:TB + WINDOW, :].astype(F32)
        vtail_ref[...] = _dot_nt(eye_ref[...], vt_s[:, TB:TB + WINDOW])

    k_s[0:WINDOW, :] = k_s[TB:TB + WINDOW, :]
    vt_s[:, 0:WINDOW] = vt_s[:, TB:TB + WINDOW]

    y_ref[...] = _post(yssm_ref[...], zs_s[...], o_s[...], za_s[...], gs_s[...], ga_s[...],
                       x, gate_ref[...], wglu_ref, bglu_ref, wbs_ref, wba_ref, wout_ref, lng_ref, lnb_ref)


def _main_prompt(x2d, yssm, modp, wmain, wvt, biast, sink_rows, eye, wglu, bglu, wbs, wba, wout, lng, lnb,
                 n_batch, seq):
    steps = seq // TB
    row_spec = pl.BlockSpec((TB, D_MODEL), lambda b, i: (b * steps + i, 0))
    mod_spec = lambda j: pl.BlockSpec((None, 1, D_MODEL), lambda b, i, j=j: (b * 3 + j, 0, 0))
    tail_spec = pl.BlockSpec((None, WINDOW, KV_WIDTH), lambda b, i: (b, 0, 0))
    consts = (wmain, wvt, biast, sink_rows, eye, wglu, bglu, wbs, wba, wout, lng, lnb)
    return pl.pallas_call(
        _main_body,
        grid=(n_batch, steps),
        in_specs=[row_spec, row_spec, mod_spec(0), mod_spec(1), mod_spec(2)] + [_const_spec(c.shape) for c in consts],
        out_specs=[row_spec, tail_spec, tail_spec],
        out_shape=[jax.ShapeDtypeStruct((n_batch * seq, D_MODEL), F32),
                   jax.ShapeDtypeStruct((n_batch, WINDOW, KV_WIDTH), F32),
                   jax.ShapeDtypeStruct((n_batch, WINDOW, KV_WIDTH), F32)],
        scratch_shapes=[pltpu.VMEM((TB, D_MODEL), BF16)] * 5
        + [pltpu.VMEM((TB + WINDOW, KV_WIDTH), BF16), pltpu.VMEM((KV_WIDTH, TB + WINDOW), BF16),
           pltpu.VMEM((NQB, KV_GROUP, KV_WIDTH, WINDOW), F32), pltpu.VMEM((TB, D_MODEL), F32)],
        compiler_params=pltpu.CompilerParams(dimension_semantics=("arbitrary", "arbitrary"),
                                             vmem_limit_bytes=VMEM_LIMIT),
        name="main_prompt",
    )(x2d, yssm, modp, modp, modp, *consts)


GT = 8


def _sample_ssm_body(u_ref, h0r_ref, h0i_ref, ar_ref, ai_ref, bre_ref, bim_ref, cblk_ref, d_ref,
                     y_ref, hr_ref, hi_ref):
    ub = u_ref[...]
    ys = []
    for a in range(GT):
        ua = ub[:, a * 128:(a + 1) * 128]
        st = slice(a * 512, (a + 1) * 512)
        xr = _dot(ua, bre_ref[a])
        xi = _dot(ua, bim_ref[a])
        h0r = h0r_ref[:, st]
        h0i = h0i_ref[:, st]
        ar = ar_ref[:, st]
        ai = ai_ref[:, st]
        hr = ar * h0r - ai * h0i + xr
        hi = ar * h0i + ai * h0r + xi
        hr_ref[:, st] = hr
        hi_ref[:, st] = hi
        hcat = jnp.concatenate([hr, hi], axis=1).astype(BF16)
        ys.append(_dot(hcat, cblk_ref[a]))
    y_ref[...] = jnp.concatenate(ys, axis=1) + d_ref[...] * ub.astype(F32)


def _sample_ssm(proj_s, h0r, h0i, ar, ai, bre, bim, cblk, d_row):
    rows = proj_s.shape[0]
    ns = SSM_GROUPS * SSM_STATE
    full = lambda shape: pl.BlockSpec(shape, lambda i: (0,) * len(shape))
    return pl.pallas_call(
        _sample_ssm_body,
        grid=(1,),
        in_specs=[pl.BlockSpec((rows, D_MODEL), lambda i: (0, COL_U)),
                  full((rows, ns)), full((rows, ns)), full((1, ns)), full((1, ns)),
                  full(bre.shape), full(bim.shape), full(cblk.shape), full((1, D_MODEL))],
        out_specs=[full((rows, D_MODEL)), full((rows, ns)), full((rows, ns))],
        out_shape=[jax.ShapeDtypeStruct((rows, D_MODEL), F32),
                   jax.ShapeDtypeStruct((rows, ns), F32),
                   jax.ShapeDtypeStruct((rows, ns), F32)],
        compiler_params=pltpu.CompilerParams(vmem_limit_bytes=VMEM_LIMIT),
        name="sample_ssm",
    )(proj_s, h0r, h0i, ar, ai, bre, bim, cblk, d_row)


SB = 16
KROWS = WINDOW + 8


def _sample_attn_body(q_ref, kn_ref, vn_ref, ck_ref, cv_ref, bias_ref, sink_ref, seg_ref, segt_ref,
                      o_ref, nk_ref, nv_ref):
    qf = q_ref[...].astype(F32)
    knf = kn_ref[...].astype(F32)
    vnf = vn_ref[...].astype(F32)
    rowk = lax.broadcasted_iota(jnp.int32, (KROWS, 128), 0)
    roww = lax.broadcasted_iota(jnp.int32, (WINDOW, KV_WIDTH), 0)
    sink = sink_ref[...]
    for b in range(SB):
        kb = ck_ref[b]
        vb = cv_ref[b]
        qrow = qf[b:b + 1]
        kn = jnp.broadcast_to(knf[b:b + 1], (8, KV_WIDTH))
        vn = jnp.broadcast_to(vnf[b:b + 1], (8, KV_WIDTH))
        kall = jnp.concatenate([kb, kn], axis=0)
        vall = jnp.concatenate([vb, vn], axis=0)
        prod = jnp.concatenate([kall * qrow[:, j * KV_WIDTH:(j + 1) * KV_WIDTH] for j in range(KV_GROUP)], axis=1)
        s = _dot(prod.astype(BF16), seg_ref[...]) + bias_ref[...]
        s = jnp.where(rowk <= WINDOW, s, NEG_INF)
        m = jnp.maximum(jnp.max(s, axis=0, keepdims=True), sink)
        e = jnp.exp(s - m)
        den = jnp.sum(e, axis=0, keepdims=True) + jnp.exp(sink - m)
        p = e / den
        pexp = _dot(p.astype(BF16), segt_ref[...])
        v4 = jnp.concatenate([vall] * KV_GROUP, axis=1)
        o_ref[b:b + 1, :] = jnp.sum(pexp * v4, axis=0, keepdims=True)
        last = roww == WINDOW - 1
        nk_ref[b] = jnp.where(last, jnp.broadcast_to(knf[b:b + 1], (WINDOW, KV_WIDTH)), pltpu.roll(kb, WINDOW - 1, 0))
        nv_ref[b] = jnp.where(last, jnp.broadcast_to(vnf[b:b + 1], (WINDOW, KV_WIDTH)), pltpu.roll(vb, WINDOW - 1, 0))


def _sample_attn(proj_s, ck, cv, bias_s, sink_row, seg, segt):
    rows = proj_s.shape[0]
    cache_spec = pl.BlockSpec((SB, WINDOW, KV_WIDTH), lambda i: (i, 0, 0))
    return pl.pallas_call(
        _sample_attn_body,
        grid=(rows // SB,),
        in_specs=[pl.BlockSpec((SB, D_MODEL), lambda i: (i, COL_Q)),
                  pl.BlockSpec((SB, KV_WIDTH), lambda i: (i, COL_K)),
                  pl.BlockSpec((SB, KV_WIDTH), lambda i: (i, COL_V)),
                  cache_spec, cache_spec,
                  pl.BlockSpec(bias_s.shape, lambda i: (0, 0)),
                  pl.BlockSpec(sink_row.shape, lambda i: (0, 0)),
                  pl.BlockSpec(seg.shape, lambda i: (0, 0)),
                  pl.BlockSpec(segt.shape, lambda i: (0, 0))],
        out_specs=[pl.BlockSpec((SB, D_MODEL), lambda i: (i, 0)), cache_spec, cache_spec],
        out_shape=[jax.ShapeDtypeStruct((rows, D_MODEL), F32),
                   jax.ShapeDtypeStruct(ck.shape, F32),
                   jax.ShapeDtypeStruct(cv.shape, F32)],
        compiler_params=pltpu.CompilerParams(vmem_limit_bytes=VMEM_LIMIT),
        name="sample_attn",
    )(proj_s, proj_s, proj_s, ck, cv, bias_s, sink_row, seg, segt)


def _sample_final_body(x_ref, yssm_ref, zs_ref, o_ref, za_ref, gs_ref, ga_ref, gate_ref,
                       wglu_ref, bglu_ref, wbs_ref, wba_ref, wout_ref, lng_ref, lnb_ref, y_ref):
    y_ref[...] = _post(yssm_ref[...], zs_ref[...], o_ref[...], za_ref[...], gs_ref[...], ga_ref[...],
                       x_ref[...], gate_ref[...], wglu_ref, bglu_ref, wbs_ref, wba_ref, wout_ref,
                       lng_ref, lnb_ref)


def _final_sample(x2d, yssm, o_a, proj_s, mods, wglu, bglu, wbs, wba, wout, lng, lnb):
    rows = x2d.shape[0]
    full = lambda shape: pl.BlockSpec(shape, lambda i: (0,) * len(shape))
    pcol = lambda c: pl.BlockSpec((rows, D_MODEL), lambda i, c=c: (0, c))
    return pl.pallas_call(
        _sample_final_body,
        grid=(1,),
        in_specs=[full((rows, D_MODEL)), full((rows, D_MODEL)), pcol(COL_ZS), full((rows, D_MODEL)),
                  pcol(COL_ZA), pcol(COL_GS), pcol(COL_GA),
                  pl.BlockSpec((rows, D_MODEL), lambda i: (0, 2)),
                  full(wglu.shape), full(bglu.shape), full(wbs.shape), full(wba.shape), full(wout.shape),
                  full(lng.shape), full(lnb.shape)],
        out_specs=full((rows, D_MODEL)),
        out_shape=jax.ShapeDtypeStruct((rows, D_MODEL), F32),
        compiler_params=pltpu.CompilerParams(vmem_limit_bytes=VMEM_LIMIT),
        name="final_sample",
    )(x2d, yssm, proj_s, o_a, proj_s, proj_s, proj_s, mods, wglu, bglu, wbs, wba, wout, lng, lnb)


def _rel_bucket(dist):
    max_exact = N_BUCKETS // 2
    df = jnp.maximum(dist, 1).astype(F32)
    large = max_exact + (jnp.log(df / max_exact) / math.log(WINDOW / max_exact)
                         * (N_BUCKETS - max_exact)).astype(jnp.int32)
    large = jnp.minimum(large, N_BUCKETS - 1)
    return jnp.where(dist < max_exact, dist, large)


def _ssm_params(lam_re, lam_im, log_delta, b_re, b_im, c_re, c_im, d_skip):
    hp = lax.Precision.HIGHEST
    g, p, c = SSM_GROUPS, SSM_STATE, SSM_GROUP
    lr, li = lam_re.astype(F32), lam_im.astype(F32)
    dt = jnp.exp(log_delta.astype(F32))[:, None]
    mag = jnp.exp(lr * dt)
    ar, ai = mag * jnp.cos(li * dt), mag * jnp.sin(li * dt)
    den = lr * lr + li * li
    nr = ar - 1.0
    cr_ = (nr * lr + ai * li) / den
    ci_ = (ai * lr - nr * li) / den
    bbr = cr_[..., None] * b_re - ci_[..., None] * b_im
    bbi = cr_[..., None] * b_im + ci_[..., None] * b_re

    def powers(taus):
        t = jnp.asarray(taus, F32)[:, None, None]
        m = jnp.exp(lr * dt * t)
        return m * jnp.cos(li * dt * t), m * jnp.sin(li * dt * t)

    pr, pi = powers(np.arange(CHUNK + 1))
    wr = pr[:CHUNK, :, :, None] * bbr - pi[:CHUNK, :, :, None] * bbi
    wi = pr[:CHUNK, :, :, None] * bbi + pi[:CHUNK, :, :, None] * bbr
    kt = (jnp.einsum('gcp,tgpd->tgcd', c_re, wr, precision=hp)
          - jnp.einsum('gcp,tgpd->tgcd', c_im, wi, precision=hp))
    tt = np.arange(CHUNK)[:, None]
    ss = np.arange(CHUNK)[None, :]
    lag = np.clip(tt - ss, 0, CHUNK - 1)
    causal = jnp.asarray((tt >= ss), F32)
    m5 = kt[lag] * causal[:, :, None, None, None]
    m5 = m5 + (jnp.eye(CHUNK, dtype=F32)[:, :, None, None, None]
               * (d_skip.reshape(g, c)[None, None, :, :, None] * jnp.eye(c, dtype=F32)[None, None, None]))
    mt = jnp.transpose(m5, (2, 1, 4, 0, 3)).reshape(g, CW, CW)
    rev = np.arange(CHUNK)[::-1]
    bpr = jnp.transpose(wr[rev], (1, 0, 3, 2)).reshape(g, CW, p)
    bpi = jnp.transpose(wi[rev], (1, 0, 3, 2)).reshape(g, CW, p)
    rhs = jnp.concatenate([mt, bpr, bpi], axis=2).astype(BF16)
    p1r, p1i = pr[1:], pi[1:]
    cpr = c_re[None] * jnp.transpose(p1r, (0, 1, 2))[:, :, None, :] - c_im[None] * p1i[:, :, None, :]
    cpi = -(c_re[None] * p1i[:, :, None, :] + c_im[None] * p1r[:, :, None, :])
    cpow = jnp.concatenate([jnp.transpose(cpr, (1, 3, 0, 2)).reshape(g, p, CW),
                            jnp.transpose(cpi, (1, 3, 0, 2)).reshape(g, p, CW)], axis=1).astype(BF16)
    qr, qi = powers(CHUNK * np.arange(1, 9))
    c1 = jnp.concatenate([qr, qr], axis=2)
    c2 = jnp.concatenate([-qi, qi], axis=2)
    log_rows = jnp.stack([c1[0], c2[0], c1[1], c2[1], c1[3], c2[3], c1[7], c2[7]], axis=1)
    coef = jnp.concatenate([log_rows, jnp.transpose(c1, (1, 0, 2)), jnp.transpose(c2, (1, 0, 2))], axis=1)

    eye8 = jnp.eye(GT, dtype=F32)
    bre = jnp.einsum('agpc,gh->agchp', bbr.reshape(GT, GT, p, c), eye8).reshape(GT, GT * c, GT * p).astype(BF16)
    bim = jnp.einsum('agpc,gh->agchp', bbi.reshape(GT, GT, p, c), eye8).reshape(GT, GT * c, GT * p).astype(BF16)
    cre = jnp.einsum('agcp,gh->agphc', c_re.reshape(GT, GT, c, p), eye8).reshape(GT, GT * p, GT * c)
    cim = jnp.einsum('agcp,gh->agphc', c_im.reshape(GT, GT, c, p), eye8).reshape(GT, GT * p, GT * c)
    cblk = jnp.concatenate([cre, -cim], axis=1).astype(BF16)
    return rhs, cpow, coef, ar.reshape(1, -1), ai.reshape(1, -1), bre, bim, cblk


def _attn_params(rel_bias, sinks):
    rb = rel_bias.astype(F32)
    qi = jnp.arange(WINDOW)[:, None]
    kj = jnp.arange(2 * WINDOW)[None, :]
    bk = _rel_bucket(jnp.clip(qi + WINDOW - kj, 0, WINDOW))
    bq = rb[bk]
    dist = qi + WINDOW - kj
    band = (dist >= 0) & (dist <= WINDOW)
    band_first = band & (kj >= WINDOW)
    masks = jnp.stack([band, band_first], axis=0)
    bm = jnp.where(masks[:, :, :, None], bq[None], NEG_INF)
    biast = jnp.transpose(bm.reshape(2, WINDOW, 2 * WINDOW, N_KV_HEADS, KV_GROUP), (0, 3, 2, 4, 1)) \
        .reshape(2, N_KV_HEADS, 2 * WINDOW, KV_GROUP * WINDOW)
    sink_rows = jnp.repeat(sinks.astype(F32).reshape(N_KV_HEADS, 1, KV_GROUP), WINDOW, axis=2)
    ds = jnp.clip(WINDOW - jnp.arange(KROWS), 0, WINDOW)
    bs = rb[_rel_bucket(ds)]
    bs = jnp.transpose(bs.reshape(KROWS, N_KV_HEADS, KV_GROUP), (0, 2, 1)).reshape(KROWS, N_HEADS)
    bias_s = jnp.pad(bs, ((0, 0), (0, 128 - N_HEADS)))
    sk = jnp.transpose(sinks.astype(F32).reshape(N_KV_HEADS, KV_GROUP), (1, 0)).reshape(1, N_HEADS)
    sink_row = jnp.pad(sk, ((0, 0), (0, 128 - N_HEADS)))
    seg_np = np.zeros((D_MODEL, 128), np.float32)
    for j in range(KV_GROUP):
        for kh in range(N_KV_HEADS):
            r0 = j * KV_WIDTH + kh * HEAD_DIM
            seg_np[r0:r0 + HEAD_DIM, j * N_KV_HEADS + kh] = 1.0
    seg = jnp.asarray(seg_np, BF16)
    segt = jnp.asarray(seg_np.T, BF16)
    return biast, sink_rows, bias_s, sink_row, seg, segt


def kernel(x_prompt, x_sample, c_prompt, c_sample, state_ssm_re, state_ssm_im, cache_swa_k, cache_swa_v,
           w_ada, b_ada, w_in, ssm_lambda_re, ssm_lambda_im, ssm_log_delta, ssm_b_re, ssm_b_im,
           ssm_c_re, ssm_c_im, ssm_d, w_glu, b_glu, attn_sinks, rel_bias, w_branch_s, w_branch_a,
           w_out, ln_g, ln_b):
    assert w_ada.shape[0] == 1, "single-layer trunk"
    n_batch, seq, _ = x_prompt.shape
    n_dec = x_sample.shape[0]
    perm = _head_perm()

    wi = w_in[0]
    w_in_b = jnp.concatenate([wi[:, 0:1024], wi[:, 1024:2048], wi[:, 2048:3072][:, perm] * (HEAD_DIM ** -0.5),
                              wi[:, 3584:4608][:, perm], wi[:, 4608:5632], wi[:, 5632:6656],
                              wi[:, 3072:3328], wi[:, 3328:3584]], axis=1).astype(BF16)
    wmain = w_in_b[:, D_MODEL:D_MODEL + W_MAIN]
    wvt = jnp.transpose(wi[:, 3328:3584]).astype(BF16)
    eye = jnp.eye(WINDOW, dtype=BF16)
    wglu = w_glu[0].astype(BF16)
    bglu = b_glu[0].reshape(1, -1).astype(F32)
    wbs = w_branch_s[0].astype(BF16)
    wba = w_branch_a[0][perm, :].astype(BF16)
    wout = w_out[0].astype(BF16)
    lng = ln_g[0].reshape(1, -1).astype(F32)
    lnb = ln_b[0].reshape(1, -1).astype(F32)
    rhs, cpow, coef, ar_row, ai_row, bre, bim, cblk = _ssm_params(
        ssm_lambda_re[0], ssm_lambda_im[0], ssm_log_delta[0], ssm_b_re[0], ssm_b_im[0],
        ssm_c_re[0], ssm_c_im[0], ssm_d[0])
    biast, sink_rows, bias_s, sink_row, seg, segt = _attn_params(rel_bias, attn_sinks[0])

    c_all = jnp.concatenate([c_prompt, jnp.zeros((8 - n_batch, D_MODEL), F32), c_sample], axis=0)
    mod = _ada(c_all, w_ada[0], b_ada[0])
    modp = mod[:n_batch].reshape(n_batch * 3, 1, D_MODEL)
    mods = mod[8:8 + n_dec]

    xp = x_prompt.reshape(n_batch * seq, D_MODEL)
    tm = 512
    steps_per_b = seq // tm
    proj_p = _proj(
        xp, modp, modp,
        pl.BlockSpec((None, 1, D_MODEL), lambda i: ((i // steps_per_b) * 3 + 0, 0, 0)),
        pl.BlockSpec((None, 1, D_MODEL), lambda i: ((i // steps_per_b) * 3 + 1, 0, 0)),
        w_in_b, tm, D_MODEL)
    nk_all = n_batch * seq // CHUNK
    z = jnp.transpose(proj_p.reshape(nk_all, CHUNK, SSM_GROUPS, SSM_GROUP), (2, 0, 1, 3)) \
        .reshape(SSM_GROUPS, nk_all, CW)
    yz, hfin = _ssm_prompt(z, rhs, cpow, coef, n_batch)
    yssm = jnp.transpose(yz.reshape(SSM_GROUPS, nk_all, CHUNK, SSM_GROUP), (1, 2, 0, 3)) \
        .reshape(n_batch * seq, D_MODEL)
    yp, ktail, vtail = _main_prompt(xp, yssm, modp, wmain, wvt, biast, sink_rows, eye,
                                    wglu, bglu, wbs, wba, wout, lng, lnb, n_batch, seq)
    yp = yp.reshape(n_batch, seq, D_MODEL)
    p_hr = hfin[:, :, 0, :SSM_STATE][None]
    p_hi = hfin[:, :, 0, SSM_STATE:][None]
    p_k = ktail.reshape(1, n_batch, WINDOW, N_KV_HEADS, HEAD_DIM)
    p_v = vtail.reshape(1, n_batch, WINDOW, N_KV_HEADS, HEAD_DIM)

    xs = x_sample.reshape(n_dec, D_MODEL)
    proj_s = _proj(xs, mods, mods,
                   pl.BlockSpec((n_dec, D_MODEL), lambda i: (0, 0)),
                   pl.BlockSpec((n_dec, D_MODEL), lambda i: (0, 1)),
                   w_in_b, n_dec, D_IN)
    ns = SSM_GROUPS * SSM_STATE
    ys_s, s_hr, s_hi = _sample_ssm(proj_s, state_ssm_re[0].reshape(n_dec, ns), state_ssm_im[0].reshape(n_dec, ns),
                                   ar_row, ai_row, bre, bim, cblk, ssm_d[0].reshape(1, -1).astype(F32))
    o_s, nk, nv = _sample_attn(proj_s, cache_swa_k[0].reshape(n_dec, WINDOW, KV_WIDTH),
                               cache_swa_v[0].reshape(n_dec, WINDOW, KV_WIDTH), bias_s, sink_row, seg, segt)
    ysmp = _final_sample(xs, ys_s, o_s, proj_s, mods, wglu, bglu, wbs, wba, wout, lng, lnb)

    return (yp, ysmp.reshape(n_dec, 1, D_MODEL),
            p_hr, p_hi, p_k, p_v,
            s_hr.reshape(1, n_dec, SSM_GROUPS, SSM_STATE), s_hi.reshape(1, n_dec, SSM_GROUPS, SSM_STATE),
            nk.reshape(1, n_dec, WINDOW, N_KV_HEADS, HEAD_DIM), nv.reshape(1, n_dec, WINDOW, N_KV_HEADS, HEAD_DIM))
```

```python
import functools
import math

import numpy as np
import jax
import jax.numpy as jnp
from jax import lax
from jax.experimental import pallas as pl
from jax.experimental.pallas import tpu as pltpu

F32 = jnp.float32
BF16 = jnp.bfloat16

D_MODEL = 1024
SSM_GROUPS = 64
SSM_GROUP = 16
SSM_STATE = 64
N_HEADS = 16
HEAD_DIM = 64
N_KV_HEADS = 4
KV_GROUP = 4
KV_WIDTH = N_KV_HEADS * HEAD_DIM
WINDOW = 128
N_BUCKETS = 32
NEG_INF = -1e30
LN_EPS = 1e-5
DEPTH = 1
DEEPNORM_ALPHA = (2 * DEPTH) ** 0.25
D_IN = 6656
CHUNK = 16
CW = CHUNK * SSM_GROUP
SW = 2 * SSM_STATE

COL_U, COL_ZS, COL_Q, COL_ZA, COL_GS, COL_GA = 0, 1, 2, 3, 4, 5
COL_K, COL_V = 24, 25

VMEM_LIMIT = 56 * 1024 * 1024


def _head_perm():
    return np.arange(N_HEADS * HEAD_DIM).reshape(N_KV_HEADS, KV_GROUP, HEAD_DIM).transpose(1, 0, 2).reshape(-1)


def _sigmoid(x):
    return 1.0 / (1.0 + jnp.exp(-x))


def _silu(x):
    return x * _sigmoid(x)


def _gelu_tanh(x):
    return 0.5 * x * (1.0 + jnp.tanh(math.sqrt(2.0 / math.pi) * (x + 0.044715 * (x * x * x))))


def _dot(a, b):
    return jnp.dot(a, b, preferred_element_type=F32)


def _dot_nt(a, b):
    return lax.dot_general(a, b, (((1,), (1,)), ((), ())), preferred_element_type=F32)


def _ada_body(c_ref, w_ref, b_ref, o_ref):
    c = c_ref[...]
    sc = _silu(c).astype(BF16)
    o_ref[...] = _dot(sc, w_ref[...].astype(BF16)) + b_ref[...]


def _ada(c_all, w_ada, b_ada):
    rows = c_all.shape[0]
    bn = 512
    return pl.pallas_call(
        _ada_body,
        grid=(3 * D_MODEL // bn,),
        in_specs=[pl.BlockSpec((rows, D_MODEL), lambda j: (0, 0)),
                  pl.BlockSpec((D_MODEL, bn), lambda j: (0, j)),
                  pl.BlockSpec((1, bn), lambda j: (0, j))],
        out_specs=pl.BlockSpec((rows, bn), lambda j: (0, j)),
        out_shape=jax.ShapeDtypeStruct((rows, 3 * D_MODEL), F32),
        name="ada",
    )(c_all, w_ada, b_ada.reshape(1, -1))


PROJ_CB = 512


def _proj_body(x_ref, shift_ref, scale_ref, w_ref, o_ref, *, n_out):
    h = x_ref[...] * (1.0 + scale_ref[...]) + shift_ref[...]
    hb = h.astype(BF16)
    for j in range(n_out // PROJ_CB):
        sl = slice(j * PROJ_CB, (j + 1) * PROJ_CB)
        o_ref[:, sl] = _dot(hb, w_ref[:, sl]).astype(BF16)


def _proj(x2d, shift_arr, scale_arr, shift_spec, scale_spec, w_in_b, tm, n_out):
    rows = x2d.shape[0]
    return pl.pallas_call(
        functools.partial(_proj_body, n_out=n_out),
        grid=(rows // tm,),
        in_specs=[pl.BlockSpec((tm, D_MODEL), lambda i: (i, 0)),
                  shift_spec, scale_spec,
                  pl.BlockSpec((D_MODEL, n_out), lambda i: (0, 0), pipeline_mode=pl.Buffered(1))],
        out_specs=pl.BlockSpec((tm, n_out), lambda i: (i, 0)),
        out_shape=jax.ShapeDtypeStruct((rows, n_out), BF16),
        compiler_params=pltpu.CompilerParams(vmem_limit_bytes=VMEM_LIMIT),
        name="proj",
    )(x2d, shift_arr, scale_arr, w_in_b)


NKS = 128
GC = 16


def _cmul(c1, c2, x):
    return c1 * x + c2 * pltpu.roll(x, SSM_STATE, x.ndim - 1)


SSM_TOK = NKS * CHUNK
NPH = CHUNK // 2


def _dot_tn(a, b):
    return lax.dot_general(a, b, (((0,), (0,)), ((), ())), preferred_element_type=F32)


def _ssm_body(x_ref, shift_ref, scale_ref, wut_ref, lhs_ref, bpow_ref, cpow_ref, coef_ref, eye_ref,
              y_ref, hfin_ref, r_scr, yt_scr, s_scr, carry_scr):
    i = pl.program_id(1)
    j = pl.program_id(2)

    @pl.when((i == 0) & (j == 0))
    def _():
        carry_scr[...] = jnp.zeros_like(carry_scr)

    @pl.when(j < NPH)
    def _():
        h = jnp.concatenate([x_ref[:, :D_MODEL], x_ref[:, D_MODEL:]], axis=0)
        hb = (h * (1.0 + scale_ref[...]) + shift_ref[...]).astype(BF16)
        ut = _dot_nt(wut_ref[...], hb).astype(BF16)
        ut = ut.reshape(SSM_GROUPS, SSM_GROUP, 2 * NKS)
        pair = jnp.concatenate([ut[:, :, :NKS], ut[:, :, NKS:]], axis=1)
        r_scr[:, pl.ds(pl.multiple_of(j * 2 * SSM_GROUP, 2 * SSM_GROUP), 2 * SSM_GROUP), :] = pair

    @pl.when(j == NPH - 1)
    def _():
        def local(g, c):
            rg = r_scr[g]
            yt_scr[g] = _dot(lhs_ref[g], rg)
            s_scr[g] = _dot_tn(rg, bpow_ref[g])
            return c

        lax.fori_loop(0, SSM_GROUPS, local, 0)

        row = lax.broadcasted_iota(jnp.int32, (GC, 8, SW), 1)
        for gc in range(SSM_GROUPS // GC):
            gs = slice(gc * GC, (gc + 1) * GC)
            coef = coef_ref[gs]
            h = carry_scr[gs]
            for r in range(NKS // 8):
                rs = slice(r * 8, (r + 1) * 8)
                blk = s_scr[gs, rs, :]
                for k, d in enumerate((1, 2, 4)):
                    sh = jnp.where(row >= d, pltpu.roll(blk, d, 1), 0.0)
                    blk = blk + _cmul(coef[:, 2 * k:2 * k + 1], coef[:, 2 * k + 1:2 * k + 2], sh)
                blk = blk + _cmul(coef[:, 8:16], coef[:, 16:24], h)
                s_scr[gs, rs, :] = jnp.where(row == 0, h, pltpu.roll(blk, 1, 1))
                h = jnp.broadcast_to(blk[:, 7:8, :], (GC, 8, SW))
            carry_scr[gs] = h
        hfin_ref[...] = carry_scr[...]

        def carried(g, c):
            yt_scr[g] += _dot_nt(cpow_ref[g], s_scr[g].astype(BF16))
            return c

        lax.fori_loop(0, SSM_GROUPS, carried, 0)

    @pl.when(j >= NPH)
    def _():
        t0 = pl.multiple_of((j - NPH) * 2 * SSM_GROUP, 2 * SSM_GROUP)
        pair = yt_scr[:, pl.ds(t0, 2 * SSM_GROUP), :]
        for half in range(2):
            yt = pair[:, half * SSM_GROUP:(half + 1) * SSM_GROUP, :].reshape(D_MODEL, NKS)
            hi = yt.astype(BF16)
            lo = (yt - hi.astype(F32)).astype(BF16)
            y_ref[:, half * D_MODEL:(half + 1) * D_MODEL] = _dot_nt(eye_ref[...], hi) + _dot_nt(eye_ref[...], lo)


def _ssm_prompt(x_prompt, modp, wut, lhs, bpow, cpow, coef, eye):
    n_batch, seq, _ = x_prompt.shape
    steps = seq // SSM_TOK
    xv = x_prompt.reshape(n_batch * steps, NKS, CHUNK * D_MODEL)
    mod_spec = lambda k: pl.BlockSpec((None, 1, D_MODEL), lambda b, i, j, k=k: (b * 3 + k, 0, 0))
    consts = (wut, lhs, bpow, cpow, coef, eye)
    y, hfin = pl.pallas_call(
        _ssm_body,
        grid=(n_batch, steps, 2 * NPH),
        in_specs=[pl.BlockSpec((None, NKS, 2 * D_MODEL), lambda b, i, j: (b * steps + i, 0, jnp.minimum(j, NPH - 1))),
                  mod_spec(0), mod_spec(1)] + [_const_spec(c.shape) for c in consts],
        out_specs=[pl.BlockSpec((None, NKS, 2 * D_MODEL), lambda b, i, j: (b * steps + i, 0, jnp.maximum(j - NPH, 0))),
                   pl.BlockSpec((None, SSM_GROUPS, 8, SW), lambda b, i, j: (b, 0, 0, 0))],
        out_shape=[jax.ShapeDtypeStruct((n_batch * steps, NKS, CHUNK * D_MODEL), F32),
                   jax.ShapeDtypeStruct((n_batch, SSM_GROUPS, 8, SW), F32)],
        scratch_shapes=[pltpu.VMEM((SSM_GROUPS, CW, NKS), BF16), pltpu.VMEM((SSM_GROUPS, CW, NKS), F32),
                        pltpu.VMEM((SSM_GROUPS, NKS, SW), F32), pltpu.VMEM((SSM_GROUPS, 8, SW), F32)],
        compiler_params=pltpu.CompilerParams(dimension_semantics=("arbitrary", "arbitrary", "arbitrary"),
                                             vmem_limit_bytes=VMEM_LIMIT),
        name="ssm_prompt",
    )(xv, modp, modp, *consts)
    return y.reshape(n_batch * seq, D_MODEL), hfin


def _post(ys, zs, o_a, za, gs, ga, x, gate, wglu_ref, bglu_ref, wbs_ref, wba_ref, wout_ref, lng_ref, lnb_ref):
    ys = _gelu_tanh(ys)
    glu = ys * _sigmoid(_dot(ys.astype(BF16), wglu_ref[...]) + bglu_ref[...])
    t = (glu * _silu(zs.astype(F32))).astype(BF16)
    b_s = _dot(t, wbs_ref[...])
    oz = (o_a * _silu(za.astype(F32))).astype(BF16)
    b_a = _dot(oz, wba_ref[...])
    m = _sigmoid(gs.astype(F32)) * b_s + _sigmoid(ga.astype(F32)) * b_a
    out = _dot(m.astype(BF16), wout_ref[...])
    r = DEEPNORM_ALPHA * x + gate * out
    mu = jnp.mean(r, axis=-1, keepdims=True)
    rc = r - mu
    var = jnp.mean(rc * rc, axis=-1, keepdims=True)
    return rc * lax.rsqrt(var + LN_EPS) * lng_ref[...] + lnb_ref[...]


TB = 256
NQB = TB // WINDOW
W_MAIN = 5 * D_MODEL + KV_WIDTH
ONES_ROWS = 16


def _const_spec(shape):
    nd = len(shape)
    return pl.BlockSpec(shape, lambda *_: (0,) * nd, pipeline_mode=pl.Buffered(1))


def _main_body(x_ref, yssm_ref, shift_ref, scale_ref, gate_ref, wmain_ref, wvt_ref, biast_ref, sink_ref, eye_ref,
               wglu_ref, bglu_ref, wbs_ref, wba_ref, wout_ref, lng_ref, lnb_ref,
               y_ref, ktail_ref, vtail_ref,
               zs_s, q_s, za_s, gs_s, ga_s, k_s, vt_s, ot_s, o_s):
    i = pl.program_id(1)
    x = x_ref[...]
    hb = (x * (1.0 + scale_ref[...]) + shift_ref[...]).astype(BF16)

    for dst, c0 in ((zs_s, 0), (q_s, D_MODEL), (za_s, 2 * D_MODEL), (gs_s, 3 * D_MODEL), (ga_s, 4 * D_MODEL)):
        for c in range(2):
            dst[:, c * 512:(c + 1) * 512] = _dot(hb, wmain_ref[:, c0 + c * 512:c0 + (c + 1) * 512]).astype(BF16)

    @pl.when(i == 0)
    def _():
        k_s[0:WINDOW, :] = jnp.zeros((WINDOW, KV_WIDTH), BF16)
        vt_s[:, 0:WINDOW] = jnp.zeros((KV_WIDTH, WINDOW), BF16)

    k_s[WINDOW:WINDOW + TB, :] = _dot(hb, wmain_ref[:, 5 * D_MODEL:W_MAIN]).astype(BF16)
    vt_s[:, WINDOW:WINDOW + TB] = _dot_nt(wvt_ref[...], hb).astype(BF16)

    first = jnp.where(i == 0, 1, 0)
    lane = lax.broadcasted_iota(jnp.int32, (2 * WINDOW, 128), 1)
    ones = jnp.ones((ONES_ROWS, 2 * WINDOW), BF16)
    for blk in range(NQB):
        rows = slice(blk * WINDOW, (blk + 1) * WINDOW)
        kcat = k_s[blk * WINDOW:blk * WINDOW + 2 * WINDOW, :]
        vtc = vt_s[:, blk * WINDOW:blk * WINDOW + 2 * WINDOW]
        sel = first if blk == 0 else 0
        for kh in range(N_KV_HEADS):
            pair, odd = kh // 2, kh % 2
            kcm = jnp.where((lane >= HEAD_DIM) == bool(odd), kcat[:, pair * 128:(pair + 1) * 128], jnp.zeros((), BF16))
            qp = jnp.concatenate(
                [q_s[rows, j * KV_WIDTH + pair * 128:j * KV_WIDTH + (pair + 1) * 128] for j in range(KV_GROUP)],
                axis=0)
            st = _dot_nt(kcm, qp) + biast_ref[sel, kh]
            sink = sink_ref[kh]
            m = jnp.maximum(jnp.max(st, axis=0, keepdims=True), sink)
            e = jnp.exp(st - m).astype(BF16)
            lhs = jnp.concatenate([vtc[kh * HEAD_DIM:(kh + 1) * HEAD_DIM, :], ones], axis=0)
            r = _dot(lhs, e)
            den = r[HEAD_DIM:HEAD_DIM + 1, :] + jnp.exp(sink - m)
            ot = r[:HEAD_DIM, :] / den
            for j in range(KV_GROUP):
                ot_s[blk, j, kh * HEAD_DIM:(kh + 1) * HEAD_DIM, :] = ot[:, j * WINDOW:(j + 1) * WINDOW]
        for j in range(KV_GROUP):
            o_s[rows, j * KV_WIDTH:(j + 1) * KV_WIDTH] = _dot_nt(eye_ref[...], ot_s[blk, j].astype(BF16))

    @pl.when(i == pl.num_programs(1) - 1)
    def _():
        ktail_ref[...] = k_s[TB:TB + WINDOW, :].astype(F32)
        vtail_ref[...] = _dot_nt(eye_ref[...], vt_s[:, TB:TB + WINDOW])

    k_s[0:WINDOW, :] = k_s[TB:TB + WINDOW, :]
    vt_s[:, 0:WINDOW] = vt_s[:, TB:TB + WINDOW]

    y_ref[...] = _post(yssm_ref[...], zs_s[...], o_s[...], za_s[...], gs_s[...], ga_s[...],
                       x, gate_ref[...], wglu_ref, bglu_ref, wbs_ref, wba_ref, wout_ref, lng_ref, lnb_ref)


def _main_prompt(x2d, yssm, modp, wmain, wvt, biast, sink_rows, eye, wglu, bglu, wbs, wba, wout, lng, lnb,
                 n_batch, seq):
    steps = seq // TB
    row_spec = pl.BlockSpec((TB, D_MODEL), lambda b, i: (b * steps + i, 0))
    mod_spec = lambda j: pl.BlockSpec((None, 1, D_MODEL), lambda b, i, j=j: (b * 3 + j, 0, 0))
    tail_spec = pl.BlockSpec((None, WINDOW, KV_WIDTH), lambda b, i: (b, 0, 0))
    consts = (wmain, wvt, biast, sink_rows, eye, wglu, bglu, wbs, wba, wout, lng, lnb)
    return pl.pallas_call(
        _main_body,
        grid=(n_batch, steps),
        in_specs=[row_spec, row_spec, mod_spec(0), mod_spec(1), mod_spec(2)] + [_const_spec(c.shape) for c in consts],
        out_specs=[row_spec, tail_spec, tail_spec],
        out_shape=[jax.ShapeDtypeStruct((n_batch * seq, D_MODEL), F32),
                   jax.ShapeDtypeStruct((n_batch, WINDOW, KV_WIDTH), F32),
                   jax.ShapeDtypeStruct((n_batch, WINDOW, KV_WIDTH), F32)],
        scratch_shapes=[pltpu.VMEM((TB, D_MODEL), BF16)] * 5
        + [pltpu.VMEM((TB + WINDOW, KV_WIDTH), BF16), pltpu.VMEM((KV_WIDTH, TB + WINDOW), BF16),
           pltpu.VMEM((NQB, KV_GROUP, KV_WIDTH, WINDOW), F32), pltpu.VMEM((TB, D_MODEL), F32)],
        compiler_params=pltpu.CompilerParams(dimension_semantics=("arbitrary", "arbitrary"),
                                             vmem_limit_bytes=VMEM_LIMIT),
        name="main_prompt",
    )(x2d, yssm, modp, modp, modp, *consts)


GT = 8


def _sample_ssm_body(u_ref, h0r_ref, h0i_ref, ar_ref, ai_ref, bre_ref, bim_ref, cblk_ref, d_ref,
                     y_ref, hr_ref, hi_ref):
    ub = u_ref[...]
    ys = []
    for a in range(GT):
        ua = ub[:, a * 128:(a + 1) * 128]
        st = slice(a * 512, (a + 1) * 512)
        xr = _dot(ua, bre_ref[a])
        xi = _dot(ua, bim_ref[a])
        h0r = h0r_ref[:, st]
        h0i = h0i_ref[:, st]
        ar = ar_ref[:, st]
        ai = ai_ref[:, st]
        hr = ar * h0r - ai * h0i + xr
        hi = ar * h0i + ai * h0r + xi
        hr_ref[:, st] = hr
        hi_ref[:, st] = hi
        hcat = jnp.concatenate([hr, hi], axis=1).astype(BF16)
        ys.append(_dot(hcat, cblk_ref[a]))
    y_ref[...] = jnp.concatenate(ys, axis=1) + d_ref[...] * ub.astype(F32)


def _sample_ssm(proj_s, h0r, h0i, ar, ai, bre, bim, cblk, d_row):
    rows = proj_s.shape[0]
    ns = SSM_GROUPS * SSM_STATE
    full = lambda shape: pl.BlockSpec(shape, lambda i: (0,) * len(shape))
    return pl.pallas_call(
        _sample_ssm_body,
        grid=(1,),
        in_specs=[pl.BlockSpec((rows, D_MODEL), lambda i: (0, COL_U)),
                  full((rows, ns)), full((rows, ns)), full((1, ns)), full((1, ns)),
                  full(bre.shape), full(bim.shape), full(cblk.shape), full((1, D_MODEL))],
        out_specs=[full((rows, D_MODEL)), full((rows, ns)), full((rows, ns))],
        out_shape=[jax.ShapeDtypeStruct((rows, D_MODEL), F32),
                   jax.ShapeDtypeStruct((rows, ns), F32),
                   jax.ShapeDtypeStruct((rows, ns), F32)],
        compiler_params=pltpu.CompilerParams(vmem_limit_bytes=VMEM_LIMIT),
        name="sample_ssm",
    )(proj_s, h0r, h0i, ar, ai, bre, bim, cblk, d_row)


SB = 16
KROWS = WINDOW + 8


def _sample_attn_body(q_ref, kn_ref, vn_ref, ck_ref, cv_ref, bias_ref, sink_ref, seg_ref, segt_ref,
                      o_ref, nk_ref, nv_ref):
    qf = q_ref[...].astype(F32)
    knf = kn_ref[...].astype(F32)
    vnf = vn_ref[...].astype(F32)
    rowk = lax.broadcasted_iota(jnp.int32, (KROWS, 128), 0)
    roww = lax.broadcasted_iota(jnp.int32, (WINDOW, KV_WIDTH), 0)
    sink = sink_ref[...]
    for b in range(SB):
        kb = ck_ref[b]
        vb = cv_ref[b]
        qrow = qf[b:b + 1]
        kn = jnp.broadcast_to(knf[b:b + 1], (8, KV_WIDTH))
        vn = jnp.broadcast_to(vnf[b:b + 1], (8, KV_WIDTH))
        kall = jnp.concatenate([kb, kn], axis=0)
        vall = jnp.concatenate([vb, vn], axis=0)
        prod = jnp.concatenate([kall * qrow[:, j * KV_WIDTH:(j + 1) * KV_WIDTH] for j in range(KV_GROUP)], axis=1)
        s = _dot(prod.astype(BF16), seg_ref[...]) + bias_ref[...]
        s = jnp.where(rowk <= WINDOW, s, NEG_INF)
        m = jnp.maximum(jnp.max(s, axis=0, keepdims=True), sink)
        e = jnp.exp(s - m)
        den = jnp.sum(e, axis=0, keepdims=True) + jnp.exp(sink - m)
        p = e / den
        pexp = _dot(p.astype(BF16), segt_ref[...])
        v4 = jnp.concatenate([vall] * KV_GROUP, axis=1)
        o_ref[b:b + 1, :] = jnp.sum(pexp * v4, axis=0, keepdims=True)
        last = roww == WINDOW - 1
        nk_ref[b] = jnp.where(last, jnp.broadcast_to(knf[b:b + 1], (WINDOW, KV_WIDTH)), pltpu.roll(kb, WINDOW - 1, 0))
        nv_ref[b] = jnp.where(last, jnp.broadcast_to(vnf[b:b + 1], (WINDOW, KV_WIDTH)), pltpu.roll(vb, WINDOW - 1, 0))


def _sample_attn(proj_s, ck, cv, bias_s, sink_row, seg, segt):
    rows = proj_s.shape[0]
    cache_spec = pl.BlockSpec((SB, WINDOW, KV_WIDTH), lambda i: (i, 0, 0))
    return pl.pallas_call(
        _sample_attn_body,
        grid=(rows // SB,),
        in_specs=[pl.BlockSpec((SB, D_MODEL), lambda i: (i, COL_Q)),
                  pl.BlockSpec((SB, KV_WIDTH), lambda i: (i, COL_K)),
                  pl.BlockSpec((SB, KV_WIDTH), lambda i: (i, COL_V)),
                  cache_spec, cache_spec,
                  pl.BlockSpec(bias_s.shape, lambda i: (0, 0)),
                  pl.BlockSpec(sink_row.shape, lambda i: (0, 0)),
                  pl.BlockSpec(seg.shape, lambda i: (0, 0)),
                  pl.BlockSpec(segt.shape, lambda i: (0, 0))],
        out_specs=[pl.BlockSpec((SB, D_MODEL), lambda i: (i, 0)), cache_spec, cache_spec],
        out_shape=[jax.ShapeDtypeStruct((rows, D_MODEL), F32),
                   jax.ShapeDtypeStruct(ck.shape, F32),
                   jax.ShapeDtypeStruct(cv.shape, F32)],
        compiler_params=pltpu.CompilerParams(vmem_limit_bytes=VMEM_LIMIT),
        name="sample_attn",
    )(proj_s, proj_s, proj_s, ck, cv, bias_s, sink_row, seg, segt)


def _sample_final_body(x_ref, yssm_ref, zs_ref, o_ref, za_ref, gs_ref, ga_ref, gate_ref,
                       wglu_ref, bglu_ref, wbs_ref, wba_ref, wout_ref, lng_ref, lnb_ref, y_ref):
    y_ref[...] = _post(yssm_ref[...], zs_ref[...], o_ref[...], za_ref[...], gs_ref[...], ga_ref[...],
                       x_ref[...], gate_ref[...], wglu_ref, bglu_ref, wbs_ref, wba_ref, wout_ref,
                       lng_ref, lnb_ref)


def _final_sample(x2d, yssm, o_a, proj_s, mods, wglu, bglu, wbs, wba, wout, lng, lnb):
    rows = x2d.shape[0]
    full = lambda shape: pl.BlockSpec(shape, lambda i: (0,) * len(shape))
    pcol = lambda c: pl.BlockSpec((rows, D_MODEL), lambda i, c=c: (0, c))
    return pl.pallas_call(
        _sample_final_body,
        grid=(1,),
        in_specs=[full((rows, D_MODEL)), full((rows, D_MODEL)), pcol(COL_ZS), full((rows, D_MODEL)),
                  pcol(COL_ZA), pcol(COL_GS), pcol(COL_GA),
                  pl.BlockSpec((rows, D_MODEL), lambda i: (0, 2)),
                  full(wglu.shape), full(bglu.shape), full(wbs.shape), full(wba.shape), full(wout.shape),
                  full(lng.shape), full(lnb.shape)],
        out_specs=full((rows, D_MODEL)),
        out_shape=jax.ShapeDtypeStruct((rows, D_MODEL), F32),
        compiler_params=pltpu.CompilerParams(vmem_limit_bytes=VMEM_LIMIT),
        name="final_sample",
    )(x2d, yssm, proj_s, o_a, proj_s, proj_s, proj_s, mods, wglu, bglu, wbs, wba, wout, lng, lnb)


def _rel_bucket(dist):
    max_exact = N_BUCKETS // 2
    df = jnp.maximum(dist, 1).astype(F32)
    large = max_exact + (jnp.log(df / max_exact) / math.log(WINDOW / max_exact)
                         * (N_BUCKETS - max_exact)).astype(jnp.int32)
    large = jnp.minimum(large, N_BUCKETS - 1)
    return jnp.where(dist < max_exact, dist, large)


def _ssm_params(lam_re, lam_im, log_delta, b_re, b_im, c_re, c_im, d_skip):
    hp = lax.Precision.HIGHEST
    g, p, c = SSM_GROUPS, SSM_STATE, SSM_GROUP
    lr, li = lam_re.astype(F32), lam_im.astype(F32)
    dt = jnp.exp(log_delta.astype(F32))[:, None]
    mag = jnp.exp(lr * dt)
    ar, ai = mag * jnp.cos(li * dt), mag * jnp.sin(li * dt)
    den = lr * lr + li * li
    nr = ar - 1.0
    cr_ = (nr * lr + ai * li) / den
    ci_ = (ai * lr - nr * li) / den
    bbr = cr_[..., None] * b_re - ci_[..., None] * b_im
    bbi = cr_[..., None] * b_im + ci_[..., None] * b_re

    def powers(taus):
        t = jnp.asarray(taus, F32)[:, None, None]
        m = jnp.exp(lr * dt * t)
        return m * jnp.cos(li * dt * t), m * jnp.sin(li * dt * t)

    pr, pi = powers(np.arange(CHUNK + 1))
    wr = pr[:CHUNK, :, :, None] * bbr - pi[:CHUNK, :, :, None] * bbi
    wi = pr[:CHUNK, :, :, None] * bbi + pi[:CHUNK, :, :, None] * bbr
    kt = (jnp.einsum('gcp,tgpd->tgcd', c_re, wr, precision=hp)
          - jnp.einsum('gcp,tgpd->tgcd', c_im, wi, precision=hp))
    tt = np.arange(CHUNK)[:, None]
    ss = np.arange(CHUNK)[None, :]
    lag = np.clip(tt - ss, 0, CHUNK - 1)
    causal = jnp.asarray((tt >= ss), F32)
    m5 = kt[lag] * causal[:, :, None, None, None]
    m5 = m5 + (jnp.eye(CHUNK, dtype=F32)[:, :, None, None, None]
               * (d_skip.reshape(g, c)[None, None, :, :, None] * jnp.eye(c, dtype=F32)[None, None, None]))
    lhs = jnp.transpose(m5, (2, 0, 3, 1, 4)).reshape(g, CW, CW).astype(BF16)
    rev = np.arange(CHUNK)[::-1]
    bpr = jnp.transpose(wr[rev], (1, 0, 3, 2)).reshape(g, CW, p)
    bpi = jnp.transpose(wi[rev], (1, 0, 3, 2)).reshape(g, CW, p)
    bpow = jnp.concatenate([bpr, bpi], axis=2).astype(BF16)
    p1r, p1i = pr[1:], pi[1:]
    cpr = c_re[None] * p1r[:, :, None, :] - c_im[None] * p1i[:, :, None, :]
    cpi = -(c_re[None] * p1i[:, :, None, :] + c_im[None] * p1r[:, :, None, :])
    cpow = jnp.concatenate([jnp.transpose(cpr, (1, 0, 2, 3)).reshape(g, CW, p),
                            jnp.transpose(cpi, (1, 0, 2, 3)).reshape(g, CW, p)], axis=2).astype(BF16)
    qr, qi = powers(CHUNK * np.arange(1, 9))
    c1 = jnp.concatenate([qr, qr], axis=2)
    c2 = jnp.concatenate([-qi, qi], axis=2)
    log_rows = jnp.stack([c1[0], c2[0], c1[1], c2[1], c1[3], c2[3], c1[7], c2[7]], axis=1)
    coef = jnp.concatenate([log_rows, jnp.transpose(c1, (1, 0, 2)), jnp.transpose(c2, (1, 0, 2))], axis=1)

    eye8 = jnp.eye(GT, dtype=F32)
    bre = jnp.einsum('agpc,gh->agchp', bbr.reshape(GT, GT, p, c), eye8).reshape(GT, GT * c, GT * p).astype(BF16)
    bim = jnp.einsum('agpc,gh->agchp', bbi.reshape(GT, GT, p, c), eye8).reshape(GT, GT * c, GT * p).astype(BF16)
    cre = jnp.einsum('agcp,gh->agphc', c_re.reshape(GT, GT, c, p), eye8).reshape(GT, GT * p, GT * c)
    cim = jnp.einsum('agcp,gh->agphc', c_im.reshape(GT, GT, c, p), eye8).reshape(GT, GT * p, GT * c)
    cblk = jnp.concatenate([cre, -cim], axis=1).astype(BF16)
    return lhs, bpow, cpow, coef, ar.reshape(1, -1), ai.reshape(1, -1), bre, bim, cblk


def _attn_params(rel_bias, sinks):
    rb = rel_bias.astype(F32)
    qi = jnp.arange(WINDOW)[:, None]
    kj = jnp.arange(2 * WINDOW)[None, :]
    bk = _rel_bucket(jnp.clip(qi + WINDOW - kj, 0, WINDOW))
    bq = rb[bk]
    dist = qi + WINDOW - kj
    band = (dist >= 0) & (dist <= WINDOW)
    band_first = band & (kj >= WINDOW)
    masks = jnp.stack([band, band_first], axis=0)
    bm = jnp.where(masks[:, :, :, None], bq[None], NEG_INF)
    biast = jnp.transpose(bm.reshape(2, WINDOW, 2 * WINDOW, N_KV_HEADS, KV_GROUP), (0, 3, 2, 4, 1)) \
        .reshape(2, N_KV_HEADS, 2 * WINDOW, KV_GROUP * WINDOW)
    sink_rows = jnp.repeat(sinks.astype(F32).reshape(N_KV_HEADS, 1, KV_GROUP), WINDOW, axis=2)
    ds = jnp.clip(WINDOW - jnp.arange(KROWS), 0, WINDOW)
    bs = rb[_rel_bucket(ds)]
    bs = jnp.transpose(bs.reshape(KROWS, N_KV_HEADS, KV_GROUP), (0, 2, 1)).reshape(KROWS, N_HEADS)
    bias_s = jnp.pad(bs, ((0, 0), (0, 128 - N_HEADS)))
    sk = jnp.transpose(sinks.astype(F32).reshape(N_KV_HEADS, KV_GROUP), (1, 0)).reshape(1, N_HEADS)
    sink_row = jnp.pad(sk, ((0, 0), (0, 128 - N_HEADS)))
    seg_np = np.zeros((D_MODEL, 128), np.float32)
    for j in range(KV_GROUP):
        for kh in range(N_KV_HEADS):
            r0 = j * KV_WIDTH + kh * HEAD_DIM
            seg_np[r0:r0 + HEAD_DIM, j * N_KV_HEADS + kh] = 1.0
    seg = jnp.asarray(seg_np, BF16)
    segt = jnp.asarray(seg_np.T, BF16)
    return biast, sink_rows, bias_s, sink_row, seg, segt


def kernel(x_prompt, x_sample, c_prompt, c_sample, state_ssm_re, state_ssm_im, cache_swa_k, cache_swa_v,
           w_ada, b_ada, w_in, ssm_lambda_re, ssm_lambda_im, ssm_log_delta, ssm_b_re, ssm_b_im,
           ssm_c_re, ssm_c_im, ssm_d, w_glu, b_glu, attn_sinks, rel_bias, w_branch_s, w_branch_a,
           w_out, ln_g, ln_b):
    assert w_ada.shape[0] == 1, "single-layer trunk"
    n_batch, seq, _ = x_prompt.shape
    n_dec = x_sample.shape[0]
    perm = _head_perm()

    wi = w_in[0]
    w_in_b = jnp.concatenate([wi[:, 0:1024], wi[:, 1024:2048], wi[:, 2048:3072][:, perm] * (HEAD_DIM ** -0.5),
                              wi[:, 3584:4608][:, perm], wi[:, 4608:5632], wi[:, 5632:6656],
                              wi[:, 3072:3328], wi[:, 3328:3584]], axis=1).astype(BF16)
    wmain = w_in_b[:, D_MODEL:D_MODEL + W_MAIN]
    wvt = jnp.transpose(wi[:, 3328:3584]).astype(BF16)
    wut = jnp.transpose(wi[:, 0:D_MODEL]).astype(BF16)
    eye = jnp.eye(WINDOW, dtype=BF16)
    wglu = w_glu[0].astype(BF16)
    bglu = b_glu[0].reshape(1, -1).astype(F32)
    wbs = w_branch_s[0].astype(BF16)
    wba = w_branch_a[0][perm, :].astype(BF16)
    wout = w_out[0].astype(BF16)
    lng = ln_g[0].reshape(1, -1).astype(F32)
    lnb = ln_b[0].reshape(1, -1).astype(F32)
    lhs, bpow, cpow, coef, ar_row, ai_row, bre, bim, cblk = _ssm_params(
        ssm_lambda_re[0], ssm_lambda_im[0], ssm_log_delta[0], ssm_b_re[0], ssm_b_im[0],
        ssm_c_re[0], ssm_c_im[0], ssm_d[0])
    biast, sink_rows, bias_s, sink_row, seg, segt = _attn_params(rel_bias, attn_sinks[0])

    c_all = jnp.concatenate([c_prompt, jnp.zeros((8 - n_batch, D_MODEL), F32), c_sample], axis=0)
    mod = _ada(c_all, w_ada[0], b_ada[0])
    modp = mod[:n_batch].reshape(n_batch * 3, 1, D_MODEL)
    mods = mod[8:8 + n_dec]

    xp = x_prompt.reshape(n_batch * seq, D_MODEL)
    yssm, hfin = _ssm_prompt(x_prompt, modp, wut, lhs, bpow, cpow, coef, eye)
    yp, ktail, vtail = _main_prompt(xp, yssm, modp, wmain, wvt, biast, sink_rows, eye,
                                    wglu, bglu, wbs, wba, wout, lng, lnb, n_batch, seq)
    yp = yp.reshape(n_batch, seq, D_MODEL)
    p_hr = hfin[:, :, 0, :SSM_STATE][None]
    p_hi = hfin[:, :, 0, SSM_STATE:][None]
    p_k = ktail.reshape(1, n_batch, WINDOW, N_KV_HEADS, HEAD_DIM)
    p_v = vtail.reshape(1, n_batch, WINDOW, N_KV_HEADS, HEAD_DIM)

    xs = x_sample.reshape(n_dec, D_MODEL)
    proj_s = _proj(xs, mods, mods,
                   pl.BlockSpec((n_dec, D_MODEL), lambda i: (0, 0)),
                   pl.BlockSpec((n_dec, D_MODEL), lambda i: (0, 1)),
                   w_in_b, n_dec, D_IN)
    ns = SSM_GROUPS * SSM_STATE
    ys_s, s_hr, s_hi = _sample_ssm(proj_s, state_ssm_re[0].reshape(n_dec, ns), state_ssm_im[0].reshape(n_dec, ns),
                                   ar_row, ai_row, bre, bim, cblk, ssm_d[0].reshape(1, -1).astype(F32))
    o_s, nk, nv = _sample_attn(proj_s, cache_swa_k[0].reshape(n_dec, WINDOW, KV_WIDTH),
                               cache_swa_v[0].reshape(n_dec, WINDOW, KV_WIDTH), bias_s, sink_row, seg, segt)
    ysmp = _final_sample(xs, ys_s, o_s, proj_s, mods, wglu, bglu, wbs, wba, wout, lng, lnb)

    return (yp, ysmp.reshape(n_dec, 1, D_MODEL),
            p_hr, p_hi, p_k, p_v,
            s_hr.reshape(1, n_dec, SSM_GROUPS, SSM_STATE), s_hi.reshape(1, n_dec, SSM_GROUPS, SSM_STATE),
            nk.reshape(1, n_dec, WINDOW, N_KV_HEADS, HEAD_DIM), nv.reshape(1, n_dec, WINDOW, N_KV_HEADS, HEAD_DIM))
```

```python
import functools
import math

import numpy as np
import jax
import jax.numpy as jnp
from jax import lax
from jax.experimental import pallas as pl
from jax.experimental.pallas import tpu as pltpu

F32 = jnp.float32
BF16 = jnp.bfloat16

D_MODEL = 1024
SSM_GROUPS = 64
SSM_GROUP = 16
SSM_STATE = 64
N_HEADS = 16
HEAD_DIM = 64
N_KV_HEADS = 4
KV_GROUP = 4
KV_WIDTH = N_KV_HEADS * HEAD_DIM
WINDOW = 128
N_BUCKETS = 32
NEG_INF = -1e30
LN_EPS = 1e-5
DEPTH = 1
DEEPNORM_ALPHA = (2 * DEPTH) ** 0.25
D_IN = 6656
CHUNK = 16
CW = CHUNK * SSM_GROUP
SW = 2 * SSM_STATE

COL_U, COL_ZS, COL_Q, COL_ZA, COL_GS, COL_GA = 0, 1, 2, 3, 4, 5
COL_K, COL_V = 24, 25

VMEM_LIMIT = 56 * 1024 * 1024


def _head_perm():
    return np.arange(N_HEADS * HEAD_DIM).reshape(N_KV_HEADS, KV_GROUP, HEAD_DIM).transpose(1, 0, 2).reshape(-1)


def _sigmoid(x):
    return 1.0 / (1.0 + jnp.exp(-x))


def _silu(x):
    return x * _sigmoid(x)


def _gelu_tanh(x):
    return 0.5 * x * (1.0 + jnp.tanh(math.sqrt(2.0 / math.pi) * (x + 0.044715 * (x * x * x))))


def _dot(a, b):
    return jnp.dot(a, b, preferred_element_type=F32)


def _dot_nt(a, b):
    return lax.dot_general(a, b, (((1,), (1,)), ((), ())), preferred_element_type=F32)


def _ada_body(c_ref, w_ref, b_ref, o_ref):
    c = c_ref[...]
    sc = _silu(c).astype(BF16)
    o_ref[...] = _dot(sc, w_ref[...].astype(BF16)) + b_ref[...]


def _ada(c_all, w_ada, b_ada):
    rows = c_all.shape[0]
    bn = 512
    return pl.pallas_call(
        _ada_body,
        grid=(3 * D_MODEL // bn,),
        in_specs=[pl.BlockSpec((rows, D_MODEL), lambda j: (0, 0)),
                  pl.BlockSpec((D_MODEL, bn), lambda j: (0, j)),
                  pl.BlockSpec((1, bn), lambda j: (0, j))],
        out_specs=pl.BlockSpec((rows, bn), lambda j: (0, j)),
        out_shape=jax.ShapeDtypeStruct((rows, 3 * D_MODEL), F32),
        name="ada",
    )(c_all, w_ada, b_ada.reshape(1, -1))


PROJ_CB = 512


def _proj_body(x_ref, shift_ref, scale_ref, w_ref, o_ref, *, n_out):
    h = x_ref[...] * (1.0 + scale_ref[...]) + shift_ref[...]
    hb = h.astype(BF16)
    for j in range(n_out // PROJ_CB):
        sl = slice(j * PROJ_CB, (j + 1) * PROJ_CB)
        o_ref[:, sl] = _dot(hb, w_ref[:, sl]).astype(BF16)


def _proj(x2d, shift_arr, scale_arr, shift_spec, scale_spec, w_in_b, tm, n_out):
    rows = x2d.shape[0]
    return pl.pallas_call(
        functools.partial(_proj_body, n_out=n_out),
        grid=(rows // tm,),
        in_specs=[pl.BlockSpec((tm, D_MODEL), lambda i: (i, 0)),
                  shift_spec, scale_spec,
                  pl.BlockSpec((D_MODEL, n_out), lambda i: (0, 0), pipeline_mode=pl.Buffered(1))],
        out_specs=pl.BlockSpec((tm, n_out), lambda i: (i, 0)),
        out_shape=jax.ShapeDtypeStruct((rows, n_out), BF16),
        compiler_params=pltpu.CompilerParams(vmem_limit_bytes=VMEM_LIMIT),
        name="proj",
    )(x2d, shift_arr, scale_arr, w_in_b)


NKS = 128
GC = 16


def _cmul(c1, c2, x):
    return c1 * x + c2 * pltpu.roll(x, SSM_STATE, x.ndim - 1)


SSM_TOK = NKS * CHUNK
SUB = 8
NPH = CHUNK // SUB
GROUP_UNROLL = 4


def _dot_tn(a, b):
    return lax.dot_general(a, b, (((0,), (0,)), ((), ())), preferred_element_type=F32)


def _ssm_body(x_ref, shift_ref, scale_ref, wut_ref, lhs_ref, bpow_ref, cpow_ref, coef_ref, eye_ref,
              y_ref, hfin_ref, r_scr, yt_scr, s_scr, carry_scr):
    i = pl.program_id(1)
    j = pl.program_id(2)

    @pl.when((i == 0) & (j == 0))
    def _():
        carry_scr[...] = jnp.zeros_like(carry_scr)

    @pl.when(j < NPH)
    def _():
        for sp in range(SUB // 2):
            h = jnp.concatenate([x_ref[:, 2 * sp, :], x_ref[:, 2 * sp + 1, :]], axis=0)
            hb = (h * (1.0 + scale_ref[...]) + shift_ref[...]).astype(BF16)
            ut = _dot_nt(wut_ref[...], hb).astype(BF16)
            ut = ut.reshape(SSM_GROUPS, SSM_GROUP, 2 * NKS)
            pair = jnp.concatenate([ut[:, :, :NKS], ut[:, :, NKS:]], axis=1)
            r0 = pl.multiple_of(j * SUB * SSM_GROUP + sp * 2 * SSM_GROUP, 2 * SSM_GROUP)
            r_scr[:, pl.ds(r0, 2 * SSM_GROUP), :] = pair

    @pl.when(j == NPH - 1)
    def _():
        def local(g, c):
            rg = r_scr[g]
            yt_scr[g] = _dot(lhs_ref[g], rg)
            s_scr[g] = _dot_tn(rg, bpow_ref[g])
            return c

        lax.fori_loop(0, SSM_GROUPS, local, 0, unroll=GROUP_UNROLL)

        row = lax.broadcasted_iota(jnp.int32, (GC, 8, SW), 1)
        for gc in range(SSM_GROUPS // GC):
            gs = slice(gc * GC, (gc + 1) * GC)
            coef = coef_ref[gs]
            for r in range(NKS // 8):
                rs = slice(r * 8, (r + 1) * 8)
                blk = s_scr[gs, rs, :]
                for k, d in enumerate((1, 2, 4)):
                    sh = jnp.where(row >= d, pltpu.roll(blk, d, 1), 0.0)
                    blk = blk + _cmul(coef[:, 2 * k:2 * k + 1], coef[:, 2 * k + 1:2 * k + 2], sh)
                s_scr[gs, rs, :] = blk
        for gc in range(SSM_GROUPS // GC):
            gs = slice(gc * GC, (gc + 1) * GC)
            coef = coef_ref[gs]
            c1, c2 = coef[:, 8:16], coef[:, 16:24]
            c1e = jnp.broadcast_to(c1[:, 7:8, :], (GC, 8, SW))
            c2e = jnp.broadcast_to(c2[:, 7:8, :], (GC, 8, SW))
            h = carry_scr[gs]
            hs = pltpu.roll(h, SSM_STATE, 2)
            for r in range(NKS // 8):
                rs = slice(r * 8, (r + 1) * 8)
                blk = s_scr[gs, rs, :]
                full = blk + c1 * h + c2 * hs
                s_scr[gs, rs, :] = jnp.where(row == 0, h, pltpu.roll(full, 1, 1))
                end = jnp.broadcast_to(blk[:, 7:8, :], (GC, 8, SW))
                ends = pltpu.roll(end, SSM_STATE, 2)
                h, hs = end + c1e * h + c2e * hs, ends + c1e * hs - c2e * h
            carry_scr[gs] = h
        hfin_ref[...] = carry_scr[...]

        def carried(g, c):
            yt_scr[g] += _dot_nt(cpow_ref[g], s_scr[g].astype(BF16))
            return c

        lax.fori_loop(0, SSM_GROUPS, carried, 0, unroll=GROUP_UNROLL)

    @pl.when(j >= NPH)
    def _():
        for sp in range(SUB):
            t0 = pl.multiple_of((j - NPH) * SUB * SSM_GROUP + sp * SSM_GROUP, SSM_GROUP)
            yt = yt_scr[:, pl.ds(t0, SSM_GROUP), :].reshape(D_MODEL, NKS)
            hi = yt.astype(BF16)
            lo = (yt - hi.astype(F32)).astype(BF16)
            y_ref[:, sp, :] = _dot_nt(eye_ref[...], hi) + _dot_nt(eye_ref[...], lo)


def _ssm_prompt(x_prompt, modp, wut, lhs, bpow, cpow, coef, eye):
    n_batch, seq, _ = x_prompt.shape
    steps = seq // SSM_TOK
    xv = x_prompt.reshape(n_batch * seq // CHUNK, CHUNK, D_MODEL)
    mod_spec = lambda k: pl.BlockSpec((None, 1, D_MODEL), lambda b, i, j, k=k: (b * 3 + k, 0, 0))
    consts = (wut, lhs, bpow, cpow, coef, eye)
    y, hfin = pl.pallas_call(
        _ssm_body,
        grid=(n_batch, steps, 2 * NPH),
        in_specs=[pl.BlockSpec((NKS, SUB, D_MODEL), lambda b, i, j: (b * steps + i, jnp.minimum(j, NPH - 1), 0),
                               pipeline_mode=pl.Buffered(1)),
                  mod_spec(0), mod_spec(1)] + [_const_spec(c.shape) for c in consts],
        out_specs=[pl.BlockSpec((NKS, SUB, D_MODEL), lambda b, i, j: (b * steps + i, jnp.maximum(j - NPH, 0), 0)),
                   pl.BlockSpec((None, SSM_GROUPS, 8, SW), lambda b, i, j: (b, 0, 0, 0))],
        out_shape=[jax.ShapeDtypeStruct((n_batch * seq // CHUNK, CHUNK, D_MODEL), F32),
                   jax.ShapeDtypeStruct((n_batch, SSM_GROUPS, 8, SW), F32)],
        scratch_shapes=[pltpu.VMEM((SSM_GROUPS, CW, NKS), BF16), pltpu.VMEM((SSM_GROUPS, CW, NKS), F32),
                        pltpu.VMEM((SSM_GROUPS, NKS, SW), F32), pltpu.VMEM((SSM_GROUPS, 8, SW), F32)],
        compiler_params=pltpu.CompilerParams(dimension_semantics=("arbitrary", "arbitrary", "arbitrary"),
                                             vmem_limit_bytes=VMEM_LIMIT),
        name="ssm_prompt",
    )(xv, modp, modp, *consts)
    return y.reshape(n_batch * seq, D_MODEL), hfin


def _post(ys, zs, o_a, za, gs, ga, x, gate, wglu_ref, bglu_ref, wbs_ref, wba_ref, wout_ref, lng_ref, lnb_ref):
    ys = _gelu_tanh(ys)
    glu = ys * _sigmoid(_dot(ys.astype(BF16), wglu_ref[...]) + bglu_ref[...])
    t = (glu * _silu(zs.astype(F32))).astype(BF16)
    b_s = _dot(t, wbs_ref[...])
    oz = (o_a * _silu(za.astype(F32))).astype(BF16)
    b_a = _dot(oz, wba_ref[...])
    m = _sigmoid(gs.astype(F32)) * b_s + _sigmoid(ga.astype(F32)) * b_a
    out = _dot(m.astype(BF16), wout_ref[...])
    r = DEEPNORM_ALPHA * x + gate * out
    mu = jnp.mean(r, axis=-1, keepdims=True)
    rc = r - mu
    var = jnp.mean(rc * rc, axis=-1, keepdims=True)
    return rc * lax.rsqrt(var + LN_EPS) * lng_ref[...] + lnb_ref[...]


TB = 256
NQB = TB // WINDOW
W_MAIN = 5 * D_MODEL + KV_WIDTH
ONES_ROWS = 16


def _const_spec(shape):
    nd = len(shape)
    return pl.BlockSpec(shape, lambda *_: (0,) * nd, pipeline_mode=pl.Buffered(1))


def _main_body(x_ref, yssm_ref, shift_ref, scale_ref, gate_ref, wmain_ref, wvt_ref, biast_ref, sink_ref, eye_ref,
               wglu_ref, bglu_ref, wbs_ref, wba_ref, wout_ref, lng_ref, lnb_ref,
               y_ref, ktail_ref, vtail_ref,
               zs_s, q_s, za_s, gs_s, ga_s, k_s, vt_s, ot_s, o_s):
    i = pl.program_id(1)
    x = x_ref[...]
    hb = (x * (1.0 + scale_ref[...]) + shift_ref[...]).astype(BF16)

    for dst, c0 in ((zs_s, 0), (q_s, D_MODEL), (za_s, 2 * D_MODEL), (gs_s, 3 * D_MODEL), (ga_s, 4 * D_MODEL)):
        for c in range(2):
            dst[:, c * 512:(c + 1) * 512] = _dot(hb, wmain_ref[:, c0 + c * 512:c0 + (c + 1) * 512]).astype(BF16)

    @pl.when(i == 0)
    def _():
        k_s[0:WINDOW, :] = jnp.zeros((WINDOW, KV_WIDTH), BF16)
        vt_s[:, 0:WINDOW] = jnp.zeros((KV_WIDTH, WINDOW), BF16)

    k_s[WINDOW:WINDOW + TB, :] = _dot(hb, wmain_ref[:, 5 * D_MODEL:W_MAIN]).astype(BF16)
    vt_s[:, WINDOW:WINDOW + TB] = _dot_nt(wvt_ref[...], hb).astype(BF16)

    first = jnp.where(i == 0, 1, 0)
    lane = lax.broadcasted_iota(jnp.int32, (2 * WINDOW, 128), 1)
    ones = jnp.ones((ONES_ROWS, 2 * WINDOW), BF16)
    for blk in range(NQB):
        rows = slice(blk * WINDOW, (blk + 1) * WINDOW)
        kcat = k_s[blk * WINDOW:blk * WINDOW + 2 * WINDOW, :]
        vtc = vt_s[:, blk * WINDOW:blk * WINDOW + 2 * WINDOW]
        sel = first if blk == 0 else 0
        for kh in range(N_KV_HEADS):
            pair, odd = kh // 2, kh % 2
            kcm = jnp.where((lane >= HEAD_DIM) == bool(odd), kcat[:, pair * 128:(pair + 1) * 128], jnp.zeros((), BF16))
            qp = jnp.concatenate(
                [q_s[rows, j * KV_WIDTH + pair * 128:j * KV_WIDTH + (pair + 1) * 128] for j in range(KV_GROUP)],
                axis=0)
            st = _dot_nt(kcm, qp) + biast_ref[sel, kh]
            sink = sink_ref[kh]
            m = jnp.maximum(jnp.max(st, axis=0, keepdims=True), sink)
            e = jnp.exp(st - m).astype(BF16)
            lhs = jnp.concatenate([vtc[kh * HEAD_DIM:(kh + 1) * HEAD_DIM, :], ones], axis=0)
            r = _dot(lhs, e)
            den = r[HEAD_DIM:HEAD_DIM + 1, :] + jnp.exp(sink - m)
            ot = r[:HEAD_DIM, :] / den
            for j in range(KV_GROUP):
                ot_s[blk, j, kh * HEAD_DIM:(kh + 1) * HEAD_DIM, :] = ot[:, j * WINDOW:(j + 1) * WINDOW]
        for j in range(KV_GROUP):
            o_s[rows, j * KV_WIDTH:(j + 1) * KV_WIDTH] = _dot_nt(eye_ref[...], ot_s[blk, j].astype(BF16))

    @pl.when(i == pl.num_programs(1) - 1)
    def _():
        ktail_ref[...] = k_s[TB:TB + WINDOW, :].astype(F32)
        vtail_ref[...] = _dot_nt(eye_ref[...], vt_s[:, TB:TB + WINDOW])

    k_s[0:WINDOW, :] = k_s[TB:TB + WINDOW, :]
    vt_s[:, 0:WINDOW] = vt_s[:, TB:TB + WINDOW]

    y_ref[...] = _post(yssm_ref[...], zs_s[...], o_s[...], za_s[...], gs_s[...], ga_s[...],
                       x, gate_ref[...], wglu_ref, bglu_ref, wbs_ref, wba_ref, wout_ref, lng_ref, lnb_ref)


def _main_prompt(x2d, yssm, modp, wmain, wvt, biast, sink_rows, eye, wglu, bglu, wbs, wba, wout, lng, lnb,
                 n_batch, seq):
    steps = seq // TB
    row_spec = pl.BlockSpec((TB, D_MODEL), lambda b, i: (b * steps + i, 0))
    mod_spec = lambda j: pl.BlockSpec((None, 1, D_MODEL), lambda b, i, j=j: (b * 3 + j, 0, 0))
    tail_spec = pl.BlockSpec((None, WINDOW, KV_WIDTH), lambda b, i: (b, 0, 0))
    consts = (wmain, wvt, biast, sink_rows, eye, wglu, bglu, wbs, wba, wout, lng, lnb)
    return pl.pallas_call(
        _main_body,
        grid=(n_batch, steps),
        in_specs=[row_spec, row_spec, mod_spec(0), mod_spec(1), mod_spec(2)] + [_const_spec(c.shape) for c in consts],
        out_specs=[row_spec, tail_spec, tail_spec],
        out_shape=[jax.ShapeDtypeStruct((n_batch * seq, D_MODEL), F32),
                   jax.ShapeDtypeStruct((n_batch, WINDOW, KV_WIDTH), F32),
                   jax.ShapeDtypeStruct((n_batch, WINDOW, KV_WIDTH), F32)],
        scratch_shapes=[pltpu.VMEM((TB, D_MODEL), BF16)] * 5
        + [pltpu.VMEM((TB + WINDOW, KV_WIDTH), BF16), pltpu.VMEM((KV_WIDTH, TB + WINDOW), BF16),
           pltpu.VMEM((NQB, KV_GROUP, KV_WIDTH, WINDOW), F32), pltpu.VMEM((TB, D_MODEL), F32)],
        compiler_params=pltpu.CompilerParams(dimension_semantics=("arbitrary", "arbitrary"),
                                             vmem_limit_bytes=VMEM_LIMIT),
        name="main_prompt",
    )(x2d, yssm, modp, modp, modp, *consts)


GT = 8


def _sample_ssm_body(u_ref, h0r_ref, h0i_ref, ar_ref, ai_ref, bre_ref, bim_ref, cblk_ref, d_ref,
                     y_ref, hr_ref, hi_ref):
    ub = u_ref[...]
    ys = []
    for a in range(GT):
        ua = ub[:, a * 128:(a + 1) * 128]
        st = slice(a * 512, (a + 1) * 512)
        xr = _dot(ua, bre_ref[a])
        xi = _dot(ua, bim_ref[a])
        h0r = h0r_ref[:, st]
        h0i = h0i_ref[:, st]
        ar = ar_ref[:, st]
        ai = ai_ref[:, st]
        hr = ar * h0r - ai * h0i + xr
        hi = ar * h0i + ai * h0r + xi
        hr_ref[:, st] = hr
        hi_ref[:, st] = hi
        hcat = jnp.concatenate([hr, hi], axis=1).astype(BF16)
        ys.append(_dot(hcat, cblk_ref[a]))
    y_ref[...] = jnp.concatenate(ys, axis=1) + d_ref[...] * ub.astype(F32)


def _sample_ssm(proj_s, h0r, h0i, ar, ai, bre, bim, cblk, d_row):
    rows = proj_s.shape[0]
    ns = SSM_GROUPS * SSM_STATE
    full = lambda shape: pl.BlockSpec(shape, lambda i: (0,) * len(shape))
    return pl.pallas_call(
        _sample_ssm_body,
        grid=(1,),
        in_specs=[pl.BlockSpec((rows, D_MODEL), lambda i: (0, COL_U)),
                  full((rows, ns)), full((rows, ns)), full((1, ns)), full((1, ns)),
                  full(bre.shape), full(bim.shape), full(cblk.shape), full((1, D_MODEL))],
        out_specs=[full((rows, D_MODEL)), full((rows, ns)), full((rows, ns))],
        out_shape=[jax.ShapeDtypeStruct((rows, D_MODEL), F32),
                   jax.ShapeDtypeStruct((rows, ns), F32),
                   jax.ShapeDtypeStruct((rows, ns), F32)],
        compiler_params=pltpu.CompilerParams(vmem_limit_bytes=VMEM_LIMIT),
        name="sample_ssm",
    )(proj_s, h0r, h0i, ar, ai, bre, bim, cblk, d_row)


SB = 16
KROWS = WINDOW + 8


def _sample_attn_body(q_ref, kn_ref, vn_ref, ck_ref, cv_ref, bias_ref, sink_ref, seg_ref, segt_ref,
                      o_ref, nk_ref, nv_ref):
    qf = q_ref[...].astype(F32)
    knf = kn_ref[...].astype(F32)
    vnf = vn_ref[...].astype(F32)
    rowk = lax.broadcasted_iota(jnp.int32, (KROWS, 128), 0)
    roww = lax.broadcasted_iota(jnp.int32, (WINDOW, KV_WIDTH), 0)
    sink = sink_ref[...]
    for b in range(SB):
        kb = ck_ref[b]
        vb = cv_ref[b]
        qrow = qf[b:b + 1]
        kn = jnp.broadcast_to(knf[b:b + 1], (8, KV_WIDTH))
        vn = jnp.broadcast_to(vnf[b:b + 1], (8, KV_WIDTH))
        kall = jnp.concatenate([kb, kn], axis=0)
        vall = jnp.concatenate([vb, vn], axis=0)
        prod = jnp.concatenate([kall * qrow[:, j * KV_WIDTH:(j + 1) * KV_WIDTH] for j in range(KV_GROUP)], axis=1)
        s = _dot(prod.astype(BF16), seg_ref[...]) + bias_ref[...]
        s = jnp.where(rowk <= WINDOW, s, NEG_INF)
        m = jnp.maximum(jnp.max(s, axis=0, keepdims=True), sink)
        e = jnp.exp(s - m)
        den = jnp.sum(e, axis=0, keepdims=True) + jnp.exp(sink - m)
        p = e / den
        pexp = _dot(p.astype(BF16), segt_ref[...])
        v4 = jnp.concatenate([vall] * KV_GROUP, axis=1)
        o_ref[b:b + 1, :] = jnp.sum(pexp * v4, axis=0, keepdims=True)
        last = roww == WINDOW - 1
        nk_ref[b] = jnp.where(last, jnp.broadcast_to(knf[b:b + 1], (WINDOW, KV_WIDTH)), pltpu.roll(kb, WINDOW - 1, 0))
        nv_ref[b] = jnp.where(last, jnp.broadcast_to(vnf[b:b + 1], (WINDOW, KV_WIDTH)), pltpu.roll(vb, WINDOW - 1, 0))


def _sample_attn(proj_s, ck, cv, bias_s, sink_row, seg, segt):
    rows = proj_s.shape[0]
    cache_spec = pl.BlockSpec((SB, WINDOW, KV_WIDTH), lambda i: (i, 0, 0))
    return pl.pallas_call(
        _sample_attn_body,
        grid=(rows // SB,),
        in_specs=[pl.BlockSpec((SB, D_MODEL), lambda i: (i, COL_Q)),
                  pl.BlockSpec((SB, KV_WIDTH), lambda i: (i, COL_K)),
                  pl.BlockSpec((SB, KV_WIDTH), lambda i: (i, COL_V)),
                  cache_spec, cache_spec,
                  pl.BlockSpec(bias_s.shape, lambda i: (0, 0)),
                  pl.BlockSpec(sink_row.shape, lambda i: (0, 0)),
                  pl.BlockSpec(seg.shape, lambda i: (0, 0)),
                  pl.BlockSpec(segt.shape, lambda i: (0, 0))],
        out_specs=[pl.BlockSpec((SB, D_MODEL), lambda i: (i, 0)), cache_spec, cache_spec],
        out_shape=[jax.ShapeDtypeStruct((rows, D_MODEL), F32),
                   jax.ShapeDtypeStruct(ck.shape, F32),
                   jax.ShapeDtypeStruct(cv.shape, F32)],
        compiler_params=pltpu.CompilerParams(vmem_limit_bytes=VMEM_LIMIT),
        name="sample_attn",
    )(proj_s, proj_s, proj_s, ck, cv, bias_s, sink_row, seg, segt)


def _sample_final_body(x_ref, yssm_ref, zs_ref, o_ref, za_ref, gs_ref, ga_ref, gate_ref,
                       wglu_ref, bglu_ref, wbs_ref, wba_ref, wout_ref, lng_ref, lnb_ref, y_ref):
    y_ref[...] = _post(yssm_ref[...], zs_ref[...], o_ref[...], za_ref[...], gs_ref[...], ga_ref[...],
                       x_ref[...], gate_ref[...], wglu_ref, bglu_ref, wbs_ref, wba_ref, wout_ref,
                       lng_ref, lnb_ref)


def _final_sample(x2d, yssm, o_a, proj_s, mods, wglu, bglu, wbs, wba, wout, lng, lnb):
    rows = x2d.shape[0]
    full = lambda shape: pl.BlockSpec(shape, lambda i: (0,) * len(shape))
    pcol = lambda c: pl.BlockSpec((rows, D_MODEL), lambda i, c=c: (0, c))
    return pl.pallas_call(
        _sample_final_body,
        grid=(1,),
        in_specs=[full((rows, D_MODEL)), full((rows, D_MODEL)), pcol(COL_ZS), full((rows, D_MODEL)),
                  pcol(COL_ZA), pcol(COL_GS), pcol(COL_GA),
                  pl.BlockSpec((rows, D_MODEL), lambda i: (0, 2)),
                  full(wglu.shape), full(bglu.shape), full(wbs.shape), full(wba.shape), full(wout.shape),
                  full(lng.shape), full(lnb.shape)],
        out_specs=full((rows, D_MODEL)),
        out_shape=jax.ShapeDtypeStruct((rows, D_MODEL), F32),
        compiler_params=pltpu.CompilerParams(vmem_limit_bytes=VMEM_LIMIT),
        name="final_sample",
    )(x2d, yssm, proj_s, o_a, proj_s, proj_s, proj_s, mods, wglu, bglu, wbs, wba, wout, lng, lnb)


def _bucket_table():
    max_exact = N_BUCKETS // 2
    dist = np.arange(WINDOW + 1)
    df = np.maximum(dist, 1).astype(np.float32)
    large = max_exact + (np.log(df / np.float32(max_exact)) / np.float32(math.log(WINDOW / max_exact))
                         * np.float32(N_BUCKETS - max_exact)).astype(np.int32)
    large = np.minimum(large, N_BUCKETS - 1)
    return np.where(dist < max_exact, dist, large)


def _ssm_params(lam_re, lam_im, log_delta, b_re, b_im, c_re, c_im, d_skip):
    hp = lax.Precision.HIGHEST
    g, p, c = SSM_GROUPS, SSM_STATE, SSM_GROUP
    lr, li = lam_re.astype(F32), lam_im.astype(F32)
    dt = jnp.exp(log_delta.astype(F32))[:, None]
    mag = jnp.exp(lr * dt)
    ar, ai = mag * jnp.cos(li * dt), mag * jnp.sin(li * dt)
    den = lr * lr + li * li
    nr = ar - 1.0
    cr_ = (nr * lr + ai * li) / den
    ci_ = (ai * lr - nr * li) / den
    bbr = cr_[..., None] * b_re - ci_[..., None] * b_im
    bbi = cr_[..., None] * b_im + ci_[..., None] * b_re

    def powers(taus):
        t = jnp.asarray(taus, F32)[:, None, None]
        m = jnp.exp(lr * dt * t)
        return m * jnp.cos(li * dt * t), m * jnp.sin(li * dt * t)

    pr, pi = powers(np.arange(CHUNK + 1))
    wr = pr[:CHUNK, :, :, None] * bbr - pi[:CHUNK, :, :, None] * bbi
    wi = pr[:CHUNK, :, :, None] * bbi + pi[:CHUNK, :, :, None] * bbr
    kt = (jnp.einsum('gcp,tgpd->tgcd', c_re, wr, precision=hp)
          - jnp.einsum('gcp,tgpd->tgcd', c_im, wi, precision=hp))
    tt = np.arange(CHUNK)[:, None, None]
    ss = np.arange(CHUNK)[None, :, None]
    lag_onehot = jnp.asarray((tt - ss) == np.arange(CHUNK)[None, None, :], F32)
    m5 = jnp.einsum('tsk,kgcd->tsgcd', lag_onehot, kt, precision=hp)
    m5 = m5 + (jnp.eye(CHUNK, dtype=F32)[:, :, None, None, None]
               * (d_skip.reshape(g, c)[None, None, :, :, None] * jnp.eye(c, dtype=F32)[None, None, None]))
    lhs = jnp.transpose(m5, (2, 0, 3, 1, 4)).reshape(g, CW, CW).astype(BF16)
    rev = np.arange(CHUNK)[::-1]
    bpr = jnp.transpose(wr[rev], (1, 0, 3, 2)).reshape(g, CW, p)
    bpi = jnp.transpose(wi[rev], (1, 0, 3, 2)).reshape(g, CW, p)
    bpow = jnp.concatenate([bpr, bpi], axis=2).astype(BF16)
    p1r, p1i = pr[1:], pi[1:]
    cpr = c_re[None] * p1r[:, :, None, :] - c_im[None] * p1i[:, :, None, :]
    cpi = -(c_re[None] * p1i[:, :, None, :] + c_im[None] * p1r[:, :, None, :])
    cpow = jnp.concatenate([jnp.transpose(cpr, (1, 0, 2, 3)).reshape(g, CW, p),
                            jnp.transpose(cpi, (1, 0, 2, 3)).reshape(g, CW, p)], axis=2).astype(BF16)
    qr, qi = powers(CHUNK * np.arange(1, 9))
    c1 = jnp.concatenate([qr, qr], axis=2)
    c2 = jnp.concatenate([-qi, qi], axis=2)
    log_rows = jnp.stack([c1[0], c2[0], c1[1], c2[1], c1[3], c2[3], c1[7], c2[7]], axis=1)
    coef = jnp.concatenate([log_rows, jnp.transpose(c1, (1, 0, 2)), jnp.transpose(c2, (1, 0, 2))], axis=1)

    eye8 = jnp.eye(GT, dtype=F32)
    bre = jnp.einsum('agpc,gh->agchp', bbr.reshape(GT, GT, p, c), eye8).reshape(GT, GT * c, GT * p).astype(BF16)
    bim = jnp.einsum('agpc,gh->agchp', bbi.reshape(GT, GT, p, c), eye8).reshape(GT, GT * c, GT * p).astype(BF16)
    cre = jnp.einsum('agcp,gh->agphc', c_re.reshape(GT, GT, c, p), eye8).reshape(GT, GT * p, GT * c)
    cim = jnp.einsum('agcp,gh->agphc', c_im.reshape(GT, GT, c, p), eye8).reshape(GT, GT * p, GT * c)
    cblk = jnp.concatenate([cre, -cim], axis=1).astype(BF16)
    return lhs, bpow, cpow, coef, ar.reshape(1, -1), ai.reshape(1, -1), bre, bim, cblk


def _attn_params(rel_bias, sinks):
    hp = lax.Precision.HIGHEST
    rb = rel_bias.astype(F32)
    onehot = jnp.asarray(_bucket_table()[:, None] == np.arange(N_BUCKETS)[None, :], F32)
    tbl = jnp.dot(onehot, rb, precision=hp)
    span = 3 * WINDOW
    w = jnp.concatenate([tbl[::-1], jnp.full((span - WINDOW - 1, N_HEADS), NEG_INF, F32)], axis=0)
    wt = jnp.broadcast_to(jnp.transpose(w)[:, None, :], (N_HEADS, WINDOW, span)).reshape(N_HEADS, WINDOW * span)
    bq = wt[:, :WINDOW * (span - 1)].reshape(N_HEADS, WINDOW, span - 1)[:, :, :2 * WINDOW]
    first = jnp.where(jnp.arange(2 * WINDOW)[None, None, :] >= WINDOW, bq, NEG_INF)
    bm = jnp.stack([bq, first], axis=0)
    biast = jnp.transpose(bm.reshape(2, N_KV_HEADS, KV_GROUP, WINDOW, 2 * WINDOW), (0, 1, 4, 2, 3)) \
        .reshape(2, N_KV_HEADS, 2 * WINDOW, KV_GROUP * WINDOW)
    sink_rows = jnp.repeat(sinks.astype(F32).reshape(N_KV_HEADS, 1, KV_GROUP), WINDOW, axis=2)
    ds = np.clip(WINDOW - np.arange(KROWS), 0, WINDOW)
    bs = jnp.dot(jnp.asarray(ds[:, None] == np.arange(WINDOW + 1)[None, :], F32), tbl, precision=hp)
    bs = jnp.transpose(bs.reshape(KROWS, N_KV_HEADS, KV_GROUP), (0, 2, 1)).reshape(KROWS, N_HEADS)
    bias_s = jnp.pad(bs, ((0, 0), (0, 128 - N_HEADS)))
    sk = jnp.transpose(sinks.astype(F32).reshape(N_KV_HEADS, KV_GROUP), (1, 0)).reshape(1, N_HEADS)
    sink_row = jnp.pad(sk, ((0, 0), (0, 128 - N_HEADS)))
    seg_np = np.zeros((D_MODEL, 128), np.float32)
    for j in range(KV_GROUP):
        for kh in range(N_KV_HEADS):
            r0 = j * KV_WIDTH + kh * HEAD_DIM
            seg_np[r0:r0 + HEAD_DIM, j * N_KV_HEADS + kh] = 1.0
    seg = jnp.asarray(seg_np, BF16)
    segt = jnp.asarray(seg_np.T, BF16)
    return biast, sink_rows, bias_s, sink_row, seg, segt


def kernel(x_prompt, x_sample, c_prompt, c_sample, state_ssm_re, state_ssm_im, cache_swa_k, cache_swa_v,
           w_ada, b_ada, w_in, ssm_lambda_re, ssm_lambda_im, ssm_log_delta, ssm_b_re, ssm_b_im,
           ssm_c_re, ssm_c_im, ssm_d, w_glu, b_glu, attn_sinks, rel_bias, w_branch_s, w_branch_a,
           w_out, ln_g, ln_b):
    assert w_ada.shape[0] == 1, "single-layer trunk"
    n_batch, seq, _ = x_prompt.shape
    n_dec = x_sample.shape[0]
    perm = _head_perm()

    wi = w_in[0]
    w_in_b = jnp.concatenate([wi[:, 0:1024], wi[:, 1024:2048], wi[:, 2048:3072][:, perm] * (HEAD_DIM ** -0.5),
                              wi[:, 3584:4608][:, perm], wi[:, 4608:5632], wi[:, 5632:6656],
                              wi[:, 3072:3328], wi[:, 3328:3584]], axis=1).astype(BF16)
    wmain = w_in_b[:, D_MODEL:D_MODEL + W_MAIN]
    wvt = jnp.transpose(wi[:, 3328:3584]).astype(BF16)
    wut = jnp.transpose(wi[:, 0:D_MODEL]).astype(BF16)
    eye = jnp.eye(WINDOW, dtype=BF16)
    wglu = w_glu[0].astype(BF16)
    bglu = b_glu[0].reshape(1, -1).astype(F32)
    wbs = w_branch_s[0].astype(BF16)
    wba = w_branch_a[0][perm, :].astype(BF16)
    wout = w_out[0].astype(BF16)
    lng = ln_g[0].reshape(1, -1).astype(F32)
    lnb = ln_b[0].reshape(1, -1).astype(F32)
    lhs, bpow, cpow, coef, ar_row, ai_row, bre, bim, cblk = _ssm_params(
        ssm_lambda_re[0], ssm_lambda_im[0], ssm_log_delta[0], ssm_b_re[0], ssm_b_im[0],
        ssm_c_re[0], ssm_c_im[0], ssm_d[0])
    biast, sink_rows, bias_s, sink_row, seg, segt = _attn_params(rel_bias, attn_sinks[0])

    c_all = jnp.concatenate([c_prompt, jnp.zeros((8 - n_batch, D_MODEL), F32), c_sample], axis=0)
    mod = _ada(c_all, w_ada[0], b_ada[0])
    modp = mod[:n_batch].reshape(n_batch * 3, 1, D_MODEL)
    mods = mod[8:8 + n_dec]

    xp = x_prompt.reshape(n_batch * seq, D_MODEL)
    yssm, hfin = _ssm_prompt(x_prompt, modp, wut, lhs, bpow, cpow, coef, eye)
    yp, ktail, vtail = _main_prompt(xp, yssm, modp, wmain, wvt, biast, sink_rows, eye,
                                    wglu, bglu, wbs, wba, wout, lng, lnb, n_batch, seq)
    yp = yp.reshape(n_batch, seq, D_MODEL)
    p_hr = hfin[:, :, 0, :SSM_STATE][None]
    p_hi = hfin[:, :, 0, SSM_STATE:][None]
    p_k = ktail.reshape(1, n_batch, WINDOW, N_KV_HEADS, HEAD_DIM)
    p_v = vtail.reshape(1, n_batch, WINDOW, N_KV_HEADS, HEAD_DIM)

    xs = x_sample.reshape(n_dec, D_MODEL)
    proj_s = _proj(xs, mods, mods,
                   pl.BlockSpec((n_dec, D_MODEL), lambda i: (0, 0)),
                   pl.BlockSpec((n_dec, D_MODEL), lambda i: (0, 1)),
                   w_in_b, n_dec, D_IN)
    ns = SSM_GROUPS * SSM_STATE
    ys_s, s_hr, s_hi = _sample_ssm(proj_s, state_ssm_re[0].reshape(n_dec, ns), state_ssm_im[0].reshape(n_dec, ns),
                                   ar_row, ai_row, bre, bim, cblk, ssm_d[0].reshape(1, -1).astype(F32))
    o_s, nk, nv = _sample_attn(proj_s, cache_swa_k[0].reshape(n_dec, WINDOW, KV_WIDTH),
                               cache_swa_v[0].reshape(n_dec, WINDOW, KV_WIDTH), bias_s, sink_row, seg, segt)
    ysmp = _final_sample(xs, ys_s, o_s, proj_s, mods, wglu, bglu, wbs, wba, wout, lng, lnb)

    return (yp, ysmp.reshape(n_dec, 1, D_MODEL),
            p_hr, p_hi, p_k, p_v,
            s_hr.reshape(1, n_dec, SSM_GROUPS, SSM_STATE), s_hi.reshape(1, n_dec, SSM_GROUPS, SSM_STATE),
            nk.reshape(1, n_dec, WINDOW, N_KV_HEADS, HEAD_DIM), nv.reshape(1, n_dec, WINDOW, N_KV_HEADS, HEAD_DIM))
```

```python
import functools
import math

import numpy as np
import jax
import jax.numpy as jnp
from jax import lax
from jax.experimental import pallas as pl
from jax.experimental.pallas import tpu as pltpu

F32 = jnp.float32
BF16 = jnp.bfloat16

D_MODEL = 1024
SSM_GROUPS = 64
SSM_GROUP = 16
SSM_STATE = 64
N_HEADS = 16
HEAD_DIM = 64
N_KV_HEADS = 4
KV_GROUP = 4
KV_WIDTH = N_KV_HEADS * HEAD_DIM
WINDOW = 128
N_BUCKETS = 32
NEG_INF = -1e30
LN_EPS = 1e-5
DEPTH = 1
DEEPNORM_ALPHA = (2 * DEPTH) ** 0.25
D_IN = 6656
CHUNK = 16
CW = CHUNK * SSM_GROUP
SW = 2 * SSM_STATE

COL_U, COL_ZS, COL_Q, COL_ZA, COL_GS, COL_GA = 0, 1, 2, 3, 4, 5
COL_K, COL_V = 24, 25

VMEM_LIMIT = 56 * 1024 * 1024


def _head_perm():
    return np.arange(N_HEADS * HEAD_DIM).reshape(N_KV_HEADS, KV_GROUP, HEAD_DIM).transpose(1, 0, 2).reshape(-1)


def _sigmoid(x):
    return 0.5 * jnp.tanh(0.5 * x) + 0.5


def _silu(x):
    return x * _sigmoid(x)


def _gelu_tanh(x):
    c = math.sqrt(2.0 / math.pi)
    hx = 0.5 * x
    return hx + hx * jnp.tanh(x * (c + (c * 0.044715) * (x * x)))


def _dot(a, b):
    return jnp.dot(a, b, preferred_element_type=F32)


def _dot_nt(a, b):
    return lax.dot_general(a, b, (((1,), (1,)), ((), ())), preferred_element_type=F32)


def _ada_body(c_ref, w_ref, b_ref, o_ref):
    c = c_ref[...]
    sc = _silu(c).astype(BF16)
    o_ref[...] = _dot(sc, w_ref[...].astype(BF16)) + b_ref[...]


def _ada(c_all, w_ada, b_ada):
    rows = c_all.shape[0]
    bn = 512
    return pl.pallas_call(
        _ada_body,
        grid=(3 * D_MODEL // bn,),
        in_specs=[pl.BlockSpec((rows, D_MODEL), lambda j: (0, 0)),
                  pl.BlockSpec((D_MODEL, bn), lambda j: (0, j)),
                  pl.BlockSpec((1, bn), lambda j: (0, j))],
        out_specs=pl.BlockSpec((rows, bn), lambda j: (0, j)),
        out_shape=jax.ShapeDtypeStruct((rows, 3 * D_MODEL), F32),
        name="ada",
    )(c_all, w_ada, b_ada.reshape(1, -1))


PROJ_CB = 512


def _proj_body(x_ref, shift_ref, scale_ref, w_ref, o_ref, *, n_out):
    h = x_ref[...] * (1.0 + scale_ref[...]) + shift_ref[...]
    hb = h.astype(BF16)
    for j in range(n_out // PROJ_CB):
        sl = slice(j * PROJ_CB, (j + 1) * PROJ_CB)
        o_ref[:, sl] = _dot(hb, w_ref[:, sl]).astype(BF16)


def _proj(x2d, shift_arr, scale_arr, shift_spec, scale_spec, w_in_b, tm, n_out):
    rows = x2d.shape[0]
    return pl.pallas_call(
        functools.partial(_proj_body, n_out=n_out),
        grid=(rows // tm,),
        in_specs=[pl.BlockSpec((tm, D_MODEL), lambda i: (i, 0)),
                  shift_spec, scale_spec,
                  pl.BlockSpec((D_MODEL, n_out), lambda i: (0, 0), pipeline_mode=pl.Buffered(1))],
        out_specs=pl.BlockSpec((tm, n_out), lambda i: (i, 0)),
        out_shape=jax.ShapeDtypeStruct((rows, n_out), BF16),
        compiler_params=pltpu.CompilerParams(vmem_limit_bytes=VMEM_LIMIT),
        name="proj",
    )(x2d, shift_arr, scale_arr, w_in_b)


NKS = 128
GC = 16


def _cmul(c1, c2, x):
    return c1 * x + c2 * pltpu.roll(x, SSM_STATE, x.ndim - 1)


SSM_TOK = NKS * CHUNK
SUB = 8
NPH = CHUNK // SUB
GROUP_UNROLL = 4


def _dot_tn(a, b):
    return lax.dot_general(a, b, (((0,), (0,)), ((), ())), preferred_element_type=F32)


def _ssm_body(x_ref, shift_ref, scale_ref, wut_ref, lhs_ref, bpow_ref, cpow_ref, coef_ref, eye_ref,
              y_ref, hfin_ref, r_scr, yt_scr, s_scr, carry_scr):
    i = pl.program_id(1)
    j = pl.program_id(2)

    @pl.when((i == 0) & (j == 0))
    def _():
        carry_scr[...] = jnp.zeros_like(carry_scr)

    @pl.when(j < NPH)
    def _():
        for sp in range(SUB // 2):
            h = jnp.concatenate([x_ref[:, 2 * sp, :], x_ref[:, 2 * sp + 1, :]], axis=0)
            hb = (h * (1.0 + scale_ref[...]) + shift_ref[...]).astype(BF16)
            ut = _dot_nt(wut_ref[...], hb).astype(BF16)
            ut = ut.reshape(SSM_GROUPS, SSM_GROUP, 2 * NKS)
            pair = jnp.concatenate([ut[:, :, :NKS], ut[:, :, NKS:]], axis=1)
            r0 = pl.multiple_of(j * SUB * SSM_GROUP + sp * 2 * SSM_GROUP, 2 * SSM_GROUP)
            r_scr[:, pl.ds(r0, 2 * SSM_GROUP), :] = pair

    @pl.when(j == NPH - 1)
    def _():
        def local(g, c):
            rg = r_scr[g]
            yt_scr[g] = _dot(lhs_ref[g], rg)
            s_scr[g] = _dot_tn(rg, bpow_ref[g])
            return c

        lax.fori_loop(0, SSM_GROUPS, local, 0, unroll=GROUP_UNROLL)

        row = lax.broadcasted_iota(jnp.int32, (GC, 8, SW), 1)
        for gc in range(SSM_GROUPS // GC):
            gs = slice(gc * GC, (gc + 1) * GC)
            coef = coef_ref[gs]
            for r in range(NKS // 8):
                rs = slice(r * 8, (r + 1) * 8)
                blk = s_scr[gs, rs, :]
                for k, d in enumerate((1, 2, 4)):
                    sh = jnp.where(row >= d, pltpu.roll(blk, d, 1), 0.0)
                    blk = blk + _cmul(coef[:, 2 * k:2 * k + 1], coef[:, 2 * k + 1:2 * k + 2], sh)
                s_scr[gs, rs, :] = blk
        for gc in range(SSM_GROUPS // GC):
            gs = slice(gc * GC, (gc + 1) * GC)
            coef = coef_ref[gs]
            c1, c2 = coef[:, 8:16], coef[:, 16:24]
            c1e = jnp.broadcast_to(c1[:, 7:8, :], (GC, 8, SW))
            c2e = jnp.broadcast_to(c2[:, 7:8, :], (GC, 8, SW))
            h = carry_scr[gs]
            hs = pltpu.roll(h, SSM_STATE, 2)
            for r in range(NKS // 8):
                rs = slice(r * 8, (r + 1) * 8)
                blk = s_scr[gs, rs, :]
                full = blk + c1 * h + c2 * hs
                s_scr[gs, rs, :] = jnp.where(row == 0, h, pltpu.roll(full, 1, 1))
                end = jnp.broadcast_to(blk[:, 7:8, :], (GC, 8, SW))
                ends = pltpu.roll(end, SSM_STATE, 2)
                h, hs = end + c1e * h + c2e * hs, ends + c1e * hs - c2e * h
            carry_scr[gs] = h
        hfin_ref[...] = carry_scr[...]

        def carried(g, c):
            yt_scr[g] += _dot_nt(cpow_ref[g], s_scr[g].astype(BF16))
            return c

        lax.fori_loop(0, SSM_GROUPS, carried, 0, unroll=GROUP_UNROLL)

    @pl.when(j >= NPH)
    def _():
        for sp in range(SUB):
            t0 = pl.multiple_of((j - NPH) * SUB * SSM_GROUP + sp * SSM_GROUP, SSM_GROUP)
            yt = yt_scr[:, pl.ds(t0, SSM_GROUP), :].reshape(D_MODEL, NKS)
            hi = yt.astype(BF16)
            lo = (yt - hi.astype(F32)).astype(BF16)
            y_ref[:, sp, :] = _dot_nt(eye_ref[...], hi) + _dot_nt(eye_ref[...], lo)


def _ssm_prompt(x_prompt, modp, wut, lhs, bpow, cpow, coef, eye):
    n_batch, seq, _ = x_prompt.shape
    steps = seq // SSM_TOK
    xv = x_prompt.reshape(n_batch * seq // CHUNK, CHUNK, D_MODEL)
    mod_spec = lambda k: pl.BlockSpec((None, 1, D_MODEL), lambda b, i, j, k=k: (b * 3 + k, 0, 0))
    consts = (wut, lhs, bpow, cpow, coef, eye)
    y, hfin = pl.pallas_call(
        _ssm_body,
        grid=(n_batch, steps, 2 * NPH),
        in_specs=[pl.BlockSpec((NKS, SUB, D_MODEL), lambda b, i, j: (b * steps + i, jnp.minimum(j, NPH - 1), 0),
                               pipeline_mode=pl.Buffered(1)),
                  mod_spec(0), mod_spec(1)] + [_const_spec(c.shape) for c in consts],
        out_specs=[pl.BlockSpec((NKS, SUB, D_MODEL), lambda b, i, j: (b * steps + i, jnp.maximum(j - NPH, 0), 0)),
                   pl.BlockSpec((None, SSM_GROUPS, 8, SW), lambda b, i, j: (b, 0, 0, 0))],
        out_shape=[jax.ShapeDtypeStruct((n_batch * seq // CHUNK, CHUNK, D_MODEL), F32),
                   jax.ShapeDtypeStruct((n_batch, SSM_GROUPS, 8, SW), F32)],
        scratch_shapes=[pltpu.VMEM((SSM_GROUPS, CW, NKS), BF16), pltpu.VMEM((SSM_GROUPS, CW, NKS), F32),
                        pltpu.VMEM((SSM_GROUPS, NKS, SW), F32), pltpu.VMEM((SSM_GROUPS, 8, SW), F32)],
        compiler_params=pltpu.CompilerParams(dimension_semantics=("arbitrary", "arbitrary", "arbitrary"),
                                             vmem_limit_bytes=VMEM_LIMIT),
        name="ssm_prompt",
    )(xv, modp, modp, *consts)
    return y.reshape(n_batch * seq, D_MODEL), hfin


def _post(ys, zs, o_a, za, gs, ga, x, gate, wglu_ref, bglu_ref, wbs_ref, wba_ref, wout_ref, lng_ref, lnb_ref):
    ys = _gelu_tanh(ys)
    glu = ys * _sigmoid(_dot(ys.astype(BF16), wglu_ref[...]) + bglu_ref[...])
    t = glu.astype(BF16) * _silu(zs)
    b_s = _dot(t, wbs_ref[...])
    oz = o_a.astype(BF16) * _silu(za)
    b_a = _dot(oz, wba_ref[...])
    m = _sigmoid(gs).astype(F32) * b_s + _sigmoid(ga).astype(F32) * b_a
    out = _dot(m.astype(BF16), wout_ref[...])
    r = DEEPNORM_ALPHA * x + gate * out
    mu = jnp.mean(r, axis=-1, keepdims=True)
    rc = r - mu
    var = jnp.mean(rc * rc, axis=-1, keepdims=True)
    return rc * lax.rsqrt(var + LN_EPS) * lng_ref[...] + lnb_ref[...]


TB = 512
NQB = TB // WINDOW
W_MAIN = 5 * D_MODEL + KV_WIDTH
ONES_ROWS = 16
POST_ROWS = 256


def _const_spec(shape):
    nd = len(shape)
    return pl.BlockSpec(shape, lambda *_: (0,) * nd, pipeline_mode=pl.Buffered(1))


def _main_body(x_ref, yssm_ref, shift_ref, scale_ref, gate_ref, wmain_ref, wvt_ref, biast_ref, sink_ref, eye_ref,
               wglu_ref, bglu_ref, wbs_ref, wba_ref, wout_ref, lng_ref, lnb_ref,
               y_ref, ktail_ref, vtail_ref,
               zs_s, q_s, za_s, gs_s, ga_s, k_s, vt_s, ot_s, o_s):
    i = pl.program_id(1)
    x = x_ref[...]
    hb = (x * (1.0 + scale_ref[...]) + shift_ref[...]).astype(BF16)

    def project(dst, c0, c):
        dst[:, c * 512:(c + 1) * 512] = _dot(hb, wmain_ref[:, c0 + c * 512:c0 + (c + 1) * 512]).astype(BF16)

    for c in range(2):
        project(q_s, D_MODEL, c)
    deferred = [(dst, c0, c) for dst, c0 in ((zs_s, 0), (za_s, 2 * D_MODEL), (gs_s, 3 * D_MODEL), (ga_s, 4 * D_MODEL))
                for c in range(2)]

    @pl.when(i == 0)
    def _():
        k_s[0:WINDOW, :] = jnp.zeros((WINDOW, KV_WIDTH), BF16)
        vt_s[:, 0:WINDOW] = jnp.zeros((KV_WIDTH, WINDOW), BF16)

    k_s[WINDOW:WINDOW + TB, :] = _dot(hb, wmain_ref[:, 5 * D_MODEL:W_MAIN]).astype(BF16)
    vt_s[:, WINDOW:WINDOW + TB] = _dot_nt(wvt_ref[...], hb).astype(BF16)

    first = jnp.where(i == 0, 1, 0)
    lane = lax.broadcasted_iota(jnp.int32, (2 * WINDOW, 128), 1)
    ones = jnp.ones((ONES_ROWS, 2 * WINDOW), BF16)
    for blk in range(NQB):
        rows = slice(blk * WINDOW, (blk + 1) * WINDOW)
        kcat = k_s[blk * WINDOW:blk * WINDOW + 2 * WINDOW, :]
        vtc = vt_s[:, blk * WINDOW:blk * WINDOW + 2 * WINDOW]
        sel = first if blk == 0 else 0
        heads = range(N_KV_HEADS)
        sts = []
        for kh in heads:
            pair, odd = kh // 2, kh % 2
            kcm = jnp.where((lane >= HEAD_DIM) == bool(odd), kcat[:, pair * 128:(pair + 1) * 128], jnp.zeros((), BF16))
            qp = jnp.concatenate(
                [q_s[rows, j * KV_WIDTH + pair * 128:j * KV_WIDTH + (pair + 1) * 128] for j in range(KV_GROUP)],
                axis=0)
            sts.append(_dot_nt(kcm, qp) + biast_ref[sel, kh])
        ms = [jnp.maximum(jnp.max(sts[kh], axis=0, keepdims=True), sink_ref[kh]) for kh in heads]
        es = [jnp.exp(sts[kh] - ms[kh]).astype(BF16) for kh in heads]
        rs = [_dot(jnp.concatenate([vtc[kh * HEAD_DIM:(kh + 1) * HEAD_DIM, :], ones], axis=0), es[kh])
              for kh in heads]
        for kh in heads:
            den = rs[kh][HEAD_DIM:HEAD_DIM + 1, :] + jnp.exp(sink_ref[kh] - ms[kh])
            ot = (rs[kh][:HEAD_DIM, :] / den).astype(BF16)
            for j in range(KV_GROUP):
                ot_s[blk, j, kh * HEAD_DIM:(kh + 1) * HEAD_DIM, :] = ot[:, j * WINDOW:(j + 1) * WINDOW]
        for spec in deferred[blk * len(deferred) // NQB:(blk + 1) * len(deferred) // NQB]:
            project(*spec)
        for j in range(KV_GROUP):
            o_s[rows, j * KV_WIDTH:(j + 1) * KV_WIDTH] = _dot_nt(eye_ref[...], ot_s[blk, j]).astype(BF16)

    @pl.when(i == pl.num_programs(1) - 1)
    def _():
        ktail_ref[...] = k_s[TB:TB + WINDOW, :].astype(F32)
        vtail_ref[...] = _dot_nt(eye_ref[...], vt_s[:, TB:TB + WINDOW])

    k_s[0:WINDOW, :] = k_s[TB:TB + WINDOW, :]
    vt_s[:, 0:WINDOW] = vt_s[:, TB:TB + WINDOW]

    for c in range(TB // POST_ROWS):
        rs = slice(c * POST_ROWS, (c + 1) * POST_ROWS)
        y_ref[rs, :] = _post(yssm_ref[rs, :], zs_s[rs, :], o_s[rs, :], za_s[rs, :], gs_s[rs, :], ga_s[rs, :],
                             x_ref[rs, :], gate_ref[...], wglu_ref, bglu_ref, wbs_ref, wba_ref, wout_ref,
                             lng_ref, lnb_ref)


def _main_prompt(x2d, yssm, modp, wmain, wvt, biast, sink_rows, eye, wglu, bglu, wbs, wba, wout, lng, lnb,
                 n_batch, seq):
    steps = seq // TB
    row_spec = pl.BlockSpec((TB, D_MODEL), lambda b, i: (b * steps + i, 0))
    mod_spec = lambda j: pl.BlockSpec((None, 1, D_MODEL), lambda b, i, j=j: (b * 3 + j, 0, 0))
    tail_spec = pl.BlockSpec((None, WINDOW, KV_WIDTH), lambda b, i: (b, 0, 0))
    consts = (wmain, wvt, biast, sink_rows, eye, wglu, bglu, wbs, wba, wout, lng, lnb)
    return pl.pallas_call(
        _main_body,
        grid=(n_batch, steps),
        in_specs=[row_spec, row_spec, mod_spec(0), mod_spec(1), mod_spec(2)] + [_const_spec(c.shape) for c in consts],
        out_specs=[row_spec, tail_spec, tail_spec],
        out_shape=[jax.ShapeDtypeStruct((n_batch * seq, D_MODEL), F32),
                   jax.ShapeDtypeStruct((n_batch, WINDOW, KV_WIDTH), F32),
                   jax.ShapeDtypeStruct((n_batch, WINDOW, KV_WIDTH), F32)],
        scratch_shapes=[pltpu.VMEM((TB, D_MODEL), BF16)] * 5
        + [pltpu.VMEM((TB + WINDOW, KV_WIDTH), BF16), pltpu.VMEM((KV_WIDTH, TB + WINDOW), BF16),
           pltpu.VMEM((NQB, KV_GROUP, KV_WIDTH, WINDOW), BF16), pltpu.VMEM((TB, D_MODEL), BF16)],
        compiler_params=pltpu.CompilerParams(dimension_semantics=("arbitrary", "arbitrary"),
                                             vmem_limit_bytes=VMEM_LIMIT),
        name="main_prompt",
    )(x2d, yssm, modp, modp, modp, *consts)


GT = 8


def _sample_ssm_body(u_ref, h0r_ref, h0i_ref, ar_ref, ai_ref, bre_ref, bim_ref, cblk_ref, d_ref,
                     y_ref, hr_ref, hi_ref):
    ub = u_ref[...]
    ys = []
    for a in range(GT):
        ua = ub[:, a * 128:(a + 1) * 128]
        st = slice(a * 512, (a + 1) * 512)
        xr = _dot(ua, bre_ref[a])
        xi = _dot(ua, bim_ref[a])
        h0r = h0r_ref[:, st]
        h0i = h0i_ref[:, st]
        ar = ar_ref[:, st]
        ai = ai_ref[:, st]
        hr = ar * h0r - ai * h0i + xr
        hi = ar * h0i + ai * h0r + xi
        hr_ref[:, st] = hr
        hi_ref[:, st] = hi
        hcat = jnp.concatenate([hr, hi], axis=1).astype(BF16)
        ys.append(_dot(hcat, cblk_ref[a]))
    y_ref[...] = jnp.concatenate(ys, axis=1) + d_ref[...] * ub.astype(F32)


def _sample_ssm(proj_s, h0r, h0i, ar, ai, bre, bim, cblk, d_row):
    rows = proj_s.shape[0]
    ns = SSM_GROUPS * SSM_STATE
    full = lambda shape: pl.BlockSpec(shape, lambda i: (0,) * len(shape))
    return pl.pallas_call(
        _sample_ssm_body,
        grid=(1,),
        in_specs=[pl.BlockSpec((rows, D_MODEL), lambda i: (0, COL_U)),
                  full((rows, ns)), full((rows, ns)), full((1, ns)), full((1, ns)),
                  full(bre.shape), full(bim.shape), full(cblk.shape), full((1, D_MODEL))],
        out_specs=[full((rows, D_MODEL)), full((rows, ns)), full((rows, ns))],
        out_shape=[jax.ShapeDtypeStruct((rows, D_MODEL), F32),
                   jax.ShapeDtypeStruct((rows, ns), F32),
                   jax.ShapeDtypeStruct((rows, ns), F32)],
        compiler_params=pltpu.CompilerParams(vmem_limit_bytes=VMEM_LIMIT),
        name="sample_ssm",
    )(proj_s, h0r, h0i, ar, ai, bre, bim, cblk, d_row)


SB = 16
KROWS = WINDOW + 8


def _sample_attn_body(q_ref, kn_ref, vn_ref, ck_ref, cv_ref, bias_ref, sink_ref, seg_ref, segt_ref,
                      o_ref, nk_ref, nv_ref):
    qf = q_ref[...].astype(F32)
    knf = kn_ref[...].astype(F32)
    vnf = vn_ref[...].astype(F32)
    rowk = lax.broadcasted_iota(jnp.int32, (KROWS, 128), 0)
    roww = lax.broadcasted_iota(jnp.int32, (WINDOW, KV_WIDTH), 0)
    sink = sink_ref[...]
    for b in range(SB):
        kb = ck_ref[b]
        vb = cv_ref[b]
        qrow = qf[b:b + 1]
        kn = jnp.broadcast_to(knf[b:b + 1], (8, KV_WIDTH))
        vn = jnp.broadcast_to(vnf[b:b + 1], (8, KV_WIDTH))
        kall = jnp.concatenate([kb, kn], axis=0)
        vall = jnp.concatenate([vb, vn], axis=0)
        prod = jnp.concatenate([kall * qrow[:, j * KV_WIDTH:(j + 1) * KV_WIDTH] for j in range(KV_GROUP)], axis=1)
        s = _dot(prod.astype(BF16), seg_ref[...]) + bias_ref[...]
        s = jnp.where(rowk <= WINDOW, s, NEG_INF)
        m = jnp.maximum(jnp.max(s, axis=0, keepdims=True), sink)
        e = jnp.exp(s - m)
        den = jnp.sum(e, axis=0, keepdims=True) + jnp.exp(sink - m)
        p = e / den
        pexp = _dot(p.astype(BF16), segt_ref[...])
        v4 = jnp.concatenate([vall] * KV_GROUP, axis=1)
        o_ref[b:b + 1, :] = jnp.sum(pexp * v4, axis=0, keepdims=True)
        last = roww == WINDOW - 1
        nk_ref[b] = jnp.where(last, jnp.broadcast_to(knf[b:b + 1], (WINDOW, KV_WIDTH)), pltpu.roll(kb, WINDOW - 1, 0))
        nv_ref[b] = jnp.where(last, jnp.broadcast_to(vnf[b:b + 1], (WINDOW, KV_WIDTH)), pltpu.roll(vb, WINDOW - 1, 0))


def _sample_attn(proj_s, ck, cv, bias_s, sink_row, seg, segt):
    rows = proj_s.shape[0]
    cache_spec = pl.BlockSpec((SB, WINDOW, KV_WIDTH), lambda i: (i, 0, 0))
    return pl.pallas_call(
        _sample_attn_body,
        grid=(rows // SB,),
        in_specs=[pl.BlockSpec((SB, D_MODEL), lambda i: (i, COL_Q)),
                  pl.BlockSpec((SB, KV_WIDTH), lambda i: (i, COL_K)),
                  pl.BlockSpec((SB, KV_WIDTH), lambda i: (i, COL_V)),
                  cache_spec, cache_spec,
                  pl.BlockSpec(bias_s.shape, lambda i: (0, 0)),
                  pl.BlockSpec(sink_row.shape, lambda i: (0, 0)),
                  pl.BlockSpec(seg.shape, lambda i: (0, 0)),
                  pl.BlockSpec(segt.shape, lambda i: (0, 0))],
        out_specs=[pl.BlockSpec((SB, D_MODEL), lambda i: (i, 0)), cache_spec, cache_spec],
        out_shape=[jax.ShapeDtypeStruct((rows, D_MODEL), F32),
                   jax.ShapeDtypeStruct(ck.shape, F32),
                   jax.ShapeDtypeStruct(cv.shape, F32)],
        compiler_params=pltpu.CompilerParams(vmem_limit_bytes=VMEM_LIMIT),
        name="sample_attn",
    )(proj_s, proj_s, proj_s, ck, cv, bias_s, sink_row, seg, segt)


def _sample_final_body(x_ref, yssm_ref, zs_ref, o_ref, za_ref, gs_ref, ga_ref, gate_ref,
                       wglu_ref, bglu_ref, wbs_ref, wba_ref, wout_ref, lng_ref, lnb_ref, y_ref):
    y_ref[...] = _post(yssm_ref[...], zs_ref[...], o_ref[...], za_ref[...], gs_ref[...], ga_ref[...],
                       x_ref[...], gate_ref[...], wglu_ref, bglu_ref, wbs_ref, wba_ref, wout_ref,
                       lng_ref, lnb_ref)


def _final_sample(x2d, yssm, o_a, proj_s, mods, wglu, bglu, wbs, wba, wout, lng, lnb):
    rows = x2d.shape[0]
    full = lambda shape: pl.BlockSpec(shape, lambda i: (0,) * len(shape))
    pcol = lambda c: pl.BlockSpec((rows, D_MODEL), lambda i, c=c: (0, c))
    return pl.pallas_call(
        _sample_final_body,
        grid=(1,),
        in_specs=[full((rows, D_MODEL)), full((rows, D_MODEL)), pcol(COL_ZS), full((rows, D_MODEL)),
                  pcol(COL_ZA), pcol(COL_GS), pcol(COL_GA),
                  pl.BlockSpec((rows, D_MODEL), lambda i: (0, 2)),
                  full(wglu.shape), full(bglu.shape), full(wbs.shape), full(wba.shape), full(wout.shape),
                  full(lng.shape), full(lnb.shape)],
        out_specs=full((rows, D_MODEL)),
        out_shape=jax.ShapeDtypeStruct((rows, D_MODEL), F32),
        compiler_params=pltpu.CompilerParams(vmem_limit_bytes=VMEM_LIMIT),
        name="final_sample",
    )(x2d, yssm, proj_s, o_a, proj_s, proj_s, proj_s, mods, wglu, bglu, wbs, wba, wout, lng, lnb)


def _bucket_table():
    max_exact = N_BUCKETS // 2
    dist = np.arange(WINDOW + 1)
    df = np.maximum(dist, 1).astype(np.float32)
    large = max_exact + (np.log(df / np.float32(max_exact)) / np.float32(math.log(WINDOW / max_exact))
                         * np.float32(N_BUCKETS - max_exact)).astype(np.int32)
    large = np.minimum(large, N_BUCKETS - 1)
    return np.where(dist < max_exact, dist, large)


def _ssm_params(lam_re, lam_im, log_delta, b_re, b_im, c_re, c_im, d_skip):
    hp = lax.Precision.HIGHEST
    g, p, c = SSM_GROUPS, SSM_STATE, SSM_GROUP
    lr, li = lam_re.astype(F32), lam_im.astype(F32)
    dt = jnp.exp(log_delta.astype(F32))[:, None]
    mag = jnp.exp(lr * dt)
    ar, ai = mag * jnp.cos(li * dt), mag * jnp.sin(li * dt)
    den = lr * lr + li * li
    nr = ar - 1.0
    cr_ = (nr * lr + ai * li) / den
    ci_ = (ai * lr - nr * li) / den
    bbr = cr_[..., None] * b_re - ci_[..., None] * b_im
    bbi = cr_[..., None] * b_im + ci_[..., None] * b_re

    def powers(taus):
        t = jnp.asarray(taus, F32)[:, None, None]
        m = jnp.exp(lr * dt * t)
        return m * jnp.cos(li * dt * t), m * jnp.sin(li * dt * t)

    pr, pi = powers(np.arange(CHUNK + 1))
    wr = pr[:CHUNK, :, :, None] * bbr - pi[:CHUNK, :, :, None] * bbi
    wi = pr[:CHUNK, :, :, None] * bbi + pi[:CHUNK, :, :, None] * bbr
    kt = (jnp.einsum('gcp,tgpd->tgcd', c_re, wr, precision=hp)
          - jnp.einsum('gcp,tgpd->tgcd', c_im, wi, precision=hp))
    tt = np.arange(CHUNK)[:, None, None]
    ss = np.arange(CHUNK)[None, :, None]
    lag_onehot = jnp.asarray((tt - ss) == np.arange(CHUNK)[None, None, :], F32)
    m5 = jnp.einsum('tsk,kgcd->tsgcd', lag_onehot, kt, precision=hp)
    m5 = m5 + (jnp.eye(CHUNK, dtype=F32)[:, :, None, None, None]
               * (d_skip.reshape(g, c)[None, None, :, :, None] * jnp.eye(c, dtype=F32)[None, None, None]))
    lhs = jnp.transpose(m5, (2, 0, 3, 1, 4)).reshape(g, CW, CW).astype(BF16)
    rev = np.arange(CHUNK)[::-1]
    bpr = jnp.transpose(wr[rev], (1, 0, 3, 2)).reshape(g, CW, p)
    bpi = jnp.transpose(wi[rev], (1, 0, 3, 2)).reshape(g, CW, p)
    bpow = jnp.concatenate([bpr, bpi], axis=2).astype(BF16)
    p1r, p1i = pr[1:], pi[1:]
    cpr = c_re[None] * p1r[:, :, None, :] - c_im[None] * p1i[:, :, None, :]
    cpi = -(c_re[None] * p1i[:, :, None, :] + c_im[None] * p1r[:, :, None, :])
    cpow = jnp.concatenate([jnp.transpose(cpr, (1, 0, 2, 3)).reshape(g, CW, p),
                            jnp.transpose(cpi, (1, 0, 2, 3)).reshape(g, CW, p)], axis=2).astype(BF16)
    qr, qi = powers(CHUNK * np.arange(1, 9))
    c1 = jnp.concatenate([qr, qr], axis=2)
    c2 = jnp.concatenate([-qi, qi], axis=2)
    log_rows = jnp.stack([c1[0], c2[0], c1[1], c2[1], c1[3], c2[3], c1[7], c2[7]], axis=1)
    coef = jnp.concatenate([log_rows, jnp.transpose(c1, (1, 0, 2)), jnp.transpose(c2, (1, 0, 2))], axis=1)

    eye8 = jnp.eye(GT, dtype=F32)
    bre = jnp.einsum('agpc,gh->agchp', bbr.reshape(GT, GT, p, c), eye8).reshape(GT, GT * c, GT * p).astype(BF16)
    bim = jnp.einsum('agpc,gh->agchp', bbi.reshape(GT, GT, p, c), eye8).reshape(GT, GT * c, GT * p).astype(BF16)
    cre = jnp.einsum('agcp,gh->agphc', c_re.reshape(GT, GT, c, p), eye8).reshape(GT, GT * p, GT * c)
    cim = jnp.einsum('agcp,gh->agphc', c_im.reshape(GT, GT, c, p), eye8).reshape(GT, GT * p, GT * c)
    cblk = jnp.concatenate([cre, -cim], axis=1).astype(BF16)
    return lhs, bpow, cpow, coef, ar.reshape(1, -1), ai.reshape(1, -1), bre, bim, cblk


def _attn_params(rel_bias, sinks):
    hp = lax.Precision.HIGHEST
    rb = rel_bias.astype(F32)
    onehot = jnp.asarray(_bucket_table()[:, None] == np.arange(N_BUCKETS)[None, :], F32)
    tbl = jnp.dot(onehot, rb, precision=hp)
    span = 3 * WINDOW
    w = jnp.concatenate([tbl[::-1], jnp.full((span - WINDOW - 1, N_HEADS), NEG_INF, F32)], axis=0)
    wt = jnp.broadcast_to(jnp.transpose(w)[:, None, :], (N_HEADS, WINDOW, span)).reshape(N_HEADS, WINDOW * span)
    bq = wt[:, :WINDOW * (span - 1)].reshape(N_HEADS, WINDOW, span - 1)[:, :, :2 * WINDOW]
    first = jnp.where(jnp.arange(2 * WINDOW)[None, None, :] >= WINDOW, bq, NEG_INF)
    bm = jnp.stack([bq, first], axis=0)
    biast = jnp.transpose(bm.reshape(2, N_KV_HEADS, KV_GROUP, WINDOW, 2 * WINDOW), (0, 1, 4, 2, 3)) \
        .reshape(2, N_KV_HEADS, 2 * WINDOW, KV_GROUP * WINDOW)
    sink_rows = jnp.repeat(sinks.astype(F32).reshape(N_KV_HEADS, 1, KV_GROUP), WINDOW, axis=2)
    ds = np.clip(WINDOW - np.arange(KROWS), 0, WINDOW)
    bs = jnp.dot(jnp.asarray(ds[:, None] == np.arange(WINDOW + 1)[None, :], F32), tbl, precision=hp)
    bs = jnp.transpose(bs.reshape(KROWS, N_KV_HEADS, KV_GROUP), (0, 2, 1)).reshape(KROWS, N_HEADS)
    bias_s = jnp.pad(bs, ((0, 0), (0, 128 - N_HEADS)))
    sk = jnp.transpose(sinks.astype(F32).reshape(N_KV_HEADS, KV_GROUP), (1, 0)).reshape(1, N_HEADS)
    sink_row = jnp.pad(sk, ((0, 0), (0, 128 - N_HEADS)))
    seg_np = np.zeros((D_MODEL, 128), np.float32)
    for j in range(KV_GROUP):
        for kh in range(N_KV_HEADS):
            r0 = j * KV_WIDTH + kh * HEAD_DIM
            seg_np[r0:r0 + HEAD_DIM, j * N_KV_HEADS + kh] = 1.0
    seg = jnp.asarray(seg_np, BF16)
    segt = jnp.asarray(seg_np.T, BF16)
    return biast, sink_rows, bias_s, sink_row, seg, segt


def kernel(x_prompt, x_sample, c_prompt, c_sample, state_ssm_re, state_ssm_im, cache_swa_k, cache_swa_v,
           w_ada, b_ada, w_in, ssm_lambda_re, ssm_lambda_im, ssm_log_delta, ssm_b_re, ssm_b_im,
           ssm_c_re, ssm_c_im, ssm_d, w_glu, b_glu, attn_sinks, rel_bias, w_branch_s, w_branch_a,
           w_out, ln_g, ln_b):
    assert w_ada.shape[0] == 1, "single-layer trunk"
    n_batch, seq, _ = x_prompt.shape
    n_dec = x_sample.shape[0]
    perm = _head_perm()

    wi = w_in[0]
    w_in_b = jnp.concatenate([wi[:, 0:1024], wi[:, 1024:2048], wi[:, 2048:3072][:, perm] * (HEAD_DIM ** -0.5),
                              wi[:, 3584:4608][:, perm], wi[:, 4608:5632], wi[:, 5632:6656],
                              wi[:, 3072:3328], wi[:, 3328:3584]], axis=1).astype(BF16)
    wmain = w_in_b[:, D_MODEL:D_MODEL + W_MAIN]
    wvt = jnp.transpose(wi[:, 3328:3584]).astype(BF16)
    wut = jnp.transpose(wi[:, 0:D_MODEL]).astype(BF16)
    eye = jnp.eye(WINDOW, dtype=BF16)
    wglu = w_glu[0].astype(BF16)
    bglu = b_glu[0].reshape(1, -1).astype(F32)
    wbs = w_branch_s[0].astype(BF16)
    wba = w_branch_a[0][perm, :].astype(BF16)
    wout = w_out[0].astype(BF16)
    lng = ln_g[0].reshape(1, -1).astype(F32)
    lnb = ln_b[0].reshape(1, -1).astype(F32)
    lhs, bpow, cpow, coef, ar_row, ai_row, bre, bim, cblk = _ssm_params(
        ssm_lambda_re[0], ssm_lambda_im[0], ssm_log_delta[0], ssm_b_re[0], ssm_b_im[0],
        ssm_c_re[0], ssm_c_im[0], ssm_d[0])
    biast, sink_rows, bias_s, sink_row, seg, segt = _attn_params(rel_bias, attn_sinks[0])

    c_all = jnp.concatenate([c_prompt, jnp.zeros((8 - n_batch, D_MODEL), F32), c_sample], axis=0)
    mod = _ada(c_all, w_ada[0], b_ada[0])
    modp = mod[:n_batch].reshape(n_batch * 3, 1, D_MODEL)
    mods = mod[8:8 + n_dec]

    xp = x_prompt.reshape(n_batch * seq, D_MODEL)
    yssm, hfin = _ssm_prompt(x_prompt, modp, wut, lhs, bpow, cpow, coef, eye)
    yp, ktail, vtail = _main_prompt(xp, yssm, modp, wmain, wvt, biast, sink_rows, eye,
                                    wglu, bglu, wbs, wba, wout, lng, lnb, n_batch, seq)
    yp = yp.reshape(n_batch, seq, D_MODEL)
    p_hr = hfin[:, :, 0, :SSM_STATE][None]
    p_hi = hfin[:, :, 0, SSM_STATE:][None]
    p_k = ktail.reshape(1, n_batch, WINDOW, N_KV_HEADS, HEAD_DIM)
    p_v = vtail.reshape(1, n_batch, WINDOW, N_KV_HEADS, HEAD_DIM)

    xs = x_sample.reshape(n_dec, D_MODEL)
    proj_s = _proj(xs, mods, mods,
                   pl.BlockSpec((n_dec, D_MODEL), lambda i: (0, 0)),
                   pl.BlockSpec((n_dec, D_MODEL), lambda i: (0, 1)),
                   w_in_b, n_dec, D_IN)
    ns = SSM_GROUPS * SSM_STATE
    ys_s, s_hr, s_hi = _sample_ssm(proj_s, state_ssm_re[0].reshape(n_dec, ns), state_ssm_im[0].reshape(n_dec, ns),
                                   ar_row, ai_row, bre, bim, cblk, ssm_d[0].reshape(1, -1).astype(F32))
    o_s, nk, nv = _sample_attn(proj_s, cache_swa_k[0].reshape(n_dec, WINDOW, KV_WIDTH),
                               cache_swa_v[0].reshape(n_dec, WINDOW, KV_WIDTH), bias_s, sink_row, seg, segt)
    ysmp = _final_sample(xs, ys_s, o_s, proj_s, mods, wglu, bglu, wbs, wba, wout, lng, lnb)

    return (yp, ysmp.reshape(n_dec, 1, D_MODEL),
            p_hr, p_hi, p_k, p_v,
            s_hr.reshape(1, n_dec, SSM_GROUPS, SSM_STATE), s_hi.reshape(1, n_dec, SSM_GROUPS, SSM_STATE),
            nk.reshape(1, n_dec, WINDOW, N_KV_HEADS, HEAD_DIM), nv.reshape(1, n_dec, WINDOW, N_KV_HEADS, HEAD_DIM))
```

```python
import functools
import math

import numpy as np
import jax
import jax.numpy as jnp
from jax import lax
from jax.experimental import pallas as pl
from jax.experimental.pallas import tpu as pltpu

F32 = jnp.float32
BF16 = jnp.bfloat16

D_MODEL = 1024
SSM_GROUPS = 64
SSM_GROUP = 16
SSM_STATE = 64
N_HEADS = 16
HEAD_DIM = 64
N_KV_HEADS = 4
KV_GROUP = 4
KV_WIDTH = N_KV_HEADS * HEAD_DIM
WINDOW = 128
N_BUCKETS = 32
NEG_INF = -1e30
LN_EPS = 1e-5
DEPTH = 1
DEEPNORM_ALPHA = (2 * DEPTH) ** 0.25
D_IN = 6656
CHUNK = 16
CW = CHUNK * SSM_GROUP
SW = 2 * SSM_STATE

COL_U, COL_ZS, COL_Q, COL_ZA, COL_GS, COL_GA = 0, 1, 2, 3, 4, 5
COL_K, COL_V = 24, 25

VMEM_LIMIT = 56 * 1024 * 1024


def _head_perm():
    return np.arange(N_HEADS * HEAD_DIM).reshape(N_KV_HEADS, KV_GROUP, HEAD_DIM).transpose(1, 0, 2).reshape(-1)


def _sigmoid(x):
    return 0.5 * jnp.tanh(0.5 * x) + 0.5


def _silu(x):
    return x * _sigmoid(x)


def _gelu_tanh(x):
    c = math.sqrt(2.0 / math.pi)
    hx = 0.5 * x
    return hx + hx * jnp.tanh(x * (c + (c * 0.044715) * (x * x)))


def _dot(a, b):
    return jnp.dot(a, b, preferred_element_type=F32)


def _dot_nt(a, b):
    return lax.dot_general(a, b, (((1,), (1,)), ((), ())), preferred_element_type=F32)


def _ada_body(c_ref, w_ref, b_ref, o_ref):
    c = c_ref[...]
    sc = _silu(c).astype(BF16)
    o_ref[...] = _dot(sc, w_ref[...].astype(BF16)) + b_ref[...]


def _ada(c_all, w_ada, b_ada):
    rows = c_all.shape[0]
    bn = 512
    return pl.pallas_call(
        _ada_body,
        grid=(3 * D_MODEL // bn,),
        in_specs=[pl.BlockSpec((rows, D_MODEL), lambda j: (0, 0)),
                  pl.BlockSpec((D_MODEL, bn), lambda j: (0, j)),
                  pl.BlockSpec((1, bn), lambda j: (0, j))],
        out_specs=pl.BlockSpec((rows, bn), lambda j: (0, j)),
        out_shape=jax.ShapeDtypeStruct((rows, 3 * D_MODEL), F32),
        name="ada",
    )(c_all, w_ada, b_ada.reshape(1, -1))


PROJ_CB = 512


def _proj_body(x_ref, shift_ref, scale_ref, w_ref, o_ref, *, n_out):
    h = x_ref[...] * (1.0 + scale_ref[...]) + shift_ref[...]
    hb = h.astype(BF16)
    for j in range(n_out // PROJ_CB):
        sl = slice(j * PROJ_CB, (j + 1) * PROJ_CB)
        o_ref[:, sl] = _dot(hb, w_ref[:, sl]).astype(BF16)


def _proj(x2d, shift_arr, scale_arr, shift_spec, scale_spec, w_in_b, tm, n_out):
    rows = x2d.shape[0]
    return pl.pallas_call(
        functools.partial(_proj_body, n_out=n_out),
        grid=(rows // tm,),
        in_specs=[pl.BlockSpec((tm, D_MODEL), lambda i: (i, 0)),
                  shift_spec, scale_spec,
                  pl.BlockSpec((D_MODEL, n_out), lambda i: (0, 0), pipeline_mode=pl.Buffered(1))],
        out_specs=pl.BlockSpec((tm, n_out), lambda i: (i, 0)),
        out_shape=jax.ShapeDtypeStruct((rows, n_out), BF16),
        compiler_params=pltpu.CompilerParams(vmem_limit_bytes=VMEM_LIMIT),
        name="proj",
    )(x2d, shift_arr, scale_arr, w_in_b)


NKS = 128
GC = 16


def _cmul(c1, c2, x):
    return c1 * x + c2 * pltpu.roll(x, SSM_STATE, x.ndim - 1)


SSM_TOK = NKS * CHUNK
SUB = 8
NPH = CHUNK // SUB
GROUP_UNROLL = 4


def _dot_tn(a, b):
    return lax.dot_general(a, b, (((0,), (0,)), ((), ())), preferred_element_type=F32)


def _ssm_body(x_ref, shift_ref, scale_ref, wut_ref, lhs_ref, bpow_ref, cpow_ref, coef_ref, eye_ref,
              y_ref, hfin_ref, r_scr, yt_scr, s_scr, carry_scr):
    i = pl.program_id(1)
    j = pl.program_id(2)

    @pl.when((i == 0) & (j == 0))
    def _():
        carry_scr[...] = jnp.zeros_like(carry_scr)

    @pl.when(j < NPH)
    def _():
        for sp in range(SUB // 2):
            h = jnp.concatenate([x_ref[:, 2 * sp, :], x_ref[:, 2 * sp + 1, :]], axis=0)
            hb = (h * (1.0 + scale_ref[...]) + shift_ref[...]).astype(BF16)
            ut = _dot_nt(wut_ref[...], hb).astype(BF16)
            ut = ut.reshape(SSM_GROUPS, SSM_GROUP, 2 * NKS)
            pair = jnp.concatenate([ut[:, :, :NKS], ut[:, :, NKS:]], axis=1)
            r0 = pl.multiple_of(j * SUB * SSM_GROUP + sp * 2 * SSM_GROUP, 2 * SSM_GROUP)
            r_scr[:, pl.ds(r0, 2 * SSM_GROUP), :] = pair

    @pl.when(j == NPH - 1)
    def _():
        def local(g, c):
            rg = r_scr[g]
            yt_scr[g] = _dot(lhs_ref[g], rg)
            s_scr[g] = _dot_tn(rg, bpow_ref[g])
            return c

        lax.fori_loop(0, SSM_GROUPS, local, 0, unroll=GROUP_UNROLL)

        row = lax.broadcasted_iota(jnp.int32, (GC, 8, SW), 1)
        for gc in range(SSM_GROUPS // GC):
            gs = slice(gc * GC, (gc + 1) * GC)
            coef = coef_ref[gs]
            for r in range(NKS // 8):
                rs = slice(r * 8, (r + 1) * 8)
                blk = s_scr[gs, rs, :]
                for k, d in enumerate((1, 2, 4)):
                    sh = jnp.where(row >= d, pltpu.roll(blk, d, 1), 0.0)
                    blk = blk + _cmul(coef[:, 2 * k:2 * k + 1], coef[:, 2 * k + 1:2 * k + 2], sh)
                s_scr[gs, rs, :] = blk
        for gc in range(SSM_GROUPS // GC):
            gs = slice(gc * GC, (gc + 1) * GC)
            coef = coef_ref[gs]
            c1, c2 = coef[:, 8:16], coef[:, 16:24]
            c1e = jnp.broadcast_to(c1[:, 7:8, :], (GC, 8, SW))
            c2e = jnp.broadcast_to(c2[:, 7:8, :], (GC, 8, SW))
            h = carry_scr[gs]
            hs = pltpu.roll(h, SSM_STATE, 2)
            for r in range(NKS // 8):
                rs = slice(r * 8, (r + 1) * 8)
                blk = s_scr[gs, rs, :]
                full = blk + c1 * h + c2 * hs
                s_scr[gs, rs, :] = jnp.where(row == 0, h, pltpu.roll(full, 1, 1))
                end = jnp.broadcast_to(blk[:, 7:8, :], (GC, 8, SW))
                ends = pltpu.roll(end, SSM_STATE, 2)
                h, hs = end + c1e * h + c2e * hs, ends + c1e * hs - c2e * h
            carry_scr[gs] = h
        hfin_ref[...] = carry_scr[...]

        def carried(g, c):
            yt_scr[g] += _dot_nt(cpow_ref[g], s_scr[g].astype(BF16))
            return c

        lax.fori_loop(0, SSM_GROUPS, carried, 0, unroll=GROUP_UNROLL)

    @pl.when(j >= NPH)
    def _():
        for sp in range(SUB):
            t0 = pl.multiple_of((j - NPH) * SUB * SSM_GROUP + sp * SSM_GROUP, SSM_GROUP)
            yt = yt_scr[:, pl.ds(t0, SSM_GROUP), :].reshape(D_MODEL, NKS)
            hi = yt.astype(BF16)
            lo = (yt - hi.astype(F32)).astype(BF16)
            y_ref[:, sp, :] = _dot_nt(eye_ref[...], hi) + _dot_nt(eye_ref[...], lo)


def _ssm_prompt(x_prompt, modp, wut, lhs, bpow, cpow, coef, eye):
    n_batch, seq, _ = x_prompt.shape
    steps = seq // SSM_TOK
    xv = x_prompt.reshape(n_batch * seq // CHUNK, CHUNK, D_MODEL)
    mod_spec = lambda k: pl.BlockSpec((None, 1, D_MODEL), lambda b, i, j, k=k: (b * 3 + k, 0, 0))
    consts = (wut, lhs, bpow, cpow, coef, eye)
    y, hfin = pl.pallas_call(
        _ssm_body,
        grid=(n_batch, steps, 2 * NPH),
        in_specs=[pl.BlockSpec((NKS, SUB, D_MODEL), lambda b, i, j: (b * steps + i, jnp.minimum(j, NPH - 1), 0),
                               pipeline_mode=pl.Buffered(1)),
                  mod_spec(0), mod_spec(1)] + [_const_spec(c.shape) for c in consts],
        out_specs=[pl.BlockSpec((NKS, SUB, D_MODEL), lambda b, i, j: (b * steps + i, jnp.maximum(j - NPH, 0), 0)),
                   pl.BlockSpec((None, SSM_GROUPS, 8, SW), lambda b, i, j: (b, 0, 0, 0))],
        out_shape=[jax.ShapeDtypeStruct((n_batch * seq // CHUNK, CHUNK, D_MODEL), F32),
                   jax.ShapeDtypeStruct((n_batch, SSM_GROUPS, 8, SW), F32)],
        scratch_shapes=[pltpu.VMEM((SSM_GROUPS, CW, NKS), BF16), pltpu.VMEM((SSM_GROUPS, CW, NKS), F32),
                        pltpu.VMEM((SSM_GROUPS, NKS, SW), F32), pltpu.VMEM((SSM_GROUPS, 8, SW), F32)],
        compiler_params=pltpu.CompilerParams(dimension_semantics=("arbitrary", "arbitrary", "arbitrary"),
                                             vmem_limit_bytes=VMEM_LIMIT),
        name="ssm_prompt",
    )(xv, modp, modp, *consts)
    return y.reshape(n_batch * seq, D_MODEL), hfin


def _post(ys, zs, o_a, za, gs, ga, x, gate, wglu_ref, bglu_ref, wbs_ref, wba_ref, wout_ref, lng_ref, lnb_ref):
    ys = _gelu_tanh(ys)
    glu = ys * _sigmoid(_dot(ys.astype(BF16), wglu_ref[...]) + bglu_ref[...])
    t = glu.astype(BF16) * _silu(zs)
    b_s = _dot(t, wbs_ref[...])
    oz = o_a.astype(BF16) * _silu(za)
    b_a = _dot(oz, wba_ref[...])
    m = _sigmoid(gs).astype(F32) * b_s + _sigmoid(ga).astype(F32) * b_a
    out = _dot(m.astype(BF16), wout_ref[...])
    r = DEEPNORM_ALPHA * x + gate * out
    mu = jnp.mean(r, axis=-1, keepdims=True)
    rc = r - mu
    var = jnp.mean(rc * rc, axis=-1, keepdims=True)
    return rc * lax.rsqrt(var + LN_EPS) * lng_ref[...] + lnb_ref[...]


TB = 512
NQB = TB // WINDOW
W_MAIN = 5 * D_MODEL + KV_WIDTH
ONES_ROWS = 16
POST_ROWS = 256


def _const_spec(shape):
    nd = len(shape)
    return pl.BlockSpec(shape, lambda *_: (0,) * nd, pipeline_mode=pl.Buffered(1))


def _main_body(x_ref, yssm_ref, shift_ref, scale_ref, gate_ref, wmain_ref, wvt_ref, biast_ref, sink_ref, eye_ref,
               wglu_ref, bglu_ref, wbs_ref, wba_ref, wout_ref, lng_ref, lnb_ref,
               y_ref, ktail_ref, vtail_ref,
               zs_s, q_s, za_s, gs_s, ga_s, k_s, vt_s, ot_s, o_s):
    i = pl.program_id(1)
    x = x_ref[...]
    hb = (x * (1.0 + scale_ref[...]) + shift_ref[...]).astype(BF16)

    def project(dst, c0, c):
        dst[:, c * 512:(c + 1) * 512] = _dot(hb, wmain_ref[:, c0 + c * 512:c0 + (c + 1) * 512]).astype(BF16)

    for c in range(2):
        project(q_s, D_MODEL, c)
    deferred = [(dst, c0, c) for dst, c0 in ((zs_s, 0), (za_s, 2 * D_MODEL), (gs_s, 3 * D_MODEL), (ga_s, 4 * D_MODEL))
                for c in range(2)]

    @pl.when(i == 0)
    def _():
        k_s[0:WINDOW, :] = jnp.zeros((WINDOW, KV_WIDTH), BF16)
        vt_s[:, 0:WINDOW] = jnp.zeros((KV_WIDTH, WINDOW), BF16)

    k_s[WINDOW:WINDOW + TB, :] = _dot(hb, wmain_ref[:, 5 * D_MODEL:W_MAIN]).astype(BF16)
    vt_s[:, WINDOW:WINDOW + TB] = _dot_nt(wvt_ref[...], hb).astype(BF16)

    first = jnp.where(i == 0, 1, 0)
    lane = lax.broadcasted_iota(jnp.int32, (2 * WINDOW, 128), 1)
    ones = jnp.ones((ONES_ROWS, 2 * WINDOW), BF16)
    for blk in range(NQB):
        rows = slice(blk * WINDOW, (blk + 1) * WINDOW)
        kcat = k_s[blk * WINDOW:blk * WINDOW + 2 * WINDOW, :]
        vtc = vt_s[:, blk * WINDOW:blk * WINDOW + 2 * WINDOW]
        sel = first if blk == 0 else 0
        heads = range(N_KV_HEADS)
        sts = []
        for kh in heads:
            pair, odd = kh // 2, kh % 2
            kcm = jnp.where((lane >= HEAD_DIM) == bool(odd), kcat[:, pair * 128:(pair + 1) * 128], jnp.zeros((), BF16))
            qp = jnp.concatenate(
                [q_s[rows, j * KV_WIDTH + pair * 128:j * KV_WIDTH + (pair + 1) * 128] for j in range(KV_GROUP)],
                axis=0)
            sts.append(_dot_nt(kcm, qp) + biast_ref[sel, kh])
        ms = [jnp.maximum(jnp.max(sts[kh], axis=0, keepdims=True), sink_ref[kh]) for kh in heads]
        es = [jnp.exp(sts[kh] - ms[kh]).astype(BF16) for kh in heads]
        rs = [_dot(jnp.concatenate([vtc[kh * HEAD_DIM:(kh + 1) * HEAD_DIM, :], ones], axis=0), es[kh])
              for kh in heads]
        for kh in heads:
            den = rs[kh][HEAD_DIM:HEAD_DIM + 1, :] + jnp.exp(sink_ref[kh] - ms[kh])
            ot = (rs[kh][:HEAD_DIM, :] / den).astype(BF16)
            for j in range(KV_GROUP):
                ot_s[blk, j, kh * HEAD_DIM:(kh + 1) * HEAD_DIM, :] = ot[:, j * WINDOW:(j + 1) * WINDOW]
        for spec in deferred[blk * len(deferred) // NQB:(blk + 1) * len(deferred) // NQB]:
            project(*spec)
        for j in range(KV_GROUP):
            o_s[rows, j * KV_WIDTH:(j + 1) * KV_WIDTH] = _dot_nt(eye_ref[...], ot_s[blk, j]).astype(BF16)

    @pl.when(i == pl.num_programs(1) - 1)
    def _():
        ktail_ref[...] = k_s[TB:TB + WINDOW, :].astype(F32)
        vtail_ref[...] = _dot_nt(eye_ref[...], vt_s[:, TB:TB + WINDOW])

    k_s[0:WINDOW, :] = k_s[TB:TB + WINDOW, :]
    vt_s[:, 0:WINDOW] = vt_s[:, TB:TB + WINDOW]

    for c in range(TB // POST_ROWS):
        rs = slice(c * POST_ROWS, (c + 1) * POST_ROWS)
        y_ref[rs, :] = _post(yssm_ref[rs, :], zs_s[rs, :], o_s[rs, :], za_s[rs, :], gs_s[rs, :], ga_s[rs, :],
                             x_ref[rs, :], gate_ref[...], wglu_ref, bglu_ref, wbs_ref, wba_ref, wout_ref,
                             lng_ref, lnb_ref)


def _main_prompt(x2d, yssm, modp, wmain, wvt, biast, sink_rows, eye, wglu, bglu, wbs, wba, wout, lng, lnb,
                 n_batch, seq):
    steps = seq // TB
    row_spec = pl.BlockSpec((TB, D_MODEL), lambda b, i: (b * steps + i, 0))
    mod_spec = lambda j: pl.BlockSpec((None, 1, D_MODEL), lambda b, i, j=j: (b * 3 + j, 0, 0))
    tail_spec = pl.BlockSpec((None, WINDOW, KV_WIDTH), lambda b, i: (b, 0, 0))
    consts = (wmain, wvt, biast, sink_rows, eye, wglu, bglu, wbs, wba, wout, lng, lnb)
    return pl.pallas_call(
        _main_body,
        grid=(n_batch, steps),
        in_specs=[row_spec, row_spec, mod_spec(0), mod_spec(1), mod_spec(2)] + [_const_spec(c.shape) for c in consts],
        out_specs=[row_spec, tail_spec, tail_spec],
        out_shape=[jax.ShapeDtypeStruct((n_batch * seq, D_MODEL), F32),
                   jax.ShapeDtypeStruct((n_batch, WINDOW, KV_WIDTH), F32),
                   jax.ShapeDtypeStruct((n_batch, WINDOW, KV_WIDTH), F32)],
        scratch_shapes=[pltpu.VMEM((TB, D_MODEL), BF16)] * 5
        + [pltpu.VMEM((TB + WINDOW, KV_WIDTH), BF16), pltpu.VMEM((KV_WIDTH, TB + WINDOW), BF16),
           pltpu.VMEM((NQB, KV_GROUP, KV_WIDTH, WINDOW), BF16), pltpu.VMEM((TB, D_MODEL), BF16)],
        compiler_params=pltpu.CompilerParams(dimension_semantics=("arbitrary", "arbitrary"),
                                             vmem_limit_bytes=VMEM_LIMIT),
        name="main_prompt",
    )(x2d, yssm, modp, modp, modp, *consts)


GT = 8


def _sample_ssm_body(u_ref, h0r_ref, h0i_ref, ar_ref, ai_ref, bre_ref, bim_ref, cblk_ref, d_ref,
                     y_ref, hr_ref, hi_ref):
    ub = u_ref[...]
    ys = []
    for a in range(GT):
        ua = ub[:, a * 128:(a + 1) * 128]
        st = slice(a * 512, (a + 1) * 512)
        xr = _dot(ua, bre_ref[a])
        xi = _dot(ua, bim_ref[a])
        h0r = h0r_ref[:, st]
        h0i = h0i_ref[:, st]
        ar = ar_ref[:, st]
        ai = ai_ref[:, st]
        hr = ar * h0r - ai * h0i + xr
        hi = ar * h0i + ai * h0r + xi
        hr_ref[:, st] = hr
        hi_ref[:, st] = hi
        hcat = jnp.concatenate([hr, hi], axis=1).astype(BF16)
        ys.append(_dot_nt(hcat, cblk_ref[a]))
    y_ref[...] = jnp.concatenate(ys, axis=1) + d_ref[...] * ub.astype(F32)


def _sample_ssm(proj_s, h0r, h0i, ar, ai, bre, bim, cblk, d_row):
    rows = proj_s.shape[0]
    ns = SSM_GROUPS * SSM_STATE
    full = lambda shape: pl.BlockSpec(shape, lambda i: (0,) * len(shape))
    return pl.pallas_call(
        _sample_ssm_body,
        grid=(1,),
        in_specs=[pl.BlockSpec((rows, D_MODEL), lambda i: (0, COL_U)),
                  full((rows, ns)), full((rows, ns)), full((1, ns)), full((1, ns)),
                  full(bre.shape), full(bim.shape), full(cblk.shape), full((1, D_MODEL))],
        out_specs=[full((rows, D_MODEL)), full((rows, ns)), full((rows, ns))],
        out_shape=[jax.ShapeDtypeStruct((rows, D_MODEL), F32),
                   jax.ShapeDtypeStruct((rows, ns), F32),
                   jax.ShapeDtypeStruct((rows, ns), F32)],
        compiler_params=pltpu.CompilerParams(vmem_limit_bytes=VMEM_LIMIT),
        name="sample_ssm",
    )(proj_s, h0r, h0i, ar, ai, bre, bim, cblk, d_row)


SB = 16
KROWS = WINDOW + 8


def _sample_attn_body(q_ref, kn_ref, vn_ref, ck_ref, cv_ref, bias_ref, sink_ref, seg_ref, segt_ref,
                      o_ref, nk_ref, nv_ref):
    qf = q_ref[...].astype(F32)
    knf = kn_ref[...].astype(F32)
    vnf = vn_ref[...].astype(F32)
    rowk = lax.broadcasted_iota(jnp.int32, (KROWS, 128), 0)
    roww = lax.broadcasted_iota(jnp.int32, (WINDOW, KV_WIDTH), 0)
    sink = sink_ref[...]
    for b in range(SB):
        kb = ck_ref[b]
        vb = cv_ref[b]
        qrow = qf[b:b + 1]
        kn = jnp.broadcast_to(knf[b:b + 1], (8, KV_WIDTH))
        vn = jnp.broadcast_to(vnf[b:b + 1], (8, KV_WIDTH))
        kall = jnp.concatenate([kb, kn], axis=0)
        vall = jnp.concatenate([vb, vn], axis=0)
        prod = jnp.concatenate([kall * qrow[:, j * KV_WIDTH:(j + 1) * KV_WIDTH] for j in range(KV_GROUP)], axis=1)
        s = _dot(prod.astype(BF16), seg_ref[...]) + bias_ref[...]
        s = jnp.where(rowk <= WINDOW, s, NEG_INF)
        m = jnp.maximum(jnp.max(s, axis=0, keepdims=True), sink)
        e = jnp.exp(s - m)
        den = jnp.sum(e, axis=0, keepdims=True) + jnp.exp(sink - m)
        p = e / den
        pexp = _dot(p.astype(BF16), segt_ref[...])
        v4 = jnp.concatenate([vall] * KV_GROUP, axis=1)
        o_ref[b:b + 1, :] = jnp.sum(pexp * v4, axis=0, keepdims=True)
        last = roww == WINDOW - 1
        nk_ref[b] = jnp.where(last, jnp.broadcast_to(knf[b:b + 1], (WINDOW, KV_WIDTH)), pltpu.roll(kb, WINDOW - 1, 0))
        nv_ref[b] = jnp.where(last, jnp.broadcast_to(vnf[b:b + 1], (WINDOW, KV_WIDTH)), pltpu.roll(vb, WINDOW - 1, 0))


def _sample_attn(proj_s, ck, cv, bias_s, sink_row, seg, segt):
    rows = proj_s.shape[0]
    cache_spec = pl.BlockSpec((SB, WINDOW, KV_WIDTH), lambda i: (i, 0, 0))
    return pl.pallas_call(
        _sample_attn_body,
        grid=(rows // SB,),
        in_specs=[pl.BlockSpec((SB, D_MODEL), lambda i: (i, COL_Q)),
                  pl.BlockSpec((SB, KV_WIDTH), lambda i: (i, COL_K)),
                  pl.BlockSpec((SB, KV_WIDTH), lambda i: (i, COL_V)),
                  cache_spec, cache_spec,
                  pl.BlockSpec(bias_s.shape, lambda i: (0, 0)),
                  pl.BlockSpec(sink_row.shape, lambda i: (0, 0)),
                  pl.BlockSpec(seg.shape, lambda i: (0, 0)),
                  pl.BlockSpec(segt.shape, lambda i: (0, 0))],
        out_specs=[pl.BlockSpec((SB, D_MODEL), lambda i: (i, 0)), cache_spec, cache_spec],
        out_shape=[jax.ShapeDtypeStruct((rows, D_MODEL), F32),
                   jax.ShapeDtypeStruct(ck.shape, F32),
                   jax.ShapeDtypeStruct(cv.shape, F32)],
        compiler_params=pltpu.CompilerParams(vmem_limit_bytes=VMEM_LIMIT),
        name="sample_attn",
    )(proj_s, proj_s, proj_s, ck, cv, bias_s, sink_row, seg, segt)


def _sample_final_body(x_ref, yssm_ref, zs_ref, o_ref, za_ref, gs_ref, ga_ref, gate_ref,
                       wglu_ref, bglu_ref, wbs_ref, wba_ref, wout_ref, lng_ref, lnb_ref, y_ref):
    y_ref[...] = _post(yssm_ref[...], zs_ref[...], o_ref[...], za_ref[...], gs_ref[...], ga_ref[...],
                       x_ref[...], gate_ref[...], wglu_ref, bglu_ref, wbs_ref, wba_ref, wout_ref,
                       lng_ref, lnb_ref)


def _final_sample(x2d, yssm, o_a, proj_s, mods, wglu, bglu, wbs, wba, wout, lng, lnb):
    rows = x2d.shape[0]
    full = lambda shape: pl.BlockSpec(shape, lambda i: (0,) * len(shape))
    pcol = lambda c: pl.BlockSpec((rows, D_MODEL), lambda i, c=c: (0, c))
    return pl.pallas_call(
        _sample_final_body,
        grid=(1,),
        in_specs=[full((rows, D_MODEL)), full((rows, D_MODEL)), pcol(COL_ZS), full((rows, D_MODEL)),
                  pcol(COL_ZA), pcol(COL_GS), pcol(COL_GA),
                  pl.BlockSpec((rows, D_MODEL), lambda i: (0, 2)),
                  full(wglu.shape), full(bglu.shape), full(wbs.shape), full(wba.shape), full(wout.shape),
                  full(lng.shape), full(lnb.shape)],
        out_specs=full((rows, D_MODEL)),
        out_shape=jax.ShapeDtypeStruct((rows, D_MODEL), F32),
        compiler_params=pltpu.CompilerParams(vmem_limit_bytes=VMEM_LIMIT),
        name="final_sample",
    )(x2d, yssm, proj_s, o_a, proj_s, proj_s, proj_s, mods, wglu, bglu, wbs, wba, wout, lng, lnb)


def _bucket_table():
    max_exact = N_BUCKETS // 2
    dist = np.arange(WINDOW + 1)
    df = np.maximum(dist, 1).astype(np.float32)
    large = max_exact + (np.log(df / np.float32(max_exact)) / np.float32(math.log(WINDOW / max_exact))
                         * np.float32(N_BUCKETS - max_exact)).astype(np.int32)
    large = np.minimum(large, N_BUCKETS - 1)
    return np.where(dist < max_exact, dist, large)


POW_TAUS = tuple(range(CHUNK + 1)) + tuple(CHUNK * m for m in range(2, 9))
ROW_CHUNK1 = CHUNK


def _dot3_nt(a, b):
    ah = a.astype(BF16)
    al = (a - ah.astype(F32)).astype(BF16)
    bh = b.astype(BF16)
    bl = (b - bh.astype(F32)).astype(BF16)
    return _dot_nt(ah, bh) + _dot_nt(ah, bl) + _dot_nt(al, bh)


def _ssm_prep_body(p1_ref, p2_ref, kf_ref, bt_ref, bts_ref, c2_ref, c2s_ref, d_ref,
                   lhs_ref, bpow_ref, cpow_ref, coef_ref, bre_ref, bim_ref, cblk_ref):
    lane = lax.broadcasted_iota(jnp.int32, (SSM_GROUP, SW), 1)
    left = lane < SSM_STATE
    sgn = jnp.where(left, 1.0, -1.0)
    lane_blk = lax.broadcasted_iota(jnp.int32, (CW, CW), 1) // SSM_GROUP
    diag = lax.broadcasted_iota(jnp.int32, (CW, CW), 0) == lax.broadcasted_iota(jnp.int32, (CW, CW), 1)
    bre_ref[...] = jnp.zeros_like(bre_ref)
    bim_ref[...] = jnp.zeros_like(bim_ref)
    cblk_ref[...] = jnp.zeros_like(cblk_ref)
    for gl in range(GT):
        p1, p2 = p1_ref[gl], p2_ref[gl]
        k1, k2 = kf_ref[gl, 0:1, :], kf_ref[gl, 1:2, :]
        bt, bts = bt_ref[gl], bts_ref[gl]
        c2, c2s = c2_ref[gl], c2s_ref[gl]
        bb = k1 * bt + k2 * bts
        bbs = k1 * bts - k2 * bt
        ca = [sgn * (c2 * p1[t:t + 1, :] + c2s * p2[t:t + 1, :]) for t in range(CHUNK + 1)]
        cpow_ref[gl] = jnp.concatenate(ca[1:], axis=0).astype(BF16)
        bpow_ref[gl] = jnp.concatenate(
            [p1[CHUNK - 1 - s:CHUNK - s, :] * bb + p2[CHUNK - 1 - s:CHUNK - s, :] * bbs for s in range(CHUNK)],
            axis=0).astype(BF16)
        e = _dot3_nt(jnp.concatenate(ca[:CHUNK], axis=0), jnp.concatenate([bb] * CHUNK, axis=0))
        acc = jnp.where(diag, d_ref[gl], 0.0)
        for s in range(CHUNK):
            sh = e if s == 0 else jnp.concatenate([jnp.zeros((s * SSM_GROUP, CW), F32), e[:CW - s * SSM_GROUP, :]], axis=0)
            acc = acc + jnp.where(lane_blk == s, sh, 0.0)
        lhs_ref[gl] = acc.astype(BF16)
        c1m, c2m = p1[ROW_CHUNK1:ROW_CHUNK1 + 8, :], p2[ROW_CHUNK1:ROW_CHUNK1 + 8, :]
        coef_ref[gl] = jnp.concatenate(
            [c1m[0:1], c2m[0:1], c1m[1:2], c2m[1:2], c1m[3:4], c2m[3:4], c1m[7:8], c2m[7:8], c1m, c2m], axis=0)
        rs = slice(gl * SSM_GROUP, (gl + 1) * SSM_GROUP)
        ts = slice((gl // 2) * SW, (gl // 2 + 1) * SW)
        ti = slice(GT * SSM_STATE + (gl // 2) * SW, GT * SSM_STATE + (gl // 2 + 1) * SW)
        if gl % 2 == 0:
            bre_ref[rs, ts] = jnp.where(left, bb, 0.0).astype(BF16)
            bim_ref[rs, ts] = jnp.where(left, bbs, 0.0).astype(BF16)
            cblk_ref[rs, ts] = jnp.where(left, c2, 0.0).astype(BF16)
            cblk_ref[rs, ti] = jnp.where(left, -c2s, 0.0).astype(BF16)
        else:
            bre_ref[rs, ts] = jnp.where(left, 0.0, bbs).astype(BF16)
            bim_ref[rs, ts] = jnp.where(left, 0.0, bb).astype(BF16)
            cblk_ref[rs, ts] = jnp.where(left, 0.0, c2s).astype(BF16)
            cblk_ref[rs, ti] = jnp.where(left, 0.0, -c2).astype(BF16)


def _ssm_params(lam_re, lam_im, log_delta, b_re, b_im, c_re, c_im, d_skip):
    g = SSM_GROUPS
    dup = lambda a: jnp.concatenate([a, a], axis=-1)
    lr, li = dup(lam_re.astype(F32)), dup(lam_im.astype(F32))
    dt = jnp.exp(log_delta.astype(F32))[:, None]
    half_sign = jnp.asarray(np.where(np.arange(SW) < SSM_STATE, -1.0, 1.0), F32)
    tau = jnp.asarray(POW_TAUS, F32)[None, :, None]
    mag = jnp.exp((lr * dt)[:, None, :] * tau)
    ang = (li * dt)[:, None, :] * tau
    p1 = mag * jnp.cos(ang)
    pim = mag * jnp.sin(ang)
    p2 = pim * half_sign
    ar, ai = p1[:, 1, :], pim[:, 1, :]
    den = lr * lr + li * li
    nr = ar - 1.0
    k1 = (nr * lr + ai * li) / den
    k2 = (ai * lr - nr * li) / den * half_sign
    kf = jnp.concatenate([k1[:, None, :], k2[:, None, :], jnp.zeros((g, 6, SW), F32)], axis=1)
    brt, bit = jnp.swapaxes(b_re, 1, 2), jnp.swapaxes(b_im, 1, 2)
    bt = jnp.concatenate([brt, bit], axis=-1)
    bts = jnp.concatenate([bit, brt], axis=-1)
    c2 = jnp.concatenate([c_re, c_im], axis=-1)
    c2s = jnp.concatenate([c_im, c_re], axis=-1)
    dtile = jnp.tile(d_skip.reshape(g, 1, SSM_GROUP), (1, 1, CHUNK))

    gspec = lambda r, w: pl.BlockSpec((GT, r, w), lambda a: (a, 0, 0))
    tile_spec = lambda w: pl.BlockSpec((None, GT * SSM_GROUP, w), lambda a: (a, 0, 0))
    nrow = len(POW_TAUS)
    lhs, bpow, cpow, coef, bre, bim, cblk = pl.pallas_call(
        _ssm_prep_body,
        grid=(g // GT,),
        in_specs=[gspec(nrow, SW), gspec(nrow, SW), gspec(8, SW), gspec(SSM_GROUP, SW), gspec(SSM_GROUP, SW),
                  gspec(SSM_GROUP, SW), gspec(SSM_GROUP, SW), gspec(1, CW)],
        out_specs=[gspec(CW, CW), gspec(CW, SW), gspec(CW, SW), gspec(24, SW),
                   tile_spec(GT * SSM_STATE), tile_spec(GT * SSM_STATE), tile_spec(2 * GT * SSM_STATE)],
        out_shape=[jax.ShapeDtypeStruct((g, CW, CW), BF16), jax.ShapeDtypeStruct((g, CW, SW), BF16),
                   jax.ShapeDtypeStruct((g, CW, SW), BF16), jax.ShapeDtypeStruct((g, 24, SW), F32),
                   jax.ShapeDtypeStruct((GT, GT * SSM_GROUP, GT * SSM_STATE), BF16),
                   jax.ShapeDtypeStruct((GT, GT * SSM_GROUP, GT * SSM_STATE), BF16),
                   jax.ShapeDtypeStruct((GT, GT * SSM_GROUP, 2 * GT * SSM_STATE), BF16)],
        name="ssm_prep",
    )(p1, p2, kf, bt, bts, c2, c2s, dtile)
    ar_row = ar[:, :SSM_STATE].reshape(1, -1)
    ai_row = ai[:, :SSM_STATE].reshape(1, -1)
    return lhs, bpow, cpow, coef, ar_row, ai_row, bre, bim, cblk


def _attn_params(rel_bias, sinks):
    hp = lax.Precision.HIGHEST
    rb = rel_bias.astype(F32)
    onehot = jnp.asarray(_bucket_table()[:, None] == np.arange(N_BUCKETS)[None, :], F32)
    tbl = jnp.dot(onehot, rb, precision=hp)
    span = 3 * WINDOW
    w = jnp.concatenate([tbl[::-1], jnp.full((span - WINDOW - 1, N_HEADS), NEG_INF, F32)], axis=0)
    wt = jnp.broadcast_to(jnp.transpose(w)[:, None, :], (N_HEADS, WINDOW, span)).reshape(N_HEADS, WINDOW * span)
    bq = wt[:, :WINDOW * (span - 1)].reshape(N_HEADS, WINDOW, span - 1)[:, :, :2 * WINDOW]
    first = jnp.where(jnp.arange(2 * WINDOW)[None, None, :] >= WINDOW, bq, NEG_INF)
    bm = jnp.stack([bq, first], axis=0)
    biast = jnp.transpose(bm.reshape(2, N_KV_HEADS, KV_GROUP, WINDOW, 2 * WINDOW), (0, 1, 4, 2, 3)) \
        .reshape(2, N_KV_HEADS, 2 * WINDOW, KV_GROUP * WINDOW)
    sink_rows = jnp.repeat(sinks.astype(F32).reshape(N_KV_HEADS, 1, KV_GROUP), WINDOW, axis=2)
    ds = np.clip(WINDOW - np.arange(KROWS), 0, WINDOW)
    bs = jnp.dot(jnp.asarray(ds[:, None] == np.arange(WINDOW + 1)[None, :], F32), tbl, precision=hp)
    bs = jnp.transpose(bs.reshape(KROWS, N_KV_HEADS, KV_GROUP), (0, 2, 1)).reshape(KROWS, N_HEADS)
    bias_s = jnp.pad(bs, ((0, 0), (0, 128 - N_HEADS)))
    sk = jnp.transpose(sinks.astype(F32).reshape(N_KV_HEADS, KV_GROUP), (1, 0)).reshape(1, N_HEADS)
    sink_row = jnp.pad(sk, ((0, 0), (0, 128 - N_HEADS)))
    seg_np = np.zeros((D_MODEL, 128), np.float32)
    for j in range(KV_GROUP):
        for kh in range(N_KV_HEADS):
            r0 = j * KV_WIDTH + kh * HEAD_DIM
            seg_np[r0:r0 + HEAD_DIM, j * N_KV_HEADS + kh] = 1.0
    seg = jnp.asarray(seg_np, BF16)
    segt = jnp.asarray(seg_np.T, BF16)
    return biast, sink_rows, bias_s, sink_row, seg, segt


def kernel(x_prompt, x_sample, c_prompt, c_sample, state_ssm_re, state_ssm_im, cache_swa_k, cache_swa_v,
           w_ada, b_ada, w_in, ssm_lambda_re, ssm_lambda_im, ssm_log_delta, ssm_b_re, ssm_b_im,
           ssm_c_re, ssm_c_im, ssm_d, w_glu, b_glu, attn_sinks, rel_bias, w_branch_s, w_branch_a,
           w_out, ln_g, ln_b):
    assert w_ada.shape[0] == 1, "single-layer trunk"
    n_batch, seq, _ = x_prompt.shape
    n_dec = x_sample.shape[0]
    perm = _head_perm()

    wi = w_in[0]
    w_in_b = jnp.concatenate([wi[:, 0:1024], wi[:, 1024:2048], wi[:, 2048:3072][:, perm] * (HEAD_DIM ** -0.5),
                              wi[:, 3584:4608][:, perm], wi[:, 4608:5632], wi[:, 5632:6656],
                              wi[:, 3072:3328], wi[:, 3328:3584]], axis=1).astype(BF16)
    wmain = w_in_b[:, D_MODEL:D_MODEL + W_MAIN]
    wvt = jnp.transpose(wi[:, 3328:3584]).astype(BF16)
    wut = jnp.transpose(wi[:, 0:D_MODEL]).astype(BF16)
    eye = jnp.eye(WINDOW, dtype=BF16)
    wglu = w_glu[0].astype(BF16)
    bglu = b_glu[0].reshape(1, -1).astype(F32)
    wbs = w_branch_s[0].astype(BF16)
    wba = w_branch_a[0][perm, :].astype(BF16)
    wout = w_out[0].astype(BF16)
    lng = ln_g[0].reshape(1, -1).astype(F32)
    lnb = ln_b[0].reshape(1, -1).astype(F32)
    lhs, bpow, cpow, coef, ar_row, ai_row, bre, bim, cblk = _ssm_params(
        ssm_lambda_re[0], ssm_lambda_im[0], ssm_log_delta[0], ssm_b_re[0], ssm_b_im[0],
        ssm_c_re[0], ssm_c_im[0], ssm_d[0])
    biast, sink_rows, bias_s, sink_row, seg, segt = _attn_params(rel_bias, attn_sinks[0])

    c_all = jnp.concatenate([c_prompt, jnp.zeros((8 - n_batch, D_MODEL), F32), c_sample], axis=0)
    mod = _ada(c_all, w_ada[0], b_ada[0])
    modp = mod[:n_batch].reshape(n_batch * 3, 1, D_MODEL)
    mods = mod[8:8 + n_dec]

    xp = x_prompt.reshape(n_batch * seq, D_MODEL)
    yssm, hfin = _ssm_prompt(x_prompt, modp, wut, lhs, bpow, cpow, coef, eye)
    yp, ktail, vtail = _main_prompt(xp, yssm, modp, wmain, wvt, biast, sink_rows, eye,
                                    wglu, bglu, wbs, wba, wout, lng, lnb, n_batch, seq)
    yp = yp.reshape(n_batch, seq, D_MODEL)
    p_hr = hfin[:, :, 0, :SSM_STATE][None]
    p_hi = hfin[:, :, 0, SSM_STATE:][None]
    p_k = ktail.reshape(1, n_batch, WINDOW, N_KV_HEADS, HEAD_DIM)
    p_v = vtail.reshape(1, n_batch, WINDOW, N_KV_HEADS, HEAD_DIM)

    xs = x_sample.reshape(n_dec, D_MODEL)
    proj_s = _proj(xs, mods, mods,
                   pl.BlockSpec((n_dec, D_MODEL), lambda i: (0, 0)),
                   pl.BlockSpec((n_dec, D_MODEL), lambda i: (0, 1)),
                   w_in_b, n_dec, D_IN)
    ns = SSM_GROUPS * SSM_STATE
    ys_s, s_hr, s_hi = _sample_ssm(proj_s, state_ssm_re[0].reshape(n_dec, ns), state_ssm_im[0].reshape(n_dec, ns),
                                   ar_row, ai_row, bre, bim, cblk, ssm_d[0].reshape(1, -1).astype(F32))
    o_s, nk, nv = _sample_attn(proj_s, cache_swa_k[0].reshape(n_dec, WINDOW, KV_WIDTH),
                               cache_swa_v[0].reshape(n_dec, WINDOW, KV_WIDTH), bias_s, sink_row, seg, segt)
    ysmp = _final_sample(xs, ys_s, o_s, proj_s, mods, wglu, bglu, wbs, wba, wout, lng, lnb)

    return (yp, ysmp.reshape(n_dec, 1, D_MODEL),
            p_hr, p_hi, p_k, p_v,
            s_hr.reshape(1, n_dec, SSM_GROUPS, SSM_STATE), s_hi.reshape(1, n_dec, SSM_GROUPS, SSM_STATE),
            nk.reshape(1, n_dec, WINDOW, N_KV_HEADS, HEAD_DIM), nv.reshape(1, n_dec, WINDOW, N_KV_HEADS, HEAD_DIM))
```

```python
import functools
import math

import numpy as np
import jax
import jax.numpy as jnp
from jax import lax
from jax.experimental import pallas as pl
from jax.experimental.pallas import tpu as pltpu

F32 = jnp.float32
BF16 = jnp.bfloat16

D_MODEL = 1024
SSM_GROUPS = 64
SSM_GROUP = 16
SSM_STATE = 64
N_HEADS = 16
HEAD_DIM = 64
N_KV_HEADS = 4
KV_GROUP = 4
KV_WIDTH = N_KV_HEADS * HEAD_DIM
WINDOW = 128
N_BUCKETS = 32
NEG_INF = -1e30
LN_EPS = 1e-5
DEPTH = 1
DEEPNORM_ALPHA = (2 * DEPTH) ** 0.25
D_IN = 6656
CHUNK = 16
CW = CHUNK * SSM_GROUP
SW = 2 * SSM_STATE

COL_U, COL_ZS, COL_Q, COL_ZA, COL_GS, COL_GA = 0, 1, 2, 3, 4, 5
COL_K, COL_V = 24, 25

VMEM_LIMIT = 56 * 1024 * 1024


def _head_perm():
    return np.arange(N_HEADS * HEAD_DIM).reshape(N_KV_HEADS, KV_GROUP, HEAD_DIM).transpose(1, 0, 2).reshape(-1)


def _sigmoid(x):
    return 0.5 * jnp.tanh(0.5 * x) + 0.5


def _silu(x):
    return x * _sigmoid(x)


def _gelu_tanh(x):
    c = math.sqrt(2.0 / math.pi)
    hx = 0.5 * x
    return hx + hx * jnp.tanh(x * (c + (c * 0.044715) * (x * x)))


def _dot(a, b):
    return jnp.dot(a, b, preferred_element_type=F32)


def _dot_nt(a, b):
    return lax.dot_general(a, b, (((1,), (1,)), ((), ())), preferred_element_type=F32)


def _ada_body(c_ref, w_ref, b_ref, o_ref):
    c = c_ref[...]
    sc = _silu(c).astype(BF16)
    o_ref[...] = _dot(sc, w_ref[...].astype(BF16)) + b_ref[...]


def _ada(c_all, w_ada, b_ada):
    rows = c_all.shape[0]
    bn = 512
    return pl.pallas_call(
        _ada_body,
        grid=(3 * D_MODEL // bn,),
        in_specs=[pl.BlockSpec((rows, D_MODEL), lambda j: (0, 0)),
                  pl.BlockSpec((D_MODEL, bn), lambda j: (0, j)),
                  pl.BlockSpec((1, bn), lambda j: (0, j))],
        out_specs=pl.BlockSpec((rows, bn), lambda j: (0, j)),
        out_shape=jax.ShapeDtypeStruct((rows, 3 * D_MODEL), F32),
        name="ada",
    )(c_all, w_ada, b_ada.reshape(1, -1))


PROJ_CB = 512


def _proj_body(x_ref, shift_ref, scale_ref, w_ref, o_ref, *, n_out):
    h = x_ref[...] * (1.0 + scale_ref[...]) + shift_ref[...]
    hb = h.astype(BF16)
    for j in range(n_out // PROJ_CB):
        sl = slice(j * PROJ_CB, (j + 1) * PROJ_CB)
        o_ref[:, sl] = _dot(hb, w_ref[:, sl]).astype(BF16)


def _proj(x2d, shift_arr, scale_arr, shift_spec, scale_spec, w_in_b, tm, n_out):
    rows = x2d.shape[0]
    return pl.pallas_call(
        functools.partial(_proj_body, n_out=n_out),
        grid=(rows // tm,),
        in_specs=[pl.BlockSpec((tm, D_MODEL), lambda i: (i, 0)),
                  shift_spec, scale_spec,
                  pl.BlockSpec((D_MODEL, n_out), lambda i: (0, 0), pipeline_mode=pl.Buffered(1))],
        out_specs=pl.BlockSpec((tm, n_out), lambda i: (i, 0)),
        out_shape=jax.ShapeDtypeStruct((rows, n_out), BF16),
        compiler_params=pltpu.CompilerParams(vmem_limit_bytes=VMEM_LIMIT),
        name="proj",
    )(x2d, shift_arr, scale_arr, w_in_b)


NKS = 128
GC = 16


def _cmul(c1, c2, x):
    return c1 * x + c2 * pltpu.roll(x, SSM_STATE, x.ndim - 1)


SSM_TOK = NKS * CHUNK
SUB = 8
NPH = CHUNK // SUB
GROUP_UNROLL = 4


def _dot_tn(a, b):
    return lax.dot_general(a, b, (((0,), (0,)), ((), ())), preferred_element_type=F32)


SEG = 8
SEG_LEN = NKS // SEG
POW_TAUS = (tuple(range(CHUNK + 1)) + tuple(CHUNK * m for m in range(2, SEG_LEN))
            + tuple(CHUNK * SEG_LEN * m for m in range(1, SEG + 1)) + (0,))
ROW_CHUNK1 = CHUNK
ROW_SEG1 = CHUNK + SEG_LEN - 1


def _ssm_body(x_ref, shift_ref, scale_ref, wut_ref, lhs_ref, bpow_ref, cpow_ref, p1_ref, p2_ref, unperm_ref,
              y_ref, hfin_ref, r_scr, yt_scr, s_scr, carry_scr):
    i = pl.program_id(1)
    j = pl.program_id(2)

    @pl.when((i == 0) & (j == 0))
    def _():
        carry_scr[...] = jnp.zeros_like(carry_scr)

    @pl.when(j < NPH)
    def _():
        for sp in range(SUB // 2):
            h = jnp.concatenate([x_ref[:, b, s, :] for s in (2 * sp, 2 * sp + 1) for b in range(SEG_LEN)],
                                axis=0)
            hb = (h * (1.0 + scale_ref[...]) + shift_ref[...]).astype(BF16)
            ut = _dot_nt(wut_ref[...], hb).astype(BF16)
            ut = ut.reshape(SSM_GROUPS, SSM_GROUP, 2 * NKS)
            pair = jnp.concatenate([ut[:, :, :NKS], ut[:, :, NKS:]], axis=1)
            r0 = pl.multiple_of(j * SUB * SSM_GROUP + sp * 2 * SSM_GROUP, 2 * SSM_GROUP)
            r_scr[:, pl.ds(r0, 2 * SSM_GROUP), :] = pair

    @pl.when(j == NPH - 1)
    def _():
        def local(g, c):
            rg = r_scr[g]
            yt_scr[g] = _dot(lhs_ref[g], rg)
            s_scr[g] = _dot_tn(rg, bpow_ref[g])
            return c

        lax.fori_loop(0, SSM_GROUPS, local, 0, unroll=GROUP_UNROLL)

        row = lax.broadcasted_iota(jnp.int32, (GC, SEG, SW), 1)
        seg_rows = lambda b: slice(b * SEG, (b + 1) * SEG)
        for gc in range(SSM_GROUPS // GC):
            gs = slice(gc * GC, (gc + 1) * GC)
            p1, p2 = p1_ref[gs], p2_ref[gs]
            c1, c2 = p1[:, ROW_CHUNK1:ROW_CHUNK1 + 1, :], p2[:, ROW_CHUNK1:ROW_CHUNK1 + 1, :]
            h = jnp.zeros((GC, SEG, SW), F32)
            hs = h
            for b in range(SEG_LEN):
                sb = s_scr[gs, seg_rows(b), :]
                h, hs = sb + c1 * h + c2 * hs, pltpu.roll(sb, SSM_STATE, 2) + c1 * hs - c2 * h
                s_scr[gs, seg_rows(b), :] = h
            e = h
            for d in (1, 2, 4):
                rd = ROW_SEG1 + d - 1
                sh = jnp.where(row >= d, pltpu.roll(e, d, 1), 0.0)
                e = e + _cmul(p1[:, rd:rd + 1, :], p2[:, rd:rd + 1, :], sh)
            carry = carry_scr[gs]
            e = e + _cmul(p1[:, ROW_SEG1:ROW_SEG1 + SEG, :], p2[:, ROW_SEG1:ROW_SEG1 + SEG, :], carry)
            ein = jnp.where(row == 0, carry, pltpu.roll(e, 1, 1))
            eins = pltpu.roll(ein, SSM_STATE, 2)
            carry_scr[gs] = jnp.broadcast_to(e[:, SEG - 1:SEG, :], (GC, SEG, SW))
            for b in range(SEG_LEN - 1, 0, -1):
                rb = ROW_CHUNK1 + b - 1
                s_scr[gs, seg_rows(b), :] = (s_scr[gs, seg_rows(b - 1), :]
                                             + p1[:, rb:rb + 1, :] * ein + p2[:, rb:rb + 1, :] * eins)
            s_scr[gs, seg_rows(0), :] = ein
        hfin_ref[...] = carry_scr[...]

        def carried(g, c):
            yt_scr[g] += _dot_nt(cpow_ref[g], s_scr[g].astype(BF16))
            return c

        lax.fori_loop(0, SSM_GROUPS, carried, 0, unroll=GROUP_UNROLL)

    @pl.when(j >= NPH)
    def _():
        for sp in range(SUB):
            t0 = pl.multiple_of((j - NPH) * SUB * SSM_GROUP + sp * SSM_GROUP, SSM_GROUP)
            yt = yt_scr[:, pl.ds(t0, SSM_GROUP), :].reshape(D_MODEL, NKS)
            hi = yt.astype(BF16)
            lo = (yt - hi.astype(F32)).astype(BF16)
            y_ref[:, sp, :] = _dot_nt(unperm_ref[...], hi) + _dot_nt(unperm_ref[...], lo)


def _ssm_prompt(x_prompt, modp, wut, lhs, bpow, cpow, p1, p2):
    n_batch, seq, _ = x_prompt.shape
    steps = seq // SSM_TOK
    xv = x_prompt.reshape(n_batch * steps * SEG, SEG_LEN, CHUNK, D_MODEL)
    mod_spec = lambda k: pl.BlockSpec((None, 1, D_MODEL), lambda b, i, j, k=k: (b * 3 + k, 0, 0))
    k = np.arange(NKS)
    perm_np = np.zeros((NKS, NKS), np.float32)
    perm_np[k, (k % SEG_LEN) * SEG + k // SEG_LEN] = 1.0
    consts = (wut, lhs, bpow, cpow, p1, p2, jnp.asarray(perm_np, BF16))
    y, hfin = pl.pallas_call(
        _ssm_body,
        grid=(n_batch, steps, 2 * NPH),
        in_specs=[pl.BlockSpec((SEG, SEG_LEN, SUB, D_MODEL),
                               lambda b, i, j: (b * steps + i, 0, jnp.minimum(j, NPH - 1), 0),
                               pipeline_mode=pl.Buffered(1)),
                  mod_spec(0), mod_spec(1)] + [_const_spec(c.shape) for c in consts],
        out_specs=[pl.BlockSpec((NKS, SUB, D_MODEL), lambda b, i, j: (b * steps + i, jnp.maximum(j - NPH, 0), 0)),
                   pl.BlockSpec((None, SSM_GROUPS, 8, SW), lambda b, i, j: (b, 0, 0, 0))],
        out_shape=[jax.ShapeDtypeStruct((n_batch * seq // CHUNK, CHUNK, D_MODEL), F32),
                   jax.ShapeDtypeStruct((n_batch, SSM_GROUPS, 8, SW), F32)],
        scratch_shapes=[pltpu.VMEM((SSM_GROUPS, CW, NKS), BF16), pltpu.VMEM((SSM_GROUPS, CW, NKS), F32),
                        pltpu.VMEM((SSM_GROUPS, NKS, SW), F32), pltpu.VMEM((SSM_GROUPS, 8, SW), F32)],
        compiler_params=pltpu.CompilerParams(dimension_semantics=("arbitrary", "arbitrary", "arbitrary"),
                                             vmem_limit_bytes=VMEM_LIMIT),
        name="ssm_prompt",
    )(xv, modp, modp, *consts)
    return y.reshape(n_batch * seq, D_MODEL), hfin


def _post(ys, zs, o_a, za, gs, ga, x, gate, wglu_ref, bglu_ref, wbs_ref, wba_ref, wout_ref, lng_ref, lnb_ref):
    ys = _gelu_tanh(ys)
    glu = ys * _sigmoid(_dot(ys.astype(BF16), wglu_ref[...]) + bglu_ref[...])
    t = glu.astype(BF16) * _silu(zs)
    b_s = _dot(t, wbs_ref[...])
    oz = o_a.astype(BF16) * _silu(za)
    b_a = _dot(oz, wba_ref[...])
    m = _sigmoid(gs).astype(F32) * b_s + _sigmoid(ga).astype(F32) * b_a
    out = _dot(m.astype(BF16), wout_ref[...])
    r = DEEPNORM_ALPHA * x + gate * out
    mu = jnp.mean(r, axis=-1, keepdims=True)
    rc = r - mu
    var = jnp.mean(rc * rc, axis=-1, keepdims=True)
    return rc * lax.rsqrt(var + LN_EPS) * lng_ref[...] + lnb_ref[...]


TB = 512
NQB = TB // WINDOW
W_MAIN = 5 * D_MODEL + KV_WIDTH
ONES_ROWS = 16
POST_ROWS = 256


def _const_spec(shape):
    nd = len(shape)
    return pl.BlockSpec(shape, lambda *_: (0,) * nd, pipeline_mode=pl.Buffered(1))


def _main_body(x_ref, yssm_ref, shift_ref, scale_ref, gate_ref, wmain_ref, wvt_ref, biast_ref, sink_ref, eye_ref,
               wglu_ref, bglu_ref, wbs_ref, wba_ref, wout_ref, lng_ref, lnb_ref,
               y_ref, ktail_ref, vtail_ref,
               zs_s, q_s, za_s, gs_s, ga_s, k_s, vt_s, ot_s, o_s):
    i = pl.program_id(1)
    x = x_ref[...]
    hb = (x * (1.0 + scale_ref[...]) + shift_ref[...]).astype(BF16)

    def project(dst, c0, c):
        dst[:, c * 512:(c + 1) * 512] = _dot(hb, wmain_ref[:, c0 + c * 512:c0 + (c + 1) * 512]).astype(BF16)

    for c in range(2):
        project(q_s, D_MODEL, c)
    deferred = [(dst, c0, c) for dst, c0 in ((zs_s, 0), (za_s, 2 * D_MODEL), (gs_s, 3 * D_MODEL), (ga_s, 4 * D_MODEL))
                for c in range(2)]

    @pl.when(i == 0)
    def _():
        k_s[0:WINDOW, :] = jnp.zeros((WINDOW, KV_WIDTH), BF16)
        vt_s[:, 0:WINDOW] = jnp.zeros((KV_WIDTH, WINDOW), BF16)

    k_s[WINDOW:WINDOW + TB, :] = _dot(hb, wmain_ref[:, 5 * D_MODEL:W_MAIN]).astype(BF16)
    vt_s[:, WINDOW:WINDOW + TB] = _dot_nt(wvt_ref[...], hb).astype(BF16)

    first = jnp.where(i == 0, 1, 0)
    lane = lax.broadcasted_iota(jnp.int32, (2 * WINDOW, 128), 1)
    ones = jnp.ones((ONES_ROWS, 2 * WINDOW), BF16)
    for blk in range(NQB):
        rows = slice(blk * WINDOW, (blk + 1) * WINDOW)
        kcat = k_s[blk * WINDOW:blk * WINDOW + 2 * WINDOW, :]
        vtc = vt_s[:, blk * WINDOW:blk * WINDOW + 2 * WINDOW]
        sel = first if blk == 0 else 0
        heads = range(N_KV_HEADS)
        sts = []
        for kh in heads:
            pair, odd = kh // 2, kh % 2
            kcm = jnp.where((lane >= HEAD_DIM) == bool(odd), kcat[:, pair * 128:(pair + 1) * 128], jnp.zeros((), BF16))
            qp = jnp.concatenate(
                [q_s[rows, j * KV_WIDTH + pair * 128:j * KV_WIDTH + (pair + 1) * 128] for j in range(KV_GROUP)],
                axis=0)
            sts.append(_dot_nt(kcm, qp) + biast_ref[sel, kh])
        ms = [jnp.maximum(jnp.max(sts[kh], axis=0, keepdims=True), sink_ref[kh]) for kh in heads]
        es = [jnp.exp(sts[kh] - ms[kh]).astype(BF16) for kh in heads]
        rs = [_dot(jnp.concatenate([vtc[kh * HEAD_DIM:(kh + 1) * HEAD_DIM, :], ones], axis=0), es[kh])
              for kh in heads]
        for kh in heads:
            den = rs[kh][HEAD_DIM:HEAD_DIM + 1, :] + jnp.exp(sink_ref[kh] - ms[kh])
            ot = (rs[kh][:HEAD_DIM, :] / den).astype(BF16)
            for j in range(KV_GROUP):
                ot_s[blk, j, kh * HEAD_DIM:(kh + 1) * HEAD_DIM, :] = ot[:, j * WINDOW:(j + 1) * WINDOW]
        for spec in deferred[blk * len(deferred) // NQB:(blk + 1) * len(deferred) // NQB]:
            project(*spec)
        for j in range(KV_GROUP):
            o_s[rows, j * KV_WIDTH:(j + 1) * KV_WIDTH] = _dot_nt(eye_ref[...], ot_s[blk, j]).astype(BF16)

    @pl.when(i == pl.num_programs(1) - 1)
    def _():
        ktail_ref[...] = k_s[TB:TB + WINDOW, :].astype(F32)
        vtail_ref[...] = _dot_nt(eye_ref[...], vt_s[:, TB:TB + WINDOW])

    k_s[0:WINDOW, :] = k_s[TB:TB + WINDOW, :]
    vt_s[:, 0:WINDOW] = vt_s[:, TB:TB + WINDOW]

    for c in range(TB // POST_ROWS):
        rs = slice(c * POST_ROWS, (c + 1) * POST_ROWS)
        y_ref[rs, :] = _post(yssm_ref[rs, :], zs_s[rs, :], o_s[rs, :], za_s[rs, :], gs_s[rs, :], ga_s[rs, :],
                             x_ref[rs, :], gate_ref[...], wglu_ref, bglu_ref, wbs_ref, wba_ref, wout_ref,
                             lng_ref, lnb_ref)


def _main_prompt(x2d, yssm, modp, wmain, wvt, biast, sink_rows, eye, wglu, bglu, wbs, wba, wout, lng, lnb,
                 n_batch, seq):
    steps = seq // TB
    row_spec = pl.BlockSpec((TB, D_MODEL), lambda b, i: (b * steps + i, 0))
    mod_spec = lambda j: pl.BlockSpec((None, 1, D_MODEL), lambda b, i, j=j: (b * 3 + j, 0, 0))
    tail_spec = pl.BlockSpec((None, WINDOW, KV_WIDTH), lambda b, i: (b, 0, 0))
    consts = (wmain, wvt, biast, sink_rows, eye, wglu, bglu, wbs, wba, wout, lng, lnb)
    return pl.pallas_call(
        _main_body,
        grid=(n_batch, steps),
        in_specs=[row_spec, row_spec, mod_spec(0), mod_spec(1), mod_spec(2)] + [_const_spec(c.shape) for c in consts],
        out_specs=[row_spec, tail_spec, tail_spec],
        out_shape=[jax.ShapeDtypeStruct((n_batch * seq, D_MODEL), F32),
                   jax.ShapeDtypeStruct((n_batch, WINDOW, KV_WIDTH), F32),
                   jax.ShapeDtypeStruct((n_batch, WINDOW, KV_WIDTH), F32)],
        scratch_shapes=[pltpu.VMEM((TB, D_MODEL), BF16)] * 5
        + [pltpu.VMEM((TB + WINDOW, KV_WIDTH), BF16), pltpu.VMEM((KV_WIDTH, TB + WINDOW), BF16),
           pltpu.VMEM((NQB, KV_GROUP, KV_WIDTH, WINDOW), BF16), pltpu.VMEM((TB, D_MODEL), BF16)],
        compiler_params=pltpu.CompilerParams(dimension_semantics=("arbitrary", "arbitrary"),
                                             vmem_limit_bytes=VMEM_LIMIT),
        name="main_prompt",
    )(x2d, yssm, modp, modp, modp, *consts)


GT = 8


def _sample_ssm_body(u_ref, h0r_ref, h0i_ref, ar_ref, ai_ref, bre_ref, bim_ref, cblk_ref, d_ref,
                     y_ref, hr_ref, hi_ref):
    ub = u_ref[...]
    ys = []
    for a in range(GT):
        ua = ub[:, a * 128:(a + 1) * 128]
        st = slice(a * 512, (a + 1) * 512)
        xr = _dot(ua, bre_ref[a])
        xi = _dot(ua, bim_ref[a])
        h0r = h0r_ref[:, st]
        h0i = h0i_ref[:, st]
        ar = ar_ref[:, st]
        ai = ai_ref[:, st]
        hr = ar * h0r - ai * h0i + xr
        hi = ar * h0i + ai * h0r + xi
        hr_ref[:, st] = hr
        hi_ref[:, st] = hi
        hcat = jnp.concatenate([hr, hi], axis=1).astype(BF16)
        ys.append(_dot_nt(hcat, cblk_ref[a]))
    y_ref[...] = jnp.concatenate(ys, axis=1) + d_ref[...] * ub.astype(F32)


def _sample_ssm(proj_s, h0r, h0i, ar, ai, bre, bim, cblk, d_row):
    rows = proj_s.shape[0]
    ns = SSM_GROUPS * SSM_STATE
    full = lambda shape: pl.BlockSpec(shape, lambda i: (0,) * len(shape))
    return pl.pallas_call(
        _sample_ssm_body,
        grid=(1,),
        in_specs=[pl.BlockSpec((rows, D_MODEL), lambda i: (0, COL_U)),
                  full((rows, ns)), full((rows, ns)), full((1, ns)), full((1, ns)),
                  full(bre.shape), full(bim.shape), full(cblk.shape), full((1, D_MODEL))],
        out_specs=[full((rows, D_MODEL)), full((rows, ns)), full((rows, ns))],
        out_shape=[jax.ShapeDtypeStruct((rows, D_MODEL), F32),
                   jax.ShapeDtypeStruct((rows, ns), F32),
                   jax.ShapeDtypeStruct((rows, ns), F32)],
        compiler_params=pltpu.CompilerParams(vmem_limit_bytes=VMEM_LIMIT),
        name="sample_ssm",
    )(proj_s, h0r, h0i, ar, ai, bre, bim, cblk, d_row)


SB = 16
KROWS = WINDOW + 8


def _sample_attn_body(q_ref, kn_ref, vn_ref, ck_ref, cv_ref, bias_ref, sink_ref, seg_ref, segt_ref,
                      o_ref, nk_ref, nv_ref):
    qf = q_ref[...].astype(F32)
    knf = kn_ref[...].astype(F32)
    vnf = vn_ref[...].astype(F32)
    rowk = lax.broadcasted_iota(jnp.int32, (KROWS, 128), 0)
    roww = lax.broadcasted_iota(jnp.int32, (WINDOW, KV_WIDTH), 0)
    sink = sink_ref[...]
    for b in range(SB):
        kb = ck_ref[b]
        vb = cv_ref[b]
        qrow = qf[b:b + 1]
        kn = jnp.broadcast_to(knf[b:b + 1], (8, KV_WIDTH))
        vn = jnp.broadcast_to(vnf[b:b + 1], (8, KV_WIDTH))
        kall = jnp.concatenate([kb, kn], axis=0)
        vall = jnp.concatenate([vb, vn], axis=0)
        prod = jnp.concatenate([kall * qrow[:, j * KV_WIDTH:(j + 1) * KV_WIDTH] for j in range(KV_GROUP)], axis=1)
        s = _dot(prod.astype(BF16), seg_ref[...]) + bias_ref[...]
        s = jnp.where(rowk <= WINDOW, s, NEG_INF)
        m = jnp.maximum(jnp.max(s, axis=0, keepdims=True), sink)
        e = jnp.exp(s - m)
        den = jnp.sum(e, axis=0, keepdims=True) + jnp.exp(sink - m)
        p = e / den
        pexp = _dot(p.astype(BF16), segt_ref[...])
        v4 = jnp.concatenate([vall] * KV_GROUP, axis=1)
        o_ref[b:b + 1, :] = jnp.sum(pexp * v4, axis=0, keepdims=True)
        last = roww == WINDOW - 1
        nk_ref[b] = jnp.where(last, jnp.broadcast_to(knf[b:b + 1], (WINDOW, KV_WIDTH)), pltpu.roll(kb, WINDOW - 1, 0))
        nv_ref[b] = jnp.where(last, jnp.broadcast_to(vnf[b:b + 1], (WINDOW, KV_WIDTH)), pltpu.roll(vb, WINDOW - 1, 0))


def _sample_attn(proj_s, ck, cv, bias_s, sink_row, seg, segt):
    rows = proj_s.shape[0]
    cache_spec = pl.BlockSpec((SB, WINDOW, KV_WIDTH), lambda i: (i, 0, 0))
    return pl.pallas_call(
        _sample_attn_body,
        grid=(rows // SB,),
        in_specs=[pl.BlockSpec((SB, D_MODEL), lambda i: (i, COL_Q)),
                  pl.BlockSpec((SB, KV_WIDTH), lambda i: (i, COL_K)),
                  pl.BlockSpec((SB, KV_WIDTH), lambda i: (i, COL_V)),
                  cache_spec, cache_spec,
                  pl.BlockSpec(bias_s.shape, lambda i: (0, 0)),
                  pl.BlockSpec(sink_row.shape, lambda i: (0, 0)),
                  pl.BlockSpec(seg.shape, lambda i: (0, 0)),
                  pl.BlockSpec(segt.shape, lambda i: (0, 0))],
        out_specs=[pl.BlockSpec((SB, D_MODEL), lambda i: (i, 0)), cache_spec, cache_spec],
        out_shape=[jax.ShapeDtypeStruct((rows, D_MODEL), F32),
                   jax.ShapeDtypeStruct(ck.shape, F32),
                   jax.ShapeDtypeStruct(cv.shape, F32)],
        compiler_params=pltpu.CompilerParams(vmem_limit_bytes=VMEM_LIMIT),
        name="sample_attn",
    )(proj_s, proj_s, proj_s, ck, cv, bias_s, sink_row, seg, segt)


def _sample_final_body(x_ref, yssm_ref, zs_ref, o_ref, za_ref, gs_ref, ga_ref, gate_ref,
                       wglu_ref, bglu_ref, wbs_ref, wba_ref, wout_ref, lng_ref, lnb_ref, y_ref):
    y_ref[...] = _post(yssm_ref[...], zs_ref[...], o_ref[...], za_ref[...], gs_ref[...], ga_ref[...],
                       x_ref[...], gate_ref[...], wglu_ref, bglu_ref, wbs_ref, wba_ref, wout_ref,
                       lng_ref, lnb_ref)


def _final_sample(x2d, yssm, o_a, proj_s, mods, wglu, bglu, wbs, wba, wout, lng, lnb):
    rows = x2d.shape[0]
    full = lambda shape: pl.BlockSpec(shape, lambda i: (0,) * len(shape))
    pcol = lambda c: pl.BlockSpec((rows, D_MODEL), lambda i, c=c: (0, c))
    return pl.pallas_call(
        _sample_final_body,
        grid=(1,),
        in_specs=[full((rows, D_MODEL)), full((rows, D_MODEL)), pcol(COL_ZS), full((rows, D_MODEL)),
                  pcol(COL_ZA), pcol(COL_GS), pcol(COL_GA),
                  pl.BlockSpec((rows, D_MODEL), lambda i: (0, 2)),
                  full(wglu.shape), full(bglu.shape), full(wbs.shape), full(wba.shape), full(wout.shape),
                  full(lng.shape), full(lnb.shape)],
        out_specs=full((rows, D_MODEL)),
        out_shape=jax.ShapeDtypeStruct((rows, D_MODEL), F32),
        compiler_params=pltpu.CompilerParams(vmem_limit_bytes=VMEM_LIMIT),
        name="final_sample",
    )(x2d, yssm, proj_s, o_a, proj_s, proj_s, proj_s, mods, wglu, bglu, wbs, wba, wout, lng, lnb)


def _bucket_table():
    max_exact = N_BUCKETS // 2
    dist = np.arange(WINDOW + 1)
    df = np.maximum(dist, 1).astype(np.float32)
    large = max_exact + (np.log(df / np.float32(max_exact)) / np.float32(math.log(WINDOW / max_exact))
                         * np.float32(N_BUCKETS - max_exact)).astype(np.int32)
    large = np.minimum(large, N_BUCKETS - 1)
    return np.where(dist < max_exact, dist, large)


def _dot3_nt(a, b):
    ah = a.astype(BF16)
    al = (a - ah.astype(F32)).astype(BF16)
    bh = b.astype(BF16)
    bl = (b - bh.astype(F32)).astype(BF16)
    return _dot_nt(ah, bh) + _dot_nt(ah, bl) + _dot_nt(al, bh)


def _ssm_prep_body(p1_ref, p2_ref, kf_ref, bt_ref, bts_ref, c2_ref, c2s_ref, d_ref,
                   lhs_ref, bpow_ref, cpow_ref, bre_ref, bim_ref, cblk_ref):
    lane = lax.broadcasted_iota(jnp.int32, (SSM_GROUP, SW), 1)
    left = lane < SSM_STATE
    sgn = jnp.where(left, 1.0, -1.0)
    lane_blk = lax.broadcasted_iota(jnp.int32, (CW, CW), 1) // SSM_GROUP
    diag = lax.broadcasted_iota(jnp.int32, (CW, CW), 0) == lax.broadcasted_iota(jnp.int32, (CW, CW), 1)
    bre_ref[...] = jnp.zeros_like(bre_ref)
    bim_ref[...] = jnp.zeros_like(bim_ref)
    cblk_ref[...] = jnp.zeros_like(cblk_ref)
    for gl in range(GT):
        p1, p2 = p1_ref[gl], p2_ref[gl]
        k1, k2 = kf_ref[gl, 0:1, :], kf_ref[gl, 1:2, :]
        bt, bts = bt_ref[gl], bts_ref[gl]
        c2, c2s = c2_ref[gl], c2s_ref[gl]
        bb = k1 * bt + k2 * bts
        bbs = k1 * bts - k2 * bt
        ca = [sgn * (c2 * p1[t:t + 1, :] + c2s * p2[t:t + 1, :]) for t in range(CHUNK + 1)]
        cpow_ref[gl] = jnp.concatenate(ca[1:], axis=0).astype(BF16)
        bpow_ref[gl] = jnp.concatenate(
            [p1[CHUNK - 1 - s:CHUNK - s, :] * bb + p2[CHUNK - 1 - s:CHUNK - s, :] * bbs for s in range(CHUNK)],
            axis=0).astype(BF16)
        e = _dot3_nt(jnp.concatenate(ca[:CHUNK], axis=0), jnp.concatenate([bb] * CHUNK, axis=0))
        acc = jnp.where(diag, d_ref[gl], 0.0)
        for s in range(CHUNK):
            sh = e if s == 0 else jnp.concatenate([jnp.zeros((s * SSM_GROUP, CW), F32), e[:CW - s * SSM_GROUP, :]], axis=0)
            acc = acc + jnp.where(lane_blk == s, sh, 0.0)
        lhs_ref[gl] = acc.astype(BF16)
        rs = slice(gl * SSM_GROUP, (gl + 1) * SSM_GROUP)
        ts = slice((gl // 2) * SW, (gl // 2 + 1) * SW)
        ti = slice(GT * SSM_STATE + (gl // 2) * SW, GT * SSM_STATE + (gl // 2 + 1) * SW)
        if gl % 2 == 0:
            bre_ref[rs, ts] = jnp.where(left, bb, 0.0).astype(BF16)
            bim_ref[rs, ts] = jnp.where(left, bbs, 0.0).astype(BF16)
            cblk_ref[rs, ts] = jnp.where(left, c2, 0.0).astype(BF16)
            cblk_ref[rs, ti] = jnp.where(left, -c2s, 0.0).astype(BF16)
        else:
            bre_ref[rs, ts] = jnp.where(left, 0.0, bbs).astype(BF16)
            bim_ref[rs, ts] = jnp.where(left, 0.0, bb).astype(BF16)
            cblk_ref[rs, ts] = jnp.where(left, 0.0, c2s).astype(BF16)
            cblk_ref[rs, ti] = jnp.where(left, 0.0, -c2).astype(BF16)


def _ssm_params(lam_re, lam_im, log_delta, b_re, b_im, c_re, c_im, d_skip):
    g = SSM_GROUPS
    dup = lambda a: jnp.concatenate([a, a], axis=-1)
    lr, li = dup(lam_re.astype(F32)), dup(lam_im.astype(F32))
    dt = jnp.exp(log_delta.astype(F32))[:, None]
    half_sign = jnp.asarray(np.where(np.arange(SW) < SSM_STATE, -1.0, 1.0), F32)
    tau = jnp.asarray(POW_TAUS, F32)[None, :, None]
    mag = jnp.exp((lr * dt)[:, None, :] * tau)
    ang = (li * dt)[:, None, :] * tau
    p1 = mag * jnp.cos(ang)
    pim = mag * jnp.sin(ang)
    p2 = pim * half_sign
    ar, ai = p1[:, 1, :], pim[:, 1, :]
    den = lr * lr + li * li
    nr = ar - 1.0
    k1 = (nr * lr + ai * li) / den
    k2 = (ai * lr - nr * li) / den * half_sign
    kf = jnp.concatenate([k1[:, None, :], k2[:, None, :], jnp.zeros((g, 6, SW), F32)], axis=1)
    brt, bit = jnp.swapaxes(b_re, 1, 2), jnp.swapaxes(b_im, 1, 2)
    bt = jnp.concatenate([brt, bit], axis=-1)
    bts = jnp.concatenate([bit, brt], axis=-1)
    c2 = jnp.concatenate([c_re, c_im], axis=-1)
    c2s = jnp.concatenate([c_im, c_re], axis=-1)
    dtile = jnp.tile(d_skip.reshape(g, 1, SSM_GROUP), (1, 1, CHUNK))

    gspec = lambda r, w: pl.BlockSpec((GT, r, w), lambda a: (a, 0, 0))
    tile_spec = lambda w: pl.BlockSpec((None, GT * SSM_GROUP, w), lambda a: (a, 0, 0))
    nrow = len(POW_TAUS)
    lhs, bpow, cpow, bre, bim, cblk = pl.pallas_call(
        _ssm_prep_body,
        grid=(g // GT,),
        in_specs=[gspec(nrow, SW), gspec(nrow, SW), gspec(8, SW), gspec(SSM_GROUP, SW), gspec(SSM_GROUP, SW),
                  gspec(SSM_GROUP, SW), gspec(SSM_GROUP, SW), gspec(1, CW)],
        out_specs=[gspec(CW, CW), gspec(CW, SW), gspec(CW, SW),
                   tile_spec(GT * SSM_STATE), tile_spec(GT * SSM_STATE), tile_spec(2 * GT * SSM_STATE)],
        out_shape=[jax.ShapeDtypeStruct((g, CW, CW), BF16), jax.ShapeDtypeStruct((g, CW, SW), BF16),
                   jax.ShapeDtypeStruct((g, CW, SW), BF16),
                   jax.ShapeDtypeStruct((GT, GT * SSM_GROUP, GT * SSM_STATE), BF16),
                   jax.ShapeDtypeStruct((GT, GT * SSM_GROUP, GT * SSM_STATE), BF16),
                   jax.ShapeDtypeStruct((GT, GT * SSM_GROUP, 2 * GT * SSM_STATE), BF16)],
        name="ssm_prep",
    )(p1, p2, kf, bt, bts, c2, c2s, dtile)
    ar_row = ar[:, :SSM_STATE].reshape(1, -1)
    ai_row = ai[:, :SSM_STATE].reshape(1, -1)
    return lhs, bpow, cpow, p1, p2, ar_row, ai_row, bre, bim, cblk


def _attn_params(rel_bias, sinks):
    hp = lax.Precision.HIGHEST
    rb = rel_bias.astype(F32)
    onehot = jnp.asarray(_bucket_table()[:, None] == np.arange(N_BUCKETS)[None, :], F32)
    tbl = jnp.dot(onehot, rb, precision=hp)
    span = 3 * WINDOW
    w = jnp.concatenate([tbl[::-1], jnp.full((span - WINDOW - 1, N_HEADS), NEG_INF, F32)], axis=0)
    wt = jnp.broadcast_to(jnp.transpose(w)[:, None, :], (N_HEADS, WINDOW, span)).reshape(N_HEADS, WINDOW * span)
    bq = wt[:, :WINDOW * (span - 1)].reshape(N_HEADS, WINDOW, span - 1)[:, :, :2 * WINDOW]
    first = jnp.where(jnp.arange(2 * WINDOW)[None, None, :] >= WINDOW, bq, NEG_INF)
    bm = jnp.stack([bq, first], axis=0)
    biast = jnp.transpose(bm.reshape(2, N_KV_HEADS, KV_GROUP, WINDOW, 2 * WINDOW), (0, 1, 4, 2, 3)) \
        .reshape(2, N_KV_HEADS, 2 * WINDOW, KV_GROUP * WINDOW)
    sink_rows = jnp.repeat(sinks.astype(F32).reshape(N_KV_HEADS, 1, KV_GROUP), WINDOW, axis=2)
    ds = np.clip(WINDOW - np.arange(KROWS), 0, WINDOW)
    bs = jnp.dot(jnp.asarray(ds[:, None] == np.arange(WINDOW + 1)[None, :], F32), tbl, precision=hp)
    bs = jnp.transpose(bs.reshape(KROWS, N_KV_HEADS, KV_GROUP), (0, 2, 1)).reshape(KROWS, N_HEADS)
    bias_s = jnp.pad(bs, ((0, 0), (0, 128 - N_HEADS)))
    sk = jnp.transpose(sinks.astype(F32).reshape(N_KV_HEADS, KV_GROUP), (1, 0)).reshape(1, N_HEADS)
    sink_row = jnp.pad(sk, ((0, 0), (0, 128 - N_HEADS)))
    seg_np = np.zeros((D_MODEL, 128), np.float32)
    for j in range(KV_GROUP):
        for kh in range(N_KV_HEADS):
            r0 = j * KV_WIDTH + kh * HEAD_DIM
            seg_np[r0:r0 + HEAD_DIM, j * N_KV_HEADS + kh] = 1.0
    seg = jnp.asarray(seg_np, BF16)
    segt = jnp.asarray(seg_np.T, BF16)
    return biast, sink_rows, bias_s, sink_row, seg, segt


def kernel(x_prompt, x_sample, c_prompt, c_sample, state_ssm_re, state_ssm_im, cache_swa_k, cache_swa_v,
           w_ada, b_ada, w_in, ssm_lambda_re, ssm_lambda_im, ssm_log_delta, ssm_b_re, ssm_b_im,
           ssm_c_re, ssm_c_im, ssm_d, w_glu, b_glu, attn_sinks, rel_bias, w_branch_s, w_branch_a,
           w_out, ln_g, ln_b):
    assert w_ada.shape[0] == 1, "single-layer trunk"
    n_batch, seq, _ = x_prompt.shape
    n_dec = x_sample.shape[0]
    perm = _head_perm()

    wi = w_in[0]
    w_in_b = jnp.concatenate([wi[:, 0:1024], wi[:, 1024:2048], wi[:, 2048:3072][:, perm] * (HEAD_DIM ** -0.5),
                              wi[:, 3584:4608][:, perm], wi[:, 4608:5632], wi[:, 5632:6656],
                              wi[:, 3072:3328], wi[:, 3328:3584]], axis=1).astype(BF16)
    wmain = w_in_b[:, D_MODEL:D_MODEL + W_MAIN]
    wvt = jnp.transpose(wi[:, 3328:3584]).astype(BF16)
    wut = jnp.transpose(wi[:, 0:D_MODEL]).astype(BF16)
    eye = jnp.eye(WINDOW, dtype=BF16)
    wglu = w_glu[0].astype(BF16)
    bglu = b_glu[0].reshape(1, -1).astype(F32)
    wbs = w_branch_s[0].astype(BF16)
    wba = w_branch_a[0][perm, :].astype(BF16)
    wout = w_out[0].astype(BF16)
    lng = ln_g[0].reshape(1, -1).astype(F32)
    lnb = ln_b[0].reshape(1, -1).astype(F32)
    lhs, bpow, cpow, p1, p2, ar_row, ai_row, bre, bim, cblk = _ssm_params(
        ssm_lambda_re[0], ssm_lambda_im[0], ssm_log_delta[0], ssm_b_re[0], ssm_b_im[0],
        ssm_c_re[0], ssm_c_im[0], ssm_d[0])
    biast, sink_rows, bias_s, sink_row, seg, segt = _attn_params(rel_bias, attn_sinks[0])

    c_all = jnp.concatenate([c_prompt, jnp.zeros((8 - n_batch, D_MODEL), F32), c_sample], axis=0)
    mod = _ada(c_all, w_ada[0], b_ada[0])
    modp = mod[:n_batch].reshape(n_batch * 3, 1, D_MODEL)
    mods = mod[8:8 + n_dec]

    xp = x_prompt.reshape(n_batch * seq, D_MODEL)
    yssm, hfin = _ssm_prompt(x_prompt, modp, wut, lhs, bpow, cpow, p1, p2)
    yp, ktail, vtail = _main_prompt(xp, yssm, modp, wmain, wvt, biast, sink_rows, eye,
                                    wglu, bglu, wbs, wba, wout, lng, lnb, n_batch, seq)
    yp = yp.reshape(n_batch, seq, D_MODEL)
    p_hr = hfin[:, :, 0, :SSM_STATE][None]
    p_hi = hfin[:, :, 0, SSM_STATE:][None]
    p_k = ktail.reshape(1, n_batch, WINDOW, N_KV_HEADS, HEAD_DIM)
    p_v = vtail.reshape(1, n_batch, WINDOW, N_KV_HEADS, HEAD_DIM)

    xs = x_sample.reshape(n_dec, D_MODEL)
    proj_s = _proj(xs, mods, mods,
                   pl.BlockSpec((n_dec, D_MODEL), lambda i: (0, 0)),
                   pl.BlockSpec((n_dec, D_MODEL), lambda i: (0, 1)),
                   w_in_b, n_dec, D_IN)
    ns = SSM_GROUPS * SSM_STATE
    ys_s, s_hr, s_hi = _sample_ssm(proj_s, state_ssm_re[0].reshape(n_dec, ns), state_ssm_im[0].reshape(n_dec, ns),
                                   ar_row, ai_row, bre, bim, cblk, ssm_d[0].reshape(1, -1).astype(F32))
    o_s, nk, nv = _sample_attn(proj_s, cache_swa_k[0].reshape(n_dec, WINDOW, KV_WIDTH),
                               cache_swa_v[0].reshape(n_dec, WINDOW, KV_WIDTH), bias_s, sink_row, seg, segt)
    ysmp = _final_sample(xs, ys_s, o_s, proj_s, mods, wglu, bglu, wbs, wba, wout, lng, lnb)

    return (yp, ysmp.reshape(n_dec, 1, D_MODEL),
            p_hr, p_hi, p_k, p_v,
            s_hr.reshape(1, n_dec, SSM_GROUPS, SSM_STATE), s_hi.reshape(1, n_dec, SSM_GROUPS, SSM_STATE),
            nk.reshape(1, n_dec, WINDOW, N_KV_HEADS, HEAD_DIM), nv.reshape(1, n_dec, WINDOW, N_KV_HEADS, HEAD_DIM))
```

```python
import functools
import math

import numpy as np
import jax
import jax.numpy as jnp
from jax import lax
from jax.experimental import pallas as pl
from jax.experimental.pallas import tpu as pltpu

F32 = jnp.float32
BF16 = jnp.bfloat16

D_MODEL = 1024
SSM_GROUPS = 64
SSM_GROUP = 16
SSM_STATE = 64
N_HEADS = 16
HEAD_DIM = 64
N_KV_HEADS = 4
KV_GROUP = 4
KV_WIDTH = N_KV_HEADS * HEAD_DIM
WINDOW = 128
N_BUCKETS = 32
NEG_INF = -1e30
LN_EPS = 1e-5
DEPTH = 1
DEEPNORM_ALPHA = (2 * DEPTH) ** 0.25
D_IN = 6656
CHUNK = 16
CW = CHUNK * SSM_GROUP
SW = 2 * SSM_STATE

COL_U, COL_ZS, COL_Q, COL_ZA, COL_GS, COL_GA = 0, 1, 2, 3, 4, 5
COL_K, COL_V = 24, 25

VMEM_LIMIT = 56 * 1024 * 1024
VMEM_LIMIT_SSM = 60 * 1024 * 1024


def _head_perm():
    return np.arange(N_HEADS * HEAD_DIM).reshape(N_KV_HEADS, KV_GROUP, HEAD_DIM).transpose(1, 0, 2).reshape(-1)


def _sigmoid(x):
    return 0.5 * jnp.tanh(0.5 * x) + 0.5


def _silu(x):
    return x * _sigmoid(x)


def _gelu_tanh(x):
    c = math.sqrt(2.0 / math.pi)
    hx = 0.5 * x
    return hx + hx * jnp.tanh(x * (c + (c * 0.044715) * (x * x)))


def _dot(a, b):
    return jnp.dot(a, b, preferred_element_type=F32)


def _dot_nt(a, b):
    return lax.dot_general(a, b, (((1,), (1,)), ((), ())), preferred_element_type=F32)


def _ada_body(c_ref, w_ref, b_ref, o_ref):
    c = c_ref[...]
    sc = _silu(c).astype(BF16)
    o_ref[...] = _dot(sc, w_ref[...].astype(BF16)) + b_ref[...]


def _ada(c_all, w_ada, b_ada):
    rows = c_all.shape[0]
    bn = 512
    return pl.pallas_call(
        _ada_body,
        grid=(3 * D_MODEL // bn,),
        in_specs=[pl.BlockSpec((rows, D_MODEL), lambda j: (0, 0)),
                  pl.BlockSpec((D_MODEL, bn), lambda j: (0, j)),
                  pl.BlockSpec((1, bn), lambda j: (0, j))],
        out_specs=pl.BlockSpec((rows, bn), lambda j: (0, j)),
        out_shape=jax.ShapeDtypeStruct((rows, 3 * D_MODEL), F32),
        name="ada",
    )(c_all, w_ada, b_ada.reshape(1, -1))


PROJ_CB = 512


def _proj_body(x_ref, shift_ref, scale_ref, w_ref, o_ref, *, n_out):
    h = x_ref[...] * (1.0 + scale_ref[...]) + shift_ref[...]
    hb = h.astype(BF16)
    for j in range(n_out // PROJ_CB):
        sl = slice(j * PROJ_CB, (j + 1) * PROJ_CB)
        o_ref[:, sl] = _dot(hb, w_ref[:, sl]).astype(BF16)


def _proj(x2d, shift_arr, scale_arr, shift_spec, scale_spec, w_in_b, tm, n_out):
    rows = x2d.shape[0]
    return pl.pallas_call(
        functools.partial(_proj_body, n_out=n_out),
        grid=(rows // tm,),
        in_specs=[pl.BlockSpec((tm, D_MODEL), lambda i: (i, 0)),
                  shift_spec, scale_spec,
                  pl.BlockSpec((D_MODEL, n_out), lambda i: (0, 0), pipeline_mode=pl.Buffered(1))],
        out_specs=pl.BlockSpec((tm, n_out), lambda i: (i, 0)),
        out_shape=jax.ShapeDtypeStruct((rows, n_out), BF16),
        compiler_params=pltpu.CompilerParams(vmem_limit_bytes=VMEM_LIMIT),
        name="proj",
    )(x2d, shift_arr, scale_arr, w_in_b)


NKS = 128
GC = 16


def _cmul(c1, c2, x):
    return c1 * x + c2 * pltpu.roll(x, SSM_STATE, x.ndim - 1)


SSM_TOK = NKS * CHUNK
SUB = 8
NPH = CHUNK // SUB
GROUP_UNROLL = 16


def _dot_tn(a, b):
    return lax.dot_general(a, b, (((0,), (0,)), ((), ())), preferred_element_type=F32)


SEG = 8
SEG_LEN = NKS // SEG
POW_TAUS = (tuple(range(CHUNK + 1)) + tuple(CHUNK * m for m in range(2, SEG_LEN))
            + tuple(CHUNK * SEG_LEN * m for m in range(1, SEG + 1)) + (0,))
ROW_CHUNK1 = CHUNK
ROW_SEG1 = CHUNK + SEG_LEN - 1


def _ssm_body(x_ref, shift_ref, scale_ref, wut_ref, lhs_ref, bpow_ref, cpow_ref, p1_ref, p2_ref, unperm_ref,
              y_ref, hfin_ref, r_scr, yt_scr, s_scr, carry_scr):
    i = pl.program_id(1)
    j = pl.program_id(2)

    @pl.when((i == 0) & (j == 0))
    def _():
        carry_scr[...] = jnp.zeros_like(carry_scr)

    @pl.when(j < NPH)
    def _():
        for sp in range(SUB // 2):
            h = jnp.concatenate([x_ref[:, b, s, :] for s in (2 * sp, 2 * sp + 1) for b in range(SEG_LEN)],
                                axis=0)
            hb = (h * (1.0 + scale_ref[...]) + shift_ref[...]).astype(BF16)
            ut = _dot_nt(wut_ref[...], hb).astype(BF16)
            ut = ut.reshape(SSM_GROUPS, SSM_GROUP, 2 * NKS)
            pair = jnp.concatenate([ut[:, :, :NKS], ut[:, :, NKS:]], axis=1)
            r0 = pl.multiple_of(j * SUB * SSM_GROUP + sp * 2 * SSM_GROUP, 2 * SSM_GROUP)
            r_scr[:, pl.ds(r0, 2 * SSM_GROUP), :] = pair

    @pl.when(j == NPH - 1)
    def _():
        def local(g, c):
            rg = r_scr[g]
            yt_scr[g] = _dot(lhs_ref[g], rg)
            s_scr[g] = _dot_tn(rg, bpow_ref[g])
            return c

        lax.fori_loop(0, SSM_GROUPS, local, 0, unroll=GROUP_UNROLL)

        row = lax.broadcasted_iota(jnp.int32, (GC, SEG, SW), 1)
        seg_rows = lambda b: slice(b * SEG, (b + 1) * SEG)
        for gc in range(SSM_GROUPS // GC):
            gs = slice(gc * GC, (gc + 1) * GC)
            p1, p2 = p1_ref[gs], p2_ref[gs]
            c1, c2 = p1[:, ROW_CHUNK1:ROW_CHUNK1 + 1, :], p2[:, ROW_CHUNK1:ROW_CHUNK1 + 1, :]
            h = jnp.zeros((GC, SEG, SW), F32)
            hs = h
            for b in range(SEG_LEN):
                sb = s_scr[gs, seg_rows(b), :]
                h, hs = sb + c1 * h + c2 * hs, pltpu.roll(sb, SSM_STATE, 2) + c1 * hs - c2 * h
                s_scr[gs, seg_rows(b), :] = h
            e = h
            for d in (1, 2, 4):
                rd = ROW_SEG1 + d - 1
                sh = jnp.where(row >= d, pltpu.roll(e, d, 1), 0.0)
                e = e + _cmul(p1[:, rd:rd + 1, :], p2[:, rd:rd + 1, :], sh)
            carry = carry_scr[gs]
            e = e + _cmul(p1[:, ROW_SEG1:ROW_SEG1 + SEG, :], p2[:, ROW_SEG1:ROW_SEG1 + SEG, :], carry)
            ein = jnp.where(row == 0, carry, pltpu.roll(e, 1, 1))
            eins = pltpu.roll(ein, SSM_STATE, 2)
            carry_scr[gs] = jnp.broadcast_to(e[:, SEG - 1:SEG, :], (GC, SEG, SW))
            for b in range(SEG_LEN - 1, 0, -1):
                rb = ROW_CHUNK1 + b - 1
                s_scr[gs, seg_rows(b), :] = (s_scr[gs, seg_rows(b - 1), :]
                                             + p1[:, rb:rb + 1, :] * ein + p2[:, rb:rb + 1, :] * eins)
            s_scr[gs, seg_rows(0), :] = ein
        hfin_ref[...] = carry_scr[...]

        def carried(g, c):
            yt_scr[g] += _dot_nt(cpow_ref[g], s_scr[g].astype(BF16))
            return c

        lax.fori_loop(0, SSM_GROUPS, carried, 0, unroll=GROUP_UNROLL)

    @pl.when(j >= NPH)
    def _():
        for sp in range(SUB):
            t0 = pl.multiple_of((j - NPH) * SUB * SSM_GROUP + sp * SSM_GROUP, SSM_GROUP)
            yt = yt_scr[:, pl.ds(t0, SSM_GROUP), :].reshape(D_MODEL, NKS)
            hi = yt.astype(BF16)
            lo = (yt - hi.astype(F32)).astype(BF16)
            y_ref[:, sp, :] = _dot_nt(unperm_ref[...], hi) + _dot_nt(unperm_ref[...], lo)


def _ssm_prompt(x_prompt, modp, wut, lhs, bpow, cpow, p1, p2):
    n_batch, seq, _ = x_prompt.shape
    steps = seq // SSM_TOK
    xv = x_prompt.reshape(n_batch * steps * SEG, SEG_LEN, CHUNK, D_MODEL)
    mod_spec = lambda k: pl.BlockSpec((None, 1, D_MODEL), lambda b, i, j, k=k: (b * 3 + k, 0, 0))
    k = np.arange(NKS)
    perm_np = np.zeros((NKS, NKS), np.float32)
    perm_np[k, (k % SEG_LEN) * SEG + k // SEG_LEN] = 1.0
    consts = (wut, lhs, bpow, cpow, p1, p2, jnp.asarray(perm_np, BF16))
    y, hfin = pl.pallas_call(
        _ssm_body,
        grid=(n_batch, steps, 2 * NPH),
        in_specs=[pl.BlockSpec((SEG, SEG_LEN, SUB, D_MODEL),
                               lambda b, i, j: (b * steps + i, 0, jnp.minimum(j, NPH - 1), 0)),
                  mod_spec(0), mod_spec(1)] + [_const_spec(c.shape) for c in consts],
        out_specs=[pl.BlockSpec((NKS, SUB, D_MODEL), lambda b, i, j: (b * steps + i, jnp.maximum(j - NPH, 0), 0)),
                   pl.BlockSpec((None, SSM_GROUPS, 8, SW), lambda b, i, j: (b, 0, 0, 0))],
        out_shape=[jax.ShapeDtypeStruct((n_batch * seq // CHUNK, CHUNK, D_MODEL), F32),
                   jax.ShapeDtypeStruct((n_batch, SSM_GROUPS, 8, SW), F32)],
        scratch_shapes=[pltpu.VMEM((SSM_GROUPS, CW, NKS), BF16), pltpu.VMEM((SSM_GROUPS, CW, NKS), F32),
                        pltpu.VMEM((SSM_GROUPS, NKS, SW), F32), pltpu.VMEM((SSM_GROUPS, 8, SW), F32)],
        compiler_params=pltpu.CompilerParams(dimension_semantics=("arbitrary", "arbitrary", "arbitrary"),
                                             vmem_limit_bytes=VMEM_LIMIT_SSM),
        name="ssm_prompt",
    )(xv, modp, modp, *consts)
    return y.reshape(n_batch * seq, D_MODEL), hfin


def _post(ys, zs, o_a, za, gs, ga, x, gate, wglu_ref, bglu_ref, wbs_ref, wba_ref, wout_ref, lng_ref, lnb_ref):
    ys = _gelu_tanh(ys)
    glu = ys * _sigmoid(_dot(ys.astype(BF16), wglu_ref[...]) + bglu_ref[...])
    t = glu.astype(BF16) * _silu(zs)
    b_s = _dot(t, wbs_ref[...])
    oz = o_a.astype(BF16) * _silu(za)
    b_a = _dot(oz, wba_ref[...])
    m = _sigmoid(gs).astype(F32) * b_s + _sigmoid(ga).astype(F32) * b_a
    out = _dot(m.astype(BF16), wout_ref[...])
    r = DEEPNORM_ALPHA * x + gate * out
    mu = jnp.mean(r, axis=-1, keepdims=True)
    rc = r - mu
    var = jnp.mean(rc * rc, axis=-1, keepdims=True)
    return rc * lax.rsqrt(var + LN_EPS) * lng_ref[...] + lnb_ref[...]


TB = 512
NQB = TB // WINDOW
W_MAIN = 5 * D_MODEL + KV_WIDTH
ONES_ROWS = 16
POST_ROWS = 256


def _const_spec(shape):
    nd = len(shape)
    return pl.BlockSpec(shape, lambda *_: (0,) * nd, pipeline_mode=pl.Buffered(1))


def _main_body(x_ref, yssm_ref, shift_ref, scale_ref, gate_ref, wmain_ref, wvt_ref, biast_ref, sink_ref, eye_ref,
               wglu_ref, bglu_ref, wbs_ref, wba_ref, wout_ref, lng_ref, lnb_ref,
               y_ref, ktail_ref, vtail_ref,
               zs_s, q_s, za_s, gs_s, ga_s, k_s, vt_s, ot_s, o_s):
    i = pl.program_id(1)
    x = x_ref[...]
    hb = (x * (1.0 + scale_ref[...]) + shift_ref[...]).astype(BF16)

    def project(dst, c0, c):
        dst[:, c * 512:(c + 1) * 512] = _dot(hb, wmain_ref[:, c0 + c * 512:c0 + (c + 1) * 512]).astype(BF16)

    for c in range(2):
        project(q_s, D_MODEL, c)
    deferred = [(dst, c0, c) for dst, c0 in ((zs_s, 0), (za_s, 2 * D_MODEL), (gs_s, 3 * D_MODEL), (ga_s, 4 * D_MODEL))
                for c in range(2)]

    @pl.when(i == 0)
    def _():
        k_s[0:WINDOW, :] = jnp.zeros((WINDOW, KV_WIDTH), BF16)
        vt_s[:, 0:WINDOW] = jnp.zeros((KV_WIDTH, WINDOW), BF16)

    k_s[WINDOW:WINDOW + TB, :] = _dot(hb, wmain_ref[:, 5 * D_MODEL:W_MAIN]).astype(BF16)
    vt_s[:, WINDOW:WINDOW + TB] = _dot_nt(wvt_ref[...], hb).astype(BF16)

    first = jnp.where(i == 0, 1, 0)
    lane = lax.broadcasted_iota(jnp.int32, (2 * WINDOW, 128), 1)
    ones = jnp.ones((ONES_ROWS, 2 * WINDOW), BF16)
    for blk in range(NQB):
        rows = slice(blk * WINDOW, (blk + 1) * WINDOW)
        kcat = k_s[blk * WINDOW:blk * WINDOW + 2 * WINDOW, :]
        vtc = vt_s[:, blk * WINDOW:blk * WINDOW + 2 * WINDOW]
        sel = first if blk == 0 else 0
        heads = range(N_KV_HEADS)
        sts = []
        for kh in heads:
            pair, odd = kh // 2, kh % 2
            kcm = jnp.where((lane >= HEAD_DIM) == bool(odd), kcat[:, pair * 128:(pair + 1) * 128], jnp.zeros((), BF16))
            qp = jnp.concatenate(
                [q_s[rows, j * KV_WIDTH + pair * 128:j * KV_WIDTH + (pair + 1) * 128] for j in range(KV_GROUP)],
                axis=0)
            sts.append(_dot_nt(kcm, qp) + biast_ref[sel, kh])
        ms = [jnp.maximum(jnp.max(sts[kh], axis=0, keepdims=True), sink_ref[kh]) for kh in heads]
        es = [jnp.exp(sts[kh] - ms[kh]).astype(BF16) for kh in heads]
        rs = [_dot(jnp.concatenate([vtc[kh * HEAD_DIM:(kh + 1) * HEAD_DIM, :], ones], axis=0), es[kh])
              for kh in heads]
        for kh in heads:
            den = rs[kh][HEAD_DIM:HEAD_DIM + 1, :] + jnp.exp(sink_ref[kh] - ms[kh])
            ot = (rs[kh][:HEAD_DIM, :] / den).astype(BF16)
            for j in range(KV_GROUP):
                ot_s[blk, j, kh * HEAD_DIM:(kh + 1) * HEAD_DIM, :] = ot[:, j * WINDOW:(j + 1) * WINDOW]
        for spec in deferred[blk * len(deferred) // NQB:(blk + 1) * len(deferred) // NQB]:
            project(*spec)
        for j in range(KV_GROUP):
            o_s[rows, j * KV_WIDTH:(j + 1) * KV_WIDTH] = _dot_nt(eye_ref[...], ot_s[blk, j]).astype(BF16)

    @pl.when(i == pl.num_programs(1) - 1)
    def _():
        ktail_ref[...] = k_s[TB:TB + WINDOW, :].astype(F32)
        vtail_ref[...] = _dot_nt(eye_ref[...], vt_s[:, TB:TB + WINDOW])

    k_s[0:WINDOW, :] = k_s[TB:TB + WINDOW, :]
    vt_s[:, 0:WINDOW] = vt_s[:, TB:TB + WINDOW]

    for c in range(TB // POST_ROWS):
        rs = slice(c * POST_ROWS, (c + 1) * POST_ROWS)
        y_ref[rs, :] = _post(yssm_ref[rs, :], zs_s[rs, :], o_s[rs, :], za_s[rs, :], gs_s[rs, :], ga_s[rs, :],
                             x_ref[rs, :], gate_ref[...], wglu_ref, bglu_ref, wbs_ref, wba_ref, wout_ref,
                             lng_ref, lnb_ref)


def _main_prompt(x2d, yssm, modp, wmain, wvt, biast, sink_rows, eye, wglu, bglu, wbs, wba, wout, lng, lnb,
                 n_batch, seq):
    steps = seq // TB
    row_spec = pl.BlockSpec((TB, D_MODEL), lambda b, i: (b * steps + i, 0))
    mod_spec = lambda j: pl.BlockSpec((None, 1, D_MODEL), lambda b, i, j=j: (b * 3 + j, 0, 0))
    tail_spec = pl.BlockSpec((None, WINDOW, KV_WIDTH), lambda b, i: (b, 0, 0))
    consts = (wmain, wvt, biast, sink_rows, eye, wglu, bglu, wbs, wba, wout, lng, lnb)
    return pl.pallas_call(
        _main_body,
        grid=(n_batch, steps),
        in_specs=[row_spec, row_spec, mod_spec(0), mod_spec(1), mod_spec(2)] + [_const_spec(c.shape) for c in consts],
        out_specs=[row_spec, tail_spec, tail_spec],
        out_shape=[jax.ShapeDtypeStruct((n_batch * seq, D_MODEL), F32),
                   jax.ShapeDtypeStruct((n_batch, WINDOW, KV_WIDTH), F32),
                   jax.ShapeDtypeStruct((n_batch, WINDOW, KV_WIDTH), F32)],
        scratch_shapes=[pltpu.VMEM((TB, D_MODEL), BF16)] * 5
        + [pltpu.VMEM((TB + WINDOW, KV_WIDTH), BF16), pltpu.VMEM((KV_WIDTH, TB + WINDOW), BF16),
           pltpu.VMEM((NQB, KV_GROUP, KV_WIDTH, WINDOW), BF16), pltpu.VMEM((TB, D_MODEL), BF16)],
        compiler_params=pltpu.CompilerParams(dimension_semantics=("arbitrary", "arbitrary"),
                                             vmem_limit_bytes=VMEM_LIMIT),
        name="main_prompt",
    )(x2d, yssm, modp, modp, modp, *consts)


GT = 8


def _sample_ssm_body(u_ref, h0r_ref, h0i_ref, ar_ref, ai_ref, bre_ref, bim_ref, cblk_ref, d_ref,
                     y_ref, hr_ref, hi_ref):
    ub = u_ref[...]
    ys = []
    for a in range(GT):
        ua = ub[:, a * 128:(a + 1) * 128]
        st = slice(a * 512, (a + 1) * 512)
        xr = _dot(ua, bre_ref[a])
        xi = _dot(ua, bim_ref[a])
        h0r = h0r_ref[:, st]
        h0i = h0i_ref[:, st]
        ar = ar_ref[:, st]
        ai = ai_ref[:, st]
        hr = ar * h0r - ai * h0i + xr
        hi = ar * h0i + ai * h0r + xi
        hr_ref[:, st] = hr
        hi_ref[:, st] = hi
        hcat = jnp.concatenate([hr, hi], axis=1).astype(BF16)
        ys.append(_dot_nt(hcat, cblk_ref[a]))
    y_ref[...] = jnp.concatenate(ys, axis=1) + d_ref[...] * ub.astype(F32)


def _sample_ssm(proj_s, h0r, h0i, ar, ai, bre, bim, cblk, d_row):
    rows = proj_s.shape[0]
    ns = SSM_GROUPS * SSM_STATE
    full = lambda shape: pl.BlockSpec(shape, lambda i: (0,) * len(shape))
    return pl.pallas_call(
        _sample_ssm_body,
        grid=(1,),
        in_specs=[pl.BlockSpec((rows, D_MODEL), lambda i: (0, COL_U)),
                  full((rows, ns)), full((rows, ns)), full((1, ns)), full((1, ns)),
                  full(bre.shape), full(bim.shape), full(cblk.shape), full((1, D_MODEL))],
        out_specs=[full((rows, D_MODEL)), full((rows, ns)), full((rows, ns))],
        out_shape=[jax.ShapeDtypeStruct((rows, D_MODEL), F32),
                   jax.ShapeDtypeStruct((rows, ns), F32),
                   jax.ShapeDtypeStruct((rows, ns), F32)],
        compiler_params=pltpu.CompilerParams(vmem_limit_bytes=VMEM_LIMIT),
        name="sample_ssm",
    )(proj_s, h0r, h0i, ar, ai, bre, bim, cblk, d_row)


SB = 16
KROWS = WINDOW + 8


def _sample_attn_body(q_ref, kn_ref, vn_ref, ck_ref, cv_ref, bias_ref, sink_ref, seg_ref, segt_ref,
                      o_ref, nk_ref, nv_ref):
    qf = q_ref[...].astype(F32)
    knf = kn_ref[...].astype(F32)
    vnf = vn_ref[...].astype(F32)
    rowk = lax.broadcasted_iota(jnp.int32, (KROWS, 128), 0)
    roww = lax.broadcasted_iota(jnp.int32, (WINDOW, KV_WIDTH), 0)
    sink = sink_ref[...]
    for b in range(SB):
        kb = ck_ref[b]
        vb = cv_ref[b]
        qrow = qf[b:b + 1]
        kn = jnp.broadcast_to(knf[b:b + 1], (8, KV_WIDTH))
        vn = jnp.broadcast_to(vnf[b:b + 1], (8, KV_WIDTH))
        kall = jnp.concatenate([kb, kn], axis=0)
        vall = jnp.concatenate([vb, vn], axis=0)
        prod = jnp.concatenate([kall * qrow[:, j * KV_WIDTH:(j + 1) * KV_WIDTH] for j in range(KV_GROUP)], axis=1)
        s = _dot(prod.astype(BF16), seg_ref[...]) + bias_ref[...]
        s = jnp.where(rowk <= WINDOW, s, NEG_INF)
        m = jnp.maximum(jnp.max(s, axis=0, keepdims=True), sink)
        e = jnp.exp(s - m)
        den = jnp.sum(e, axis=0, keepdims=True) + jnp.exp(sink - m)
        p = e / den
        pexp = _dot(p.astype(BF16), segt_ref[...])
        v4 = jnp.concatenate([vall] * KV_GROUP, axis=1)
        o_ref[b:b + 1, :] = jnp.sum(pexp * v4, axis=0, keepdims=True)
        last = roww == WINDOW - 1
        nk_ref[b] = jnp.where(last, jnp.broadcast_to(knf[b:b + 1], (WINDOW, KV_WIDTH)), pltpu.roll(kb, WINDOW - 1, 0))
        nv_ref[b] = jnp.where(last, jnp.broadcast_to(vnf[b:b + 1], (WINDOW, KV_WIDTH)), pltpu.roll(vb, WINDOW - 1, 0))


def _sample_attn(proj_s, ck, cv, bias_s, sink_row, seg, segt):
    rows = proj_s.shape[0]
    cache_spec = pl.BlockSpec((SB, WINDOW, KV_WIDTH), lambda i: (i, 0, 0))
    return pl.pallas_call(
        _sample_attn_body,
        grid=(rows // SB,),
        in_specs=[pl.BlockSpec((SB, D_MODEL), lambda i: (i, COL_Q)),
                  pl.BlockSpec((SB, KV_WIDTH), lambda i: (i, COL_K)),
                  pl.BlockSpec((SB, KV_WIDTH), lambda i: (i, COL_V)),
                  cache_spec, cache_spec,
                  pl.BlockSpec(bias_s.shape, lambda i: (0, 0)),
                  pl.BlockSpec(sink_row.shape, lambda i: (0, 0)),
                  pl.BlockSpec(seg.shape, lambda i: (0, 0)),
                  pl.BlockSpec(segt.shape, lambda i: (0, 0))],
        out_specs=[pl.BlockSpec((SB, D_MODEL), lambda i: (i, 0)), cache_spec, cache_spec],
        out_shape=[jax.ShapeDtypeStruct((rows, D_MODEL), F32),
                   jax.ShapeDtypeStruct(ck.shape, F32),
                   jax.ShapeDtypeStruct(cv.shape, F32)],
        compiler_params=pltpu.CompilerParams(vmem_limit_bytes=VMEM_LIMIT),
        name="sample_attn",
    )(proj_s, proj_s, proj_s, ck, cv, bias_s, sink_row, seg, segt)


def _sample_final_body(x_ref, yssm_ref, zs_ref, o_ref, za_ref, gs_ref, ga_ref, gate_ref,
                       wglu_ref, bglu_ref, wbs_ref, wba_ref, wout_ref, lng_ref, lnb_ref, y_ref):
    y_ref[...] = _post(yssm_ref[...], zs_ref[...], o_ref[...], za_ref[...], gs_ref[...], ga_ref[...],
                       x_ref[...], gate_ref[...], wglu_ref, bglu_ref, wbs_ref, wba_ref, wout_ref,
                       lng_ref, lnb_ref)


def _final_sample(x2d, yssm, o_a, proj_s, mods, wglu, bglu, wbs, wba, wout, lng, lnb):
    rows = x2d.shape[0]
    full = lambda shape: pl.BlockSpec(shape, lambda i: (0,) * len(shape))
    pcol = lambda c: pl.BlockSpec((rows, D_MODEL), lambda i, c=c: (0, c))
    return pl.pallas_call(
        _sample_final_body,
        grid=(1,),
        in_specs=[full((rows, D_MODEL)), full((rows, D_MODEL)), pcol(COL_ZS), full((rows, D_MODEL)),
                  pcol(COL_ZA), pcol(COL_GS), pcol(COL_GA),
                  pl.BlockSpec((rows, D_MODEL), lambda i: (0, 2)),
                  full(wglu.shape), full(bglu.shape), full(wbs.shape), full(wba.shape), full(wout.shape),
                  full(lng.shape), full(lnb.shape)],
        out_specs=full((rows, D_MODEL)),
        out_shape=jax.ShapeDtypeStruct((rows, D_MODEL), F32),
        compiler_params=pltpu.CompilerParams(vmem_limit_bytes=VMEM_LIMIT),
        name="final_sample",
    )(x2d, yssm, proj_s, o_a, proj_s, proj_s, proj_s, mods, wglu, bglu, wbs, wba, wout, lng, lnb)


def _bucket_table():
    max_exact = N_BUCKETS // 2
    dist = np.arange(WINDOW + 1)
    df = np.maximum(dist, 1).astype(np.float32)
    large = max_exact + (np.log(df / np.float32(max_exact)) / np.float32(math.log(WINDOW / max_exact))
                         * np.float32(N_BUCKETS - max_exact)).astype(np.int32)
    large = np.minimum(large, N_BUCKETS - 1)
    return np.where(dist < max_exact, dist, large)


def _dot3_nt(a, b):
    ah = a.astype(BF16)
    al = (a - ah.astype(F32)).astype(BF16)
    bh = b.astype(BF16)
    bl = (b - bh.astype(F32)).astype(BF16)
    return _dot_nt(ah, bh) + _dot_nt(ah, bl) + _dot_nt(al, bh)


def _ssm_prep_body(p1_ref, p2_ref, kf_ref, bt_ref, bts_ref, c2_ref, c2s_ref, d_ref,
                   lhs_ref, bpow_ref, cpow_ref, bre_ref, bim_ref, cblk_ref):
    lane = lax.broadcasted_iota(jnp.int32, (SSM_GROUP, SW), 1)
    left = lane < SSM_STATE
    sgn = jnp.where(left, 1.0, -1.0)
    lane_blk = lax.broadcasted_iota(jnp.int32, (CW, CW), 1) // SSM_GROUP
    diag = lax.broadcasted_iota(jnp.int32, (CW, CW), 0) == lax.broadcasted_iota(jnp.int32, (CW, CW), 1)
    bre_ref[...] = jnp.zeros_like(bre_ref)
    bim_ref[...] = jnp.zeros_like(bim_ref)
    cblk_ref[...] = jnp.zeros_like(cblk_ref)
    for gl in range(GT):
        p1, p2 = p1_ref[gl], p2_ref[gl]
        k1, k2 = kf_ref[gl, 0:1, :], kf_ref[gl, 1:2, :]
        bt, bts = bt_ref[gl], bts_ref[gl]
        c2, c2s = c2_ref[gl], c2s_ref[gl]
        bb = k1 * bt + k2 * bts
        bbs = k1 * bts - k2 * bt
        ca = [sgn * (c2 * p1[t:t + 1, :] + c2s * p2[t:t + 1, :]) for t in range(CHUNK + 1)]
        cpow_ref[gl] = jnp.concatenate(ca[1:], axis=0).astype(BF16)
        bpow_ref[gl] = jnp.concatenate(
            [p1[CHUNK - 1 - s:CHUNK - s, :] * bb + p2[CHUNK - 1 - s:CHUNK - s, :] * bbs for s in range(CHUNK)],
            axis=0).astype(BF16)
        e = _dot3_nt(jnp.concatenate(ca[:CHUNK], axis=0), jnp.concatenate([bb] * CHUNK, axis=0))
        acc = jnp.where(diag, d_ref[gl], 0.0)
        for s in range(CHUNK):
            sh = e if s == 0 else jnp.concatenate([jnp.zeros((s * SSM_GROUP, CW), F32), e[:CW - s * SSM_GROUP, :]], axis=0)
            acc = acc + jnp.where(lane_blk == s, sh, 0.0)
        lhs_ref[gl] = acc.astype(BF16)
        rs = slice(gl * SSM_GROUP, (gl + 1) * SSM_GROUP)
        ts = slice((gl // 2) * SW, (gl // 2 + 1) * SW)
        ti = slice(GT * SSM_STATE + (gl // 2) * SW, GT * SSM_STATE + (gl // 2 + 1) * SW)
        if gl % 2 == 0:
            bre_ref[rs, ts] = jnp.where(left, bb, 0.0).astype(BF16)
            bim_ref[rs, ts] = jnp.where(left, bbs, 0.0).astype(BF16)
            cblk_ref[rs, ts] = jnp.where(left, c2, 0.0).astype(BF16)
            cblk_ref[rs, ti] = jnp.where(left, -c2s, 0.0).astype(BF16)
        else:
            bre_ref[rs, ts] = jnp.where(left, 0.0, bbs).astype(BF16)
            bim_ref[rs, ts] = jnp.where(left, 0.0, bb).astype(BF16)
            cblk_ref[rs, ts] = jnp.where(left, 0.0, c2s).astype(BF16)
            cblk_ref[rs, ti] = jnp.where(left, 0.0, -c2).astype(BF16)


def _ssm_params(lam_re, lam_im, log_delta, b_re, b_im, c_re, c_im, d_skip):
    g = SSM_GROUPS
    dup = lambda a: jnp.concatenate([a, a], axis=-1)
    lr, li = dup(lam_re.astype(F32)), dup(lam_im.astype(F32))
    dt = jnp.exp(log_delta.astype(F32))[:, None]
    half_sign = jnp.asarray(np.where(np.arange(SW) < SSM_STATE, -1.0, 1.0), F32)
    tau = jnp.asarray(POW_TAUS, F32)[None, :, None]
    mag = jnp.exp((lr * dt)[:, None, :] * tau)
    ang = (li * dt)[:, None, :] * tau
    p1 = mag * jnp.cos(ang)
    pim = mag * jnp.sin(ang)
    p2 = pim * half_sign
    ar, ai = p1[:, 1, :], pim[:, 1, :]
    den = lr * lr + li * li
    nr = ar - 1.0
    k1 = (nr * lr + ai * li) / den
    k2 = (ai * lr - nr * li) / den * half_sign
    kf = jnp.concatenate([k1[:, None, :], k2[:, None, :], jnp.zeros((g, 6, SW), F32)], axis=1)
    brt, bit = jnp.swapaxes(b_re, 1, 2), jnp.swapaxes(b_im, 1, 2)
    bt = jnp.concatenate([brt, bit], axis=-1)
    bts = jnp.concatenate([bit, brt], axis=-1)
    c2 = jnp.concatenate([c_re, c_im], axis=-1)
    c2s = jnp.concatenate([c_im, c_re], axis=-1)
    dtile = jnp.tile(d_skip.reshape(g, 1, SSM_GROUP), (1, 1, CHUNK))

    gspec = lambda r, w: pl.BlockSpec((GT, r, w), lambda a: (a, 0, 0))
    tile_spec = lambda w: pl.BlockSpec((None, GT * SSM_GROUP, w), lambda a: (a, 0, 0))
    nrow = len(POW_TAUS)
    lhs, bpow, cpow, bre, bim, cblk = pl.pallas_call(
        _ssm_prep_body,
        grid=(g // GT,),
        in_specs=[gspec(nrow, SW), gspec(nrow, SW), gspec(8, SW), gspec(SSM_GROUP, SW), gspec(SSM_GROUP, SW),
                  gspec(SSM_GROUP, SW), gspec(SSM_GROUP, SW), gspec(1, CW)],
        out_specs=[gspec(CW, CW), gspec(CW, SW), gspec(CW, SW),
                   tile_spec(GT * SSM_STATE), tile_spec(GT * SSM_STATE), tile_spec(2 * GT * SSM_STATE)],
        out_shape=[jax.ShapeDtypeStruct((g, CW, CW), BF16), jax.ShapeDtypeStruct((g, CW, SW), BF16),
                   jax.ShapeDtypeStruct((g, CW, SW), BF16),
                   jax.ShapeDtypeStruct((GT, GT * SSM_GROUP, GT * SSM_STATE), BF16),
                   jax.ShapeDtypeStruct((GT, GT * SSM_GROUP, GT * SSM_STATE), BF16),
                   jax.ShapeDtypeStruct((GT, GT * SSM_GROUP, 2 * GT * SSM_STATE), BF16)],
        name="ssm_prep",
    )(p1, p2, kf, bt, bts, c2, c2s, dtile)
    ar_row = ar[:, :SSM_STATE].reshape(1, -1)
    ai_row = ai[:, :SSM_STATE].reshape(1, -1)
    return lhs, bpow, cpow, p1, p2, ar_row, ai_row, bre, bim, cblk


def _attn_params(rel_bias, sinks):
    hp = lax.Precision.HIGHEST
    rb = rel_bias.astype(F32)
    onehot = jnp.asarray(_bucket_table()[:, None] == np.arange(N_BUCKETS)[None, :], F32)
    tbl = jnp.dot(onehot, rb, precision=hp)
    span = 3 * WINDOW
    w = jnp.concatenate([tbl[::-1], jnp.full((span - WINDOW - 1, N_HEADS), NEG_INF, F32)], axis=0)
    wt = jnp.broadcast_to(jnp.transpose(w)[:, None, :], (N_HEADS, WINDOW, span)).reshape(N_HEADS, WINDOW * span)
    bq = wt[:, :WINDOW * (span - 1)].reshape(N_HEADS, WINDOW, span - 1)[:, :, :2 * WINDOW]
    first = jnp.where(jnp.arange(2 * WINDOW)[None, None, :] >= WINDOW, bq, NEG_INF)
    bm = jnp.stack([bq, first], axis=0)
    biast = jnp.transpose(bm.reshape(2, N_KV_HEADS, KV_GROUP, WINDOW, 2 * WINDOW), (0, 1, 4, 2, 3)) \
        .reshape(2, N_KV_HEADS, 2 * WINDOW, KV_GROUP * WINDOW)
    sink_rows = jnp.repeat(sinks.astype(F32).reshape(N_KV_HEADS, 1, KV_GROUP), WINDOW, axis=2)
    ds = np.clip(WINDOW - np.arange(KROWS), 0, WINDOW)
    bs = jnp.dot(jnp.asarray(ds[:, None] == np.arange(WINDOW + 1)[None, :], F32), tbl, precision=hp)
    bs = jnp.transpose(bs.reshape(KROWS, N_KV_HEADS, KV_GROUP), (0, 2, 1)).reshape(KROWS, N_HEADS)
    bias_s = jnp.pad(bs, ((0, 0), (0, 128 - N_HEADS)))
    sk = jnp.transpose(sinks.astype(F32).reshape(N_KV_HEADS, KV_GROUP), (1, 0)).reshape(1, N_HEADS)
    sink_row = jnp.pad(sk, ((0, 0), (0, 128 - N_HEADS)))
    seg_np = np.zeros((D_MODEL, 128), np.float32)
    for j in range(KV_GROUP):
        for kh in range(N_KV_HEADS):
            r0 = j * KV_WIDTH + kh * HEAD_DIM
            seg_np[r0:r0 + HEAD_DIM, j * N_KV_HEADS + kh] = 1.0
    seg = jnp.asarray(seg_np, BF16)
    segt = jnp.asarray(seg_np.T, BF16)
    return biast, sink_rows, bias_s, sink_row, seg, segt


def kernel(x_prompt, x_sample, c_prompt, c_sample, state_ssm_re, state_ssm_im, cache_swa_k, cache_swa_v,
           w_ada, b_ada, w_in, ssm_lambda_re, ssm_lambda_im, ssm_log_delta, ssm_b_re, ssm_b_im,
           ssm_c_re, ssm_c_im, ssm_d, w_glu, b_glu, attn_sinks, rel_bias, w_branch_s, w_branch_a,
           w_out, ln_g, ln_b):
    assert w_ada.shape[0] == 1, "single-layer trunk"
    n_batch, seq, _ = x_prompt.shape
    n_dec = x_sample.shape[0]
    perm = _head_perm()

    wi = w_in[0]
    w_in_b = jnp.concatenate([wi[:, 0:1024], wi[:, 1024:2048], wi[:, 2048:3072][:, perm] * (HEAD_DIM ** -0.5),
                              wi[:, 3584:4608][:, perm], wi[:, 4608:5632], wi[:, 5632:6656],
                              wi[:, 3072:3328], wi[:, 3328:3584]], axis=1).astype(BF16)
    wmain = w_in_b[:, D_MODEL:D_MODEL + W_MAIN]
    wvt = jnp.transpose(wi[:, 3328:3584]).astype(BF16)
    wut = jnp.transpose(wi[:, 0:D_MODEL]).astype(BF16)
    eye = jnp.eye(WINDOW, dtype=BF16)
    wglu = w_glu[0].astype(BF16)
    bglu = b_glu[0].reshape(1, -1).astype(F32)
    wbs = w_branch_s[0].astype(BF16)
    wba = w_branch_a[0][perm, :].astype(BF16)
    wout = w_out[0].astype(BF16)
    lng = ln_g[0].reshape(1, -1).astype(F32)
    lnb = ln_b[0].reshape(1, -1).astype(F32)
    lhs, bpow, cpow, p1, p2, ar_row, ai_row, bre, bim, cblk = _ssm_params(
        ssm_lambda_re[0], ssm_lambda_im[0], ssm_log_delta[0], ssm_b_re[0], ssm_b_im[0],
        ssm_c_re[0], ssm_c_im[0], ssm_d[0])
    biast, sink_rows, bias_s, sink_row, seg, segt = _attn_params(rel_bias, attn_sinks[0])

    c_all = jnp.concatenate([c_prompt, jnp.zeros((8 - n_batch, D_MODEL), F32), c_sample], axis=0)
    mod = _ada(c_all, w_ada[0], b_ada[0])
    modp = mod[:n_batch].reshape(n_batch * 3, 1, D_MODEL)
    mods = mod[8:8 + n_dec]

    xp = x_prompt.reshape(n_batch * seq, D_MODEL)
    yssm, hfin = _ssm_prompt(x_prompt, modp, wut, lhs, bpow, cpow, p1, p2)
    yp, ktail, vtail = _main_prompt(xp, yssm, modp, wmain, wvt, biast, sink_rows, eye,
                                    wglu, bglu, wbs, wba, wout, lng, lnb, n_batch, seq)
    yp = yp.reshape(n_batch, seq, D_MODEL)
    p_hr = hfin[:, :, 0, :SSM_STATE][None]
    p_hi = hfin[:, :, 0, SSM_STATE:][None]
    p_k = ktail.reshape(1, n_batch, WINDOW, N_KV_HEADS, HEAD_DIM)
    p_v = vtail.reshape(1, n_batch, WINDOW, N_KV_HEADS, HEAD_DIM)

    xs = x_sample.reshape(n_dec, D_MODEL)
    proj_s = _proj(xs, mods, mods,
                   pl.BlockSpec((n_dec, D_MODEL), lambda i: (0, 0)),
                   pl.BlockSpec((n_dec, D_MODEL), lambda i: (0, 1)),
                   w_in_b, n_dec, D_IN)
    ns = SSM_GROUPS * SSM_STATE
    ys_s, s_hr, s_hi = _sample_ssm(proj_s, state_ssm_re[0].reshape(n_dec, ns), state_ssm_im[0].reshape(n_dec, ns),
                                   ar_row, ai_row, bre, bim, cblk, ssm_d[0].reshape(1, -1).astype(F32))
    o_s, nk, nv = _sample_attn(proj_s, cache_swa_k[0].reshape(n_dec, WINDOW, KV_WIDTH),
                               cache_swa_v[0].reshape(n_dec, WINDOW, KV_WIDTH), bias_s, sink_row, seg, segt)
    ysmp = _final_sample(xs, ys_s, o_s, proj_s, mods, wglu, bglu, wbs, wba, wout, lng, lnb)

    return (yp, ysmp.reshape(n_dec, 1, D_MODEL),
            p_hr, p_hi, p_k, p_v,
            s_hr.reshape(1, n_dec, SSM_GROUPS, SSM_STATE), s_hi.reshape(1, n_dec, SSM_GROUPS, SSM_STATE),
            nk.reshape(1, n_dec, WINDOW, N_KV_HEADS, HEAD_DIM), nv.reshape(1, n_dec, WINDOW, N_KV_HEADS, HEAD_DIM))
```

```python
import functools
import math

import numpy as np
import jax
import jax.numpy as jnp
from jax import lax
from jax.experimental import pallas as pl
from jax.experimental.pallas import tpu as pltpu

F32 = jnp.float32
BF16 = jnp.bfloat16

D_MODEL = 1024
SSM_GROUPS = 64
SSM_GROUP = 16
SSM_STATE = 64
N_HEADS = 16
HEAD_DIM = 64
N_KV_HEADS = 4
KV_GROUP = 4
KV_WIDTH = N_KV_HEADS * HEAD_DIM
WINDOW = 128
N_BUCKETS = 32
NEG_INF = -1e30
LN_EPS = 1e-5
DEPTH = 1
DEEPNORM_ALPHA = (2 * DEPTH) ** 0.25
D_IN = 6656
CHUNK = 16
CW = CHUNK * SSM_GROUP
SW = 2 * SSM_STATE

COL_U, COL_ZS, COL_Q, COL_ZA, COL_GS, COL_GA = 0, 1, 2, 3, 4, 5
COL_K, COL_V = 24, 25

VMEM_LIMIT = 56 * 1024 * 1024
VMEM_LIMIT_SSM = 60 * 1024 * 1024


def _head_perm():
    return np.arange(N_HEADS * HEAD_DIM).reshape(N_KV_HEADS, KV_GROUP, HEAD_DIM).transpose(1, 0, 2).reshape(-1)


def _sigmoid(x):
    return 0.5 * jnp.tanh(0.5 * x) + 0.5


def _silu(x):
    return x * _sigmoid(x)


def _gelu_tanh(x):
    c = math.sqrt(2.0 / math.pi)
    hx = 0.5 * x
    return hx + hx * jnp.tanh(x * (c + (c * 0.044715) * (x * x)))


def _dot(a, b):
    return jnp.dot(a, b, preferred_element_type=F32)


def _dot_nt(a, b):
    return lax.dot_general(a, b, (((1,), (1,)), ((), ())), preferred_element_type=F32)


def _ada_body(c_ref, w_ref, b_ref, o_ref):
    c = c_ref[...]
    sc = _silu(c).astype(BF16)
    o_ref[...] = _dot(sc, w_ref[...].astype(BF16)) + b_ref[...]


def _ada(c_all, w_ada, b_ada):
    rows = c_all.shape[0]
    bn = 512
    return pl.pallas_call(
        _ada_body,
        grid=(3 * D_MODEL // bn,),
        in_specs=[pl.BlockSpec((rows, D_MODEL), lambda j: (0, 0)),
                  pl.BlockSpec((D_MODEL, bn), lambda j: (0, j)),
                  pl.BlockSpec((1, bn), lambda j: (0, j))],
        out_specs=pl.BlockSpec((rows, bn), lambda j: (0, j)),
        out_shape=jax.ShapeDtypeStruct((rows, 3 * D_MODEL), F32),
        name="ada",
    )(c_all, w_ada, b_ada.reshape(1, -1))


PROJ_CB = 512


def _proj_body(x_ref, shift_ref, scale_ref, w_ref, o_ref, *, n_out):
    h = x_ref[...] * (1.0 + scale_ref[...]) + shift_ref[...]
    hb = h.astype(BF16)
    for j in range(n_out // PROJ_CB):
        sl = slice(j * PROJ_CB, (j + 1) * PROJ_CB)
        o_ref[:, sl] = _dot(hb, w_ref[:, sl]).astype(BF16)


def _proj(x2d, shift_arr, scale_arr, shift_spec, scale_spec, w_in_b, tm, n_out):
    rows = x2d.shape[0]
    return pl.pallas_call(
        functools.partial(_proj_body, n_out=n_out),
        grid=(rows // tm,),
        in_specs=[pl.BlockSpec((tm, D_MODEL), lambda i: (i, 0)),
                  shift_spec, scale_spec,
                  pl.BlockSpec((D_MODEL, n_out), lambda i: (0, 0), pipeline_mode=pl.Buffered(1))],
        out_specs=pl.BlockSpec((tm, n_out), lambda i: (i, 0)),
        out_shape=jax.ShapeDtypeStruct((rows, n_out), BF16),
        compiler_params=pltpu.CompilerParams(vmem_limit_bytes=VMEM_LIMIT),
        name="proj",
    )(x2d, shift_arr, scale_arr, w_in_b)


NKS = 128
GC = 16


def _cmul(c1, c2, x):
    return c1 * x + c2 * pltpu.roll(x, SSM_STATE, x.ndim - 1)


SSM_TOK = NKS * CHUNK
SUB = 8
NPH = CHUNK // SUB
GROUP_UNROLL = 16


def _dot_tn(a, b):
    return lax.dot_general(a, b, (((0,), (0,)), ((), ())), preferred_element_type=F32)


SEG = 8
SEG_LEN = NKS // SEG
POW_TAUS = (tuple(range(CHUNK + 1)) + tuple(CHUNK * m for m in range(2, SEG_LEN))
            + tuple(CHUNK * SEG_LEN * m for m in range(1, SEG + 1)) + (0,))
ROW_CHUNK1 = CHUNK
ROW_SEG1 = CHUNK + SEG_LEN - 1


def _ssm_body(x_ref, shift_ref, scale_ref, wut_ref, lhs_ref, bpow_ref, cpow_ref, p1_ref, p2_ref, unperm_ref,
              y_ref, hfin_ref, r_scr, yt_scr, s_scr, carry_scr):
    i = pl.program_id(1)
    j = pl.program_id(2)

    @pl.when((i == 0) & (j == 0))
    def _():
        carry_scr[...] = jnp.zeros_like(carry_scr)

    @pl.when(j < NPH)
    def _():
        for sp in range(SUB // 2):
            h = jnp.concatenate([x_ref[:, b, s, :] for s in (2 * sp, 2 * sp + 1) for b in range(SEG_LEN)],
                                axis=0)
            hb = (h * (1.0 + scale_ref[...]) + shift_ref[...]).astype(BF16)
            ut = _dot_nt(wut_ref[...], hb).astype(BF16)
            ut = ut.reshape(SSM_GROUPS, SSM_GROUP, 2 * NKS)
            pair = jnp.concatenate([ut[:, :, :NKS], ut[:, :, NKS:]], axis=1)
            r0 = pl.multiple_of(j * SUB * SSM_GROUP + sp * 2 * SSM_GROUP, 2 * SSM_GROUP)
            r_scr[:, pl.ds(r0, 2 * SSM_GROUP), :] = pair

    @pl.when(j == NPH - 1)
    def _():
        def local(g, c):
            rg = r_scr[g]
            yt_scr[g] = _dot(lhs_ref[g], rg)
            s_scr[g] = _dot_tn(rg, bpow_ref[g])
            return c

        lax.fori_loop(0, SSM_GROUPS, local, 0, unroll=GROUP_UNROLL)

        row = lax.broadcasted_iota(jnp.int32, (GC, SEG, SW), 1)
        seg_rows = lambda b: slice(b * SEG, (b + 1) * SEG)
        for gc in range(SSM_GROUPS // GC):
            gs = slice(gc * GC, (gc + 1) * GC)
            p1, p2 = p1_ref[gs], p2_ref[gs]
            c1, c2 = p1[:, ROW_CHUNK1:ROW_CHUNK1 + 1, :], p2[:, ROW_CHUNK1:ROW_CHUNK1 + 1, :]
            h = jnp.zeros((GC, SEG, SW), F32)
            hs = h
            for b in range(SEG_LEN):
                sb = s_scr[gs, seg_rows(b), :]
                h, hs = sb + c1 * h + c2 * hs, pltpu.roll(sb, SSM_STATE, 2) + c1 * hs - c2 * h
                s_scr[gs, seg_rows(b), :] = h
            e = h
            for d in (1, 2, 4):
                rd = ROW_SEG1 + d - 1
                sh = jnp.where(row >= d, pltpu.roll(e, d, 1), 0.0)
                e = e + _cmul(p1[:, rd:rd + 1, :], p2[:, rd:rd + 1, :], sh)
            carry = carry_scr[gs]
            e = e + _cmul(p1[:, ROW_SEG1:ROW_SEG1 + SEG, :], p2[:, ROW_SEG1:ROW_SEG1 + SEG, :], carry)
            ein = jnp.where(row == 0, carry, pltpu.roll(e, 1, 1))
            eins = pltpu.roll(ein, SSM_STATE, 2)
            carry_scr[gs] = jnp.broadcast_to(e[:, SEG - 1:SEG, :], (GC, SEG, SW))
            for b in range(SEG_LEN - 1, 0, -1):
                rb = ROW_CHUNK1 + b - 1
                s_scr[gs, seg_rows(b), :] = (s_scr[gs, seg_rows(b - 1), :]
                                             + p1[:, rb:rb + 1, :] * ein + p2[:, rb:rb + 1, :] * eins)
            s_scr[gs, seg_rows(0), :] = ein
        hfin_ref[...] = carry_scr[...]

        def carried(g, c):
            yt_scr[g] += _dot_nt(cpow_ref[g], s_scr[g].astype(BF16))
            return c

        lax.fori_loop(0, SSM_GROUPS, carried, 0, unroll=GROUP_UNROLL)

    @pl.when(j >= NPH)
    def _():
        for sp in range(SUB):
            t0 = pl.multiple_of((j - NPH) * SUB * SSM_GROUP + sp * SSM_GROUP, SSM_GROUP)
            yt = yt_scr[:, pl.ds(t0, SSM_GROUP), :].reshape(D_MODEL, NKS)
            hi = yt.astype(BF16)
            lo = (yt - hi.astype(F32)).astype(BF16)
            y_ref[:, sp, :] = _dot_nt(unperm_ref[...], hi) + _dot_nt(unperm_ref[...], lo)


def _ssm_prompt(x_prompt, modp, wut, lhs, bpow, cpow, p1, p2):
    n_batch, seq, _ = x_prompt.shape
    steps = seq // SSM_TOK
    xv = x_prompt.reshape(n_batch * steps * SEG, SEG_LEN, CHUNK, D_MODEL)
    mod_spec = lambda k: pl.BlockSpec((None, 1, D_MODEL), lambda b, i, j, k=k: (b * 3 + k, 0, 0))
    k = np.arange(NKS)
    perm_np = np.zeros((NKS, NKS), np.float32)
    perm_np[k, (k % SEG_LEN) * SEG + k // SEG_LEN] = 1.0
    consts = (wut, lhs, bpow, cpow, p1, p2, jnp.asarray(perm_np, BF16))
    y, hfin = pl.pallas_call(
        _ssm_body,
        grid=(n_batch, steps, 2 * NPH),
        in_specs=[pl.BlockSpec((SEG, SEG_LEN, SUB, D_MODEL),
                               lambda b, i, j: (b * steps + i, 0, jnp.minimum(j, NPH - 1), 0)),
                  mod_spec(0), mod_spec(1)] + [_const_spec(c.shape) for c in consts],
        out_specs=[pl.BlockSpec((NKS, SUB, D_MODEL), lambda b, i, j: (b * steps + i, jnp.maximum(j - NPH, 0), 0)),
                   pl.BlockSpec((None, SSM_GROUPS, 8, SW), lambda b, i, j: (b, 0, 0, 0))],
        out_shape=[jax.ShapeDtypeStruct((n_batch * seq // CHUNK, CHUNK, D_MODEL), F32),
                   jax.ShapeDtypeStruct((n_batch, SSM_GROUPS, 8, SW), F32)],
        scratch_shapes=[pltpu.VMEM((SSM_GROUPS, CW, NKS), BF16), pltpu.VMEM((SSM_GROUPS, CW, NKS), F32),
                        pltpu.VMEM((SSM_GROUPS, NKS, SW), F32), pltpu.VMEM((SSM_GROUPS, 8, SW), F32)],
        compiler_params=pltpu.CompilerParams(dimension_semantics=("arbitrary", "arbitrary", "arbitrary"),
                                             vmem_limit_bytes=VMEM_LIMIT_SSM),
        name="ssm_prompt",
    )(xv, modp, modp, *consts)
    return y.reshape(n_batch * seq, D_MODEL), hfin


def _post(ys, zs, o_a, za, gs, ga, x, gate, wglu_ref, bglu_ref, wbs_ref, wba_ref, wout_ref, lng_ref, lnb_ref):
    ys = _gelu_tanh(ys)
    glu = ys * _sigmoid(_dot(ys.astype(BF16), wglu_ref[...]) + bglu_ref[...])
    t = glu.astype(BF16) * _silu(zs)
    b_s = _dot(t, wbs_ref[...])
    oz = o_a.astype(BF16) * _silu(za)
    b_a = _dot(oz, wba_ref[...])
    m = _sigmoid(gs).astype(F32) * b_s + _sigmoid(ga).astype(F32) * b_a
    out = _dot(m.astype(BF16), wout_ref[...])
    r = DEEPNORM_ALPHA * x + gate * out
    mu = jnp.mean(r, axis=-1, keepdims=True)
    rc = r - mu
    var = jnp.mean(rc * rc, axis=-1, keepdims=True)
    return rc * lax.rsqrt(var + LN_EPS) * lng_ref[...] + lnb_ref[...]


TB = 512
NQB = TB // WINDOW
W_MAIN = 5 * D_MODEL + KV_WIDTH
ONES_ROWS = 16
POST_ROWS = 256


def _const_spec(shape):
    nd = len(shape)
    return pl.BlockSpec(shape, lambda *_: (0,) * nd, pipeline_mode=pl.Buffered(1))


def _bucket_thresholds():
    tbl = _bucket_table()
    return [int(np.argmax(tbl >= k)) for k in range(1, N_BUCKETS)]


def _fill_bias_table(rb_ref, bias_s):
    si = lax.broadcasted_iota(jnp.int32, (2 * WINDOW, WINDOW), 0)
    qi = lax.broadcasted_iota(jnp.int32, (2 * WINDOW, WINDOW), 1)
    dist = qi + WINDOW - si
    bid = jnp.zeros((2 * WINDOW, WINDOW), jnp.int32)
    for thr in _bucket_thresholds():
        bid = bid + (dist >= thr).astype(jnp.int32)
    valid = (dist >= 0) & (dist <= WINDOW)
    own = si >= WINDOW

    def per_head(h, c):
        t = jnp.zeros((2 * WINDOW, WINDOW), F32)
        for b in range(N_BUCKETS):
            t = jnp.where(bid == b, rb_ref[b * N_HEADS + h], t)
        t = jnp.where(valid, t, NEG_INF)
        kh, j = h // KV_GROUP, h % KV_GROUP
        bias_s[0, kh, j] = t
        bias_s[1, kh, j] = jnp.where(own, t, NEG_INF)
        return c

    lax.fori_loop(0, N_HEADS, per_head, 0)


def _main_body(rb_ref, x_ref, yssm_ref, shift_ref, scale_ref, gate_ref, wmain_ref, wvt_ref, sink_ref, eye_ref,
               wglu_ref, bglu_ref, wbs_ref, wba_ref, wout_ref, lng_ref, lnb_ref,
               y_ref, ktail_ref, vtail_ref,
               zs_s, q_s, za_s, gs_s, ga_s, k_s, vt_s, ot_s, o_s, bias_s):
    i = pl.program_id(1)

    @pl.when((pl.program_id(0) == 0) & (i == 0))
    def _():
        _fill_bias_table(rb_ref, bias_s)

    x = x_ref[...]
    hb = (x * (1.0 + scale_ref[...]) + shift_ref[...]).astype(BF16)

    def project(dst, c0, c):
        dst[:, c * 512:(c + 1) * 512] = _dot(hb, wmain_ref[:, c0 + c * 512:c0 + (c + 1) * 512]).astype(BF16)

    for c in range(2):
        project(q_s, D_MODEL, c)
    deferred = [(dst, c0, c) for dst, c0 in ((zs_s, 0), (za_s, 2 * D_MODEL), (gs_s, 3 * D_MODEL), (ga_s, 4 * D_MODEL))
                for c in range(2)]

    @pl.when(i == 0)
    def _():
        k_s[0:WINDOW, :] = jnp.zeros((WINDOW, KV_WIDTH), BF16)
        vt_s[:, 0:WINDOW] = jnp.zeros((KV_WIDTH, WINDOW), BF16)

    k_s[WINDOW:WINDOW + TB, :] = _dot(hb, wmain_ref[:, 5 * D_MODEL:W_MAIN]).astype(BF16)
    vt_s[:, WINDOW:WINDOW + TB] = _dot_nt(wvt_ref[...], hb).astype(BF16)

    first = jnp.where(i == 0, 1, 0)
    lane = lax.broadcasted_iota(jnp.int32, (2 * WINDOW, 128), 1)
    ones = jnp.ones((ONES_ROWS, 2 * WINDOW), BF16)
    for blk in range(NQB):
        rows = slice(blk * WINDOW, (blk + 1) * WINDOW)
        kcat = k_s[blk * WINDOW:blk * WINDOW + 2 * WINDOW, :]
        vtc = vt_s[:, blk * WINDOW:blk * WINDOW + 2 * WINDOW]
        sel = first if blk == 0 else 0
        heads = range(N_KV_HEADS)
        sts = []
        for kh in heads:
            pair, odd = kh // 2, kh % 2
            kcm = jnp.where((lane >= HEAD_DIM) == bool(odd), kcat[:, pair * 128:(pair + 1) * 128], jnp.zeros((), BF16))
            qp = jnp.concatenate(
                [q_s[rows, j * KV_WIDTH + pair * 128:j * KV_WIDTH + (pair + 1) * 128] for j in range(KV_GROUP)],
                axis=0)
            bias = jnp.concatenate([bias_s[sel, kh, j] for j in range(KV_GROUP)], axis=1)
            sts.append(_dot_nt(kcm, qp) + bias)
        ms = [jnp.maximum(jnp.max(sts[kh], axis=0, keepdims=True), sink_ref[kh]) for kh in heads]
        es = [jnp.exp(sts[kh] - ms[kh]).astype(BF16) for kh in heads]
        rs = [_dot(jnp.concatenate([vtc[kh * HEAD_DIM:(kh + 1) * HEAD_DIM, :], ones], axis=0), es[kh])
              for kh in heads]
        for kh in heads:
            den = rs[kh][HEAD_DIM:HEAD_DIM + 1, :] + jnp.exp(sink_ref[kh] - ms[kh])
            ot = (rs[kh][:HEAD_DIM, :] / den).astype(BF16)
            for j in range(KV_GROUP):
                ot_s[blk, j, kh * HEAD_DIM:(kh + 1) * HEAD_DIM, :] = ot[:, j * WINDOW:(j + 1) * WINDOW]
        for spec in deferred[blk * len(deferred) // NQB:(blk + 1) * len(deferred) // NQB]:
            project(*spec)
        for j in range(KV_GROUP):
            o_s[rows, j * KV_WIDTH:(j + 1) * KV_WIDTH] = _dot_nt(eye_ref[...], ot_s[blk, j]).astype(BF16)

    @pl.when(i == pl.num_programs(1) - 1)
    def _():
        ktail_ref[...] = k_s[TB:TB + WINDOW, :].astype(F32)
        vtail_ref[...] = _dot_nt(eye_ref[...], vt_s[:, TB:TB + WINDOW])

    k_s[0:WINDOW, :] = k_s[TB:TB + WINDOW, :]
    vt_s[:, 0:WINDOW] = vt_s[:, TB:TB + WINDOW]

    for c in range(TB // POST_ROWS):
        rs = slice(c * POST_ROWS, (c + 1) * POST_ROWS)
        y_ref[rs, :] = _post(yssm_ref[rs, :], zs_s[rs, :], o_s[rs, :], za_s[rs, :], gs_s[rs, :], ga_s[rs, :],
                             x_ref[rs, :], gate_ref[...], wglu_ref, bglu_ref, wbs_ref, wba_ref, wout_ref,
                             lng_ref, lnb_ref)


def _main_prompt(rel_bias_flat, x2d, yssm, modp, wmain, wvt, sink_rows, eye, wglu, bglu, wbs, wba, wout, lng, lnb,
                 n_batch, seq):
    steps = seq // TB
    row_spec = pl.BlockSpec((TB, D_MODEL), lambda b, i: (b * steps + i, 0))
    mod_spec = lambda j: pl.BlockSpec((None, 1, D_MODEL), lambda b, i, j=j: (b * 3 + j, 0, 0))
    tail_spec = pl.BlockSpec((None, WINDOW, KV_WIDTH), lambda b, i: (b, 0, 0))
    consts = (wmain, wvt, sink_rows, eye, wglu, bglu, wbs, wba, wout, lng, lnb)
    return pl.pallas_call(
        _main_body,
        grid=(n_batch, steps),
        in_specs=[pl.BlockSpec(memory_space=pltpu.SMEM), row_spec, row_spec, mod_spec(0), mod_spec(1), mod_spec(2)]
        + [_const_spec(c.shape) for c in consts],
        out_specs=[row_spec, tail_spec, tail_spec],
        out_shape=[jax.ShapeDtypeStruct((n_batch * seq, D_MODEL), F32),
                   jax.ShapeDtypeStruct((n_batch, WINDOW, KV_WIDTH), F32),
                   jax.ShapeDtypeStruct((n_batch, WINDOW, KV_WIDTH), F32)],
        scratch_shapes=[pltpu.VMEM((TB, D_MODEL), BF16)] * 5
        + [pltpu.VMEM((TB + WINDOW, KV_WIDTH), BF16), pltpu.VMEM((KV_WIDTH, TB + WINDOW), BF16),
           pltpu.VMEM((NQB, KV_GROUP, KV_WIDTH, WINDOW), BF16), pltpu.VMEM((TB, D_MODEL), BF16),
           pltpu.VMEM((2, N_KV_HEADS, KV_GROUP, 2 * WINDOW, WINDOW), F32)],
        compiler_params=pltpu.CompilerParams(dimension_semantics=("arbitrary", "arbitrary"),
                                             vmem_limit_bytes=VMEM_LIMIT),
        name="main_prompt",
    )(rel_bias_flat, x2d, yssm, modp, modp, modp, *consts)


GT = 8


def _sample_ssm_body(u_ref, h0r_ref, h0i_ref, ar_ref, ai_ref, bre_ref, bim_ref, cblk_ref, d_ref,
                     y_ref, hr_ref, hi_ref):
    ub = u_ref[...]
    ys = []
    for a in range(GT):
        ua = ub[:, a * 128:(a + 1) * 128]
        st = slice(a * 512, (a + 1) * 512)
        xr = _dot(ua, bre_ref[a])
        xi = _dot(ua, bim_ref[a])
        h0r = h0r_ref[:, st]
        h0i = h0i_ref[:, st]
        ar = ar_ref[:, st]
        ai = ai_ref[:, st]
        hr = ar * h0r - ai * h0i + xr
        hi = ar * h0i + ai * h0r + xi
        hr_ref[:, st] = hr
        hi_ref[:, st] = hi
        hcat = jnp.concatenate([hr, hi], axis=1).astype(BF16)
        ys.append(_dot_nt(hcat, cblk_ref[a]))
    y_ref[...] = jnp.concatenate(ys, axis=1) + d_ref[...] * ub.astype(F32)


def _sample_ssm(proj_s, h0r, h0i, ar, ai, bre, bim, cblk, d_row):
    rows = proj_s.shape[0]
    ns = SSM_GROUPS * SSM_STATE
    full = lambda shape: pl.BlockSpec(shape, lambda i: (0,) * len(shape))
    return pl.pallas_call(
        _sample_ssm_body,
        grid=(1,),
        in_specs=[pl.BlockSpec((rows, D_MODEL), lambda i: (0, COL_U)),
                  full((rows, ns)), full((rows, ns)), full((1, ns)), full((1, ns)),
                  full(bre.shape), full(bim.shape), full(cblk.shape), full((1, D_MODEL))],
        out_specs=[full((rows, D_MODEL)), full((rows, ns)), full((rows, ns))],
        out_shape=[jax.ShapeDtypeStruct((rows, D_MODEL), F32),
                   jax.ShapeDtypeStruct((rows, ns), F32),
                   jax.ShapeDtypeStruct((rows, ns), F32)],
        compiler_params=pltpu.CompilerParams(vmem_limit_bytes=VMEM_LIMIT),
        name="sample_ssm",
    )(proj_s, h0r, h0i, ar, ai, bre, bim, cblk, d_row)


SB = 16
KROWS = WINDOW + 8
SEQ_GROUP = 4


def _sample_attn_body(q_ref, kn_ref, vn_ref, ck_ref, cv_ref, bias_ref, sink_ref, seg_ref, segt_ref,
                      o_ref, nk_ref, nv_ref):
    qf = q_ref[...].astype(F32)
    knf = kn_ref[...].astype(F32)
    vnf = vn_ref[...].astype(F32)
    rowk = lax.broadcasted_iota(jnp.int32, (KROWS, 128), 0)
    roww = lax.broadcasted_iota(jnp.int32, (WINDOW, KV_WIDTH), 0)
    sink = sink_ref[...]
    last = roww == WINDOW - 1
    for g0 in range(0, SB, SEQ_GROUP):
        seqs = range(g0, g0 + SEQ_GROUP)
        scores = []
        for b in seqs:
            kall = jnp.concatenate([ck_ref[b], jnp.broadcast_to(knf[b:b + 1], (8, KV_WIDTH))], axis=0)
            qrow = qf[b:b + 1]
            prod = jnp.concatenate([kall * qrow[:, j * KV_WIDTH:(j + 1) * KV_WIDTH] for j in range(KV_GROUP)], axis=1)
            s = _dot(prod.astype(BF16), seg_ref[...]) + bias_ref[...]
            scores.append(jnp.where(rowk <= WINDOW, s, NEG_INF))
        probs = []
        for s in scores:
            m = jnp.maximum(jnp.max(s, axis=0, keepdims=True), sink)
            e = jnp.exp(s - m)
            den = jnp.sum(e, axis=0, keepdims=True) + jnp.exp(sink - m)
            probs.append((e / den).astype(BF16))
        pexps = [_dot(p, segt_ref[...]) for p in probs]
        for b, pexp in zip(seqs, pexps):
            vb = cv_ref[b]
            vall = jnp.concatenate([vb, jnp.broadcast_to(vnf[b:b + 1], (8, KV_WIDTH))], axis=0)
            v4 = jnp.concatenate([vall] * KV_GROUP, axis=1)
            o_ref[b:b + 1, :] = jnp.sum(pexp * v4, axis=0, keepdims=True)
            nv_ref[b] = jnp.where(last, jnp.broadcast_to(vnf[b:b + 1], (WINDOW, KV_WIDTH)), pltpu.roll(vb, WINDOW - 1, 0))
            nk_ref[b] = jnp.where(last, jnp.broadcast_to(knf[b:b + 1], (WINDOW, KV_WIDTH)),
                                  pltpu.roll(ck_ref[b], WINDOW - 1, 0))


def _sample_attn(proj_s, ck, cv, bias_s, sink_row, seg, segt):
    rows = proj_s.shape[0]
    cache_spec = pl.BlockSpec((SB, WINDOW, KV_WIDTH), lambda i: (i, 0, 0))
    return pl.pallas_call(
        _sample_attn_body,
        grid=(rows // SB,),
        in_specs=[pl.BlockSpec((SB, D_MODEL), lambda i: (i, COL_Q)),
                  pl.BlockSpec((SB, KV_WIDTH), lambda i: (i, COL_K)),
                  pl.BlockSpec((SB, KV_WIDTH), lambda i: (i, COL_V)),
                  cache_spec, cache_spec,
                  pl.BlockSpec(bias_s.shape, lambda i: (0, 0)),
                  pl.BlockSpec(sink_row.shape, lambda i: (0, 0)),
                  pl.BlockSpec(seg.shape, lambda i: (0, 0)),
                  pl.BlockSpec(segt.shape, lambda i: (0, 0))],
        out_specs=[pl.BlockSpec((SB, D_MODEL), lambda i: (i, 0)), cache_spec, cache_spec],
        out_shape=[jax.ShapeDtypeStruct((rows, D_MODEL), F32),
                   jax.ShapeDtypeStruct(ck.shape, F32),
                   jax.ShapeDtypeStruct(cv.shape, F32)],
        compiler_params=pltpu.CompilerParams(vmem_limit_bytes=VMEM_LIMIT),
        name="sample_attn",
    )(proj_s, proj_s, proj_s, ck, cv, bias_s, sink_row, seg, segt)


def _sample_final_body(x_ref, yssm_ref, zs_ref, o_ref, za_ref, gs_ref, ga_ref, gate_ref,
                       wglu_ref, bglu_ref, wbs_ref, wba_ref, wout_ref, lng_ref, lnb_ref, y_ref):
    y_ref[...] = _post(yssm_ref[...], zs_ref[...], o_ref[...], za_ref[...], gs_ref[...], ga_ref[...],
                       x_ref[...], gate_ref[...], wglu_ref, bglu_ref, wbs_ref, wba_ref, wout_ref,
                       lng_ref, lnb_ref)


def _final_sample(x2d, yssm, o_a, proj_s, mods, wglu, bglu, wbs, wba, wout, lng, lnb):
    rows = x2d.shape[0]
    full = lambda shape: pl.BlockSpec(shape, lambda i: (0,) * len(shape))
    pcol = lambda c: pl.BlockSpec((rows, D_MODEL), lambda i, c=c: (0, c))
    return pl.pallas_call(
        _sample_final_body,
        grid=(1,),
        in_specs=[full((rows, D_MODEL)), full((rows, D_MODEL)), pcol(COL_ZS), full((rows, D_MODEL)),
                  pcol(COL_ZA), pcol(COL_GS), pcol(COL_GA),
                  pl.BlockSpec((rows, D_MODEL), lambda i: (0, 2)),
                  full(wglu.shape), full(bglu.shape), full(wbs.shape), full(wba.shape), full(wout.shape),
                  full(lng.shape), full(lnb.shape)],
        out_specs=full((rows, D_MODEL)),
        out_shape=jax.ShapeDtypeStruct((rows, D_MODEL), F32),
        compiler_params=pltpu.CompilerParams(vmem_limit_bytes=VMEM_LIMIT),
        name="final_sample",
    )(x2d, yssm, proj_s, o_a, proj_s, proj_s, proj_s, mods, wglu, bglu, wbs, wba, wout, lng, lnb)


def _bucket_table():
    max_exact = N_BUCKETS // 2
    dist = np.arange(WINDOW + 1)
    df = np.maximum(dist, 1).astype(np.float32)
    large = max_exact + (np.log(df / np.float32(max_exact)) / np.float32(math.log(WINDOW / max_exact))
                         * np.float32(N_BUCKETS - max_exact)).astype(np.int32)
    large = np.minimum(large, N_BUCKETS - 1)
    return np.where(dist < max_exact, dist, large)


def _dot3_nt(a, b):
    ah = a.astype(BF16)
    al = (a - ah.astype(F32)).astype(BF16)
    bh = b.astype(BF16)
    bl = (b - bh.astype(F32)).astype(BF16)
    return _dot_nt(ah, bh) + _dot_nt(ah, bl) + _dot_nt(al, bh)


def _ssm_prep_body(p1_ref, p2_ref, kf_ref, bt_ref, bts_ref, c2_ref, c2s_ref, d_ref,
                   lhs_ref, bpow_ref, cpow_ref, bre_ref, bim_ref, cblk_ref):
    lane = lax.broadcasted_iota(jnp.int32, (SSM_GROUP, SW), 1)
    left = lane < SSM_STATE
    sgn = jnp.where(left, 1.0, -1.0)
    lane_blk = lax.broadcasted_iota(jnp.int32, (CW, CW), 1) // SSM_GROUP
    diag = lax.broadcasted_iota(jnp.int32, (CW, CW), 0) == lax.broadcasted_iota(jnp.int32, (CW, CW), 1)
    bre_ref[...] = jnp.zeros_like(bre_ref)
    bim_ref[...] = jnp.zeros_like(bim_ref)
    cblk_ref[...] = jnp.zeros_like(cblk_ref)
    for gl in range(GT):
        p1, p2 = p1_ref[gl], p2_ref[gl]
        k1, k2 = kf_ref[gl, 0:1, :], kf_ref[gl, 1:2, :]
        bt, bts = bt_ref[gl], bts_ref[gl]
        c2, c2s = c2_ref[gl], c2s_ref[gl]
        bb = k1 * bt + k2 * bts
        bbs = k1 * bts - k2 * bt
        ca = [sgn * (c2 * p1[t:t + 1, :] + c2s * p2[t:t + 1, :]) for t in range(CHUNK + 1)]
        cpow_ref[gl] = jnp.concatenate(ca[1:], axis=0).astype(BF16)
        bpow_ref[gl] = jnp.concatenate(
            [p1[CHUNK - 1 - s:CHUNK - s, :] * bb + p2[CHUNK - 1 - s:CHUNK - s, :] * bbs for s in range(CHUNK)],
            axis=0).astype(BF16)
        e = _dot3_nt(jnp.concatenate(ca[:CHUNK], axis=0), jnp.concatenate([bb] * CHUNK, axis=0))
        acc = jnp.where(diag, d_ref[gl], 0.0)
        for s in range(CHUNK):
            sh = e if s == 0 else jnp.concatenate([jnp.zeros((s * SSM_GROUP, CW), F32), e[:CW - s * SSM_GROUP, :]], axis=0)
            acc = acc + jnp.where(lane_blk == s, sh, 0.0)
        lhs_ref[gl] = acc.astype(BF16)
        rs = slice(gl * SSM_GROUP, (gl + 1) * SSM_GROUP)
        ts = slice((gl // 2) * SW, (gl // 2 + 1) * SW)
        ti = slice(GT * SSM_STATE + (gl // 2) * SW, GT * SSM_STATE + (gl // 2 + 1) * SW)
        if gl % 2 == 0:
            bre_ref[rs, ts] = jnp.where(left, bb, 0.0).astype(BF16)
            bim_ref[rs, ts] = jnp.where(left, bbs, 0.0).astype(BF16)
            cblk_ref[rs, ts] = jnp.where(left, c2, 0.0).astype(BF16)
            cblk_ref[rs, ti] = jnp.where(left, -c2s, 0.0).astype(BF16)
        else:
            bre_ref[rs, ts] = jnp.where(left, 0.0, bbs).astype(BF16)
            bim_ref[rs, ts] = jnp.where(left, 0.0, bb).astype(BF16)
            cblk_ref[rs, ts] = jnp.where(left, 0.0, c2s).astype(BF16)
            cblk_ref[rs, ti] = jnp.where(left, 0.0, -c2).astype(BF16)


def _ssm_params(lam_re, lam_im, log_delta, b_re, b_im, c_re, c_im, d_skip):
    g = SSM_GROUPS
    dup = lambda a: jnp.concatenate([a, a], axis=-1)
    lr, li = dup(lam_re.astype(F32)), dup(lam_im.astype(F32))
    dt = jnp.exp(log_delta.astype(F32))[:, None]
    half_sign = jnp.asarray(np.where(np.arange(SW) < SSM_STATE, -1.0, 1.0), F32)
    tau = jnp.asarray(POW_TAUS, F32)[None, :, None]
    mag = jnp.exp((lr * dt)[:, None, :] * tau)
    ang = (li * dt)[:, None, :] * tau
    p1 = mag * jnp.cos(ang)
    pim = mag * jnp.sin(ang)
    p2 = pim * half_sign
    ar, ai = p1[:, 1, :], pim[:, 1, :]
    den = lr * lr + li * li
    nr = ar - 1.0
    k1 = (nr * lr + ai * li) / den
    k2 = (ai * lr - nr * li) / den * half_sign
    kf = jnp.concatenate([k1[:, None, :], k2[:, None, :], jnp.zeros((g, 6, SW), F32)], axis=1)
    brt, bit = jnp.swapaxes(b_re, 1, 2), jnp.swapaxes(b_im, 1, 2)
    bt = jnp.concatenate([brt, bit], axis=-1)
    bts = jnp.concatenate([bit, brt], axis=-1)
    c2 = jnp.concatenate([c_re, c_im], axis=-1)
    c2s = jnp.concatenate([c_im, c_re], axis=-1)
    dtile = jnp.tile(d_skip.reshape(g, 1, SSM_GROUP), (1, 1, CHUNK))

    gspec = lambda r, w: pl.BlockSpec((GT, r, w), lambda a: (a, 0, 0))
    tile_spec = lambda w: pl.BlockSpec((None, GT * SSM_GROUP, w), lambda a: (a, 0, 0))
    nrow = len(POW_TAUS)
    lhs, bpow, cpow, bre, bim, cblk = pl.pallas_call(
        _ssm_prep_body,
        grid=(g // GT,),
        in_specs=[gspec(nrow, SW), gspec(nrow, SW), gspec(8, SW), gspec(SSM_GROUP, SW), gspec(SSM_GROUP, SW),
                  gspec(SSM_GROUP, SW), gspec(SSM_GROUP, SW), gspec(1, CW)],
        out_specs=[gspec(CW, CW), gspec(CW, SW), gspec(CW, SW),
                   tile_spec(GT * SSM_STATE), tile_spec(GT * SSM_STATE), tile_spec(2 * GT * SSM_STATE)],
        out_shape=[jax.ShapeDtypeStruct((g, CW, CW), BF16), jax.ShapeDtypeStruct((g, CW, SW), BF16),
                   jax.ShapeDtypeStruct((g, CW, SW), BF16),
                   jax.ShapeDtypeStruct((GT, GT * SSM_GROUP, GT * SSM_STATE), BF16),
                   jax.ShapeDtypeStruct((GT, GT * SSM_GROUP, GT * SSM_STATE), BF16),
                   jax.ShapeDtypeStruct((GT, GT * SSM_GROUP, 2 * GT * SSM_STATE), BF16)],
        name="ssm_prep",
    )(p1, p2, kf, bt, bts, c2, c2s, dtile)
    ar_row = ar[:, :SSM_STATE].reshape(1, -1)
    ai_row = ai[:, :SSM_STATE].reshape(1, -1)
    return lhs, bpow, cpow, p1, p2, ar_row, ai_row, bre, bim, cblk


def _attn_params(rel_bias, sinks):
    hp = lax.Precision.HIGHEST
    rb = rel_bias.astype(F32)
    onehot = jnp.asarray(_bucket_table()[:, None] == np.arange(N_BUCKETS)[None, :], F32)
    tbl = jnp.dot(onehot, rb, precision=hp)
    sink_rows = jnp.repeat(sinks.astype(F32).reshape(N_KV_HEADS, 1, KV_GROUP), WINDOW, axis=2)
    ds = np.clip(WINDOW - np.arange(KROWS), 0, WINDOW)
    bs = jnp.dot(jnp.asarray(ds[:, None] == np.arange(WINDOW + 1)[None, :], F32), tbl, precision=hp)
    bs = jnp.transpose(bs.reshape(KROWS, N_KV_HEADS, KV_GROUP), (0, 2, 1)).reshape(KROWS, N_HEADS)
    bias_s = jnp.pad(bs, ((0, 0), (0, 128 - N_HEADS)))
    sk = jnp.transpose(sinks.astype(F32).reshape(N_KV_HEADS, KV_GROUP), (1, 0)).reshape(1, N_HEADS)
    sink_row = jnp.pad(sk, ((0, 0), (0, 128 - N_HEADS)))
    seg_np = np.zeros((D_MODEL, 128), np.float32)
    for j in range(KV_GROUP):
        for kh in range(N_KV_HEADS):
            r0 = j * KV_WIDTH + kh * HEAD_DIM
            seg_np[r0:r0 + HEAD_DIM, j * N_KV_HEADS + kh] = 1.0
    seg = jnp.asarray(seg_np, BF16)
    segt = jnp.asarray(seg_np.T, BF16)
    return sink_rows, bias_s, sink_row, seg, segt


def kernel(x_prompt, x_sample, c_prompt, c_sample, state_ssm_re, state_ssm_im, cache_swa_k, cache_swa_v,
           w_ada, b_ada, w_in, ssm_lambda_re, ssm_lambda_im, ssm_log_delta, ssm_b_re, ssm_b_im,
           ssm_c_re, ssm_c_im, ssm_d, w_glu, b_glu, attn_sinks, rel_bias, w_branch_s, w_branch_a,
           w_out, ln_g, ln_b):
    assert w_ada.shape[0] == 1, "single-layer trunk"
    n_batch, seq, _ = x_prompt.shape
    n_dec = x_sample.shape[0]
    perm = _head_perm()

    wi = w_in[0]
    w_in_b = jnp.concatenate([wi[:, 0:1024], wi[:, 1024:2048], wi[:, 2048:3072][:, perm] * (HEAD_DIM ** -0.5),
                              wi[:, 3584:4608][:, perm], wi[:, 4608:5632], wi[:, 5632:6656],
                              wi[:, 3072:3328], wi[:, 3328:3584]], axis=1).astype(BF16)
    wmain = w_in_b[:, D_MODEL:D_MODEL + W_MAIN]
    wvt = jnp.transpose(wi[:, 3328:3584]).astype(BF16)
    wut = jnp.transpose(wi[:, 0:D_MODEL]).astype(BF16)
    eye = jnp.eye(WINDOW, dtype=BF16)
    wglu = w_glu[0].astype(BF16)
    bglu = b_glu[0].reshape(1, -1).astype(F32)
    wbs = w_branch_s[0].astype(BF16)
    wba = w_branch_a[0][perm, :].astype(BF16)
    wout = w_out[0].astype(BF16)
    lng = ln_g[0].reshape(1, -1).astype(F32)
    lnb = ln_b[0].reshape(1, -1).astype(F32)
    lhs, bpow, cpow, p1, p2, ar_row, ai_row, bre, bim, cblk = _ssm_params(
        ssm_lambda_re[0], ssm_lambda_im[0], ssm_log_delta[0], ssm_b_re[0], ssm_b_im[0],
        ssm_c_re[0], ssm_c_im[0], ssm_d[0])
    sink_rows, bias_s, sink_row, seg, segt = _attn_params(rel_bias, attn_sinks[0])

    c_all = jnp.concatenate([c_prompt, jnp.zeros((8 - n_batch, D_MODEL), F32), c_sample], axis=0)
    mod = _ada(c_all, w_ada[0], b_ada[0])
    modp = mod[:n_batch].reshape(n_batch * 3, 1, D_MODEL)
    mods = mod[8:8 + n_dec]

    xp = x_prompt.reshape(n_batch * seq, D_MODEL)
    yssm, hfin = _ssm_prompt(x_prompt, modp, wut, lhs, bpow, cpow, p1, p2)
    yp, ktail, vtail = _main_prompt(rel_bias.astype(F32).reshape(-1), xp, yssm, modp, wmain, wvt, sink_rows, eye,
                                    wglu, bglu, wbs, wba, wout, lng, lnb, n_batch, seq)
    yp = yp.reshape(n_batch, seq, D_MODEL)
    p_hr = hfin[:, :, 0, :SSM_STATE][None]
    p_hi = hfin[:, :, 0, SSM_STATE:][None]
    p_k = ktail.reshape(1, n_batch, WINDOW, N_KV_HEADS, HEAD_DIM)
    p_v = vtail.reshape(1, n_batch, WINDOW, N_KV_HEADS, HEAD_DIM)

    xs = x_sample.reshape(n_dec, D_MODEL)
    proj_s = _proj(xs, mods, mods,
                   pl.BlockSpec((n_dec, D_MODEL), lambda i: (0, 0)),
                   pl.BlockSpec((n_dec, D_MODEL), lambda i: (0, 1)),
                   w_in_b, n_dec, D_IN)
    ns = SSM_GROUPS * SSM_STATE
    ys_s, s_hr, s_hi = _sample_ssm(proj_s, state_ssm_re[0].reshape(n_dec, ns), state_ssm_im[0].reshape(n_dec, ns),
                                   ar_row, ai_row, bre, bim, cblk, ssm_d[0].reshape(1, -1).astype(F32))
    o_s, nk, nv = _sample_attn(proj_s, cache_swa_k[0].reshape(n_dec, WINDOW, KV_WIDTH),
                               cache_swa_v[0].reshape(n_dec, WINDOW, KV_WIDTH), bias_s, sink_row, seg, segt)
    ysmp = _final_sample(xs, ys_s, o_s, proj_s, mods, wglu, bglu, wbs, wba, wout, lng, lnb)

    return (yp, ysmp.reshape(n_dec, 1, D_MODEL),
            p_hr, p_hi, p_k, p_v,
            s_hr.reshape(1, n_dec, SSM_GROUPS, SSM_STATE), s_hi.reshape(1, n_dec, SSM_GROUPS, SSM_STATE),
            nk.reshape(1, n_dec, WINDOW, N_KV_HEADS, HEAD_DIM), nv.reshape(1, n_dec, WINDOW, N_KV_HEADS, HEAD_DIM))
```

```python
import functools
import math

import numpy as np
import jax
import jax.numpy as jnp
from jax import lax
from jax.experimental import pallas as pl
from jax.experimental.pallas import tpu as pltpu

F32 = jnp.float32
BF16 = jnp.bfloat16

D_MODEL = 1024
SSM_GROUPS = 64
SSM_GROUP = 16
SSM_STATE = 64
N_HEADS = 16
HEAD_DIM = 64
N_KV_HEADS = 4
KV_GROUP = 4
KV_WIDTH = N_KV_HEADS * HEAD_DIM
WINDOW = 128
N_BUCKETS = 32
NEG_INF = -1e30
LN_EPS = 1e-5
DEPTH = 1
DEEPNORM_ALPHA = (2 * DEPTH) ** 0.25
D_IN = 6656
CHUNK = 16
CW = CHUNK * SSM_GROUP
SW = 2 * SSM_STATE

COL_U, COL_ZS, COL_Q, COL_ZA, COL_GS, COL_GA = 0, 1, 2, 3, 4, 5
COL_K, COL_V = 24, 25

VMEM_LIMIT = 56 * 1024 * 1024
VMEM_LIMIT_SSM = 60 * 1024 * 1024


def _regroup_heads(w, axis):
    shp = w.shape
    if axis == 1:
        return jnp.transpose(w.reshape(shp[0], N_KV_HEADS, KV_GROUP, HEAD_DIM), (0, 2, 1, 3)).reshape(shp)
    return jnp.transpose(w.reshape(N_KV_HEADS, KV_GROUP, HEAD_DIM, shp[1]), (1, 0, 2, 3)).reshape(shp)


def _sigmoid(x):
    return 0.5 * jnp.tanh(0.5 * x) + 0.5


def _silu(x):
    return x * _sigmoid(x)


def _gelu_tanh(x):
    c = math.sqrt(2.0 / math.pi)
    hx = 0.5 * x
    return hx + hx * jnp.tanh(x * (c + (c * 0.044715) * (x * x)))


def _dot(a, b):
    return jnp.dot(a, b, preferred_element_type=F32)


def _dot_nt(a, b):
    return lax.dot_general(a, b, (((1,), (1,)), ((), ())), preferred_element_type=F32)


def _ada_body(c_ref, w_ref, b_ref, o_ref):
    c = c_ref[...]
    sc = _silu(c).astype(BF16)
    o_ref[...] = _dot(sc, w_ref[...].astype(BF16)) + b_ref[...]


def _ada(c_all, w_ada, b_ada):
    rows = c_all.shape[0]
    bn = 512
    return pl.pallas_call(
        _ada_body,
        grid=(3 * D_MODEL // bn,),
        in_specs=[pl.BlockSpec((rows, D_MODEL), lambda j: (0, 0)),
                  pl.BlockSpec((D_MODEL, bn), lambda j: (0, j)),
                  pl.BlockSpec((1, bn), lambda j: (0, j))],
        out_specs=pl.BlockSpec((rows, bn), lambda j: (0, j)),
        out_shape=jax.ShapeDtypeStruct((rows, 3 * D_MODEL), F32),
        name="ada",
    )(c_all, w_ada, b_ada.reshape(1, -1))


PROJ_CB = 512


def _proj_body(x_ref, shift_ref, scale_ref, w_ref, o_ref, *, n_out):
    h = x_ref[...] * (1.0 + scale_ref[...]) + shift_ref[...]
    hb = h.astype(BF16)
    for j in range(n_out // PROJ_CB):
        sl = slice(j * PROJ_CB, (j + 1) * PROJ_CB)
        o_ref[:, sl] = _dot(hb, w_ref[:, sl]).astype(BF16)


def _proj(x2d, shift_arr, scale_arr, shift_spec, scale_spec, w_in_b, tm, n_out):
    rows = x2d.shape[0]
    return pl.pallas_call(
        functools.partial(_proj_body, n_out=n_out),
        grid=(rows // tm,),
        in_specs=[pl.BlockSpec((tm, D_MODEL), lambda i: (i, 0)),
                  shift_spec, scale_spec,
                  pl.BlockSpec((D_MODEL, n_out), lambda i: (0, 0), pipeline_mode=pl.Buffered(1))],
        out_specs=pl.BlockSpec((tm, n_out), lambda i: (i, 0)),
        out_shape=jax.ShapeDtypeStruct((rows, n_out), BF16),
        compiler_params=pltpu.CompilerParams(vmem_limit_bytes=VMEM_LIMIT),
        name="proj",
    )(x2d, shift_arr, scale_arr, w_in_b)


NKS = 128
GC = 16


def _cmul(c1, c2, x):
    return c1 * x + c2 * pltpu.roll(x, SSM_STATE, x.ndim - 1)


SSM_TOK = NKS * CHUNK
SUB = 8
NPH = CHUNK // SUB
GROUP_UNROLL = 16


def _dot_tn(a, b):
    return lax.dot_general(a, b, (((0,), (0,)), ((), ())), preferred_element_type=F32)


SEG = 8
SEG_LEN = NKS // SEG
POW_TAUS = (tuple(range(CHUNK + 1)) + tuple(CHUNK * m for m in range(2, SEG_LEN))
            + tuple(CHUNK * SEG_LEN * m for m in range(1, SEG + 1)) + (0,))
ROW_CHUNK1 = CHUNK
ROW_SEG1 = CHUNK + SEG_LEN - 1


def _ssm_body(x_ref, shift_ref, scale_ref, wut_ref, lhs_ref, bpow_ref, cpow_ref, p1_ref, p2_ref, unperm_ref,
              y_ref, hfin_ref, r_scr, yt_scr, s_scr, carry_scr):
    i = pl.program_id(1)
    j = pl.program_id(2)

    @pl.when((i == 0) & (j == 0))
    def _():
        carry_scr[...] = jnp.zeros_like(carry_scr)

    @pl.when(j < NPH)
    def _():
        for sp in range(SUB // 2):
            h = jnp.concatenate([x_ref[:, b, s, :] for s in (2 * sp, 2 * sp + 1) for b in range(SEG_LEN)],
                                axis=0)
            hb = (h * (1.0 + scale_ref[...]) + shift_ref[...]).astype(BF16)
            ut = _dot_nt(wut_ref[...], hb).astype(BF16)
            ut = ut.reshape(SSM_GROUPS, SSM_GROUP, 2 * NKS)
            pair = jnp.concatenate([ut[:, :, :NKS], ut[:, :, NKS:]], axis=1)
            r0 = pl.multiple_of(j * SUB * SSM_GROUP + sp * 2 * SSM_GROUP, 2 * SSM_GROUP)
            r_scr[:, pl.ds(r0, 2 * SSM_GROUP), :] = pair

    @pl.when(j == NPH - 1)
    def _():
        def local(g, c):
            rg = r_scr[g]
            yt_scr[g] = _dot(lhs_ref[g], rg)
            s_scr[g] = _dot_tn(rg, bpow_ref[g])
            return c

        lax.fori_loop(0, SSM_GROUPS, local, 0, unroll=GROUP_UNROLL)

        row = lax.broadcasted_iota(jnp.int32, (GC, SEG, SW), 1)
        seg_rows = lambda b: slice(b * SEG, (b + 1) * SEG)
        for gc in range(SSM_GROUPS // GC):
            gs = slice(gc * GC, (gc + 1) * GC)
            p1, p2 = p1_ref[gs], p2_ref[gs]
            c1, c2 = p1[:, ROW_CHUNK1:ROW_CHUNK1 + 1, :], p2[:, ROW_CHUNK1:ROW_CHUNK1 + 1, :]
            h = jnp.zeros((GC, SEG, SW), F32)
            hs = h
            for b in range(SEG_LEN):
                sb = s_scr[gs, seg_rows(b), :]
                h, hs = sb + c1 * h + c2 * hs, pltpu.roll(sb, SSM_STATE, 2) + c1 * hs - c2 * h
                s_scr[gs, seg_rows(b), :] = h
            e = h
            for d in (1, 2, 4):
                rd = ROW_SEG1 + d - 1
                sh = jnp.where(row >= d, pltpu.roll(e, d, 1), 0.0)
                e = e + _cmul(p1[:, rd:rd + 1, :], p2[:, rd:rd + 1, :], sh)
            carry = carry_scr[gs]
            e = e + _cmul(p1[:, ROW_SEG1:ROW_SEG1 + SEG, :], p2[:, ROW_SEG1:ROW_SEG1 + SEG, :], carry)
            ein = jnp.where(row == 0, carry, pltpu.roll(e, 1, 1))
            eins = pltpu.roll(ein, SSM_STATE, 2)
            carry_scr[gs] = jnp.broadcast_to(e[:, SEG - 1:SEG, :], (GC, SEG, SW))
            for b in range(SEG_LEN - 1, 0, -1):
                rb = ROW_CHUNK1 + b - 1
                s_scr[gs, seg_rows(b), :] = (s_scr[gs, seg_rows(b - 1), :]
                                             + p1[:, rb:rb + 1, :] * ein + p2[:, rb:rb + 1, :] * eins)
            s_scr[gs, seg_rows(0), :] = ein
        hfin_ref[...] = carry_scr[...]

        def carried(g, c):
            yt_scr[g] += _dot_nt(cpow_ref[g], s_scr[g].astype(BF16))
            return c

        lax.fori_loop(0, SSM_GROUPS, carried, 0, unroll=GROUP_UNROLL)

    @pl.when(j >= NPH)
    def _():
        for sp in range(SUB):
            t0 = pl.multiple_of((j - NPH) * SUB * SSM_GROUP + sp * SSM_GROUP, SSM_GROUP)
            yt = yt_scr[:, pl.ds(t0, SSM_GROUP), :].reshape(D_MODEL, NKS)
            hi = yt.astype(BF16)
            lo = (yt - hi.astype(F32)).astype(BF16)
            y_ref[:, sp, :] = _dot_nt(unperm_ref[...], hi) + _dot_nt(unperm_ref[...], lo)


def _ssm_prompt(x_prompt, modp, wut, lhs, bpow, cpow, p1, p2):
    n_batch, seq, _ = x_prompt.shape
    steps = seq // SSM_TOK
    xv = x_prompt.reshape(n_batch * steps * SEG, SEG_LEN, CHUNK, D_MODEL)
    mod_spec = lambda k: pl.BlockSpec((None, 1, D_MODEL), lambda b, i, j, k=k: (b * 3 + k, 0, 0))
    k = np.arange(NKS)
    perm_np = np.zeros((NKS, NKS), np.float32)
    perm_np[k, (k % SEG_LEN) * SEG + k // SEG_LEN] = 1.0
    consts = (wut, lhs, bpow, cpow, p1, p2, jnp.asarray(perm_np, BF16))
    y, hfin = pl.pallas_call(
        _ssm_body,
        grid=(n_batch, steps, 2 * NPH),
        in_specs=[pl.BlockSpec((SEG, SEG_LEN, SUB, D_MODEL),
                               lambda b, i, j: (b * steps + i, 0, jnp.minimum(j, NPH - 1), 0)),
                  mod_spec(0), mod_spec(1)] + [_const_spec(c.shape) for c in consts],
        out_specs=[pl.BlockSpec((NKS, SUB, D_MODEL), lambda b, i, j: (b * steps + i, jnp.maximum(j - NPH, 0), 0)),
                   pl.BlockSpec((None, SSM_GROUPS, 8, SW), lambda b, i, j: (b, 0, 0, 0))],
        out_shape=[jax.ShapeDtypeStruct((n_batch * seq // CHUNK, CHUNK, D_MODEL), F32),
                   jax.ShapeDtypeStruct((n_batch, SSM_GROUPS, 8, SW), F32)],
        scratch_shapes=[pltpu.VMEM((SSM_GROUPS, CW, NKS), BF16), pltpu.VMEM((SSM_GROUPS, CW, NKS), F32),
                        pltpu.VMEM((SSM_GROUPS, NKS, SW), F32), pltpu.VMEM((SSM_GROUPS, 8, SW), F32)],
        compiler_params=pltpu.CompilerParams(dimension_semantics=("arbitrary", "arbitrary", "arbitrary"),
                                             vmem_limit_bytes=VMEM_LIMIT_SSM),
        name="ssm_prompt",
    )(xv, modp, modp, *consts)
    return y.reshape(n_batch * seq, D_MODEL), hfin


def _post(ys, zs, o_a, za, gs, ga, x, gate, wglu_ref, bglu_ref, wbs_ref, wba_ref, wout_ref, lng_ref, lnb_ref):
    ys = _gelu_tanh(ys)
    glu = ys * _sigmoid(_dot(ys.astype(BF16), wglu_ref[...]) + bglu_ref[...])
    t = glu.astype(BF16) * _silu(zs)
    b_s = _dot(t, wbs_ref[...])
    oz = o_a.astype(BF16) * _silu(za)
    b_a = _dot(oz, wba_ref[...])
    m = _sigmoid(gs).astype(F32) * b_s + _sigmoid(ga).astype(F32) * b_a
    out = _dot(m.astype(BF16), wout_ref[...])
    r = DEEPNORM_ALPHA * x + gate * out
    mu = jnp.mean(r, axis=-1, keepdims=True)
    rc = r - mu
    var = jnp.mean(rc * rc, axis=-1, keepdims=True)
    return rc * lax.rsqrt(var + LN_EPS) * lng_ref[...] + lnb_ref[...]


TB = 512
NQB = TB // WINDOW
W_OFF = D_MODEL
W_MAIN = 5 * D_MODEL + KV_WIDTH
ONES_ROWS = 16
POST_ROWS = 256


def _const_spec(shape):
    nd = len(shape)
    return pl.BlockSpec(shape, lambda *_: (0,) * nd, pipeline_mode=pl.Buffered(1))


def _bucket_thresholds():
    tbl = _bucket_table()
    return [int(np.argmax(tbl >= k)) for k in range(1, N_BUCKETS)]


def _fill_bias_table(rb_ref, bias_s):
    si = lax.broadcasted_iota(jnp.int32, (2 * WINDOW, WINDOW), 0)
    qi = lax.broadcasted_iota(jnp.int32, (2 * WINDOW, WINDOW), 1)
    dist = qi + WINDOW - si
    bid = jnp.zeros((2 * WINDOW, WINDOW), jnp.int32)
    for thr in _bucket_thresholds():
        bid = bid + (dist >= thr).astype(jnp.int32)
    valid = (dist >= 0) & (dist <= WINDOW)
    own = si >= WINDOW

    def per_head(h, c):
        t = jnp.zeros((2 * WINDOW, WINDOW), F32)
        for b in range(N_BUCKETS):
            t = jnp.where(bid == b, rb_ref[b * N_HEADS + h], t)
        t = jnp.where(valid, t, NEG_INF)
        kh, j = h // KV_GROUP, h % KV_GROUP
        bias_s[0, kh, j] = t
        bias_s[1, kh, j] = jnp.where(own, t, NEG_INF)
        return c

    lax.fori_loop(0, N_HEADS, per_head, 0)


def _main_body(rb_ref, x_ref, yssm_ref, shift_ref, scale_ref, gate_ref, wmain_ref, wvt_ref, sink_ref, eye_ref,
               wglu_ref, bglu_ref, wbs_ref, wba_ref, wout_ref, lng_ref, lnb_ref,
               y_ref, ktail_ref, vtail_ref,
               zs_s, q_s, za_s, gs_s, ga_s, k_s, vt_s, ot_s, o_s, bias_s):
    i = pl.program_id(1)

    @pl.when((pl.program_id(0) == 0) & (i == 0))
    def _():
        _fill_bias_table(rb_ref, bias_s)

    x = x_ref[...]
    hb = (x * (1.0 + scale_ref[...]) + shift_ref[...]).astype(BF16)

    def project(dst, c0, c):
        lo = W_OFF + c0 + c * 512
        dst[:, c * 512:(c + 1) * 512] = _dot(hb, wmain_ref[:, lo:lo + 512]).astype(BF16)

    for c in range(2):
        project(q_s, D_MODEL, c)
    deferred = [(dst, c0, c) for dst, c0 in ((zs_s, 0), (za_s, 2 * D_MODEL), (gs_s, 3 * D_MODEL), (ga_s, 4 * D_MODEL))
                for c in range(2)]

    @pl.when(i == 0)
    def _():
        k_s[0:WINDOW, :] = jnp.zeros((WINDOW, KV_WIDTH), BF16)
        vt_s[:, 0:WINDOW] = jnp.zeros((KV_WIDTH, WINDOW), BF16)

    k_s[WINDOW:WINDOW + TB, :] = _dot(hb, wmain_ref[:, W_OFF + 5 * D_MODEL:W_OFF + W_MAIN]).astype(BF16)
    vt_s[:, WINDOW:WINDOW + TB] = _dot_nt(wvt_ref[...], hb).astype(BF16)

    first = jnp.where(i == 0, 1, 0)
    lane = lax.broadcasted_iota(jnp.int32, (2 * WINDOW, 128), 1)
    ones = jnp.ones((ONES_ROWS, 2 * WINDOW), BF16)
    for blk in range(NQB):
        rows = slice(blk * WINDOW, (blk + 1) * WINDOW)
        kcat = k_s[blk * WINDOW:blk * WINDOW + 2 * WINDOW, :]
        vtc = vt_s[:, blk * WINDOW:blk * WINDOW + 2 * WINDOW]
        sel = first if blk == 0 else 0
        heads = range(N_KV_HEADS)
        sts = []
        for kh in heads:
            pair, odd = kh // 2, kh % 2
            kcm = jnp.where((lane >= HEAD_DIM) == bool(odd), kcat[:, pair * 128:(pair + 1) * 128], jnp.zeros((), BF16))
            qp = jnp.concatenate(
                [q_s[rows, j * KV_WIDTH + pair * 128:j * KV_WIDTH + (pair + 1) * 128] for j in range(KV_GROUP)],
                axis=0)
            bias = jnp.concatenate([bias_s[sel, kh, j] for j in range(KV_GROUP)], axis=1)
            sts.append(_dot_nt(kcm, qp) + bias)
        mine = deferred[blk * len(deferred) // NQB:(blk + 1) * len(deferred) // NQB]
        ms = [jnp.maximum(jnp.max(sts[kh], axis=0, keepdims=True), sink_ref[kh]) for kh in heads]
        project(*mine[0])
        es = [jnp.exp(sts[kh] - ms[kh]).astype(BF16) for kh in heads]
        rs = [_dot(jnp.concatenate([vtc[kh * HEAD_DIM:(kh + 1) * HEAD_DIM, :], ones], axis=0), es[kh])
              for kh in heads]
        for spec in mine[1:]:
            project(*spec)
        for kh in heads:
            den = rs[kh][HEAD_DIM:HEAD_DIM + 1, :] + jnp.exp(sink_ref[kh] - ms[kh])
            ot = (rs[kh][:HEAD_DIM, :] / den).astype(BF16)
            for j in range(KV_GROUP):
                ot_s[blk, j, kh * HEAD_DIM:(kh + 1) * HEAD_DIM, :] = ot[:, j * WINDOW:(j + 1) * WINDOW]
        for j in range(KV_GROUP):
            o_s[rows, j * KV_WIDTH:(j + 1) * KV_WIDTH] = _dot_nt(eye_ref[...], ot_s[blk, j]).astype(BF16)

    @pl.when(i == pl.num_programs(1) - 1)
    def _():
        ktail_ref[...] = k_s[TB:TB + WINDOW, :].astype(F32)
        vtail_ref[...] = _dot_nt(eye_ref[...], vt_s[:, TB:TB + WINDOW])

    k_s[0:WINDOW, :] = k_s[TB:TB + WINDOW, :]
    vt_s[:, 0:WINDOW] = vt_s[:, TB:TB + WINDOW]

    for c in range(TB // POST_ROWS):
        rs = slice(c * POST_ROWS, (c + 1) * POST_ROWS)
        y_ref[rs, :] = _post(yssm_ref[rs, :], zs_s[rs, :], o_s[rs, :], za_s[rs, :], gs_s[rs, :], ga_s[rs, :],
                             x_ref[rs, :], gate_ref[...], wglu_ref, bglu_ref, wbs_ref, wba_ref, wout_ref,
                             lng_ref, lnb_ref)


def _main_prompt(rel_bias_flat, x2d, yssm, modp, wmain, wvt, sink_rows, eye, wglu, bglu, wbs, wba, wout, lng, lnb,
                 n_batch, seq):
    steps = seq // TB
    row_spec = pl.BlockSpec((TB, D_MODEL), lambda b, i: (b * steps + i, 0))
    mod_spec = lambda j: pl.BlockSpec((None, 1, D_MODEL), lambda b, i, j=j: (b * 3 + j, 0, 0))
    tail_spec = pl.BlockSpec((None, WINDOW, KV_WIDTH), lambda b, i: (b, 0, 0))
    consts = (wmain, wvt, sink_rows, eye, wglu, bglu, wbs, wba, wout, lng, lnb)
    return pl.pallas_call(
        _main_body,
        grid=(n_batch, steps),
        in_specs=[pl.BlockSpec(memory_space=pltpu.SMEM), row_spec, row_spec, mod_spec(0), mod_spec(1), mod_spec(2)]
        + [_const_spec(c.shape) for c in consts],
        out_specs=[row_spec, tail_spec, tail_spec],
        out_shape=[jax.ShapeDtypeStruct((n_batch * seq, D_MODEL), F32),
                   jax.ShapeDtypeStruct((n_batch, WINDOW, KV_WIDTH), F32),
                   jax.ShapeDtypeStruct((n_batch, WINDOW, KV_WIDTH), F32)],
        scratch_shapes=[pltpu.VMEM((TB, D_MODEL), BF16)] * 5
        + [pltpu.VMEM((TB + WINDOW, KV_WIDTH), BF16), pltpu.VMEM((KV_WIDTH, TB + WINDOW), BF16),
           pltpu.VMEM((NQB, KV_GROUP, KV_WIDTH, WINDOW), BF16), pltpu.VMEM((TB, D_MODEL), BF16),
           pltpu.VMEM((2, N_KV_HEADS, KV_GROUP, 2 * WINDOW, WINDOW), F32)],
        compiler_params=pltpu.CompilerParams(dimension_semantics=("arbitrary", "arbitrary"),
                                             vmem_limit_bytes=VMEM_LIMIT),
        name="main_prompt",
    )(rel_bias_flat, x2d, yssm, modp, modp, modp, *consts)


GT = 8


def _sample_ssm_body(u_ref, h0r_ref, h0i_ref, ar_ref, ai_ref, bre_ref, bim_ref, cblk_ref, d_ref,
                     y_ref, hr_ref, hi_ref):
    ub = u_ref[...]
    ys = []
    for a in range(GT):
        ua = ub[:, a * 128:(a + 1) * 128]
        st = slice(a * 512, (a + 1) * 512)
        xr = _dot(ua, bre_ref[a])
        xi = _dot(ua, bim_ref[a])
        h0r = h0r_ref[:, st]
        h0i = h0i_ref[:, st]
        ar = ar_ref[:, st]
        ai = ai_ref[:, st]
        hr = ar * h0r - ai * h0i + xr
        hi = ar * h0i + ai * h0r + xi
        hr_ref[:, st] = hr
        hi_ref[:, st] = hi
        hcat = jnp.concatenate([hr, hi], axis=1).astype(BF16)
        ys.append(_dot_nt(hcat, cblk_ref[a]))
    y_ref[...] = jnp.concatenate(ys, axis=1) + d_ref[...] * ub.astype(F32)


def _sample_ssm(proj_s, h0r, h0i, ar, ai, bre, bim, cblk, d_row):
    rows = proj_s.shape[0]
    ns = SSM_GROUPS * SSM_STATE
    full = lambda shape: pl.BlockSpec(shape, lambda i: (0,) * len(shape))
    return pl.pallas_call(
        _sample_ssm_body,
        grid=(1,),
        in_specs=[pl.BlockSpec((rows, D_MODEL), lambda i: (0, COL_U)),
                  full((rows, ns)), full((rows, ns)), full((1, ns)), full((1, ns)),
                  full(bre.shape), full(bim.shape), full(cblk.shape), full((1, D_MODEL))],
        out_specs=[full((rows, D_MODEL)), full((rows, ns)), full((rows, ns))],
        out_shape=[jax.ShapeDtypeStruct((rows, D_MODEL), F32),
                   jax.ShapeDtypeStruct((rows, ns), F32),
                   jax.ShapeDtypeStruct((rows, ns), F32)],
        compiler_params=pltpu.CompilerParams(vmem_limit_bytes=VMEM_LIMIT),
        name="sample_ssm",
    )(proj_s, h0r, h0i, ar, ai, bre, bim, cblk, d_row)


SB = 16
KROWS = WINDOW + 8
SEQ_GROUP = 4


def _sample_attn_body(q_ref, kn_ref, vn_ref, ck_ref, cv_ref, bias_ref, sink_ref, seg_ref, segt_ref,
                      o_ref, nk_ref, nv_ref):
    qf = q_ref[...].astype(F32)
    knf = kn_ref[...].astype(F32)
    vnf = vn_ref[...].astype(F32)
    rowk = lax.broadcasted_iota(jnp.int32, (KROWS, 128), 0)
    roww = lax.broadcasted_iota(jnp.int32, (WINDOW, KV_WIDTH), 0)
    sink = sink_ref[...]
    last = roww == WINDOW - 1
    for g0 in range(0, SB, SEQ_GROUP):
        seqs = range(g0, g0 + SEQ_GROUP)
        scores = []
        for b in seqs:
            kall = jnp.concatenate([ck_ref[b], jnp.broadcast_to(knf[b:b + 1], (8, KV_WIDTH))], axis=0)
            qrow = qf[b:b + 1]
            prod = jnp.concatenate([kall * qrow[:, j * KV_WIDTH:(j + 1) * KV_WIDTH] for j in range(KV_GROUP)], axis=1)
            s = _dot(prod.astype(BF16), seg_ref[...]) + bias_ref[...]
            scores.append(jnp.where(rowk <= WINDOW, s, NEG_INF))
        probs = []
        for s in scores:
            m = jnp.maximum(jnp.max(s, axis=0, keepdims=True), sink)
            e = jnp.exp(s - m)
            den = jnp.sum(e, axis=0, keepdims=True) + jnp.exp(sink - m)
            probs.append((e / den).astype(BF16))
        pexps = [_dot(p, segt_ref[...]) for p in probs]
        for b, pexp in zip(seqs, pexps):
            vb = cv_ref[b]
            vall = jnp.concatenate([vb, jnp.broadcast_to(vnf[b:b + 1], (8, KV_WIDTH))], axis=0)
            v4 = jnp.concatenate([vall] * KV_GROUP, axis=1)
            o_ref[b:b + 1, :] = jnp.sum(pexp * v4, axis=0, keepdims=True)
            nv_ref[b] = jnp.where(last, jnp.broadcast_to(vnf[b:b + 1], (WINDOW, KV_WIDTH)), pltpu.roll(vb, WINDOW - 1, 0))
            nk_ref[b] = jnp.where(last, jnp.broadcast_to(knf[b:b + 1], (WINDOW, KV_WIDTH)),
                                  pltpu.roll(ck_ref[b], WINDOW - 1, 0))


def _sample_attn(proj_s, ck, cv, bias_s, sink_row, seg, segt):
    rows = proj_s.shape[0]
    cache_spec = pl.BlockSpec((SB, WINDOW, KV_WIDTH), lambda i: (i, 0, 0))
    return pl.pallas_call(
        _sample_attn_body,
        grid=(rows // SB,),
        in_specs=[pl.BlockSpec((SB, D_MODEL), lambda i: (i, COL_Q)),
                  pl.BlockSpec((SB, KV_WIDTH), lambda i: (i, COL_K)),
                  pl.BlockSpec((SB, KV_WIDTH), lambda i: (i, COL_V)),
                  cache_spec, cache_spec,
                  pl.BlockSpec(bias_s.shape, lambda i: (0, 0)),
                  pl.BlockSpec(sink_row.shape, lambda i: (0, 0)),
                  pl.BlockSpec(seg.shape, lambda i: (0, 0)),
                  pl.BlockSpec(segt.shape, lambda i: (0, 0))],
        out_specs=[pl.BlockSpec((SB, D_MODEL), lambda i: (i, 0)), cache_spec, cache_spec],
        out_shape=[jax.ShapeDtypeStruct((rows, D_MODEL), F32),
                   jax.ShapeDtypeStruct(ck.shape, F32),
                   jax.ShapeDtypeStruct(cv.shape, F32)],
        compiler_params=pltpu.CompilerParams(vmem_limit_bytes=VMEM_LIMIT),
        name="sample_attn",
    )(proj_s, proj_s, proj_s, ck, cv, bias_s, sink_row, seg, segt)


def _sample_final_body(x_ref, yssm_ref, zs_ref, o_ref, za_ref, gs_ref, ga_ref, gate_ref,
                       wglu_ref, bglu_ref, wbs_ref, wba_ref, wout_ref, lng_ref, lnb_ref, y_ref):
    y_ref[...] = _post(yssm_ref[...], zs_ref[...], o_ref[...], za_ref[...], gs_ref[...], ga_ref[...],
                       x_ref[...], gate_ref[...], wglu_ref, bglu_ref, wbs_ref, wba_ref, wout_ref,
                       lng_ref, lnb_ref)


def _final_sample(x2d, yssm, o_a, proj_s, mods, wglu, bglu, wbs, wba, wout, lng, lnb):
    rows = x2d.shape[0]
    full = lambda shape: pl.BlockSpec(shape, lambda i: (0,) * len(shape))
    pcol = lambda c: pl.BlockSpec((rows, D_MODEL), lambda i, c=c: (0, c))
    return pl.pallas_call(
        _sample_final_body,
        grid=(1,),
        in_specs=[full((rows, D_MODEL)), full((rows, D_MODEL)), pcol(COL_ZS), full((rows, D_MODEL)),
                  pcol(COL_ZA), pcol(COL_GS), pcol(COL_GA),
                  pl.BlockSpec((rows, D_MODEL), lambda i: (0, 2)),
                  full(wglu.shape), full(bglu.shape), full(wbs.shape), full(wba.shape), full(wout.shape),
                  full(lng.shape), full(lnb.shape)],
        out_specs=full((rows, D_MODEL)),
        out_shape=jax.ShapeDtypeStruct((rows, D_MODEL), F32),
        compiler_params=pltpu.CompilerParams(vmem_limit_bytes=VMEM_LIMIT),
        name="final_sample",
    )(x2d, yssm, proj_s, o_a, proj_s, proj_s, proj_s, mods, wglu, bglu, wbs, wba, wout, lng, lnb)


def _bucket_table():
    max_exact = N_BUCKETS // 2
    dist = np.arange(WINDOW + 1)
    df = np.maximum(dist, 1).astype(np.float32)
    large = max_exact + (np.log(df / np.float32(max_exact)) / np.float32(math.log(WINDOW / max_exact))
                         * np.float32(N_BUCKETS - max_exact)).astype(np.int32)
    large = np.minimum(large, N_BUCKETS - 1)
    return np.where(dist < max_exact, dist, large)


def _dot3_nt(a, b):
    ah = a.astype(BF16)
    al = (a - ah.astype(F32)).astype(BF16)
    bh = b.astype(BF16)
    bl = (b - bh.astype(F32)).astype(BF16)
    return _dot_nt(ah, bh) + _dot_nt(ah, bl) + _dot_nt(al, bh)


def _ssm_prep_body(p1_ref, p2_ref, kf_ref, bt_ref, bts_ref, c2_ref, c2s_ref, d_ref,
                   lhs_ref, bpow_ref, cpow_ref, bre_ref, bim_ref, cblk_ref):
    lane = lax.broadcasted_iota(jnp.int32, (SSM_GROUP, SW), 1)
    left = lane < SSM_STATE
    sgn = jnp.where(left, 1.0, -1.0)
    lane_blk = lax.broadcasted_iota(jnp.int32, (CW, CW), 1) // SSM_GROUP
    diag = lax.broadcasted_iota(jnp.int32, (CW, CW), 0) == lax.broadcasted_iota(jnp.int32, (CW, CW), 1)
    bre_ref[...] = jnp.zeros_like(bre_ref)
    bim_ref[...] = jnp.zeros_like(bim_ref)
    cblk_ref[...] = jnp.zeros_like(cblk_ref)
    for gl in range(GT):
        p1, p2 = p1_ref[gl], p2_ref[gl]
        k1, k2 = kf_ref[gl, 0:1, :], kf_ref[gl, 1:2, :]
        bt, bts = bt_ref[gl], bts_ref[gl]
        c2, c2s = c2_ref[gl], c2s_ref[gl]
        bb = k1 * bt + k2 * bts
        bbs = k1 * bts - k2 * bt
        ca = [sgn * (c2 * p1[t:t + 1, :] + c2s * p2[t:t + 1, :]) for t in range(CHUNK + 1)]
        cpow_ref[gl] = jnp.concatenate(ca[1:], axis=0).astype(BF16)
        bpow_ref[gl] = jnp.concatenate(
            [p1[CHUNK - 1 - s:CHUNK - s, :] * bb + p2[CHUNK - 1 - s:CHUNK - s, :] * bbs for s in range(CHUNK)],
            axis=0).astype(BF16)
        e = _dot3_nt(jnp.concatenate(ca[:CHUNK], axis=0), jnp.concatenate([bb] * CHUNK, axis=0))
        acc = jnp.where(diag, d_ref[gl], 0.0)
        for s in range(CHUNK):
            sh = e if s == 0 else jnp.concatenate([jnp.zeros((s * SSM_GROUP, CW), F32), e[:CW - s * SSM_GROUP, :]], axis=0)
            acc = acc + jnp.where(lane_blk == s, sh, 0.0)
        lhs_ref[gl] = acc.astype(BF16)
        rs = slice(gl * SSM_GROUP, (gl + 1) * SSM_GROUP)
        ts = slice((gl // 2) * SW, (gl // 2 + 1) * SW)
        ti = slice(GT * SSM_STATE + (gl // 2) * SW, GT * SSM_STATE + (gl // 2 + 1) * SW)
        if gl % 2 == 0:
            bre_ref[rs, ts] = jnp.where(left, bb, 0.0).astype(BF16)
            bim_ref[rs, ts] = jnp.where(left, bbs, 0.0).astype(BF16)
            cblk_ref[rs, ts] = jnp.where(left, c2, 0.0).astype(BF16)
            cblk_ref[rs, ti] = jnp.where(left, -c2s, 0.0).astype(BF16)
        else:
            bre_ref[rs, ts] = jnp.where(left, 0.0, bbs).astype(BF16)
            bim_ref[rs, ts] = jnp.where(left, 0.0, bb).astype(BF16)
            cblk_ref[rs, ts] = jnp.where(left, 0.0, c2s).astype(BF16)
            cblk_ref[rs, ti] = jnp.where(left, 0.0, -c2).astype(BF16)


def _ssm_params(lam_re, lam_im, log_delta, b_re, b_im, c_re, c_im, d_skip):
    g = SSM_GROUPS
    dup = lambda a: jnp.concatenate([a, a], axis=-1)
    lr, li = dup(lam_re.astype(F32)), dup(lam_im.astype(F32))
    dt = jnp.exp(log_delta.astype(F32))[:, None]
    half_sign = jnp.asarray(np.where(np.arange(SW) < SSM_STATE, -1.0, 1.0), F32)
    tau = jnp.asarray(POW_TAUS, F32)[None, :, None]
    mag = jnp.exp((lr * dt)[:, None, :] * tau)
    ang = (li * dt)[:, None, :] * tau
    p1 = mag * jnp.cos(ang)
    pim = mag * jnp.sin(ang)
    p2 = pim * half_sign
    ar, ai = p1[:, 1, :], pim[:, 1, :]
    den = lr * lr + li * li
    nr = ar - 1.0
    k1 = (nr * lr + ai * li) / den
    k2 = (ai * lr - nr * li) / den * half_sign
    kf = jnp.concatenate([k1[:, None, :], k2[:, None, :], jnp.zeros((g, 6, SW), F32)], axis=1)
    brt, bit = jnp.swapaxes(b_re, 1, 2), jnp.swapaxes(b_im, 1, 2)
    bt = jnp.concatenate([brt, bit], axis=-1)
    bts = jnp.concatenate([bit, brt], axis=-1)
    c2 = jnp.concatenate([c_re, c_im], axis=-1)
    c2s = jnp.concatenate([c_im, c_re], axis=-1)
    dtile = jnp.tile(d_skip.reshape(g, 1, SSM_GROUP), (1, 1, CHUNK))

    gspec = lambda r, w: pl.BlockSpec((GT, r, w), lambda a: (a, 0, 0))
    tile_spec = lambda w: pl.BlockSpec((None, GT * SSM_GROUP, w), lambda a: (a, 0, 0))
    nrow = len(POW_TAUS)
    lhs, bpow, cpow, bre, bim, cblk = pl.pallas_call(
        _ssm_prep_body,
        grid=(g // GT,),
        in_specs=[gspec(nrow, SW), gspec(nrow, SW), gspec(8, SW), gspec(SSM_GROUP, SW), gspec(SSM_GROUP, SW),
                  gspec(SSM_GROUP, SW), gspec(SSM_GROUP, SW), gspec(1, CW)],
        out_specs=[gspec(CW, CW), gspec(CW, SW), gspec(CW, SW),
                   tile_spec(GT * SSM_STATE), tile_spec(GT * SSM_STATE), tile_spec(2 * GT * SSM_STATE)],
        out_shape=[jax.ShapeDtypeStruct((g, CW, CW), BF16), jax.ShapeDtypeStruct((g, CW, SW), BF16),
                   jax.ShapeDtypeStruct((g, CW, SW), BF16),
                   jax.ShapeDtypeStruct((GT, GT * SSM_GROUP, GT * SSM_STATE), BF16),
                   jax.ShapeDtypeStruct((GT, GT * SSM_GROUP, GT * SSM_STATE), BF16),
                   jax.ShapeDtypeStruct((GT, GT * SSM_GROUP, 2 * GT * SSM_STATE), BF16)],
        name="ssm_prep",
    )(p1, p2, kf, bt, bts, c2, c2s, dtile)
    ar_row = ar[:, :SSM_STATE].reshape(1, -1)
    ai_row = ai[:, :SSM_STATE].reshape(1, -1)
    return lhs, bpow, cpow, p1, p2, ar_row, ai_row, bre, bim, cblk


def _attn_params(rel_bias, sinks):
    hp = lax.Precision.HIGHEST
    rb = rel_bias.astype(F32)
    onehot = jnp.asarray(_bucket_table()[:, None] == np.arange(N_BUCKETS)[None, :], F32)
    tbl = jnp.dot(onehot, rb, precision=hp)
    sink_rows = jnp.repeat(sinks.astype(F32).reshape(N_KV_HEADS, 1, KV_GROUP), WINDOW, axis=2)
    ds = np.clip(WINDOW - np.arange(KROWS), 0, WINDOW)
    bs = jnp.dot(jnp.asarray(ds[:, None] == np.arange(WINDOW + 1)[None, :], F32), tbl, precision=hp)
    bs = jnp.transpose(bs.reshape(KROWS, N_KV_HEADS, KV_GROUP), (0, 2, 1)).reshape(KROWS, N_HEADS)
    bias_s = jnp.pad(bs, ((0, 0), (0, 128 - N_HEADS)))
    sk = jnp.transpose(sinks.astype(F32).reshape(N_KV_HEADS, KV_GROUP), (1, 0)).reshape(1, N_HEADS)
    sink_row = jnp.pad(sk, ((0, 0), (0, 128 - N_HEADS)))
    seg_np = np.zeros((D_MODEL, 128), np.float32)
    for j in range(KV_GROUP):
        for kh in range(N_KV_HEADS):
            r0 = j * KV_WIDTH + kh * HEAD_DIM
            seg_np[r0:r0 + HEAD_DIM, j * N_KV_HEADS + kh] = 1.0
    seg = jnp.asarray(seg_np, BF16)
    segt = jnp.asarray(seg_np.T, BF16)
    return sink_rows, bias_s, sink_row, seg, segt


def kernel(x_prompt, x_sample, c_prompt, c_sample, state_ssm_re, state_ssm_im, cache_swa_k, cache_swa_v,
           w_ada, b_ada, w_in, ssm_lambda_re, ssm_lambda_im, ssm_log_delta, ssm_b_re, ssm_b_im,
           ssm_c_re, ssm_c_im, ssm_d, w_glu, b_glu, attn_sinks, rel_bias, w_branch_s, w_branch_a,
           w_out, ln_g, ln_b):
    assert w_ada.shape[0] == 1, "single-layer trunk"
    n_batch, seq, _ = x_prompt.shape
    n_dec = x_sample.shape[0]

    wi = w_in[0]
    w_in_b = jnp.concatenate([wi[:, 0:1024], wi[:, 1024:2048],
                              _regroup_heads(wi[:, 2048:3072], 1) * (HEAD_DIM ** -0.5),
                              _regroup_heads(wi[:, 3584:4608], 1), wi[:, 4608:5632], wi[:, 5632:6656],
                              wi[:, 3072:3328], wi[:, 3328:3584]], axis=1).astype(BF16)
    wvt = jnp.transpose(wi[:, 3328:3584]).astype(BF16)
    wut = jnp.transpose(wi[:, 0:D_MODEL]).astype(BF16)
    eye = jnp.eye(WINDOW, dtype=BF16)
    wglu = w_glu[0].astype(BF16)
    bglu = b_glu[0].reshape(1, -1).astype(F32)
    wbs = w_branch_s[0].astype(BF16)
    wba = _regroup_heads(w_branch_a[0], 0).astype(BF16)
    wout = w_out[0].astype(BF16)
    lng = ln_g[0].reshape(1, -1).astype(F32)
    lnb = ln_b[0].reshape(1, -1).astype(F32)
    lhs, bpow, cpow, p1, p2, ar_row, ai_row, bre, bim, cblk = _ssm_params(
        ssm_lambda_re[0], ssm_lambda_im[0], ssm_log_delta[0], ssm_b_re[0], ssm_b_im[0],
        ssm_c_re[0], ssm_c_im[0], ssm_d[0])
    sink_rows, bias_s, sink_row, seg, segt = _attn_params(rel_bias, attn_sinks[0])

    c_all = jnp.concatenate([c_prompt, jnp.zeros((8 - n_batch, D_MODEL), F32), c_sample], axis=0)
    mod = _ada(c_all, w_ada[0], b_ada[0])
    modp = mod[:n_batch].reshape(n_batch * 3, 1, D_MODEL)
    mods = mod[8:8 + n_dec]

    xp = x_prompt.reshape(n_batch * seq, D_MODEL)
    yssm, hfin = _ssm_prompt(x_prompt, modp, wut, lhs, bpow, cpow, p1, p2)
    yp, ktail, vtail = _main_prompt(rel_bias.astype(F32).reshape(-1), xp, yssm, modp, w_in_b, wvt, sink_rows, eye,
                                    wglu, bglu, wbs, wba, wout, lng, lnb, n_batch, seq)
    yp = yp.reshape(n_batch, seq, D_MODEL)
    p_hr = hfin[:, :, 0, :SSM_STATE][None]
    p_hi = hfin[:, :, 0, SSM_STATE:][None]
    p_k = ktail.reshape(1, n_batch, WINDOW, N_KV_HEADS, HEAD_DIM)
    p_v = vtail.reshape(1, n_batch, WINDOW, N_KV_HEADS, HEAD_DIM)

    xs = x_sample.reshape(n_dec, D_MODEL)
    proj_s = _proj(xs, mods, mods,
                   pl.BlockSpec((n_dec, D_MODEL), lambda i: (0, 0)),
                   pl.BlockSpec((n_dec, D_MODEL), lambda i: (0, 1)),
                   w_in_b, n_dec, D_IN)
    ns = SSM_GROUPS * SSM_STATE
    ys_s, s_hr, s_hi = _sample_ssm(proj_s, state_ssm_re[0].reshape(n_dec, ns), state_ssm_im[0].reshape(n_dec, ns),
                                   ar_row, ai_row, bre, bim, cblk, ssm_d[0].reshape(1, -1).astype(F32))
    o_s, nk, nv = _sample_attn(proj_s, cache_swa_k[0].reshape(n_dec, WINDOW, KV_WIDTH),
                               cache_swa_v[0].reshape(n_dec, WINDOW, KV_WIDTH), bias_s, sink_row, seg, segt)
    ysmp = _final_sample(xs, ys_s, o_s, proj_s, mods, wglu, bglu, wbs, wba, wout, lng, lnb)

    return (yp, ysmp.reshape(n_dec, 1, D_MODEL),
            p_hr, p_hi, p_k, p_v,
            s_hr.reshape(1, n_dec, SSM_GROUPS, SSM_STATE), s_hi.reshape(1, n_dec, SSM_GROUPS, SSM_STATE),
            nk.reshape(1, n_dec, WINDOW, N_KV_HEADS, HEAD_DIM), nv.reshape(1, n_dec, WINDOW, N_KV_HEADS, HEAD_DIM))
```

```python
import functools
import math

import numpy as np
import jax
import jax.numpy as jnp
from jax import lax
from jax.experimental import pallas as pl
from jax.experimental.pallas import tpu as pltpu

F32 = jnp.float32
BF16 = jnp.bfloat16

D_MODEL = 1024
SSM_GROUPS = 64
SSM_GROUP = 16
SSM_STATE = 64
N_HEADS = 16
HEAD_DIM = 64
N_KV_HEADS = 4
KV_GROUP = 4
KV_WIDTH = N_KV_HEADS * HEAD_DIM
WINDOW = 128
N_BUCKETS = 32
NEG_INF = -1e30
LN_EPS = 1e-5
DEPTH = 1
DEEPNORM_ALPHA = (2 * DEPTH) ** 0.25
D_IN = 6656
CHUNK = 16
CW = CHUNK * SSM_GROUP
SW = 2 * SSM_STATE

COL_U, COL_ZS, COL_Q, COL_ZA, COL_GS, COL_GA = 0, 1, 2, 3, 4, 5
COL_K, COL_V = 24, 25

VMEM_LIMIT = 56 * 1024 * 1024
VMEM_LIMIT_SSM = 60 * 1024 * 1024


def _regroup_heads(w, axis):
    shp = w.shape
    if axis == 1:
        return jnp.transpose(w.reshape(shp[0], N_KV_HEADS, KV_GROUP, HEAD_DIM), (0, 2, 1, 3)).reshape(shp)
    return jnp.transpose(w.reshape(N_KV_HEADS, KV_GROUP, HEAD_DIM, shp[1]), (1, 0, 2, 3)).reshape(shp)


def _sigmoid(x):
    return 0.5 * jnp.tanh(0.5 * x) + 0.5


def _silu(x):
    return x * _sigmoid(x)


def _gelu_tanh(x):
    c = math.sqrt(2.0 / math.pi)
    hx = 0.5 * x
    return hx + hx * jnp.tanh(x * (c + (c * 0.044715) * (x * x)))


def _dot(a, b):
    return jnp.dot(a, b, preferred_element_type=F32)


def _dot_nt(a, b):
    return lax.dot_general(a, b, (((1,), (1,)), ((), ())), preferred_element_type=F32)


def _ada_body(c_ref, w_ref, b_ref, o_ref):
    c = c_ref[...]
    sc = _silu(c).astype(BF16)
    o_ref[...] = _dot(sc, w_ref[...].astype(BF16)) + b_ref[...]


def _ada(c_all, w_ada, b_ada):
    rows = c_all.shape[0]
    bn = 512
    return pl.pallas_call(
        _ada_body,
        grid=(3 * D_MODEL // bn,),
        in_specs=[pl.BlockSpec((rows, D_MODEL), lambda j: (0, 0)),
                  pl.BlockSpec((D_MODEL, bn), lambda j: (0, j)),
                  pl.BlockSpec((1, bn), lambda j: (0, j))],
        out_specs=pl.BlockSpec((rows, bn), lambda j: (0, j)),
        out_shape=jax.ShapeDtypeStruct((rows, 3 * D_MODEL), F32),
        name="ada",
    )(c_all, w_ada, b_ada.reshape(1, -1))


PROJ_CB = 512


def _proj_body(x_ref, shift_ref, scale_ref, w_ref, o_ref, *, n_out):
    h = x_ref[...] * (1.0 + scale_ref[...]) + shift_ref[...]
    hb = h.astype(BF16)
    for j in range(n_out // PROJ_CB):
        sl = slice(j * PROJ_CB, (j + 1) * PROJ_CB)
        o_ref[:, sl] = _dot(hb, w_ref[:, sl]).astype(BF16)


def _proj(x2d, shift_arr, scale_arr, shift_spec, scale_spec, w_in_b, tm, n_out):
    rows = x2d.shape[0]
    return pl.pallas_call(
        functools.partial(_proj_body, n_out=n_out),
        grid=(rows // tm,),
        in_specs=[pl.BlockSpec((tm, D_MODEL), lambda i: (i, 0)),
                  shift_spec, scale_spec,
                  pl.BlockSpec((D_MODEL, n_out), lambda i: (0, 0), pipeline_mode=pl.Buffered(1))],
        out_specs=pl.BlockSpec((tm, n_out), lambda i: (i, 0)),
        out_shape=jax.ShapeDtypeStruct((rows, n_out), BF16),
        compiler_params=pltpu.CompilerParams(vmem_limit_bytes=VMEM_LIMIT),
        name="proj",
    )(x2d, shift_arr, scale_arr, w_in_b)


NKS = 128
GC = 16


def _cmul(c1, c2, x):
    return c1 * x + c2 * pltpu.roll(x, SSM_STATE, x.ndim - 1)


SSM_TOK = NKS * CHUNK
SUB = 8
NPH = CHUNK // SUB
GROUP_UNROLL = 16


def _dot_tn(a, b):
    return lax.dot_general(a, b, (((0,), (0,)), ((), ())), preferred_element_type=F32)


SEG = 8
SEG_LEN = NKS // SEG
POW_TAUS = (tuple(range(CHUNK + 1)) + tuple(CHUNK * m for m in range(2, SEG_LEN))
            + tuple(CHUNK * SEG_LEN * m for m in range(1, SEG + 1)) + (0,))
ROW_CHUNK1 = CHUNK
ROW_SEG1 = CHUNK + SEG_LEN - 1


def _ssm_body(x_ref, shift_ref, scale_ref, wut_ref, lhs_ref, bpow_ref, cpow_ref, p1_ref, p2_ref, unperm_ref,
              y_ref, hfin_ref, r_scr, yt_scr, s_scr, carry_scr):
    i = pl.program_id(1)
    j = pl.program_id(2)

    @pl.when((i == 0) & (j == 0))
    def _():
        carry_scr[...] = jnp.zeros_like(carry_scr)

    @pl.when(j < NPH)
    def _():
        for sp in range(SUB // 2):
            h = jnp.concatenate([x_ref[:, b, s, :] for s in (2 * sp, 2 * sp + 1) for b in range(SEG_LEN)],
                                axis=0)
            hb = (h * (1.0 + scale_ref[...]) + shift_ref[...]).astype(BF16)
            ut = _dot_nt(wut_ref[...], hb).astype(BF16)
            ut = ut.reshape(SSM_GROUPS, SSM_GROUP, 2 * NKS)
            pair = jnp.concatenate([ut[:, :, :NKS], ut[:, :, NKS:]], axis=1)
            r0 = pl.multiple_of(j * SUB * SSM_GROUP + sp * 2 * SSM_GROUP, 2 * SSM_GROUP)
            r_scr[:, pl.ds(r0, 2 * SSM_GROUP), :] = pair

    @pl.when(j == NPH - 1)
    def _():
        def local(g):
            rg = r_scr[g]
            yt_scr[g] = _dot(lhs_ref[g], rg)
            s_scr[g] = _dot_tn(rg, bpow_ref[g])

        def carried(g):
            yt_scr[g] += _dot_nt(cpow_ref[g], s_scr[g].astype(BF16))

        row = lax.broadcasted_iota(jnp.int32, (GC, SEG, SW), 1)
        seg_rows = lambda b: slice(b * SEG, (b + 1) * SEG)

        def scan(gc):
            gs = slice(gc * GC, (gc + 1) * GC)
            p1, p2 = p1_ref[gs], p2_ref[gs]
            c1, c2 = p1[:, ROW_CHUNK1:ROW_CHUNK1 + 1, :], p2[:, ROW_CHUNK1:ROW_CHUNK1 + 1, :]
            h = jnp.zeros((GC, SEG, SW), F32)
            hs = h
            for b in range(SEG_LEN):
                sb = s_scr[gs, seg_rows(b), :]
                h, hs = sb + c1 * h + c2 * hs, pltpu.roll(sb, SSM_STATE, 2) + c1 * hs - c2 * h
                s_scr[gs, seg_rows(b), :] = h
                yield
            e = h
            for d in (1, 2, 4):
                rd = ROW_SEG1 + d - 1
                sh = jnp.where(row >= d, pltpu.roll(e, d, 1), 0.0)
                e = e + _cmul(p1[:, rd:rd + 1, :], p2[:, rd:rd + 1, :], sh)
            carry = carry_scr[gs]
            e = e + _cmul(p1[:, ROW_SEG1:ROW_SEG1 + SEG, :], p2[:, ROW_SEG1:ROW_SEG1 + SEG, :], carry)
            ein = jnp.where(row == 0, carry, pltpu.roll(e, 1, 1))
            eins = pltpu.roll(ein, SSM_STATE, 2)
            carry_scr[gs] = jnp.broadcast_to(e[:, SEG - 1:SEG, :], (GC, SEG, SW))
            yield
            for b in range(SEG_LEN - 1, 0, -1):
                rb = ROW_CHUNK1 + b - 1
                s_scr[gs, seg_rows(b), :] = (s_scr[gs, seg_rows(b - 1), :]
                                             + p1[:, rb:rb + 1, :] * ein + p2[:, rb:rb + 1, :] * eins)
                yield
            s_scr[gs, seg_rows(0), :] = ein

        def interleave(fn, groups, gen):
            for g in groups:
                fn(g)
                for _ in range(2):
                    next(gen, None)
            for _ in gen:
                pass

        n_sets = SSM_GROUPS // GC
        sets = [range(q * GC, (q + 1) * GC) for q in range(n_sets)]
        for g in sets[0]:
            local(g)
        for q in range(1, n_sets):
            interleave(local, sets[q], scan(q - 1))
        interleave(carried, sets[0], scan(n_sets - 1))
        for q in range(1, n_sets):
            for g in sets[q]:
                carried(g)
        hfin_ref[...] = carry_scr[...]

    @pl.when(j >= NPH)
    def _():
        for sp in range(SUB):
            t0 = pl.multiple_of((j - NPH) * SUB * SSM_GROUP + sp * SSM_GROUP, SSM_GROUP)
            yt = yt_scr[:, pl.ds(t0, SSM_GROUP), :].reshape(D_MODEL, NKS)
            hi = yt.astype(BF16)
            lo = (yt - hi.astype(F32)).astype(BF16)
            y_ref[:, sp, :] = _dot_nt(unperm_ref[...], hi) + _dot_nt(unperm_ref[...], lo)


def _ssm_prompt(x_prompt, modp, wut, lhs, bpow, cpow, p1, p2):
    n_batch, seq, _ = x_prompt.shape
    steps = seq // SSM_TOK
    xv = x_prompt.reshape(n_batch * steps * SEG, SEG_LEN, CHUNK, D_MODEL)
    mod_spec = lambda k: pl.BlockSpec((None, 1, D_MODEL), lambda b, i, j, k=k: (b * 3 + k, 0, 0))
    k = np.arange(NKS)
    perm_np = np.zeros((NKS, NKS), np.float32)
    perm_np[k, (k % SEG_LEN) * SEG + k // SEG_LEN] = 1.0
    consts = (wut, lhs, bpow, cpow, p1, p2, jnp.asarray(perm_np, BF16))
    y, hfin = pl.pallas_call(
        _ssm_body,
        grid=(n_batch, steps, 2 * NPH),
        in_specs=[pl.BlockSpec((SEG, SEG_LEN, SUB, D_MODEL),
                               lambda b, i, j: (b * steps + i, 0, jnp.minimum(j, NPH - 1), 0)),
                  mod_spec(0), mod_spec(1)] + [_const_spec(c.shape) for c in consts],
        out_specs=[pl.BlockSpec((NKS, SUB, D_MODEL), lambda b, i, j: (b * steps + i, jnp.maximum(j - NPH, 0), 0)),
                   pl.BlockSpec((None, SSM_GROUPS, 8, SW), lambda b, i, j: (b, 0, 0, 0))],
        out_shape=[jax.ShapeDtypeStruct((n_batch * seq // CHUNK, CHUNK, D_MODEL), F32),
                   jax.ShapeDtypeStruct((n_batch, SSM_GROUPS, 8, SW), F32)],
        scratch_shapes=[pltpu.VMEM((SSM_GROUPS, CW, NKS), BF16), pltpu.VMEM((SSM_GROUPS, CW, NKS), F32),
                        pltpu.VMEM((SSM_GROUPS, NKS, SW), F32), pltpu.VMEM((SSM_GROUPS, 8, SW), F32)],
        compiler_params=pltpu.CompilerParams(dimension_semantics=("arbitrary", "arbitrary", "arbitrary"),
                                             vmem_limit_bytes=VMEM_LIMIT_SSM),
        name="ssm_prompt",
    )(xv, modp, modp, *consts)
    return y.reshape(n_batch * seq, D_MODEL), hfin


def _post(ys, zs, o_a, za, gs, ga, x, gate, wglu_ref, bglu_ref, wbs_ref, wba_ref, wout_ref, lng_ref, lnb_ref):
    ys = _gelu_tanh(ys)
    glu = ys * _sigmoid(_dot(ys.astype(BF16), wglu_ref[...]) + bglu_ref[...])
    t = glu.astype(BF16) * _silu(zs)
    b_s = _dot(t, wbs_ref[...])
    oz = o_a.astype(BF16) * _silu(za)
    b_a = _dot(oz, wba_ref[...])
    m = _sigmoid(gs).astype(F32) * b_s + _sigmoid(ga).astype(F32) * b_a
    out = _dot(m.astype(BF16), wout_ref[...])
    r = DEEPNORM_ALPHA * x + gate * out
    mu = jnp.mean(r, axis=-1, keepdims=True)
    rc = r - mu
    var = jnp.mean(rc * rc, axis=-1, keepdims=True)
    return rc * lax.rsqrt(var + LN_EPS) * lng_ref[...] + lnb_ref[...]


TB = 512
NQB = TB // WINDOW
W_OFF = D_MODEL
W_MAIN = 5 * D_MODEL + KV_WIDTH
ONES_ROWS = 16
POST_ROWS = 512


def _const_spec(shape):
    nd = len(shape)
    return pl.BlockSpec(shape, lambda *_: (0,) * nd, pipeline_mode=pl.Buffered(1))


def _bucket_thresholds():
    tbl = _bucket_table()
    return [int(np.argmax(tbl >= k)) for k in range(1, N_BUCKETS)]


def _fill_bias_table(rb_ref, bias_s):
    si = lax.broadcasted_iota(jnp.int32, (2 * WINDOW, WINDOW), 0)
    qi = lax.broadcasted_iota(jnp.int32, (2 * WINDOW, WINDOW), 1)
    dist = qi + WINDOW - si
    bid = jnp.zeros((2 * WINDOW, WINDOW), jnp.int32)
    for thr in _bucket_thresholds():
        bid = bid + (dist >= thr).astype(jnp.int32)
    valid = (dist >= 0) & (dist <= WINDOW)
    own = si >= WINDOW

    def per_head(h, c):
        t = jnp.zeros((2 * WINDOW, WINDOW), F32)
        for b in range(N_BUCKETS):
            t = jnp.where(bid == b, rb_ref[b * N_HEADS + h], t)
        t = jnp.where(valid, t, NEG_INF)
        kh, j = h // KV_GROUP, h % KV_GROUP
        bias_s[0, kh, j] = t
        bias_s[1, kh, j] = jnp.where(own, t, NEG_INF)
        return c

    lax.fori_loop(0, N_HEADS, per_head, 0)


def _main_body(rb_ref, x_ref, yssm_ref, shift_ref, scale_ref, gate_ref, wmain_ref, wvt_ref, sink_ref, eye_ref,
               wglu_ref, bglu_ref, wbs_ref, wba_ref, wout_ref, lng_ref, lnb_ref,
               y_ref, ktail_ref, vtail_ref,
               zs_s, q_s, za_s, gs_s, ga_s, k_s, vt_s, ot_s, o_s, bias_s):
    i = pl.program_id(1)

    @pl.when((pl.program_id(0) == 0) & (i == 0))
    def _():
        _fill_bias_table(rb_ref, bias_s)

    x = x_ref[...]
    hb = (x * (1.0 + scale_ref[...]) + shift_ref[...]).astype(BF16)

    def project(dst, c0, c):
        lo = W_OFF + c0 + c * 512
        dst[:, c * 512:(c + 1) * 512] = _dot(hb, wmain_ref[:, lo:lo + 512]).astype(BF16)

    for c in range(2):
        project(q_s, D_MODEL, c)
    deferred = [(dst, c0, c) for dst, c0 in ((zs_s, 0), (za_s, 2 * D_MODEL), (gs_s, 3 * D_MODEL), (ga_s, 4 * D_MODEL))
                for c in range(2)]

    @pl.when(i == 0)
    def _():
        k_s[0:WINDOW, :] = jnp.zeros((WINDOW, KV_WIDTH), BF16)
        vt_s[:, 0:WINDOW] = jnp.zeros((KV_WIDTH, WINDOW), BF16)

    k_s[WINDOW:WINDOW + TB, :] = _dot(hb, wmain_ref[:, W_OFF + 5 * D_MODEL:W_OFF + W_MAIN]).astype(BF16)
    vt_s[:, WINDOW:WINDOW + TB] = _dot_nt(wvt_ref[...], hb).astype(BF16)

    first = jnp.where(i == 0, 1, 0)
    lane = lax.broadcasted_iota(jnp.int32, (2 * WINDOW, 128), 1)
    ones = jnp.ones((ONES_ROWS, 2 * WINDOW), BF16)
    for blk in range(NQB):
        rows = slice(blk * WINDOW, (blk + 1) * WINDOW)
        kcat = k_s[blk * WINDOW:blk * WINDOW + 2 * WINDOW, :]
        vtc = vt_s[:, blk * WINDOW:blk * WINDOW + 2 * WINDOW]
        sel = first if blk == 0 else 0
        heads = range(N_KV_HEADS)
        sts = []
        for kh in heads:
            pair, odd = kh // 2, kh % 2
            kcm = jnp.where((lane >= HEAD_DIM) == bool(odd), kcat[:, pair * 128:(pair + 1) * 128], jnp.zeros((), BF16))
            qp = jnp.concatenate(
                [q_s[rows, j * KV_WIDTH + pair * 128:j * KV_WIDTH + (pair + 1) * 128] for j in range(KV_GROUP)],
                axis=0)
            bias = jnp.concatenate([bias_s[sel, kh, j] for j in range(KV_GROUP)], axis=1)
            sts.append(_dot_nt(kcm, qp) + bias)
        mine = deferred[blk * len(deferred) // NQB:(blk + 1) * len(deferred) // NQB]
        ms = [jnp.maximum(jnp.max(sts[kh], axis=0, keepdims=True), sink_ref[kh]) for kh in heads]
        project(*mine[0])
        es = [jnp.exp(sts[kh] - ms[kh]).astype(BF16) for kh in heads]
        rs = [_dot(jnp.concatenate([vtc[kh * HEAD_DIM:(kh + 1) * HEAD_DIM, :], ones], axis=0), es[kh])
              for kh in heads]
        for spec in mine[1:]:
            project(*spec)
        for kh in heads:
            den = rs[kh][HEAD_DIM:HEAD_DIM + 1, :] + jnp.exp(sink_ref[kh] - ms[kh])
            ot = (rs[kh][:HEAD_DIM, :] / den).astype(BF16)
            for j in range(KV_GROUP):
                ot_s[blk, j, kh * HEAD_DIM:(kh + 1) * HEAD_DIM, :] = ot[:, j * WINDOW:(j + 1) * WINDOW]
        for j in range(KV_GROUP):
            o_s[rows, j * KV_WIDTH:(j + 1) * KV_WIDTH] = _dot_nt(eye_ref[...], ot_s[blk, j]).astype(BF16)

    @pl.when(i == pl.num_programs(1) - 1)
    def _():
        ktail_ref[...] = k_s[TB:TB + WINDOW, :].astype(F32)
        vtail_ref[...] = _dot_nt(eye_ref[...], vt_s[:, TB:TB + WINDOW])

    k_s[0:WINDOW, :] = k_s[TB:TB + WINDOW, :]
    vt_s[:, 0:WINDOW] = vt_s[:, TB:TB + WINDOW]

    for c in range(TB // POST_ROWS):
        rs = slice(c * POST_ROWS, (c + 1) * POST_ROWS)
        y_ref[rs, :] = _post(yssm_ref[rs, :], zs_s[rs, :], o_s[rs, :], za_s[rs, :], gs_s[rs, :], ga_s[rs, :],
                             x_ref[rs, :], gate_ref[...], wglu_ref, bglu_ref, wbs_ref, wba_ref, wout_ref,
                             lng_ref, lnb_ref)


def _main_prompt(rel_bias_flat, x2d, yssm, modp, wmain, wvt, sink_rows, eye, wglu, bglu, wbs, wba, wout, lng, lnb,
                 n_batch, seq):
    steps = seq // TB
    row_spec = pl.BlockSpec((TB, D_MODEL), lambda b, i: (b * steps + i, 0))
    mod_spec = lambda j: pl.BlockSpec((None, 1, D_MODEL), lambda b, i, j=j: (b * 3 + j, 0, 0))
    tail_spec = pl.BlockSpec((None, WINDOW, KV_WIDTH), lambda b, i: (b, 0, 0))
    consts = (wmain, wvt, sink_rows, eye, wglu, bglu, wbs, wba, wout, lng, lnb)
    return pl.pallas_call(
        _main_body,
        grid=(n_batch, steps),
        in_specs=[pl.BlockSpec(memory_space=pltpu.SMEM), row_spec, row_spec, mod_spec(0), mod_spec(1), mod_spec(2)]
        + [_const_spec(c.shape) for c in consts],
        out_specs=[row_spec, tail_spec, tail_spec],
        out_shape=[jax.ShapeDtypeStruct((n_batch * seq, D_MODEL), F32),
                   jax.ShapeDtypeStruct((n_batch, WINDOW, KV_WIDTH), F32),
                   jax.ShapeDtypeStruct((n_batch, WINDOW, KV_WIDTH), F32)],
        scratch_shapes=[pltpu.VMEM((TB, D_MODEL), BF16)] * 5
        + [pltpu.VMEM((TB + WINDOW, KV_WIDTH), BF16), pltpu.VMEM((KV_WIDTH, TB + WINDOW), BF16),
           pltpu.VMEM((NQB, KV_GROUP, KV_WIDTH, WINDOW), BF16), pltpu.VMEM((TB, D_MODEL), BF16),
           pltpu.VMEM((2, N_KV_HEADS, KV_GROUP, 2 * WINDOW, WINDOW), F32)],
        compiler_params=pltpu.CompilerParams(dimension_semantics=("arbitrary", "arbitrary"),
                                             vmem_limit_bytes=VMEM_LIMIT),
        name="main_prompt",
    )(rel_bias_flat, x2d, yssm, modp, modp, modp, *consts)


GT = 8


def _sample_ssm_body(u_ref, h0r_ref, h0i_ref, ar_ref, ai_ref, bre_ref, bim_ref, cblk_ref, d_ref,
                     y_ref, hr_ref, hi_ref):
    ub = u_ref[...]
    ys = []
    for a in range(GT):
        ua = ub[:, a * 128:(a + 1) * 128]
        st = slice(a * 512, (a + 1) * 512)
        xr = _dot(ua, bre_ref[a])
        xi = _dot(ua, bim_ref[a])
        h0r = h0r_ref[:, st]
        h0i = h0i_ref[:, st]
        ar = ar_ref[:, st]
        ai = ai_ref[:, st]
        hr = ar * h0r - ai * h0i + xr
        hi = ar * h0i + ai * h0r + xi
        hr_ref[:, st] = hr
        hi_ref[:, st] = hi
        hcat = jnp.concatenate([hr, hi], axis=1).astype(BF16)
        ys.append(_dot_nt(hcat, cblk_ref[a]))
    y_ref[...] = jnp.concatenate(ys, axis=1) + d_ref[...] * ub.astype(F32)


def _sample_ssm(proj_s, h0r, h0i, ar, ai, bre, bim, cblk, d_row):
    rows = proj_s.shape[0]
    ns = SSM_GROUPS * SSM_STATE
    full = lambda shape: pl.BlockSpec(shape, lambda i: (0,) * len(shape))
    return pl.pallas_call(
        _sample_ssm_body,
        grid=(1,),
        in_specs=[pl.BlockSpec((rows, D_MODEL), lambda i: (0, COL_U)),
                  full((rows, ns)), full((rows, ns)), full((1, ns)), full((1, ns)),
                  full(bre.shape), full(bim.shape), full(cblk.shape), full((1, D_MODEL))],
        out_specs=[full((rows, D_MODEL)), full((rows, ns)), full((rows, ns))],
        out_shape=[jax.ShapeDtypeStruct((rows, D_MODEL), F32),
                   jax.ShapeDtypeStruct((rows, ns), F32),
                   jax.ShapeDtypeStruct((rows, ns), F32)],
        compiler_params=pltpu.CompilerParams(vmem_limit_bytes=VMEM_LIMIT),
        name="sample_ssm",
    )(proj_s, h0r, h0i, ar, ai, bre, bim, cblk, d_row)


SB = 16
KROWS = WINDOW + 8
SEQ_GROUP = 4


def _sample_attn_body(q_ref, kn_ref, vn_ref, ck_ref, cv_ref, bias_ref, sink_ref, seg_ref, segt_ref,
                      o_ref, nk_ref, nv_ref):
    qf = q_ref[...].astype(F32)
    knf = kn_ref[...].astype(F32)
    vnf = vn_ref[...].astype(F32)
    rowk = lax.broadcasted_iota(jnp.int32, (KROWS, 128), 0)
    roww = lax.broadcasted_iota(jnp.int32, (WINDOW, KV_WIDTH), 0)
    sink = sink_ref[...]
    last = roww == WINDOW - 1
    for g0 in range(0, SB, SEQ_GROUP):
        seqs = range(g0, g0 + SEQ_GROUP)
        scores = []
        for b in seqs:
            kall = jnp.concatenate([ck_ref[b], jnp.broadcast_to(knf[b:b + 1], (8, KV_WIDTH))], axis=0)
            qrow = qf[b:b + 1]
            prod = jnp.concatenate([kall * qrow[:, j * KV_WIDTH:(j + 1) * KV_WIDTH] for j in range(KV_GROUP)], axis=1)
            s = _dot(prod.astype(BF16), seg_ref[...]) + bias_ref[...]
            scores.append(jnp.where(rowk <= WINDOW, s, NEG_INF))
        probs = []
        for s in scores:
            m = jnp.maximum(jnp.max(s, axis=0, keepdims=True), sink)
            e = jnp.exp(s - m)
            den = jnp.sum(e, axis=0, keepdims=True) + jnp.exp(sink - m)
            probs.append((e / den).astype(BF16))
        pexps = [_dot(p, segt_ref[...]) for p in probs]
        for b, pexp in zip(seqs, pexps):
            vb = cv_ref[b]
            vall = jnp.concatenate([vb, jnp.broadcast_to(vnf[b:b + 1], (8, KV_WIDTH))], axis=0)
            v4 = jnp.concatenate([vall] * KV_GROUP, axis=1)
            o_ref[b:b + 1, :] = jnp.sum(pexp * v4, axis=0, keepdims=True)
            nv_ref[b] = jnp.where(last, jnp.broadcast_to(vnf[b:b + 1], (WINDOW, KV_WIDTH)), pltpu.roll(vb, WINDOW - 1, 0))
            nk_ref[b] = jnp.where(last, jnp.broadcast_to(knf[b:b + 1], (WINDOW, KV_WIDTH)),
                                  pltpu.roll(ck_ref[b], WINDOW - 1, 0))


def _sample_attn(proj_s, ck, cv, bias_s, sink_row, seg, segt):
    rows = proj_s.shape[0]
    cache_spec = pl.BlockSpec((SB, WINDOW, KV_WIDTH), lambda i: (i, 0, 0))
    return pl.pallas_call(
        _sample_attn_body,
        grid=(rows // SB,),
        in_specs=[pl.BlockSpec((SB, D_MODEL), lambda i: (i, COL_Q)),
                  pl.BlockSpec((SB, KV_WIDTH), lambda i: (i, COL_K)),
                  pl.BlockSpec((SB, KV_WIDTH), lambda i: (i, COL_V)),
                  cache_spec, cache_spec,
                  pl.BlockSpec(bias_s.shape, lambda i: (0, 0)),
                  pl.BlockSpec(sink_row.shape, lambda i: (0, 0)),
                  pl.BlockSpec(seg.shape, lambda i: (0, 0)),
                  pl.BlockSpec(segt.shape, lambda i: (0, 0))],
        out_specs=[pl.BlockSpec((SB, D_MODEL), lambda i: (i, 0)), cache_spec, cache_spec],
        out_shape=[jax.ShapeDtypeStruct((rows, D_MODEL), F32),
                   jax.ShapeDtypeStruct(ck.shape, F32),
                   jax.ShapeDtypeStruct(cv.shape, F32)],
        compiler_params=pltpu.CompilerParams(vmem_limit_bytes=VMEM_LIMIT),
        name="sample_attn",
    )(proj_s, proj_s, proj_s, ck, cv, bias_s, sink_row, seg, segt)


def _sample_final_body(x_ref, yssm_ref, zs_ref, o_ref, za_ref, gs_ref, ga_ref, gate_ref,
                       wglu_ref, bglu_ref, wbs_ref, wba_ref, wout_ref, lng_ref, lnb_ref, y_ref):
    y_ref[...] = _post(yssm_ref[...], zs_ref[...], o_ref[...], za_ref[...], gs_ref[...], ga_ref[...],
                       x_ref[...], gate_ref[...], wglu_ref, bglu_ref, wbs_ref, wba_ref, wout_ref,
                       lng_ref, lnb_ref)


def _final_sample(x2d, yssm, o_a, proj_s, mods, wglu, bglu, wbs, wba, wout, lng, lnb):
    rows = x2d.shape[0]
    full = lambda shape: pl.BlockSpec(shape, lambda i: (0,) * len(shape))
    pcol = lambda c: pl.BlockSpec((rows, D_MODEL), lambda i, c=c: (0, c))
    return pl.pallas_call(
        _sample_final_body,
        grid=(1,),
        in_specs=[full((rows, D_MODEL)), full((rows, D_MODEL)), pcol(COL_ZS), full((rows, D_MODEL)),
                  pcol(COL_ZA), pcol(COL_GS), pcol(COL_GA),
                  pl.BlockSpec((rows, D_MODEL), lambda i: (0, 2)),
                  full(wglu.shape), full(bglu.shape), full(wbs.shape), full(wba.shape), full(wout.shape),
                  full(lng.shape), full(lnb.shape)],
        out_specs=full((rows, D_MODEL)),
        out_shape=jax.ShapeDtypeStruct((rows, D_MODEL), F32),
        compiler_params=pltpu.CompilerParams(vmem_limit_bytes=VMEM_LIMIT),
        name="final_sample",
    )(x2d, yssm, proj_s, o_a, proj_s, proj_s, proj_s, mods, wglu, bglu, wbs, wba, wout, lng, lnb)


def _bucket_table():
    max_exact = N_BUCKETS // 2
    dist = np.arange(WINDOW + 1)
    df = np.maximum(dist, 1).astype(np.float32)
    large = max_exact + (np.log(df / np.float32(max_exact)) / np.float32(math.log(WINDOW / max_exact))
                         * np.float32(N_BUCKETS - max_exact)).astype(np.int32)
    large = np.minimum(large, N_BUCKETS - 1)
    return np.where(dist < max_exact, dist, large)


def _dot3_nt(a, b):
    ah = a.astype(BF16)
    al = (a - ah.astype(F32)).astype(BF16)
    bh = b.astype(BF16)
    bl = (b - bh.astype(F32)).astype(BF16)
    return _dot_nt(ah, bh) + _dot_nt(ah, bl) + _dot_nt(al, bh)


def _ssm_prep_body(p1_ref, p2_ref, kf_ref, bt_ref, bts_ref, c2_ref, c2s_ref, d_ref,
                   lhs_ref, bpow_ref, cpow_ref, bre_ref, bim_ref, cblk_ref):
    lane = lax.broadcasted_iota(jnp.int32, (SSM_GROUP, SW), 1)
    left = lane < SSM_STATE
    sgn = jnp.where(left, 1.0, -1.0)
    lane_blk = lax.broadcasted_iota(jnp.int32, (CW, CW), 1) // SSM_GROUP
    diag = lax.broadcasted_iota(jnp.int32, (CW, CW), 0) == lax.broadcasted_iota(jnp.int32, (CW, CW), 1)
    bre_ref[...] = jnp.zeros_like(bre_ref)
    bim_ref[...] = jnp.zeros_like(bim_ref)
    cblk_ref[...] = jnp.zeros_like(cblk_ref)
    for gl in range(GT):
        p1, p2 = p1_ref[gl], p2_ref[gl]
        k1, k2 = kf_ref[gl, 0:1, :], kf_ref[gl, 1:2, :]
        bt, bts = bt_ref[gl], bts_ref[gl]
        c2, c2s = c2_ref[gl], c2s_ref[gl]
        bb = k1 * bt + k2 * bts
        bbs = k1 * bts - k2 * bt
        ca = [sgn * (c2 * p1[t:t + 1, :] + c2s * p2[t:t + 1, :]) for t in range(CHUNK + 1)]
        cpow_ref[gl] = jnp.concatenate(ca[1:], axis=0).astype(BF16)
        bpow_ref[gl] = jnp.concatenate(
            [p1[CHUNK - 1 - s:CHUNK - s, :] * bb + p2[CHUNK - 1 - s:CHUNK - s, :] * bbs for s in range(CHUNK)],
            axis=0).astype(BF16)
        e = _dot3_nt(jnp.concatenate(ca[:CHUNK], axis=0), jnp.concatenate([bb] * CHUNK, axis=0))
        acc = jnp.where(diag, d_ref[gl], 0.0)
        for s in range(CHUNK):
            sh = e if s == 0 else jnp.concatenate([jnp.zeros((s * SSM_GROUP, CW), F32), e[:CW - s * SSM_GROUP, :]], axis=0)
            acc = acc + jnp.where(lane_blk == s, sh, 0.0)
        lhs_ref[gl] = acc.astype(BF16)
        rs = slice(gl * SSM_GROUP, (gl + 1) * SSM_GROUP)
        ts = slice((gl // 2) * SW, (gl // 2 + 1) * SW)
        ti = slice(GT * SSM_STATE + (gl // 2) * SW, GT * SSM_STATE + (gl // 2 + 1) * SW)
        if gl % 2 == 0:
            bre_ref[rs, ts] = jnp.where(left, bb, 0.0).astype(BF16)
            bim_ref[rs, ts] = jnp.where(left, bbs, 0.0).astype(BF16)
            cblk_ref[rs, ts] = jnp.where(left, c2, 0.0).astype(BF16)
            cblk_ref[rs, ti] = jnp.where(left, -c2s, 0.0).astype(BF16)
        else:
            bre_ref[rs, ts] = jnp.where(left, 0.0, bbs).astype(BF16)
            bim_ref[rs, ts] = jnp.where(left, 0.0, bb).astype(BF16)
            cblk_ref[rs, ts] = jnp.where(left, 0.0, c2s).astype(BF16)
            cblk_ref[rs, ti] = jnp.where(left, 0.0, -c2).astype(BF16)


def _ssm_params(lam_re, lam_im, log_delta, b_re, b_im, c_re, c_im, d_skip):
    g = SSM_GROUPS
    dup = lambda a: jnp.concatenate([a, a], axis=-1)
    lr, li = dup(lam_re.astype(F32)), dup(lam_im.astype(F32))
    dt = jnp.exp(log_delta.astype(F32))[:, None]
    half_sign = jnp.asarray(np.where(np.arange(SW) < SSM_STATE, -1.0, 1.0), F32)
    tau = jnp.asarray(POW_TAUS, F32)[None, :, None]
    mag = jnp.exp((lr * dt)[:, None, :] * tau)
    ang = (li * dt)[:, None, :] * tau
    p1 = mag * jnp.cos(ang)
    pim = mag * jnp.sin(ang)
    p2 = pim * half_sign
    ar, ai = p1[:, 1, :], pim[:, 1, :]
    den = lr * lr + li * li
    nr = ar - 1.0
    k1 = (nr * lr + ai * li) / den
    k2 = (ai * lr - nr * li) / den * half_sign
    kf = jnp.concatenate([k1[:, None, :], k2[:, None, :], jnp.zeros((g, 6, SW), F32)], axis=1)
    brt, bit = jnp.swapaxes(b_re, 1, 2), jnp.swapaxes(b_im, 1, 2)
    bt = jnp.concatenate([brt, bit], axis=-1)
    bts = jnp.concatenate([bit, brt], axis=-1)
    c2 = jnp.concatenate([c_re, c_im], axis=-1)
    c2s = jnp.concatenate([c_im, c_re], axis=-1)
    dtile = jnp.tile(d_skip.reshape(g, 1, SSM_GROUP), (1, 1, CHUNK))

    gspec = lambda r, w: pl.BlockSpec((GT, r, w), lambda a: (a, 0, 0))
    tile_spec = lambda w: pl.BlockSpec((None, GT * SSM_GROUP, w), lambda a: (a, 0, 0))
    nrow = len(POW_TAUS)
    lhs, bpow, cpow, bre, bim, cblk = pl.pallas_call(
        _ssm_prep_body,
        grid=(g // GT,),
        in_specs=[gspec(nrow, SW), gspec(nrow, SW), gspec(8, SW), gspec(SSM_GROUP, SW), gspec(SSM_GROUP, SW),
                  gspec(SSM_GROUP, SW), gspec(SSM_GROUP, SW), gspec(1, CW)],
        out_specs=[gspec(CW, CW), gspec(CW, SW), gspec(CW, SW),
                   tile_spec(GT * SSM_STATE), tile_spec(GT * SSM_STATE), tile_spec(2 * GT * SSM_STATE)],
        out_shape=[jax.ShapeDtypeStruct((g, CW, CW), BF16), jax.ShapeDtypeStruct((g, CW, SW), BF16),
                   jax.ShapeDtypeStruct((g, CW, SW), BF16),
                   jax.ShapeDtypeStruct((GT, GT * SSM_GROUP, GT * SSM_STATE), BF16),
                   jax.ShapeDtypeStruct((GT, GT * SSM_GROUP, GT * SSM_STATE), BF16),
                   jax.ShapeDtypeStruct((GT, GT * SSM_GROUP, 2 * GT * SSM_STATE), BF16)],
        name="ssm_prep",
    )(p1, p2, kf, bt, bts, c2, c2s, dtile)
    ar_row = ar[:, :SSM_STATE].reshape(1, -1)
    ai_row = ai[:, :SSM_STATE].reshape(1, -1)
    return lhs, bpow, cpow, p1, p2, ar_row, ai_row, bre, bim, cblk


def _attn_params(rel_bias, sinks):
    hp = lax.Precision.HIGHEST
    rb = rel_bias.astype(F32)
    onehot = jnp.asarray(_bucket_table()[:, None] == np.arange(N_BUCKETS)[None, :], F32)
    tbl = jnp.dot(onehot, rb, precision=hp)
    sink_rows = jnp.repeat(sinks.astype(F32).reshape(N_KV_HEADS, 1, KV_GROUP), WINDOW, axis=2)
    ds = np.clip(WINDOW - np.arange(KROWS), 0, WINDOW)
    bs = jnp.dot(jnp.asarray(ds[:, None] == np.arange(WINDOW + 1)[None, :], F32), tbl, precision=hp)
    bs = jnp.transpose(bs.reshape(KROWS, N_KV_HEADS, KV_GROUP), (0, 2, 1)).reshape(KROWS, N_HEADS)
    bias_s = jnp.pad(bs, ((0, 0), (0, 128 - N_HEADS)))
    sk = jnp.transpose(sinks.astype(F32).reshape(N_KV_HEADS, KV_GROUP), (1, 0)).reshape(1, N_HEADS)
    sink_row = jnp.pad(sk, ((0, 0), (0, 128 - N_HEADS)))
    seg_np = np.zeros((D_MODEL, 128), np.float32)
    for j in range(KV_GROUP):
        for kh in range(N_KV_HEADS):
            r0 = j * KV_WIDTH + kh * HEAD_DIM
            seg_np[r0:r0 + HEAD_DIM, j * N_KV_HEADS + kh] = 1.0
    seg = jnp.asarray(seg_np, BF16)
    segt = jnp.asarray(seg_np.T, BF16)
    return sink_rows, bias_s, sink_row, seg, segt


def kernel(x_prompt, x_sample, c_prompt, c_sample, state_ssm_re, state_ssm_im, cache_swa_k, cache_swa_v,
           w_ada, b_ada, w_in, ssm_lambda_re, ssm_lambda_im, ssm_log_delta, ssm_b_re, ssm_b_im,
           ssm_c_re, ssm_c_im, ssm_d, w_glu, b_glu, attn_sinks, rel_bias, w_branch_s, w_branch_a,
           w_out, ln_g, ln_b):
    assert w_ada.shape[0] == 1, "single-layer trunk"
    n_batch, seq, _ = x_prompt.shape
    n_dec = x_sample.shape[0]

    wi = w_in[0]
    w_in_b = jnp.concatenate([wi[:, 0:1024], wi[:, 1024:2048],
                              _regroup_heads(wi[:, 2048:3072], 1) * (HEAD_DIM ** -0.5),
                              _regroup_heads(wi[:, 3584:4608], 1), wi[:, 4608:5632], wi[:, 5632:6656],
                              wi[:, 3072:3328], wi[:, 3328:3584]], axis=1).astype(BF16)
    wvt = jnp.transpose(wi[:, 3328:3584]).astype(BF16)
    wut = jnp.transpose(wi[:, 0:D_MODEL]).astype(BF16)
    eye = jnp.eye(WINDOW, dtype=BF16)
    wglu = w_glu[0].astype(BF16)
    bglu = b_glu[0].reshape(1, -1).astype(F32)
    wbs = w_branch_s[0].astype(BF16)
    wba = _regroup_heads(w_branch_a[0], 0).astype(BF16)
    wout = w_out[0].astype(BF16)
    lng = ln_g[0].reshape(1, -1).astype(F32)
    lnb = ln_b[0].reshape(1, -1).astype(F32)
    lhs, bpow, cpow, p1, p2, ar_row, ai_row, bre, bim, cblk = _ssm_params(
        ssm_lambda_re[0], ssm_lambda_im[0], ssm_log_delta[0], ssm_b_re[0], ssm_b_im[0],
        ssm_c_re[0], ssm_c_im[0], ssm_d[0])
    sink_rows, bias_s, sink_row, seg, segt = _attn_params(rel_bias, attn_sinks[0])

    c_all = jnp.concatenate([c_prompt, jnp.zeros((8 - n_batch, D_MODEL), F32), c_sample], axis=0)
    mod = _ada(c_all, w_ada[0], b_ada[0])
    modp = mod[:n_batch].reshape(n_batch * 3, 1, D_MODEL)
    mods = mod[8:8 + n_dec]

    xp = x_prompt.reshape(n_batch * seq, D_MODEL)
    yssm, hfin = _ssm_prompt(x_prompt, modp, wut, lhs, bpow, cpow, p1, p2)
    yp, ktail, vtail = _main_prompt(rel_bias.astype(F32).reshape(-1), xp, yssm, modp, w_in_b, wvt, sink_rows, eye,
                                    wglu, bglu, wbs, wba, wout, lng, lnb, n_batch, seq)
    yp = yp.reshape(n_batch, seq, D_MODEL)
    p_hr = hfin[:, :, 0, :SSM_STATE][None]
    p_hi = hfin[:, :, 0, SSM_STATE:][None]
    p_k = ktail.reshape(1, n_batch, WINDOW, N_KV_HEADS, HEAD_DIM)
    p_v = vtail.reshape(1, n_batch, WINDOW, N_KV_HEADS, HEAD_DIM)

    xs = x_sample.reshape(n_dec, D_MODEL)
    proj_s = _proj(xs, mods, mods,
                   pl.BlockSpec((n_dec, D_MODEL), lambda i: (0, 0)),
                   pl.BlockSpec((n_dec, D_MODEL), lambda i: (0, 1)),
                   w_in_b, n_dec, D_IN)
    ns = SSM_GROUPS * SSM_STATE
    ys_s, s_hr, s_hi = _sample_ssm(proj_s, state_ssm_re[0].reshape(n_dec, ns), state_ssm_im[0].reshape(n_dec, ns),
                                   ar_row, ai_row, bre, bim, cblk, ssm_d[0].reshape(1, -1).astype(F32))
    o_s, nk, nv = _sample_attn(proj_s, cache_swa_k[0].reshape(n_dec, WINDOW, KV_WIDTH),
                               cache_swa_v[0].reshape(n_dec, WINDOW, KV_WIDTH), bias_s, sink_row, seg, segt)
    ysmp = _final_sample(xs, ys_s, o_s, proj_s, mods, wglu, bglu, wbs, wba, wout, lng, lnb)

    return (yp, ysmp.reshape(n_dec, 1, D_MODEL),
            p_hr, p_hi, p_k, p_v,
            s_hr.reshape(1, n_dec, SSM_GROUPS, SSM_STATE), s_hi.reshape(1, n_dec, SSM_GROUPS, SSM_STATE),
            nk.reshape(1, n_dec, WINDOW, N_KV_HEADS, HEAD_DIM), nv.reshape(1, n_dec, WINDOW, N_KV_HEADS, HEAD_DIM))
```

```python
import functools
import math

import numpy as np
import jax
import jax.numpy as jnp
from jax import lax
from jax.experimental import pallas as pl
from jax.experimental.pallas import tpu as pltpu

F32 = jnp.float32
BF16 = jnp.bfloat16

D_MODEL = 1024
SSM_GROUPS = 64
SSM_GROUP = 16
SSM_STATE = 64
N_HEADS = 16
HEAD_DIM = 64
N_KV_HEADS = 4
KV_GROUP = 4
KV_WIDTH = N_KV_HEADS * HEAD_DIM
WINDOW = 128
N_BUCKETS = 32
NEG_INF = -1e30
LN_EPS = 1e-5
DEPTH = 1
DEEPNORM_ALPHA = (2 * DEPTH) ** 0.25
D_IN = 6656
CHUNK = 16
CW = CHUNK * SSM_GROUP
SW = 2 * SSM_STATE

COL_U, COL_ZS, COL_Q, COL_ZA, COL_GS, COL_GA = 0, 1, 2, 3, 4, 5
COL_K, COL_V = 24, 25

VMEM_LIMIT = 56 * 1024 * 1024
VMEM_LIMIT_SSM = 60 * 1024 * 1024


def _regroup_heads(w, axis):
    shp = w.shape
    if axis == 1:
        return jnp.transpose(w.reshape(shp[0], N_KV_HEADS, KV_GROUP, HEAD_DIM), (0, 2, 1, 3)).reshape(shp)
    return jnp.transpose(w.reshape(N_KV_HEADS, KV_GROUP, HEAD_DIM, shp[1]), (1, 0, 2, 3)).reshape(shp)


def _sigmoid(x):
    return 0.5 * jnp.tanh(0.5 * x) + 0.5


def _silu(x):
    return x * _sigmoid(x)


def _gelu_tanh(x):
    c = math.sqrt(2.0 / math.pi)
    hx = 0.5 * x
    return hx + hx * jnp.tanh(x * (c + (c * 0.044715) * (x * x)))


def _dot(a, b):
    return jnp.dot(a, b, preferred_element_type=F32)


def _dot_nt(a, b):
    return lax.dot_general(a, b, (((1,), (1,)), ((), ())), preferred_element_type=F32)


def _ada_body(c_ref, w_ref, b_ref, o_ref):
    c = c_ref[...]
    sc = _silu(c).astype(BF16)
    o_ref[...] = _dot(sc, w_ref[...].astype(BF16)) + b_ref[...]


def _ada(c_all, w_ada, b_ada):
    rows = c_all.shape[0]
    bn = 512
    return pl.pallas_call(
        _ada_body,
        grid=(3 * D_MODEL // bn,),
        in_specs=[pl.BlockSpec((rows, D_MODEL), lambda j: (0, 0)),
                  pl.BlockSpec((D_MODEL, bn), lambda j: (0, j)),
                  pl.BlockSpec((1, bn), lambda j: (0, j))],
        out_specs=pl.BlockSpec((rows, bn), lambda j: (0, j)),
        out_shape=jax.ShapeDtypeStruct((rows, 3 * D_MODEL), F32),
        name="ada",
    )(c_all, w_ada, b_ada.reshape(1, -1))


PROJ_CB = 512


def _proj_body(x_ref, shift_ref, scale_ref, w_ref, o_ref, *, n_out):
    h = x_ref[...] * (1.0 + scale_ref[...]) + shift_ref[...]
    hb = h.astype(BF16)
    for j in range(n_out // PROJ_CB):
        sl = slice(j * PROJ_CB, (j + 1) * PROJ_CB)
        o_ref[:, sl] = _dot(hb, w_ref[:, sl]).astype(BF16)


def _proj(x2d, shift_arr, scale_arr, shift_spec, scale_spec, w_in_b, tm, n_out):
    rows = x2d.shape[0]
    return pl.pallas_call(
        functools.partial(_proj_body, n_out=n_out),
        grid=(rows // tm,),
        in_specs=[pl.BlockSpec((tm, D_MODEL), lambda i: (i, 0)),
                  shift_spec, scale_spec,
                  pl.BlockSpec((D_MODEL, n_out), lambda i: (0, 0), pipeline_mode=pl.Buffered(1))],
        out_specs=pl.BlockSpec((tm, n_out), lambda i: (i, 0)),
        out_shape=jax.ShapeDtypeStruct((rows, n_out), BF16),
        compiler_params=pltpu.CompilerParams(vmem_limit_bytes=VMEM_LIMIT),
        name="proj",
    )(x2d, shift_arr, scale_arr, w_in_b)


NKS = 128
GC = 16


def _cmul(c1, c2, x):
    return c1 * x + c2 * pltpu.roll(x, SSM_STATE, x.ndim - 1)


SSM_TOK = NKS * CHUNK
SUB = 8
NPH = CHUNK // SUB


def _dot_tn(a, b):
    return lax.dot_general(a, b, (((0,), (0,)), ((), ())), preferred_element_type=F32)


SEG = 8
SEG_LEN = NKS // SEG
POW_TAUS = (tuple(range(CHUNK + 1)) + tuple(CHUNK * m for m in range(2, SEG_LEN))
            + tuple(CHUNK * SEG_LEN * m for m in range(1, SEG + 1)) + (0,))
ROW_CHUNK1 = CHUNK
ROW_SEG1 = CHUNK + SEG_LEN - 1


def _ssm_body(x_ref, shift_ref, scale_ref, wut_ref, lhs_ref, bpow_ref, cpow_ref, p1_ref, p2_ref, unperm_ref,
              y_ref, hfin_ref, r_scr, yt_scr, s_scr, carry_scr):
    i = pl.program_id(1)
    j = pl.program_id(2)

    @pl.when((i == 0) & (j == 0))
    def _():
        carry_scr[...] = jnp.zeros_like(carry_scr)

    @pl.when(j < NPH)
    def _():
        pairs = range(SUB // 2)
        hbs = []
        for sp in pairs:
            h = jnp.concatenate([x_ref[:, b, s, :] for s in (2 * sp, 2 * sp + 1) for b in range(SEG_LEN)],
                                axis=0)
            hbs.append((h * (1.0 + scale_ref[...]) + shift_ref[...]).astype(BF16))
        uts = [_dot_nt(wut_ref[...], hb).astype(BF16) for hb in hbs]
        for sp, ut in zip(pairs, uts):
            ut = ut.reshape(SSM_GROUPS, SSM_GROUP, 2 * NKS)
            pair = jnp.concatenate([ut[:, :, :NKS], ut[:, :, NKS:]], axis=1)
            r0 = pl.multiple_of(j * SUB * SSM_GROUP + sp * 2 * SSM_GROUP, 2 * SSM_GROUP)
            r_scr[:, pl.ds(r0, 2 * SSM_GROUP), :] = pair

    @pl.when(j == NPH - 1)
    def _():
        def local(g):
            rg = r_scr[g]
            yt_scr[g] = _dot(lhs_ref[g], rg)
            s_scr[g] = _dot_tn(rg, bpow_ref[g])

        def carried(g):
            yt_scr[g] += _dot_nt(cpow_ref[g], s_scr[g].astype(BF16))

        row = lax.broadcasted_iota(jnp.int32, (GC, SEG, SW), 1)
        seg_rows = lambda b: slice(b * SEG, (b + 1) * SEG)

        def scan(gc):
            gs = slice(gc * GC, (gc + 1) * GC)
            p1, p2 = p1_ref[gs], p2_ref[gs]
            c1, c2 = p1[:, ROW_CHUNK1:ROW_CHUNK1 + 1, :], p2[:, ROW_CHUNK1:ROW_CHUNK1 + 1, :]
            h = jnp.zeros((GC, SEG, SW), F32)
            hs = h
            for b in range(SEG_LEN):
                sb = s_scr[gs, seg_rows(b), :]
                h, hs = sb + c1 * h + c2 * hs, pltpu.roll(sb, SSM_STATE, 2) + c1 * hs - c2 * h
                s_scr[gs, seg_rows(b), :] = h
                yield
            e = h
            for d in (1, 2, 4):
                rd = ROW_SEG1 + d - 1
                sh = jnp.where(row >= d, pltpu.roll(e, d, 1), 0.0)
                e = e + _cmul(p1[:, rd:rd + 1, :], p2[:, rd:rd + 1, :], sh)
            carry = carry_scr[gs]
            e = e + _cmul(p1[:, ROW_SEG1:ROW_SEG1 + SEG, :], p2[:, ROW_SEG1:ROW_SEG1 + SEG, :], carry)
            ein = jnp.where(row == 0, carry, pltpu.roll(e, 1, 1))
            eins = pltpu.roll(ein, SSM_STATE, 2)
            carry_scr[gs] = jnp.broadcast_to(e[:, SEG - 1:SEG, :], (GC, SEG, SW))
            yield
            for b in range(SEG_LEN - 1, 0, -1):
                rb = ROW_CHUNK1 + b - 1
                s_scr[gs, seg_rows(b), :] = (s_scr[gs, seg_rows(b - 1), :]
                                             + p1[:, rb:rb + 1, :] * ein + p2[:, rb:rb + 1, :] * eins)
                yield
            s_scr[gs, seg_rows(0), :] = ein

        def interleave(fn, groups, gen):
            for g in groups:
                fn(g)
                for _ in range(2):
                    next(gen, None)
            for _ in gen:
                pass

        n_sets = SSM_GROUPS // GC
        sets = [range(q * GC, (q + 1) * GC) for q in range(n_sets)]
        for g in sets[0]:
            local(g)
        for q in range(1, n_sets):
            interleave(local, sets[q], scan(q - 1))
        interleave(carried, sets[0], scan(n_sets - 1))
        for q in range(1, n_sets):
            for g in sets[q]:
                carried(g)
        hfin_ref[...] = carry_scr[...]

    @pl.when(j >= NPH)
    def _():
        for sp in range(SUB):
            t0 = pl.multiple_of((j - NPH) * SUB * SSM_GROUP + sp * SSM_GROUP, SSM_GROUP)
            yt = yt_scr[:, pl.ds(t0, SSM_GROUP), :].reshape(D_MODEL, NKS)
            y_ref[:, sp, :] = _dot_nt(unperm_ref[...], yt.astype(BF16))


def _ssm_prompt(x_prompt, modp, wut, lhs, bpow, cpow, p1, p2):
    n_batch, seq, _ = x_prompt.shape
    steps = seq // SSM_TOK
    xv = x_prompt.reshape(n_batch * steps * SEG, SEG_LEN, CHUNK, D_MODEL)
    mod_spec = lambda k: pl.BlockSpec((None, 1, D_MODEL), lambda b, i, j, k=k: (b * 3 + k, 0, 0))
    k = np.arange(NKS)
    perm_np = np.zeros((NKS, NKS), np.float32)
    perm_np[k, (k % SEG_LEN) * SEG + k // SEG_LEN] = 1.0
    consts = (wut, lhs, bpow, cpow, p1, p2, jnp.asarray(perm_np, BF16))
    y, hfin = pl.pallas_call(
        _ssm_body,
        grid=(n_batch, steps, 2 * NPH),
        in_specs=[pl.BlockSpec((SEG, SEG_LEN, SUB, D_MODEL),
                               lambda b, i, j: (b * steps + i, 0, jnp.minimum(j, NPH - 1), 0)),
                  mod_spec(0), mod_spec(1)] + [_const_spec(c.shape) for c in consts],
        out_specs=[pl.BlockSpec((NKS, SUB, D_MODEL), lambda b, i, j: (b * steps + i, jnp.maximum(j - NPH, 0), 0)),
                   pl.BlockSpec((None, SSM_GROUPS, 8, SW), lambda b, i, j: (b, 0, 0, 0))],
        out_shape=[jax.ShapeDtypeStruct((n_batch * seq // CHUNK, CHUNK, D_MODEL), F32),
                   jax.ShapeDtypeStruct((n_batch, SSM_GROUPS, 8, SW), F32)],
        scratch_shapes=[pltpu.VMEM((SSM_GROUPS, CW, NKS), BF16), pltpu.VMEM((SSM_GROUPS, CW, NKS), F32),
                        pltpu.VMEM((SSM_GROUPS, NKS, SW), F32), pltpu.VMEM((SSM_GROUPS, 8, SW), F32)],
        compiler_params=pltpu.CompilerParams(dimension_semantics=("arbitrary", "arbitrary", "arbitrary"),
                                             vmem_limit_bytes=VMEM_LIMIT_SSM),
        name="ssm_prompt",
    )(xv, modp, modp, *consts)
    return y.reshape(n_batch * seq, D_MODEL), hfin


def _post(chunks, gate, wglu_ref, bglu_ref, wbs_ref, wba_ref, wout_ref, lng_ref, lnb_ref):
    data = [load() for load in chunks]
    ys = [_gelu_tanh(d[0]) for d in data]
    glu = [y * _sigmoid(_dot(y.astype(BF16), wglu_ref[...]) + bglu_ref[...]) for y in ys]
    b_s = [_dot(g.astype(BF16) * _silu(d[1]), wbs_ref[...]) for g, d in zip(glu, data)]
    b_a = [_dot(d[2].astype(BF16) * _silu(d[3]), wba_ref[...]) for d in data]
    m = [_sigmoid(d[4]).astype(F32) * s + _sigmoid(d[5]).astype(F32) * a for d, s, a in zip(data, b_s, b_a)]
    out = [_dot(v.astype(BF16), wout_ref[...]) for v in m]
    res = []
    for d, o in zip(data, out):
        r = DEEPNORM_ALPHA * d[6] + gate * o
        mu = jnp.mean(r, axis=-1, keepdims=True)
        rc = r - mu
        var = jnp.mean(rc * rc, axis=-1, keepdims=True)
        res.append(rc * lax.rsqrt(var + LN_EPS) * lng_ref[...] + lnb_ref[...])
    return res


TB = 512
NQB = TB // WINDOW
W_OFF = D_MODEL
W_MAIN = 5 * D_MODEL + KV_WIDTH
ONES_ROWS = 16
POST_ROWS = 256


def _const_spec(shape):
    nd = len(shape)
    return pl.BlockSpec(shape, lambda *_: (0,) * nd, pipeline_mode=pl.Buffered(1))


def _bucket_thresholds():
    tbl = _bucket_table()
    return [int(np.argmax(tbl >= k)) for k in range(1, N_BUCKETS)]


def _fill_bias_table(rb_ref, bias_s):
    si = lax.broadcasted_iota(jnp.int32, (2 * WINDOW, WINDOW), 0)
    qi = lax.broadcasted_iota(jnp.int32, (2 * WINDOW, WINDOW), 1)
    dist = qi + WINDOW - si
    bid = jnp.zeros((2 * WINDOW, WINDOW), jnp.int32)
    for thr in _bucket_thresholds():
        bid = bid + (dist >= thr).astype(jnp.int32)
    valid = (dist >= 0) & (dist <= WINDOW)
    own = si >= WINDOW

    def per_head(h, c):
        t = jnp.zeros((2 * WINDOW, WINDOW), F32)
        for b in range(N_BUCKETS):
            t = jnp.where(bid == b, rb_ref[b * N_HEADS + h], t)
        t = jnp.where(valid, t, NEG_INF)
        kh, j = h // KV_GROUP, h % KV_GROUP
        bias_s[0, kh, j] = t
        bias_s[1, kh, j] = jnp.where(own, t, NEG_INF)
        return c

    lax.fori_loop(0, N_HEADS, per_head, 0)


def _main_body(rb_ref, x_ref, yssm_ref, shift_ref, scale_ref, gate_ref, wmain_ref, wvt_ref, sink_ref, eye_ref,
               wglu_ref, bglu_ref, wbs_ref, wba_ref, wout_ref, lng_ref, lnb_ref,
               y_ref, ktail_ref, vtail_ref,
               zs_s, q_s, za_s, gs_s, ga_s, k_s, vt_s, ot_s, o_s, bias_s):
    i = pl.program_id(1)

    @pl.when((pl.program_id(0) == 0) & (i == 0))
    def _():
        _fill_bias_table(rb_ref, bias_s)

    @pl.when(i == 0)
    def _():
        k_s[0:WINDOW, :] = jnp.zeros((WINDOW, KV_WIDTH), BF16)
        vt_s[:, 0:WINDOW] = jnp.zeros((KV_WIDTH, WINDOW), BF16)

    x = x_ref[...]
    hb = (x * (1.0 + scale_ref[...]) + shift_ref[...]).astype(BF16)

    def project(dst, c0, c):
        lo = W_OFF + c0 + c * 512
        dst[:, c * 512:(c + 1) * 512] = _dot(hb, wmain_ref[:, lo:lo + 512]).astype(BF16)

    for c in range(2):
        project(q_s, D_MODEL, c)
    deferred = [(dst, c0, c) for dst, c0 in ((zs_s, 0), (za_s, 2 * D_MODEL), (gs_s, 3 * D_MODEL), (ga_s, 4 * D_MODEL))
                for c in range(2)]

    k_s[WINDOW:WINDOW + TB, :] = _dot(hb, wmain_ref[:, W_OFF + 5 * D_MODEL:W_OFF + W_MAIN]).astype(BF16)
    vt_s[:, WINDOW:WINDOW + TB] = _dot_nt(wvt_ref[...], hb).astype(BF16)

    first = jnp.where(i == 0, 1, 0)
    lane = lax.broadcasted_iota(jnp.int32, (2 * WINDOW, 128), 1)
    ones = jnp.ones((ONES_ROWS, 2 * WINDOW), BF16)
    for blk in range(NQB):
        rows = slice(blk * WINDOW, (blk + 1) * WINDOW)
        kcat = k_s[blk * WINDOW:blk * WINDOW + 2 * WINDOW, :]
        vtc = vt_s[:, blk * WINDOW:blk * WINDOW + 2 * WINDOW]
        sel = first if blk == 0 else 0
        heads = range(N_KV_HEADS)
        sts = []
        for kh in heads:
            pair, odd = kh // 2, kh % 2
            kcm = jnp.where((lane >= HEAD_DIM) == bool(odd), kcat[:, pair * 128:(pair + 1) * 128], jnp.zeros((), BF16))
            qp = jnp.concatenate(
                [q_s[rows, j * KV_WIDTH + pair * 128:j * KV_WIDTH + (pair + 1) * 128] for j in range(KV_GROUP)],
                axis=0)
            bias = jnp.concatenate([bias_s[sel, kh, j] for j in range(KV_GROUP)], axis=1)
            sts.append(_dot_nt(kcm, qp) + bias)
        mine = deferred[blk * len(deferred) // NQB:(blk + 1) * len(deferred) // NQB]
        ms = [jnp.maximum(jnp.max(sts[kh], axis=0, keepdims=True), sink_ref[kh]) for kh in heads]
        project(*mine[0])
        es = [jnp.exp(sts[kh] - ms[kh]).astype(BF16) for kh in heads]
        rs = [_dot(jnp.concatenate([vtc[kh * HEAD_DIM:(kh + 1) * HEAD_DIM, :], ones], axis=0), es[kh])
              for kh in heads]
        for spec in mine[1:]:
            project(*spec)
        for kh in heads:
            den = rs[kh][HEAD_DIM:HEAD_DIM + 1, :] + jnp.exp(sink_ref[kh] - ms[kh])
            ot = (rs[kh][:HEAD_DIM, :] / den).astype(BF16)
            for j in range(KV_GROUP):
                ot_s[blk, j, kh * HEAD_DIM:(kh + 1) * HEAD_DIM, :] = ot[:, j * WINDOW:(j + 1) * WINDOW]
        for j in range(KV_GROUP):
            o_s[rows, j * KV_WIDTH:(j + 1) * KV_WIDTH] = _dot_nt(eye_ref[...], ot_s[blk, j]).astype(BF16)

    k_s[0:WINDOW, :] = k_s[TB:TB + WINDOW, :]
    vt_s[:, 0:WINDOW] = vt_s[:, TB:TB + WINDOW]

    def loader(c):
        rs = slice(c * POST_ROWS, (c + 1) * POST_ROWS)
        return lambda: (yssm_ref[rs, :], zs_s[rs, :], o_s[rs, :], za_s[rs, :], gs_s[rs, :], ga_s[rs, :], x_ref[rs, :])

    ys = _post([loader(c) for c in range(TB // POST_ROWS)], gate_ref[...],
               wglu_ref, bglu_ref, wbs_ref, wba_ref, wout_ref, lng_ref, lnb_ref)
    for c, yc in enumerate(ys):
        y_ref[c * POST_ROWS:(c + 1) * POST_ROWS, :] = yc

    @pl.when(i == pl.num_programs(1) - 1)
    def _():
        ktail_ref[...] = k_s[0:WINDOW, :].astype(F32)
        vtail_ref[...] = _dot_nt(eye_ref[...], vt_s[:, 0:WINDOW])


def _main_prompt(rel_bias_flat, x2d, yssm, modp, wmain, wvt, sink_rows, eye, wglu, bglu, wbs, wba, wout, lng, lnb,
                 n_batch, seq):
    steps = seq // TB
    row_spec = pl.BlockSpec((TB, D_MODEL), lambda b, i: (b * steps + i, 0))
    mod_spec = lambda j: pl.BlockSpec((None, 1, D_MODEL), lambda b, i, j=j: (b * 3 + j, 0, 0))
    tail_spec = pl.BlockSpec((None, WINDOW, KV_WIDTH), lambda b, i: (b, 0, 0))
    consts = (wmain, wvt, sink_rows, eye, wglu, bglu, wbs, wba, wout, lng, lnb)
    return pl.pallas_call(
        _main_body,
        grid=(n_batch, steps),
        in_specs=[pl.BlockSpec(memory_space=pltpu.SMEM), row_spec, row_spec, mod_spec(0), mod_spec(1), mod_spec(2)]
        + [_const_spec(c.shape) for c in consts],
        out_specs=[row_spec, tail_spec, tail_spec],
        out_shape=[jax.ShapeDtypeStruct((n_batch * seq, D_MODEL), F32),
                   jax.ShapeDtypeStruct((n_batch, WINDOW, KV_WIDTH), F32),
                   jax.ShapeDtypeStruct((n_batch, WINDOW, KV_WIDTH), F32)],
        scratch_shapes=[pltpu.VMEM((TB, D_MODEL), BF16)] * 5
        + [pltpu.VMEM((TB + WINDOW, KV_WIDTH), BF16), pltpu.VMEM((KV_WIDTH, TB + WINDOW), BF16),
           pltpu.VMEM((NQB, KV_GROUP, KV_WIDTH, WINDOW), BF16), pltpu.VMEM((TB, D_MODEL), BF16),
           pltpu.VMEM((2, N_KV_HEADS, KV_GROUP, 2 * WINDOW, WINDOW), F32)],
        compiler_params=pltpu.CompilerParams(dimension_semantics=("arbitrary", "arbitrary"),
                                             vmem_limit_bytes=VMEM_LIMIT),
        name="main_prompt",
    )(rel_bias_flat, x2d, yssm, modp, modp, modp, *consts)


GT = 8


def _sample_ssm_body(u_ref, h0r_ref, h0i_ref, ar_ref, ai_ref, bre_ref, bim_ref, cblk_ref, d_ref,
                     y_ref, hr_ref, hi_ref):
    ub = u_ref[...]
    ys = []
    for a in range(GT):
        ua = ub[:, a * 128:(a + 1) * 128]
        st = slice(a * 512, (a + 1) * 512)
        xr = _dot(ua, bre_ref[a])
        xi = _dot(ua, bim_ref[a])
        h0r = h0r_ref[:, st]
        h0i = h0i_ref[:, st]
        ar = ar_ref[:, st]
        ai = ai_ref[:, st]
        hr = ar * h0r - ai * h0i + xr
        hi = ar * h0i + ai * h0r + xi
        hr_ref[:, st] = hr
        hi_ref[:, st] = hi
        hcat = jnp.concatenate([hr, hi], axis=1).astype(BF16)
        ys.append(_dot_nt(hcat, cblk_ref[a]))
    y_ref[...] = jnp.concatenate(ys, axis=1) + d_ref[...] * ub.astype(F32)


def _sample_ssm(proj_s, h0r, h0i, ar, ai, bre, bim, cblk, d_row):
    rows = proj_s.shape[0]
    ns = SSM_GROUPS * SSM_STATE
    full = lambda shape: pl.BlockSpec(shape, lambda i: (0,) * len(shape))
    return pl.pallas_call(
        _sample_ssm_body,
        grid=(1,),
        in_specs=[pl.BlockSpec((rows, D_MODEL), lambda i: (0, COL_U)),
                  full((rows, ns)), full((rows, ns)), full((1, ns)), full((1, ns)),
                  full(bre.shape), full(bim.shape), full(cblk.shape), full((1, D_MODEL))],
        out_specs=[full((rows, D_MODEL)), full((rows, ns)), full((rows, ns))],
        out_shape=[jax.ShapeDtypeStruct((rows, D_MODEL), F32),
                   jax.ShapeDtypeStruct((rows, ns), F32),
                   jax.ShapeDtypeStruct((rows, ns), F32)],
        compiler_params=pltpu.CompilerParams(vmem_limit_bytes=VMEM_LIMIT),
        name="sample_ssm",
    )(proj_s, h0r, h0i, ar, ai, bre, bim, cblk, d_row)


SB = 16
KROWS = WINDOW + 8
SEQ_GROUP = 4


def _sample_attn_body(q_ref, kn_ref, vn_ref, ck_ref, cv_ref, bias_ref, sink_ref, seg_ref, segt_ref,
                      o_ref, nk_ref, nv_ref):
    qf = q_ref[...].astype(F32)
    knf = kn_ref[...].astype(F32)
    vnf = vn_ref[...].astype(F32)
    rowk = lax.broadcasted_iota(jnp.int32, (KROWS, 128), 0)
    roww = lax.broadcasted_iota(jnp.int32, (WINDOW, KV_WIDTH), 0)
    sink = sink_ref[...]
    last = roww == WINDOW - 1
    for g0 in range(0, SB, SEQ_GROUP):
        seqs = range(g0, g0 + SEQ_GROUP)
        scores = []
        for b in seqs:
            kall = jnp.concatenate([ck_ref[b], jnp.broadcast_to(knf[b:b + 1], (8, KV_WIDTH))], axis=0)
            qrow = qf[b:b + 1]
            prod = jnp.concatenate([kall * qrow[:, j * KV_WIDTH:(j + 1) * KV_WIDTH] for j in range(KV_GROUP)], axis=1)
            s = _dot(prod.astype(BF16), seg_ref[...]) + bias_ref[...]
            scores.append(jnp.where(rowk <= WINDOW, s, NEG_INF))
        probs = []
        for s in scores:
            m = jnp.maximum(jnp.max(s, axis=0, keepdims=True), sink)
            e = jnp.exp(s - m)
            den = jnp.sum(e, axis=0, keepdims=True) + jnp.exp(sink - m)
            probs.append((e / den).astype(BF16))
        pexps = [_dot(p, segt_ref[...]) for p in probs]
        for b, pexp in zip(seqs, pexps):
            vb = cv_ref[b]
            vall = jnp.concatenate([vb, jnp.broadcast_to(vnf[b:b + 1], (8, KV_WIDTH))], axis=0)
            v4 = jnp.concatenate([vall] * KV_GROUP, axis=1)
            o_ref[b:b + 1, :] = jnp.sum(pexp * v4, axis=0, keepdims=True)
            nv_ref[b] = jnp.where(last, jnp.broadcast_to(vnf[b:b + 1], (WINDOW, KV_WIDTH)), pltpu.roll(vb, WINDOW - 1, 0))
            nk_ref[b] = jnp.where(last, jnp.broadcast_to(knf[b:b + 1], (WINDOW, KV_WIDTH)),
                                  pltpu.roll(ck_ref[b], WINDOW - 1, 0))


def _sample_attn(proj_s, ck, cv, bias_s, sink_row, seg, segt):
    rows = proj_s.shape[0]
    cache_spec = pl.BlockSpec((SB, WINDOW, KV_WIDTH), lambda i: (i, 0, 0))
    return pl.pallas_call(
        _sample_attn_body,
        grid=(rows // SB,),
        in_specs=[pl.BlockSpec((SB, D_MODEL), lambda i: (i, COL_Q)),
                  pl.BlockSpec((SB, KV_WIDTH), lambda i: (i, COL_K)),
                  pl.BlockSpec((SB, KV_WIDTH), lambda i: (i, COL_V)),
                  cache_spec, cache_spec,
                  pl.BlockSpec(bias_s.shape, lambda i: (0, 0)),
                  pl.BlockSpec(sink_row.shape, lambda i: (0, 0)),
                  pl.BlockSpec(seg.shape, lambda i: (0, 0)),
                  pl.BlockSpec(segt.shape, lambda i: (0, 0))],
        out_specs=[pl.BlockSpec((SB, D_MODEL), lambda i: (i, 0)), cache_spec, cache_spec],
        out_shape=[jax.ShapeDtypeStruct((rows, D_MODEL), F32),
                   jax.ShapeDtypeStruct(ck.shape, F32),
                   jax.ShapeDtypeStruct(cv.shape, F32)],
        compiler_params=pltpu.CompilerParams(vmem_limit_bytes=VMEM_LIMIT),
        name="sample_attn",
    )(proj_s, proj_s, proj_s, ck, cv, bias_s, sink_row, seg, segt)


def _sample_final_body(x_ref, yssm_ref, zs_ref, o_ref, za_ref, gs_ref, ga_ref, gate_ref,
                       wglu_ref, bglu_ref, wbs_ref, wba_ref, wout_ref, lng_ref, lnb_ref, y_ref):
    load = lambda: (yssm_ref[...], zs_ref[...], o_ref[...], za_ref[...], gs_ref[...], ga_ref[...], x_ref[...])
    y_ref[...] = _post([load], gate_ref[...], wglu_ref, bglu_ref, wbs_ref, wba_ref, wout_ref, lng_ref, lnb_ref)[0]


def _final_sample(x2d, yssm, o_a, proj_s, mods, wglu, bglu, wbs, wba, wout, lng, lnb):
    rows = x2d.shape[0]
    full = lambda shape: pl.BlockSpec(shape, lambda i: (0,) * len(shape))
    pcol = lambda c: pl.BlockSpec((rows, D_MODEL), lambda i, c=c: (0, c))
    return pl.pallas_call(
        _sample_final_body,
        grid=(1,),
        in_specs=[full((rows, D_MODEL)), full((rows, D_MODEL)), pcol(COL_ZS), full((rows, D_MODEL)),
                  pcol(COL_ZA), pcol(COL_GS), pcol(COL_GA),
                  pl.BlockSpec((rows, D_MODEL), lambda i: (0, 2)),
                  full(wglu.shape), full(bglu.shape), full(wbs.shape), full(wba.shape), full(wout.shape),
                  full(lng.shape), full(lnb.shape)],
        out_specs=full((rows, D_MODEL)),
        out_shape=jax.ShapeDtypeStruct((rows, D_MODEL), F32),
        compiler_params=pltpu.CompilerParams(vmem_limit_bytes=VMEM_LIMIT),
        name="final_sample",
    )(x2d, yssm, proj_s, o_a, proj_s, proj_s, proj_s, mods, wglu, bglu, wbs, wba, wout, lng, lnb)


def _bucket_table():
    max_exact = N_BUCKETS // 2
    dist = np.arange(WINDOW + 1)
    df = np.maximum(dist, 1).astype(np.float32)
    large = max_exact + (np.log(df / np.float32(max_exact)) / np.float32(math.log(WINDOW / max_exact))
                         * np.float32(N_BUCKETS - max_exact)).astype(np.int32)
    large = np.minimum(large, N_BUCKETS - 1)
    return np.where(dist < max_exact, dist, large)


def _dot3_nt(a, b):
    ah = a.astype(BF16)
    al = (a - ah.astype(F32)).astype(BF16)
    bh = b.astype(BF16)
    bl = (b - bh.astype(F32)).astype(BF16)
    return _dot_nt(ah, bh) + _dot_nt(ah, bl) + _dot_nt(al, bh)


def _ssm_prep_body(p1_ref, p2_ref, kf_ref, bt_ref, bts_ref, c2_ref, c2s_ref, d_ref,
                   lhs_ref, bpow_ref, cpow_ref, bre_ref, bim_ref, cblk_ref):
    lane = lax.broadcasted_iota(jnp.int32, (SSM_GROUP, SW), 1)
    left = lane < SSM_STATE
    sgn = jnp.where(left, 1.0, -1.0)
    lane_blk = lax.broadcasted_iota(jnp.int32, (CW, CW), 1) // SSM_GROUP
    diag = lax.broadcasted_iota(jnp.int32, (CW, CW), 0) == lax.broadcasted_iota(jnp.int32, (CW, CW), 1)
    bre_ref[...] = jnp.zeros_like(bre_ref)
    bim_ref[...] = jnp.zeros_like(bim_ref)
    cblk_ref[...] = jnp.zeros_like(cblk_ref)
    for gl in range(GT):
        p1, p2 = p1_ref[gl], p2_ref[gl]
        k1, k2 = kf_ref[gl, 0:1, :], kf_ref[gl, 1:2, :]
        bt, bts = bt_ref[gl], bts_ref[gl]
        c2, c2s = c2_ref[gl], c2s_ref[gl]
        bb = k1 * bt + k2 * bts
        bbs = k1 * bts - k2 * bt
        ca = [sgn * (c2 * p1[t:t + 1, :] + c2s * p2[t:t + 1, :]) for t in range(CHUNK + 1)]
        cpow_ref[gl] = jnp.concatenate(ca[1:], axis=0).astype(BF16)
        bpow_ref[gl] = jnp.concatenate(
            [p1[CHUNK - 1 - s:CHUNK - s, :] * bb + p2[CHUNK - 1 - s:CHUNK - s, :] * bbs for s in range(CHUNK)],
            axis=0).astype(BF16)
        e = _dot3_nt(jnp.concatenate(ca[:CHUNK], axis=0), jnp.concatenate([bb] * CHUNK, axis=0))
        acc = jnp.where(diag, d_ref[gl], 0.0)
        for s in range(CHUNK):
            sh = e if s == 0 else jnp.concatenate([jnp.zeros((s * SSM_GROUP, CW), F32), e[:CW - s * SSM_GROUP, :]], axis=0)
            acc = acc + jnp.where(lane_blk == s, sh, 0.0)
        lhs_ref[gl] = acc.astype(BF16)
        rs = slice(gl * SSM_GROUP, (gl + 1) * SSM_GROUP)
        ts = slice((gl // 2) * SW, (gl // 2 + 1) * SW)
        ti = slice(GT * SSM_STATE + (gl // 2) * SW, GT * SSM_STATE + (gl // 2 + 1) * SW)
        if gl % 2 == 0:
            bre_ref[rs, ts] = jnp.where(left, bb, 0.0).astype(BF16)
            bim_ref[rs, ts] = jnp.where(left, bbs, 0.0).astype(BF16)
            cblk_ref[rs, ts] = jnp.where(left, c2, 0.0).astype(BF16)
            cblk_ref[rs, ti] = jnp.where(left, -c2s, 0.0).astype(BF16)
        else:
            bre_ref[rs, ts] = jnp.where(left, 0.0, bbs).astype(BF16)
            bim_ref[rs, ts] = jnp.where(left, 0.0, bb).astype(BF16)
            cblk_ref[rs, ts] = jnp.where(left, 0.0, c2s).astype(BF16)
            cblk_ref[rs, ti] = jnp.where(left, 0.0, -c2).astype(BF16)


def _ssm_params(lam_re, lam_im, log_delta, b_re, b_im, c_re, c_im, d_skip):
    g = SSM_GROUPS
    dup = lambda a: jnp.concatenate([a, a], axis=-1)
    lr, li = dup(lam_re.astype(F32)), dup(lam_im.astype(F32))
    dt = jnp.exp(log_delta.astype(F32))[:, None]
    half_sign = jnp.asarray(np.where(np.arange(SW) < SSM_STATE, -1.0, 1.0), F32)
    tau = jnp.asarray(POW_TAUS, F32)[None, :, None]
    mag = jnp.exp((lr * dt)[:, None, :] * tau)
    ang = (li * dt)[:, None, :] * tau
    p1 = mag * jnp.cos(ang)
    pim = mag * jnp.sin(ang)
    p2 = pim * half_sign
    ar, ai = p1[:, 1, :], pim[:, 1, :]
    den = lr * lr + li * li
    nr = ar - 1.0
    k1 = (nr * lr + ai * li) / den
    k2 = (ai * lr - nr * li) / den * half_sign
    kf = jnp.concatenate([k1[:, None, :], k2[:, None, :], jnp.zeros((g, 6, SW), F32)], axis=1)
    brt, bit = jnp.swapaxes(b_re, 1, 2), jnp.swapaxes(b_im, 1, 2)
    bt = jnp.concatenate([brt, bit], axis=-1)
    bts = jnp.concatenate([bit, brt], axis=-1)
    c2 = jnp.concatenate([c_re, c_im], axis=-1)
    c2s = jnp.concatenate([c_im, c_re], axis=-1)
    dtile = jnp.tile(d_skip.reshape(g, 1, SSM_GROUP), (1, 1, CHUNK))

    gspec = lambda r, w: pl.BlockSpec((GT, r, w), lambda a: (a, 0, 0))
    tile_spec = lambda w: pl.BlockSpec((None, GT * SSM_GROUP, w), lambda a: (a, 0, 0))
    nrow = len(POW_TAUS)
    lhs, bpow, cpow, bre, bim, cblk = pl.pallas_call(
        _ssm_prep_body,
        grid=(g // GT,),
        in_specs=[gspec(nrow, SW), gspec(nrow, SW), gspec(8, SW), gspec(SSM_GROUP, SW), gspec(SSM_GROUP, SW),
                  gspec(SSM_GROUP, SW), gspec(SSM_GROUP, SW), gspec(1, CW)],
        out_specs=[gspec(CW, CW), gspec(CW, SW), gspec(CW, SW),
                   tile_spec(GT * SSM_STATE), tile_spec(GT * SSM_STATE), tile_spec(2 * GT * SSM_STATE)],
        out_shape=[jax.ShapeDtypeStruct((g, CW, CW), BF16), jax.ShapeDtypeStruct((g, CW, SW), BF16),
                   jax.ShapeDtypeStruct((g, CW, SW), BF16),
                   jax.ShapeDtypeStruct((GT, GT * SSM_GROUP, GT * SSM_STATE), BF16),
                   jax.ShapeDtypeStruct((GT, GT * SSM_GROUP, GT * SSM_STATE), BF16),
                   jax.ShapeDtypeStruct((GT, GT * SSM_GROUP, 2 * GT * SSM_STATE), BF16)],
        name="ssm_prep",
    )(p1, p2, kf, bt, bts, c2, c2s, dtile)
    ar_row = ar[:, :SSM_STATE].reshape(1, -1)
    ai_row = ai[:, :SSM_STATE].reshape(1, -1)
    return lhs, bpow, cpow, p1, p2, ar_row, ai_row, bre, bim, cblk


def _attn_params(rel_bias, sinks):
    hp = lax.Precision.HIGHEST
    rb = rel_bias.astype(F32)
    onehot = jnp.asarray(_bucket_table()[:, None] == np.arange(N_BUCKETS)[None, :], F32)
    tbl = jnp.dot(onehot, rb, precision=hp)
    sink_rows = jnp.repeat(sinks.astype(F32).reshape(N_KV_HEADS, 1, KV_GROUP), WINDOW, axis=2)
    ds = np.clip(WINDOW - np.arange(KROWS), 0, WINDOW)
    bs = jnp.dot(jnp.asarray(ds[:, None] == np.arange(WINDOW + 1)[None, :], F32), tbl, precision=hp)
    bs = jnp.transpose(bs.reshape(KROWS, N_KV_HEADS, KV_GROUP), (0, 2, 1)).reshape(KROWS, N_HEADS)
    bias_s = jnp.pad(bs, ((0, 0), (0, 128 - N_HEADS)))
    sk = jnp.transpose(sinks.astype(F32).reshape(N_KV_HEADS, KV_GROUP), (1, 0)).reshape(1, N_HEADS)
    sink_row = jnp.pad(sk, ((0, 0), (0, 128 - N_HEADS)))
    seg_np = np.zeros((D_MODEL, 128), np.float32)
    for j in range(KV_GROUP):
        for kh in range(N_KV_HEADS):
            r0 = j * KV_WIDTH + kh * HEAD_DIM
            seg_np[r0:r0 + HEAD_DIM, j * N_KV_HEADS + kh] = 1.0
    seg = jnp.asarray(seg_np, BF16)
    segt = jnp.asarray(seg_np.T, BF16)
    return sink_rows, bias_s, sink_row, seg, segt


def kernel(x_prompt, x_sample, c_prompt, c_sample, state_ssm_re, state_ssm_im, cache_swa_k, cache_swa_v,
           w_ada, b_ada, w_in, ssm_lambda_re, ssm_lambda_im, ssm_log_delta, ssm_b_re, ssm_b_im,
           ssm_c_re, ssm_c_im, ssm_d, w_glu, b_glu, attn_sinks, rel_bias, w_branch_s, w_branch_a,
           w_out, ln_g, ln_b):
    assert w_ada.shape[0] == 1, "single-layer trunk"
    n_batch, seq, _ = x_prompt.shape
    n_dec = x_sample.shape[0]

    wi = w_in[0]
    w_in_b = jnp.concatenate([wi[:, 0:1024], wi[:, 1024:2048],
                              _regroup_heads(wi[:, 2048:3072], 1) * (HEAD_DIM ** -0.5),
                              _regroup_heads(wi[:, 3584:4608], 1), wi[:, 4608:5632], wi[:, 5632:6656],
                              wi[:, 3072:3328], wi[:, 3328:3584]], axis=1).astype(BF16)
    wvt = jnp.transpose(wi[:, 3328:3584]).astype(BF16)
    wut = jnp.transpose(wi[:, 0:D_MODEL]).astype(BF16)
    eye = jnp.eye(WINDOW, dtype=BF16)
    wglu = w_glu[0].astype(BF16)
    bglu = b_glu[0].reshape(1, -1).astype(F32)
    wbs = w_branch_s[0].astype(BF16)
    wba = _regroup_heads(w_branch_a[0], 0).astype(BF16)
    wout = w_out[0].astype(BF16)
    lng = ln_g[0].reshape(1, -1).astype(F32)
    lnb = ln_b[0].reshape(1, -1).astype(F32)
    lhs, bpow, cpow, p1, p2, ar_row, ai_row, bre, bim, cblk = _ssm_params(
        ssm_lambda_re[0], ssm_lambda_im[0], ssm_log_delta[0], ssm_b_re[0], ssm_b_im[0],
        ssm_c_re[0], ssm_c_im[0], ssm_d[0])
    sink_rows, bias_s, sink_row, seg, segt = _attn_params(rel_bias, attn_sinks[0])

    c_all = jnp.concatenate([c_prompt, jnp.zeros((8 - n_batch, D_MODEL), F32), c_sample], axis=0)
    mod = _ada(c_all, w_ada[0], b_ada[0])
    modp = mod[:n_batch].reshape(n_batch * 3, 1, D_MODEL)
    mods = mod[8:8 + n_dec]

    xp = x_prompt.reshape(n_batch * seq, D_MODEL)
    yssm, hfin = _ssm_prompt(x_prompt, modp, wut, lhs, bpow, cpow, p1, p2)
    yp, ktail, vtail = _main_prompt(rel_bias.astype(F32).reshape(-1), xp, yssm, modp, w_in_b, wvt, sink_rows, eye,
                                    wglu, bglu, wbs, wba, wout, lng, lnb, n_batch, seq)
    yp = yp.reshape(n_batch, seq, D_MODEL)
    p_hr = hfin[:, :, 0, :SSM_STATE][None]
    p_hi = hfin[:, :, 0, SSM_STATE:][None]
    p_k = ktail.reshape(1, n_batch, WINDOW, N_KV_HEADS, HEAD_DIM)
    p_v = vtail.reshape(1, n_batch, WINDOW, N_KV_HEADS, HEAD_DIM)

    xs = x_sample.reshape(n_dec, D_MODEL)
    proj_s = _proj(xs, mods, mods,
                   pl.BlockSpec((n_dec, D_MODEL), lambda i: (0, 0)),
                   pl.BlockSpec((n_dec, D_MODEL), lambda i: (0, 1)),
                   w_in_b, n_dec, D_IN)
    ns = SSM_GROUPS * SSM_STATE
    ys_s, s_hr, s_hi = _sample_ssm(proj_s, state_ssm_re[0].reshape(n_dec, ns), state_ssm_im[0].reshape(n_dec, ns),
                                   ar_row, ai_row, bre, bim, cblk, ssm_d[0].reshape(1, -1).astype(F32))
    o_s, nk, nv = _sample_attn(proj_s, cache_swa_k[0].reshape(n_dec, WINDOW, KV_WIDTH),
                               cache_swa_v[0].reshape(n_dec, WINDOW, KV_WIDTH), bias_s, sink_row, seg, segt)
    ysmp = _final_sample(xs, ys_s, o_s, proj_s, mods, wglu, bglu, wbs, wba, wout, lng, lnb)

    return (yp, ysmp.reshape(n_dec, 1, D_MODEL),
            p_hr, p_hi, p_k, p_v,
            s_hr.reshape(1, n_dec, SSM_GROUPS, SSM_STATE), s_hi.reshape(1, n_dec, SSM_GROUPS, SSM_STATE),
            nk.reshape(1, n_dec, WINDOW, N_KV_HEADS, HEAD_DIM), nv.reshape(1, n_dec, WINDOW, N_KV_HEADS, HEAD_DIM))
```

```python
import functools
import math

import numpy as np
import jax
import jax.numpy as jnp
from jax import lax
from jax.experimental import pallas as pl
from jax.experimental.pallas import tpu as pltpu

F32 = jnp.float32
BF16 = jnp.bfloat16

D_MODEL = 1024
SSM_GROUPS = 64
SSM_GROUP = 16
SSM_STATE = 64
N_HEADS = 16
HEAD_DIM = 64
N_KV_HEADS = 4
KV_GROUP = 4
KV_WIDTH = N_KV_HEADS * HEAD_DIM
WINDOW = 128
N_BUCKETS = 32
NEG_INF = -1e30
LN_EPS = 1e-5
DEPTH = 1
DEEPNORM_ALPHA = (2 * DEPTH) ** 0.25
D_IN = 6656
CHUNK = 16
CW = CHUNK * SSM_GROUP
SW = 2 * SSM_STATE

COL_U, COL_ZS, COL_Q, COL_ZA, COL_GS, COL_GA = 0, 1, 2, 3, 4, 5
COL_K, COL_V = 24, 25

VMEM_LIMIT = 56 * 1024 * 1024
VMEM_LIMIT_SSM = 60 * 1024 * 1024


def _regroup_heads(w, axis):
    shp = w.shape
    if axis == 1:
        return jnp.transpose(w.reshape(shp[0], N_KV_HEADS, KV_GROUP, HEAD_DIM), (0, 2, 1, 3)).reshape(shp)
    return jnp.transpose(w.reshape(N_KV_HEADS, KV_GROUP, HEAD_DIM, shp[1]), (1, 0, 2, 3)).reshape(shp)


def _sigmoid(x):
    return 0.5 * jnp.tanh(0.5 * x) + 0.5


def _silu(x):
    return x * _sigmoid(x)


def _gelu_tanh(x):
    c = math.sqrt(2.0 / math.pi)
    hx = 0.5 * x
    return hx + hx * jnp.tanh(x * (c + (c * 0.044715) * (x * x)))


def _dot(a, b):
    return jnp.dot(a, b, preferred_element_type=F32)


def _dot_nt(a, b):
    return lax.dot_general(a, b, (((1,), (1,)), ((), ())), preferred_element_type=F32)


def _ada_body(c_ref, w_ref, b_ref, o_ref):
    c = c_ref[...]
    sc = _silu(c).astype(BF16)
    o_ref[...] = _dot(sc, w_ref[...].astype(BF16)) + b_ref[...]


def _ada(c_all, w_ada, b_ada):
    rows = c_all.shape[0]
    bn = 512
    return pl.pallas_call(
        _ada_body,
        grid=(3 * D_MODEL // bn,),
        in_specs=[pl.BlockSpec((rows, D_MODEL), lambda j: (0, 0)),
                  pl.BlockSpec((D_MODEL, bn), lambda j: (0, j)),
                  pl.BlockSpec((1, bn), lambda j: (0, j))],
        out_specs=pl.BlockSpec((rows, bn), lambda j: (0, j)),
        out_shape=jax.ShapeDtypeStruct((rows, 3 * D_MODEL), F32),
        name="ada",
    )(c_all, w_ada, b_ada.reshape(1, -1))


PROJ_CB = 512


def _proj_body(x_ref, shift_ref, scale_ref, w_ref, o_ref, *, n_out):
    h = x_ref[...] * (1.0 + scale_ref[...]) + shift_ref[...]
    hb = h.astype(BF16)
    for j in range(n_out // PROJ_CB):
        sl = slice(j * PROJ_CB, (j + 1) * PROJ_CB)
        o_ref[:, sl] = _dot(hb, w_ref[:, sl]).astype(BF16)


def _proj(x2d, shift_arr, scale_arr, shift_spec, scale_spec, w_in_b, tm, n_out):
    rows = x2d.shape[0]
    return pl.pallas_call(
        functools.partial(_proj_body, n_out=n_out),
        grid=(rows // tm,),
        in_specs=[pl.BlockSpec((tm, D_MODEL), lambda i: (i, 0)),
                  shift_spec, scale_spec,
                  pl.BlockSpec((D_MODEL, n_out), lambda i: (0, 0), pipeline_mode=pl.Buffered(1))],
        out_specs=pl.BlockSpec((tm, n_out), lambda i: (i, 0)),
        out_shape=jax.ShapeDtypeStruct((rows, n_out), BF16),
        compiler_params=pltpu.CompilerParams(vmem_limit_bytes=VMEM_LIMIT),
        name="proj",
    )(x2d, shift_arr, scale_arr, w_in_b)


NKS = 128
GC = 16


def _cmul(c1, c2, x):
    return c1 * x + c2 * pltpu.roll(x, SSM_STATE, x.ndim - 1)


SSM_TOK = NKS * CHUNK
SUB = 8
NPH = CHUNK // SUB


def _dot_tn(a, b):
    return lax.dot_general(a, b, (((0,), (0,)), ((), ())), preferred_element_type=F32)


SEG = 8
SEG_LEN = NKS // SEG
POW_TAUS = (tuple(range(CHUNK + 1)) + tuple(CHUNK * m for m in range(2, SEG_LEN))
            + tuple(CHUNK * SEG_LEN * m for m in range(1, SEG + 1)) + (0,))
ROW_CHUNK1 = CHUNK
ROW_SEG1 = CHUNK + SEG_LEN - 1


def _ssm_body(x_ref, shift_ref, scale_ref, wut_ref, lhs_ref, bpow_ref, cpow_ref, p1_ref, p2_ref,
              y_ref, hfin_ref, r_scr, yt_scr, s_scr, carry_scr):
    i = pl.program_id(1)
    j = pl.program_id(2)

    @pl.when((i == 0) & (j == 0))
    def _():
        carry_scr[...] = jnp.zeros_like(carry_scr)

    @pl.when(j < NPH)
    def _():
        pairs = range(SUB // 2)
        hbs = []
        for sp in pairs:
            h = jnp.concatenate([x_ref[:, b, s, :] for s in (2 * sp, 2 * sp + 1) for b in range(SEG_LEN)],
                                axis=0)
            hbs.append((h * (1.0 + scale_ref[...]) + shift_ref[...]).astype(BF16))
        uts = [_dot_nt(wut_ref[...], hb).astype(BF16) for hb in hbs]
        for sp, ut in zip(pairs, uts):
            ut = ut.reshape(SSM_GROUPS, SSM_GROUP, 2 * NKS)
            pair = jnp.concatenate([ut[:, :, :NKS], ut[:, :, NKS:]], axis=1)
            r0 = pl.multiple_of(j * SUB * SSM_GROUP + sp * 2 * SSM_GROUP, 2 * SSM_GROUP)
            r_scr[:, pl.ds(r0, 2 * SSM_GROUP), :] = pair

    @pl.when(j == NPH - 1)
    def _():
        def local(g):
            rg = r_scr[g]
            yt_scr[g] = _dot(lhs_ref[g], rg)
            s_scr[g] = _dot_tn(rg, bpow_ref[g])

        def carried(g):
            yt_scr[g] += _dot_nt(cpow_ref[g], s_scr[g].astype(BF16))

        row = lax.broadcasted_iota(jnp.int32, (GC, SEG, SW), 1)
        seg_rows = lambda b: slice(b * SEG, (b + 1) * SEG)

        def scan(gc):
            gs = slice(gc * GC, (gc + 1) * GC)
            p1, p2 = p1_ref[gs], p2_ref[gs]
            c1, c2 = p1[:, ROW_CHUNK1:ROW_CHUNK1 + 1, :], p2[:, ROW_CHUNK1:ROW_CHUNK1 + 1, :]
            h = jnp.zeros((GC, SEG, SW), F32)
            hs = h
            for b in range(SEG_LEN):
                sb = s_scr[gs, seg_rows(b), :]
                h, hs = sb + c1 * h + c2 * hs, pltpu.roll(sb, SSM_STATE, 2) + c1 * hs - c2 * h
                s_scr[gs, seg_rows(b), :] = h
                yield
            e = h
            for d in (1, 2, 4):
                rd = ROW_SEG1 + d - 1
                sh = jnp.where(row >= d, pltpu.roll(e, d, 1), 0.0)
                e = e + _cmul(p1[:, rd:rd + 1, :], p2[:, rd:rd + 1, :], sh)
            carry = carry_scr[gs]
            e = e + _cmul(p1[:, ROW_SEG1:ROW_SEG1 + SEG, :], p2[:, ROW_SEG1:ROW_SEG1 + SEG, :], carry)
            ein = jnp.where(row == 0, carry, pltpu.roll(e, 1, 1))
            eins = pltpu.roll(ein, SSM_STATE, 2)
            carry_scr[gs] = jnp.broadcast_to(e[:, SEG - 1:SEG, :], (GC, SEG, SW))
            yield
            for b in range(SEG_LEN - 1, 0, -1):
                rb = ROW_CHUNK1 + b - 1
                s_scr[gs, seg_rows(b), :] = (s_scr[gs, seg_rows(b - 1), :]
                                             + p1[:, rb:rb + 1, :] * ein + p2[:, rb:rb + 1, :] * eins)
                yield
            s_scr[gs, seg_rows(0), :] = ein

        def interleave(fn, groups, gen):
            for g in groups:
                fn(g)
                for _ in range(2):
                    next(gen, None)
            for _ in gen:
                pass

        n_sets = SSM_GROUPS // GC
        sets = [range(q * GC, (q + 1) * GC) for q in range(n_sets)]
        for g in sets[0]:
            local(g)
        for q in range(1, n_sets):
            interleave(local, sets[q], scan(q - 1))
        interleave(carried, sets[0], scan(n_sets - 1))
        for q in range(1, n_sets):
            for g in sets[q]:
                carried(g)
        hfin_ref[...] = carry_scr[...]

    @pl.when(j >= NPH)
    def _():
        for sp in range(SUB):
            t0 = pl.multiple_of((j - NPH) * SUB * SSM_GROUP + sp * SSM_GROUP, SSM_GROUP)
            yt = yt_scr[:, pl.ds(t0, SSM_GROUP), :].reshape(D_MODEL, NKS)
            yr = yt.T
            for b in range(SEG_LEN):
                y_ref[:, b, sp, :] = yr[b * SEG:(b + 1) * SEG, :]


def _ssm_prompt(x_prompt, modp, wut, lhs, bpow, cpow, p1, p2):
    n_batch, seq, _ = x_prompt.shape
    steps = seq // SSM_TOK
    xv = x_prompt.reshape(n_batch * steps * SEG, SEG_LEN, CHUNK, D_MODEL)
    mod_spec = lambda k: pl.BlockSpec((None, 1, D_MODEL), lambda b, i, j, k=k: (b * 3 + k, 0, 0))
    consts = (wut, lhs, bpow, cpow, p1, p2)
    y, hfin = pl.pallas_call(
        _ssm_body,
        grid=(n_batch, steps, 2 * NPH),
        in_specs=[pl.BlockSpec((SEG, SEG_LEN, SUB, D_MODEL),
                               lambda b, i, j: (b * steps + i, 0, jnp.minimum(j, NPH - 1), 0)),
                  mod_spec(0), mod_spec(1)] + [_const_spec(c.shape) for c in consts],
        out_specs=[pl.BlockSpec((SEG, SEG_LEN, SUB, D_MODEL),
                                lambda b, i, j: (b * steps + i, 0, jnp.maximum(j - NPH, 0), 0)),
                   pl.BlockSpec((None, SSM_GROUPS, 8, SW), lambda b, i, j: (b, 0, 0, 0))],
        out_shape=[jax.ShapeDtypeStruct(xv.shape, F32),
                   jax.ShapeDtypeStruct((n_batch, SSM_GROUPS, 8, SW), F32)],
        scratch_shapes=[pltpu.VMEM((SSM_GROUPS, CW, NKS), BF16), pltpu.VMEM((SSM_GROUPS, CW, NKS), F32),
                        pltpu.VMEM((SSM_GROUPS, NKS, SW), F32), pltpu.VMEM((SSM_GROUPS, 8, SW), F32)],
        compiler_params=pltpu.CompilerParams(dimension_semantics=("arbitrary", "arbitrary", "arbitrary"),
                                             vmem_limit_bytes=VMEM_LIMIT_SSM),
        name="ssm_prompt",
    )(xv, modp, modp, *consts)
    return y.reshape(n_batch * seq, D_MODEL), hfin


def _post(chunks, gate, wglu_ref, bglu_ref, wbs_ref, wba_ref, wout_ref, lng_ref, lnb_ref):
    data = [load() for load in chunks]
    ys = [_gelu_tanh(d[0]) for d in data]
    glu = [y * _sigmoid(_dot(y.astype(BF16), wglu_ref[...]) + bglu_ref[...]) for y in ys]
    b_s = [_dot(g.astype(BF16) * _silu(d[1]), wbs_ref[...]) for g, d in zip(glu, data)]
    b_a = [_dot(d[2].astype(BF16) * _silu(d[3]), wba_ref[...]) for d in data]
    m = [_sigmoid(d[4]).astype(F32) * s + _sigmoid(d[5]).astype(F32) * a for d, s, a in zip(data, b_s, b_a)]
    out = [_dot(v.astype(BF16), wout_ref[...]) for v in m]
    res = []
    for d, o in zip(data, out):
        r = DEEPNORM_ALPHA * d[6] + gate * o
        mu = jnp.mean(r, axis=-1, keepdims=True)
        rc = r - mu
        var = jnp.mean(rc * rc, axis=-1, keepdims=True)
        res.append(rc * lax.rsqrt(var + LN_EPS) * lng_ref[...] + lnb_ref[...])
    return res


TB = 512
NQB = TB // WINDOW
W_OFF = D_MODEL
W_MAIN = 5 * D_MODEL + KV_WIDTH
ONES_ROWS = 16
POST_ROWS = 256


def _const_spec(shape):
    nd = len(shape)
    return pl.BlockSpec(shape, lambda *_: (0,) * nd, pipeline_mode=pl.Buffered(1))


def _bucket_thresholds():
    tbl = _bucket_table()
    return [int(np.argmax(tbl >= k)) for k in range(1, N_BUCKETS)]


def _fill_bias_table(rb_ref, bias_s):
    si = lax.broadcasted_iota(jnp.int32, (2 * WINDOW, WINDOW), 0)
    qi = lax.broadcasted_iota(jnp.int32, (2 * WINDOW, WINDOW), 1)
    dist = qi + WINDOW - si
    bid = jnp.zeros((2 * WINDOW, WINDOW), jnp.int32)
    for thr in _bucket_thresholds():
        bid = bid + (dist >= thr).astype(jnp.int32)
    valid = (dist >= 0) & (dist <= WINDOW)
    own = si >= WINDOW

    def per_head(h, c):
        t = jnp.zeros((2 * WINDOW, WINDOW), F32)
        for b in range(N_BUCKETS):
            t = jnp.where(bid == b, rb_ref[b * N_HEADS + h], t)
        t = jnp.where(valid, t, NEG_INF)
        kh, j = h // KV_GROUP, h % KV_GROUP
        bias_s[0, kh, j] = t
        bias_s[1, kh, j] = jnp.where(own, t, NEG_INF)
        return c

    lax.fori_loop(0, N_HEADS, per_head, 0)


def _main_body(rb_ref, x_ref, yssm_ref, shift_ref, scale_ref, gate_ref, wmain_ref, wvt_ref, sink_ref,
               wglu_ref, bglu_ref, wbs_ref, wba_ref, wout_ref, lng_ref, lnb_ref,
               y_ref, ktail_ref, vtail_ref,
               zs_s, q_s, za_s, gs_s, ga_s, k_s, vt_s, ot_s, o_s, bias_s):
    i = pl.program_id(1)

    @pl.when((pl.program_id(0) == 0) & (i == 0))
    def _():
        _fill_bias_table(rb_ref, bias_s)

    @pl.when(i == 0)
    def _():
        k_s[0:WINDOW, :] = jnp.zeros((WINDOW, KV_WIDTH), BF16)
        vt_s[:, 0:WINDOW] = jnp.zeros((KV_WIDTH, WINDOW), BF16)

    x = x_ref[...]
    hb = (x * (1.0 + scale_ref[...]) + shift_ref[...]).astype(BF16)

    def project(dst, c0, c):
        lo = W_OFF + c0 + c * 512
        dst[:, c * 512:(c + 1) * 512] = _dot(hb, wmain_ref[:, lo:lo + 512]).astype(BF16)

    for c in range(2):
        project(q_s, D_MODEL, c)
    deferred = [(dst, c0, c) for dst, c0 in ((zs_s, 0), (za_s, 2 * D_MODEL), (gs_s, 3 * D_MODEL), (ga_s, 4 * D_MODEL))
                for c in range(2)]

    k_s[WINDOW:WINDOW + TB, :] = _dot(hb, wmain_ref[:, W_OFF + 5 * D_MODEL:W_OFF + W_MAIN]).astype(BF16)
    vt_s[:, WINDOW:WINDOW + TB] = _dot_nt(wvt_ref[...], hb).astype(BF16)

    first = jnp.where(i == 0, 1, 0)
    lane = lax.broadcasted_iota(jnp.int32, (2 * WINDOW, 128), 1)
    ones = jnp.ones((ONES_ROWS, 2 * WINDOW), BF16)
    for blk in range(NQB):
        rows = slice(blk * WINDOW, (blk + 1) * WINDOW)
        kcat = k_s[blk * WINDOW:blk * WINDOW + 2 * WINDOW, :]
        vtc = vt_s[:, blk * WINDOW:blk * WINDOW + 2 * WINDOW]
        sel = first if blk == 0 else 0
        heads = range(N_KV_HEADS)
        sts = []
        for kh in heads:
            pair, odd = kh // 2, kh % 2
            kcm = jnp.where((lane >= HEAD_DIM) == bool(odd), kcat[:, pair * 128:(pair + 1) * 128], jnp.zeros((), BF16))
            qp = jnp.concatenate(
                [q_s[rows, j * KV_WIDTH + pair * 128:j * KV_WIDTH + (pair + 1) * 128] for j in range(KV_GROUP)],
                axis=0)
            bias = jnp.concatenate([bias_s[sel, kh, j] for j in range(KV_GROUP)], axis=1)
            sts.append(_dot_nt(kcm, qp) + bias)
        mine = deferred[blk * len(deferred) // NQB:(blk + 1) * len(deferred) // NQB]
        ms = [jnp.maximum(jnp.max(sts[kh], axis=0, keepdims=True), sink_ref[kh]) for kh in heads]
        project(*mine[0])
        es = [jnp.exp(sts[kh] - ms[kh]).astype(BF16) for kh in heads]
        rs = [_dot(jnp.concatenate([vtc[kh * HEAD_DIM:(kh + 1) * HEAD_DIM, :], ones], axis=0), es[kh])
              for kh in heads]
        for spec in mine[1:]:
            project(*spec)
        for kh in heads:
            den = rs[kh][HEAD_DIM:HEAD_DIM + 1, :] + jnp.exp(sink_ref[kh] - ms[kh])
            ot = (rs[kh][:HEAD_DIM, :] / den).astype(BF16)
            for j in range(KV_GROUP):
                ot_s[blk, j, kh * HEAD_DIM:(kh + 1) * HEAD_DIM, :] = ot[:, j * WINDOW:(j + 1) * WINDOW]
        for j in range(KV_GROUP):
            o_s[rows, j * KV_WIDTH:(j + 1) * KV_WIDTH] = ot_s[blk, j].T

    k_s[0:WINDOW, :] = k_s[TB:TB + WINDOW, :]
    vt_s[:, 0:WINDOW] = vt_s[:, TB:TB + WINDOW]

    def loader(c):
        rs = slice(c * POST_ROWS, (c + 1) * POST_ROWS)
        return lambda: (yssm_ref[rs, :], zs_s[rs, :], o_s[rs, :], za_s[rs, :], gs_s[rs, :], ga_s[rs, :], x_ref[rs, :])

    ys = _post([loader(c) for c in range(TB // POST_ROWS)], gate_ref[...],
               wglu_ref, bglu_ref, wbs_ref, wba_ref, wout_ref, lng_ref, lnb_ref)
    for c, yc in enumerate(ys):
        y_ref[c * POST_ROWS:(c + 1) * POST_ROWS, :] = yc

    @pl.when(i == pl.num_programs(1) - 1)
    def _():
        ktail_ref[...] = k_s[0:WINDOW, :].astype(F32)
        vtail_ref[...] = vt_s[:, 0:WINDOW].T.astype(F32)


def _main_prompt(rel_bias_flat, x2d, yssm, modp, wmain, wvt, sink_rows, wglu, bglu, wbs, wba, wout, lng, lnb,
                 n_batch, seq):
    steps = seq // TB
    row_spec = pl.BlockSpec((TB, D_MODEL), lambda b, i: (b * steps + i, 0))
    mod_spec = lambda j: pl.BlockSpec((None, 1, D_MODEL), lambda b, i, j=j: (b * 3 + j, 0, 0))
    tail_spec = pl.BlockSpec((None, WINDOW, KV_WIDTH), lambda b, i: (b, 0, 0))
    consts = (wmain, wvt, sink_rows, wglu, bglu, wbs, wba, wout, lng, lnb)
    return pl.pallas_call(
        _main_body,
        grid=(n_batch, steps),
        in_specs=[pl.BlockSpec(memory_space=pltpu.SMEM), row_spec, row_spec, mod_spec(0), mod_spec(1), mod_spec(2)]
        + [_const_spec(c.shape) for c in consts],
        out_specs=[row_spec, tail_spec, tail_spec],
        out_shape=[jax.ShapeDtypeStruct((n_batch * seq, D_MODEL), F32),
                   jax.ShapeDtypeStruct((n_batch, WINDOW, KV_WIDTH), F32),
                   jax.ShapeDtypeStruct((n_batch, WINDOW, KV_WIDTH), F32)],
        scratch_shapes=[pltpu.VMEM((TB, D_MODEL), BF16)] * 5
        + [pltpu.VMEM((TB + WINDOW, KV_WIDTH), BF16), pltpu.VMEM((KV_WIDTH, TB + WINDOW), BF16),
           pltpu.VMEM((NQB, KV_GROUP, KV_WIDTH, WINDOW), BF16), pltpu.VMEM((TB, D_MODEL), BF16),
           pltpu.VMEM((2, N_KV_HEADS, KV_GROUP, 2 * WINDOW, WINDOW), F32)],
        compiler_params=pltpu.CompilerParams(dimension_semantics=("arbitrary", "arbitrary"),
                                             vmem_limit_bytes=VMEM_LIMIT),
        name="main_prompt",
    )(rel_bias_flat, x2d, yssm, modp, modp, modp, *consts)


GT = 8


def _sample_ssm_body(u_ref, h0r_ref, h0i_ref, ar_ref, ai_ref, bre_ref, bim_ref, cblk_ref, d_ref,
                     y_ref, hr_ref, hi_ref):
    ub = u_ref[...]
    ys = []
    for a in range(GT):
        ua = ub[:, a * 128:(a + 1) * 128]
        st = slice(a * 512, (a + 1) * 512)
        xr = _dot(ua, bre_ref[a])
        xi = _dot(ua, bim_ref[a])
        h0r = h0r_ref[:, st]
        h0i = h0i_ref[:, st]
        ar = ar_ref[:, st]
        ai = ai_ref[:, st]
        hr = ar * h0r - ai * h0i + xr
        hi = ar * h0i + ai * h0r + xi
        hr_ref[:, st] = hr
        hi_ref[:, st] = hi
        hcat = jnp.concatenate([hr, hi], axis=1).astype(BF16)
        ys.append(_dot_nt(hcat, cblk_ref[a]))
    y_ref[...] = jnp.concatenate(ys, axis=1) + d_ref[...] * ub.astype(F32)


def _sample_ssm(proj_s, h0r, h0i, ar, ai, bre, bim, cblk, d_row):
    rows = proj_s.shape[0]
    ns = SSM_GROUPS * SSM_STATE
    full = lambda shape: pl.BlockSpec(shape, lambda i: (0,) * len(shape))
    return pl.pallas_call(
        _sample_ssm_body,
        grid=(1,),
        in_specs=[pl.BlockSpec((rows, D_MODEL), lambda i: (0, COL_U)),
                  full((rows, ns)), full((rows, ns)), full((1, ns)), full((1, ns)),
                  full(bre.shape), full(bim.shape), full(cblk.shape), full((1, D_MODEL))],
        out_specs=[full((rows, D_MODEL)), full((rows, ns)), full((rows, ns))],
        out_shape=[jax.ShapeDtypeStruct((rows, D_MODEL), F32),
                   jax.ShapeDtypeStruct((rows, ns), F32),
                   jax.ShapeDtypeStruct((rows, ns), F32)],
        compiler_params=pltpu.CompilerParams(vmem_limit_bytes=VMEM_LIMIT),
        name="sample_ssm",
    )(proj_s, h0r, h0i, ar, ai, bre, bim, cblk, d_row)


SB = 16
KROWS = WINDOW + 8
SEQ_GROUP = 4


def _sample_attn_body(q_ref, kn_ref, vn_ref, ck_ref, cv_ref, bias_ref, sink_ref, seg_ref, segt_ref,
                      o_ref, nk_ref, nv_ref):
    qf = q_ref[...].astype(F32)
    knf = kn_ref[...].astype(F32)
    vnf = vn_ref[...].astype(F32)
    rowk = lax.broadcasted_iota(jnp.int32, (KROWS, 128), 0)
    roww = lax.broadcasted_iota(jnp.int32, (WINDOW, KV_WIDTH), 0)
    sink = sink_ref[...]
    last = roww == WINDOW - 1
    for g0 in range(0, SB, SEQ_GROUP):
        seqs = range(g0, g0 + SEQ_GROUP)
        scores = []
        for b in seqs:
            kall = jnp.concatenate([ck_ref[b], jnp.broadcast_to(knf[b:b + 1], (8, KV_WIDTH))], axis=0)
            qrow = qf[b:b + 1]
            prod = jnp.concatenate([kall * qrow[:, j * KV_WIDTH:(j + 1) * KV_WIDTH] for j in range(KV_GROUP)], axis=1)
            s = _dot(prod.astype(BF16), seg_ref[...]) + bias_ref[...]
            scores.append(jnp.where(rowk <= WINDOW, s, NEG_INF))
        probs = []
        for s in scores:
            m = jnp.maximum(jnp.max(s, axis=0, keepdims=True), sink)
            e = jnp.exp(s - m)
            den = jnp.sum(e, axis=0, keepdims=True) + jnp.exp(sink - m)
            probs.append((e / den).astype(BF16))
        pexps = [_dot(p, segt_ref[...]) for p in probs]
        for b, pexp in zip(seqs, pexps):
            vb = cv_ref[b]
            vall = jnp.concatenate([vb, jnp.broadcast_to(vnf[b:b + 1], (8, KV_WIDTH))], axis=0)
            v4 = jnp.concatenate([vall] * KV_GROUP, axis=1)
            o_ref[b:b + 1, :] = jnp.sum(pexp * v4, axis=0, keepdims=True)
            nv_ref[b] = jnp.where(last, jnp.broadcast_to(vnf[b:b + 1], (WINDOW, KV_WIDTH)), pltpu.roll(vb, WINDOW - 1, 0))
            nk_ref[b] = jnp.where(last, jnp.broadcast_to(knf[b:b + 1], (WINDOW, KV_WIDTH)),
                                  pltpu.roll(ck_ref[b], WINDOW - 1, 0))


def _sample_attn(proj_s, ck, cv, bias_s, sink_row, seg, segt):
    rows = proj_s.shape[0]
    cache_spec = pl.BlockSpec((SB, WINDOW, KV_WIDTH), lambda i: (i, 0, 0))
    return pl.pallas_call(
        _sample_attn_body,
        grid=(rows // SB,),
        in_specs=[pl.BlockSpec((SB, D_MODEL), lambda i: (i, COL_Q)),
                  pl.BlockSpec((SB, KV_WIDTH), lambda i: (i, COL_K)),
                  pl.BlockSpec((SB, KV_WIDTH), lambda i: (i, COL_V)),
                  cache_spec, cache_spec,
                  pl.BlockSpec(bias_s.shape, lambda i: (0, 0)),
                  pl.BlockSpec(sink_row.shape, lambda i: (0, 0)),
                  pl.BlockSpec(seg.shape, lambda i: (0, 0)),
                  pl.BlockSpec(segt.shape, lambda i: (0, 0))],
        out_specs=[pl.BlockSpec((SB, D_MODEL), lambda i: (i, 0)), cache_spec, cache_spec],
        out_shape=[jax.ShapeDtypeStruct((rows, D_MODEL), F32),
                   jax.ShapeDtypeStruct(ck.shape, F32),
                   jax.ShapeDtypeStruct(cv.shape, F32)],
        compiler_params=pltpu.CompilerParams(vmem_limit_bytes=VMEM_LIMIT),
        name="sample_attn",
    )(proj_s, proj_s, proj_s, ck, cv, bias_s, sink_row, seg, segt)


def _sample_final_body(x_ref, yssm_ref, zs_ref, o_ref, za_ref, gs_ref, ga_ref, gate_ref,
                       wglu_ref, bglu_ref, wbs_ref, wba_ref, wout_ref, lng_ref, lnb_ref, y_ref):
    load = lambda: (yssm_ref[...], zs_ref[...], o_ref[...], za_ref[...], gs_ref[...], ga_ref[...], x_ref[...])
    y_ref[...] = _post([load], gate_ref[...], wglu_ref, bglu_ref, wbs_ref, wba_ref, wout_ref, lng_ref, lnb_ref)[0]


def _final_sample(x2d, yssm, o_a, proj_s, mods, wglu, bglu, wbs, wba, wout, lng, lnb):
    rows = x2d.shape[0]
    full = lambda shape: pl.BlockSpec(shape, lambda i: (0,) * len(shape))
    pcol = lambda c: pl.BlockSpec((rows, D_MODEL), lambda i, c=c: (0, c))
    return pl.pallas_call(
        _sample_final_body,
        grid=(1,),
        in_specs=[full((rows, D_MODEL)), full((rows, D_MODEL)), pcol(COL_ZS), full((rows, D_MODEL)),
                  pcol(COL_ZA), pcol(COL_GS), pcol(COL_GA),
                  pl.BlockSpec((rows, D_MODEL), lambda i: (0, 2)),
                  full(wglu.shape), full(bglu.shape), full(wbs.shape), full(wba.shape), full(wout.shape),
                  full(lng.shape), full(lnb.shape)],
        out_specs=full((rows, D_MODEL)),
        out_shape=jax.ShapeDtypeStruct((rows, D_MODEL), F32),
        compiler_params=pltpu.CompilerParams(vmem_limit_bytes=VMEM_LIMIT),
        name="final_sample",
    )(x2d, yssm, proj_s, o_a, proj_s, proj_s, proj_s, mods, wglu, bglu, wbs, wba, wout, lng, lnb)


def _bucket_table():
    max_exact = N_BUCKETS // 2
    dist = np.arange(WINDOW + 1)
    df = np.maximum(dist, 1).astype(np.float32)
    large = max_exact + (np.log(df / np.float32(max_exact)) / np.float32(math.log(WINDOW / max_exact))
                         * np.float32(N_BUCKETS - max_exact)).astype(np.int32)
    large = np.minimum(large, N_BUCKETS - 1)
    return np.where(dist < max_exact, dist, large)


def _dot3_nt(a, b):
    ah = a.astype(BF16)
    al = (a - ah.astype(F32)).astype(BF16)
    bh = b.astype(BF16)
    bl = (b - bh.astype(F32)).astype(BF16)
    return _dot_nt(ah, bh) + _dot_nt(ah, bl) + _dot_nt(al, bh)


def _ssm_prep_body(p1_ref, p2_ref, kf_ref, bt_ref, bts_ref, c2_ref, c2s_ref, d_ref,
                   lhs_ref, bpow_ref, cpow_ref, bre_ref, bim_ref, cblk_ref):
    lane = lax.broadcasted_iota(jnp.int32, (SSM_GROUP, SW), 1)
    left = lane < SSM_STATE
    sgn = jnp.where(left, 1.0, -1.0)
    lane_blk = lax.broadcasted_iota(jnp.int32, (CW, CW), 1) // SSM_GROUP
    diag = lax.broadcasted_iota(jnp.int32, (CW, CW), 0) == lax.broadcasted_iota(jnp.int32, (CW, CW), 1)
    bre_ref[...] = jnp.zeros_like(bre_ref)
    bim_ref[...] = jnp.zeros_like(bim_ref)
    cblk_ref[...] = jnp.zeros_like(cblk_ref)
    for gl in range(GT):
        p1, p2 = p1_ref[gl], p2_ref[gl]
        k1, k2 = kf_ref[gl, 0:1, :], kf_ref[gl, 1:2, :]
        bt, bts = bt_ref[gl], bts_ref[gl]
        c2, c2s = c2_ref[gl], c2s_ref[gl]
        bb = k1 * bt + k2 * bts
        bbs = k1 * bts - k2 * bt
        ca = [sgn * (c2 * p1[t:t + 1, :] + c2s * p2[t:t + 1, :]) for t in range(CHUNK + 1)]
        cpow_ref[gl] = jnp.concatenate(ca[1:], axis=0).astype(BF16)
        bpow_ref[gl] = jnp.concatenate(
            [p1[CHUNK - 1 - s:CHUNK - s, :] * bb + p2[CHUNK - 1 - s:CHUNK - s, :] * bbs for s in range(CHUNK)],
            axis=0).astype(BF16)
        e = _dot3_nt(jnp.concatenate(ca[:CHUNK], axis=0), jnp.concatenate([bb] * CHUNK, axis=0))
        acc = jnp.where(diag, d_ref[gl], 0.0)
        for s in range(CHUNK):
            sh = e if s == 0 else jnp.concatenate([jnp.zeros((s * SSM_GROUP, CW), F32), e[:CW - s * SSM_GROUP, :]], axis=0)
            acc = acc + jnp.where(lane_blk == s, sh, 0.0)
        lhs_ref[gl] = acc.astype(BF16)
        rs = slice(gl * SSM_GROUP, (gl + 1) * SSM_GROUP)
        ts = slice((gl // 2) * SW, (gl // 2 + 1) * SW)
        ti = slice(GT * SSM_STATE + (gl // 2) * SW, GT * SSM_STATE + (gl // 2 + 1) * SW)
        if gl % 2 == 0:
            bre_ref[rs, ts] = jnp.where(left, bb, 0.0).astype(BF16)
            bim_ref[rs, ts] = jnp.where(left, bbs, 0.0).astype(BF16)
            cblk_ref[rs, ts] = jnp.where(left, c2, 0.0).astype(BF16)
            cblk_ref[rs, ti] = jnp.where(left, -c2s, 0.0).astype(BF16)
        else:
            bre_ref[rs, ts] = jnp.where(left, 0.0, bbs).astype(BF16)
            bim_ref[rs, ts] = jnp.where(left, 0.0, bb).astype(BF16)
            cblk_ref[rs, ts] = jnp.where(left, 0.0, c2s).astype(BF16)
            cblk_ref[rs, ti] = jnp.where(left, 0.0, -c2).astype(BF16)


def _ssm_params(lam_re, lam_im, log_delta, b_re, b_im, c_re, c_im, d_skip):
    g = SSM_GROUPS
    dup = lambda a: jnp.concatenate([a, a], axis=-1)
    lr, li = dup(lam_re.astype(F32)), dup(lam_im.astype(F32))
    dt = jnp.exp(log_delta.astype(F32))[:, None]
    half_sign = jnp.asarray(np.where(np.arange(SW) < SSM_STATE, -1.0, 1.0), F32)
    tau = jnp.asarray(POW_TAUS, F32)[None, :, None]
    mag = jnp.exp((lr * dt)[:, None, :] * tau)
    ang = (li * dt)[:, None, :] * tau
    p1 = mag * jnp.cos(ang)
    pim = mag * jnp.sin(ang)
    p2 = pim * half_sign
    ar, ai = p1[:, 1, :], pim[:, 1, :]
    den = lr * lr + li * li
    nr = ar - 1.0
    k1 = (nr * lr + ai * li) / den
    k2 = (ai * lr - nr * li) / den * half_sign
    kf = jnp.concatenate([k1[:, None, :], k2[:, None, :], jnp.zeros((g, 6, SW), F32)], axis=1)
    brt, bit = jnp.swapaxes(b_re, 1, 2), jnp.swapaxes(b_im, 1, 2)
    bt = jnp.concatenate([brt, bit], axis=-1)
    bts = jnp.concatenate([bit, brt], axis=-1)
    c2 = jnp.concatenate([c_re, c_im], axis=-1)
    c2s = jnp.concatenate([c_im, c_re], axis=-1)
    dtile = jnp.tile(d_skip.reshape(g, 1, SSM_GROUP), (1, 1, CHUNK))

    gspec = lambda r, w: pl.BlockSpec((GT, r, w), lambda a: (a, 0, 0))
    tile_spec = lambda w: pl.BlockSpec((None, GT * SSM_GROUP, w), lambda a: (a, 0, 0))
    nrow = len(POW_TAUS)
    lhs, bpow, cpow, bre, bim, cblk = pl.pallas_call(
        _ssm_prep_body,
        grid=(g // GT,),
        in_specs=[gspec(nrow, SW), gspec(nrow, SW), gspec(8, SW), gspec(SSM_GROUP, SW), gspec(SSM_GROUP, SW),
                  gspec(SSM_GROUP, SW), gspec(SSM_GROUP, SW), gspec(1, CW)],
        out_specs=[gspec(CW, CW), gspec(CW, SW), gspec(CW, SW),
                   tile_spec(GT * SSM_STATE), tile_spec(GT * SSM_STATE), tile_spec(2 * GT * SSM_STATE)],
        out_shape=[jax.ShapeDtypeStruct((g, CW, CW), BF16), jax.ShapeDtypeStruct((g, CW, SW), BF16),
                   jax.ShapeDtypeStruct((g, CW, SW), BF16),
                   jax.ShapeDtypeStruct((GT, GT * SSM_GROUP, GT * SSM_STATE), BF16),
                   jax.ShapeDtypeStruct((GT, GT * SSM_GROUP, GT * SSM_STATE), BF16),
                   jax.ShapeDtypeStruct((GT, GT * SSM_GROUP, 2 * GT * SSM_STATE), BF16)],
        name="ssm_prep",
    )(p1, p2, kf, bt, bts, c2, c2s, dtile)
    ar_row = ar[:, :SSM_STATE].reshape(1, -1)
    ai_row = ai[:, :SSM_STATE].reshape(1, -1)
    return lhs, bpow, cpow, p1, p2, ar_row, ai_row, bre, bim, cblk


def _attn_params(rel_bias, sinks):
    hp = lax.Precision.HIGHEST
    rb = rel_bias.astype(F32)
    onehot = jnp.asarray(_bucket_table()[:, None] == np.arange(N_BUCKETS)[None, :], F32)
    tbl = jnp.dot(onehot, rb, precision=hp)
    sink_rows = jnp.repeat(sinks.astype(F32).reshape(N_KV_HEADS, 1, KV_GROUP), WINDOW, axis=2)
    ds = np.clip(WINDOW - np.arange(KROWS), 0, WINDOW)
    bs = jnp.dot(jnp.asarray(ds[:, None] == np.arange(WINDOW + 1)[None, :], F32), tbl, precision=hp)
    bs = jnp.transpose(bs.reshape(KROWS, N_KV_HEADS, KV_GROUP), (0, 2, 1)).reshape(KROWS, N_HEADS)
    bias_s = jnp.pad(bs, ((0, 0), (0, 128 - N_HEADS)))
    sk = jnp.transpose(sinks.astype(F32).reshape(N_KV_HEADS, KV_GROUP), (1, 0)).reshape(1, N_HEADS)
    sink_row = jnp.pad(sk, ((0, 0), (0, 128 - N_HEADS)))
    seg_np = np.zeros((D_MODEL, 128), np.float32)
    for j in range(KV_GROUP):
        for kh in range(N_KV_HEADS):
            r0 = j * KV_WIDTH + kh * HEAD_DIM
            seg_np[r0:r0 + HEAD_DIM, j * N_KV_HEADS + kh] = 1.0
    seg = jnp.asarray(seg_np, BF16)
    segt = jnp.asarray(seg_np.T, BF16)
    return sink_rows, bias_s, sink_row, seg, segt


def kernel(x_prompt, x_sample, c_prompt, c_sample, state_ssm_re, state_ssm_im, cache_swa_k, cache_swa_v,
           w_ada, b_ada, w_in, ssm_lambda_re, ssm_lambda_im, ssm_log_delta, ssm_b_re, ssm_b_im,
           ssm_c_re, ssm_c_im, ssm_d, w_glu, b_glu, attn_sinks, rel_bias, w_branch_s, w_branch_a,
           w_out, ln_g, ln_b):
    assert w_ada.shape[0] == 1, "single-layer trunk"
    n_batch, seq, _ = x_prompt.shape
    n_dec = x_sample.shape[0]

    wi = w_in[0]
    w_in_b = jnp.concatenate([wi[:, 0:1024], wi[:, 1024:2048],
                              _regroup_heads(wi[:, 2048:3072], 1) * (HEAD_DIM ** -0.5),
                              _regroup_heads(wi[:, 3584:4608], 1), wi[:, 4608:5632], wi[:, 5632:6656],
                              wi[:, 3072:3328], wi[:, 3328:3584]], axis=1).astype(BF16)
    wvt = jnp.transpose(wi[:, 3328:3584]).astype(BF16)
    wut = jnp.transpose(wi[:, 0:D_MODEL]).astype(BF16)
    wglu = w_glu[0].astype(BF16)
    bglu = b_glu[0].reshape(1, -1).astype(F32)
    wbs = w_branch_s[0].astype(BF16)
    wba = _regroup_heads(w_branch_a[0], 0).astype(BF16)
    wout = w_out[0].astype(BF16)
    lng = ln_g[0].reshape(1, -1).astype(F32)
    lnb = ln_b[0].reshape(1, -1).astype(F32)
    lhs, bpow, cpow, p1, p2, ar_row, ai_row, bre, bim, cblk = _ssm_params(
        ssm_lambda_re[0], ssm_lambda_im[0], ssm_log_delta[0], ssm_b_re[0], ssm_b_im[0],
        ssm_c_re[0], ssm_c_im[0], ssm_d[0])
    sink_rows, bias_s, sink_row, seg, segt = _attn_params(rel_bias, attn_sinks[0])

    c_all = jnp.concatenate([c_prompt, jnp.zeros((8 - n_batch, D_MODEL), F32), c_sample], axis=0)
    mod = _ada(c_all, w_ada[0], b_ada[0])
    modp = mod[:n_batch].reshape(n_batch * 3, 1, D_MODEL)
    mods = mod[8:8 + n_dec]

    xp = x_prompt.reshape(n_batch * seq, D_MODEL)
    yssm, hfin = _ssm_prompt(x_prompt, modp, wut, lhs, bpow, cpow, p1, p2)
    yp, ktail, vtail = _main_prompt(rel_bias.astype(F32).reshape(-1), xp, yssm, modp, w_in_b, wvt, sink_rows,
                                    wglu, bglu, wbs, wba, wout, lng, lnb, n_batch, seq)
    yp = yp.reshape(n_batch, seq, D_MODEL)
    p_hr = hfin[:, :, 0, :SSM_STATE][None]
    p_hi = hfin[:, :, 0, SSM_STATE:][None]
    p_k = ktail.reshape(1, n_batch, WINDOW, N_KV_HEADS, HEAD_DIM)
    p_v = vtail.reshape(1, n_batch, WINDOW, N_KV_HEADS, HEAD_DIM)

    xs = x_sample.reshape(n_dec, D_MODEL)
    proj_s = _proj(xs, mods, mods,
                   pl.BlockSpec((n_dec, D_MODEL), lambda i: (0, 0)),
                   pl.BlockSpec((n_dec, D_MODEL), lambda i: (0, 1)),
                   w_in_b, n_dec, D_IN)
    ns = SSM_GROUPS * SSM_STATE
    ys_s, s_hr, s_hi = _sample_ssm(proj_s, state_ssm_re[0].reshape(n_dec, ns), state_ssm_im[0].reshape(n_dec, ns),
                                   ar_row, ai_row, bre, bim, cblk, ssm_d[0].reshape(1, -1).astype(F32))
    o_s, nk, nv = _sample_attn(proj_s, cache_swa_k[0].reshape(n_dec, WINDOW, KV_WIDTH),
                               cache_swa_v[0].reshape(n_dec, WINDOW, KV_WIDTH), bias_s, sink_row, seg, segt)
    ysmp = _final_sample(xs, ys_s, o_s, proj_s, mods, wglu, bglu, wbs, wba, wout, lng, lnb)

    return (yp, ysmp.reshape(n_dec, 1, D_MODEL),
            p_hr, p_hi, p_k, p_v,
            s_hr.reshape(1, n_dec, SSM_GROUPS, SSM_STATE), s_hi.reshape(1, n_dec, SSM_GROUPS, SSM_STATE),
            nk.reshape(1, n_dec, WINDOW, N_KV_HEADS, HEAD_DIM), nv.reshape(1, n_dec, WINDOW, N_KV_HEADS, HEAD_DIM))
```

```python
import functools
import math

import numpy as np
import jax
import jax.numpy as jnp
from jax import lax
from jax.experimental import pallas as pl
from jax.experimental.pallas import tpu as pltpu

F32 = jnp.float32
BF16 = jnp.bfloat16

D_MODEL = 1024
SSM_GROUPS = 64
SSM_GROUP = 16
SSM_STATE = 64
N_HEADS = 16
HEAD_DIM = 64
N_KV_HEADS = 4
KV_GROUP = 4
KV_WIDTH = N_KV_HEADS * HEAD_DIM
WINDOW = 128
N_BUCKETS = 32
NEG_INF = -1e30
LN_EPS = 1e-5
DEPTH = 1
DEEPNORM_ALPHA = (2 * DEPTH) ** 0.25
D_IN = 6656
CHUNK = 16
CW = CHUNK * SSM_GROUP
SW = 2 * SSM_STATE

COL_U, COL_ZS, COL_Q, COL_ZA, COL_GS, COL_GA = 0, 1, 2, 3, 4, 5
COL_K, COL_V = 24, 25

VMEM_LIMIT = 56 * 1024 * 1024
VMEM_LIMIT_SSM = 60 * 1024 * 1024


def _regroup_heads(w, axis):
    shp = w.shape
    if axis == 1:
        return jnp.transpose(w.reshape(shp[0], N_KV_HEADS, KV_GROUP, HEAD_DIM), (0, 2, 1, 3)).reshape(shp)
    return jnp.transpose(w.reshape(N_KV_HEADS, KV_GROUP, HEAD_DIM, shp[1]), (1, 0, 2, 3)).reshape(shp)


def _sigmoid(x):
    return 0.5 * jnp.tanh(0.5 * x) + 0.5


def _silu(x):
    return x * _sigmoid(x)


def _gelu_tanh(x):
    c = math.sqrt(2.0 / math.pi)
    hx = 0.5 * x
    return hx + hx * jnp.tanh(x * (c + (c * 0.044715) * (x * x)))


def _dot(a, b):
    return jnp.dot(a, b, preferred_element_type=F32)


def _dot_nt(a, b):
    return lax.dot_general(a, b, (((1,), (1,)), ((), ())), preferred_element_type=F32)


def _ada_body(c_ref, w_ref, b_ref, o_ref):
    c = c_ref[...]
    sc = _silu(c).astype(BF16)
    o_ref[...] = _dot(sc, w_ref[...].astype(BF16)) + b_ref[...]


def _ada(c_all, w_ada, b_ada):
    rows = c_all.shape[0]
    bn = 512
    return pl.pallas_call(
        _ada_body,
        grid=(3 * D_MODEL // bn,),
        in_specs=[pl.BlockSpec((rows, D_MODEL), lambda j: (0, 0)),
                  pl.BlockSpec((D_MODEL, bn), lambda j: (0, j)),
                  pl.BlockSpec((1, bn), lambda j: (0, j))],
        out_specs=pl.BlockSpec((rows, bn), lambda j: (0, j)),
        out_shape=jax.ShapeDtypeStruct((rows, 3 * D_MODEL), F32),
        name="ada",
    )(c_all, w_ada, b_ada.reshape(1, -1))


PROJ_CB = 512


def _proj_body(x_ref, shift_ref, scale_ref, w_ref, o_ref, *, n_out):
    h = x_ref[...] * (1.0 + scale_ref[...]) + shift_ref[...]
    hb = h.astype(BF16)
    for j in range(n_out // PROJ_CB):
        sl = slice(j * PROJ_CB, (j + 1) * PROJ_CB)
        o_ref[:, sl] = _dot(hb, w_ref[:, sl]).astype(BF16)


def _proj(x2d, shift_arr, scale_arr, shift_spec, scale_spec, w_in_b, tm, n_out):
    rows = x2d.shape[0]
    return pl.pallas_call(
        functools.partial(_proj_body, n_out=n_out),
        grid=(rows // tm,),
        in_specs=[pl.BlockSpec((tm, D_MODEL), lambda i: (i, 0)),
                  shift_spec, scale_spec,
                  pl.BlockSpec((D_MODEL, n_out), lambda i: (0, 0), pipeline_mode=pl.Buffered(1))],
        out_specs=pl.BlockSpec((tm, n_out), lambda i: (i, 0)),
        out_shape=jax.ShapeDtypeStruct((rows, n_out), BF16),
        compiler_params=pltpu.CompilerParams(vmem_limit_bytes=VMEM_LIMIT),
        name="proj",
    )(x2d, shift_arr, scale_arr, w_in_b)


NKS = 128
GC = 16


def _cmul(c1, c2, x):
    return c1 * x + c2 * pltpu.roll(x, SSM_STATE, x.ndim - 1)


SSM_TOK = NKS * CHUNK
SUB = 8
NPH = CHUNK // SUB


def _dot_tn(a, b):
    return lax.dot_general(a, b, (((0,), (0,)), ((), ())), preferred_element_type=F32)


SEG = 8
SEG_LEN = NKS // SEG
POW_TAUS = (tuple(range(CHUNK + 1)) + tuple(CHUNK * m for m in range(2, SEG_LEN))
            + tuple(CHUNK * SEG_LEN * m for m in range(1, SEG + 1)) + (0,))
ROW_CHUNK1 = CHUNK
ROW_SEG1 = CHUNK + SEG_LEN - 1


def _ssm_body(x_ref, shift_ref, scale_ref, wut_ref, lhs_ref, bpow_ref, cpow_ref, p1_ref, p2_ref,
              y_ref, hfin_ref, r_scr, yt_scr, s_scr, carry_scr):
    i = pl.program_id(1)
    j = pl.program_id(2)

    @pl.when((i == 0) & (j == 0))
    def _():
        carry_scr[...] = jnp.zeros_like(carry_scr)

    @pl.when(j < NPH)
    def _():
        pairs = range(SUB // 2)
        hbs = []
        for sp in pairs:
            h = jnp.concatenate([x_ref[:, b, s, :] for s in (2 * sp, 2 * sp + 1) for b in range(SEG_LEN)],
                                axis=0)
            hbs.append((h * (1.0 + scale_ref[...]) + shift_ref[...]).astype(BF16))
        uts = [_dot_nt(wut_ref[...], hb).astype(BF16) for hb in hbs]
        for sp, ut in zip(pairs, uts):
            ut = ut.reshape(SSM_GROUPS, SSM_GROUP, 2 * NKS)
            pair = jnp.concatenate([ut[:, :, :NKS], ut[:, :, NKS:]], axis=1)
            r0 = pl.multiple_of(j * SUB * SSM_GROUP + sp * 2 * SSM_GROUP, 2 * SSM_GROUP)
            r_scr[:, pl.ds(r0, 2 * SSM_GROUP), :] = pair

    @pl.when(j == NPH - 1)
    def _():
        def local(g):
            rg = r_scr[g]
            yt_scr[g] = _dot(lhs_ref[g], rg)
            s_scr[g] = _dot_tn(rg, bpow_ref[g])

        def carried(g):
            yt_scr[g] += _dot_nt(cpow_ref[g], s_scr[g].astype(BF16))

        row = lax.broadcasted_iota(jnp.int32, (GC, SEG, SW), 1)
        seg_rows = lambda b: slice(b * SEG, (b + 1) * SEG)

        def scan(gc):
            gs = slice(gc * GC, (gc + 1) * GC)
            p1, p2 = p1_ref[gs], p2_ref[gs]
            c1, c2 = p1[:, ROW_CHUNK1:ROW_CHUNK1 + 1, :], p2[:, ROW_CHUNK1:ROW_CHUNK1 + 1, :]
            h = jnp.zeros((GC, SEG, SW), F32)
            hs = h
            for b in range(SEG_LEN):
                sb = s_scr[gs, seg_rows(b), :]
                h, hs = sb + c1 * h + c2 * hs, pltpu.roll(sb, SSM_STATE, 2) + c1 * hs - c2 * h
                s_scr[gs, seg_rows(b), :] = h
                yield
            e = h
            for d in (1, 2, 4):
                rd = ROW_SEG1 + d - 1
                sh = jnp.where(row >= d, pltpu.roll(e, d, 1), 0.0)
                e = e + _cmul(p1[:, rd:rd + 1, :], p2[:, rd:rd + 1, :], sh)
            carry = carry_scr[gs]
            e = e + _cmul(p1[:, ROW_SEG1:ROW_SEG1 + SEG, :], p2[:, ROW_SEG1:ROW_SEG1 + SEG, :], carry)
            ein = jnp.where(row == 0, carry, pltpu.roll(e, 1, 1))
            eins = pltpu.roll(ein, SSM_STATE, 2)
            carry_scr[gs] = jnp.broadcast_to(e[:, SEG - 1:SEG, :], (GC, SEG, SW))
            yield
            for b in range(SEG_LEN - 1, 0, -1):
                rb = ROW_CHUNK1 + b - 1
                s_scr[gs, seg_rows(b), :] = (s_scr[gs, seg_rows(b - 1), :]
                                             + p1[:, rb:rb + 1, :] * ein + p2[:, rb:rb + 1, :] * eins)
                yield
            s_scr[gs, seg_rows(0), :] = ein

        def interleave(fn, groups, gen):
            for g in groups:
                fn(g)
                for _ in range(2):
                    next(gen, None)
            for _ in gen:
                pass

        n_sets = SSM_GROUPS // GC
        sets = [range(q * GC, (q + 1) * GC) for q in range(n_sets)]
        for g in sets[0]:
            local(g)
        for q in range(1, n_sets):
            interleave(local, sets[q], scan(q - 1))
        interleave(carried, sets[0], scan(n_sets - 1))
        for q in range(1, n_sets):
            for g in sets[q]:
                carried(g)
        hfin_ref[...] = carry_scr[...]

    @pl.when(j >= NPH)
    def _():
        for sp in range(SUB):
            t0 = pl.multiple_of((j - NPH) * SUB * SSM_GROUP + sp * SSM_GROUP, SSM_GROUP)
            yt = yt_scr[:, pl.ds(t0, SSM_GROUP), :].reshape(D_MODEL, NKS)
            yr = yt.T
            for b in range(SEG_LEN):
                y_ref[:, b, sp, :] = yr[b * SEG:(b + 1) * SEG, :]


def _ssm_prompt(x_prompt, modp, wut, lhs, bpow, cpow, p1, p2):
    n_batch, seq, _ = x_prompt.shape
    steps = seq // SSM_TOK
    xv = x_prompt.reshape(n_batch * steps * SEG, SEG_LEN, CHUNK, D_MODEL)
    mod_spec = lambda k: pl.BlockSpec((None, 1, D_MODEL), lambda b, i, j, k=k: (b * 3 + k, 0, 0))
    consts = (wut, lhs, bpow, cpow, p1, p2)
    y, hfin = pl.pallas_call(
        _ssm_body,
        grid=(n_batch, steps, 2 * NPH),
        in_specs=[pl.BlockSpec((SEG, SEG_LEN, SUB, D_MODEL),
                               lambda b, i, j: (b * steps + i, 0, jnp.minimum(j, NPH - 1), 0)),
                  mod_spec(0), mod_spec(1)] + [_const_spec(c.shape) for c in consts],
        out_specs=[pl.BlockSpec((SEG, SEG_LEN, SUB, D_MODEL),
                                lambda b, i, j: (b * steps + i, 0, jnp.maximum(j - NPH, 0), 0)),
                   pl.BlockSpec((None, SSM_GROUPS, 8, SW), lambda b, i, j: (b, 0, 0, 0))],
        out_shape=[jax.ShapeDtypeStruct(xv.shape, F32),
                   jax.ShapeDtypeStruct((n_batch, SSM_GROUPS, 8, SW), F32)],
        scratch_shapes=[pltpu.VMEM((SSM_GROUPS, CW, NKS), BF16), pltpu.VMEM((SSM_GROUPS, CW, NKS), F32),
                        pltpu.VMEM((SSM_GROUPS, NKS, SW), F32), pltpu.VMEM((SSM_GROUPS, 8, SW), F32)],
        compiler_params=pltpu.CompilerParams(dimension_semantics=("arbitrary", "arbitrary", "arbitrary"),
                                             vmem_limit_bytes=VMEM_LIMIT_SSM),
        name="ssm_prompt",
    )(xv, modp, modp, *consts)
    return y.reshape(n_batch * seq, D_MODEL), hfin


def _post(chunks, gate, wglu_ref, bglu_ref, wbs_ref, wba_ref, wout_ref, lng_ref, lnb_ref):
    data = [load() for load in chunks]
    ys = [_gelu_tanh(d[0]) for d in data]
    glu = [y * _sigmoid(_dot(y.astype(BF16), wglu_ref[...]) + bglu_ref[...]) for y in ys]
    b_s = [_dot(g.astype(BF16) * _silu(d[1]), wbs_ref[...]) for g, d in zip(glu, data)]
    b_a = [_dot(d[2].astype(BF16) * _silu(d[3]), wba_ref[...]) for d in data]
    m = [_sigmoid(d[4]).astype(F32) * s + _sigmoid(d[5]).astype(F32) * a for d, s, a in zip(data, b_s, b_a)]
    out = [_dot(v.astype(BF16), wout_ref[...]) for v in m]
    res = []
    for d, o in zip(data, out):
        r = DEEPNORM_ALPHA * d[6] + gate * o
        mu = jnp.mean(r, axis=-1, keepdims=True)
        rc = r - mu
        var = jnp.mean(rc * rc, axis=-1, keepdims=True)
        res.append(rc * lax.rsqrt(var + LN_EPS) * lng_ref[...] + lnb_ref[...])
    return res


TB = 512
NQB = TB // WINDOW
W_OFF = D_MODEL
ONES_ROWS = 16
POST_ROWS = 256


def _const_spec(shape):
    nd = len(shape)
    return pl.BlockSpec(shape, lambda *_: (0,) * nd, pipeline_mode=pl.Buffered(1))


def _bucket_thresholds():
    tbl = _bucket_table()
    return [int(np.argmax(tbl >= k)) for k in range(1, N_BUCKETS)]


def _fill_bias_table(rb_ref, bias_s):
    si = lax.broadcasted_iota(jnp.int32, (2 * WINDOW, WINDOW), 0)
    qi = lax.broadcasted_iota(jnp.int32, (2 * WINDOW, WINDOW), 1)
    dist = qi + WINDOW - si
    bid = jnp.zeros((2 * WINDOW, WINDOW), jnp.int32)
    for thr in _bucket_thresholds():
        bid = bid + (dist >= thr).astype(jnp.int32)
    valid = (dist >= 0) & (dist <= WINDOW)
    own = si >= WINDOW

    def per_head(h, c):
        t = jnp.zeros((2 * WINDOW, WINDOW), F32)
        for b in range(N_BUCKETS):
            t = jnp.where(bid == b, rb_ref[b * N_HEADS + h], t)
        t = jnp.where(valid, t, NEG_INF)
        kh, j = h // KV_GROUP, h % KV_GROUP
        bias_s[0, kh, j] = t
        bias_s[1, kh, j] = jnp.where(own, t, NEG_INF)
        return c

    lax.fori_loop(0, N_HEADS, per_head, 0)


def _main_body(rb_ref, x_ref, yssm_ref, shift_ref, scale_ref, gate_ref, wmain_ref, sink_ref,
               wglu_ref, bglu_ref, wbs_ref, wba_ref, wout_ref, lng_ref, lnb_ref,
               y_ref, ktail_ref, vtail_ref,
               zs_s, q_s, za_s, gs_s, ga_s, k_s, vt_s, ot_s, o_s, bias_s):
    i = pl.program_id(1)

    @pl.when((pl.program_id(0) == 0) & (i == 0))
    def _():
        _fill_bias_table(rb_ref, bias_s)

    @pl.when(i == 0)
    def _():
        k_s[0:WINDOW, :] = jnp.zeros((WINDOW, KV_WIDTH), BF16)
        vt_s[:, 0:WINDOW] = jnp.zeros((KV_WIDTH, WINDOW), BF16)

    x = x_ref[...]
    hb = (x * (1.0 + scale_ref[...]) + shift_ref[...]).astype(BF16)

    def project(dst, c0, c):
        lo = W_OFF + c0 + c * 512
        dst[:, c * 512:(c + 1) * 512] = _dot(hb, wmain_ref[:, lo:lo + 512]).astype(BF16)

    for c in range(2):
        project(q_s, D_MODEL, c)
    deferred = [(dst, c0, c) for dst, c0 in ((zs_s, 0), (za_s, 2 * D_MODEL), (gs_s, 3 * D_MODEL), (ga_s, 4 * D_MODEL))
                for c in range(2)]

    kv = _dot(hb, wmain_ref[:, W_OFF + 5 * D_MODEL:W_OFF + 5 * D_MODEL + 2 * KV_WIDTH]).astype(BF16)
    k_s[WINDOW:WINDOW + TB, :] = kv[:, :KV_WIDTH]
    vt_s[:, WINDOW:WINDOW + TB] = kv[:, KV_WIDTH:].T

    first = jnp.where(i == 0, 1, 0)
    lane = lax.broadcasted_iota(jnp.int32, (2 * WINDOW, 128), 1)
    ones = jnp.ones((ONES_ROWS, 2 * WINDOW), BF16)
    for blk in range(NQB):
        rows = slice(blk * WINDOW, (blk + 1) * WINDOW)
        kcat = k_s[blk * WINDOW:blk * WINDOW + 2 * WINDOW, :]
        vtc = vt_s[:, blk * WINDOW:blk * WINDOW + 2 * WINDOW]
        sel = first if blk == 0 else 0
        heads = range(N_KV_HEADS)
        sts = []
        for kh in heads:
            pair, odd = kh // 2, kh % 2
            kcm = jnp.where((lane >= HEAD_DIM) == bool(odd), kcat[:, pair * 128:(pair + 1) * 128], jnp.zeros((), BF16))
            qp = jnp.concatenate(
                [q_s[rows, j * KV_WIDTH + pair * 128:j * KV_WIDTH + (pair + 1) * 128] for j in range(KV_GROUP)],
                axis=0)
            bias = jnp.concatenate([bias_s[sel, kh, j] for j in range(KV_GROUP)], axis=1)
            sts.append(_dot_nt(kcm, qp) + bias)
        mine = deferred[blk * len(deferred) // NQB:(blk + 1) * len(deferred) // NQB]
        ms = [jnp.maximum(jnp.max(sts[kh], axis=0, keepdims=True), sink_ref[kh]) for kh in heads]
        project(*mine[0])
        es = [jnp.exp(sts[kh] - ms[kh]).astype(BF16) for kh in heads]
        rs = [_dot(jnp.concatenate([vtc[kh * HEAD_DIM:(kh + 1) * HEAD_DIM, :], ones], axis=0), es[kh])
              for kh in heads]
        for spec in mine[1:]:
            project(*spec)
        for kh in heads:
            den = rs[kh][HEAD_DIM:HEAD_DIM + 1, :] + jnp.exp(sink_ref[kh] - ms[kh])
            ot = (rs[kh][:HEAD_DIM, :] / den).astype(BF16)
            for j in range(KV_GROUP):
                ot_s[blk, j, kh * HEAD_DIM:(kh + 1) * HEAD_DIM, :] = ot[:, j * WINDOW:(j + 1) * WINDOW]
        for j in range(KV_GROUP):
            o_s[rows, j * KV_WIDTH:(j + 1) * KV_WIDTH] = ot_s[blk, j].T

    k_s[0:WINDOW, :] = k_s[TB:TB + WINDOW, :]
    vt_s[:, 0:WINDOW] = vt_s[:, TB:TB + WINDOW]

    def loader(c):
        rs = slice(c * POST_ROWS, (c + 1) * POST_ROWS)
        return lambda: (yssm_ref[rs, :], zs_s[rs, :], o_s[rs, :], za_s[rs, :], gs_s[rs, :], ga_s[rs, :], x_ref[rs, :])

    ys = _post([loader(c) for c in range(TB // POST_ROWS)], gate_ref[...],
               wglu_ref, bglu_ref, wbs_ref, wba_ref, wout_ref, lng_ref, lnb_ref)
    for c, yc in enumerate(ys):
        y_ref[c * POST_ROWS:(c + 1) * POST_ROWS, :] = yc

    @pl.when(i == pl.num_programs(1) - 1)
    def _():
        ktail_ref[...] = k_s[0:WINDOW, :].astype(F32)
        vtail_ref[...] = vt_s[:, 0:WINDOW].T.astype(F32)


def _main_prompt(rel_bias_flat, x2d, yssm, modp, wmain, sink_rows, wglu, bglu, wbs, wba, wout, lng, lnb,
                 n_batch, seq):
    steps = seq // TB
    row_spec = pl.BlockSpec((TB, D_MODEL), lambda b, i: (b * steps + i, 0))
    mod_spec = lambda j: pl.BlockSpec((None, 1, D_MODEL), lambda b, i, j=j: (b * 3 + j, 0, 0))
    tail_spec = pl.BlockSpec((None, WINDOW, KV_WIDTH), lambda b, i: (b, 0, 0))
    consts = (wmain, sink_rows, wglu, bglu, wbs, wba, wout, lng, lnb)
    return pl.pallas_call(
        _main_body,
        grid=(n_batch, steps),
        in_specs=[pl.BlockSpec(memory_space=pltpu.SMEM), row_spec, row_spec, mod_spec(0), mod_spec(1), mod_spec(2)]
        + [_const_spec(c.shape) for c in consts],
        out_specs=[row_spec, tail_spec, tail_spec],
        out_shape=[jax.ShapeDtypeStruct((n_batch * seq, D_MODEL), F32),
                   jax.ShapeDtypeStruct((n_batch, WINDOW, KV_WIDTH), F32),
                   jax.ShapeDtypeStruct((n_batch, WINDOW, KV_WIDTH), F32)],
        scratch_shapes=[pltpu.VMEM((TB, D_MODEL), BF16)] * 5
        + [pltpu.VMEM((TB + WINDOW, KV_WIDTH), BF16), pltpu.VMEM((KV_WIDTH, TB + WINDOW), BF16),
           pltpu.VMEM((NQB, KV_GROUP, KV_WIDTH, WINDOW), BF16), pltpu.VMEM((TB, D_MODEL), BF16),
           pltpu.VMEM((2, N_KV_HEADS, KV_GROUP, 2 * WINDOW, WINDOW), F32)],
        compiler_params=pltpu.CompilerParams(dimension_semantics=("arbitrary", "arbitrary"),
                                             vmem_limit_bytes=VMEM_LIMIT),
        name="main_prompt",
    )(rel_bias_flat, x2d, yssm, modp, modp, modp, *consts)


GT = 8


def _sample_ssm_body(u_ref, h0r_ref, h0i_ref, ar_ref, ai_ref, bre_ref, bim_ref, cblk_ref, d_ref,
                     y_ref, hr_ref, hi_ref):
    ub = u_ref[...]
    ys = []
    for a in range(GT):
        ua = ub[:, a * 128:(a + 1) * 128]
        st = slice(a * 512, (a + 1) * 512)
        xr = _dot(ua, bre_ref[a])
        xi = _dot(ua, bim_ref[a])
        h0r = h0r_ref[:, st]
        h0i = h0i_ref[:, st]
        ar = ar_ref[:, st]
        ai = ai_ref[:, st]
        hr = ar * h0r - ai * h0i + xr
        hi = ar * h0i + ai * h0r + xi
        hr_ref[:, st] = hr
        hi_ref[:, st] = hi
        hcat = jnp.concatenate([hr, hi], axis=1).astype(BF16)
        ys.append(_dot_nt(hcat, cblk_ref[a]))
    y_ref[...] = jnp.concatenate(ys, axis=1) + d_ref[...] * ub.astype(F32)


def _sample_ssm(proj_s, h0r, h0i, ar, ai, bre, bim, cblk, d_row):
    rows = proj_s.shape[0]
    ns = SSM_GROUPS * SSM_STATE
    full = lambda shape: pl.BlockSpec(shape, lambda i: (0,) * len(shape))
    return pl.pallas_call(
        _sample_ssm_body,
        grid=(1,),
        in_specs=[pl.BlockSpec((rows, D_MODEL), lambda i: (0, COL_U)),
                  full((rows, ns)), full((rows, ns)), full((1, ns)), full((1, ns)),
                  full(bre.shape), full(bim.shape), full(cblk.shape), full((1, D_MODEL))],
        out_specs=[full((rows, D_MODEL)), full((rows, ns)), full((rows, ns))],
        out_shape=[jax.ShapeDtypeStruct((rows, D_MODEL), F32),
                   jax.ShapeDtypeStruct((rows, ns), F32),
                   jax.ShapeDtypeStruct((rows, ns), F32)],
        compiler_params=pltpu.CompilerParams(vmem_limit_bytes=VMEM_LIMIT),
        name="sample_ssm",
    )(proj_s, h0r, h0i, ar, ai, bre, bim, cblk, d_row)


SB = 16
KROWS = WINDOW + 8
SEQ_GROUP = 4


def _sample_attn_body(q_ref, kn_ref, vn_ref, ck_ref, cv_ref, bias_ref, sink_ref, seg_ref, segt_ref,
                      o_ref, nk_ref, nv_ref):
    qf = q_ref[...].astype(F32)
    knf = kn_ref[...].astype(F32)
    vnf = vn_ref[...].astype(F32)
    rowk = lax.broadcasted_iota(jnp.int32, (KROWS, 128), 0)
    roww = lax.broadcasted_iota(jnp.int32, (WINDOW, KV_WIDTH), 0)
    sink = sink_ref[...]
    last = roww == WINDOW - 1
    for g0 in range(0, SB, SEQ_GROUP):
        seqs = range(g0, g0 + SEQ_GROUP)
        scores = []
        for b in seqs:
            kall = jnp.concatenate([ck_ref[b], jnp.broadcast_to(knf[b:b + 1], (8, KV_WIDTH))], axis=0)
            qrow = qf[b:b + 1]
            prod = jnp.concatenate([kall * qrow[:, j * KV_WIDTH:(j + 1) * KV_WIDTH] for j in range(KV_GROUP)], axis=1)
            s = _dot(prod.astype(BF16), seg_ref[...]) + bias_ref[...]
            scores.append(jnp.where(rowk <= WINDOW, s, NEG_INF))
        probs = []
        for s in scores:
            m = jnp.maximum(jnp.max(s, axis=0, keepdims=True), sink)
            e = jnp.exp(s - m)
            den = jnp.sum(e, axis=0, keepdims=True) + jnp.exp(sink - m)
            probs.append((e / den).astype(BF16))
        pexps = [_dot(p, segt_ref[...]) for p in probs]
        for b, pexp in zip(seqs, pexps):
            vb = cv_ref[b]
            vall = jnp.concatenate([vb, jnp.broadcast_to(vnf[b:b + 1], (8, KV_WIDTH))], axis=0)
            v4 = jnp.concatenate([vall] * KV_GROUP, axis=1)
            o_ref[b:b + 1, :] = jnp.sum(pexp * v4, axis=0, keepdims=True)
            nv_ref[b] = jnp.where(last, jnp.broadcast_to(vnf[b:b + 1], (WINDOW, KV_WIDTH)), pltpu.roll(vb, WINDOW - 1, 0))
            nk_ref[b] = jnp.where(last, jnp.broadcast_to(knf[b:b + 1], (WINDOW, KV_WIDTH)),
                                  pltpu.roll(ck_ref[b], WINDOW - 1, 0))


def _sample_attn(proj_s, ck, cv, bias_s, sink_row, seg, segt):
    rows = proj_s.shape[0]
    cache_spec = pl.BlockSpec((SB, WINDOW, KV_WIDTH), lambda i: (i, 0, 0))
    return pl.pallas_call(
        _sample_attn_body,
        grid=(rows // SB,),
        in_specs=[pl.BlockSpec((SB, D_MODEL), lambda i: (i, COL_Q)),
                  pl.BlockSpec((SB, KV_WIDTH), lambda i: (i, COL_K)),
                  pl.BlockSpec((SB, KV_WIDTH), lambda i: (i, COL_V)),
                  cache_spec, cache_spec,
                  pl.BlockSpec(bias_s.shape, lambda i: (0, 0)),
                  pl.BlockSpec(sink_row.shape, lambda i: (0, 0)),
                  pl.BlockSpec(seg.shape, lambda i: (0, 0)),
                  pl.BlockSpec(segt.shape, lambda i: (0, 0))],
        out_specs=[pl.BlockSpec((SB, D_MODEL), lambda i: (i, 0)), cache_spec, cache_spec],
        out_shape=[jax.ShapeDtypeStruct((rows, D_MODEL), F32),
                   jax.ShapeDtypeStruct(ck.shape, F32),
                   jax.ShapeDtypeStruct(cv.shape, F32)],
        compiler_params=pltpu.CompilerParams(vmem_limit_bytes=VMEM_LIMIT),
        name="sample_attn",
    )(proj_s, proj_s, proj_s, ck, cv, bias_s, sink_row, seg, segt)


def _sample_final_body(x_ref, yssm_ref, zs_ref, o_ref, za_ref, gs_ref, ga_ref, gate_ref,
                       wglu_ref, bglu_ref, wbs_ref, wba_ref, wout_ref, lng_ref, lnb_ref, y_ref):
    load = lambda: (yssm_ref[...], zs_ref[...], o_ref[...], za_ref[...], gs_ref[...], ga_ref[...], x_ref[...])
    y_ref[...] = _post([load], gate_ref[...], wglu_ref, bglu_ref, wbs_ref, wba_ref, wout_ref, lng_ref, lnb_ref)[0]


def _final_sample(x2d, yssm, o_a, proj_s, mods, wglu, bglu, wbs, wba, wout, lng, lnb):
    rows = x2d.shape[0]
    full = lambda shape: pl.BlockSpec(shape, lambda i: (0,) * len(shape))
    pcol = lambda c: pl.BlockSpec((rows, D_MODEL), lambda i, c=c: (0, c))
    return pl.pallas_call(
        _sample_final_body,
        grid=(1,),
        in_specs=[full((rows, D_MODEL)), full((rows, D_MODEL)), pcol(COL_ZS), full((rows, D_MODEL)),
                  pcol(COL_ZA), pcol(COL_GS), pcol(COL_GA),
                  pl.BlockSpec((rows, D_MODEL), lambda i: (0, 2)),
                  full(wglu.shape), full(bglu.shape), full(wbs.shape), full(wba.shape), full(wout.shape),
                  full(lng.shape), full(lnb.shape)],
        out_specs=full((rows, D_MODEL)),
        out_shape=jax.ShapeDtypeStruct((rows, D_MODEL), F32),
        compiler_params=pltpu.CompilerParams(vmem_limit_bytes=VMEM_LIMIT),
        name="final_sample",
    )(x2d, yssm, proj_s, o_a, proj_s, proj_s, proj_s, mods, wglu, bglu, wbs, wba, wout, lng, lnb)


def _bucket_table():
    max_exact = N_BUCKETS // 2
    dist = np.arange(WINDOW + 1)
    df = np.maximum(dist, 1).astype(np.float32)
    large = max_exact + (np.log(df / np.float32(max_exact)) / np.float32(math.log(WINDOW / max_exact))
                         * np.float32(N_BUCKETS - max_exact)).astype(np.int32)
    large = np.minimum(large, N_BUCKETS - 1)
    return np.where(dist < max_exact, dist, large)


def _dot3_nt(a, b):
    ah = a.astype(BF16)
    al = (a - ah.astype(F32)).astype(BF16)
    bh = b.astype(BF16)
    bl = (b - bh.astype(F32)).astype(BF16)
    return _dot_nt(ah, bh) + _dot_nt(ah, bl) + _dot_nt(al, bh)


def _ssm_prep_body(p1_ref, p2_ref, kf_ref, bt_ref, bts_ref, c2_ref, c2s_ref, d_ref,
                   lhs_ref, bpow_ref, cpow_ref, bre_ref, bim_ref, cblk_ref):
    lane = lax.broadcasted_iota(jnp.int32, (SSM_GROUP, SW), 1)
    left = lane < SSM_STATE
    sgn = jnp.where(left, 1.0, -1.0)
    row_t = lax.broadcasted_iota(jnp.int32, (CW, 128), 0)
    lane_t = lax.broadcasted_iota(jnp.int32, (CW, 128), 1)
    bre_ref[...] = jnp.zeros_like(bre_ref)
    bim_ref[...] = jnp.zeros_like(bim_ref)
    cblk_ref[...] = jnp.zeros_like(cblk_ref)
    for gl in range(GT):
        p1, p2 = p1_ref[gl], p2_ref[gl]
        k1, k2 = kf_ref[gl, 0:1, :], kf_ref[gl, 1:2, :]
        bt, bts = bt_ref[gl], bts_ref[gl]
        c2, c2s = c2_ref[gl], c2s_ref[gl]
        bb = k1 * bt + k2 * bts
        bbs = k1 * bts - k2 * bt
        ca = [sgn * (c2 * p1[t:t + 1, :] + c2s * p2[t:t + 1, :]) for t in range(CHUNK + 1)]
        cpow_ref[gl] = jnp.concatenate(ca[1:], axis=0).astype(BF16)
        bpow_ref[gl] = jnp.concatenate(
            [p1[CHUNK - 1 - s:CHUNK - s, :] * bb + p2[CHUNK - 1 - s:CHUNK - s, :] * bbs for s in range(CHUNK)],
            axis=0).astype(BF16)
        e = _dot3_nt(jnp.concatenate(ca[:CHUNK], axis=0), jnp.concatenate([bb] * CHUNK, axis=0))
        tiles = []
        for lt in range(CW // 128):
            cols = slice(lt * 128, (lt + 1) * 128)
            et = e[:, cols]
            blk = (lane_t + lt * 128) // SSM_GROUP
            at = jnp.where(row_t == lane_t + lt * 128, d_ref[gl, :, cols], 0.0)
            for s in range(lt * 128 // SSM_GROUP, (lt + 1) * 128 // SSM_GROUP):
                sh = et if s == 0 else jnp.concatenate(
                    [jnp.zeros((s * SSM_GROUP, 128), F32), et[:CW - s * SSM_GROUP, :]], axis=0)
                at = at + jnp.where(blk == s, sh, 0.0)
            tiles.append(at)
        lhs_ref[gl] = jnp.concatenate(tiles, axis=1).astype(BF16)
        rs = slice(gl * SSM_GROUP, (gl + 1) * SSM_GROUP)
        ts = slice((gl // 2) * SW, (gl // 2 + 1) * SW)
        ti = slice(GT * SSM_STATE + (gl // 2) * SW, GT * SSM_STATE + (gl // 2 + 1) * SW)
        if gl % 2 == 0:
            bre_ref[rs, ts] = jnp.where(left, bb, 0.0).astype(BF16)
            bim_ref[rs, ts] = jnp.where(left, bbs, 0.0).astype(BF16)
            cblk_ref[rs, ts] = jnp.where(left, c2, 0.0).astype(BF16)
            cblk_ref[rs, ti] = jnp.where(left, -c2s, 0.0).astype(BF16)
        else:
            bre_ref[rs, ts] = jnp.where(left, 0.0, bbs).astype(BF16)
            bim_ref[rs, ts] = jnp.where(left, 0.0, bb).astype(BF16)
            cblk_ref[rs, ts] = jnp.where(left, 0.0, c2s).astype(BF16)
            cblk_ref[rs, ti] = jnp.where(left, 0.0, -c2).astype(BF16)


def _ssm_params(lam_re, lam_im, log_delta, b_re, b_im, c_re, c_im, d_skip):
    g = SSM_GROUPS
    dup = lambda a: jnp.concatenate([a, a], axis=-1)
    lr, li = dup(lam_re.astype(F32)), dup(lam_im.astype(F32))
    dt = jnp.exp(log_delta.astype(F32))[:, None]
    half_sign = jnp.asarray(np.where(np.arange(SW) < SSM_STATE, -1.0, 1.0), F32)
    tau = jnp.asarray(POW_TAUS, F32)[None, :, None]
    mag = jnp.exp((lr * dt)[:, None, :] * tau)
    ang = (li * dt)[:, None, :] * tau
    p1 = mag * jnp.cos(ang)
    pim = mag * jnp.sin(ang)
    p2 = pim * half_sign
    ar, ai = p1[:, 1, :], pim[:, 1, :]
    den = lr * lr + li * li
    nr = ar - 1.0
    k1 = (nr * lr + ai * li) / den
    k2 = (ai * lr - nr * li) / den * half_sign
    kf = jnp.concatenate([k1[:, None, :], k2[:, None, :], jnp.zeros((g, 6, SW), F32)], axis=1)
    brt, bit = jnp.swapaxes(b_re, 1, 2), jnp.swapaxes(b_im, 1, 2)
    bt = jnp.concatenate([brt, bit], axis=-1)
    bts = jnp.concatenate([bit, brt], axis=-1)
    c2 = jnp.concatenate([c_re, c_im], axis=-1)
    c2s = jnp.concatenate([c_im, c_re], axis=-1)
    dtile = jnp.tile(d_skip.reshape(g, 1, SSM_GROUP), (1, 1, CHUNK))

    gspec = lambda r, w: pl.BlockSpec((GT, r, w), lambda a: (a, 0, 0))
    tile_spec = lambda w: pl.BlockSpec((None, GT * SSM_GROUP, w), lambda a: (a, 0, 0))
    nrow = len(POW_TAUS)
    lhs, bpow, cpow, bre, bim, cblk = pl.pallas_call(
        _ssm_prep_body,
        grid=(g // GT,),
        in_specs=[gspec(nrow, SW), gspec(nrow, SW), gspec(8, SW), gspec(SSM_GROUP, SW), gspec(SSM_GROUP, SW),
                  gspec(SSM_GROUP, SW), gspec(SSM_GROUP, SW), gspec(1, CW)],
        out_specs=[gspec(CW, CW), gspec(CW, SW), gspec(CW, SW),
                   tile_spec(GT * SSM_STATE), tile_spec(GT * SSM_STATE), tile_spec(2 * GT * SSM_STATE)],
        out_shape=[jax.ShapeDtypeStruct((g, CW, CW), BF16), jax.ShapeDtypeStruct((g, CW, SW), BF16),
                   jax.ShapeDtypeStruct((g, CW, SW), BF16),
                   jax.ShapeDtypeStruct((GT, GT * SSM_GROUP, GT * SSM_STATE), BF16),
                   jax.ShapeDtypeStruct((GT, GT * SSM_GROUP, GT * SSM_STATE), BF16),
                   jax.ShapeDtypeStruct((GT, GT * SSM_GROUP, 2 * GT * SSM_STATE), BF16)],
        name="ssm_prep",
    )(p1, p2, kf, bt, bts, c2, c2s, dtile)
    ar_row = ar[:, :SSM_STATE].reshape(1, -1)
    ai_row = ai[:, :SSM_STATE].reshape(1, -1)
    return lhs, bpow, cpow, p1, p2, ar_row, ai_row, bre, bim, cblk


def _attn_params(rel_bias, sinks):
    hp = lax.Precision.HIGHEST
    rb = rel_bias.astype(F32)
    onehot = jnp.asarray(_bucket_table()[:, None] == np.arange(N_BUCKETS)[None, :], F32)
    tbl = jnp.dot(onehot, rb, precision=hp)
    sink_rows = jnp.repeat(sinks.astype(F32).reshape(N_KV_HEADS, 1, KV_GROUP), WINDOW, axis=2)
    ds = np.clip(WINDOW - np.arange(KROWS), 0, WINDOW)
    bs = jnp.dot(jnp.asarray(ds[:, None] == np.arange(WINDOW + 1)[None, :], F32), tbl, precision=hp)
    bs = jnp.transpose(bs.reshape(KROWS, N_KV_HEADS, KV_GROUP), (0, 2, 1)).reshape(KROWS, N_HEADS)
    bias_s = jnp.pad(bs, ((0, 0), (0, 128 - N_HEADS)))
    sk = jnp.transpose(sinks.astype(F32).reshape(N_KV_HEADS, KV_GROUP), (1, 0)).reshape(1, N_HEADS)
    sink_row = jnp.pad(sk, ((0, 0), (0, 128 - N_HEADS)))
    seg_np = np.zeros((D_MODEL, 128), np.float32)
    for j in range(KV_GROUP):
        for kh in range(N_KV_HEADS):
            r0 = j * KV_WIDTH + kh * HEAD_DIM
            seg_np[r0:r0 + HEAD_DIM, j * N_KV_HEADS + kh] = 1.0
    seg = jnp.asarray(seg_np, BF16)
    segt = jnp.asarray(seg_np.T, BF16)
    return sink_rows, bias_s, sink_row, seg, segt


def kernel(x_prompt, x_sample, c_prompt, c_sample, state_ssm_re, state_ssm_im, cache_swa_k, cache_swa_v,
           w_ada, b_ada, w_in, ssm_lambda_re, ssm_lambda_im, ssm_log_delta, ssm_b_re, ssm_b_im,
           ssm_c_re, ssm_c_im, ssm_d, w_glu, b_glu, attn_sinks, rel_bias, w_branch_s, w_branch_a,
           w_out, ln_g, ln_b):
    assert w_ada.shape[0] == 1, "single-layer trunk"
    n_batch, seq, _ = x_prompt.shape
    n_dec = x_sample.shape[0]

    wi = w_in[0]
    w_in_b = jnp.concatenate([wi[:, 0:1024], wi[:, 1024:2048],
                              _regroup_heads(wi[:, 2048:3072], 1) * (HEAD_DIM ** -0.5),
                              _regroup_heads(wi[:, 3584:4608], 1), wi[:, 4608:5632], wi[:, 5632:6656],
                              wi[:, 3072:3328], wi[:, 3328:3584]], axis=1).astype(BF16)
    wut = jnp.transpose(wi[:, 0:D_MODEL]).astype(BF16)
    wglu = w_glu[0].astype(BF16)
    bglu = b_glu[0].reshape(1, -1).astype(F32)
    wbs = w_branch_s[0].astype(BF16)
    wba = _regroup_heads(w_branch_a[0], 0).astype(BF16)
    wout = w_out[0].astype(BF16)
    lng = ln_g[0].reshape(1, -1).astype(F32)
    lnb = ln_b[0].reshape(1, -1).astype(F32)
    lhs, bpow, cpow, p1, p2, ar_row, ai_row, bre, bim, cblk = _ssm_params(
        ssm_lambda_re[0], ssm_lambda_im[0], ssm_log_delta[0], ssm_b_re[0], ssm_b_im[0],
        ssm_c_re[0], ssm_c_im[0], ssm_d[0])
    sink_rows, bias_s, sink_row, seg, segt = _attn_params(rel_bias, attn_sinks[0])

    c_all = jnp.concatenate([c_prompt, jnp.zeros((8 - n_batch, D_MODEL), F32), c_sample], axis=0)
    mod = _ada(c_all, w_ada[0], b_ada[0])
    modp = mod[:n_batch].reshape(n_batch * 3, 1, D_MODEL)
    mods = mod[8:8 + n_dec]

    xp = x_prompt.reshape(n_batch * seq, D_MODEL)
    yssm, hfin = _ssm_prompt(x_prompt, modp, wut, lhs, bpow, cpow, p1, p2)
    yp, ktail, vtail = _main_prompt(rel_bias.astype(F32).reshape(-1), xp, yssm, modp, w_in_b, sink_rows,
                                    wglu, bglu, wbs, wba, wout, lng, lnb, n_batch, seq)
    yp = yp.reshape(n_batch, seq, D_MODEL)
    p_hr = hfin[:, :, 0, :SSM_STATE][None]
    p_hi = hfin[:, :, 0, SSM_STATE:][None]
    p_k = ktail.reshape(1, n_batch, WINDOW, N_KV_HEADS, HEAD_DIM)
    p_v = vtail.reshape(1, n_batch, WINDOW, N_KV_HEADS, HEAD_DIM)

    xs = x_sample.reshape(n_dec, D_MODEL)
    proj_s = _proj(xs, mods, mods,
                   pl.BlockSpec((n_dec, D_MODEL), lambda i: (0, 0)),
                   pl.BlockSpec((n_dec, D_MODEL), lambda i: (0, 1)),
                   w_in_b, n_dec, D_IN)
    ns = SSM_GROUPS * SSM_STATE
    ys_s, s_hr, s_hi = _sample_ssm(proj_s, state_ssm_re[0].reshape(n_dec, ns), state_ssm_im[0].reshape(n_dec, ns),
                                   ar_row, ai_row, bre, bim, cblk, ssm_d[0].reshape(1, -1).astype(F32))
    o_s, nk, nv = _sample_attn(proj_s, cache_swa_k[0].reshape(n_dec, WINDOW, KV_WIDTH),
                               cache_swa_v[0].reshape(n_dec, WINDOW, KV_WIDTH), bias_s, sink_row, seg, segt)
    ysmp = _final_sample(xs, ys_s, o_s, proj_s, mods, wglu, bglu, wbs, wba, wout, lng, lnb)

    return (yp, ysmp.reshape(n_dec, 1, D_MODEL),
            p_hr, p_hi, p_k, p_v,
            s_hr.reshape(1, n_dec, SSM_GROUPS, SSM_STATE), s_hi.reshape(1, n_dec, SSM_GROUPS, SSM_STATE),
            nk.reshape(1, n_dec, WINDOW, N_KV_HEADS, HEAD_DIM), nv.reshape(1, n_dec, WINDOW, N_KV_HEADS, HEAD_DIM))
```

```python
import functools
import math

import numpy as np
import jax
import jax.numpy as jnp
from jax import lax
from jax.experimental import pallas as pl
from jax.experimental.pallas import tpu as pltpu

F32 = jnp.float32
BF16 = jnp.bfloat16

D_MODEL = 1024
SSM_GROUPS = 64
SSM_GROUP = 16
SSM_STATE = 64
N_HEADS = 16
HEAD_DIM = 64
N_KV_HEADS = 4
KV_GROUP = 4
KV_WIDTH = N_KV_HEADS * HEAD_DIM
WINDOW = 128
N_BUCKETS = 32
NEG_INF = -1e30
LN_EPS = 1e-5
DEPTH = 1
DEEPNORM_ALPHA = (2 * DEPTH) ** 0.25
D_IN = 6656
CHUNK = 16
CW = CHUNK * SSM_GROUP
SW = 2 * SSM_STATE

COL_U, COL_ZS, COL_Q, COL_ZA, COL_GS, COL_GA = 0, 1, 2, 3, 4, 5
COL_K, COL_V = 24, 25

VMEM_LIMIT = 56 * 1024 * 1024
VMEM_LIMIT_SSM = 60 * 1024 * 1024


def _sigmoid(x):
    return 0.5 * jnp.tanh(0.5 * x) + 0.5


def _silu(x):
    return x * _sigmoid(x)


def _gelu_tanh(x):
    c = math.sqrt(2.0 / math.pi)
    hx = 0.5 * x
    return hx + hx * jnp.tanh(x * (c + (c * 0.044715) * (x * x)))


def _dot(a, b):
    return jnp.dot(a, b, preferred_element_type=F32)


def _dot_nt(a, b):
    return lax.dot_general(a, b, (((1,), (1,)), ((), ())), preferred_element_type=F32)


def _ada_body(c_ref, w_ref, b_ref, o_ref):
    c = c_ref[...]
    sc = _silu(c).astype(BF16)
    o_ref[...] = _dot(sc, w_ref[...].astype(BF16)) + b_ref[...]


def _ada(c_all, w_ada, b_ada):
    rows = c_all.shape[0]
    bn = 512
    return pl.pallas_call(
        _ada_body,
        grid=(3 * D_MODEL // bn,),
        in_specs=[pl.BlockSpec((rows, D_MODEL), lambda j: (0, 0)),
                  pl.BlockSpec((D_MODEL, bn), lambda j: (0, j)),
                  pl.BlockSpec((1, bn), lambda j: (0, j))],
        out_specs=pl.BlockSpec((rows, bn), lambda j: (0, j)),
        out_shape=jax.ShapeDtypeStruct((rows, 3 * D_MODEL), F32),
        name="ada",
    )(c_all, w_ada, b_ada.reshape(1, -1))


PROJ_CB = 512


def _proj_body(x_ref, shift_ref, scale_ref, w_ref, o_ref, *, n_out):
    h = x_ref[...] * (1.0 + scale_ref[...]) + shift_ref[...]
    hb = h.astype(BF16)
    for j in range(n_out // PROJ_CB):
        sl = slice(j * PROJ_CB, (j + 1) * PROJ_CB)
        o_ref[:, sl] = _dot(hb, w_ref[:, sl]).astype(BF16)


def _proj(x2d, shift_arr, scale_arr, shift_spec, scale_spec, w_in_b, tm, n_out):
    rows = x2d.shape[0]
    return pl.pallas_call(
        functools.partial(_proj_body, n_out=n_out),
        grid=(rows // tm,),
        in_specs=[pl.BlockSpec((tm, D_MODEL), lambda i: (i, 0)),
                  shift_spec, scale_spec,
                  pl.BlockSpec((D_MODEL, n_out), lambda i: (0, 0), pipeline_mode=pl.Buffered(1))],
        out_specs=pl.BlockSpec((tm, n_out), lambda i: (i, 0)),
        out_shape=jax.ShapeDtypeStruct((rows, n_out), BF16),
        compiler_params=pltpu.CompilerParams(vmem_limit_bytes=VMEM_LIMIT),
        name="proj",
    )(x2d, shift_arr, scale_arr, w_in_b)


NKS = 128
GC = 16


def _cmul(c1, c2, x):
    return c1 * x + c2 * pltpu.roll(x, SSM_STATE, x.ndim - 1)


SSM_TOK = NKS * CHUNK
SUB = 8
NPH = CHUNK // SUB


def _dot_tn(a, b):
    return lax.dot_general(a, b, (((0,), (0,)), ((), ())), preferred_element_type=F32)


SEG = 8
SEG_LEN = NKS // SEG
POW_TAUS = (tuple(range(CHUNK + 1)) + tuple(CHUNK * m for m in range(2, SEG_LEN))
            + tuple(CHUNK * SEG_LEN * m for m in range(1, SEG + 1)) + (0,))
ROW_CHUNK1 = CHUNK
ROW_SEG1 = CHUNK + SEG_LEN - 1


def _ssm_body(x_ref, shift_ref, scale_ref, wut_ref, lhs_ref, bpow_ref, cpow_ref, p1_ref, p2_ref,
              y_ref, hfin_ref, r_scr, yt_scr, s_scr, carry_scr):
    i = pl.program_id(1)
    j = pl.program_id(2)

    @pl.when((i == 0) & (j == 0))
    def _():
        carry_scr[...] = jnp.zeros_like(carry_scr)

    @pl.when(j < NPH)
    def _():
        pairs = range(SUB // 2)
        hbs = []
        for sp in pairs:
            h = jnp.concatenate([x_ref[:, b, s, :] for s in (2 * sp, 2 * sp + 1) for b in range(SEG_LEN)],
                                axis=0)
            hbs.append((h * (1.0 + scale_ref[...]) + shift_ref[...]).astype(BF16))
        uts = [_dot_nt(wut_ref[...], hb).astype(BF16) for hb in hbs]
        for sp, ut in zip(pairs, uts):
            ut = ut.reshape(SSM_GROUPS, SSM_GROUP, 2 * NKS)
            pair = jnp.concatenate([ut[:, :, :NKS], ut[:, :, NKS:]], axis=1)
            r0 = pl.multiple_of(j * SUB * SSM_GROUP + sp * 2 * SSM_GROUP, 2 * SSM_GROUP)
            r_scr[:, pl.ds(r0, 2 * SSM_GROUP), :] = pair

    @pl.when(j == NPH - 1)
    def _():
        def local(g):
            rg = r_scr[g]
            yt_scr[g] = _dot(lhs_ref[g], rg)
            s_scr[g] = _dot_tn(rg, bpow_ref[g])

        def carried(g):
            yt_scr[g] += _dot_nt(cpow_ref[g], s_scr[g].astype(BF16))

        row = lax.broadcasted_iota(jnp.int32, (GC, SEG, SW), 1)
        seg_rows = lambda b: slice(b * SEG, (b + 1) * SEG)

        def scan(gc):
            gs = slice(gc * GC, (gc + 1) * GC)
            p1, p2 = p1_ref[gs], p2_ref[gs]
            c1, c2 = p1[:, ROW_CHUNK1:ROW_CHUNK1 + 1, :], p2[:, ROW_CHUNK1:ROW_CHUNK1 + 1, :]
            h = jnp.zeros((GC, SEG, SW), F32)
            hs = h
            for b in range(SEG_LEN):
                sb = s_scr[gs, seg_rows(b), :]
                h, hs = sb + c1 * h + c2 * hs, pltpu.roll(sb, SSM_STATE, 2) + c1 * hs - c2 * h
                s_scr[gs, seg_rows(b), :] = h
                yield
            e = h
            for d in (1, 2, 4):
                rd = ROW_SEG1 + d - 1
                sh = jnp.where(row >= d, pltpu.roll(e, d, 1), 0.0)
                e = e + _cmul(p1[:, rd:rd + 1, :], p2[:, rd:rd + 1, :], sh)
            carry = carry_scr[gs]
            e = e + _cmul(p1[:, ROW_SEG1:ROW_SEG1 + SEG, :], p2[:, ROW_SEG1:ROW_SEG1 + SEG, :], carry)
            ein = jnp.where(row == 0, carry, pltpu.roll(e, 1, 1))
            eins = pltpu.roll(ein, SSM_STATE, 2)
            carry_scr[gs] = jnp.broadcast_to(e[:, SEG - 1:SEG, :], (GC, SEG, SW))
            yield
            for b in range(SEG_LEN - 1, 0, -1):
                rb = ROW_CHUNK1 + b - 1
                s_scr[gs, seg_rows(b), :] = (s_scr[gs, seg_rows(b - 1), :]
                                             + p1[:, rb:rb + 1, :] * ein + p2[:, rb:rb + 1, :] * eins)
                yield
            s_scr[gs, seg_rows(0), :] = ein

        def interleave(fn, groups, gen):
            for g in groups:
                fn(g)
                for _ in range(2):
                    next(gen, None)
            for _ in gen:
                pass

        n_sets = SSM_GROUPS // GC
        sets = [range(q * GC, (q + 1) * GC) for q in range(n_sets)]
        for g in sets[0]:
            local(g)
        for q in range(1, n_sets):
            interleave(local, sets[q], scan(q - 1))
        interleave(carried, sets[0], scan(n_sets - 1))
        for q in range(1, n_sets):
            for g in sets[q]:
                carried(g)
        hfin_ref[...] = carry_scr[...]

    @pl.when(j >= NPH)
    def _():
        for sp in range(SUB):
            t0 = pl.multiple_of((j - NPH) * SUB * SSM_GROUP + sp * SSM_GROUP, SSM_GROUP)
            yt = yt_scr[:, pl.ds(t0, SSM_GROUP), :].reshape(D_MODEL, NKS)
            yr = yt.T
            for b in range(SEG_LEN):
                y_ref[:, b, sp, :] = yr[b * SEG:(b + 1) * SEG, :]


def _ssm_prompt(x_prompt, modp, wut, lhs, bpow, cpow, p1, p2):
    n_batch, seq, _ = x_prompt.shape
    steps = seq // SSM_TOK
    xv = x_prompt.reshape(n_batch * steps * SEG, SEG_LEN, CHUNK, D_MODEL)
    mod_spec = lambda k: pl.BlockSpec((None, 1, D_MODEL), lambda b, i, j, k=k: (b * 3 + k, 0, 0))
    consts = (wut, lhs, bpow, cpow, p1, p2)
    y, hfin = pl.pallas_call(
        _ssm_body,
        grid=(n_batch, steps, 2 * NPH),
        in_specs=[pl.BlockSpec((SEG, SEG_LEN, SUB, D_MODEL),
                               lambda b, i, j: (b * steps + i, 0, jnp.minimum(j, NPH - 1), 0)),
                  mod_spec(0), mod_spec(1)] + [_const_spec(c.shape) for c in consts],
        out_specs=[pl.BlockSpec((SEG, SEG_LEN, SUB, D_MODEL),
                                lambda b, i, j: (b * steps + i, 0, jnp.maximum(j - NPH, 0), 0)),
                   pl.BlockSpec((None, SSM_GROUPS, 8, SW), lambda b, i, j: (b, 0, 0, 0))],
        out_shape=[jax.ShapeDtypeStruct(xv.shape, F32),
                   jax.ShapeDtypeStruct((n_batch, SSM_GROUPS, 8, SW), F32)],
        scratch_shapes=[pltpu.VMEM((SSM_GROUPS, CW, NKS), BF16), pltpu.VMEM((SSM_GROUPS, CW, NKS), F32),
                        pltpu.VMEM((SSM_GROUPS, NKS, SW), F32), pltpu.VMEM((SSM_GROUPS, 8, SW), F32)],
        compiler_params=pltpu.CompilerParams(dimension_semantics=("arbitrary", "arbitrary", "arbitrary"),
                                             vmem_limit_bytes=VMEM_LIMIT_SSM),
        name="ssm_prompt",
    )(xv, modp, modp, *consts)
    return y.reshape(n_batch * seq, D_MODEL), hfin


def _post(chunks, gate, wglu_ref, bglu_ref, wbs_ref, wba_ref, wout_ref, lng_ref, lnb_ref):
    data = [load() for load in chunks]
    ys = [_gelu_tanh(d[0]) for d in data]
    glu = [y * _sigmoid(_dot(y.astype(BF16), wglu_ref[...]) + bglu_ref[...]) for y in ys]
    b_s = [_dot(g.astype(BF16) * _silu(d[1]), wbs_ref[...]) for g, d in zip(glu, data)]
    b_a = [_dot(d[2].astype(BF16) * _silu(d[3]), wba_ref[...]) for d in data]
    m = [_sigmoid(d[4]).astype(F32) * s + _sigmoid(d[5]).astype(F32) * a for d, s, a in zip(data, b_s, b_a)]
    out = [_dot(v.astype(BF16), wout_ref[...]) for v in m]
    res = []
    for d, o in zip(data, out):
        r = DEEPNORM_ALPHA * d[6] + gate * o
        mu = jnp.mean(r, axis=-1, keepdims=True)
        rc = r - mu
        var = jnp.mean(rc * rc, axis=-1, keepdims=True)
        res.append(rc * lax.rsqrt(var + LN_EPS) * lng_ref[...] + lnb_ref[...])
    return res


TB = 512
NQB = TB // WINDOW
W_OFF = D_MODEL
ONES_ROWS = 16
POST_ROWS = 256


def _const_spec(shape):
    nd = len(shape)
    return pl.BlockSpec(shape, lambda *_: (0,) * nd, pipeline_mode=pl.Buffered(1))


def _bucket_thresholds():
    tbl = _bucket_table()
    return [int(np.argmax(tbl >= k)) for k in range(1, N_BUCKETS)]


def _fill_bias_table(rb_ref, bias_s):
    si = lax.broadcasted_iota(jnp.int32, (2 * WINDOW, WINDOW), 0)
    qi = lax.broadcasted_iota(jnp.int32, (2 * WINDOW, WINDOW), 1)
    dist = qi + WINDOW - si
    bid = jnp.zeros((2 * WINDOW, WINDOW), jnp.int32)
    for thr in _bucket_thresholds():
        bid = bid + (dist >= thr).astype(jnp.int32)
    valid = (dist >= 0) & (dist <= WINDOW)
    own = si >= WINDOW

    def per_head(h, c):
        t = jnp.zeros((2 * WINDOW, WINDOW), F32)
        for b in range(N_BUCKETS):
            t = jnp.where(bid == b, rb_ref[b * N_HEADS + h], t)
        t = jnp.where(valid, t, NEG_INF)
        kh, j = h // KV_GROUP, h % KV_GROUP
        bias_s[0, kh, j] = t
        bias_s[1, kh, j] = jnp.where(own, t, NEG_INF)
        return c

    lax.fori_loop(0, N_HEADS, per_head, 0)


def _main_body(rb_ref, x_ref, yssm_ref, shift_ref, scale_ref, gate_ref, wmain_ref, sink_ref,
               wglu_ref, bglu_ref, wbs_ref, wba_ref, wout_ref, lng_ref, lnb_ref,
               y_ref, ktail_ref, vtail_ref,
               zs_s, q_s, za_s, gs_s, ga_s, k_s, vt_s, ot_s, o_s, bias_s):
    i = pl.program_id(1)

    @pl.when((pl.program_id(0) == 0) & (i == 0))
    def _():
        _fill_bias_table(rb_ref, bias_s)

    @pl.when(i == 0)
    def _():
        k_s[0:WINDOW, :] = jnp.zeros((WINDOW, KV_WIDTH), BF16)
        vt_s[:, 0:WINDOW] = jnp.zeros((KV_WIDTH, WINDOW), BF16)

    x = x_ref[...]
    hb = (x * (1.0 + scale_ref[...]) + shift_ref[...]).astype(BF16)

    def project(dst, c0, c):
        lo = W_OFF + c0 + c * 512
        dst[:, c * 512:(c + 1) * 512] = _dot(hb, wmain_ref[:, lo:lo + 512]).astype(BF16)

    for c in range(2):
        project(q_s, D_MODEL, c)
    deferred = [(dst, c0, c) for dst, c0 in ((zs_s, 0), (za_s, 2 * D_MODEL), (gs_s, 3 * D_MODEL), (ga_s, 4 * D_MODEL))
                for c in range(2)]

    kv = _dot(hb, wmain_ref[:, W_OFF + 5 * D_MODEL:W_OFF + 5 * D_MODEL + 2 * KV_WIDTH]).astype(BF16)
    k_s[WINDOW:WINDOW + TB, :] = kv[:, :KV_WIDTH]
    vt_s[:, WINDOW:WINDOW + TB] = kv[:, KV_WIDTH:].T

    first = jnp.where(i == 0, 1, 0)
    lane = lax.broadcasted_iota(jnp.int32, (2 * WINDOW, 128), 1)
    ones = jnp.ones((ONES_ROWS, 2 * WINDOW), BF16)
    for blk in range(NQB):
        rows = slice(blk * WINDOW, (blk + 1) * WINDOW)
        kcat = k_s[blk * WINDOW:blk * WINDOW + 2 * WINDOW, :]
        vtc = vt_s[:, blk * WINDOW:blk * WINDOW + 2 * WINDOW]
        sel = first if blk == 0 else 0
        heads = range(N_KV_HEADS)
        sts = []
        for kh in heads:
            pair, odd = kh // 2, kh % 2
            kcm = jnp.where((lane >= HEAD_DIM) == bool(odd), kcat[:, pair * 128:(pair + 1) * 128], jnp.zeros((), BF16))
            qp = jnp.concatenate(
                [q_s[rows, j * KV_WIDTH + pair * 128:j * KV_WIDTH + (pair + 1) * 128] for j in range(KV_GROUP)],
                axis=0)
            bias = jnp.concatenate([bias_s[sel, kh, j] for j in range(KV_GROUP)], axis=1)
            sts.append(_dot_nt(kcm, qp) + bias)
        mine = deferred[blk * len(deferred) // NQB:(blk + 1) * len(deferred) // NQB]
        ms = [jnp.maximum(jnp.max(sts[kh], axis=0, keepdims=True), sink_ref[kh]) for kh in heads]
        project(*mine[0])
        es = [jnp.exp(sts[kh] - ms[kh]).astype(BF16) for kh in heads]
        rs = [_dot(jnp.concatenate([vtc[kh * HEAD_DIM:(kh + 1) * HEAD_DIM, :], ones], axis=0), es[kh])
              for kh in heads]
        for spec in mine[1:]:
            project(*spec)
        for kh in heads:
            den = rs[kh][HEAD_DIM:HEAD_DIM + 1, :] + jnp.exp(sink_ref[kh] - ms[kh])
            ot = (rs[kh][:HEAD_DIM, :] / den).astype(BF16)
            for j in range(KV_GROUP):
                ot_s[blk, j, kh * HEAD_DIM:(kh + 1) * HEAD_DIM, :] = ot[:, j * WINDOW:(j + 1) * WINDOW]
        for j in range(KV_GROUP):
            o_s[rows, j * KV_WIDTH:(j + 1) * KV_WIDTH] = ot_s[blk, j].T

    k_s[0:WINDOW, :] = k_s[TB:TB + WINDOW, :]
    vt_s[:, 0:WINDOW] = vt_s[:, TB:TB + WINDOW]

    def loader(c):
        rs = slice(c * POST_ROWS, (c + 1) * POST_ROWS)
        return lambda: (yssm_ref[rs, :], zs_s[rs, :], o_s[rs, :], za_s[rs, :], gs_s[rs, :], ga_s[rs, :], x_ref[rs, :])

    ys = _post([loader(c) for c in range(TB // POST_ROWS)], gate_ref[...],
               wglu_ref, bglu_ref, wbs_ref, wba_ref, wout_ref, lng_ref, lnb_ref)
    for c, yc in enumerate(ys):
        y_ref[c * POST_ROWS:(c + 1) * POST_ROWS, :] = yc

    @pl.when(i == pl.num_programs(1) - 1)
    def _():
        ktail_ref[...] = k_s[0:WINDOW, :].astype(F32)
        vtail_ref[...] = vt_s[:, 0:WINDOW].T.astype(F32)


def _main_prompt(rel_bias_flat, x2d, yssm, modp, wmain, sink_rows, wglu, bglu, wbs, wba, wout, lng, lnb,
                 n_batch, seq):
    steps = seq // TB
    row_spec = pl.BlockSpec((TB, D_MODEL), lambda b, i: (b * steps + i, 0))
    mod_spec = lambda j: pl.BlockSpec((None, 1, D_MODEL), lambda b, i, j=j: (b * 3 + j, 0, 0))
    tail_spec = pl.BlockSpec((None, WINDOW, KV_WIDTH), lambda b, i: (b, 0, 0))
    consts = (wmain, sink_rows, wglu, bglu, wbs, wba, wout, lng, lnb)
    return pl.pallas_call(
        _main_body,
        grid=(n_batch, steps),
        in_specs=[pl.BlockSpec(memory_space=pltpu.SMEM), row_spec, row_spec, mod_spec(0), mod_spec(1), mod_spec(2)]
        + [_const_spec(c.shape) for c in consts],
        out_specs=[row_spec, tail_spec, tail_spec],
        out_shape=[jax.ShapeDtypeStruct((n_batch * seq, D_MODEL), F32),
                   jax.ShapeDtypeStruct((n_batch, WINDOW, KV_WIDTH), F32),
                   jax.ShapeDtypeStruct((n_batch, WINDOW, KV_WIDTH), F32)],
        scratch_shapes=[pltpu.VMEM((TB, D_MODEL), BF16)] * 5
        + [pltpu.VMEM((TB + WINDOW, KV_WIDTH), BF16), pltpu.VMEM((KV_WIDTH, TB + WINDOW), BF16),
           pltpu.VMEM((NQB, KV_GROUP, KV_WIDTH, WINDOW), BF16), pltpu.VMEM((TB, D_MODEL), BF16),
           pltpu.VMEM((2, N_KV_HEADS, KV_GROUP, 2 * WINDOW, WINDOW), F32)],
        compiler_params=pltpu.CompilerParams(dimension_semantics=("arbitrary", "arbitrary"),
                                             vmem_limit_bytes=VMEM_LIMIT),
        name="main_prompt",
    )(rel_bias_flat, x2d, yssm, modp, modp, modp, *consts)


GT = 8


def _sample_ssm_body(u_ref, h0r_ref, h0i_ref, ar_ref, ai_ref, bre_ref, bim_ref, cblk_ref, d_ref,
                     y_ref, hr_ref, hi_ref):
    ub = u_ref[...]
    ys = []
    for a in range(GT):
        ua = ub[:, a * 128:(a + 1) * 128]
        st = slice(a * 512, (a + 1) * 512)
        xr = _dot(ua, bre_ref[a])
        xi = _dot(ua, bim_ref[a])
        h0r = h0r_ref[:, st]
        h0i = h0i_ref[:, st]
        ar = ar_ref[:, st]
        ai = ai_ref[:, st]
        hr = ar * h0r - ai * h0i + xr
        hi = ar * h0i + ai * h0r + xi
        hr_ref[:, st] = hr
        hi_ref[:, st] = hi
        hcat = jnp.concatenate([hr, hi], axis=1).astype(BF16)
        ys.append(_dot_nt(hcat, cblk_ref[a]))
    y_ref[...] = jnp.concatenate(ys, axis=1) + d_ref[...] * ub.astype(F32)


def _sample_ssm(proj_s, h0r, h0i, ar, ai, bre, bim, cblk, d_row):
    rows = proj_s.shape[0]
    ns = SSM_GROUPS * SSM_STATE
    full = lambda shape: pl.BlockSpec(shape, lambda i: (0,) * len(shape))
    return pl.pallas_call(
        _sample_ssm_body,
        grid=(1,),
        in_specs=[pl.BlockSpec((rows, D_MODEL), lambda i: (0, COL_U)),
                  full((rows, ns)), full((rows, ns)), full((1, ns)), full((1, ns)),
                  full(bre.shape), full(bim.shape), full(cblk.shape), full((1, D_MODEL))],
        out_specs=[full((rows, D_MODEL)), full((rows, ns)), full((rows, ns))],
        out_shape=[jax.ShapeDtypeStruct((rows, D_MODEL), F32),
                   jax.ShapeDtypeStruct((rows, ns), F32),
                   jax.ShapeDtypeStruct((rows, ns), F32)],
        compiler_params=pltpu.CompilerParams(vmem_limit_bytes=VMEM_LIMIT),
        name="sample_ssm",
    )(proj_s, h0r, h0i, ar, ai, bre, bim, cblk, d_row)


SB = 16
KROWS = WINDOW + 8
SEQ_GROUP = 4


def _sample_attn_body(q_ref, kn_ref, vn_ref, ck_ref, cv_ref, bias_ref, sink_ref, seg_ref, segt_ref,
                      o_ref, nk_ref, nv_ref):
    qf = q_ref[...].astype(F32)
    knf = kn_ref[...].astype(F32)
    vnf = vn_ref[...].astype(F32)
    rowk = lax.broadcasted_iota(jnp.int32, (KROWS, 128), 0)
    roww = lax.broadcasted_iota(jnp.int32, (WINDOW, KV_WIDTH), 0)
    sink = sink_ref[...]
    last = roww == WINDOW - 1
    for g0 in range(0, SB, SEQ_GROUP):
        seqs = range(g0, g0 + SEQ_GROUP)
        scores = []
        for b in seqs:
            kall = jnp.concatenate([ck_ref[b], jnp.broadcast_to(knf[b:b + 1], (8, KV_WIDTH))], axis=0)
            qrow = qf[b:b + 1]
            prod = jnp.concatenate([kall * qrow[:, j * KV_WIDTH:(j + 1) * KV_WIDTH] for j in range(KV_GROUP)], axis=1)
            s = _dot(prod.astype(BF16), seg_ref[...]) + bias_ref[...]
            scores.append(jnp.where(rowk <= WINDOW, s, NEG_INF))
        probs = []
        for s in scores:
            m = jnp.maximum(jnp.max(s, axis=0, keepdims=True), sink)
            e = jnp.exp(s - m)
            den = jnp.sum(e, axis=0, keepdims=True) + jnp.exp(sink - m)
            probs.append((e / den).astype(BF16))
        pexps = [_dot(p, segt_ref[...]) for p in probs]
        for b, pexp in zip(seqs, pexps):
            vb = cv_ref[b]
            vall = jnp.concatenate([vb, jnp.broadcast_to(vnf[b:b + 1], (8, KV_WIDTH))], axis=0)
            v4 = jnp.concatenate([vall] * KV_GROUP, axis=1)
            o_ref[b:b + 1, :] = jnp.sum(pexp * v4, axis=0, keepdims=True)
            nv_ref[b] = jnp.where(last, jnp.broadcast_to(vnf[b:b + 1], (WINDOW, KV_WIDTH)), pltpu.roll(vb, WINDOW - 1, 0))
            nk_ref[b] = jnp.where(last, jnp.broadcast_to(knf[b:b + 1], (WINDOW, KV_WIDTH)),
                                  pltpu.roll(ck_ref[b], WINDOW - 1, 0))


def _sample_attn(proj_s, ck, cv, bias_s, sink_row, seg, segt):
    rows = proj_s.shape[0]
    cache_spec = pl.BlockSpec((SB, WINDOW, KV_WIDTH), lambda i: (i, 0, 0))
    return pl.pallas_call(
        _sample_attn_body,
        grid=(rows // SB,),
        in_specs=[pl.BlockSpec((SB, D_MODEL), lambda i: (i, COL_Q)),
                  pl.BlockSpec((SB, KV_WIDTH), lambda i: (i, COL_K)),
                  pl.BlockSpec((SB, KV_WIDTH), lambda i: (i, COL_V)),
                  cache_spec, cache_spec,
                  pl.BlockSpec(bias_s.shape, lambda i: (0, 0)),
                  pl.BlockSpec(sink_row.shape, lambda i: (0, 0)),
                  pl.BlockSpec(seg.shape, lambda i: (0, 0)),
                  pl.BlockSpec(segt.shape, lambda i: (0, 0))],
        out_specs=[pl.BlockSpec((SB, D_MODEL), lambda i: (i, 0)), cache_spec, cache_spec],
        out_shape=[jax.ShapeDtypeStruct((rows, D_MODEL), F32),
                   jax.ShapeDtypeStruct(ck.shape, F32),
                   jax.ShapeDtypeStruct(cv.shape, F32)],
        compiler_params=pltpu.CompilerParams(vmem_limit_bytes=VMEM_LIMIT),
        name="sample_attn",
    )(proj_s, proj_s, proj_s, ck, cv, bias_s, sink_row, seg, segt)


def _sample_final_body(x_ref, yssm_ref, zs_ref, o_ref, za_ref, gs_ref, ga_ref, gate_ref,
                       wglu_ref, bglu_ref, wbs_ref, wba_ref, wout_ref, lng_ref, lnb_ref, y_ref):
    load = lambda: (yssm_ref[...], zs_ref[...], o_ref[...], za_ref[...], gs_ref[...], ga_ref[...], x_ref[...])
    y_ref[...] = _post([load], gate_ref[...], wglu_ref, bglu_ref, wbs_ref, wba_ref, wout_ref, lng_ref, lnb_ref)[0]


def _final_sample(x2d, yssm, o_a, proj_s, mods, wglu, bglu, wbs, wba, wout, lng, lnb):
    rows = x2d.shape[0]
    full = lambda shape: pl.BlockSpec(shape, lambda i: (0,) * len(shape))
    pcol = lambda c: pl.BlockSpec((rows, D_MODEL), lambda i, c=c: (0, c))
    return pl.pallas_call(
        _sample_final_body,
        grid=(1,),
        in_specs=[full((rows, D_MODEL)), full((rows, D_MODEL)), pcol(COL_ZS), full((rows, D_MODEL)),
                  pcol(COL_ZA), pcol(COL_GS), pcol(COL_GA),
                  pl.BlockSpec((rows, D_MODEL), lambda i: (0, 2)),
                  full(wglu.shape), full(bglu.shape), full(wbs.shape), full(wba.shape), full(wout.shape),
                  full(lng.shape), full(lnb.shape)],
        out_specs=full((rows, D_MODEL)),
        out_shape=jax.ShapeDtypeStruct((rows, D_MODEL), F32),
        compiler_params=pltpu.CompilerParams(vmem_limit_bytes=VMEM_LIMIT),
        name="final_sample",
    )(x2d, yssm, proj_s, o_a, proj_s, proj_s, proj_s, mods, wglu, bglu, wbs, wba, wout, lng, lnb)


def _bucket_table():
    max_exact = N_BUCKETS // 2
    dist = np.arange(WINDOW + 1)
    df = np.maximum(dist, 1).astype(np.float32)
    large = max_exact + (np.log(df / np.float32(max_exact)) / np.float32(math.log(WINDOW / max_exact))
                         * np.float32(N_BUCKETS - max_exact)).astype(np.int32)
    large = np.minimum(large, N_BUCKETS - 1)
    return np.where(dist < max_exact, dist, large)


def _dot3_nt(a, b):
    ah = a.astype(BF16)
    al = (a - ah.astype(F32)).astype(BF16)
    bh = b.astype(BF16)
    bl = (b - bh.astype(F32)).astype(BF16)
    return _dot_nt(ah, bh) + _dot_nt(ah, bl) + _dot_nt(al, bh)


def _ssm_prep_body(p1_ref, p2_ref, kf_ref, bt_ref, bts_ref, c2_ref, c2s_ref, d_ref,
                   lhs_ref, bpow_ref, cpow_ref, bre_ref, bim_ref, cblk_ref):
    lane = lax.broadcasted_iota(jnp.int32, (SSM_GROUP, SW), 1)
    left = lane < SSM_STATE
    sgn = jnp.where(left, 1.0, -1.0)
    row_t = lax.broadcasted_iota(jnp.int32, (CW, 128), 0)
    lane_t = lax.broadcasted_iota(jnp.int32, (CW, 128), 1)
    bre_ref[...] = jnp.zeros_like(bre_ref)
    bim_ref[...] = jnp.zeros_like(bim_ref)
    cblk_ref[...] = jnp.zeros_like(cblk_ref)
    for gl in range(GT):
        p1, p2 = p1_ref[gl], p2_ref[gl]
        k1, k2 = kf_ref[gl, 0:1, :], kf_ref[gl, 1:2, :]
        bt, bts = bt_ref[gl], bts_ref[gl]
        c2, c2s = c2_ref[gl], c2s_ref[gl]
        bb = k1 * bt + k2 * bts
        bbs = k1 * bts - k2 * bt
        ca = [sgn * (c2 * p1[t:t + 1, :] + c2s * p2[t:t + 1, :]) for t in range(CHUNK + 1)]
        cpow_ref[gl] = jnp.concatenate(ca[1:], axis=0).astype(BF16)
        bpow_ref[gl] = jnp.concatenate(
            [p1[CHUNK - 1 - s:CHUNK - s, :] * bb + p2[CHUNK - 1 - s:CHUNK - s, :] * bbs for s in range(CHUNK)],
            axis=0).astype(BF16)
        e = _dot3_nt(jnp.concatenate(ca[:CHUNK], axis=0), jnp.concatenate([bb] * CHUNK, axis=0))
        tiles = []
        for lt in range(CW // 128):
            cols = slice(lt * 128, (lt + 1) * 128)
            et = e[:, cols]
            blk = (lane_t + lt * 128) // SSM_GROUP
            at = jnp.where(row_t == lane_t + lt * 128, d_ref[gl, :, cols], 0.0)
            for s in range(lt * 128 // SSM_GROUP, (lt + 1) * 128 // SSM_GROUP):
                sh = et if s == 0 else jnp.concatenate(
                    [jnp.zeros((s * SSM_GROUP, 128), F32), et[:CW - s * SSM_GROUP, :]], axis=0)
                at = at + jnp.where(blk == s, sh, 0.0)
            tiles.append(at)
        lhs_ref[gl] = jnp.concatenate(tiles, axis=1).astype(BF16)
        rs = slice(gl * SSM_GROUP, (gl + 1) * SSM_GROUP)
        ts = slice((gl // 2) * SW, (gl // 2 + 1) * SW)
        ti = slice(GT * SSM_STATE + (gl // 2) * SW, GT * SSM_STATE + (gl // 2 + 1) * SW)
        if gl % 2 == 0:
            bre_ref[rs, ts] = jnp.where(left, bb, 0.0).astype(BF16)
            bim_ref[rs, ts] = jnp.where(left, bbs, 0.0).astype(BF16)
            cblk_ref[rs, ts] = jnp.where(left, c2, 0.0).astype(BF16)
            cblk_ref[rs, ti] = jnp.where(left, -c2s, 0.0).astype(BF16)
        else:
            bre_ref[rs, ts] = jnp.where(left, 0.0, bbs).astype(BF16)
            bim_ref[rs, ts] = jnp.where(left, 0.0, bb).astype(BF16)
            cblk_ref[rs, ts] = jnp.where(left, 0.0, c2s).astype(BF16)
            cblk_ref[rs, ti] = jnp.where(left, 0.0, -c2).astype(BF16)


def _ssm_params(lam_re, lam_im, log_delta, b_re, b_im, c_re, c_im, d_skip):
    g = SSM_GROUPS
    dup = lambda a: jnp.concatenate([a, a], axis=-1)
    lr, li = dup(lam_re.astype(F32)), dup(lam_im.astype(F32))
    dt = jnp.exp(log_delta.astype(F32))[:, None]
    half_sign = jnp.asarray(np.where(np.arange(SW) < SSM_STATE, -1.0, 1.0), F32)
    tau = jnp.asarray(POW_TAUS, F32)[None, :, None]
    mag = jnp.exp((lr * dt)[:, None, :] * tau)
    ang = (li * dt)[:, None, :] * tau
    p1 = mag * jnp.cos(ang)
    pim = mag * jnp.sin(ang)
    p2 = pim * half_sign
    ar, ai = p1[:, 1, :], pim[:, 1, :]
    den = lr * lr + li * li
    nr = ar - 1.0
    k1 = (nr * lr + ai * li) / den
    k2 = (ai * lr - nr * li) / den * half_sign
    kf = jnp.concatenate([k1[:, None, :], k2[:, None, :], jnp.zeros((g, 6, SW), F32)], axis=1)
    brt, bit = jnp.swapaxes(b_re, 1, 2), jnp.swapaxes(b_im, 1, 2)
    bt = jnp.concatenate([brt, bit], axis=-1)
    bts = jnp.concatenate([bit, brt], axis=-1)
    c2 = jnp.concatenate([c_re, c_im], axis=-1)
    c2s = jnp.concatenate([c_im, c_re], axis=-1)
    dtile = jnp.tile(d_skip.reshape(g, 1, SSM_GROUP), (1, 1, CHUNK))

    gspec = lambda r, w: pl.BlockSpec((GT, r, w), lambda a: (a, 0, 0))
    tile_spec = lambda w: pl.BlockSpec((None, GT * SSM_GROUP, w), lambda a: (a, 0, 0))
    nrow = len(POW_TAUS)
    lhs, bpow, cpow, bre, bim, cblk = pl.pallas_call(
        _ssm_prep_body,
        grid=(g // GT,),
        in_specs=[gspec(nrow, SW), gspec(nrow, SW), gspec(8, SW), gspec(SSM_GROUP, SW), gspec(SSM_GROUP, SW),
                  gspec(SSM_GROUP, SW), gspec(SSM_GROUP, SW), gspec(1, CW)],
        out_specs=[gspec(CW, CW), gspec(CW, SW), gspec(CW, SW),
                   tile_spec(GT * SSM_STATE), tile_spec(GT * SSM_STATE), tile_spec(2 * GT * SSM_STATE)],
        out_shape=[jax.ShapeDtypeStruct((g, CW, CW), BF16), jax.ShapeDtypeStruct((g, CW, SW), BF16),
                   jax.ShapeDtypeStruct((g, CW, SW), BF16),
                   jax.ShapeDtypeStruct((GT, GT * SSM_GROUP, GT * SSM_STATE), BF16),
                   jax.ShapeDtypeStruct((GT, GT * SSM_GROUP, GT * SSM_STATE), BF16),
                   jax.ShapeDtypeStruct((GT, GT * SSM_GROUP, 2 * GT * SSM_STATE), BF16)],
        name="ssm_prep",
    )(p1, p2, kf, bt, bts, c2, c2s, dtile)
    ar_row = ar[:, :SSM_STATE].reshape(1, -1)
    ai_row = ai[:, :SSM_STATE].reshape(1, -1)
    return lhs, bpow, cpow, p1, p2, ar_row, ai_row, bre, bim, cblk


def _attn_params(rel_bias, sinks):
    hp = lax.Precision.HIGHEST
    rb = rel_bias.astype(F32)
    onehot = jnp.asarray(_bucket_table()[:, None] == np.arange(N_BUCKETS)[None, :], F32)
    tbl = jnp.dot(onehot, rb, precision=hp)
    sink_rows = jnp.repeat(sinks.astype(F32).reshape(N_KV_HEADS, 1, KV_GROUP), WINDOW, axis=2)
    ds = np.clip(WINDOW - np.arange(KROWS), 0, WINDOW)
    bs = jnp.dot(jnp.asarray(ds[:, None] == np.arange(WINDOW + 1)[None, :], F32), tbl, precision=hp)
    bs = jnp.transpose(bs.reshape(KROWS, N_KV_HEADS, KV_GROUP), (0, 2, 1)).reshape(KROWS, N_HEADS)
    bias_s = jnp.pad(bs, ((0, 0), (0, 128 - N_HEADS)))
    sk = jnp.transpose(sinks.astype(F32).reshape(N_KV_HEADS, KV_GROUP), (1, 0)).reshape(1, N_HEADS)
    sink_row = jnp.pad(sk, ((0, 0), (0, 128 - N_HEADS)))
    seg_np = np.zeros((D_MODEL, 128), np.float32)
    for j in range(KV_GROUP):
        for kh in range(N_KV_HEADS):
            r0 = j * KV_WIDTH + kh * HEAD_DIM
            seg_np[r0:r0 + HEAD_DIM, j * N_KV_HEADS + kh] = 1.0
    seg = jnp.asarray(seg_np, BF16)
    segt = jnp.asarray(seg_np.T, BF16)
    return sink_rows, bias_s, sink_row, seg, segt


WP_ROWS = 256


def _regroup_tiles(w):
    left = lax.broadcasted_iota(jnp.int32, (w.shape[0], 128), 1) < HEAD_DIM
    tile = lambda t: w[:, t * 128:(t + 1) * 128]
    out = []
    for j in range(KV_GROUP):
        for pair in range(N_KV_HEADS // 2):
            t1 = tile((2 * pair) * 2 + j // 2)
            t2 = tile((2 * pair + 1) * 2 + j // 2)
            if j % 2 == 0:
                out.append(jnp.where(left, t1, pltpu.roll(t2, HEAD_DIM, 1)))
            else:
                out.append(jnp.where(left, pltpu.roll(t1, HEAD_DIM, 1), t2))
    return jnp.concatenate(out, axis=1)


def _weights_body(win_ref, wglu_ref, wbs_ref, wba_ref, wout_ref,
                  winb_ref, wut_ref, wglub_ref, wbsb_ref, wbab_ref, woutb_ref):
    i = pl.program_id(0)
    seg = lambda a: win_ref[:, a:a + D_MODEL]
    u = seg(0)
    winb_ref[:, 0:D_MODEL] = u.astype(BF16)
    winb_ref[:, D_MODEL:2 * D_MODEL] = seg(D_MODEL).astype(BF16)
    winb_ref[:, 2 * D_MODEL:3 * D_MODEL] = (_regroup_tiles(seg(2 * D_MODEL)) * (HEAD_DIM ** -0.5)).astype(BF16)
    winb_ref[:, 3 * D_MODEL:4 * D_MODEL] = _regroup_tiles(seg(3 * D_MODEL + 2 * KV_WIDTH)).astype(BF16)
    winb_ref[:, 4 * D_MODEL:5 * D_MODEL] = seg(4 * D_MODEL + 2 * KV_WIDTH).astype(BF16)
    winb_ref[:, 5 * D_MODEL:6 * D_MODEL] = seg(5 * D_MODEL + 2 * KV_WIDTH).astype(BF16)
    winb_ref[:, 6 * D_MODEL:6 * D_MODEL + 2 * KV_WIDTH] = \
        win_ref[:, 3 * D_MODEL:3 * D_MODEL + 2 * KV_WIDTH].astype(BF16)
    wut_ref[...] = u.astype(BF16).T
    wglub_ref[...] = wglu_ref[...].astype(BF16)
    wbsb_ref[...] = wbs_ref[...].astype(BF16)
    woutb_ref[...] = wout_ref[...].astype(BF16)
    wba = wba_ref[...].astype(BF16)
    for j in range(KV_GROUP):
        r0 = pl.multiple_of(j * KV_WIDTH + i * HEAD_DIM, HEAD_DIM)
        wbab_ref[pl.ds(r0, HEAD_DIM), :] = wba[j * HEAD_DIM:(j + 1) * HEAD_DIM, :]


def _weights_prep(w_in0, w_glu0, w_bs0, w_ba0, w_out0):
    assert WP_ROWS == KV_GROUP * HEAD_DIM
    rows = lambda w: pl.BlockSpec((WP_ROWS, w), lambda i: (i, 0))
    sq = jax.ShapeDtypeStruct((D_MODEL, D_MODEL), BF16)
    return pl.pallas_call(
        _weights_body,
        grid=(D_MODEL // WP_ROWS,),
        in_specs=[rows(D_IN), rows(D_MODEL), rows(D_MODEL), rows(D_MODEL), rows(D_MODEL)],
        out_specs=[rows(D_IN), pl.BlockSpec((D_MODEL, WP_ROWS), lambda i: (0, i)), rows(D_MODEL), rows(D_MODEL),
                   pl.BlockSpec((D_MODEL, D_MODEL), lambda i: (0, 0)), rows(D_MODEL)],
        out_shape=[jax.ShapeDtypeStruct((D_MODEL, D_IN), BF16), sq, sq, sq, sq, sq],
        compiler_params=pltpu.CompilerParams(dimension_semantics=("arbitrary",), vmem_limit_bytes=VMEM_LIMIT),
        name="weights_prep",
    )(w_in0, w_glu0, w_bs0, w_ba0, w_out0)


def kernel(x_prompt, x_sample, c_prompt, c_sample, state_ssm_re, state_ssm_im, cache_swa_k, cache_swa_v,
           w_ada, b_ada, w_in, ssm_lambda_re, ssm_lambda_im, ssm_log_delta, ssm_b_re, ssm_b_im,
           ssm_c_re, ssm_c_im, ssm_d, w_glu, b_glu, attn_sinks, rel_bias, w_branch_s, w_branch_a,
           w_out, ln_g, ln_b):
    assert w_ada.shape[0] == 1, "single-layer trunk"
    n_batch, seq, _ = x_prompt.shape
    n_dec = x_sample.shape[0]

    w_in_b, wut, wglu, wbs, wba, wout = _weights_prep(w_in[0], w_glu[0], w_branch_s[0], w_branch_a[0], w_out[0])
    bglu = b_glu[0].reshape(1, -1).astype(F32)
    lng = ln_g[0].reshape(1, -1).astype(F32)
    lnb = ln_b[0].reshape(1, -1).astype(F32)
    lhs, bpow, cpow, p1, p2, ar_row, ai_row, bre, bim, cblk = _ssm_params(
        ssm_lambda_re[0], ssm_lambda_im[0], ssm_log_delta[0], ssm_b_re[0], ssm_b_im[0],
        ssm_c_re[0], ssm_c_im[0], ssm_d[0])
    sink_rows, bias_s, sink_row, seg, segt = _attn_params(rel_bias, attn_sinks[0])

    c_all = jnp.concatenate([c_prompt, jnp.zeros((8 - n_batch, D_MODEL), F32), c_sample], axis=0)
    mod = _ada(c_all, w_ada[0], b_ada[0])
    modp = mod[:n_batch].reshape(n_batch * 3, 1, D_MODEL)
    mods = mod[8:8 + n_dec]

    xp = x_prompt.reshape(n_batch * seq, D_MODEL)
    yssm, hfin = _ssm_prompt(x_prompt, modp, wut, lhs, bpow, cpow, p1, p2)
    yp, ktail, vtail = _main_prompt(rel_bias.astype(F32).reshape(-1), xp, yssm, modp, w_in_b, sink_rows,
                                    wglu, bglu, wbs, wba, wout, lng, lnb, n_batch, seq)
    yp = yp.reshape(n_batch, seq, D_MODEL)
    p_hr = hfin[:, :, 0, :SSM_STATE][None]
    p_hi = hfin[:, :, 0, SSM_STATE:][None]
    p_k = ktail.reshape(1, n_batch, WINDOW, N_KV_HEADS, HEAD_DIM)
    p_v = vtail.reshape(1, n_batch, WINDOW, N_KV_HEADS, HEAD_DIM)

    xs = x_sample.reshape(n_dec, D_MODEL)
    proj_s = _proj(xs, mods, mods,
                   pl.BlockSpec((n_dec, D_MODEL), lambda i: (0, 0)),
                   pl.BlockSpec((n_dec, D_MODEL), lambda i: (0, 1)),
                   w_in_b, n_dec, D_IN)
    ns = SSM_GROUPS * SSM_STATE
    ys_s, s_hr, s_hi = _sample_ssm(proj_s, state_ssm_re[0].reshape(n_dec, ns), state_ssm_im[0].reshape(n_dec, ns),
                                   ar_row, ai_row, bre, bim, cblk, ssm_d[0].reshape(1, -1).astype(F32))
    o_s, nk, nv = _sample_attn(proj_s, cache_swa_k[0].reshape(n_dec, WINDOW, KV_WIDTH),
                               cache_swa_v[0].reshape(n_dec, WINDOW, KV_WIDTH), bias_s, sink_row, seg, segt)
    ysmp = _final_sample(xs, ys_s, o_s, proj_s, mods, wglu, bglu, wbs, wba, wout, lng, lnb)

    return (yp, ysmp.reshape(n_dec, 1, D_MODEL),
            p_hr, p_hi, p_k, p_v,
            s_hr.reshape(1, n_dec, SSM_GROUPS, SSM_STATE), s_hi.reshape(1, n_dec, SSM_GROUPS, SSM_STATE),
            nk.reshape(1, n_dec, WINDOW, N_KV_HEADS, HEAD_DIM), nv.reshape(1, n_dec, WINDOW, N_KV_HEADS, HEAD_DIM))
```

```python
import math

import numpy as np
import jax
import jax.numpy as jnp
from jax import lax
from jax.experimental import pallas as pl
from jax.experimental.pallas import tpu as pltpu

F32 = jnp.float32
BF16 = jnp.bfloat16

D_MODEL = 1024
SSM_GROUPS = 64
SSM_GROUP = 16
SSM_STATE = 64
N_HEADS = 16
HEAD_DIM = 64
N_KV_HEADS = 4
KV_GROUP = 4
KV_WIDTH = N_KV_HEADS * HEAD_DIM
WINDOW = 128
N_BUCKETS = 32
NEG_INF = -1e30
LN_EPS = 1e-5
DEPTH = 1
DEEPNORM_ALPHA = (2 * DEPTH) ** 0.25
D_IN = 6656
CHUNK = 16
CW = CHUNK * SSM_GROUP
SW = 2 * SSM_STATE

COL_U, COL_ZS, COL_Q, COL_ZA, COL_GS, COL_GA = 0, 1, 2, 3, 4, 5
COL_K, COL_V = 24, 25

VMEM_LIMIT = 56 * 1024 * 1024
VMEM_LIMIT_SSM = 60 * 1024 * 1024


def _sigmoid(x):
    return 0.5 * jnp.tanh(0.5 * x) + 0.5


def _silu(x):
    return x * _sigmoid(x)


def _gelu_tanh(x):
    c = math.sqrt(2.0 / math.pi)
    hx = 0.5 * x
    return hx + hx * jnp.tanh(x * (c + (c * 0.044715) * (x * x)))


def _dot(a, b):
    return jnp.dot(a, b, preferred_element_type=F32)


def _dot_nt(a, b):
    return lax.dot_general(a, b, (((1,), (1,)), ((), ())), preferred_element_type=F32)


def _ada_body(c_ref, w_ref, b_ref, o_ref):
    c = c_ref[...]
    sc = _silu(c).astype(BF16)
    o_ref[...] = _dot(sc, w_ref[...].astype(BF16)) + b_ref[...]


def _ada(c_all, w_ada, b_ada):
    rows = c_all.shape[0]
    bn = 512
    return pl.pallas_call(
        _ada_body,
        grid=(3 * D_MODEL // bn,),
        in_specs=[pl.BlockSpec((rows, D_MODEL), lambda j: (0, 0)),
                  pl.BlockSpec((D_MODEL, bn), lambda j: (0, j)),
                  pl.BlockSpec((1, bn), lambda j: (0, j))],
        out_specs=pl.BlockSpec((rows, bn), lambda j: (0, j)),
        out_shape=jax.ShapeDtypeStruct((rows, 3 * D_MODEL), F32),
        name="ada",
    )(c_all, w_ada, b_ada.reshape(1, -1))


PROJ_CB = 512


def _proj_body(x_ref, shift_ref, scale_ref, w_ref, o_ref):
    h = x_ref[...] * (1.0 + scale_ref[...]) + shift_ref[...]
    hb = h.astype(BF16)
    for j in range(D_IN // PROJ_CB):
        sl = slice(j * PROJ_CB, (j + 1) * PROJ_CB)
        o_ref[:, sl] = _dot(hb, w_ref[:, sl]).astype(BF16)


def _proj_sample(xs, mods, w_in_b):
    rows = xs.shape[0]
    return pl.pallas_call(
        _proj_body,
        grid=(1,),
        in_specs=[pl.BlockSpec((rows, D_MODEL), lambda i: (0, 0)),
                  pl.BlockSpec((rows, D_MODEL), lambda i: (0, 0)),
                  pl.BlockSpec((rows, D_MODEL), lambda i: (0, 1)),
                  pl.BlockSpec((D_MODEL, D_IN), lambda i: (0, 0))],
        out_specs=pl.BlockSpec((rows, D_IN), lambda i: (0, 0)),
        out_shape=jax.ShapeDtypeStruct((rows, D_IN), BF16),
        compiler_params=pltpu.CompilerParams(vmem_limit_bytes=VMEM_LIMIT),
        name="proj",
    )(xs, mods, mods, w_in_b)


NKS = 128
GC = 16


def _cmul(c1, c2, x):
    return c1 * x + c2 * pltpu.roll(x, SSM_STATE, x.ndim - 1)


SSM_TOK = NKS * CHUNK
SUB = 8
NPH = CHUNK // SUB


def _dot_tn(a, b):
    return lax.dot_general(a, b, (((0,), (0,)), ((), ())), preferred_element_type=F32)


SEG = 8
SEG_LEN = NKS // SEG
POW_TAUS = (tuple(range(CHUNK + 1)) + tuple(CHUNK * m for m in range(2, SEG_LEN))
            + tuple(CHUNK * SEG_LEN * m for m in range(1, SEG + 1)) + (0,))
ROW_CHUNK1 = CHUNK
ROW_SEG1 = CHUNK + SEG_LEN - 1


def _ssm_body(x_ref, shift_ref, scale_ref, wut_ref, lhs_ref, bpow_ref, cpow_ref, p1_ref, p2_ref,
              y_ref, hfin_ref, r_scr, yt_scr, s_scr, carry_scr):
    i = pl.program_id(1)
    j = pl.program_id(2)

    @pl.when((i == 0) & (j == 0))
    def _():
        carry_scr[...] = jnp.zeros_like(carry_scr)

    @pl.when(j < NPH)
    def _():
        pairs = range(SUB // 2)
        hbs = []
        for sp in pairs:
            h = jnp.concatenate([x_ref[:, b, s, :] for s in (2 * sp, 2 * sp + 1) for b in range(SEG_LEN)],
                                axis=0)
            hbs.append((h * (1.0 + scale_ref[...]) + shift_ref[...]).astype(BF16))
        uts = [_dot_nt(wut_ref[...], hb).astype(BF16) for hb in hbs]
        for sp, ut in zip(pairs, uts):
            ut = ut.reshape(SSM_GROUPS, SSM_GROUP, 2 * NKS)
            pair = jnp.concatenate([ut[:, :, :NKS], ut[:, :, NKS:]], axis=1)
            r0 = pl.multiple_of(j * SUB * SSM_GROUP + sp * 2 * SSM_GROUP, 2 * SSM_GROUP)
            r_scr[:, pl.ds(r0, 2 * SSM_GROUP), :] = pair

    @pl.when(j == NPH - 1)
    def _():
        def local(g):
            rg = r_scr[g]
            yt_scr[g] = _dot(lhs_ref[g], rg)
            s_scr[g] = _dot_tn(rg, bpow_ref[g])

        def carried(g):
            yt_scr[g] += _dot_nt(cpow_ref[g], s_scr[g].astype(BF16))

        row = lax.broadcasted_iota(jnp.int32, (GC, SEG, SW), 1)
        seg_rows = lambda b: slice(b * SEG, (b + 1) * SEG)

        def scan(gc):
            gs = slice(gc * GC, (gc + 1) * GC)
            p1, p2 = p1_ref[gs], p2_ref[gs]
            c1, c2 = p1[:, ROW_CHUNK1:ROW_CHUNK1 + 1, :], p2[:, ROW_CHUNK1:ROW_CHUNK1 + 1, :]
            h = jnp.zeros((GC, SEG, SW), F32)
            hs = h
            for b in range(SEG_LEN):
                sb = s_scr[gs, seg_rows(b), :]
                h, hs = sb + c1 * h + c2 * hs, pltpu.roll(sb, SSM_STATE, 2) + c1 * hs - c2 * h
                s_scr[gs, seg_rows(b), :] = h
                yield
            e = h
            for d in (1, 2, 4):
                rd = ROW_SEG1 + d - 1
                sh = jnp.where(row >= d, pltpu.roll(e, d, 1), 0.0)
                e = e + _cmul(p1[:, rd:rd + 1, :], p2[:, rd:rd + 1, :], sh)
            carry = carry_scr[gs]
            e = e + _cmul(p1[:, ROW_SEG1:ROW_SEG1 + SEG, :], p2[:, ROW_SEG1:ROW_SEG1 + SEG, :], carry)
            ein = jnp.where(row == 0, carry, pltpu.roll(e, 1, 1))
            eins = pltpu.roll(ein, SSM_STATE, 2)
            carry_scr[gs] = jnp.broadcast_to(e[:, SEG - 1:SEG, :], (GC, SEG, SW))
            yield
            for b in range(SEG_LEN - 1, 0, -1):
                rb = ROW_CHUNK1 + b - 1
                s_scr[gs, seg_rows(b), :] = (s_scr[gs, seg_rows(b - 1), :]
                                             + p1[:, rb:rb + 1, :] * ein + p2[:, rb:rb + 1, :] * eins)
                yield
            s_scr[gs, seg_rows(0), :] = ein

        def interleave(fn, groups, gen):
            for g in groups:
                fn(g)
                for _ in range(2):
                    next(gen, None)
            for _ in gen:
                pass

        n_sets = SSM_GROUPS // GC
        sets = [range(q * GC, (q + 1) * GC) for q in range(n_sets)]
        for g in sets[0]:
            local(g)
        for q in range(1, n_sets):
            interleave(local, sets[q], scan(q - 1))
        interleave(carried, sets[0], scan(n_sets - 1))
        for q in range(1, n_sets):
            for g in sets[q]:
                carried(g)
        hfin_ref[...] = carry_scr[...]

    @pl.when(j >= NPH)
    def _():
        for sp in range(SUB):
            t0 = pl.multiple_of((j - NPH) * SUB * SSM_GROUP + sp * SSM_GROUP, SSM_GROUP)
            yt = yt_scr[:, pl.ds(t0, SSM_GROUP), :].reshape(D_MODEL, NKS)
            yr = yt.T
            for b in range(SEG_LEN):
                y_ref[:, b, sp, :] = yr[b * SEG:(b + 1) * SEG, :]


def _ssm_prompt(x_prompt, modp, wut, lhs, bpow, cpow, p1, p2):
    n_batch, seq, _ = x_prompt.shape
    steps = seq // SSM_TOK
    xv = x_prompt.reshape(n_batch * steps * SEG, SEG_LEN, CHUNK, D_MODEL)
    mod_spec = lambda k: pl.BlockSpec((None, 1, D_MODEL), lambda b, i, j, k=k: (b * 3 + k, 0, 0))
    consts = (wut, lhs, bpow, cpow, p1, p2)
    y, hfin = pl.pallas_call(
        _ssm_body,
        grid=(n_batch, steps, 2 * NPH),
        in_specs=[pl.BlockSpec((SEG, SEG_LEN, SUB, D_MODEL),
                               lambda b, i, j: (b * steps + i, 0, jnp.minimum(j, NPH - 1), 0)),
                  mod_spec(0), mod_spec(1)] + [_const_spec(c.shape) for c in consts],
        out_specs=[pl.BlockSpec((SEG, SEG_LEN, SUB, D_MODEL),
                                lambda b, i, j: (b * steps + i, 0, jnp.maximum(j - NPH, 0), 0)),
                   pl.BlockSpec((None, SSM_GROUPS, 8, SW), lambda b, i, j: (b, 0, 0, 0))],
        out_shape=[jax.ShapeDtypeStruct(xv.shape, F32),
                   jax.ShapeDtypeStruct((n_batch, SSM_GROUPS, 8, SW), F32)],
        scratch_shapes=[pltpu.VMEM((SSM_GROUPS, CW, NKS), BF16), pltpu.VMEM((SSM_GROUPS, CW, NKS), F32),
                        pltpu.VMEM((SSM_GROUPS, NKS, SW), F32), pltpu.VMEM((SSM_GROUPS, 8, SW), F32)],
        compiler_params=pltpu.CompilerParams(dimension_semantics=("arbitrary", "arbitrary", "arbitrary"),
                                             vmem_limit_bytes=VMEM_LIMIT_SSM),
        name="ssm_prompt",
    )(xv, modp, modp, *consts)
    return y.reshape(n_batch * seq, D_MODEL), hfin


def _post(chunks, gate, wglu_ref, bglu_ref, wbs_ref, wba_ref, wout_ref, lng_ref, lnb_ref):
    data = [load() for load in chunks]
    ys = [_gelu_tanh(d[0]) for d in data]
    glu = [y * _sigmoid(_dot(y.astype(BF16), wglu_ref[...]) + bglu_ref[...]) for y in ys]
    b_s = [_dot(g.astype(BF16) * _silu(d[1]), wbs_ref[...]) for g, d in zip(glu, data)]
    b_a = [_dot(d[2].astype(BF16) * _silu(d[3]), wba_ref[...]) for d in data]
    m = [_sigmoid(d[4]).astype(F32) * s + _sigmoid(d[5]).astype(F32) * a for d, s, a in zip(data, b_s, b_a)]
    out = [_dot(v.astype(BF16), wout_ref[...]) for v in m]
    res = []
    for d, o in zip(data, out):
        r = DEEPNORM_ALPHA * d[6] + gate * o
        mu = jnp.mean(r, axis=-1, keepdims=True)
        rc = r - mu
        var = jnp.mean(rc * rc, axis=-1, keepdims=True)
        res.append(rc * lax.rsqrt(var + LN_EPS) * lng_ref[...] + lnb_ref[...])
    return res


TB = 512
NQB = TB // WINDOW
W_OFF = D_MODEL
ONES_ROWS = 16
POST_ROWS = 256
BLK_GROUP = 1
PROJ_W = 256


def _const_spec(shape):
    nd = len(shape)
    return pl.BlockSpec(shape, lambda *_: (0,) * nd, pipeline_mode=pl.Buffered(1))


def _bucket_thresholds():
    tbl = _bucket_table()
    return [int(np.argmax(tbl >= k)) for k in range(1, N_BUCKETS)]


def _fill_bias_table(rb_ref, bias_s):
    si = lax.broadcasted_iota(jnp.int32, (2 * WINDOW, WINDOW), 0)
    qi = lax.broadcasted_iota(jnp.int32, (2 * WINDOW, WINDOW), 1)
    dist = qi + WINDOW - si
    bid = jnp.zeros((2 * WINDOW, WINDOW), jnp.int32)
    for thr in _bucket_thresholds():
        bid = bid + (dist >= thr).astype(jnp.int32)
    valid = (dist >= 0) & (dist <= WINDOW)
    own = si >= WINDOW

    def per_head(h, c):
        t = jnp.zeros((2 * WINDOW, WINDOW), F32)
        for b in range(N_BUCKETS):
            t = jnp.where(bid == b, rb_ref[b * N_HEADS + h], t)
        t = jnp.where(valid, t, NEG_INF)
        kh, j = h // KV_GROUP, h % KV_GROUP
        bias_s[0, kh, j] = t
        bias_s[1, kh, j] = jnp.where(own, t, NEG_INF)
        return c

    lax.fori_loop(0, N_HEADS, per_head, 0)


def _main_body(rb_ref, x_ref, yssm_ref, shift_ref, scale_ref, gate_ref, wmain_ref, sink_ref,
               wglu_ref, bglu_ref, wbs_ref, wba_ref, wout_ref, lng_ref, lnb_ref,
               y_ref, ktail_ref, vtail_ref,
               zs_s, q_s, za_s, gs_s, ga_s, k_s, vt_s, ot_s, o_s, bias_s):
    i = pl.program_id(1)

    @pl.when((pl.program_id(0) == 0) & (i == 0))
    def _():
        _fill_bias_table(rb_ref, bias_s)

    @pl.when(i == 0)
    def _():
        k_s[0:WINDOW, :] = jnp.zeros((WINDOW, KV_WIDTH), BF16)
        vt_s[:, 0:WINDOW] = jnp.zeros((KV_WIDTH, WINDOW), BF16)

    x = x_ref[...]
    hb = (x * (1.0 + scale_ref[...]) + shift_ref[...]).astype(BF16)

    def project(dst, c0, c, width=PROJ_W):
        lo = W_OFF + c0 + c * width
        dst[:, c * width:(c + 1) * width] = _dot(hb, wmain_ref[:, lo:lo + width]).astype(BF16)

    for c in range(D_MODEL // 512):
        project(q_s, D_MODEL, c, 512)
    deferred = [(dst, c0, c) for dst, c0 in ((zs_s, 0), (za_s, 2 * D_MODEL), (gs_s, 3 * D_MODEL), (ga_s, 4 * D_MODEL))
                for c in range(D_MODEL // PROJ_W)]

    kv = _dot(hb, wmain_ref[:, W_OFF + 5 * D_MODEL:W_OFF + 5 * D_MODEL + 2 * KV_WIDTH]).astype(BF16)
    k_s[WINDOW:WINDOW + TB, :] = kv[:, :KV_WIDTH]
    vt_s[:, WINDOW:WINDOW + TB] = kv[:, KV_WIDTH:].T

    first = jnp.where(i == 0, 1, 0)
    lane = lax.broadcasted_iota(jnp.int32, (2 * WINDOW, 128), 1)
    ones = jnp.ones((ONES_ROWS, 2 * WINDOW), BF16)
    n_def = len(deferred) * BLK_GROUP // NQB
    for grp in range(NQB // BLK_GROUP):
        units = [(blk, kh) for blk in range(grp * BLK_GROUP, (grp + 1) * BLK_GROUP) for kh in range(N_KV_HEADS)]
        mine = deferred[grp * n_def:(grp + 1) * n_def]
        sts = []
        for blk, kh in units:
            rows = slice(blk * WINDOW, (blk + 1) * WINDOW)
            kcat = k_s[blk * WINDOW:blk * WINDOW + 2 * WINDOW, :]
            pair, odd = kh // 2, kh % 2
            kcm = jnp.where((lane >= HEAD_DIM) == bool(odd), kcat[:, pair * 128:(pair + 1) * 128], jnp.zeros((), BF16))
            qp = jnp.concatenate(
                [q_s[rows, j * KV_WIDTH + pair * 128:j * KV_WIDTH + (pair + 1) * 128] for j in range(KV_GROUP)],
                axis=0)
            sel = first if blk == 0 else 0
            bias = jnp.concatenate([bias_s[sel, kh, j] for j in range(KV_GROUP)], axis=1)
            sts.append(_dot_nt(kcm, qp) + bias)
        share = [mine[k * len(mine) // 3:(k + 1) * len(mine) // 3] for k in range(3)]
        ms = [jnp.maximum(jnp.max(st, axis=0, keepdims=True), sink_ref[kh]) for st, (_, kh) in zip(sts, units)]
        for spec in share[0]:
            project(*spec)
        es = [jnp.exp(st - m).astype(BF16) for st, m in zip(sts, ms)]
        for spec in share[1]:
            project(*spec)
        rs = []
        for e, (blk, kh) in zip(es, units):
            vtc = vt_s[kh * HEAD_DIM:(kh + 1) * HEAD_DIM, blk * WINDOW:blk * WINDOW + 2 * WINDOW]
            rs.append(_dot(jnp.concatenate([vtc, ones], axis=0), e))
        for spec in share[2]:
            project(*spec)
        for r, m, (blk, kh) in zip(rs, ms, units):
            den = r[HEAD_DIM:HEAD_DIM + 1, :] + jnp.exp(sink_ref[kh] - m)
            ot = (r[:HEAD_DIM, :] / den).astype(BF16)
            for j in range(KV_GROUP):
                ot_s[blk, j, kh * HEAD_DIM:(kh + 1) * HEAD_DIM, :] = ot[:, j * WINDOW:(j + 1) * WINDOW]
        for blk in range(grp * BLK_GROUP, (grp + 1) * BLK_GROUP):
            for j in range(KV_GROUP):
                o_s[blk * WINDOW:(blk + 1) * WINDOW, j * KV_WIDTH:(j + 1) * KV_WIDTH] = ot_s[blk, j].T

    k_s[0:WINDOW, :] = k_s[TB:TB + WINDOW, :]
    vt_s[:, 0:WINDOW] = vt_s[:, TB:TB + WINDOW]

    def loader(c):
        rs = slice(c * POST_ROWS, (c + 1) * POST_ROWS)
        return lambda: (yssm_ref[rs, :], zs_s[rs, :], o_s[rs, :], za_s[rs, :], gs_s[rs, :], ga_s[rs, :], x_ref[rs, :])

    ys = _post([loader(c) for c in range(TB // POST_ROWS)], gate_ref[...],
               wglu_ref, bglu_ref, wbs_ref, wba_ref, wout_ref, lng_ref, lnb_ref)
    for c, yc in enumerate(ys):
        y_ref[c * POST_ROWS:(c + 1) * POST_ROWS, :] = yc

    @pl.when(i == pl.num_programs(1) - 1)
    def _():
        ktail_ref[...] = k_s[0:WINDOW, :].astype(F32)
        vtail_ref[...] = vt_s[:, 0:WINDOW].T.astype(F32)


def _main_prompt(rel_bias_flat, x2d, yssm, modp, wmain, sink_rows, wglu, bglu, wbs, wba, wout, lng, lnb,
                 n_batch, seq):
    steps = seq // TB
    row_spec = pl.BlockSpec((TB, D_MODEL), lambda b, i: (b * steps + i, 0))
    mod_spec = lambda j: pl.BlockSpec((None, 1, D_MODEL), lambda b, i, j=j: (b * 3 + j, 0, 0))
    tail_spec = pl.BlockSpec((None, WINDOW, KV_WIDTH), lambda b, i: (b, 0, 0))
    consts = (wmain, sink_rows, wglu, bglu, wbs, wba, wout, lng, lnb)
    return pl.pallas_call(
        _main_body,
        grid=(n_batch, steps),
        in_specs=[pl.BlockSpec(memory_space=pltpu.SMEM), row_spec, row_spec, mod_spec(0), mod_spec(1), mod_spec(2)]
        + [_const_spec(c.shape) for c in consts],
        out_specs=[row_spec, tail_spec, tail_spec],
        out_shape=[jax.ShapeDtypeStruct((n_batch * seq, D_MODEL), F32),
                   jax.ShapeDtypeStruct((n_batch, WINDOW, KV_WIDTH), F32),
                   jax.ShapeDtypeStruct((n_batch, WINDOW, KV_WIDTH), F32)],
        scratch_shapes=[pltpu.VMEM((TB, D_MODEL), BF16)] * 5
        + [pltpu.VMEM((TB + WINDOW, KV_WIDTH), BF16), pltpu.VMEM((KV_WIDTH, TB + WINDOW), BF16),
           pltpu.VMEM((NQB, KV_GROUP, KV_WIDTH, WINDOW), BF16), pltpu.VMEM((TB, D_MODEL), BF16),
           pltpu.VMEM((2, N_KV_HEADS, KV_GROUP, 2 * WINDOW, WINDOW), F32)],
        compiler_params=pltpu.CompilerParams(dimension_semantics=("arbitrary", "arbitrary"),
                                             vmem_limit_bytes=VMEM_LIMIT),
        name="main_prompt",
    )(rel_bias_flat, x2d, yssm, modp, modp, modp, *consts)


GT = 8


def _sample_ssm_body(u_ref, h0r_ref, h0i_ref, ar_ref, ai_ref, bre_ref, bim_ref, cblk_ref, d_ref,
                     y_ref, hr_ref, hi_ref):
    ub = u_ref[...]
    ys = []
    for a in range(GT):
        ua = ub[:, a * 128:(a + 1) * 128]
        st = slice(a * 512, (a + 1) * 512)
        xr = _dot(ua, bre_ref[a])
        xi = _dot(ua, bim_ref[a])
        h0r = h0r_ref[:, st]
        h0i = h0i_ref[:, st]
        ar = ar_ref[:, st]
        ai = ai_ref[:, st]
        hr = ar * h0r - ai * h0i + xr
        hi = ar * h0i + ai * h0r + xi
        hr_ref[:, st] = hr
        hi_ref[:, st] = hi
        hcat = jnp.concatenate([hr, hi], axis=1).astype(BF16)
        ys.append(_dot_nt(hcat, cblk_ref[a]))
    y_ref[...] = jnp.concatenate(ys, axis=1) + d_ref[...] * ub.astype(F32)


def _sample_ssm(proj_s, h0r, h0i, ar, ai, bre, bim, cblk, d_row):
    rows = proj_s.shape[0]
    ns = SSM_GROUPS * SSM_STATE
    full = lambda shape: pl.BlockSpec(shape, lambda i: (0,) * len(shape))
    return pl.pallas_call(
        _sample_ssm_body,
        grid=(1,),
        in_specs=[pl.BlockSpec((rows, D_MODEL), lambda i: (0, COL_U)),
                  full((rows, ns)), full((rows, ns)), full((1, ns)), full((1, ns)),
                  full(bre.shape), full(bim.shape), full(cblk.shape), full((1, D_MODEL))],
        out_specs=[full((rows, D_MODEL)), full((rows, ns)), full((rows, ns))],
        out_shape=[jax.ShapeDtypeStruct((rows, D_MODEL), F32),
                   jax.ShapeDtypeStruct((rows, ns), F32),
                   jax.ShapeDtypeStruct((rows, ns), F32)],
        compiler_params=pltpu.CompilerParams(vmem_limit_bytes=VMEM_LIMIT),
        name="sample_ssm",
    )(proj_s, h0r, h0i, ar, ai, bre, bim, cblk, d_row)


SB = 16
KROWS = WINDOW + 8
SEQ_GROUP = 4


def _sample_attn_body(q_ref, kn_ref, vn_ref, ck_ref, cv_ref, bias_ref, sink_ref, seg_ref, segt_ref,
                      o_ref, nk_ref, nv_ref):
    qf = q_ref[...].astype(F32)
    knf = kn_ref[...].astype(F32)
    vnf = vn_ref[...].astype(F32)
    rowk = lax.broadcasted_iota(jnp.int32, (KROWS, 128), 0)
    roww = lax.broadcasted_iota(jnp.int32, (WINDOW, KV_WIDTH), 0)
    sink = sink_ref[...]
    last = roww == WINDOW - 1
    for g0 in range(0, SB, SEQ_GROUP):
        seqs = range(g0, g0 + SEQ_GROUP)
        scores = []
        for b in seqs:
            kall = jnp.concatenate([ck_ref[b], jnp.broadcast_to(knf[b:b + 1], (8, KV_WIDTH))], axis=0)
            qrow = qf[b:b + 1]
            prod = jnp.concatenate([kall * qrow[:, j * KV_WIDTH:(j + 1) * KV_WIDTH] for j in range(KV_GROUP)], axis=1)
            s = _dot(prod.astype(BF16), seg_ref[...]) + bias_ref[...]
            scores.append(jnp.where(rowk <= WINDOW, s, NEG_INF))
        probs = []
        for s in scores:
            m = jnp.maximum(jnp.max(s, axis=0, keepdims=True), sink)
            e = jnp.exp(s - m)
            den = jnp.sum(e, axis=0, keepdims=True) + jnp.exp(sink - m)
            probs.append((e / den).astype(BF16))
        pexps = [_dot(p, segt_ref[...]) for p in probs]
        for b, pexp in zip(seqs, pexps):
            vb = cv_ref[b]
            vall = jnp.concatenate([vb, jnp.broadcast_to(vnf[b:b + 1], (8, KV_WIDTH))], axis=0)
            v4 = jnp.concatenate([vall] * KV_GROUP, axis=1)
            o_ref[b:b + 1, :] = jnp.sum(pexp * v4, axis=0, keepdims=True)
            nv_ref[b] = jnp.where(last, jnp.broadcast_to(vnf[b:b + 1], (WINDOW, KV_WIDTH)), pltpu.roll(vb, WINDOW - 1, 0))
            nk_ref[b] = jnp.where(last, jnp.broadcast_to(knf[b:b + 1], (WINDOW, KV_WIDTH)),
                                  pltpu.roll(ck_ref[b], WINDOW - 1, 0))


def _sample_attn(proj_s, ck, cv, bias_s, sink_row, seg, segt):
    rows = proj_s.shape[0]
    cache_spec = pl.BlockSpec((SB, WINDOW, KV_WIDTH), lambda i: (i, 0, 0))
    return pl.pallas_call(
        _sample_attn_body,
        grid=(rows // SB,),
        in_specs=[pl.BlockSpec((SB, D_MODEL), lambda i: (i, COL_Q)),
                  pl.BlockSpec((SB, KV_WIDTH), lambda i: (i, COL_K)),
                  pl.BlockSpec((SB, KV_WIDTH), lambda i: (i, COL_V)),
                  cache_spec, cache_spec,
                  pl.BlockSpec(bias_s.shape, lambda i: (0, 0)),
                  pl.BlockSpec(sink_row.shape, lambda i: (0, 0)),
                  pl.BlockSpec(seg.shape, lambda i: (0, 0)),
                  pl.BlockSpec(segt.shape, lambda i: (0, 0))],
        out_specs=[pl.BlockSpec((SB, D_MODEL), lambda i: (i, 0)), cache_spec, cache_spec],
        out_shape=[jax.ShapeDtypeStruct((rows, D_MODEL), F32),
                   jax.ShapeDtypeStruct(ck.shape, F32),
                   jax.ShapeDtypeStruct(cv.shape, F32)],
        compiler_params=pltpu.CompilerParams(vmem_limit_bytes=VMEM_LIMIT),
        name="sample_attn",
    )(proj_s, proj_s, proj_s, ck, cv, bias_s, sink_row, seg, segt)


def _sample_final_body(x_ref, yssm_ref, zs_ref, o_ref, za_ref, gs_ref, ga_ref, gate_ref,
                       wglu_ref, bglu_ref, wbs_ref, wba_ref, wout_ref, lng_ref, lnb_ref, y_ref):
    load = lambda: (yssm_ref[...], zs_ref[...], o_ref[...], za_ref[...], gs_ref[...], ga_ref[...], x_ref[...])
    y_ref[...] = _post([load], gate_ref[...], wglu_ref, bglu_ref, wbs_ref, wba_ref, wout_ref, lng_ref, lnb_ref)[0]


def _final_sample(x2d, yssm, o_a, proj_s, mods, wglu, bglu, wbs, wba, wout, lng, lnb):
    rows = x2d.shape[0]
    full = lambda shape: pl.BlockSpec(shape, lambda i: (0,) * len(shape))
    pcol = lambda c: pl.BlockSpec((rows, D_MODEL), lambda i, c=c: (0, c))
    return pl.pallas_call(
        _sample_final_body,
        grid=(1,),
        in_specs=[full((rows, D_MODEL)), full((rows, D_MODEL)), pcol(COL_ZS), full((rows, D_MODEL)),
                  pcol(COL_ZA), pcol(COL_GS), pcol(COL_GA),
                  pl.BlockSpec((rows, D_MODEL), lambda i: (0, 2)),
                  full(wglu.shape), full(bglu.shape), full(wbs.shape), full(wba.shape), full(wout.shape),
                  full(lng.shape), full(lnb.shape)],
        out_specs=full((rows, D_MODEL)),
        out_shape=jax.ShapeDtypeStruct((rows, D_MODEL), F32),
        compiler_params=pltpu.CompilerParams(vmem_limit_bytes=VMEM_LIMIT),
        name="final_sample",
    )(x2d, yssm, proj_s, o_a, proj_s, proj_s, proj_s, mods, wglu, bglu, wbs, wba, wout, lng, lnb)


def _bucket_table():
    max_exact = N_BUCKETS // 2
    dist = np.arange(WINDOW + 1)
    df = np.maximum(dist, 1).astype(np.float32)
    large = max_exact + (np.log(df / np.float32(max_exact)) / np.float32(math.log(WINDOW / max_exact))
                         * np.float32(N_BUCKETS - max_exact)).astype(np.int32)
    large = np.minimum(large, N_BUCKETS - 1)
    return np.where(dist < max_exact, dist, large)


def _dot3_nt(a, b):
    ah = a.astype(BF16)
    al = (a - ah.astype(F32)).astype(BF16)
    bh = b.astype(BF16)
    bl = (b - bh.astype(F32)).astype(BF16)
    return _dot_nt(ah, bh) + _dot_nt(ah, bl) + _dot_nt(al, bh)


def _ssm_prep_body(p1_ref, p2_ref, kf_ref, bt_ref, bts_ref, c2_ref, c2s_ref, d_ref,
                   lhs_ref, bpow_ref, cpow_ref, bre_ref, bim_ref, cblk_ref):
    lane = lax.broadcasted_iota(jnp.int32, (SSM_GROUP, SW), 1)
    left = lane < SSM_STATE
    sgn = jnp.where(left, 1.0, -1.0)
    row_t = lax.broadcasted_iota(jnp.int32, (CW, 128), 0)
    lane_t = lax.broadcasted_iota(jnp.int32, (CW, 128), 1)
    bre_ref[...] = jnp.zeros_like(bre_ref)
    bim_ref[...] = jnp.zeros_like(bim_ref)
    cblk_ref[...] = jnp.zeros_like(cblk_ref)
    for gl in range(GT):
        p1, p2 = p1_ref[gl], p2_ref[gl]
        k1, k2 = kf_ref[gl, 0:1, :], kf_ref[gl, 1:2, :]
        bt, bts = bt_ref[gl], bts_ref[gl]
        c2, c2s = c2_ref[gl], c2s_ref[gl]
        bb = k1 * bt + k2 * bts
        bbs = k1 * bts - k2 * bt
        ca = [sgn * (c2 * p1[t:t + 1, :] + c2s * p2[t:t + 1, :]) for t in range(CHUNK + 1)]
        cpow_ref[gl] = jnp.concatenate(ca[1:], axis=0).astype(BF16)
        bpow_ref[gl] = jnp.concatenate(
            [p1[CHUNK - 1 - s:CHUNK - s, :] * bb + p2[CHUNK - 1 - s:CHUNK - s, :] * bbs for s in range(CHUNK)],
            axis=0).astype(BF16)
        e = _dot3_nt(jnp.concatenate(ca[:CHUNK], axis=0), jnp.concatenate([bb] * CHUNK, axis=0))
        tiles = []
        for lt in range(CW // 128):
            cols = slice(lt * 128, (lt + 1) * 128)
            et = e[:, cols]
            blk = (lane_t + lt * 128) // SSM_GROUP
            at = jnp.where(row_t == lane_t + lt * 128, d_ref[gl, :, cols], 0.0)
            for s in range(lt * 128 // SSM_GROUP, (lt + 1) * 128 // SSM_GROUP):
                sh = et if s == 0 else jnp.concatenate(
                    [jnp.zeros((s * SSM_GROUP, 128), F32), et[:CW - s * SSM_GROUP, :]], axis=0)
                at = at + jnp.where(blk == s, sh, 0.0)
            tiles.append(at)
        lhs_ref[gl] = jnp.concatenate(tiles, axis=1).astype(BF16)
        rs = slice(gl * SSM_GROUP, (gl + 1) * SSM_GROUP)
        ts = slice((gl // 2) * SW, (gl // 2 + 1) * SW)
        ti = slice(GT * SSM_STATE + (gl // 2) * SW, GT * SSM_STATE + (gl // 2 + 1) * SW)
        if gl % 2 == 0:
            bre_ref[rs, ts] = jnp.where(left, bb, 0.0).astype(BF16)
            bim_ref[rs, ts] = jnp.where(left, bbs, 0.0).astype(BF16)
            cblk_ref[rs, ts] = jnp.where(left, c2, 0.0).astype(BF16)
            cblk_ref[rs, ti] = jnp.where(left, -c2s, 0.0).astype(BF16)
        else:
            bre_ref[rs, ts] = jnp.where(left, 0.0, bbs).astype(BF16)
            bim_ref[rs, ts] = jnp.where(left, 0.0, bb).astype(BF16)
            cblk_ref[rs, ts] = jnp.where(left, 0.0, c2s).astype(BF16)
            cblk_ref[rs, ti] = jnp.where(left, 0.0, -c2).astype(BF16)


def _ssm_params(lam_re, lam_im, log_delta, b_re, b_im, c_re, c_im, d_skip):
    g = SSM_GROUPS
    dup = lambda a: jnp.concatenate([a, a], axis=-1)
    lr, li = dup(lam_re.astype(F32)), dup(lam_im.astype(F32))
    dt = jnp.exp(log_delta.astype(F32))[:, None]
    half_sign = jnp.asarray(np.where(np.arange(SW) < SSM_STATE, -1.0, 1.0), F32)
    tau = jnp.asarray(POW_TAUS, F32)[None, :, None]
    mag = jnp.exp((lr * dt)[:, None, :] * tau)
    ang = (li * dt)[:, None, :] * tau
    p1 = mag * jnp.cos(ang)
    pim = mag * jnp.sin(ang)
    p2 = pim * half_sign
    ar, ai = p1[:, 1, :], pim[:, 1, :]
    den = lr * lr + li * li
    nr = ar - 1.0
    k1 = (nr * lr + ai * li) / den
    k2 = (ai * lr - nr * li) / den * half_sign
    kf = jnp.concatenate([k1[:, None, :], k2[:, None, :], jnp.zeros((g, 6, SW), F32)], axis=1)
    brt, bit = jnp.swapaxes(b_re, 1, 2), jnp.swapaxes(b_im, 1, 2)
    bt = jnp.concatenate([brt, bit], axis=-1)
    bts = jnp.concatenate([bit, brt], axis=-1)
    c2 = jnp.concatenate([c_re, c_im], axis=-1)
    c2s = jnp.concatenate([c_im, c_re], axis=-1)
    dtile = jnp.tile(d_skip.reshape(g, 1, SSM_GROUP), (1, 1, CHUNK))

    gspec = lambda r, w: pl.BlockSpec((GT, r, w), lambda a: (a, 0, 0))
    tile_spec = lambda w: pl.BlockSpec((None, GT * SSM_GROUP, w), lambda a: (a, 0, 0))
    nrow = len(POW_TAUS)
    lhs, bpow, cpow, bre, bim, cblk = pl.pallas_call(
        _ssm_prep_body,
        grid=(g // GT,),
        in_specs=[gspec(nrow, SW), gspec(nrow, SW), gspec(8, SW), gspec(SSM_GROUP, SW), gspec(SSM_GROUP, SW),
                  gspec(SSM_GROUP, SW), gspec(SSM_GROUP, SW), gspec(1, CW)],
        out_specs=[gspec(CW, CW), gspec(CW, SW), gspec(CW, SW),
                   tile_spec(GT * SSM_STATE), tile_spec(GT * SSM_STATE), tile_spec(2 * GT * SSM_STATE)],
        out_shape=[jax.ShapeDtypeStruct((g, CW, CW), BF16), jax.ShapeDtypeStruct((g, CW, SW), BF16),
                   jax.ShapeDtypeStruct((g, CW, SW), BF16),
                   jax.ShapeDtypeStruct((GT, GT * SSM_GROUP, GT * SSM_STATE), BF16),
                   jax.ShapeDtypeStruct((GT, GT * SSM_GROUP, GT * SSM_STATE), BF16),
                   jax.ShapeDtypeStruct((GT, GT * SSM_GROUP, 2 * GT * SSM_STATE), BF16)],
        name="ssm_prep",
    )(p1, p2, kf, bt, bts, c2, c2s, dtile)
    ar_row = ar[:, :SSM_STATE].reshape(1, -1)
    ai_row = ai[:, :SSM_STATE].reshape(1, -1)
    return lhs, bpow, cpow, p1, p2, ar_row, ai_row, bre, bim, cblk


def _attn_params(rel_bias, sinks):
    hp = lax.Precision.HIGHEST
    rb = rel_bias.astype(F32)
    onehot = jnp.asarray(_bucket_table()[:, None] == np.arange(N_BUCKETS)[None, :], F32)
    tbl = jnp.dot(onehot, rb, precision=hp)
    sink_rows = jnp.repeat(sinks.astype(F32).reshape(N_KV_HEADS, 1, KV_GROUP), WINDOW, axis=2)
    ds = np.clip(WINDOW - np.arange(KROWS), 0, WINDOW)
    bs = jnp.dot(jnp.asarray(ds[:, None] == np.arange(WINDOW + 1)[None, :], F32), tbl, precision=hp)
    bs = jnp.transpose(bs.reshape(KROWS, N_KV_HEADS, KV_GROUP), (0, 2, 1)).reshape(KROWS, N_HEADS)
    bias_s = jnp.pad(bs, ((0, 0), (0, 128 - N_HEADS)))
    sk = jnp.transpose(sinks.astype(F32).reshape(N_KV_HEADS, KV_GROUP), (1, 0)).reshape(1, N_HEADS)
    sink_row = jnp.pad(sk, ((0, 0), (0, 128 - N_HEADS)))
    seg_np = np.zeros((D_MODEL, 128), np.float32)
    for j in range(KV_GROUP):
        for kh in range(N_KV_HEADS):
            r0 = j * KV_WIDTH + kh * HEAD_DIM
            seg_np[r0:r0 + HEAD_DIM, j * N_KV_HEADS + kh] = 1.0
    seg = jnp.asarray(seg_np, BF16)
    segt = jnp.asarray(seg_np.T, BF16)
    return sink_rows, bias_s, sink_row, seg, segt


WP_ROWS = 256


def _regroup_tiles(w):
    left = lax.broadcasted_iota(jnp.int32, (w.shape[0], 128), 1) < HEAD_DIM
    tile = lambda t: w[:, t * 128:(t + 1) * 128]
    out = []
    for j in range(KV_GROUP):
        for pair in range(N_KV_HEADS // 2):
            t1 = tile((2 * pair) * 2 + j // 2)
            t2 = tile((2 * pair + 1) * 2 + j // 2)
            if j % 2 == 0:
                out.append(jnp.where(left, t1, pltpu.roll(t2, HEAD_DIM, 1)))
            else:
                out.append(jnp.where(left, pltpu.roll(t1, HEAD_DIM, 1), t2))
    return jnp.concatenate(out, axis=1)


def _weights_body(win_ref, wglu_ref, wbs_ref, wba_ref, wout_ref,
                  winb_ref, wut_ref, wglub_ref, wbsb_ref, wbab_ref, woutb_ref):
    i = pl.program_id(0)
    seg = lambda a: win_ref[:, a:a + D_MODEL]
    u = seg(0)
    winb_ref[:, 0:D_MODEL] = u.astype(BF16)
    winb_ref[:, D_MODEL:2 * D_MODEL] = seg(D_MODEL).astype(BF16)
    winb_ref[:, 2 * D_MODEL:3 * D_MODEL] = (_regroup_tiles(seg(2 * D_MODEL)) * (HEAD_DIM ** -0.5)).astype(BF16)
    winb_ref[:, 3 * D_MODEL:4 * D_MODEL] = _regroup_tiles(seg(3 * D_MODEL + 2 * KV_WIDTH)).astype(BF16)
    winb_ref[:, 4 * D_MODEL:5 * D_MODEL] = seg(4 * D_MODEL + 2 * KV_WIDTH).astype(BF16)
    winb_ref[:, 5 * D_MODEL:6 * D_MODEL] = seg(5 * D_MODEL + 2 * KV_WIDTH).astype(BF16)
    winb_ref[:, 6 * D_MODEL:6 * D_MODEL + 2 * KV_WIDTH] = \
        win_ref[:, 3 * D_MODEL:3 * D_MODEL + 2 * KV_WIDTH].astype(BF16)
    wut_ref[...] = u.astype(BF16).T
    wglub_ref[...] = wglu_ref[...].astype(BF16)
    wbsb_ref[...] = wbs_ref[...].astype(BF16)
    woutb_ref[...] = wout_ref[...].astype(BF16)
    wba = wba_ref[...].astype(BF16)
    for j in range(KV_GROUP):
        r0 = pl.multiple_of(j * KV_WIDTH + i * HEAD_DIM, HEAD_DIM)
        wbab_ref[pl.ds(r0, HEAD_DIM), :] = wba[j * HEAD_DIM:(j + 1) * HEAD_DIM, :]


def _weights_prep(w_in0, w_glu0, w_bs0, w_ba0, w_out0):
    assert WP_ROWS == KV_GROUP * HEAD_DIM
    rows = lambda w: pl.BlockSpec((WP_ROWS, w), lambda i: (i, 0))
    sq = jax.ShapeDtypeStruct((D_MODEL, D_MODEL), BF16)
    return pl.pallas_call(
        _weights_body,
        grid=(D_MODEL // WP_ROWS,),
        in_specs=[rows(D_IN), rows(D_MODEL), rows(D_MODEL), rows(D_MODEL), rows(D_MODEL)],
        out_specs=[rows(D_IN), pl.BlockSpec((D_MODEL, WP_ROWS), lambda i: (0, i)), rows(D_MODEL), rows(D_MODEL),
                   pl.BlockSpec((D_MODEL, D_MODEL), lambda i: (0, 0)), rows(D_MODEL)],
        out_shape=[jax.ShapeDtypeStruct((D_MODEL, D_IN), BF16), sq, sq, sq, sq, sq],
        compiler_params=pltpu.CompilerParams(dimension_semantics=("arbitrary",), vmem_limit_bytes=VMEM_LIMIT),
        name="weights_prep",
    )(w_in0, w_glu0, w_bs0, w_ba0, w_out0)


def kernel(x_prompt, x_sample, c_prompt, c_sample, state_ssm_re, state_ssm_im, cache_swa_k, cache_swa_v,
           w_ada, b_ada, w_in, ssm_lambda_re, ssm_lambda_im, ssm_log_delta, ssm_b_re, ssm_b_im,
           ssm_c_re, ssm_c_im, ssm_d, w_glu, b_glu, attn_sinks, rel_bias, w_branch_s, w_branch_a,
           w_out, ln_g, ln_b):
    assert w_ada.shape[0] == 1, "single-layer trunk"
    n_batch, seq, _ = x_prompt.shape
    n_dec = x_sample.shape[0]

    w_in_b, wut, wglu, wbs, wba, wout = _weights_prep(w_in[0], w_glu[0], w_branch_s[0], w_branch_a[0], w_out[0])
    bglu = b_glu[0].reshape(1, -1).astype(F32)
    lng = ln_g[0].reshape(1, -1).astype(F32)
    lnb = ln_b[0].reshape(1, -1).astype(F32)
    lhs, bpow, cpow, p1, p2, ar_row, ai_row, bre, bim, cblk = _ssm_params(
        ssm_lambda_re[0], ssm_lambda_im[0], ssm_log_delta[0], ssm_b_re[0], ssm_b_im[0],
        ssm_c_re[0], ssm_c_im[0], ssm_d[0])
    sink_rows, bias_s, sink_row, seg, segt = _attn_params(rel_bias, attn_sinks[0])

    c_all = jnp.concatenate([c_prompt, jnp.zeros((8 - n_batch, D_MODEL), F32), c_sample], axis=0)
    mod = _ada(c_all, w_ada[0], b_ada[0])
    modp = mod[:n_batch].reshape(n_batch * 3, 1, D_MODEL)
    mods = mod[8:8 + n_dec]

    xp = x_prompt.reshape(n_batch * seq, D_MODEL)
    yssm, hfin = _ssm_prompt(x_prompt, modp, wut, lhs, bpow, cpow, p1, p2)
    yp, ktail, vtail = _main_prompt(rel_bias.astype(F32).reshape(-1), xp, yssm, modp, w_in_b, sink_rows,
                                    wglu, bglu, wbs, wba, wout, lng, lnb, n_batch, seq)
    yp = yp.reshape(n_batch, seq, D_MODEL)
    p_hr = hfin[:, :, 0, :SSM_STATE][None]
    p_hi = hfin[:, :, 0, SSM_STATE:][None]
    p_k = ktail.reshape(1, n_batch, WINDOW, N_KV_HEADS, HEAD_DIM)
    p_v = vtail.reshape(1, n_batch, WINDOW, N_KV_HEADS, HEAD_DIM)

    xs = x_sample.reshape(n_dec, D_MODEL)
    proj_s = _proj_sample(xs, mods, w_in_b)
    ns = SSM_GROUPS * SSM_STATE
    ys_s, s_hr, s_hi = _sample_ssm(proj_s, state_ssm_re[0].reshape(n_dec, ns), state_ssm_im[0].reshape(n_dec, ns),
                                   ar_row, ai_row, bre, bim, cblk, ssm_d[0].reshape(1, -1).astype(F32))
    o_s, nk, nv = _sample_attn(proj_s, cache_swa_k[0].reshape(n_dec, WINDOW, KV_WIDTH),
                               cache_swa_v[0].reshape(n_dec, WINDOW, KV_WIDTH), bias_s, sink_row, seg, segt)
    ysmp = _final_sample(xs, ys_s, o_s, proj_s, mods, wglu, bglu, wbs, wba, wout, lng, lnb)

    return (yp, ysmp.reshape(n_dec, 1, D_MODEL),
            p_hr, p_hi, p_k, p_v,
            s_hr.reshape(1, n_dec, SSM_GROUPS, SSM_STATE), s_hi.reshape(1, n_dec, SSM_GROUPS, SSM_STATE),
            nk.reshape(1, n_dec, WINDOW, N_KV_HEADS, HEAD_DIM), nv.reshape(1, n_dec, WINDOW, N_KV_HEADS, HEAD_DIM))
```

```python
import math

import numpy as np
import jax
import jax.numpy as jnp
from jax import lax
from jax.experimental import pallas as pl
from jax.experimental.pallas import tpu as pltpu

F32 = jnp.float32
BF16 = jnp.bfloat16

D_MODEL = 1024
SSM_GROUPS = 64
SSM_GROUP = 16
SSM_STATE = 64
N_HEADS = 16
HEAD_DIM = 64
N_KV_HEADS = 4
KV_GROUP = 4
KV_WIDTH = N_KV_HEADS * HEAD_DIM
WINDOW = 128
N_BUCKETS = 32
NEG_INF = -1e30
LN_EPS = 1e-5
DEPTH = 1
DEEPNORM_ALPHA = (2 * DEPTH) ** 0.25
D_IN = 6656
CHUNK = 16
CW = CHUNK * SSM_GROUP
SW = 2 * SSM_STATE

COL_U, COL_ZS, COL_Q, COL_ZA, COL_GS, COL_GA = 0, 1, 2, 3, 4, 5
COL_K, COL_V = 24, 25

VMEM_LIMIT = 56 * 1024 * 1024
VMEM_LIMIT_SSM = 60 * 1024 * 1024


def _sigmoid(x):
    return 0.5 * jnp.tanh(0.5 * x) + 0.5


def _silu(x):
    return x * _sigmoid(x)


def _gelu_tanh(x):
    c = math.sqrt(2.0 / math.pi)
    hx = 0.5 * x
    return hx + hx * jnp.tanh(x * (c + (c * 0.044715) * (x * x)))


def _dot(a, b):
    return jnp.dot(a, b, preferred_element_type=F32)


def _dot_nt(a, b):
    return lax.dot_general(a, b, (((1,), (1,)), ((), ())), preferred_element_type=F32)


def _ada_body(c_ref, w_ref, b_ref, o_ref):
    c = c_ref[...]
    sc = _silu(c).astype(BF16)
    o_ref[...] = _dot(sc, w_ref[...].astype(BF16)) + b_ref[...]


def _ada(c_all, w_ada, b_ada):
    rows = c_all.shape[0]
    bn = 512
    return pl.pallas_call(
        _ada_body,
        grid=(3 * D_MODEL // bn,),
        in_specs=[pl.BlockSpec((rows, D_MODEL), lambda j: (0, 0)),
                  pl.BlockSpec((D_MODEL, bn), lambda j: (0, j)),
                  pl.BlockSpec((1, bn), lambda j: (0, j))],
        out_specs=pl.BlockSpec((rows, bn), lambda j: (0, j)),
        out_shape=jax.ShapeDtypeStruct((rows, 3 * D_MODEL), F32),
        name="ada",
    )(c_all, w_ada, b_ada.reshape(1, -1))


PROJ_CB = 512


def _proj_body(x_ref, shift_ref, scale_ref, w_ref, o_ref):
    h = x_ref[...] * (1.0 + scale_ref[...]) + shift_ref[...]
    hb = h.astype(BF16)
    for j in range(D_IN // PROJ_CB):
        sl = slice(j * PROJ_CB, (j + 1) * PROJ_CB)
        o_ref[:, sl] = _dot(hb, w_ref[:, sl]).astype(BF16)


def _proj_sample(xs, mods, w_in_b):
    rows = xs.shape[0]
    return pl.pallas_call(
        _proj_body,
        grid=(1,),
        in_specs=[pl.BlockSpec((rows, D_MODEL), lambda i: (0, 0)),
                  pl.BlockSpec((rows, D_MODEL), lambda i: (0, 0)),
                  pl.BlockSpec((rows, D_MODEL), lambda i: (0, 1)),
                  pl.BlockSpec((D_MODEL, D_IN), lambda i: (0, 0))],
        out_specs=pl.BlockSpec((rows, D_IN), lambda i: (0, 0)),
        out_shape=jax.ShapeDtypeStruct((rows, D_IN), BF16),
        compiler_params=pltpu.CompilerParams(vmem_limit_bytes=VMEM_LIMIT),
        name="proj",
    )(xs, mods, mods, w_in_b)


NKS = 128
GC = 16


def _cmul(c1, c2, x):
    return c1 * x + c2 * pltpu.roll(x, SSM_STATE, x.ndim - 1)


SSM_TOK = NKS * CHUNK
SUB = 8
NPH = CHUNK // SUB


def _dot_tn(a, b):
    return lax.dot_general(a, b, (((0,), (0,)), ((), ())), preferred_element_type=F32)


SEG = 8
SEG_LEN = NKS // SEG
POW_TAUS = (tuple(range(CHUNK + 1)) + tuple(CHUNK * m for m in range(2, SEG_LEN))
            + tuple(CHUNK * SEG_LEN * m for m in range(1, SEG + 1)) + (0,))
ROW_CHUNK1 = CHUNK
ROW_SEG1 = CHUNK + SEG_LEN - 1


def _ssm_body(x_ref, shift_ref, scale_ref, wut_ref, lhs_ref, bpow_ref, cpow_ref, p1_ref, p2_ref,
              y_ref, hfin_ref, r_scr, yt_scr, s_scr, carry_scr):
    i = pl.program_id(1)
    j = pl.program_id(2)

    @pl.when((i == 0) & (j == 0))
    def _():
        carry_scr[...] = jnp.zeros_like(carry_scr)

    @pl.when(j < NPH)
    def _():
        pairs = range(SUB // 2)
        hbs = []
        for sp in pairs:
            h = jnp.concatenate([x_ref[:, b, s, :] for s in (2 * sp, 2 * sp + 1) for b in range(SEG_LEN)],
                                axis=0)
            hbs.append((h * (1.0 + scale_ref[...]) + shift_ref[...]).astype(BF16))
        uts = [_dot_nt(wut_ref[...], hb).astype(BF16) for hb in hbs]
        for sp, ut in zip(pairs, uts):
            ut = ut.reshape(SSM_GROUPS, SSM_GROUP, 2 * NKS)
            pair = jnp.concatenate([ut[:, :, :NKS], ut[:, :, NKS:]], axis=1)
            r0 = pl.multiple_of(j * SUB * SSM_GROUP + sp * 2 * SSM_GROUP, 2 * SSM_GROUP)
            r_scr[:, pl.ds(r0, 2 * SSM_GROUP), :] = pair

    @pl.when(j == NPH - 1)
    def _():
        def local(g):
            rg = r_scr[g]
            yt_scr[g] = _dot(lhs_ref[g], rg)
            s_scr[g] = _dot_tn(rg, bpow_ref[g])

        def carried(g):
            yt_scr[g] += _dot_nt(cpow_ref[g], s_scr[g].astype(BF16))

        row = lax.broadcasted_iota(jnp.int32, (GC, SEG, SW), 1)
        seg_rows = lambda b: slice(b * SEG, (b + 1) * SEG)

        def scan(gc):
            gs = slice(gc * GC, (gc + 1) * GC)
            p1, p2 = p1_ref[gs], p2_ref[gs]
            c1, c2 = p1[:, ROW_CHUNK1:ROW_CHUNK1 + 1, :], p2[:, ROW_CHUNK1:ROW_CHUNK1 + 1, :]
            h = jnp.zeros((GC, SEG, SW), F32)
            hs = h
            for b in range(SEG_LEN):
                sb = s_scr[gs, seg_rows(b), :]
                h, hs = sb + c1 * h + c2 * hs, pltpu.roll(sb, SSM_STATE, 2) + c1 * hs - c2 * h
                s_scr[gs, seg_rows(b), :] = h
                yield
            e = h
            for d in (1, 2, 4):
                rd = ROW_SEG1 + d - 1
                sh = jnp.where(row >= d, pltpu.roll(e, d, 1), 0.0)
                e = e + _cmul(p1[:, rd:rd + 1, :], p2[:, rd:rd + 1, :], sh)
            carry = carry_scr[gs]
            e = e + _cmul(p1[:, ROW_SEG1:ROW_SEG1 + SEG, :], p2[:, ROW_SEG1:ROW_SEG1 + SEG, :], carry)
            ein = jnp.where(row == 0, carry, pltpu.roll(e, 1, 1))
            eins = pltpu.roll(ein, SSM_STATE, 2)
            carry_scr[gs] = jnp.broadcast_to(e[:, SEG - 1:SEG, :], (GC, SEG, SW))
            yield
            for b in range(SEG_LEN - 1, 0, -1):
                rb = ROW_CHUNK1 + b - 1
                s_scr[gs, seg_rows(b), :] = (s_scr[gs, seg_rows(b - 1), :]
                                             + p1[:, rb:rb + 1, :] * ein + p2[:, rb:rb + 1, :] * eins)
                yield
            s_scr[gs, seg_rows(0), :] = ein

        def interleave(fn, groups, gen):
            for g in groups:
                fn(g)
                for _ in range(2):
                    next(gen, None)
            for _ in gen:
                pass

        n_sets = SSM_GROUPS // GC
        sets = [range(q * GC, (q + 1) * GC) for q in range(n_sets)]
        for g in sets[0]:
            local(g)
        for q in range(1, n_sets):
            interleave(local, sets[q], scan(q - 1))
        interleave(carried, sets[0], scan(n_sets - 1))
        for q in range(1, n_sets):
            for g in sets[q]:
                carried(g)
        hfin_ref[...] = carry_scr[...]

    @pl.when(j >= NPH)
    def _():
        for sp in range(SUB):
            t0 = pl.multiple_of((j - NPH) * SUB * SSM_GROUP + sp * SSM_GROUP, SSM_GROUP)
            yt = yt_scr[:, pl.ds(t0, SSM_GROUP), :].reshape(D_MODEL, NKS)
            yr = yt.T
            for b in range(SEG_LEN):
                y_ref[:, b, sp, :] = yr[b * SEG:(b + 1) * SEG, :]


def _ssm_prompt(x_prompt, modp, wut, lhs, bpow, cpow, p1, p2):
    n_batch, seq, _ = x_prompt.shape
    steps = seq // SSM_TOK
    xv = x_prompt.reshape(n_batch * steps * SEG, SEG_LEN, CHUNK, D_MODEL)
    mod_spec = lambda k: pl.BlockSpec((None, 1, D_MODEL), lambda b, i, j, k=k: (b * 3 + k, 0, 0))
    consts = (wut, lhs, bpow, cpow, p1, p2)
    y, hfin = pl.pallas_call(
        _ssm_body,
        grid=(n_batch, steps, 2 * NPH),
        in_specs=[pl.BlockSpec((SEG, SEG_LEN, SUB, D_MODEL),
                               lambda b, i, j: (b * steps + i, 0, jnp.minimum(j, NPH - 1), 0)),
                  mod_spec(0), mod_spec(1)] + [_const_spec(c.shape) for c in consts],
        out_specs=[pl.BlockSpec((SEG, SEG_LEN, SUB, D_MODEL),
                                lambda b, i, j: (b * steps + i, 0, jnp.maximum(j - NPH, 0), 0)),
                   pl.BlockSpec((None, SSM_GROUPS, 8, SW), lambda b, i, j: (b, 0, 0, 0))],
        out_shape=[jax.ShapeDtypeStruct(xv.shape, F32),
                   jax.ShapeDtypeStruct((n_batch, SSM_GROUPS, 8, SW), F32)],
        scratch_shapes=[pltpu.VMEM((SSM_GROUPS, CW, NKS), BF16), pltpu.VMEM((SSM_GROUPS, CW, NKS), F32),
                        pltpu.VMEM((SSM_GROUPS, NKS, SW), F32), pltpu.VMEM((SSM_GROUPS, 8, SW), F32)],
        compiler_params=pltpu.CompilerParams(dimension_semantics=("arbitrary", "arbitrary", "arbitrary"),
                                             vmem_limit_bytes=VMEM_LIMIT_SSM),
        name="ssm_prompt",
    )(xv, modp, modp, *consts)
    return y.reshape(n_batch * seq, D_MODEL), hfin


def _post(chunks, gate, wglu_ref, bglu_ref, wbs_ref, wba_ref, wout_ref, lng_ref, lnb_ref):
    data = [load() for load in chunks]
    ys = [_gelu_tanh(d[0]) for d in data]
    glu = [y * _sigmoid(_dot(y.astype(BF16), wglu_ref[...]) + bglu_ref[...]) for y in ys]
    b_s = [_dot(g.astype(BF16) * _silu(d[1]), wbs_ref[...]) for g, d in zip(glu, data)]
    b_a = [_dot(d[2].astype(BF16) * _silu(d[3]), wba_ref[...]) for d in data]
    m = [_sigmoid(d[4]).astype(F32) * s + _sigmoid(d[5]).astype(F32) * a for d, s, a in zip(data, b_s, b_a)]
    out = [_dot(v.astype(BF16), wout_ref[...]) for v in m]
    res = []
    for d, o in zip(data, out):
        r = DEEPNORM_ALPHA * d[6] + gate * o
        mu = jnp.mean(r, axis=-1, keepdims=True)
        rc = r - mu
        var = jnp.mean(rc * rc, axis=-1, keepdims=True)
        res.append(rc * lax.rsqrt(var + LN_EPS) * lng_ref[...] + lnb_ref[...])
    return res


TB = 512
NQB = TB // WINDOW
W_OFF = D_MODEL
ONES_ROWS = 16
POST_ROWS = 256
BLK_GROUP = 1
PROJ_W = 256


def _const_spec(shape):
    nd = len(shape)
    return pl.BlockSpec(shape, lambda *_: (0,) * nd, pipeline_mode=pl.Buffered(1))


def _bucket_thresholds():
    tbl = _bucket_table()
    return [int(np.argmax(tbl >= k)) for k in range(1, N_BUCKETS)]


def _fill_bias_table(rb_ref, bias_s):
    si = lax.broadcasted_iota(jnp.int32, (2 * WINDOW, WINDOW), 0)
    qi = lax.broadcasted_iota(jnp.int32, (2 * WINDOW, WINDOW), 1)
    dist = qi + WINDOW - si
    bid = jnp.zeros((2 * WINDOW, WINDOW), jnp.int32)
    for thr in _bucket_thresholds():
        bid = bid + (dist >= thr).astype(jnp.int32)
    valid = (dist >= 0) & (dist <= WINDOW)
    own = si >= WINDOW

    def per_head(h, c):
        t = jnp.zeros((2 * WINDOW, WINDOW), F32)
        for b in range(N_BUCKETS):
            t = jnp.where(bid == b, rb_ref[b * N_HEADS + h], t)
        t = jnp.where(valid, t, NEG_INF)
        kh, j = h // KV_GROUP, h % KV_GROUP
        bias_s[0, kh, j] = t
        bias_s[1, kh, j] = jnp.where(own, t, NEG_INF)
        return c

    lax.fori_loop(0, N_HEADS, per_head, 0)


def _main_body(rb_ref, x_ref, yssm_ref, shift_ref, scale_ref, gate_ref, wmain_ref, sink_ref,
               wglu_ref, bglu_ref, wbs_ref, wba_ref, wout_ref, lng_ref, lnb_ref,
               y_ref, ktail_ref, vtail_ref,
               zs_s, q_s, za_s, gs_s, ga_s, k_s, vt_s, ot_s, o_s, bias_s):
    i = pl.program_id(1)

    @pl.when((pl.program_id(0) == 0) & (i == 0))
    def _():
        _fill_bias_table(rb_ref, bias_s)

    @pl.when(i == 0)
    def _():
        k_s[0:WINDOW, :] = jnp.zeros((WINDOW, KV_WIDTH), BF16)
        vt_s[:, 0:WINDOW] = jnp.zeros((KV_WIDTH, WINDOW), BF16)

    x = x_ref[...]
    hb = (x * (1.0 + scale_ref[...]) + shift_ref[...]).astype(BF16)

    def project(dst, c0, c, width=PROJ_W):
        lo = W_OFF + c0 + c * width
        dst[:, c * width:(c + 1) * width] = _dot(hb, wmain_ref[:, lo:lo + width]).astype(BF16)

    for c in range(D_MODEL // 512):
        project(q_s, D_MODEL, c, 512)
    deferred = [(dst, c0, c) for dst, c0 in ((zs_s, 0), (za_s, 2 * D_MODEL), (gs_s, 3 * D_MODEL), (ga_s, 4 * D_MODEL))
                for c in range(D_MODEL // PROJ_W)]

    kv = _dot(hb, wmain_ref[:, W_OFF + 5 * D_MODEL:W_OFF + 5 * D_MODEL + 2 * KV_WIDTH]).astype(BF16)
    k_s[WINDOW:WINDOW + TB, :] = kv[:, :KV_WIDTH]
    vt_s[:, WINDOW:WINDOW + TB] = kv[:, KV_WIDTH:].T

    first = jnp.where(i == 0, 1, 0)
    lane = lax.broadcasted_iota(jnp.int32, (2 * WINDOW, 128), 1)
    ones = jnp.ones((ONES_ROWS, 2 * WINDOW), BF16)
    n_def = len(deferred) * BLK_GROUP // NQB
    for grp in range(NQB // BLK_GROUP):
        units = [(blk, kh) for blk in range(grp * BLK_GROUP, (grp + 1) * BLK_GROUP) for kh in range(N_KV_HEADS)]
        mine = deferred[grp * n_def:(grp + 1) * n_def]
        sts = []
        for blk, kh in units:
            rows = slice(blk * WINDOW, (blk + 1) * WINDOW)
            kcat = k_s[blk * WINDOW:blk * WINDOW + 2 * WINDOW, :]
            pair, odd = kh // 2, kh % 2
            kcm = jnp.where((lane >= HEAD_DIM) == bool(odd), kcat[:, pair * 128:(pair + 1) * 128], jnp.zeros((), BF16))
            qp = jnp.concatenate(
                [q_s[rows, j * KV_WIDTH + pair * 128:j * KV_WIDTH + (pair + 1) * 128] for j in range(KV_GROUP)],
                axis=0)
            sel = first if blk == 0 else 0
            bias = jnp.concatenate([bias_s[sel, kh, j] for j in range(KV_GROUP)], axis=1)
            sts.append(_dot_nt(kcm, qp) + bias)
        share = [mine[k * len(mine) // 3:(k + 1) * len(mine) // 3] for k in range(3)]
        ms = [jnp.maximum(jnp.max(st, axis=0, keepdims=True), sink_ref[kh]) for st, (_, kh) in zip(sts, units)]
        for spec in share[0]:
            project(*spec)
        es = [jnp.exp(st - m).astype(BF16) for st, m in zip(sts, ms)]
        for spec in share[1]:
            project(*spec)
        rs = []
        for e, (blk, kh) in zip(es, units):
            vtc = vt_s[kh * HEAD_DIM:(kh + 1) * HEAD_DIM, blk * WINDOW:blk * WINDOW + 2 * WINDOW]
            rs.append(_dot(jnp.concatenate([vtc, ones], axis=0), e))
        for spec in share[2]:
            project(*spec)
        for r, m, (blk, kh) in zip(rs, ms, units):
            den = r[HEAD_DIM:HEAD_DIM + 1, :] + jnp.exp(sink_ref[kh] - m)
            ot = (r[:HEAD_DIM, :] / den).astype(BF16)
            for j in range(KV_GROUP):
                ot_s[blk, j, kh * HEAD_DIM:(kh + 1) * HEAD_DIM, :] = ot[:, j * WINDOW:(j + 1) * WINDOW]
        for blk in range(grp * BLK_GROUP, (grp + 1) * BLK_GROUP):
            for j in range(KV_GROUP):
                o_s[blk * WINDOW:(blk + 1) * WINDOW, j * KV_WIDTH:(j + 1) * KV_WIDTH] = ot_s[blk, j].T

    k_s[0:WINDOW, :] = k_s[TB:TB + WINDOW, :]
    vt_s[:, 0:WINDOW] = vt_s[:, TB:TB + WINDOW]

    def loader(c):
        rs = slice(c * POST_ROWS, (c + 1) * POST_ROWS)
        return lambda: (yssm_ref[rs, :], zs_s[rs, :], o_s[rs, :], za_s[rs, :], gs_s[rs, :], ga_s[rs, :], x_ref[rs, :])

    ys = _post([loader(c) for c in range(TB // POST_ROWS)], gate_ref[...],
               wglu_ref, bglu_ref, wbs_ref, wba_ref, wout_ref, lng_ref, lnb_ref)
    for c, yc in enumerate(ys):
        y_ref[c * POST_ROWS:(c + 1) * POST_ROWS, :] = yc

    @pl.when(i == pl.num_programs(1) - 1)
    def _():
        ktail_ref[...] = k_s[0:WINDOW, :].astype(F32)
        vtail_ref[...] = vt_s[:, 0:WINDOW].T.astype(F32)


def _main_prompt(rel_bias_flat, x2d, yssm, modp, wmain, sink_rows, wglu, bglu, wbs, wba, wout, lng, lnb,
                 n_batch, seq):
    steps = seq // TB
    row_spec = pl.BlockSpec((TB, D_MODEL), lambda b, i: (b * steps + i, 0))
    mod_spec = lambda j: pl.BlockSpec((None, 1, D_MODEL), lambda b, i, j=j: (b * 3 + j, 0, 0))
    tail_spec = pl.BlockSpec((None, WINDOW, KV_WIDTH), lambda b, i: (b, 0, 0))
    consts = (wmain, sink_rows, wglu, bglu, wbs, wba, wout, lng, lnb)
    return pl.pallas_call(
        _main_body,
        grid=(n_batch, steps),
        in_specs=[pl.BlockSpec(memory_space=pltpu.SMEM), row_spec, row_spec, mod_spec(0), mod_spec(1), mod_spec(2)]
        + [_const_spec(c.shape) for c in consts],
        out_specs=[row_spec, tail_spec, tail_spec],
        out_shape=[jax.ShapeDtypeStruct((n_batch * seq, D_MODEL), F32),
                   jax.ShapeDtypeStruct((n_batch, WINDOW, KV_WIDTH), F32),
                   jax.ShapeDtypeStruct((n_batch, WINDOW, KV_WIDTH), F32)],
        scratch_shapes=[pltpu.VMEM((TB, D_MODEL), BF16)] * 5
        + [pltpu.VMEM((TB + WINDOW, KV_WIDTH), BF16), pltpu.VMEM((KV_WIDTH, TB + WINDOW), BF16),
           pltpu.VMEM((NQB, KV_GROUP, KV_WIDTH, WINDOW), BF16), pltpu.VMEM((TB, D_MODEL), BF16),
           pltpu.VMEM((2, N_KV_HEADS, KV_GROUP, 2 * WINDOW, WINDOW), F32)],
        compiler_params=pltpu.CompilerParams(dimension_semantics=("arbitrary", "arbitrary"),
                                             vmem_limit_bytes=VMEM_LIMIT),
        name="main_prompt",
    )(rel_bias_flat, x2d, yssm, modp, modp, modp, *consts)


GT = 8


def _sample_ssm_body(u_ref, h0r_ref, h0i_ref, ar_ref, ai_ref, bre_ref, bim_ref, cblk_ref, d_ref,
                     y_ref, hr_ref, hi_ref):
    ub = u_ref[...]
    ys = []
    for a in range(GT):
        ua = ub[:, a * 128:(a + 1) * 128]
        st = slice(a * 512, (a + 1) * 512)
        xr = _dot(ua, bre_ref[a])
        xi = _dot(ua, bim_ref[a])
        h0r = h0r_ref[:, st]
        h0i = h0i_ref[:, st]
        ar = ar_ref[:, st]
        ai = ai_ref[:, st]
        hr = ar * h0r - ai * h0i + xr
        hi = ar * h0i + ai * h0r + xi
        hr_ref[:, st] = hr
        hi_ref[:, st] = hi
        hcat = jnp.concatenate([hr, hi], axis=1).astype(BF16)
        ys.append(_dot_nt(hcat, cblk_ref[a]))
    y_ref[...] = jnp.concatenate(ys, axis=1) + d_ref[...] * ub.astype(F32)


def _sample_ssm(proj_s, h0r, h0i, ar, ai, bre, bim, cblk, d_row):
    rows = proj_s.shape[0]
    ns = SSM_GROUPS * SSM_STATE
    full = lambda shape: pl.BlockSpec(shape, lambda i: (0,) * len(shape))
    return pl.pallas_call(
        _sample_ssm_body,
        grid=(1,),
        in_specs=[pl.BlockSpec((rows, D_MODEL), lambda i: (0, COL_U)),
                  full((rows, ns)), full((rows, ns)), full((1, ns)), full((1, ns)),
                  full(bre.shape), full(bim.shape), full(cblk.shape), full((1, D_MODEL))],
        out_specs=[full((rows, D_MODEL)), full((rows, ns)), full((rows, ns))],
        out_shape=[jax.ShapeDtypeStruct((rows, D_MODEL), F32),
                   jax.ShapeDtypeStruct((rows, ns), F32),
                   jax.ShapeDtypeStruct((rows, ns), F32)],
        compiler_params=pltpu.CompilerParams(vmem_limit_bytes=VMEM_LIMIT),
        name="sample_ssm",
    )(proj_s, h0r, h0i, ar, ai, bre, bim, cblk, d_row)


SB = 16
KROWS = WINDOW + 8
SEQ_GROUP = 4


def _sample_attn_body(q_ref, kn_ref, vn_ref, ck_ref, cv_ref, bias_ref, sink_ref, seg_ref, segt_ref,
                      o_ref, nk_ref, nv_ref):
    qf = q_ref[...].astype(F32)
    knf = kn_ref[...].astype(F32)
    vnf = vn_ref[...].astype(F32)
    rowk = lax.broadcasted_iota(jnp.int32, (KROWS, 128), 0)
    roww = lax.broadcasted_iota(jnp.int32, (WINDOW, KV_WIDTH), 0)
    sink = sink_ref[...]
    last = roww == WINDOW - 1
    for g0 in range(0, SB, SEQ_GROUP):
        seqs = range(g0, g0 + SEQ_GROUP)
        scores = []
        for b in seqs:
            kall = jnp.concatenate([ck_ref[b], jnp.broadcast_to(knf[b:b + 1], (8, KV_WIDTH))], axis=0)
            qrow = qf[b:b + 1]
            prod = jnp.concatenate([kall * qrow[:, j * KV_WIDTH:(j + 1) * KV_WIDTH] for j in range(KV_GROUP)], axis=1)
            s = _dot(prod.astype(BF16), seg_ref[...]) + bias_ref[...]
            scores.append(jnp.where(rowk <= WINDOW, s, NEG_INF))
        probs = []
        for s in scores:
            m = jnp.maximum(jnp.max(s, axis=0, keepdims=True), sink)
            e = jnp.exp(s - m)
            den = jnp.sum(e, axis=0, keepdims=True) + jnp.exp(sink - m)
            probs.append((e / den).astype(BF16))
        pexps = [_dot(p, segt_ref[...]) for p in probs]
        for b, pexp in zip(seqs, pexps):
            vb = cv_ref[b]
            vall = jnp.concatenate([vb, jnp.broadcast_to(vnf[b:b + 1], (8, KV_WIDTH))], axis=0)
            v4 = jnp.concatenate([vall] * KV_GROUP, axis=1)
            o_ref[b:b + 1, :] = jnp.sum(pexp * v4, axis=0, keepdims=True)
            nv_ref[b] = jnp.where(last, jnp.broadcast_to(vnf[b:b + 1], (WINDOW, KV_WIDTH)), pltpu.roll(vb, WINDOW - 1, 0))
            nk_ref[b] = jnp.where(last, jnp.broadcast_to(knf[b:b + 1], (WINDOW, KV_WIDTH)),
                                  pltpu.roll(ck_ref[b], WINDOW - 1, 0))


def _sample_attn(proj_s, ck, cv, bias_s, sink_row, seg, segt):
    rows = proj_s.shape[0]
    cache_spec = pl.BlockSpec((SB, WINDOW, KV_WIDTH), lambda i: (i, 0, 0))
    return pl.pallas_call(
        _sample_attn_body,
        grid=(rows // SB,),
        in_specs=[pl.BlockSpec((SB, D_MODEL), lambda i: (i, COL_Q)),
                  pl.BlockSpec((SB, KV_WIDTH), lambda i: (i, COL_K)),
                  pl.BlockSpec((SB, KV_WIDTH), lambda i: (i, COL_V)),
                  cache_spec, cache_spec,
                  pl.BlockSpec(bias_s.shape, lambda i: (0, 0)),
                  pl.BlockSpec(sink_row.shape, lambda i: (0, 0)),
                  pl.BlockSpec(seg.shape, lambda i: (0, 0)),
                  pl.BlockSpec(segt.shape, lambda i: (0, 0))],
        out_specs=[pl.BlockSpec((SB, D_MODEL), lambda i: (i, 0)), cache_spec, cache_spec],
        out_shape=[jax.ShapeDtypeStruct((rows, D_MODEL), F32),
                   jax.ShapeDtypeStruct(ck.shape, F32),
                   jax.ShapeDtypeStruct(cv.shape, F32)],
        compiler_params=pltpu.CompilerParams(vmem_limit_bytes=VMEM_LIMIT),
        name="sample_attn",
    )(proj_s, proj_s, proj_s, ck, cv, bias_s, sink_row, seg, segt)


def _sample_final_body(x_ref, yssm_ref, zs_ref, o_ref, za_ref, gs_ref, ga_ref, gate_ref,
                       wglu_ref, bglu_ref, wbs_ref, wba_ref, wout_ref, lng_ref, lnb_ref, y_ref):
    load = lambda: (yssm_ref[...], zs_ref[...], o_ref[...], za_ref[...], gs_ref[...], ga_ref[...], x_ref[...])
    y_ref[...] = _post([load], gate_ref[...], wglu_ref, bglu_ref, wbs_ref, wba_ref, wout_ref, lng_ref, lnb_ref)[0]


def _final_sample(x2d, yssm, o_a, proj_s, mods, wglu, bglu, wbs, wba, wout, lng, lnb):
    rows = x2d.shape[0]
    full = lambda shape: pl.BlockSpec(shape, lambda i: (0,) * len(shape))
    pcol = lambda c: pl.BlockSpec((rows, D_MODEL), lambda i, c=c: (0, c))
    return pl.pallas_call(
        _sample_final_body,
        grid=(1,),
        in_specs=[full((rows, D_MODEL)), full((rows, D_MODEL)), pcol(COL_ZS), full((rows, D_MODEL)),
                  pcol(COL_ZA), pcol(COL_GS), pcol(COL_GA),
                  pl.BlockSpec((rows, D_MODEL), lambda i: (0, 2)),
                  full(wglu.shape), full(bglu.shape), full(wbs.shape), full(wba.shape), full(wout.shape),
                  full(lng.shape), full(lnb.shape)],
        out_specs=full((rows, D_MODEL)),
        out_shape=jax.ShapeDtypeStruct((rows, D_MODEL), F32),
        compiler_params=pltpu.CompilerParams(vmem_limit_bytes=VMEM_LIMIT),
        name="final_sample",
    )(x2d, yssm, proj_s, o_a, proj_s, proj_s, proj_s, mods, wglu, bglu, wbs, wba, wout, lng, lnb)


def _bucket_table():
    max_exact = N_BUCKETS // 2
    dist = np.arange(WINDOW + 1)
    df = np.maximum(dist, 1).astype(np.float32)
    large = max_exact + (np.log(df / np.float32(max_exact)) / np.float32(math.log(WINDOW / max_exact))
                         * np.float32(N_BUCKETS - max_exact)).astype(np.int32)
    large = np.minimum(large, N_BUCKETS - 1)
    return np.where(dist < max_exact, dist, large)


def _dot3_nt(a, b):
    ah = a.astype(BF16)
    al = (a - ah.astype(F32)).astype(BF16)
    bh = b.astype(BF16)
    bl = (b - bh.astype(F32)).astype(BF16)
    return _dot_nt(ah, bh) + _dot_nt(ah, bl) + _dot_nt(al, bh)


def _ssm_prep_body(lre_ref, lim_ref, ld_ref, tau_ref, bt_ref, bts_ref, c2_ref, c2s_ref, d_ref,
                   lhs_ref, bpow_ref, cpow_ref, bre_ref, bim_ref, cblk_ref, p1_ref, p2_ref):
    lane = lax.broadcasted_iota(jnp.int32, (SSM_GROUP, SW), 1)
    left = lane < SSM_STATE
    sgn = jnp.where(left, 1.0, -1.0)
    row_t = lax.broadcasted_iota(jnp.int32, (CW, 128), 0)
    lane_t = lax.broadcasted_iota(jnp.int32, (CW, 128), 1)
    bre_ref[...] = jnp.zeros_like(bre_ref)
    bim_ref[...] = jnp.zeros_like(bim_ref)
    cblk_ref[...] = jnp.zeros_like(cblk_ref)
    half_sign = jnp.where(lax.broadcasted_iota(jnp.int32, (1, SW), 1) < SSM_STATE, -1.0, 1.0)
    tau = tau_ref[...]
    for gl in range(GT):
        lr = jnp.concatenate([lre_ref[gl:gl + 1, :]] * 2, axis=1)
        li = jnp.concatenate([lim_ref[gl:gl + 1, :]] * 2, axis=1)
        dt = jnp.exp(ld_ref[gl:gl + 1, :])
        mag = jnp.exp((lr * dt) * tau)
        ang = (li * dt) * tau
        p1 = mag * jnp.cos(ang)
        pim = mag * jnp.sin(ang)
        p2 = pim * half_sign
        p1_ref[gl] = p1
        p2_ref[gl] = p2
        ar, ai = p1[1:2, :], pim[1:2, :]
        den = lr * lr + li * li
        nr = ar - 1.0
        k1 = (nr * lr + ai * li) / den
        k2 = (ai * lr - nr * li) / den * half_sign
        bt, bts = bt_ref[gl], bts_ref[gl]
        c2, c2s = c2_ref[gl], c2s_ref[gl]
        bb = k1 * bt + k2 * bts
        bbs = k1 * bts - k2 * bt
        ca = [sgn * (c2 * p1[t:t + 1, :] + c2s * p2[t:t + 1, :]) for t in range(CHUNK + 1)]
        cpow_ref[gl] = jnp.concatenate(ca[1:], axis=0).astype(BF16)
        bpow_ref[gl] = jnp.concatenate(
            [p1[CHUNK - 1 - s:CHUNK - s, :] * bb + p2[CHUNK - 1 - s:CHUNK - s, :] * bbs for s in range(CHUNK)],
            axis=0).astype(BF16)
        e = _dot3_nt(jnp.concatenate(ca[:CHUNK], axis=0), jnp.concatenate([bb] * CHUNK, axis=0))
        tiles = []
        for lt in range(CW // 128):
            cols = slice(lt * 128, (lt + 1) * 128)
            et = e[:, cols]
            blk = (lane_t + lt * 128) // SSM_GROUP
            at = jnp.where(row_t == lane_t + lt * 128, d_ref[gl, :, cols], 0.0)
            for s in range(lt * 128 // SSM_GROUP, (lt + 1) * 128 // SSM_GROUP):
                sh = et if s == 0 else jnp.concatenate(
                    [jnp.zeros((s * SSM_GROUP, 128), F32), et[:CW - s * SSM_GROUP, :]], axis=0)
                at = at + jnp.where(blk == s, sh, 0.0)
            tiles.append(at)
        lhs_ref[gl] = jnp.concatenate(tiles, axis=1).astype(BF16)
        rs = slice(gl * SSM_GROUP, (gl + 1) * SSM_GROUP)
        ts = slice((gl // 2) * SW, (gl // 2 + 1) * SW)
        ti = slice(GT * SSM_STATE + (gl // 2) * SW, GT * SSM_STATE + (gl // 2 + 1) * SW)
        if gl % 2 == 0:
            bre_ref[rs, ts] = jnp.where(left, bb, 0.0).astype(BF16)
            bim_ref[rs, ts] = jnp.where(left, bbs, 0.0).astype(BF16)
            cblk_ref[rs, ts] = jnp.where(left, c2, 0.0).astype(BF16)
            cblk_ref[rs, ti] = jnp.where(left, -c2s, 0.0).astype(BF16)
        else:
            bre_ref[rs, ts] = jnp.where(left, 0.0, bbs).astype(BF16)
            bim_ref[rs, ts] = jnp.where(left, 0.0, bb).astype(BF16)
            cblk_ref[rs, ts] = jnp.where(left, 0.0, c2s).astype(BF16)
            cblk_ref[rs, ti] = jnp.where(left, 0.0, -c2).astype(BF16)


def _ssm_params(lam_re, lam_im, log_delta, b_re, b_im, c_re, c_im, d_skip):
    g = SSM_GROUPS
    nrow = len(POW_TAUS)
    tau = jnp.asarray(np.broadcast_to(np.asarray(POW_TAUS, np.float32)[:, None], (nrow, SW)))
    brt, bit = jnp.swapaxes(b_re, 1, 2), jnp.swapaxes(b_im, 1, 2)
    bt = jnp.concatenate([brt, bit], axis=-1)
    bts = jnp.concatenate([bit, brt], axis=-1)
    c2 = jnp.concatenate([c_re, c_im], axis=-1)
    c2s = jnp.concatenate([c_im, c_re], axis=-1)
    dtile = jnp.tile(d_skip.reshape(g, 1, SSM_GROUP), (1, 1, CHUNK))

    gspec = lambda r, w: pl.BlockSpec((GT, r, w), lambda a: (a, 0, 0))
    rspec = lambda w: pl.BlockSpec((GT, w), lambda a: (a, 0))
    tile_spec = lambda w: pl.BlockSpec((None, GT * SSM_GROUP, w), lambda a: (a, 0, 0))
    lhs, bpow, cpow, bre, bim, cblk, p1, p2 = pl.pallas_call(
        _ssm_prep_body,
        grid=(g // GT,),
        in_specs=[rspec(SSM_STATE), rspec(SSM_STATE), rspec(1), pl.BlockSpec((nrow, SW), lambda a: (0, 0)),
                  gspec(SSM_GROUP, SW), gspec(SSM_GROUP, SW), gspec(SSM_GROUP, SW), gspec(SSM_GROUP, SW),
                  gspec(1, CW)],
        out_specs=[gspec(CW, CW), gspec(CW, SW), gspec(CW, SW),
                   tile_spec(GT * SSM_STATE), tile_spec(GT * SSM_STATE), tile_spec(2 * GT * SSM_STATE),
                   gspec(nrow, SW), gspec(nrow, SW)],
        out_shape=[jax.ShapeDtypeStruct((g, CW, CW), BF16), jax.ShapeDtypeStruct((g, CW, SW), BF16),
                   jax.ShapeDtypeStruct((g, CW, SW), BF16),
                   jax.ShapeDtypeStruct((GT, GT * SSM_GROUP, GT * SSM_STATE), BF16),
                   jax.ShapeDtypeStruct((GT, GT * SSM_GROUP, GT * SSM_STATE), BF16),
                   jax.ShapeDtypeStruct((GT, GT * SSM_GROUP, 2 * GT * SSM_STATE), BF16),
                   jax.ShapeDtypeStruct((g, nrow, SW), F32), jax.ShapeDtypeStruct((g, nrow, SW), F32)],
        name="ssm_prep",
    )(lam_re.astype(F32), lam_im.astype(F32), log_delta.astype(F32).reshape(g, 1), tau, bt, bts, c2, c2s, dtile)
    ar_row = p1[:, 1, :SSM_STATE].reshape(1, -1)
    ai_row = p2[:, 1, SSM_STATE:].reshape(1, -1)
    return lhs, bpow, cpow, p1, p2, ar_row, ai_row, bre, bim, cblk


def _attn_params(rel_bias, sinks):
    hp = lax.Precision.HIGHEST
    rb = rel_bias.astype(F32)
    onehot = jnp.asarray(_bucket_table()[:, None] == np.arange(N_BUCKETS)[None, :], F32)
    tbl = jnp.dot(onehot, rb, precision=hp)
    sink_rows = jnp.repeat(sinks.astype(F32).reshape(N_KV_HEADS, 1, KV_GROUP), WINDOW, axis=2)
    ds = np.clip(WINDOW - np.arange(KROWS), 0, WINDOW)
    bs = jnp.dot(jnp.asarray(ds[:, None] == np.arange(WINDOW + 1)[None, :], F32), tbl, precision=hp)
    bs = jnp.transpose(bs.reshape(KROWS, N_KV_HEADS, KV_GROUP), (0, 2, 1)).reshape(KROWS, N_HEADS)
    bias_s = jnp.pad(bs, ((0, 0), (0, 128 - N_HEADS)))
    sk = jnp.transpose(sinks.astype(F32).reshape(N_KV_HEADS, KV_GROUP), (1, 0)).reshape(1, N_HEADS)
    sink_row = jnp.pad(sk, ((0, 0), (0, 128 - N_HEADS)))
    seg_np = np.zeros((D_MODEL, 128), np.float32)
    for j in range(KV_GROUP):
        for kh in range(N_KV_HEADS):
            r0 = j * KV_WIDTH + kh * HEAD_DIM
            seg_np[r0:r0 + HEAD_DIM, j * N_KV_HEADS + kh] = 1.0
    seg = jnp.asarray(seg_np, BF16)
    segt = jnp.asarray(seg_np.T, BF16)
    return sink_rows, bias_s, sink_row, seg, segt


WP_ROWS = 256


def _regroup_tiles(w):
    left = lax.broadcasted_iota(jnp.int32, (w.shape[0], 128), 1) < HEAD_DIM
    tile = lambda t: w[:, t * 128:(t + 1) * 128]
    out = []
    for j in range(KV_GROUP):
        for pair in range(N_KV_HEADS // 2):
            t1 = tile((2 * pair) * 2 + j // 2)
            t2 = tile((2 * pair + 1) * 2 + j // 2)
            if j % 2 == 0:
                out.append(jnp.where(left, t1, pltpu.roll(t2, HEAD_DIM, 1)))
            else:
                out.append(jnp.where(left, pltpu.roll(t1, HEAD_DIM, 1), t2))
    return jnp.concatenate(out, axis=1)


def _weights_body(win_ref, wglu_ref, wbs_ref, wba_ref, wout_ref,
                  winb_ref, wut_ref, wglub_ref, wbsb_ref, wbab_ref, woutb_ref):
    i = pl.program_id(0)
    seg = lambda a: win_ref[:, a:a + D_MODEL]
    u = seg(0)
    winb_ref[:, 0:D_MODEL] = u.astype(BF16)
    winb_ref[:, D_MODEL:2 * D_MODEL] = seg(D_MODEL).astype(BF16)
    winb_ref[:, 2 * D_MODEL:3 * D_MODEL] = (_regroup_tiles(seg(2 * D_MODEL)) * (HEAD_DIM ** -0.5)).astype(BF16)
    winb_ref[:, 3 * D_MODEL:4 * D_MODEL] = _regroup_tiles(seg(3 * D_MODEL + 2 * KV_WIDTH)).astype(BF16)
    winb_ref[:, 4 * D_MODEL:5 * D_MODEL] = seg(4 * D_MODEL + 2 * KV_WIDTH).astype(BF16)
    winb_ref[:, 5 * D_MODEL:6 * D_MODEL] = seg(5 * D_MODEL + 2 * KV_WIDTH).astype(BF16)
    winb_ref[:, 6 * D_MODEL:6 * D_MODEL + 2 * KV_WIDTH] = \
        win_ref[:, 3 * D_MODEL:3 * D_MODEL + 2 * KV_WIDTH].astype(BF16)
    wut_ref[...] = u.astype(BF16).T
    wglub_ref[...] = wglu_ref[...].astype(BF16)
    wbsb_ref[...] = wbs_ref[...].astype(BF16)
    woutb_ref[...] = wout_ref[...].astype(BF16)
    wba = wba_ref[...].astype(BF16)
    for j in range(KV_GROUP):
        r0 = pl.multiple_of(j * KV_WIDTH + i * HEAD_DIM, HEAD_DIM)
        wbab_ref[pl.ds(r0, HEAD_DIM), :] = wba[j * HEAD_DIM:(j + 1) * HEAD_DIM, :]


def _weights_prep(w_in0, w_glu0, w_bs0, w_ba0, w_out0):
    assert WP_ROWS == KV_GROUP * HEAD_DIM
    rows = lambda w: pl.BlockSpec((WP_ROWS, w), lambda i: (i, 0))
    sq = jax.ShapeDtypeStruct((D_MODEL, D_MODEL), BF16)
    return pl.pallas_call(
        _weights_body,
        grid=(D_MODEL // WP_ROWS,),
        in_specs=[rows(D_IN), rows(D_MODEL), rows(D_MODEL), rows(D_MODEL), rows(D_MODEL)],
        out_specs=[rows(D_IN), pl.BlockSpec((D_MODEL, WP_ROWS), lambda i: (0, i)), rows(D_MODEL), rows(D_MODEL),
                   pl.BlockSpec((D_MODEL, D_MODEL), lambda i: (0, 0)), rows(D_MODEL)],
        out_shape=[jax.ShapeDtypeStruct((D_MODEL, D_IN), BF16), sq, sq, sq, sq, sq],
        compiler_params=pltpu.CompilerParams(dimension_semantics=("arbitrary",), vmem_limit_bytes=VMEM_LIMIT),
        name="weights_prep",
    )(w_in0, w_glu0, w_bs0, w_ba0, w_out0)


def kernel(x_prompt, x_sample, c_prompt, c_sample, state_ssm_re, state_ssm_im, cache_swa_k, cache_swa_v,
           w_ada, b_ada, w_in, ssm_lambda_re, ssm_lambda_im, ssm_log_delta, ssm_b_re, ssm_b_im,
           ssm_c_re, ssm_c_im, ssm_d, w_glu, b_glu, attn_sinks, rel_bias, w_branch_s, w_branch_a,
           w_out, ln_g, ln_b):
    assert w_ada.shape[0] == 1, "single-layer trunk"
    n_batch, seq, _ = x_prompt.shape
    n_dec = x_sample.shape[0]

    w_in_b, wut, wglu, wbs, wba, wout = _weights_prep(w_in[0], w_glu[0], w_branch_s[0], w_branch_a[0], w_out[0])
    bglu = b_glu[0].reshape(1, -1).astype(F32)
    lng = ln_g[0].reshape(1, -1).astype(F32)
    lnb = ln_b[0].reshape(1, -1).astype(F32)
    lhs, bpow, cpow, p1, p2, ar_row, ai_row, bre, bim, cblk = _ssm_params(
        ssm_lambda_re[0], ssm_lambda_im[0], ssm_log_delta[0], ssm_b_re[0], ssm_b_im[0],
        ssm_c_re[0], ssm_c_im[0], ssm_d[0])
    sink_rows, bias_s, sink_row, seg, segt = _attn_params(rel_bias, attn_sinks[0])

    c_all = jnp.concatenate([c_prompt, jnp.zeros((8 - n_batch, D_MODEL), F32), c_sample], axis=0)
    mod = _ada(c_all, w_ada[0], b_ada[0])
    modp = mod[:n_batch].reshape(n_batch * 3, 1, D_MODEL)
    mods = mod[8:8 + n_dec]

    xp = x_prompt.reshape(n_batch * seq, D_MODEL)
    yssm, hfin = _ssm_prompt(x_prompt, modp, wut, lhs, bpow, cpow, p1, p2)
    yp, ktail, vtail = _main_prompt(rel_bias.astype(F32).reshape(-1), xp, yssm, modp, w_in_b, sink_rows,
                                    wglu, bglu, wbs, wba, wout, lng, lnb, n_batch, seq)
    yp = yp.reshape(n_batch, seq, D_MODEL)
    p_hr = hfin[:, :, 0, :SSM_STATE][None]
    p_hi = hfin[:, :, 0, SSM_STATE:][None]
    p_k = ktail.reshape(1, n_batch, WINDOW, N_KV_HEADS, HEAD_DIM)
    p_v = vtail.reshape(1, n_batch, WINDOW, N_KV_HEADS, HEAD_DIM)

    xs = x_sample.reshape(n_dec, D_MODEL)
    proj_s = _proj_sample(xs, mods, w_in_b)
    ns = SSM_GROUPS * SSM_STATE
    ys_s, s_hr, s_hi = _sample_ssm(proj_s, state_ssm_re[0].reshape(n_dec, ns), state_ssm_im[0].reshape(n_dec, ns),
                                   ar_row, ai_row, bre, bim, cblk, ssm_d[0].reshape(1, -1).astype(F32))
    o_s, nk, nv = _sample_attn(proj_s, cache_swa_k[0].reshape(n_dec, WINDOW, KV_WIDTH),
                               cache_swa_v[0].reshape(n_dec, WINDOW, KV_WIDTH), bias_s, sink_row, seg, segt)
    ysmp = _final_sample(xs, ys_s, o_s, proj_s, mods, wglu, bglu, wbs, wba, wout, lng, lnb)

    return (yp, ysmp.reshape(n_dec, 1, D_MODEL),
            p_hr, p_hi, p_k, p_v,
            s_hr.reshape(1, n_dec, SSM_GROUPS, SSM_STATE), s_hi.reshape(1, n_dec, SSM_GROUPS, SSM_STATE),
            nk.reshape(1, n_dec, WINDOW, N_KV_HEADS, HEAD_DIM), nv.reshape(1, n_dec, WINDOW, N_KV_HEADS, HEAD_DIM))
```

```python
import math

import numpy as np
import jax
import jax.numpy as jnp
from jax import lax
from jax.experimental import pallas as pl
from jax.experimental.pallas import tpu as pltpu

F32 = jnp.float32
BF16 = jnp.bfloat16

D_MODEL = 1024
SSM_GROUPS = 64
SSM_GROUP = 16
SSM_STATE = 64
N_HEADS = 16
HEAD_DIM = 64
N_KV_HEADS = 4
KV_GROUP = 4
KV_WIDTH = N_KV_HEADS * HEAD_DIM
WINDOW = 128
N_BUCKETS = 32
NEG_INF = -1e30
LN_EPS = 1e-5
DEPTH = 1
DEEPNORM_ALPHA = (2 * DEPTH) ** 0.25
D_IN = 6656
CHUNK = 16
CW = CHUNK * SSM_GROUP
SW = 2 * SSM_STATE

COL_U, COL_ZS, COL_Q, COL_ZA, COL_GS, COL_GA = 0, 1, 2, 3, 4, 5
COL_K, COL_V = 24, 25

VMEM_LIMIT = 56 * 1024 * 1024
VMEM_LIMIT_SSM = 60 * 1024 * 1024


def _sigmoid(x):
    return 0.5 * jnp.tanh(0.5 * x) + 0.5


def _silu(x):
    return x * _sigmoid(x)


def _gelu_tanh(x):
    c = math.sqrt(2.0 / math.pi)
    hx = 0.5 * x
    return hx + hx * jnp.tanh(x * (c + (c * 0.044715) * (x * x)))


def _dot(a, b):
    return jnp.dot(a, b, preferred_element_type=F32)


def _dot_nt(a, b):
    return lax.dot_general(a, b, (((1,), (1,)), ((), ())), preferred_element_type=F32)


def _ada_body(c_ref, w_ref, b_ref, o_ref):
    c = c_ref[...]
    sc = _silu(c).astype(BF16)
    o_ref[...] = _dot(sc, w_ref[...].astype(BF16)) + b_ref[...]


def _ada(c_all, w_ada, b_ada):
    rows = c_all.shape[0]
    bn = 512
    return pl.pallas_call(
        _ada_body,
        grid=(3 * D_MODEL // bn,),
        in_specs=[pl.BlockSpec((rows, D_MODEL), lambda j: (0, 0)),
                  pl.BlockSpec((D_MODEL, bn), lambda j: (0, j)),
                  pl.BlockSpec((1, bn), lambda j: (0, j))],
        out_specs=pl.BlockSpec((rows, bn), lambda j: (0, j)),
        out_shape=jax.ShapeDtypeStruct((rows, 3 * D_MODEL), F32),
        name="ada",
    )(c_all, w_ada, b_ada.reshape(1, -1))


PROJ_CB = 512


def _proj_body(x_ref, shift_ref, scale_ref, w_ref, o_ref):
    h = x_ref[...] * (1.0 + scale_ref[...]) + shift_ref[...]
    hb = h.astype(BF16)
    for j in range(D_IN // PROJ_CB):
        sl = slice(j * PROJ_CB, (j + 1) * PROJ_CB)
        o_ref[:, sl] = _dot(hb, w_ref[:, sl]).astype(BF16)


def _proj_sample(xs, mods, w_in_b):
    rows = xs.shape[0]
    return pl.pallas_call(
        _proj_body,
        grid=(1,),
        in_specs=[pl.BlockSpec((rows, D_MODEL), lambda i: (0, 0)),
                  pl.BlockSpec((rows, D_MODEL), lambda i: (0, 0)),
                  pl.BlockSpec((rows, D_MODEL), lambda i: (0, 1)),
                  pl.BlockSpec((D_MODEL, D_IN), lambda i: (0, 0))],
        out_specs=pl.BlockSpec((rows, D_IN), lambda i: (0, 0)),
        out_shape=jax.ShapeDtypeStruct((rows, D_IN), BF16),
        compiler_params=pltpu.CompilerParams(vmem_limit_bytes=VMEM_LIMIT),
        name="proj",
    )(xs, mods, mods, w_in_b)


NKS = 128
GC = 16


def _cmul(c1, c2, x):
    return c1 * x + c2 * pltpu.roll(x, SSM_STATE, x.ndim - 1)


SSM_TOK = NKS * CHUNK
SUB = 8
NPH = CHUNK // SUB


def _dot_tn(a, b):
    return lax.dot_general(a, b, (((0,), (0,)), ((), ())), preferred_element_type=F32)


SEG = 8
SEG_LEN = NKS // SEG
POW_TAUS = (tuple(range(CHUNK + 1)) + tuple(CHUNK * m for m in range(2, SEG_LEN))
            + tuple(CHUNK * SEG_LEN * m for m in range(1, SEG + 1)) + (0,))
ROW_CHUNK1 = CHUNK
ROW_SEG1 = CHUNK + SEG_LEN - 1


def _ssm_body(x_ref, shift_ref, scale_ref, wut_ref, lhs_ref, bpow_ref, cpow_ref, p1_ref, p2_ref,
              y_ref, hfin_ref, r_scr, yt_scr, s_scr, carry_scr):
    i = pl.program_id(1)
    j = pl.program_id(2)

    @pl.when((i == 0) & (j == 0))
    def _():
        carry_scr[...] = jnp.zeros_like(carry_scr)

    @pl.when(j < NPH)
    def _():
        pairs = range(SUB // 2)
        hbs = []
        for sp in pairs:
            h = jnp.concatenate([x_ref[:, b, s, :] for s in (2 * sp, 2 * sp + 1) for b in range(SEG_LEN)],
                                axis=0)
            hbs.append((h * (1.0 + scale_ref[...]) + shift_ref[...]).astype(BF16))
        uts = [_dot_nt(wut_ref[...], hb).astype(BF16) for hb in hbs]
        for sp, ut in zip(pairs, uts):
            ut = ut.reshape(SSM_GROUPS, SSM_GROUP, 2 * NKS)
            pair = jnp.concatenate([ut[:, :, :NKS], ut[:, :, NKS:]], axis=1)
            r0 = pl.multiple_of(j * SUB * SSM_GROUP + sp * 2 * SSM_GROUP, 2 * SSM_GROUP)
            r_scr[:, pl.ds(r0, 2 * SSM_GROUP), :] = pair

    @pl.when(j == NPH - 1)
    def _():
        def local(g):
            rg = r_scr[g]
            yt_scr[g] = _dot(lhs_ref[g], rg)
            s_scr[g] = _dot_tn(rg, bpow_ref[g])

        def carried(g):
            yt_scr[g] += _dot_nt(cpow_ref[g], s_scr[g].astype(BF16))

        row = lax.broadcasted_iota(jnp.int32, (GC, SEG, SW), 1)
        seg_rows = lambda b: slice(b * SEG, (b + 1) * SEG)

        def scan(gc):
            gs = slice(gc * GC, (gc + 1) * GC)
            p1, p2 = p1_ref[gs], p2_ref[gs]
            c1, c2 = p1[:, ROW_CHUNK1:ROW_CHUNK1 + 1, :], p2[:, ROW_CHUNK1:ROW_CHUNK1 + 1, :]
            h = jnp.zeros((GC, SEG, SW), F32)
            hs = h
            for b in range(SEG_LEN):
                sb = s_scr[gs, seg_rows(b), :]
                h, hs = sb + c1 * h + c2 * hs, pltpu.roll(sb, SSM_STATE, 2) + c1 * hs - c2 * h
                s_scr[gs, seg_rows(b), :] = h
                yield
            e = h
            for d in (1, 2, 4):
                rd = ROW_SEG1 + d - 1
                sh = jnp.where(row >= d, pltpu.roll(e, d, 1), 0.0)
                e = e + _cmul(p1[:, rd:rd + 1, :], p2[:, rd:rd + 1, :], sh)
            carry = carry_scr[gs]
            e = e + _cmul(p1[:, ROW_SEG1:ROW_SEG1 + SEG, :], p2[:, ROW_SEG1:ROW_SEG1 + SEG, :], carry)
            ein = jnp.where(row == 0, carry, pltpu.roll(e, 1, 1))
            eins = pltpu.roll(ein, SSM_STATE, 2)
            carry_scr[gs] = jnp.broadcast_to(e[:, SEG - 1:SEG, :], (GC, SEG, SW))
            yield
            for b in range(SEG_LEN - 1, 0, -1):
                rb = ROW_CHUNK1 + b - 1
                s_scr[gs, seg_rows(b), :] = (s_scr[gs, seg_rows(b - 1), :]
                                             + p1[:, rb:rb + 1, :] * ein + p2[:, rb:rb + 1, :] * eins)
                yield
            s_scr[gs, seg_rows(0), :] = ein

        def interleave(fn, groups, gen):
            for g in groups:
                fn(g)
                for _ in range(2):
                    next(gen, None)
            for _ in gen:
                pass

        n_sets = SSM_GROUPS // GC
        sets = [range(q * GC, (q + 1) * GC) for q in range(n_sets)]
        for g in sets[0]:
            local(g)
        for q in range(1, n_sets):
            interleave(local, sets[q], scan(q - 1))
        interleave(carried, sets[0], scan(n_sets - 1))
        for q in range(1, n_sets):
            for g in sets[q]:
                carried(g)
        hfin_ref[...] = carry_scr[...]

    @pl.when(j >= NPH)
    def _():
        for sp in range(SUB):
            t0 = pl.multiple_of((j - NPH) * SUB * SSM_GROUP + sp * SSM_GROUP, SSM_GROUP)
            yt = yt_scr[:, pl.ds(t0, SSM_GROUP), :].reshape(D_MODEL, NKS)
            yr = yt.T
            for b in range(SEG_LEN):
                y_ref[:, b, sp, :] = yr[b * SEG:(b + 1) * SEG, :]


def _ssm_prompt(x_prompt, modp, wut, lhs, bpow, cpow, p1, p2):
    n_batch, seq, _ = x_prompt.shape
    steps = seq // SSM_TOK
    xv = x_prompt.reshape(n_batch * steps * SEG, SEG_LEN, CHUNK, D_MODEL)
    mod_spec = lambda k: pl.BlockSpec((None, 1, D_MODEL), lambda b, i, j, k=k: (b * 3 + k, 0, 0))
    consts = (wut, lhs, bpow, cpow, p1, p2)
    y, hfin = pl.pallas_call(
        _ssm_body,
        grid=(n_batch, steps, 2 * NPH),
        in_specs=[pl.BlockSpec((SEG, SEG_LEN, SUB, D_MODEL),
                               lambda b, i, j: (b * steps + i, 0, jnp.minimum(j, NPH - 1), 0)),
                  mod_spec(0), mod_spec(1)] + [_const_spec(c.shape) for c in consts],
        out_specs=[pl.BlockSpec((SEG, SEG_LEN, SUB, D_MODEL),
                                lambda b, i, j: (b * steps + i, 0, jnp.maximum(j - NPH, 0), 0)),
                   pl.BlockSpec((None, SSM_GROUPS, 8, SW), lambda b, i, j: (b, 0, 0, 0))],
        out_shape=[jax.ShapeDtypeStruct(xv.shape, F32),
                   jax.ShapeDtypeStruct((n_batch, SSM_GROUPS, 8, SW), F32)],
        scratch_shapes=[pltpu.VMEM((SSM_GROUPS, CW, NKS), BF16), pltpu.VMEM((SSM_GROUPS, CW, NKS), F32),
                        pltpu.VMEM((SSM_GROUPS, NKS, SW), F32), pltpu.VMEM((SSM_GROUPS, 8, SW), F32)],
        compiler_params=pltpu.CompilerParams(dimension_semantics=("arbitrary", "arbitrary", "arbitrary"),
                                             vmem_limit_bytes=VMEM_LIMIT_SSM),
        name="ssm_prompt",
    )(xv, modp, modp, *consts)
    return y.reshape(n_batch * seq, D_MODEL), hfin


def _post(chunks, gate, wglu_ref, bglu_ref, wbs_ref, wba_ref, wout_ref, lng_ref, lnb_ref):
    data = [load() for load in chunks]
    ys = [_gelu_tanh(d[0]) for d in data]
    glu = [y * _sigmoid(_dot(y.astype(BF16), wglu_ref[...]) + bglu_ref[...]) for y in ys]
    b_s = [_dot(g.astype(BF16) * _silu(d[1]), wbs_ref[...]) for g, d in zip(glu, data)]
    b_a = [_dot(d[2].astype(BF16) * _silu(d[3]), wba_ref[...]) for d in data]
    m = [_sigmoid(d[4]).astype(F32) * s + _sigmoid(d[5]).astype(F32) * a for d, s, a in zip(data, b_s, b_a)]
    out = [_dot(v.astype(BF16), wout_ref[...]) for v in m]
    res = []
    for d, o in zip(data, out):
        r = DEEPNORM_ALPHA * d[6] + gate * o
        mu = jnp.mean(r, axis=-1, keepdims=True)
        rc = r - mu
        var = jnp.mean(rc * rc, axis=-1, keepdims=True)
        res.append(rc * lax.rsqrt(var + LN_EPS) * lng_ref[...] + lnb_ref[...])
    return res


TB = 512
NQB = TB // WINDOW
W_OFF = D_MODEL
ONES_ROWS = 16
POST_ROWS = 256
BLK_GROUP = 1
PROJ_W = 256


def _const_spec(shape):
    nd = len(shape)
    return pl.BlockSpec(shape, lambda *_: (0,) * nd, pipeline_mode=pl.Buffered(1))


def _bucket_thresholds():
    tbl = _bucket_table()
    return [int(np.argmax(tbl >= k)) for k in range(1, N_BUCKETS)]


def _fill_bias_table(rb_ref, bias_s):
    si = lax.broadcasted_iota(jnp.int32, (2 * WINDOW, WINDOW), 0)
    qi = lax.broadcasted_iota(jnp.int32, (2 * WINDOW, WINDOW), 1)
    dist = qi + WINDOW - si
    bid = jnp.zeros((2 * WINDOW, WINDOW), jnp.int32)
    for thr in _bucket_thresholds():
        bid = bid + (dist >= thr).astype(jnp.int32)
    valid = (dist >= 0) & (dist <= WINDOW)
    own = si >= WINDOW

    def per_head(h, c):
        t = jnp.zeros((2 * WINDOW, WINDOW), F32)
        for b in range(N_BUCKETS):
            t = jnp.where(bid == b, rb_ref[b * N_HEADS + h], t)
        t = jnp.where(valid, t, NEG_INF)
        kh, j = h // KV_GROUP, h % KV_GROUP
        bias_s[0, kh, j] = t
        bias_s[1, kh, j] = jnp.where(own, t, NEG_INF)
        return c

    lax.fori_loop(0, N_HEADS, per_head, 0)


def _main_body(rb_ref, x_ref, yssm_ref, shift_ref, scale_ref, gate_ref, wmain_ref, sink_ref,
               wglu_ref, bglu_ref, wbs_ref, wba_ref, wout_ref, lng_ref, lnb_ref,
               y_ref, ktail_ref, vtail_ref,
               zs_s, q_s, za_s, gs_s, ga_s, k_s, vt_s, ot_s, o_s, bias_s):
    i = pl.program_id(1)

    @pl.when((pl.program_id(0) == 0) & (i == 0))
    def _():
        _fill_bias_table(rb_ref, bias_s)

    @pl.when(i == 0)
    def _():
        k_s[0:WINDOW, :] = jnp.zeros((WINDOW, KV_WIDTH), BF16)
        vt_s[:, 0:WINDOW] = jnp.zeros((KV_WIDTH, WINDOW), BF16)

    x = x_ref[...]
    hb = (x * (1.0 + scale_ref[...]) + shift_ref[...]).astype(BF16)

    def project(dst, c0, c, width=PROJ_W):
        lo = W_OFF + c0 + c * width
        dst[:, c * width:(c + 1) * width] = _dot(hb, wmain_ref[:, lo:lo + width]).astype(BF16)

    for c in range(D_MODEL // 512):
        project(q_s, D_MODEL, c, 512)
    deferred = [(dst, c0, c) for dst, c0 in ((zs_s, 0), (za_s, 2 * D_MODEL), (gs_s, 3 * D_MODEL), (ga_s, 4 * D_MODEL))
                for c in range(D_MODEL // PROJ_W)]

    kv = _dot(hb, wmain_ref[:, W_OFF + 5 * D_MODEL:W_OFF + 5 * D_MODEL + 2 * KV_WIDTH]).astype(BF16)
    k_s[WINDOW:WINDOW + TB, :] = kv[:, :KV_WIDTH]
    vt_s[:, WINDOW:WINDOW + TB] = kv[:, KV_WIDTH:].T

    first = jnp.where(i == 0, 1, 0)
    lane = lax.broadcasted_iota(jnp.int32, (2 * WINDOW, 128), 1)
    ones = jnp.ones((ONES_ROWS, 2 * WINDOW), BF16)
    n_def = len(deferred) * BLK_GROUP // NQB
    for grp in range(NQB // BLK_GROUP):
        units = [(blk, kh) for blk in range(grp * BLK_GROUP, (grp + 1) * BLK_GROUP) for kh in range(N_KV_HEADS)]
        mine = deferred[grp * n_def:(grp + 1) * n_def]
        sts = []
        for blk, kh in units:
            rows = slice(blk * WINDOW, (blk + 1) * WINDOW)
            kcat = k_s[blk * WINDOW:blk * WINDOW + 2 * WINDOW, :]
            pair, odd = kh // 2, kh % 2
            kcm = jnp.where((lane >= HEAD_DIM) == bool(odd), kcat[:, pair * 128:(pair + 1) * 128], jnp.zeros((), BF16))
            qp = jnp.concatenate(
                [q_s[rows, j * KV_WIDTH + pair * 128:j * KV_WIDTH + (pair + 1) * 128] for j in range(KV_GROUP)],
                axis=0)
            sel = first if blk == 0 else 0
            bias = jnp.concatenate([bias_s[sel, kh, j] for j in range(KV_GROUP)], axis=1)
            sts.append(_dot_nt(kcm, qp) + bias)
        share = [mine[k * len(mine) // 3:(k + 1) * len(mine) // 3] for k in range(3)]
        ms = [jnp.maximum(jnp.max(st, axis=0, keepdims=True), sink_ref[kh]) for st, (_, kh) in zip(sts, units)]
        for spec in share[0]:
            project(*spec)
        es = [jnp.exp(st - m).astype(BF16) for st, m in zip(sts, ms)]
        for spec in share[1]:
            project(*spec)
        rs = []
        for e, (blk, kh) in zip(es, units):
            vtc = vt_s[kh * HEAD_DIM:(kh + 1) * HEAD_DIM, blk * WINDOW:blk * WINDOW + 2 * WINDOW]
            rs.append(_dot(jnp.concatenate([vtc, ones], axis=0), e))
        for spec in share[2]:
            project(*spec)
        for r, m, (blk, kh) in zip(rs, ms, units):
            den = r[HEAD_DIM:HEAD_DIM + 1, :] + jnp.exp(sink_ref[kh] - m)
            ot = (r[:HEAD_DIM, :] / den).astype(BF16)
            for j in range(KV_GROUP):
                ot_s[blk, j, kh * HEAD_DIM:(kh + 1) * HEAD_DIM, :] = ot[:, j * WINDOW:(j + 1) * WINDOW]
        for blk in range(grp * BLK_GROUP, (grp + 1) * BLK_GROUP):
            for j in range(KV_GROUP):
                o_s[blk * WINDOW:(blk + 1) * WINDOW, j * KV_WIDTH:(j + 1) * KV_WIDTH] = ot_s[blk, j].T

    k_s[0:WINDOW, :] = k_s[TB:TB + WINDOW, :]
    vt_s[:, 0:WINDOW] = vt_s[:, TB:TB + WINDOW]

    def loader(c):
        rs = slice(c * POST_ROWS, (c + 1) * POST_ROWS)
        return lambda: (yssm_ref[rs, :], zs_s[rs, :], o_s[rs, :], za_s[rs, :], gs_s[rs, :], ga_s[rs, :], x_ref[rs, :])

    ys = _post([loader(c) for c in range(TB // POST_ROWS)], gate_ref[...],
               wglu_ref, bglu_ref, wbs_ref, wba_ref, wout_ref, lng_ref, lnb_ref)
    for c, yc in enumerate(ys):
        y_ref[c * POST_ROWS:(c + 1) * POST_ROWS, :] = yc

    @pl.when(i == pl.num_programs(1) - 1)
    def _():
        ktail_ref[...] = k_s[0:WINDOW, :].astype(F32)
        vtail_ref[...] = vt_s[:, 0:WINDOW].T.astype(F32)


def _main_prompt(rel_bias_flat, x2d, yssm, modp, wmain, sink_rows, wglu, bglu, wbs, wba, wout, lng, lnb,
                 n_batch, seq):
    steps = seq // TB
    row_spec = pl.BlockSpec((TB, D_MODEL), lambda b, i: (b * steps + i, 0))
    mod_spec = lambda j: pl.BlockSpec((None, 1, D_MODEL), lambda b, i, j=j: (b * 3 + j, 0, 0))
    tail_spec = pl.BlockSpec((None, WINDOW, KV_WIDTH), lambda b, i: (b, 0, 0))
    consts = (wmain, sink_rows, wglu, bglu, wbs, wba, wout, lng, lnb)
    return pl.pallas_call(
        _main_body,
        grid=(n_batch, steps),
        in_specs=[pl.BlockSpec(memory_space=pltpu.SMEM), row_spec, row_spec, mod_spec(0), mod_spec(1), mod_spec(2)]
        + [_const_spec(c.shape) for c in consts],
        out_specs=[row_spec, tail_spec, tail_spec],
        out_shape=[jax.ShapeDtypeStruct((n_batch * seq, D_MODEL), F32),
                   jax.ShapeDtypeStruct((n_batch, WINDOW, KV_WIDTH), F32),
                   jax.ShapeDtypeStruct((n_batch, WINDOW, KV_WIDTH), F32)],
        scratch_shapes=[pltpu.VMEM((TB, D_MODEL), BF16)] * 5
        + [pltpu.VMEM((TB + WINDOW, KV_WIDTH), BF16), pltpu.VMEM((KV_WIDTH, TB + WINDOW), BF16),
           pltpu.VMEM((NQB, KV_GROUP, KV_WIDTH, WINDOW), BF16), pltpu.VMEM((TB, D_MODEL), BF16),
           pltpu.VMEM((2, N_KV_HEADS, KV_GROUP, 2 * WINDOW, WINDOW), F32)],
        compiler_params=pltpu.CompilerParams(dimension_semantics=("arbitrary", "arbitrary"),
                                             vmem_limit_bytes=VMEM_LIMIT),
        name="main_prompt",
    )(rel_bias_flat, x2d, yssm, modp, modp, modp, *consts)


GT = 8


def _sample_ssm_body(u_ref, h0r_ref, h0i_ref, ar_ref, ai_ref, bre_ref, bim_ref, cblk_ref, d_ref,
                     y_ref, hr_ref, hi_ref):
    ub = u_ref[...]
    ys = []
    for a in range(GT):
        ua = ub[:, a * 128:(a + 1) * 128]
        st = slice(a * 512, (a + 1) * 512)
        xr = _dot(ua, bre_ref[a])
        xi = _dot(ua, bim_ref[a])
        h0r = h0r_ref[:, st]
        h0i = h0i_ref[:, st]
        ar = ar_ref[:, st]
        ai = ai_ref[:, st]
        hr = ar * h0r - ai * h0i + xr
        hi = ar * h0i + ai * h0r + xi
        hr_ref[:, st] = hr
        hi_ref[:, st] = hi
        hcat = jnp.concatenate([hr, hi], axis=1).astype(BF16)
        ys.append(_dot_nt(hcat, cblk_ref[a]))
    y_ref[...] = jnp.concatenate(ys, axis=1) + d_ref[...] * ub.astype(F32)


def _sample_ssm(proj_s, h0r, h0i, ar, ai, bre, bim, cblk, d_row):
    rows = proj_s.shape[0]
    ns = SSM_GROUPS * SSM_STATE
    full = lambda shape: pl.BlockSpec(shape, lambda i: (0,) * len(shape))
    return pl.pallas_call(
        _sample_ssm_body,
        grid=(1,),
        in_specs=[pl.BlockSpec((rows, D_MODEL), lambda i: (0, COL_U)),
                  full((rows, ns)), full((rows, ns)), full((1, ns)), full((1, ns)),
                  full(bre.shape), full(bim.shape), full(cblk.shape), full((1, D_MODEL))],
        out_specs=[full((rows, D_MODEL)), full((rows, ns)), full((rows, ns))],
        out_shape=[jax.ShapeDtypeStruct((rows, D_MODEL), F32),
                   jax.ShapeDtypeStruct((rows, ns), F32),
                   jax.ShapeDtypeStruct((rows, ns), F32)],
        compiler_params=pltpu.CompilerParams(vmem_limit_bytes=VMEM_LIMIT),
        name="sample_ssm",
    )(proj_s, h0r, h0i, ar, ai, bre, bim, cblk, d_row)


SB = 32
KROWS = WINDOW + 8
SEQ_GROUP = 4


def _sample_attn_body(q_ref, kn_ref, vn_ref, ck_ref, cv_ref, bias_ref, sink_ref, seg_ref, segt_ref,
                      o_ref, nk_ref, nv_ref):
    qf = q_ref[...].astype(F32)
    knf = kn_ref[...].astype(F32)
    vnf = vn_ref[...].astype(F32)
    rowk = lax.broadcasted_iota(jnp.int32, (KROWS, 128), 0)
    roww = lax.broadcasted_iota(jnp.int32, (WINDOW, KV_WIDTH), 0)
    sink = sink_ref[...]
    last = roww == WINDOW - 1
    for g0 in range(0, SB, SEQ_GROUP):
        seqs = range(g0, g0 + SEQ_GROUP)
        scores = []
        for b in seqs:
            kall = jnp.concatenate([ck_ref[b], jnp.broadcast_to(knf[b:b + 1], (8, KV_WIDTH))], axis=0)
            qrow = qf[b:b + 1]
            prod = jnp.concatenate([kall * qrow[:, j * KV_WIDTH:(j + 1) * KV_WIDTH] for j in range(KV_GROUP)], axis=1)
            s = _dot(prod.astype(BF16), seg_ref[...]) + bias_ref[...]
            scores.append(jnp.where(rowk <= WINDOW, s, NEG_INF))
        probs = []
        for s in scores:
            m = jnp.maximum(jnp.max(s, axis=0, keepdims=True), sink)
            e = jnp.exp(s - m)
            den = jnp.sum(e, axis=0, keepdims=True) + jnp.exp(sink - m)
            probs.append((e / den).astype(BF16))
        pexps = [_dot(p, segt_ref[...]) for p in probs]
        for b, pexp in zip(seqs, pexps):
            vb = cv_ref[b]
            vall = jnp.concatenate([vb, jnp.broadcast_to(vnf[b:b + 1], (8, KV_WIDTH))], axis=0)
            v4 = jnp.concatenate([vall] * KV_GROUP, axis=1)
            o_ref[b:b + 1, :] = jnp.sum(pexp * v4, axis=0, keepdims=True)
            nv_ref[b] = jnp.where(last, jnp.broadcast_to(vnf[b:b + 1], (WINDOW, KV_WIDTH)), pltpu.roll(vb, WINDOW - 1, 0))
            nk_ref[b] = jnp.where(last, jnp.broadcast_to(knf[b:b + 1], (WINDOW, KV_WIDTH)),
                                  pltpu.roll(ck_ref[b], WINDOW - 1, 0))


def _sample_attn(proj_s, ck, cv, bias_s, sink_row, seg, segt):
    rows = proj_s.shape[0]
    cache_spec = pl.BlockSpec((SB, WINDOW, KV_WIDTH), lambda i: (i, 0, 0))
    return pl.pallas_call(
        _sample_attn_body,
        grid=(rows // SB,),
        in_specs=[pl.BlockSpec((SB, D_MODEL), lambda i: (i, COL_Q)),
                  pl.BlockSpec((SB, KV_WIDTH), lambda i: (i, COL_K)),
                  pl.BlockSpec((SB, KV_WIDTH), lambda i: (i, COL_V)),
                  cache_spec, cache_spec,
                  pl.BlockSpec(bias_s.shape, lambda i: (0, 0)),
                  pl.BlockSpec(sink_row.shape, lambda i: (0, 0)),
                  pl.BlockSpec(seg.shape, lambda i: (0, 0)),
                  pl.BlockSpec(segt.shape, lambda i: (0, 0))],
        out_specs=[pl.BlockSpec((SB, D_MODEL), lambda i: (i, 0)), cache_spec, cache_spec],
        out_shape=[jax.ShapeDtypeStruct((rows, D_MODEL), F32),
                   jax.ShapeDtypeStruct(ck.shape, F32),
                   jax.ShapeDtypeStruct(cv.shape, F32)],
        compiler_params=pltpu.CompilerParams(vmem_limit_bytes=VMEM_LIMIT),
        name="sample_attn",
    )(proj_s, proj_s, proj_s, ck, cv, bias_s, sink_row, seg, segt)


def _sample_final_body(x_ref, yssm_ref, zs_ref, o_ref, za_ref, gs_ref, ga_ref, gate_ref,
                       wglu_ref, bglu_ref, wbs_ref, wba_ref, wout_ref, lng_ref, lnb_ref, y_ref):
    load = lambda: (yssm_ref[...], zs_ref[...], o_ref[...], za_ref[...], gs_ref[...], ga_ref[...], x_ref[...])
    y_ref[...] = _post([load], gate_ref[...], wglu_ref, bglu_ref, wbs_ref, wba_ref, wout_ref, lng_ref, lnb_ref)[0]


def _final_sample(x2d, yssm, o_a, proj_s, mods, wglu, bglu, wbs, wba, wout, lng, lnb):
    rows = x2d.shape[0]
    full = lambda shape: pl.BlockSpec(shape, lambda i: (0,) * len(shape))
    pcol = lambda c: pl.BlockSpec((rows, D_MODEL), lambda i, c=c: (0, c))
    return pl.pallas_call(
        _sample_final_body,
        grid=(1,),
        in_specs=[full((rows, D_MODEL)), full((rows, D_MODEL)), pcol(COL_ZS), full((rows, D_MODEL)),
                  pcol(COL_ZA), pcol(COL_GS), pcol(COL_GA),
                  pl.BlockSpec((rows, D_MODEL), lambda i: (0, 2)),
                  full(wglu.shape), full(bglu.shape), full(wbs.shape), full(wba.shape), full(wout.shape),
                  full(lng.shape), full(lnb.shape)],
        out_specs=full((rows, D_MODEL)),
        out_shape=jax.ShapeDtypeStruct((rows, D_MODEL), F32),
        compiler_params=pltpu.CompilerParams(vmem_limit_bytes=VMEM_LIMIT),
        name="final_sample",
    )(x2d, yssm, proj_s, o_a, proj_s, proj_s, proj_s, mods, wglu, bglu, wbs, wba, wout, lng, lnb)


def _bucket_table():
    max_exact = N_BUCKETS // 2
    dist = np.arange(WINDOW + 1)
    df = np.maximum(dist, 1).astype(np.float32)
    large = max_exact + (np.log(df / np.float32(max_exact)) / np.float32(math.log(WINDOW / max_exact))
                         * np.float32(N_BUCKETS - max_exact)).astype(np.int32)
    large = np.minimum(large, N_BUCKETS - 1)
    return np.where(dist < max_exact, dist, large)


def _dot3_nt(a, b):
    ah = a.astype(BF16)
    al = (a - ah.astype(F32)).astype(BF16)
    bh = b.astype(BF16)
    bl = (b - bh.astype(F32)).astype(BF16)
    return _dot_nt(ah, bh) + _dot_nt(ah, bl) + _dot_nt(al, bh)


def _ssm_prep_body(lre_ref, lim_ref, ld_ref, tau_ref, bt_ref, bts_ref, c2_ref, c2s_ref, d_ref,
                   lhs_ref, bpow_ref, cpow_ref, bre_ref, bim_ref, cblk_ref, p1_ref, p2_ref):
    lane = lax.broadcasted_iota(jnp.int32, (SSM_GROUP, SW), 1)
    left = lane < SSM_STATE
    sgn = jnp.where(left, 1.0, -1.0)
    row_t = lax.broadcasted_iota(jnp.int32, (CW, 128), 0)
    lane_t = lax.broadcasted_iota(jnp.int32, (CW, 128), 1)
    bre_ref[...] = jnp.zeros_like(bre_ref)
    bim_ref[...] = jnp.zeros_like(bim_ref)
    cblk_ref[...] = jnp.zeros_like(cblk_ref)
    half_sign = jnp.where(lax.broadcasted_iota(jnp.int32, (1, SW), 1) < SSM_STATE, -1.0, 1.0)
    tau = tau_ref[...]
    for gl in range(GT):
        lr = jnp.concatenate([lre_ref[gl:gl + 1, :]] * 2, axis=1)
        li = jnp.concatenate([lim_ref[gl:gl + 1, :]] * 2, axis=1)
        dt = jnp.exp(ld_ref[gl:gl + 1, :])
        mag = jnp.exp((lr * dt) * tau)
        ang = (li * dt) * tau
        p1 = mag * jnp.cos(ang)
        pim = mag * jnp.sin(ang)
        p2 = pim * half_sign
        p1_ref[gl] = p1
        p2_ref[gl] = p2
        ar, ai = p1[1:2, :], pim[1:2, :]
        den = lr * lr + li * li
        nr = ar - 1.0
        k1 = (nr * lr + ai * li) / den
        k2 = (ai * lr - nr * li) / den * half_sign
        bt, bts = bt_ref[gl], bts_ref[gl]
        c2, c2s = c2_ref[gl], c2s_ref[gl]
        bb = k1 * bt + k2 * bts
        bbs = k1 * bts - k2 * bt
        ca = [sgn * (c2 * p1[t:t + 1, :] + c2s * p2[t:t + 1, :]) for t in range(CHUNK + 1)]
        cpow_ref[gl] = jnp.concatenate(ca[1:], axis=0).astype(BF16)
        bpow_ref[gl] = jnp.concatenate(
            [p1[CHUNK - 1 - s:CHUNK - s, :] * bb + p2[CHUNK - 1 - s:CHUNK - s, :] * bbs for s in range(CHUNK)],
            axis=0).astype(BF16)
        e = _dot3_nt(jnp.concatenate(ca[:CHUNK], axis=0), jnp.concatenate([bb] * CHUNK, axis=0))
        tiles = []
        for lt in range(CW // 128):
            cols = slice(lt * 128, (lt + 1) * 128)
            et = e[:, cols]
            blk = (lane_t + lt * 128) // SSM_GROUP
            at = jnp.where(row_t == lane_t + lt * 128, d_ref[gl, :, cols], 0.0)
            for s in range(lt * 128 // SSM_GROUP, (lt + 1) * 128 // SSM_GROUP):
                sh = et if s == 0 else jnp.concatenate(
                    [jnp.zeros((s * SSM_GROUP, 128), F32), et[:CW - s * SSM_GROUP, :]], axis=0)
                at = at + jnp.where(blk == s, sh, 0.0)
            tiles.append(at)
        lhs_ref[gl] = jnp.concatenate(tiles, axis=1).astype(BF16)
        rs = slice(gl * SSM_GROUP, (gl + 1) * SSM_GROUP)
        ts = slice((gl // 2) * SW, (gl // 2 + 1) * SW)
        ti = slice(GT * SSM_STATE + (gl // 2) * SW, GT * SSM_STATE + (gl // 2 + 1) * SW)
        if gl % 2 == 0:
            bre_ref[rs, ts] = jnp.where(left, bb, 0.0).astype(BF16)
            bim_ref[rs, ts] = jnp.where(left, bbs, 0.0).astype(BF16)
            cblk_ref[rs, ts] = jnp.where(left, c2, 0.0).astype(BF16)
            cblk_ref[rs, ti] = jnp.where(left, -c2s, 0.0).astype(BF16)
        else:
            bre_ref[rs, ts] = jnp.where(left, 0.0, bbs).astype(BF16)
            bim_ref[rs, ts] = jnp.where(left, 0.0, bb).astype(BF16)
            cblk_ref[rs, ts] = jnp.where(left, 0.0, c2s).astype(BF16)
            cblk_ref[rs, ti] = jnp.where(left, 0.0, -c2).astype(BF16)


def _ssm_params(lam_re, lam_im, log_delta, b_re, b_im, c_re, c_im, d_skip):
    g = SSM_GROUPS
    nrow = len(POW_TAUS)
    tau = jnp.asarray(np.broadcast_to(np.asarray(POW_TAUS, np.float32)[:, None], (nrow, SW)))
    brt, bit = jnp.swapaxes(b_re, 1, 2), jnp.swapaxes(b_im, 1, 2)
    bt = jnp.concatenate([brt, bit], axis=-1)
    bts = jnp.concatenate([bit, brt], axis=-1)
    c2 = jnp.concatenate([c_re, c_im], axis=-1)
    c2s = jnp.concatenate([c_im, c_re], axis=-1)
    dtile = jnp.tile(d_skip.reshape(g, 1, SSM_GROUP), (1, 1, CHUNK))

    gspec = lambda r, w: pl.BlockSpec((GT, r, w), lambda a: (a, 0, 0))
    rspec = lambda w: pl.BlockSpec((GT, w), lambda a: (a, 0))
    tile_spec = lambda w: pl.BlockSpec((None, GT * SSM_GROUP, w), lambda a: (a, 0, 0))
    lhs, bpow, cpow, bre, bim, cblk, p1, p2 = pl.pallas_call(
        _ssm_prep_body,
        grid=(g // GT,),
        in_specs=[rspec(SSM_STATE), rspec(SSM_STATE), rspec(1), pl.BlockSpec((nrow, SW), lambda a: (0, 0)),
                  gspec(SSM_GROUP, SW), gspec(SSM_GROUP, SW), gspec(SSM_GROUP, SW), gspec(SSM_GROUP, SW),
                  gspec(1, CW)],
        out_specs=[gspec(CW, CW), gspec(CW, SW), gspec(CW, SW),
                   tile_spec(GT * SSM_STATE), tile_spec(GT * SSM_STATE), tile_spec(2 * GT * SSM_STATE),
                   gspec(nrow, SW), gspec(nrow, SW)],
        out_shape=[jax.ShapeDtypeStruct((g, CW, CW), BF16), jax.ShapeDtypeStruct((g, CW, SW), BF16),
                   jax.ShapeDtypeStruct((g, CW, SW), BF16),
                   jax.ShapeDtypeStruct((GT, GT * SSM_GROUP, GT * SSM_STATE), BF16),
                   jax.ShapeDtypeStruct((GT, GT * SSM_GROUP, GT * SSM_STATE), BF16),
                   jax.ShapeDtypeStruct((GT, GT * SSM_GROUP, 2 * GT * SSM_STATE), BF16),
                   jax.ShapeDtypeStruct((g, nrow, SW), F32), jax.ShapeDtypeStruct((g, nrow, SW), F32)],
        name="ssm_prep",
    )(lam_re.astype(F32), lam_im.astype(F32), log_delta.astype(F32).reshape(g, 1), tau, bt, bts, c2, c2s, dtile)
    ar_row = p1[:, 1, :SSM_STATE].reshape(1, -1)
    ai_row = p2[:, 1, SSM_STATE:].reshape(1, -1)
    return lhs, bpow, cpow, p1, p2, ar_row, ai_row, bre, bim, cblk


def _attn_params(rel_bias, sinks):
    hp = lax.Precision.HIGHEST
    rb = rel_bias.astype(F32)
    onehot = jnp.asarray(_bucket_table()[:, None] == np.arange(N_BUCKETS)[None, :], F32)
    tbl = jnp.dot(onehot, rb, precision=hp)
    sink_rows = jnp.repeat(sinks.astype(F32).reshape(N_KV_HEADS, 1, KV_GROUP), WINDOW, axis=2)
    ds = np.clip(WINDOW - np.arange(KROWS), 0, WINDOW)
    bs = jnp.dot(jnp.asarray(ds[:, None] == np.arange(WINDOW + 1)[None, :], F32), tbl, precision=hp)
    bs = jnp.transpose(bs.reshape(KROWS, N_KV_HEADS, KV_GROUP), (0, 2, 1)).reshape(KROWS, N_HEADS)
    bias_s = jnp.pad(bs, ((0, 0), (0, 128 - N_HEADS)))
    sk = jnp.transpose(sinks.astype(F32).reshape(N_KV_HEADS, KV_GROUP), (1, 0)).reshape(1, N_HEADS)
    sink_row = jnp.pad(sk, ((0, 0), (0, 128 - N_HEADS)))
    seg_np = np.zeros((D_MODEL, 128), np.float32)
    for j in range(KV_GROUP):
        for kh in range(N_KV_HEADS):
            r0 = j * KV_WIDTH + kh * HEAD_DIM
            seg_np[r0:r0 + HEAD_DIM, j * N_KV_HEADS + kh] = 1.0
    seg = jnp.asarray(seg_np, BF16)
    segt = jnp.asarray(seg_np.T, BF16)
    return sink_rows, bias_s, sink_row, seg, segt


WP_ROWS = 256


def _regroup_tiles(w):
    left = lax.broadcasted_iota(jnp.int32, (w.shape[0], 128), 1) < HEAD_DIM
    tile = lambda t: w[:, t * 128:(t + 1) * 128]
    out = []
    for j in range(KV_GROUP):
        for pair in range(N_KV_HEADS // 2):
            t1 = tile((2 * pair) * 2 + j // 2)
            t2 = tile((2 * pair + 1) * 2 + j // 2)
            if j % 2 == 0:
                out.append(jnp.where(left, t1, pltpu.roll(t2, HEAD_DIM, 1)))
            else:
                out.append(jnp.where(left, pltpu.roll(t1, HEAD_DIM, 1), t2))
    return jnp.concatenate(out, axis=1)


def _weights_body(win_ref, wglu_ref, wbs_ref, wba_ref, wout_ref,
                  winb_ref, wut_ref, wglub_ref, wbsb_ref, wbab_ref, woutb_ref):
    i = pl.program_id(0)
    seg = lambda a: win_ref[:, a:a + D_MODEL]
    u = seg(0)
    winb_ref[:, 0:D_MODEL] = u.astype(BF16)
    winb_ref[:, D_MODEL:2 * D_MODEL] = seg(D_MODEL).astype(BF16)
    winb_ref[:, 2 * D_MODEL:3 * D_MODEL] = (_regroup_tiles(seg(2 * D_MODEL)) * (HEAD_DIM ** -0.5)).astype(BF16)
    winb_ref[:, 3 * D_MODEL:4 * D_MODEL] = _regroup_tiles(seg(3 * D_MODEL + 2 * KV_WIDTH)).astype(BF16)
    winb_ref[:, 4 * D_MODEL:5 * D_MODEL] = seg(4 * D_MODEL + 2 * KV_WIDTH).astype(BF16)
    winb_ref[:, 5 * D_MODEL:6 * D_MODEL] = seg(5 * D_MODEL + 2 * KV_WIDTH).astype(BF16)
    winb_ref[:, 6 * D_MODEL:6 * D_MODEL + 2 * KV_WIDTH] = \
        win_ref[:, 3 * D_MODEL:3 * D_MODEL + 2 * KV_WIDTH].astype(BF16)
    wut_ref[...] = u.astype(BF16).T
    wglub_ref[...] = wglu_ref[...].astype(BF16)
    wbsb_ref[...] = wbs_ref[...].astype(BF16)
    woutb_ref[...] = wout_ref[...].astype(BF16)
    wba = wba_ref[...].astype(BF16)
    for j in range(KV_GROUP):
        r0 = pl.multiple_of(j * KV_WIDTH + i * HEAD_DIM, HEAD_DIM)
        wbab_ref[pl.ds(r0, HEAD_DIM), :] = wba[j * HEAD_DIM:(j + 1) * HEAD_DIM, :]


def _weights_prep(w_in0, w_glu0, w_bs0, w_ba0, w_out0):
    assert WP_ROWS == KV_GROUP * HEAD_DIM
    rows = lambda w: pl.BlockSpec((WP_ROWS, w), lambda i: (i, 0))
    sq = jax.ShapeDtypeStruct((D_MODEL, D_MODEL), BF16)
    return pl.pallas_call(
        _weights_body,
        grid=(D_MODEL // WP_ROWS,),
        in_specs=[rows(D_IN), rows(D_MODEL), rows(D_MODEL), rows(D_MODEL), rows(D_MODEL)],
        out_specs=[rows(D_IN), pl.BlockSpec((D_MODEL, WP_ROWS), lambda i: (0, i)), rows(D_MODEL), rows(D_MODEL),
                   pl.BlockSpec((D_MODEL, D_MODEL), lambda i: (0, 0)), rows(D_MODEL)],
        out_shape=[jax.ShapeDtypeStruct((D_MODEL, D_IN), BF16), sq, sq, sq, sq, sq],
        compiler_params=pltpu.CompilerParams(dimension_semantics=("arbitrary",), vmem_limit_bytes=VMEM_LIMIT),
        name="weights_prep",
    )(w_in0, w_glu0, w_bs0, w_ba0, w_out0)


def kernel(x_prompt, x_sample, c_prompt, c_sample, state_ssm_re, state_ssm_im, cache_swa_k, cache_swa_v,
           w_ada, b_ada, w_in, ssm_lambda_re, ssm_lambda_im, ssm_log_delta, ssm_b_re, ssm_b_im,
           ssm_c_re, ssm_c_im, ssm_d, w_glu, b_glu, attn_sinks, rel_bias, w_branch_s, w_branch_a,
           w_out, ln_g, ln_b):
    assert w_ada.shape[0] == 1, "single-layer trunk"
    n_batch, seq, _ = x_prompt.shape
    n_dec = x_sample.shape[0]

    w_in_b, wut, wglu, wbs, wba, wout = _weights_prep(w_in[0], w_glu[0], w_branch_s[0], w_branch_a[0], w_out[0])
    bglu = b_glu[0].reshape(1, -1).astype(F32)
    lng = ln_g[0].reshape(1, -1).astype(F32)
    lnb = ln_b[0].reshape(1, -1).astype(F32)
    lhs, bpow, cpow, p1, p2, ar_row, ai_row, bre, bim, cblk = _ssm_params(
        ssm_lambda_re[0], ssm_lambda_im[0], ssm_log_delta[0], ssm_b_re[0], ssm_b_im[0],
        ssm_c_re[0], ssm_c_im[0], ssm_d[0])
    sink_rows, bias_s, sink_row, seg, segt = _attn_params(rel_bias, attn_sinks[0])

    c_all = jnp.concatenate([c_prompt, jnp.zeros((8 - n_batch, D_MODEL), F32), c_sample], axis=0)
    mod = _ada(c_all, w_ada[0], b_ada[0])
    modp = mod[:n_batch].reshape(n_batch * 3, 1, D_MODEL)
    mods = mod[8:8 + n_dec]

    xp = x_prompt.reshape(n_batch * seq, D_MODEL)
    yssm, hfin = _ssm_prompt(x_prompt, modp, wut, lhs, bpow, cpow, p1, p2)
    yp, ktail, vtail = _main_prompt(rel_bias.astype(F32).reshape(-1), xp, yssm, modp, w_in_b, sink_rows,
                                    wglu, bglu, wbs, wba, wout, lng, lnb, n_batch, seq)
    yp = yp.reshape(n_batch, seq, D_MODEL)
    p_hr = hfin[:, :, 0, :SSM_STATE][None]
    p_hi = hfin[:, :, 0, SSM_STATE:][None]
    p_k = ktail.reshape(1, n_batch, WINDOW, N_KV_HEADS, HEAD_DIM)
    p_v = vtail.reshape(1, n_batch, WINDOW, N_KV_HEADS, HEAD_DIM)

    xs = x_sample.reshape(n_dec, D_MODEL)
    proj_s = _proj_sample(xs, mods, w_in_b)
    ns = SSM_GROUPS * SSM_STATE
    ys_s, s_hr, s_hi = _sample_ssm(proj_s, state_ssm_re[0].reshape(n_dec, ns), state_ssm_im[0].reshape(n_dec, ns),
                                   ar_row, ai_row, bre, bim, cblk, ssm_d[0].reshape(1, -1).astype(F32))
    o_s, nk, nv = _sample_attn(proj_s, cache_swa_k[0].reshape(n_dec, WINDOW, KV_WIDTH),
                               cache_swa_v[0].reshape(n_dec, WINDOW, KV_WIDTH), bias_s, sink_row, seg, segt)
    ysmp = _final_sample(xs, ys_s, o_s, proj_s, mods, wglu, bglu, wbs, wba, wout, lng, lnb)

    return (yp, ysmp.reshape(n_dec, 1, D_MODEL),
            p_hr, p_hi, p_k, p_v,
            s_hr.reshape(1, n_dec, SSM_GROUPS, SSM_STATE), s_hi.reshape(1, n_dec, SSM_GROUPS, SSM_STATE),
            nk.reshape(1, n_dec, WINDOW, N_KV_HEADS, HEAD_DIM), nv.reshape(1, n_dec, WINDOW, N_KV_HEADS, HEAD_DIM))
```

```python
import math

import numpy as np
import jax
import jax.numpy as jnp
from jax import lax
from jax.experimental import pallas as pl
from jax.experimental.pallas import tpu as pltpu

F32 = jnp.float32
BF16 = jnp.bfloat16

D_MODEL = 1024
SSM_GROUPS = 64
SSM_GROUP = 16
SSM_STATE = 64
N_HEADS = 16
HEAD_DIM = 64
N_KV_HEADS = 4
KV_GROUP = 4
KV_WIDTH = N_KV_HEADS * HEAD_DIM
WINDOW = 128
N_BUCKETS = 32
NEG_INF = -1e30
LN_EPS = 1e-5
DEPTH = 1
DEEPNORM_ALPHA = (2 * DEPTH) ** 0.25
D_IN = 6656
CHUNK = 16
CW = CHUNK * SSM_GROUP
SW = 2 * SSM_STATE

COL_U, COL_ZS, COL_Q, COL_ZA, COL_GS, COL_GA = 0, 1, 2, 3, 4, 5
COL_K, COL_V = 24, 25

VMEM_LIMIT = 56 * 1024 * 1024
VMEM_LIMIT_SSM = 60 * 1024 * 1024


def _sigmoid(x):
    return 0.5 * jnp.tanh(0.5 * x) + 0.5


def _silu(x):
    return x * _sigmoid(x)


def _gelu_tanh(x):
    c = math.sqrt(2.0 / math.pi)
    hx = 0.5 * x
    return hx + hx * jnp.tanh(x * (c + (c * 0.044715) * (x * x)))


def _dot(a, b):
    return jnp.dot(a, b, preferred_element_type=F32)


def _dot_nt(a, b):
    return lax.dot_general(a, b, (((1,), (1,)), ((), ())), preferred_element_type=F32)


def _ada_body(c_ref, w_ref, b_ref, o_ref):
    c = c_ref[...]
    sc = _silu(c).astype(BF16)
    o_ref[...] = _dot(sc, w_ref[...].astype(BF16)) + b_ref[...]


def _ada(c_all, w_ada, b_ada):
    rows = c_all.shape[0]
    bn = 512
    return pl.pallas_call(
        _ada_body,
        grid=(3 * D_MODEL // bn,),
        in_specs=[pl.BlockSpec((rows, D_MODEL), lambda j: (0, 0)),
                  pl.BlockSpec((D_MODEL, bn), lambda j: (0, j)),
                  pl.BlockSpec((1, bn), lambda j: (0, j))],
        out_specs=pl.BlockSpec((rows, bn), lambda j: (0, j)),
        out_shape=jax.ShapeDtypeStruct((rows, 3 * D_MODEL), F32),
        name="ada",
    )(c_all, w_ada, b_ada.reshape(1, -1))


PROJ_CB = 512


def _proj_body(x_ref, shift_ref, scale_ref, w_ref, o_ref):
    h = x_ref[...] * (1.0 + scale_ref[...]) + shift_ref[...]
    hb = h.astype(BF16)
    for j in range(D_IN // PROJ_CB):
        sl = slice(j * PROJ_CB, (j + 1) * PROJ_CB)
        o_ref[:, sl] = _dot(hb, w_ref[:, sl]).astype(BF16)


def _proj_sample(xs, mods, w_in_b):
    rows = xs.shape[0]
    return pl.pallas_call(
        _proj_body,
        grid=(1,),
        in_specs=[pl.BlockSpec((rows, D_MODEL), lambda i: (0, 0)),
                  pl.BlockSpec((rows, D_MODEL), lambda i: (0, 0)),
                  pl.BlockSpec((rows, D_MODEL), lambda i: (0, 1)),
                  pl.BlockSpec((D_MODEL, D_IN), lambda i: (0, 0))],
        out_specs=pl.BlockSpec((rows, D_IN), lambda i: (0, 0)),
        out_shape=jax.ShapeDtypeStruct((rows, D_IN), BF16),
        compiler_params=pltpu.CompilerParams(vmem_limit_bytes=VMEM_LIMIT),
        name="proj",
    )(xs, mods, mods, w_in_b)


NKS = 128
GC = 16


def _cmul(c1, c2, x):
    return c1 * x + c2 * pltpu.roll(x, SSM_STATE, x.ndim - 1)


SSM_TOK = NKS * CHUNK
SUB = 8
NPH = CHUNK // SUB


def _dot_tn(a, b):
    return lax.dot_general(a, b, (((0,), (0,)), ((), ())), preferred_element_type=F32)


SEG = 8
SEG_LEN = NKS // SEG
POW_TAUS = (tuple(range(CHUNK + 1)) + tuple(CHUNK * m for m in range(2, SEG_LEN))
            + tuple(CHUNK * SEG_LEN * m for m in range(1, SEG + 1)) + (0,))
ROW_CHUNK1 = CHUNK
ROW_SEG1 = CHUNK + SEG_LEN - 1


def _ssm_body(x_ref, shift_ref, scale_ref, wut_ref, lhs_ref, bpow_ref, cpow_ref, p1_ref, p2_ref,
              y_ref, hfin_ref, r_scr, yt_scr, s_scr, carry_scr):
    i = pl.program_id(1)
    j = pl.program_id(2)
    n_blocks = pl.num_programs(1) - 1

    @pl.when((i == 0) & (j == 0))
    def _():
        carry_scr[...] = jnp.zeros_like(carry_scr)

    def load_pair(sp):
        h = jnp.concatenate([x_ref[:, b, s, :] for s in (2 * sp, 2 * sp + 1) for b in range(SEG_LEN)],
                            axis=0)
        hb = (h * (1.0 + scale_ref[...]) + shift_ref[...]).astype(BF16)
        ut = _dot_nt(wut_ref[...], hb).astype(BF16)
        ut = ut.reshape(SSM_GROUPS, SSM_GROUP, 2 * NKS)
        pair = jnp.concatenate([ut[:, :, :NKS], ut[:, :, NKS:]], axis=1)
        r0 = pl.multiple_of(j * SUB * SSM_GROUP + sp * 2 * SSM_GROUP, 2 * SSM_GROUP)
        r_scr[:, pl.ds(r0, 2 * SSM_GROUP), :] = pair

    def emit_subset(sp):
        t0 = pl.multiple_of(j * SUB * SSM_GROUP + sp * SSM_GROUP, SSM_GROUP)
        yt = yt_scr[:, pl.ds(t0, SSM_GROUP), :].reshape(D_MODEL, NKS)
        yr = yt.T
        for b in range(SEG_LEN):
            y_ref[:, b, sp, :] = yr[b * SEG:(b + 1) * SEG, :]

    @pl.when(i == 0)
    def _():
        y_ref[...] = jnp.zeros_like(y_ref)
        for sp in range(SUB // 2):
            load_pair(sp)

    @pl.when((i > 0) & (i < n_blocks))
    def _():
        for sp in range(SUB // 2):
            load_pair(sp)
            emit_subset(2 * sp)
            emit_subset(2 * sp + 1)

    @pl.when(i == n_blocks)
    def _():
        for sp in range(SUB):
            emit_subset(sp)

    @pl.when((i < n_blocks) & (j == NPH - 1))
    def _():
        def local(g):
            rg = r_scr[g]
            yt_scr[g] = _dot(lhs_ref[g], rg)
            s_scr[g] = _dot_tn(rg, bpow_ref[g])

        def carried(g):
            yt_scr[g] += _dot_nt(cpow_ref[g], s_scr[g].astype(BF16))

        row = lax.broadcasted_iota(jnp.int32, (GC, SEG, SW), 1)
        seg_rows = lambda b: slice(b * SEG, (b + 1) * SEG)

        def scan(gc):
            gs = slice(gc * GC, (gc + 1) * GC)
            p1, p2 = p1_ref[gs], p2_ref[gs]
            c1, c2 = p1[:, ROW_CHUNK1:ROW_CHUNK1 + 1, :], p2[:, ROW_CHUNK1:ROW_CHUNK1 + 1, :]
            h = jnp.zeros((GC, SEG, SW), F32)
            hs = h
            for b in range(SEG_LEN):
                sb = s_scr[gs, seg_rows(b), :]
                h, hs = sb + c1 * h + c2 * hs, pltpu.roll(sb, SSM_STATE, 2) + c1 * hs - c2 * h
                s_scr[gs, seg_rows(b), :] = h
                yield
            e = h
            for d in (1, 2, 4):
                rd = ROW_SEG1 + d - 1
                sh = jnp.where(row >= d, pltpu.roll(e, d, 1), 0.0)
                e = e + _cmul(p1[:, rd:rd + 1, :], p2[:, rd:rd + 1, :], sh)
            carry = carry_scr[gs]
            e = e + _cmul(p1[:, ROW_SEG1:ROW_SEG1 + SEG, :], p2[:, ROW_SEG1:ROW_SEG1 + SEG, :], carry)
            ein = jnp.where(row == 0, carry, pltpu.roll(e, 1, 1))
            eins = pltpu.roll(ein, SSM_STATE, 2)
            carry_scr[gs] = jnp.broadcast_to(e[:, SEG - 1:SEG, :], (GC, SEG, SW))
            yield
            for b in range(SEG_LEN - 1, 0, -1):
                rb = ROW_CHUNK1 + b - 1
                s_scr[gs, seg_rows(b), :] = (s_scr[gs, seg_rows(b - 1), :]
                                             + p1[:, rb:rb + 1, :] * ein + p2[:, rb:rb + 1, :] * eins)
                yield
            s_scr[gs, seg_rows(0), :] = ein

        def interleave(fn, groups, gen):
            for g in groups:
                fn(g)
                for _ in range(2):
                    next(gen, None)
            for _ in gen:
                pass

        n_sets = SSM_GROUPS // GC
        sets = [range(q * GC, (q + 1) * GC) for q in range(n_sets)]
        for g in sets[0]:
            local(g)
        for q in range(1, n_sets):
            interleave(local, sets[q], scan(q - 1))
        interleave(carried, sets[0], scan(n_sets - 1))
        for q in range(1, n_sets):
            for g in sets[q]:
                carried(g)
        hfin_ref[...] = carry_scr[...]


def _ssm_prompt(x_prompt, modp, wut, lhs, bpow, cpow, p1, p2):
    n_batch, seq, _ = x_prompt.shape
    steps = seq // SSM_TOK
    xv = x_prompt.reshape(n_batch * steps * SEG, SEG_LEN, CHUNK, D_MODEL)
    mod_spec = lambda k: pl.BlockSpec((None, 1, D_MODEL), lambda b, i, j, k=k: (b * 3 + k, 0, 0))
    consts = (wut, lhs, bpow, cpow, p1, p2)
    y, hfin = pl.pallas_call(
        _ssm_body,
        grid=(n_batch, steps + 1, NPH),
        in_specs=[pl.BlockSpec((SEG, SEG_LEN, SUB, D_MODEL),
                               lambda b, i, j: (b * steps + jnp.minimum(i, steps - 1), 0, j, 0)),
                  mod_spec(0), mod_spec(1)] + [_const_spec(c.shape) for c in consts],
        out_specs=[pl.BlockSpec((SEG, SEG_LEN, SUB, D_MODEL),
                                lambda b, i, j: (b * (steps + 1) + (i + steps) % (steps + 1), 0, j, 0)),
                   pl.BlockSpec((None, SSM_GROUPS, 8, SW), lambda b, i, j: (b, 0, 0, 0))],
        out_shape=[jax.ShapeDtypeStruct((n_batch * (steps + 1) * SEG, SEG_LEN, CHUNK, D_MODEL), F32),
                   jax.ShapeDtypeStruct((n_batch, SSM_GROUPS, 8, SW), F32)],
        scratch_shapes=[pltpu.VMEM((SSM_GROUPS, CW, NKS), BF16), pltpu.VMEM((SSM_GROUPS, CW, NKS), F32),
                        pltpu.VMEM((SSM_GROUPS, NKS, SW), F32), pltpu.VMEM((SSM_GROUPS, 8, SW), F32)],
        compiler_params=pltpu.CompilerParams(dimension_semantics=("arbitrary", "arbitrary", "arbitrary"),
                                             vmem_limit_bytes=VMEM_LIMIT_SSM),
        name="ssm_prompt",
    )(xv, modp, modp, *consts)
    return y.reshape(n_batch * (seq + SSM_TOK), D_MODEL), hfin


def _post(chunks, gate, wglu_ref, bglu_ref, wbs_ref, wba_ref, wout_ref, lng_ref, lnb_ref):
    data = [load() for load in chunks]
    ys = [_gelu_tanh(d[0]) for d in data]
    glu = [y * _sigmoid(_dot(y.astype(BF16), wglu_ref[...]) + bglu_ref[...]) for y in ys]
    b_s = [_dot(g.astype(BF16) * _silu(d[1]), wbs_ref[...]) for g, d in zip(glu, data)]
    b_a = [_dot(d[2].astype(BF16) * _silu(d[3]), wba_ref[...]) for d in data]
    m = [_sigmoid(d[4]).astype(F32) * s + _sigmoid(d[5]).astype(F32) * a for d, s, a in zip(data, b_s, b_a)]
    out = [_dot(v.astype(BF16), wout_ref[...]) for v in m]
    res = []
    for d, o in zip(data, out):
        r = DEEPNORM_ALPHA * d[6] + gate * o
        mu = jnp.mean(r, axis=-1, keepdims=True)
        rc = r - mu
        var = jnp.mean(rc * rc, axis=-1, keepdims=True)
        res.append(rc * lax.rsqrt(var + LN_EPS) * lng_ref[...] + lnb_ref[...])
    return res


TB = 512
NQB = TB // WINDOW
W_OFF = D_MODEL
ONES_ROWS = 16
POST_ROWS = 256
BLK_GROUP = 1
PROJ_W = 256


def _const_spec(shape):
    nd = len(shape)
    return pl.BlockSpec(shape, lambda *_: (0,) * nd, pipeline_mode=pl.Buffered(1))


def _bucket_thresholds():
    tbl = _bucket_table()
    return [int(np.argmax(tbl >= k)) for k in range(1, N_BUCKETS)]


def _fill_bias_table(rb_ref, bias_s):
    si = lax.broadcasted_iota(jnp.int32, (2 * WINDOW, WINDOW), 0)
    qi = lax.broadcasted_iota(jnp.int32, (2 * WINDOW, WINDOW), 1)
    dist = qi + WINDOW - si
    bid = jnp.zeros((2 * WINDOW, WINDOW), jnp.int32)
    for thr in _bucket_thresholds():
        bid = bid + (dist >= thr).astype(jnp.int32)
    valid = (dist >= 0) & (dist <= WINDOW)
    own = si >= WINDOW

    def per_head(h, c):
        t = jnp.zeros((2 * WINDOW, WINDOW), F32)
        for b in range(N_BUCKETS):
            t = jnp.where(bid == b, rb_ref[b * N_HEADS + h], t)
        t = jnp.where(valid, t, NEG_INF)
        kh, j = h // KV_GROUP, h % KV_GROUP
        bias_s[0, kh, j] = t
        bias_s[1, kh, j] = jnp.where(own, t, NEG_INF)
        return c

    lax.fori_loop(0, N_HEADS, per_head, 0)


def _main_body(rb_ref, x_ref, yssm_ref, shift_ref, scale_ref, gate_ref, wmain_ref, sink_ref,
               wglu_ref, bglu_ref, wbs_ref, wba_ref, wout_ref, lng_ref, lnb_ref,
               y_ref, ktail_ref, vtail_ref,
               zs_s, q_s, za_s, gs_s, ga_s, k_s, vt_s, ot_s, o_s, bias_s):
    i = pl.program_id(1)

    @pl.when((pl.program_id(0) == 0) & (i == 0))
    def _():
        _fill_bias_table(rb_ref, bias_s)

    @pl.when(i == 0)
    def _():
        k_s[0:WINDOW, :] = jnp.zeros((WINDOW, KV_WIDTH), BF16)
        vt_s[:, 0:WINDOW] = jnp.zeros((KV_WIDTH, WINDOW), BF16)

    x = x_ref[...]
    hb = (x * (1.0 + scale_ref[...]) + shift_ref[...]).astype(BF16)

    def project(dst, c0, c, width=PROJ_W):
        lo = W_OFF + c0 + c * width
        dst[:, c * width:(c + 1) * width] = _dot(hb, wmain_ref[:, lo:lo + width]).astype(BF16)

    for c in range(D_MODEL // 512):
        project(q_s, D_MODEL, c, 512)
    deferred = [(dst, c0, c) for dst, c0 in ((zs_s, 0), (za_s, 2 * D_MODEL), (gs_s, 3 * D_MODEL), (ga_s, 4 * D_MODEL))
                for c in range(D_MODEL // PROJ_W)]

    kv = _dot(hb, wmain_ref[:, W_OFF + 5 * D_MODEL:W_OFF + 5 * D_MODEL + 2 * KV_WIDTH]).astype(BF16)
    k_s[WINDOW:WINDOW + TB, :] = kv[:, :KV_WIDTH]
    vt_s[:, WINDOW:WINDOW + TB] = kv[:, KV_WIDTH:].T

    first = jnp.where(i == 0, 1, 0)
    lane = lax.broadcasted_iota(jnp.int32, (2 * WINDOW, 128), 1)
    ones = jnp.ones((ONES_ROWS, 2 * WINDOW), BF16)
    n_def = len(deferred) * BLK_GROUP // NQB
    for grp in range(NQB // BLK_GROUP):
        units = [(blk, kh) for blk in range(grp * BLK_GROUP, (grp + 1) * BLK_GROUP) for kh in range(N_KV_HEADS)]
        mine = deferred[grp * n_def:(grp + 1) * n_def]
        sts = []
        for blk, kh in units:
            rows = slice(blk * WINDOW, (blk + 1) * WINDOW)
            kcat = k_s[blk * WINDOW:blk * WINDOW + 2 * WINDOW, :]
            pair, odd = kh // 2, kh % 2
            kcm = jnp.where((lane >= HEAD_DIM) == bool(odd), kcat[:, pair * 128:(pair + 1) * 128], jnp.zeros((), BF16))
            qp = jnp.concatenate(
                [q_s[rows, j * KV_WIDTH + pair * 128:j * KV_WIDTH + (pair + 1) * 128] for j in range(KV_GROUP)],
                axis=0)
            sel = first if blk == 0 else 0
            bias = jnp.concatenate([bias_s[sel, kh, j] for j in range(KV_GROUP)], axis=1)
            sts.append(_dot_nt(kcm, qp) + bias)
        share = [mine[k * len(mine) // 3:(k + 1) * len(mine) // 3] for k in range(3)]
        ms = [jnp.maximum(jnp.max(st, axis=0, keepdims=True), sink_ref[kh]) for st, (_, kh) in zip(sts, units)]
        for spec in share[0]:
            project(*spec)
        es = [jnp.exp(st - m).astype(BF16) for st, m in zip(sts, ms)]
        for spec in share[1]:
            project(*spec)
        rs = []
        for e, (blk, kh) in zip(es, units):
            vtc = vt_s[kh * HEAD_DIM:(kh + 1) * HEAD_DIM, blk * WINDOW:blk * WINDOW + 2 * WINDOW]
            rs.append(_dot(jnp.concatenate([vtc, ones], axis=0), e))
        for spec in share[2]:
            project(*spec)
        for r, m, (blk, kh) in zip(rs, ms, units):
            den = r[HEAD_DIM:HEAD_DIM + 1, :] + jnp.exp(sink_ref[kh] - m)
            ot = (r[:HEAD_DIM, :] / den).astype(BF16)
            for j in range(KV_GROUP):
                ot_s[blk, j, kh * HEAD_DIM:(kh + 1) * HEAD_DIM, :] = ot[:, j * WINDOW:(j + 1) * WINDOW]
        for blk in range(grp * BLK_GROUP, (grp + 1) * BLK_GROUP):
            for j in range(KV_GROUP):
                o_s[blk * WINDOW:(blk + 1) * WINDOW, j * KV_WIDTH:(j + 1) * KV_WIDTH] = ot_s[blk, j].T

    k_s[0:WINDOW, :] = k_s[TB:TB + WINDOW, :]
    vt_s[:, 0:WINDOW] = vt_s[:, TB:TB + WINDOW]

    def loader(c):
        rs = slice(c * POST_ROWS, (c + 1) * POST_ROWS)
        return lambda: (yssm_ref[rs, :], zs_s[rs, :], o_s[rs, :], za_s[rs, :], gs_s[rs, :], ga_s[rs, :], x_ref[rs, :])

    ys = _post([loader(c) for c in range(TB // POST_ROWS)], gate_ref[...],
               wglu_ref, bglu_ref, wbs_ref, wba_ref, wout_ref, lng_ref, lnb_ref)
    for c, yc in enumerate(ys):
        y_ref[c * POST_ROWS:(c + 1) * POST_ROWS, :] = yc

    @pl.when(i == pl.num_programs(1) - 1)
    def _():
        ktail_ref[...] = k_s[0:WINDOW, :].astype(F32)
        vtail_ref[...] = vt_s[:, 0:WINDOW].T.astype(F32)


def _main_prompt(rel_bias_flat, x2d, yssm, modp, wmain, sink_rows, wglu, bglu, wbs, wba, wout, lng, lnb,
                 n_batch, seq):
    steps = seq // TB
    row_spec = pl.BlockSpec((TB, D_MODEL), lambda b, i: (b * steps + i, 0))
    yssm_spec = pl.BlockSpec((TB, D_MODEL), lambda b, i: (b * (steps + SSM_TOK // TB) + i, 0))
    mod_spec = lambda j: pl.BlockSpec((None, 1, D_MODEL), lambda b, i, j=j: (b * 3 + j, 0, 0))
    tail_spec = pl.BlockSpec((None, WINDOW, KV_WIDTH), lambda b, i: (b, 0, 0))
    consts = (wmain, sink_rows, wglu, bglu, wbs, wba, wout, lng, lnb)
    return pl.pallas_call(
        _main_body,
        grid=(n_batch, steps),
        in_specs=[pl.BlockSpec(memory_space=pltpu.SMEM), row_spec, yssm_spec, mod_spec(0), mod_spec(1), mod_spec(2)]
        + [_const_spec(c.shape) for c in consts],
        out_specs=[row_spec, tail_spec, tail_spec],
        out_shape=[jax.ShapeDtypeStruct((n_batch * seq, D_MODEL), F32),
                   jax.ShapeDtypeStruct((n_batch, WINDOW, KV_WIDTH), F32),
                   jax.ShapeDtypeStruct((n_batch, WINDOW, KV_WIDTH), F32)],
        scratch_shapes=[pltpu.VMEM((TB, D_MODEL), BF16)] * 5
        + [pltpu.VMEM((TB + WINDOW, KV_WIDTH), BF16), pltpu.VMEM((KV_WIDTH, TB + WINDOW), BF16),
           pltpu.VMEM((NQB, KV_GROUP, KV_WIDTH, WINDOW), BF16), pltpu.VMEM((TB, D_MODEL), BF16),
           pltpu.VMEM((2, N_KV_HEADS, KV_GROUP, 2 * WINDOW, WINDOW), F32)],
        compiler_params=pltpu.CompilerParams(dimension_semantics=("arbitrary", "arbitrary"),
                                             vmem_limit_bytes=VMEM_LIMIT),
        name="main_prompt",
    )(rel_bias_flat, x2d, yssm, modp, modp, modp, *consts)


GT = 8


def _sample_ssm_body(u_ref, h0r_ref, h0i_ref, ar_ref, ai_ref, bre_ref, bim_ref, cblk_ref, d_ref,
                     y_ref, hr_ref, hi_ref):
    ub = u_ref[...]
    ys = []
    for a in range(GT):
        ua = ub[:, a * 128:(a + 1) * 128]
        st = slice(a * 512, (a + 1) * 512)
        xr = _dot(ua, bre_ref[a])
        xi = _dot(ua, bim_ref[a])
        h0r = h0r_ref[:, st]
        h0i = h0i_ref[:, st]
        ar = ar_ref[:, st]
        ai = ai_ref[:, st]
        hr = ar * h0r - ai * h0i + xr
        hi = ar * h0i + ai * h0r + xi
        hr_ref[:, st] = hr
        hi_ref[:, st] = hi
        hcat = jnp.concatenate([hr, hi], axis=1).astype(BF16)
        ys.append(_dot_nt(hcat, cblk_ref[a]))
    y_ref[...] = jnp.concatenate(ys, axis=1) + d_ref[...] * ub.astype(F32)


def _sample_ssm(proj_s, h0r, h0i, ar, ai, bre, bim, cblk, d_row):
    rows = proj_s.shape[0]
    ns = SSM_GROUPS * SSM_STATE
    full = lambda shape: pl.BlockSpec(shape, lambda i: (0,) * len(shape))
    return pl.pallas_call(
        _sample_ssm_body,
        grid=(1,),
        in_specs=[pl.BlockSpec((rows, D_MODEL), lambda i: (0, COL_U)),
                  full((rows, ns)), full((rows, ns)), full((1, ns)), full((1, ns)),
                  full(bre.shape), full(bim.shape), full(cblk.shape), full((1, D_MODEL))],
        out_specs=[full((rows, D_MODEL)), full((rows, ns)), full((rows, ns))],
        out_shape=[jax.ShapeDtypeStruct((rows, D_MODEL), F32),
                   jax.ShapeDtypeStruct((rows, ns), F32),
                   jax.ShapeDtypeStruct((rows, ns), F32)],
        compiler_params=pltpu.CompilerParams(vmem_limit_bytes=VMEM_LIMIT),
        name="sample_ssm",
    )(proj_s, h0r, h0i, ar, ai, bre, bim, cblk, d_row)


SB = 32
KROWS = WINDOW + 8
SEQ_GROUP = 4


def _sample_attn_body(q_ref, kn_ref, vn_ref, ck_ref, cv_ref, bias_ref, sink_ref, seg_ref, segt_ref,
                      o_ref, nk_ref, nv_ref):
    qf = q_ref[...].astype(F32)
    knf = kn_ref[...].astype(F32)
    vnf = vn_ref[...].astype(F32)
    rowk = lax.broadcasted_iota(jnp.int32, (KROWS, 128), 0)
    roww = lax.broadcasted_iota(jnp.int32, (WINDOW, KV_WIDTH), 0)
    sink = sink_ref[...]
    last = roww == WINDOW - 1
    for g0 in range(0, SB, SEQ_GROUP):
        seqs = range(g0, g0 + SEQ_GROUP)
        scores = []
        for b in seqs:
            kall = jnp.concatenate([ck_ref[b], jnp.broadcast_to(knf[b:b + 1], (8, KV_WIDTH))], axis=0)
            qrow = qf[b:b + 1]
            prod = jnp.concatenate([kall * qrow[:, j * KV_WIDTH:(j + 1) * KV_WIDTH] for j in range(KV_GROUP)], axis=1)
            s = _dot(prod.astype(BF16), seg_ref[...]) + bias_ref[...]
            scores.append(jnp.where(rowk <= WINDOW, s, NEG_INF))
        probs = []
        for s in scores:
            m = jnp.maximum(jnp.max(s, axis=0, keepdims=True), sink)
            e = jnp.exp(s - m)
            den = jnp.sum(e, axis=0, keepdims=True) + jnp.exp(sink - m)
            probs.append((e / den).astype(BF16))
        pexps = [_dot(p, segt_ref[...]) for p in probs]
        for b, pexp in zip(seqs, pexps):
            vb = cv_ref[b]
            vall = jnp.concatenate([vb, jnp.broadcast_to(vnf[b:b + 1], (8, KV_WIDTH))], axis=0)
            v4 = jnp.concatenate([vall] * KV_GROUP, axis=1)
            o_ref[b:b + 1, :] = jnp.sum(pexp * v4, axis=0, keepdims=True)
            nv_ref[b] = jnp.where(last, jnp.broadcast_to(vnf[b:b + 1], (WINDOW, KV_WIDTH)), pltpu.roll(vb, WINDOW - 1, 0))
            nk_ref[b] = jnp.where(last, jnp.broadcast_to(knf[b:b + 1], (WINDOW, KV_WIDTH)),
                                  pltpu.roll(ck_ref[b], WINDOW - 1, 0))


def _sample_attn(proj_s, ck, cv, bias_s, sink_row, seg, segt):
    rows = proj_s.shape[0]
    cache_spec = pl.BlockSpec((SB, WINDOW, KV_WIDTH), lambda i: (i, 0, 0))
    return pl.pallas_call(
        _sample_attn_body,
        grid=(rows // SB,),
        in_specs=[pl.BlockSpec((SB, D_MODEL), lambda i: (i, COL_Q)),
                  pl.BlockSpec((SB, KV_WIDTH), lambda i: (i, COL_K)),
                  pl.BlockSpec((SB, KV_WIDTH), lambda i: (i, COL_V)),
                  cache_spec, cache_spec,
                  pl.BlockSpec(bias_s.shape, lambda i: (0, 0)),
                  pl.BlockSpec(sink_row.shape, lambda i: (0, 0)),
                  pl.BlockSpec(seg.shape, lambda i: (0, 0)),
                  pl.BlockSpec(segt.shape, lambda i: (0, 0))],
        out_specs=[pl.BlockSpec((SB, D_MODEL), lambda i: (i, 0)), cache_spec, cache_spec],
        out_shape=[jax.ShapeDtypeStruct((rows, D_MODEL), F32),
                   jax.ShapeDtypeStruct(ck.shape, F32),
                   jax.ShapeDtypeStruct(cv.shape, F32)],
        compiler_params=pltpu.CompilerParams(vmem_limit_bytes=VMEM_LIMIT),
        name="sample_attn",
    )(proj_s, proj_s, proj_s, ck, cv, bias_s, sink_row, seg, segt)


def _sample_final_body(x_ref, yssm_ref, zs_ref, o_ref, za_ref, gs_ref, ga_ref, gate_ref,
                       wglu_ref, bglu_ref, wbs_ref, wba_ref, wout_ref, lng_ref, lnb_ref, y_ref):
    load = lambda: (yssm_ref[...], zs_ref[...], o_ref[...], za_ref[...], gs_ref[...], ga_ref[...], x_ref[...])
    y_ref[...] = _post([load], gate_ref[...], wglu_ref, bglu_ref, wbs_ref, wba_ref, wout_ref, lng_ref, lnb_ref)[0]


def _final_sample(x2d, yssm, o_a, proj_s, mods, wglu, bglu, wbs, wba, wout, lng, lnb):
    rows = x2d.shape[0]
    full = lambda shape: pl.BlockSpec(shape, lambda i: (0,) * len(shape))
    pcol = lambda c: pl.BlockSpec((rows, D_MODEL), lambda i, c=c: (0, c))
    return pl.pallas_call(
        _sample_final_body,
        grid=(1,),
        in_specs=[full((rows, D_MODEL)), full((rows, D_MODEL)), pcol(COL_ZS), full((rows, D_MODEL)),
                  pcol(COL_ZA), pcol(COL_GS), pcol(COL_GA),
                  pl.BlockSpec((rows, D_MODEL), lambda i: (0, 2)),
                  full(wglu.shape), full(bglu.shape), full(wbs.shape), full(wba.shape), full(wout.shape),
                  full(lng.shape), full(lnb.shape)],
        out_specs=full((rows, D_MODEL)),
        out_shape=jax.ShapeDtypeStruct((rows, D_MODEL), F32),
        compiler_params=pltpu.CompilerParams(vmem_limit_bytes=VMEM_LIMIT),
        name="final_sample",
    )(x2d, yssm, proj_s, o_a, proj_s, proj_s, proj_s, mods, wglu, bglu, wbs, wba, wout, lng, lnb)


def _bucket_table():
    max_exact = N_BUCKETS // 2
    dist = np.arange(WINDOW + 1)
    df = np.maximum(dist, 1).astype(np.float32)
    large = max_exact + (np.log(df / np.float32(max_exact)) / np.float32(math.log(WINDOW / max_exact))
                         * np.float32(N_BUCKETS - max_exact)).astype(np.int32)
    large = np.minimum(large, N_BUCKETS - 1)
    return np.where(dist < max_exact, dist, large)


def _dot3_nt(a, b):
    ah = a.astype(BF16)
    al = (a - ah.astype(F32)).astype(BF16)
    bh = b.astype(BF16)
    bl = (b - bh.astype(F32)).astype(BF16)
    return _dot_nt(ah, bh) + _dot_nt(ah, bl) + _dot_nt(al, bh)


def _ssm_prep_body(lre_ref, lim_ref, ld_ref, tau_ref, bt_ref, bts_ref, c2_ref, c2s_ref, d_ref,
                   lhs_ref, bpow_ref, cpow_ref, bre_ref, bim_ref, cblk_ref, p1_ref, p2_ref):
    lane = lax.broadcasted_iota(jnp.int32, (SSM_GROUP, SW), 1)
    left = lane < SSM_STATE
    sgn = jnp.where(left, 1.0, -1.0)
    row_t = lax.broadcasted_iota(jnp.int32, (CW, 128), 0)
    lane_t = lax.broadcasted_iota(jnp.int32, (CW, 128), 1)
    bre_ref[...] = jnp.zeros_like(bre_ref)
    bim_ref[...] = jnp.zeros_like(bim_ref)
    cblk_ref[...] = jnp.zeros_like(cblk_ref)
    half_sign = jnp.where(lax.broadcasted_iota(jnp.int32, (1, SW), 1) < SSM_STATE, -1.0, 1.0)
    tau = tau_ref[...]
    for gl in range(GT):
        lr = jnp.concatenate([lre_ref[gl:gl + 1, :]] * 2, axis=1)
        li = jnp.concatenate([lim_ref[gl:gl + 1, :]] * 2, axis=1)
        dt = jnp.exp(ld_ref[gl:gl + 1, :])
        mag = jnp.exp((lr * dt) * tau)
        ang = (li * dt) * tau
        p1 = mag * jnp.cos(ang)
        pim = mag * jnp.sin(ang)
        p2 = pim * half_sign
        p1_ref[gl] = p1
        p2_ref[gl] = p2
        ar, ai = p1[1:2, :], pim[1:2, :]
        den = lr * lr + li * li
        nr = ar - 1.0
        k1 = (nr * lr + ai * li) / den
        k2 = (ai * lr - nr * li) / den * half_sign
        bt, bts = bt_ref[gl], bts_ref[gl]
        c2, c2s = c2_ref[gl], c2s_ref[gl]
        bb = k1 * bt + k2 * bts
        bbs = k1 * bts - k2 * bt
        ca = [sgn * (c2 * p1[t:t + 1, :] + c2s * p2[t:t + 1, :]) for t in range(CHUNK + 1)]
        cpow_ref[gl] = jnp.concatenate(ca[1:], axis=0).astype(BF16)
        bpow_ref[gl] = jnp.concatenate(
            [p1[CHUNK - 1 - s:CHUNK - s, :] * bb + p2[CHUNK - 1 - s:CHUNK - s, :] * bbs for s in range(CHUNK)],
            axis=0).astype(BF16)
        e = _dot3_nt(jnp.concatenate(ca[:CHUNK], axis=0), jnp.concatenate([bb] * CHUNK, axis=0))
        tiles = []
        for lt in range(CW // 128):
            cols = slice(lt * 128, (lt + 1) * 128)
            et = e[:, cols]
            blk = (lane_t + lt * 128) // SSM_GROUP
            at = jnp.where(row_t == lane_t + lt * 128, d_ref[gl, :, cols], 0.0)
            for s in range(lt * 128 // SSM_GROUP, (lt + 1) * 128 // SSM_GROUP):
                sh = et if s == 0 else jnp.concatenate(
                    [jnp.zeros((s * SSM_GROUP, 128), F32), et[:CW - s * SSM_GROUP, :]], axis=0)
                at = at + jnp.where(blk == s, sh, 0.0)
            tiles.append(at)
        lhs_ref[gl] = jnp.concatenate(tiles, axis=1).astype(BF16)
        rs = slice(gl * SSM_GROUP, (gl + 1) * SSM_GROUP)
        ts = slice((gl // 2) * SW, (gl // 2 + 1) * SW)
        ti = slice(GT * SSM_STATE + (gl // 2) * SW, GT * SSM_STATE + (gl // 2 + 1) * SW)
        if gl % 2 == 0:
            bre_ref[rs, ts] = jnp.where(left, bb, 0.0).astype(BF16)
            bim_ref[rs, ts] = jnp.where(left, bbs, 0.0).astype(BF16)
            cblk_ref[rs, ts] = jnp.where(left, c2, 0.0).astype(BF16)
            cblk_ref[rs, ti] = jnp.where(left, -c2s, 0.0).astype(BF16)
        else:
            bre_ref[rs, ts] = jnp.where(left, 0.0, bbs).astype(BF16)
            bim_ref[rs, ts] = jnp.where(left, 0.0, bb).astype(BF16)
            cblk_ref[rs, ts] = jnp.where(left, 0.0, c2s).astype(BF16)
            cblk_ref[rs, ti] = jnp.where(left, 0.0, -c2).astype(BF16)


def _ssm_params(lam_re, lam_im, log_delta, b_re, b_im, c_re, c_im, d_skip):
    g = SSM_GROUPS
    nrow = len(POW_TAUS)
    tau = jnp.asarray(np.broadcast_to(np.asarray(POW_TAUS, np.float32)[:, None], (nrow, SW)))
    brt, bit = jnp.swapaxes(b_re, 1, 2), jnp.swapaxes(b_im, 1, 2)
    bt = jnp.concatenate([brt, bit], axis=-1)
    bts = jnp.concatenate([bit, brt], axis=-1)
    c2 = jnp.concatenate([c_re, c_im], axis=-1)
    c2s = jnp.concatenate([c_im, c_re], axis=-1)
    dtile = jnp.tile(d_skip.reshape(g, 1, SSM_GROUP), (1, 1, CHUNK))

    gspec = lambda r, w: pl.BlockSpec((GT, r, w), lambda a: (a, 0, 0))
    rspec = lambda w: pl.BlockSpec((GT, w), lambda a: (a, 0))
    tile_spec = lambda w: pl.BlockSpec((None, GT * SSM_GROUP, w), lambda a: (a, 0, 0))
    lhs, bpow, cpow, bre, bim, cblk, p1, p2 = pl.pallas_call(
        _ssm_prep_body,
        grid=(g // GT,),
        in_specs=[rspec(SSM_STATE), rspec(SSM_STATE), rspec(1), pl.BlockSpec((nrow, SW), lambda a: (0, 0)),
                  gspec(SSM_GROUP, SW), gspec(SSM_GROUP, SW), gspec(SSM_GROUP, SW), gspec(SSM_GROUP, SW),
                  gspec(1, CW)],
        out_specs=[gspec(CW, CW), gspec(CW, SW), gspec(CW, SW),
                   tile_spec(GT * SSM_STATE), tile_spec(GT * SSM_STATE), tile_spec(2 * GT * SSM_STATE),
                   gspec(nrow, SW), gspec(nrow, SW)],
        out_shape=[jax.ShapeDtypeStruct((g, CW, CW), BF16), jax.ShapeDtypeStruct((g, CW, SW), BF16),
                   jax.ShapeDtypeStruct((g, CW, SW), BF16),
                   jax.ShapeDtypeStruct((GT, GT * SSM_GROUP, GT * SSM_STATE), BF16),
                   jax.ShapeDtypeStruct((GT, GT * SSM_GROUP, GT * SSM_STATE), BF16),
                   jax.ShapeDtypeStruct((GT, GT * SSM_GROUP, 2 * GT * SSM_STATE), BF16),
                   jax.ShapeDtypeStruct((g, nrow, SW), F32), jax.ShapeDtypeStruct((g, nrow, SW), F32)],
        name="ssm_prep",
    )(lam_re.astype(F32), lam_im.astype(F32), log_delta.astype(F32).reshape(g, 1), tau, bt, bts, c2, c2s, dtile)
    ar_row = p1[:, 1, :SSM_STATE].reshape(1, -1)
    ai_row = p2[:, 1, SSM_STATE:].reshape(1, -1)
    return lhs, bpow, cpow, p1, p2, ar_row, ai_row, bre, bim, cblk


def _attn_params(rel_bias, sinks):
    hp = lax.Precision.HIGHEST
    rb = rel_bias.astype(F32)
    onehot = jnp.asarray(_bucket_table()[:, None] == np.arange(N_BUCKETS)[None, :], F32)
    tbl = jnp.dot(onehot, rb, precision=hp)
    sink_rows = jnp.repeat(sinks.astype(F32).reshape(N_KV_HEADS, 1, KV_GROUP), WINDOW, axis=2)
    ds = np.clip(WINDOW - np.arange(KROWS), 0, WINDOW)
    bs = jnp.dot(jnp.asarray(ds[:, None] == np.arange(WINDOW + 1)[None, :], F32), tbl, precision=hp)
    bs = jnp.transpose(bs.reshape(KROWS, N_KV_HEADS, KV_GROUP), (0, 2, 1)).reshape(KROWS, N_HEADS)
    bias_s = jnp.pad(bs, ((0, 0), (0, 128 - N_HEADS)))
    sk = jnp.transpose(sinks.astype(F32).reshape(N_KV_HEADS, KV_GROUP), (1, 0)).reshape(1, N_HEADS)
    sink_row = jnp.pad(sk, ((0, 0), (0, 128 - N_HEADS)))
    seg_np = np.zeros((D_MODEL, 128), np.float32)
    for j in range(KV_GROUP):
        for kh in range(N_KV_HEADS):
            r0 = j * KV_WIDTH + kh * HEAD_DIM
            seg_np[r0:r0 + HEAD_DIM, j * N_KV_HEADS + kh] = 1.0
    seg = jnp.asarray(seg_np, BF16)
    segt = jnp.asarray(seg_np.T, BF16)
    return sink_rows, bias_s, sink_row, seg, segt


WP_ROWS = 256


def _regroup_tiles(w):
    left = lax.broadcasted_iota(jnp.int32, (w.shape[0], 128), 1) < HEAD_DIM
    tile = lambda t: w[:, t * 128:(t + 1) * 128]
    out = []
    for j in range(KV_GROUP):
        for pair in range(N_KV_HEADS // 2):
            t1 = tile((2 * pair) * 2 + j // 2)
            t2 = tile((2 * pair + 1) * 2 + j // 2)
            if j % 2 == 0:
                out.append(jnp.where(left, t1, pltpu.roll(t2, HEAD_DIM, 1)))
            else:
                out.append(jnp.where(left, pltpu.roll(t1, HEAD_DIM, 1), t2))
    return jnp.concatenate(out, axis=1)


def _weights_body(win_ref, wglu_ref, wbs_ref, wba_ref, wout_ref,
                  winb_ref, wut_ref, wglub_ref, wbsb_ref, wbab_ref, woutb_ref):
    i = pl.program_id(0)
    seg = lambda a: win_ref[:, a:a + D_MODEL]
    u = seg(0)
    winb_ref[:, 0:D_MODEL] = u.astype(BF16)
    winb_ref[:, D_MODEL:2 * D_MODEL] = seg(D_MODEL).astype(BF16)
    winb_ref[:, 2 * D_MODEL:3 * D_MODEL] = (_regroup_tiles(seg(2 * D_MODEL)) * (HEAD_DIM ** -0.5)).astype(BF16)
    winb_ref[:, 3 * D_MODEL:4 * D_MODEL] = _regroup_tiles(seg(3 * D_MODEL + 2 * KV_WIDTH)).astype(BF16)
    winb_ref[:, 4 * D_MODEL:5 * D_MODEL] = seg(4 * D_MODEL + 2 * KV_WIDTH).astype(BF16)
    winb_ref[:, 5 * D_MODEL:6 * D_MODEL] = seg(5 * D_MODEL + 2 * KV_WIDTH).astype(BF16)
    winb_ref[:, 6 * D_MODEL:6 * D_MODEL + 2 * KV_WIDTH] = \
        win_ref[:, 3 * D_MODEL:3 * D_MODEL + 2 * KV_WIDTH].astype(BF16)
    wut_ref[...] = u.astype(BF16).T
    wglub_ref[...] = wglu_ref[...].astype(BF16)
    wbsb_ref[...] = wbs_ref[...].astype(BF16)
    woutb_ref[...] = wout_ref[...].astype(BF16)
    wba = wba_ref[...].astype(BF16)
    for j in range(KV_GROUP):
        r0 = pl.multiple_of(j * KV_WIDTH + i * HEAD_DIM, HEAD_DIM)
        wbab_ref[pl.ds(r0, HEAD_DIM), :] = wba[j * HEAD_DIM:(j + 1) * HEAD_DIM, :]


def _weights_prep(w_in0, w_glu0, w_bs0, w_ba0, w_out0):
    assert WP_ROWS == KV_GROUP * HEAD_DIM
    rows = lambda w: pl.BlockSpec((WP_ROWS, w), lambda i: (i, 0))
    sq = jax.ShapeDtypeStruct((D_MODEL, D_MODEL), BF16)
    return pl.pallas_call(
        _weights_body,
        grid=(D_MODEL // WP_ROWS,),
        in_specs=[rows(D_IN), rows(D_MODEL), rows(D_MODEL), rows(D_MODEL), rows(D_MODEL)],
        out_specs=[rows(D_IN), pl.BlockSpec((D_MODEL, WP_ROWS), lambda i: (0, i)), rows(D_MODEL), rows(D_MODEL),
                   pl.BlockSpec((D_MODEL, D_MODEL), lambda i: (0, 0)), rows(D_MODEL)],
        out_shape=[jax.ShapeDtypeStruct((D_MODEL, D_IN), BF16), sq, sq, sq, sq, sq],
        compiler_params=pltpu.CompilerParams(dimension_semantics=("arbitrary",), vmem_limit_bytes=VMEM_LIMIT),
        name="weights_prep",
    )(w_in0, w_glu0, w_bs0, w_ba0, w_out0)


def kernel(x_prompt, x_sample, c_prompt, c_sample, state_ssm_re, state_ssm_im, cache_swa_k, cache_swa_v,
           w_ada, b_ada, w_in, ssm_lambda_re, ssm_lambda_im, ssm_log_delta, ssm_b_re, ssm_b_im,
           ssm_c_re, ssm_c_im, ssm_d, w_glu, b_glu, attn_sinks, rel_bias, w_branch_s, w_branch_a,
           w_out, ln_g, ln_b):
    assert w_ada.shape[0] == 1, "single-layer trunk"
    n_batch, seq, _ = x_prompt.shape
    n_dec = x_sample.shape[0]

    w_in_b, wut, wglu, wbs, wba, wout = _weights_prep(w_in[0], w_glu[0], w_branch_s[0], w_branch_a[0], w_out[0])
    bglu = b_glu[0].reshape(1, -1).astype(F32)
    lng = ln_g[0].reshape(1, -1).astype(F32)
    lnb = ln_b[0].reshape(1, -1).astype(F32)
    lhs, bpow, cpow, p1, p2, ar_row, ai_row, bre, bim, cblk = _ssm_params(
        ssm_lambda_re[0], ssm_lambda_im[0], ssm_log_delta[0], ssm_b_re[0], ssm_b_im[0],
        ssm_c_re[0], ssm_c_im[0], ssm_d[0])
    sink_rows, bias_s, sink_row, seg, segt = _attn_params(rel_bias, attn_sinks[0])

    c_all = jnp.concatenate([c_prompt, jnp.zeros((8 - n_batch, D_MODEL), F32), c_sample], axis=0)
    mod = _ada(c_all, w_ada[0], b_ada[0])
    modp = mod[:n_batch].reshape(n_batch * 3, 1, D_MODEL)
    mods = mod[8:8 + n_dec]

    xp = x_prompt.reshape(n_batch * seq, D_MODEL)
    yssm, hfin = _ssm_prompt(x_prompt, modp, wut, lhs, bpow, cpow, p1, p2)
    yp, ktail, vtail = _main_prompt(rel_bias.astype(F32).reshape(-1), xp, yssm, modp, w_in_b, sink_rows,
                                    wglu, bglu, wbs, wba, wout, lng, lnb, n_batch, seq)
    yp = yp.reshape(n_batch, seq, D_MODEL)
    p_hr = hfin[:, :, 0, :SSM_STATE][None]
    p_hi = hfin[:, :, 0, SSM_STATE:][None]
    p_k = ktail.reshape(1, n_batch, WINDOW, N_KV_HEADS, HEAD_DIM)
    p_v = vtail.reshape(1, n_batch, WINDOW, N_KV_HEADS, HEAD_DIM)

    xs = x_sample.reshape(n_dec, D_MODEL)
    proj_s = _proj_sample(xs, mods, w_in_b)
    ns = SSM_GROUPS * SSM_STATE
    ys_s, s_hr, s_hi = _sample_ssm(proj_s, state_ssm_re[0].reshape(n_dec, ns), state_ssm_im[0].reshape(n_dec, ns),
                                   ar_row, ai_row, bre, bim, cblk, ssm_d[0].reshape(1, -1).astype(F32))
    o_s, nk, nv = _sample_attn(proj_s, cache_swa_k[0].reshape(n_dec, WINDOW, KV_WIDTH),
                               cache_swa_v[0].reshape(n_dec, WINDOW, KV_WIDTH), bias_s, sink_row, seg, segt)
    ysmp = _final_sample(xs, ys_s, o_s, proj_s, mods, wglu, bglu, wbs, wba, wout, lng, lnb)

    return (yp, ysmp.reshape(n_dec, 1, D_MODEL),
            p_hr, p_hi, p_k, p_v,
            s_hr.reshape(1, n_dec, SSM_GROUPS, SSM_STATE), s_hi.reshape(1, n_dec, SSM_GROUPS, SSM_STATE),
            nk.reshape(1, n_dec, WINDOW, N_KV_HEADS, HEAD_DIM), nv.reshape(1, n_dec, WINDOW, N_KV_HEADS, HEAD_DIM))
```

```python
import math

import numpy as np
import jax
import jax.numpy as jnp
from jax import lax
from jax.experimental import pallas as pl
from jax.experimental.pallas import tpu as pltpu

F32 = jnp.float32
BF16 = jnp.bfloat16

D_MODEL = 1024
SSM_GROUPS = 64
SSM_GROUP = 16
SSM_STATE = 64
N_HEADS = 16
HEAD_DIM = 64
N_KV_HEADS = 4
KV_GROUP = 4
KV_WIDTH = N_KV_HEADS * HEAD_DIM
WINDOW = 128
N_BUCKETS = 32
NEG_INF = -1e30
LN_EPS = 1e-5
DEPTH = 1
DEEPNORM_ALPHA = (2 * DEPTH) ** 0.25
D_IN = 6656
CHUNK = 16
CW = CHUNK * SSM_GROUP
SW = 2 * SSM_STATE

COL_U, COL_ZS, COL_Q, COL_ZA, COL_GS, COL_GA = 0, 1, 2, 3, 4, 5
COL_K, COL_V = 24, 25

VMEM_LIMIT = 56 * 1024 * 1024
VMEM_LIMIT_SSM = 60 * 1024 * 1024


def _sigmoid(x):
    return 0.5 * jnp.tanh(0.5 * x) + 0.5


def _silu(x):
    return x * _sigmoid(x)


def _gelu_tanh(x):
    c = math.sqrt(2.0 / math.pi)
    hx = 0.5 * x
    return hx + hx * jnp.tanh(x * (c + (c * 0.044715) * (x * x)))


def _dot(a, b):
    return jnp.dot(a, b, preferred_element_type=F32)


def _dot_nt(a, b):
    return lax.dot_general(a, b, (((1,), (1,)), ((), ())), preferred_element_type=F32)


def _ada_body(c_ref, w_ref, b_ref, o_ref):
    c = c_ref[...]
    sc = _silu(c).astype(BF16)
    o_ref[...] = _dot(sc, w_ref[...].astype(BF16)) + b_ref[...]


def _ada(c_all, w_ada, b_ada):
    rows = c_all.shape[0]
    bn = 1536
    return pl.pallas_call(
        _ada_body,
        grid=(3 * D_MODEL // bn,),
        in_specs=[pl.BlockSpec((rows, D_MODEL), lambda j: (0, 0)),
                  pl.BlockSpec((D_MODEL, bn), lambda j: (0, j)),
                  pl.BlockSpec((1, bn), lambda j: (0, j))],
        out_specs=pl.BlockSpec((rows, bn), lambda j: (0, j)),
        out_shape=jax.ShapeDtypeStruct((rows, 3 * D_MODEL), F32),
        name="ada",
    )(c_all, w_ada, b_ada.reshape(1, -1))


PROJ_CB = 512


def _proj_body(x_ref, shift_ref, scale_ref, w_ref, o_ref):
    h = x_ref[...] * (1.0 + scale_ref[...]) + shift_ref[...]
    hb = h.astype(BF16)
    for j in range(D_IN // PROJ_CB):
        sl = slice(j * PROJ_CB, (j + 1) * PROJ_CB)
        o_ref[:, sl] = _dot(hb, w_ref[:, sl]).astype(BF16)


def _proj_sample(xs, mods, w_in_b):
    rows = xs.shape[0]
    return pl.pallas_call(
        _proj_body,
        grid=(1,),
        in_specs=[pl.BlockSpec((rows, D_MODEL), lambda i: (0, 0)),
                  pl.BlockSpec((rows, D_MODEL), lambda i: (0, 0)),
                  pl.BlockSpec((rows, D_MODEL), lambda i: (0, 1)),
                  pl.BlockSpec((D_MODEL, D_IN), lambda i: (0, 0))],
        out_specs=pl.BlockSpec((rows, D_IN), lambda i: (0, 0)),
        out_shape=jax.ShapeDtypeStruct((rows, D_IN), BF16),
        compiler_params=pltpu.CompilerParams(vmem_limit_bytes=VMEM_LIMIT),
        name="proj",
    )(xs, mods, mods, w_in_b)


NKS = 128
GC = 16


def _cmul(c1, c2, x):
    return c1 * x + c2 * pltpu.roll(x, SSM_STATE, x.ndim - 1)


SSM_TOK = NKS * CHUNK
SUB = 8
NPH = CHUNK // SUB


def _dot_tn(a, b):
    return lax.dot_general(a, b, (((0,), (0,)), ((), ())), preferred_element_type=F32)


SEG = 8
SEG_LEN = NKS // SEG
POW_TAUS = (tuple(range(CHUNK + 1)) + tuple(CHUNK * m for m in range(2, SEG_LEN))
            + tuple(CHUNK * SEG_LEN * m for m in range(1, SEG + 1)) + (0,))
ROW_CHUNK1 = CHUNK
ROW_SEG1 = CHUNK + SEG_LEN - 1


def _ssm_body(x_ref, shift_ref, scale_ref, wut_ref, lhs_ref, bpow_ref, cpow_ref, p1_ref, p2_ref,
              y_ref, hfin_ref, r_scr, yt_scr, s_scr, carry_scr):
    i = pl.program_id(1)
    j = pl.program_id(2)
    n_blocks = pl.num_programs(1) - 1

    @pl.when((i == 0) & (j == 0))
    def _():
        carry_scr[...] = jnp.zeros_like(carry_scr)

    def load_pair(sp):
        h = jnp.concatenate([x_ref[:, b, s, :] for s in (2 * sp, 2 * sp + 1) for b in range(SEG_LEN)],
                            axis=0)
        hb = (h * (1.0 + scale_ref[...]) + shift_ref[...]).astype(BF16)
        ut = _dot_nt(wut_ref[...], hb).astype(BF16)
        ut = ut.reshape(SSM_GROUPS, SSM_GROUP, 2 * NKS)
        pair = jnp.concatenate([ut[:, :, :NKS], ut[:, :, NKS:]], axis=1)
        r0 = pl.multiple_of(j * SUB * SSM_GROUP + sp * 2 * SSM_GROUP, 2 * SSM_GROUP)
        r_scr[:, pl.ds(r0, 2 * SSM_GROUP), :] = pair

    def emit_subset(sp):
        t0 = pl.multiple_of(j * SUB * SSM_GROUP + sp * SSM_GROUP, SSM_GROUP)
        yt = yt_scr[:, pl.ds(t0, SSM_GROUP), :].reshape(D_MODEL, NKS)
        yr = yt.T
        for b in range(SEG_LEN):
            y_ref[:, b, sp, :] = yr[b * SEG:(b + 1) * SEG, :]

    @pl.when(i == 0)
    def _():
        y_ref[...] = jnp.zeros_like(y_ref)
        for sp in range(SUB // 2):
            load_pair(sp)

    @pl.when((i > 0) & (i < n_blocks))
    def _():
        for sp in range(SUB // 2):
            load_pair(sp)
            emit_subset(2 * sp)
            emit_subset(2 * sp + 1)

    @pl.when(i == n_blocks)
    def _():
        for sp in range(SUB):
            emit_subset(sp)

    @pl.when((i < n_blocks) & (j == NPH - 1))
    def _():
        def local(g):
            rg = r_scr[g]
            yt_scr[g] = _dot(lhs_ref[g], rg)
            s_scr[g] = _dot_tn(rg, bpow_ref[g])

        def carried(g):
            yt_scr[g] += _dot_nt(cpow_ref[g], s_scr[g].astype(BF16))

        row = lax.broadcasted_iota(jnp.int32, (GC, SEG, SW), 1)
        seg_rows = lambda b: slice(b * SEG, (b + 1) * SEG)

        def scan(gc):
            gs = slice(gc * GC, (gc + 1) * GC)
            p1, p2 = p1_ref[gs], p2_ref[gs]
            c1, c2 = p1[:, ROW_CHUNK1:ROW_CHUNK1 + 1, :], p2[:, ROW_CHUNK1:ROW_CHUNK1 + 1, :]
            h = jnp.zeros((GC, SEG, SW), F32)
            hs = h
            for b in range(SEG_LEN):
                sb = s_scr[gs, seg_rows(b), :]
                h, hs = sb + c1 * h + c2 * hs, pltpu.roll(sb, SSM_STATE, 2) + c1 * hs - c2 * h
                s_scr[gs, seg_rows(b), :] = h
                yield
            e = h
            for d in (1, 2, 4):
                rd = ROW_SEG1 + d - 1
                sh = jnp.where(row >= d, pltpu.roll(e, d, 1), 0.0)
                e = e + _cmul(p1[:, rd:rd + 1, :], p2[:, rd:rd + 1, :], sh)
            carry = carry_scr[gs]
            e = e + _cmul(p1[:, ROW_SEG1:ROW_SEG1 + SEG, :], p2[:, ROW_SEG1:ROW_SEG1 + SEG, :], carry)
            ein = jnp.where(row == 0, carry, pltpu.roll(e, 1, 1))
            eins = pltpu.roll(ein, SSM_STATE, 2)
            carry_scr[gs] = jnp.broadcast_to(e[:, SEG - 1:SEG, :], (GC, SEG, SW))
            yield
            for b in range(SEG_LEN - 1, 0, -1):
                rb = ROW_CHUNK1 + b - 1
                s_scr[gs, seg_rows(b), :] = (s_scr[gs, seg_rows(b - 1), :]
                                             + p1[:, rb:rb + 1, :] * ein + p2[:, rb:rb + 1, :] * eins)
                yield
            s_scr[gs, seg_rows(0), :] = ein

        def interleave(fn, groups, gen):
            for g in groups:
                fn(g)
                for _ in range(2):
                    next(gen, None)
            for _ in gen:
                pass

        n_sets = SSM_GROUPS // GC
        sets = [range(q * GC, (q + 1) * GC) for q in range(n_sets)]
        for g in sets[0]:
            local(g)
        for q in range(1, n_sets):
            interleave(local, sets[q], scan(q - 1))
        interleave(carried, sets[0], scan(n_sets - 1))
        for q in range(1, n_sets):
            for g in sets[q]:
                carried(g)
        hfin_ref[...] = carry_scr[...]


def _ssm_prompt(x_prompt, modp, wut, lhs, bpow, cpow, p1, p2):
    n_batch, seq, _ = x_prompt.shape
    steps = seq // SSM_TOK
    xv = x_prompt.reshape(n_batch * steps * SEG, SEG_LEN, CHUNK, D_MODEL)
    mod_spec = lambda k: pl.BlockSpec((None, 1, D_MODEL), lambda b, i, j, k=k: (b * 3 + k, 0, 0))
    consts = (wut, lhs, bpow, cpow, p1, p2)
    y, hfin = pl.pallas_call(
        _ssm_body,
        grid=(n_batch, steps + 1, NPH),
        in_specs=[pl.BlockSpec((SEG, SEG_LEN, SUB, D_MODEL),
                               lambda b, i, j: (b * steps + jnp.minimum(i, steps - 1), 0, j, 0)),
                  mod_spec(0), mod_spec(1)] + [_const_spec(c.shape) for c in consts],
        out_specs=[pl.BlockSpec((SEG, SEG_LEN, SUB, D_MODEL),
                                lambda b, i, j: (b * (steps + 1) + (i + steps) % (steps + 1), 0, j, 0)),
                   pl.BlockSpec((None, SSM_GROUPS, 8, SW), lambda b, i, j: (b, 0, 0, 0))],
        out_shape=[jax.ShapeDtypeStruct((n_batch * (steps + 1) * SEG, SEG_LEN, CHUNK, D_MODEL), F32),
                   jax.ShapeDtypeStruct((n_batch, SSM_GROUPS, 8, SW), F32)],
        scratch_shapes=[pltpu.VMEM((SSM_GROUPS, CW, NKS), BF16), pltpu.VMEM((SSM_GROUPS, CW, NKS), F32),
                        pltpu.VMEM((SSM_GROUPS, NKS, SW), F32), pltpu.VMEM((SSM_GROUPS, 8, SW), F32)],
        compiler_params=pltpu.CompilerParams(dimension_semantics=("arbitrary", "arbitrary", "arbitrary"),
                                             vmem_limit_bytes=VMEM_LIMIT_SSM),
        name="ssm_prompt",
    )(xv, modp, modp, *consts)
    return y.reshape(n_batch * (seq + SSM_TOK), D_MODEL), hfin


def _post(chunks, gate, wglu_ref, bglu_ref, wbs_ref, wba_ref, wout_ref, lng_ref, lnb_ref):
    data = [load() for load in chunks]
    ys = [_gelu_tanh(d[0]) for d in data]
    glu = [y * _sigmoid(_dot(y.astype(BF16), wglu_ref[...]) + bglu_ref[...]) for y in ys]
    b_s = [_dot(g.astype(BF16) * _silu(d[1]), wbs_ref[...]) for g, d in zip(glu, data)]
    b_a = [_dot(d[2].astype(BF16) * _silu(d[3]), wba_ref[...]) for d in data]
    m = [_sigmoid(d[4]).astype(F32) * s + _sigmoid(d[5]).astype(F32) * a for d, s, a in zip(data, b_s, b_a)]
    out = [_dot(v.astype(BF16), wout_ref[...]) for v in m]
    res = []
    for d, o in zip(data, out):
        r = DEEPNORM_ALPHA * d[6] + gate * o
        mu = jnp.mean(r, axis=-1, keepdims=True)
        rc = r - mu
        var = jnp.mean(rc * rc, axis=-1, keepdims=True)
        res.append(rc * lax.rsqrt(var + LN_EPS) * lng_ref[...] + lnb_ref[...])
    return res


TB = 512
NQB = TB // WINDOW
W_OFF = D_MODEL
ONES_ROWS = 16
POST_ROWS = 256
BLK_GROUP = 1
PROJ_W = 256


def _const_spec(shape):
    nd = len(shape)
    return pl.BlockSpec(shape, lambda *_: (0,) * nd, pipeline_mode=pl.Buffered(1))


def _bucket_thresholds():
    tbl = _bucket_table()
    return [int(np.argmax(tbl >= k)) for k in range(1, N_BUCKETS)]


def _fill_bias_table(rb_ref, bias_s):
    si = lax.broadcasted_iota(jnp.int32, (2 * WINDOW, WINDOW), 0)
    qi = lax.broadcasted_iota(jnp.int32, (2 * WINDOW, WINDOW), 1)
    dist = qi + WINDOW - si
    bid = jnp.zeros((2 * WINDOW, WINDOW), jnp.int32)
    for thr in _bucket_thresholds():
        bid = bid + (dist >= thr).astype(jnp.int32)
    valid = (dist >= 0) & (dist <= WINDOW)
    own = si >= WINDOW

    def per_head(h, c):
        t = jnp.zeros((2 * WINDOW, WINDOW), F32)
        for b in range(N_BUCKETS):
            t = jnp.where(bid == b, rb_ref[b * N_HEADS + h], t)
        t = jnp.where(valid, t, NEG_INF)
        kh, j = h // KV_GROUP, h % KV_GROUP
        bias_s[0, kh, j] = t
        bias_s[1, kh, j] = jnp.where(own, t, NEG_INF)
        return c

    lax.fori_loop(0, N_HEADS, per_head, 0)


def _main_body(rb_ref, x_ref, yssm_ref, shift_ref, scale_ref, gate_ref, wmain_ref, sink_ref,
               wglu_ref, bglu_ref, wbs_ref, wba_ref, wout_ref, lng_ref, lnb_ref,
               y_ref, ktail_ref, vtail_ref,
               zs_s, q_s, za_s, gs_s, ga_s, k_s, vt_s, ot_s, o_s, bias_s):
    i = pl.program_id(1)

    @pl.when((pl.program_id(0) == 0) & (i == 0))
    def _():
        _fill_bias_table(rb_ref, bias_s)

    @pl.when(i == 0)
    def _():
        k_s[0:WINDOW, :] = jnp.zeros((WINDOW, KV_WIDTH), BF16)
        vt_s[:, 0:WINDOW] = jnp.zeros((KV_WIDTH, WINDOW), BF16)

    x = x_ref[...]
    hb = (x * (1.0 + scale_ref[...]) + shift_ref[...]).astype(BF16)

    def project(dst, c0, c, width=PROJ_W):
        lo = W_OFF + c0 + c * width
        dst[:, c * width:(c + 1) * width] = _dot(hb, wmain_ref[:, lo:lo + width]).astype(BF16)

    for c in range(D_MODEL // 512):
        project(q_s, D_MODEL, c, 512)
    deferred = [(dst, c0, c) for dst, c0 in ((zs_s, 0), (za_s, 2 * D_MODEL), (gs_s, 3 * D_MODEL), (ga_s, 4 * D_MODEL))
                for c in range(D_MODEL // PROJ_W)]

    kv = _dot(hb, wmain_ref[:, W_OFF + 5 * D_MODEL:W_OFF + 5 * D_MODEL + 2 * KV_WIDTH]).astype(BF16)
    k_s[WINDOW:WINDOW + TB, :] = kv[:, :KV_WIDTH]
    vt_s[:, WINDOW:WINDOW + TB] = kv[:, KV_WIDTH:].T

    first = jnp.where(i == 0, 1, 0)
    lane = lax.broadcasted_iota(jnp.int32, (2 * WINDOW, 128), 1)
    ones = jnp.ones((ONES_ROWS, 2 * WINDOW), BF16)
    n_def = len(deferred) * BLK_GROUP // NQB
    for grp in range(NQB // BLK_GROUP):
        units = [(blk, kh) for blk in range(grp * BLK_GROUP, (grp + 1) * BLK_GROUP) for kh in range(N_KV_HEADS)]
        mine = deferred[grp * n_def:(grp + 1) * n_def]
        sts = []
        for blk, kh in units:
            rows = slice(blk * WINDOW, (blk + 1) * WINDOW)
            kcat = k_s[blk * WINDOW:blk * WINDOW + 2 * WINDOW, :]
            pair, odd = kh // 2, kh % 2
            kcm = jnp.where((lane >= HEAD_DIM) == bool(odd), kcat[:, pair * 128:(pair + 1) * 128], jnp.zeros((), BF16))
            qp = jnp.concatenate(
                [q_s[rows, j * KV_WIDTH + pair * 128:j * KV_WIDTH + (pair + 1) * 128] for j in range(KV_GROUP)],
                axis=0)
            sel = first if blk == 0 else 0
            bias = jnp.concatenate([bias_s[sel, kh, j] for j in range(KV_GROUP)], axis=1)
            sts.append(_dot_nt(kcm, qp) + bias)
        share = [mine[k * len(mine) // 3:(k + 1) * len(mine) // 3] for k in range(3)]
        ms = [jnp.maximum(jnp.max(st, axis=0, keepdims=True), sink_ref[kh]) for st, (_, kh) in zip(sts, units)]
        for spec in share[0]:
            project(*spec)
        es = [jnp.exp(st - m).astype(BF16) for st, m in zip(sts, ms)]
        for spec in share[1]:
            project(*spec)
        rs = []
        for e, (blk, kh) in zip(es, units):
            vtc = vt_s[kh * HEAD_DIM:(kh + 1) * HEAD_DIM, blk * WINDOW:blk * WINDOW + 2 * WINDOW]
            rs.append(_dot(jnp.concatenate([vtc, ones], axis=0), e))
        for spec in share[2]:
            project(*spec)
        for r, m, (blk, kh) in zip(rs, ms, units):
            den = r[HEAD_DIM:HEAD_DIM + 1, :] + jnp.exp(sink_ref[kh] - m)
            ot = (r[:HEAD_DIM, :] / den).astype(BF16)
            for j in range(KV_GROUP):
                ot_s[blk, j, kh * HEAD_DIM:(kh + 1) * HEAD_DIM, :] = ot[:, j * WINDOW:(j + 1) * WINDOW]
        for blk in range(grp * BLK_GROUP, (grp + 1) * BLK_GROUP):
            for j in range(KV_GROUP):
                o_s[blk * WINDOW:(blk + 1) * WINDOW, j * KV_WIDTH:(j + 1) * KV_WIDTH] = ot_s[blk, j].T

    k_s[0:WINDOW, :] = k_s[TB:TB + WINDOW, :]
    vt_s[:, 0:WINDOW] = vt_s[:, TB:TB + WINDOW]

    def loader(c):
        rs = slice(c * POST_ROWS, (c + 1) * POST_ROWS)
        return lambda: (yssm_ref[rs, :], zs_s[rs, :], o_s[rs, :], za_s[rs, :], gs_s[rs, :], ga_s[rs, :], x_ref[rs, :])

    ys = _post([loader(c) for c in range(TB // POST_ROWS)], gate_ref[...],
               wglu_ref, bglu_ref, wbs_ref, wba_ref, wout_ref, lng_ref, lnb_ref)
    for c, yc in enumerate(ys):
        y_ref[c * POST_ROWS:(c + 1) * POST_ROWS, :] = yc

    @pl.when(i == pl.num_programs(1) - 1)
    def _():
        ktail_ref[...] = k_s[0:WINDOW, :].astype(F32)
        vtail_ref[...] = vt_s[:, 0:WINDOW].T.astype(F32)


def _main_prompt(rel_bias_flat, x2d, yssm, modp, wmain, sink_rows, wglu, bglu, wbs, wba, wout, lng, lnb,
                 n_batch, seq):
    steps = seq // TB
    row_spec = pl.BlockSpec((TB, D_MODEL), lambda b, i: (b * steps + i, 0))
    yssm_spec = pl.BlockSpec((TB, D_MODEL), lambda b, i: (b * (steps + SSM_TOK // TB) + i, 0))
    mod_spec = lambda j: pl.BlockSpec((None, 1, D_MODEL), lambda b, i, j=j: (b * 3 + j, 0, 0))
    tail_spec = pl.BlockSpec((None, WINDOW, KV_WIDTH), lambda b, i: (b, 0, 0))
    consts = (wmain, sink_rows, wglu, bglu, wbs, wba, wout, lng, lnb)
    return pl.pallas_call(
        _main_body,
        grid=(n_batch, steps),
        in_specs=[pl.BlockSpec(memory_space=pltpu.SMEM), row_spec, yssm_spec, mod_spec(0), mod_spec(1), mod_spec(2)]
        + [_const_spec(c.shape) for c in consts],
        out_specs=[row_spec, tail_spec, tail_spec],
        out_shape=[jax.ShapeDtypeStruct((n_batch * seq, D_MODEL), F32),
                   jax.ShapeDtypeStruct((n_batch, WINDOW, KV_WIDTH), F32),
                   jax.ShapeDtypeStruct((n_batch, WINDOW, KV_WIDTH), F32)],
        scratch_shapes=[pltpu.VMEM((TB, D_MODEL), BF16)] * 5
        + [pltpu.VMEM((TB + WINDOW, KV_WIDTH), BF16), pltpu.VMEM((KV_WIDTH, TB + WINDOW), BF16),
           pltpu.VMEM((NQB, KV_GROUP, KV_WIDTH, WINDOW), BF16), pltpu.VMEM((TB, D_MODEL), BF16),
           pltpu.VMEM((2, N_KV_HEADS, KV_GROUP, 2 * WINDOW, WINDOW), F32)],
        compiler_params=pltpu.CompilerParams(dimension_semantics=("arbitrary", "arbitrary"),
                                             vmem_limit_bytes=VMEM_LIMIT),
        name="main_prompt",
    )(rel_bias_flat, x2d, yssm, modp, modp, modp, *consts)


GT = 8
PREP_TILES = 2


def _sample_ssm_body(u_ref, h0r_ref, h0i_ref, ar_ref, ai_ref, bre_ref, bim_ref, cblk_ref, d_ref,
                     y_ref, hr_ref, hi_ref):
    ub = u_ref[...]
    ys = []
    for a in range(GT):
        ua = ub[:, a * 128:(a + 1) * 128]
        st = slice(a * 512, (a + 1) * 512)
        xr = _dot(ua, bre_ref[a])
        xi = _dot(ua, bim_ref[a])
        h0r = h0r_ref[:, st]
        h0i = h0i_ref[:, st]
        ar = ar_ref[:, st]
        ai = ai_ref[:, st]
        hr = ar * h0r - ai * h0i + xr
        hi = ar * h0i + ai * h0r + xi
        hr_ref[:, st] = hr
        hi_ref[:, st] = hi
        hcat = jnp.concatenate([hr, hi], axis=1).astype(BF16)
        ys.append(_dot_nt(hcat, cblk_ref[a]))
    y_ref[...] = jnp.concatenate(ys, axis=1) + d_ref[...] * ub.astype(F32)


def _sample_ssm(proj_s, h0r, h0i, ar, ai, bre, bim, cblk, d_row):
    rows = proj_s.shape[0]
    ns = SSM_GROUPS * SSM_STATE
    full = lambda shape: pl.BlockSpec(shape, lambda i: (0,) * len(shape))
    return pl.pallas_call(
        _sample_ssm_body,
        grid=(1,),
        in_specs=[pl.BlockSpec((rows, D_MODEL), lambda i: (0, COL_U)),
                  full((rows, ns)), full((rows, ns)), full((1, ns)), full((1, ns)),
                  full(bre.shape), full(bim.shape), full(cblk.shape), full((1, D_MODEL))],
        out_specs=[full((rows, D_MODEL)), full((rows, ns)), full((rows, ns))],
        out_shape=[jax.ShapeDtypeStruct((rows, D_MODEL), F32),
                   jax.ShapeDtypeStruct((rows, ns), F32),
                   jax.ShapeDtypeStruct((rows, ns), F32)],
        compiler_params=pltpu.CompilerParams(vmem_limit_bytes=VMEM_LIMIT),
        name="sample_ssm",
    )(proj_s, h0r, h0i, ar, ai, bre, bim, cblk, d_row)


SB = 32
KROWS = WINDOW + 8
SEQ_GROUP = 4


def _sample_attn_body(q_ref, kn_ref, vn_ref, ck_ref, cv_ref, bias_ref, sink_ref, seg_ref, segt_ref,
                      o_ref, nk_ref, nv_ref):
    qf = q_ref[...].astype(F32)
    knf = kn_ref[...].astype(F32)
    vnf = vn_ref[...].astype(F32)
    rowk = lax.broadcasted_iota(jnp.int32, (KROWS, 128), 0)
    roww = lax.broadcasted_iota(jnp.int32, (WINDOW, KV_WIDTH), 0)
    sink = sink_ref[...]
    last = roww == WINDOW - 1
    for g0 in range(0, SB, SEQ_GROUP):
        seqs = range(g0, g0 + SEQ_GROUP)
        scores = []
        for b in seqs:
            kall = jnp.concatenate([ck_ref[b], jnp.broadcast_to(knf[b:b + 1], (8, KV_WIDTH))], axis=0)
            qrow = qf[b:b + 1]
            prod = jnp.concatenate([kall * qrow[:, j * KV_WIDTH:(j + 1) * KV_WIDTH] for j in range(KV_GROUP)], axis=1)
            s = _dot(prod.astype(BF16), seg_ref[...]) + bias_ref[...]
            scores.append(jnp.where(rowk <= WINDOW, s, NEG_INF))
        probs = []
        for s in scores:
            m = jnp.maximum(jnp.max(s, axis=0, keepdims=True), sink)
            e = jnp.exp(s - m)
            den = jnp.sum(e, axis=0, keepdims=True) + jnp.exp(sink - m)
            probs.append((e / den).astype(BF16))
        pexps = [_dot(p, segt_ref[...]) for p in probs]
        for b, pexp in zip(seqs, pexps):
            vb = cv_ref[b]
            vall = jnp.concatenate([vb, jnp.broadcast_to(vnf[b:b + 1], (8, KV_WIDTH))], axis=0)
            v4 = jnp.concatenate([vall] * KV_GROUP, axis=1)
            o_ref[b:b + 1, :] = jnp.sum(pexp * v4, axis=0, keepdims=True)
            nv_ref[b] = jnp.where(last, jnp.broadcast_to(vnf[b:b + 1], (WINDOW, KV_WIDTH)), pltpu.roll(vb, WINDOW - 1, 0))
            nk_ref[b] = jnp.where(last, jnp.broadcast_to(knf[b:b + 1], (WINDOW, KV_WIDTH)),
                                  pltpu.roll(ck_ref[b], WINDOW - 1, 0))


def _sample_attn(proj_s, ck, cv, bias_s, sink_row, seg, segt):
    rows = proj_s.shape[0]
    cache_spec = pl.BlockSpec((SB, WINDOW, KV_WIDTH), lambda i: (i, 0, 0))
    return pl.pallas_call(
        _sample_attn_body,
        grid=(rows // SB,),
        in_specs=[pl.BlockSpec((SB, D_MODEL), lambda i: (i, COL_Q)),
                  pl.BlockSpec((SB, KV_WIDTH), lambda i: (i, COL_K)),
                  pl.BlockSpec((SB, KV_WIDTH), lambda i: (i, COL_V)),
                  cache_spec, cache_spec,
                  pl.BlockSpec(bias_s.shape, lambda i: (0, 0)),
                  pl.BlockSpec(sink_row.shape, lambda i: (0, 0)),
                  pl.BlockSpec(seg.shape, lambda i: (0, 0)),
                  pl.BlockSpec(segt.shape, lambda i: (0, 0))],
        out_specs=[pl.BlockSpec((SB, D_MODEL), lambda i: (i, 0)), cache_spec, cache_spec],
        out_shape=[jax.ShapeDtypeStruct((rows, D_MODEL), F32),
                   jax.ShapeDtypeStruct(ck.shape, F32),
                   jax.ShapeDtypeStruct(cv.shape, F32)],
        compiler_params=pltpu.CompilerParams(vmem_limit_bytes=VMEM_LIMIT),
        name="sample_attn",
    )(proj_s, proj_s, proj_s, ck, cv, bias_s, sink_row, seg, segt)


def _sample_final_body(x_ref, yssm_ref, zs_ref, o_ref, za_ref, gs_ref, ga_ref, gate_ref,
                       wglu_ref, bglu_ref, wbs_ref, wba_ref, wout_ref, lng_ref, lnb_ref, y_ref):
    load = lambda: (yssm_ref[...], zs_ref[...], o_ref[...], za_ref[...], gs_ref[...], ga_ref[...], x_ref[...])
    y_ref[...] = _post([load], gate_ref[...], wglu_ref, bglu_ref, wbs_ref, wba_ref, wout_ref, lng_ref, lnb_ref)[0]


def _final_sample(x2d, yssm, o_a, proj_s, mods, wglu, bglu, wbs, wba, wout, lng, lnb):
    rows = x2d.shape[0]
    full = lambda shape: pl.BlockSpec(shape, lambda i: (0,) * len(shape))
    pcol = lambda c: pl.BlockSpec((rows, D_MODEL), lambda i, c=c: (0, c))
    return pl.pallas_call(
        _sample_final_body,
        grid=(1,),
        in_specs=[full((rows, D_MODEL)), full((rows, D_MODEL)), pcol(COL_ZS), full((rows, D_MODEL)),
                  pcol(COL_ZA), pcol(COL_GS), pcol(COL_GA),
                  pl.BlockSpec((rows, D_MODEL), lambda i: (0, 2)),
                  full(wglu.shape), full(bglu.shape), full(wbs.shape), full(wba.shape), full(wout.shape),
                  full(lng.shape), full(lnb.shape)],
        out_specs=full((rows, D_MODEL)),
        out_shape=jax.ShapeDtypeStruct((rows, D_MODEL), F32),
        compiler_params=pltpu.CompilerParams(vmem_limit_bytes=VMEM_LIMIT),
        name="final_sample",
    )(x2d, yssm, proj_s, o_a, proj_s, proj_s, proj_s, mods, wglu, bglu, wbs, wba, wout, lng, lnb)


def _bucket_table():
    max_exact = N_BUCKETS // 2
    dist = np.arange(WINDOW + 1)
    df = np.maximum(dist, 1).astype(np.float32)
    large = max_exact + (np.log(df / np.float32(max_exact)) / np.float32(math.log(WINDOW / max_exact))
                         * np.float32(N_BUCKETS - max_exact)).astype(np.int32)
    large = np.minimum(large, N_BUCKETS - 1)
    return np.where(dist < max_exact, dist, large)


def _dot3_nt(a, b):
    ah = a.astype(BF16)
    al = (a - ah.astype(F32)).astype(BF16)
    bh = b.astype(BF16)
    bl = (b - bh.astype(F32)).astype(BF16)
    return _dot_nt(ah, bh) + _dot_nt(ah, bl) + _dot_nt(al, bh)


def _ssm_prep_body(lre_ref, lim_ref, ld_ref, tau_ref, bt_ref, bts_ref, c2_ref, c2s_ref, d_ref,
                   lhs_ref, bpow_ref, cpow_ref, bre_ref, bim_ref, cblk_ref, p1_ref, p2_ref):
    lane = lax.broadcasted_iota(jnp.int32, (SSM_GROUP, SW), 1)
    left = lane < SSM_STATE
    sgn = jnp.where(left, 1.0, -1.0)
    row_t = lax.broadcasted_iota(jnp.int32, (CW, 128), 0)
    lane_t = lax.broadcasted_iota(jnp.int32, (CW, 128), 1)
    bre_ref[...] = jnp.zeros_like(bre_ref)
    bim_ref[...] = jnp.zeros_like(bim_ref)
    cblk_ref[...] = jnp.zeros_like(cblk_ref)
    half_sign = jnp.where(lax.broadcasted_iota(jnp.int32, (1, SW), 1) < SSM_STATE, -1.0, 1.0)
    tau = tau_ref[...]
    for gi in range(PREP_TILES * GT):
        ti_, gl = divmod(gi, GT)
        lr = jnp.concatenate([lre_ref[gi:gi + 1, :]] * 2, axis=1)
        li = jnp.concatenate([lim_ref[gi:gi + 1, :]] * 2, axis=1)
        dt = jnp.exp(ld_ref[gi:gi + 1, :])
        mag = jnp.exp((lr * dt) * tau)
        ang = (li * dt) * tau
        p1 = mag * jnp.cos(ang)
        pim = mag * jnp.sin(ang)
        p2 = pim * half_sign
        p1_ref[gi] = p1
        p2_ref[gi] = p2
        ar, ai = p1[1:2, :], pim[1:2, :]
        den = lr * lr + li * li
        nr = ar - 1.0
        k1 = (nr * lr + ai * li) / den
        k2 = (ai * lr - nr * li) / den * half_sign
        bt, bts = bt_ref[gi], bts_ref[gi]
        c2, c2s = c2_ref[gi], c2s_ref[gi]
        bb = k1 * bt + k2 * bts
        bbs = k1 * bts - k2 * bt
        ca = [sgn * (c2 * p1[t:t + 1, :] + c2s * p2[t:t + 1, :]) for t in range(CHUNK + 1)]
        cpow_ref[gi] = jnp.concatenate(ca[1:], axis=0).astype(BF16)
        bpow_ref[gi] = jnp.concatenate(
            [p1[CHUNK - 1 - s:CHUNK - s, :] * bb + p2[CHUNK - 1 - s:CHUNK - s, :] * bbs for s in range(CHUNK)],
            axis=0).astype(BF16)
        e = _dot3_nt(jnp.concatenate(ca[:CHUNK], axis=0), jnp.concatenate([bb] * CHUNK, axis=0))
        tiles = []
        for lt in range(CW // 128):
            cols = slice(lt * 128, (lt + 1) * 128)
            et = e[:, cols]
            blk = (lane_t + lt * 128) // SSM_GROUP
            at = jnp.where(row_t == lane_t + lt * 128, d_ref[gi, :, cols], 0.0)
            for s in range(lt * 128 // SSM_GROUP, (lt + 1) * 128 // SSM_GROUP):
                sh = et if s == 0 else jnp.concatenate(
                    [jnp.zeros((s * SSM_GROUP, 128), F32), et[:CW - s * SSM_GROUP, :]], axis=0)
                at = at + jnp.where(blk == s, sh, 0.0)
            tiles.append(at)
        lhs_ref[gi] = jnp.concatenate(tiles, axis=1).astype(BF16)
        rs = slice(gl * SSM_GROUP, (gl + 1) * SSM_GROUP)
        ts = slice((gl // 2) * SW, (gl // 2 + 1) * SW)
        ti = slice(GT * SSM_STATE + (gl // 2) * SW, GT * SSM_STATE + (gl // 2 + 1) * SW)
        if gl % 2 == 0:
            bre_ref[ti_, rs, ts] = jnp.where(left, bb, 0.0).astype(BF16)
            bim_ref[ti_, rs, ts] = jnp.where(left, bbs, 0.0).astype(BF16)
            cblk_ref[ti_, rs, ts] = jnp.where(left, c2, 0.0).astype(BF16)
            cblk_ref[ti_, rs, ti] = jnp.where(left, -c2s, 0.0).astype(BF16)
        else:
            bre_ref[ti_, rs, ts] = jnp.where(left, 0.0, bbs).astype(BF16)
            bim_ref[ti_, rs, ts] = jnp.where(left, 0.0, bb).astype(BF16)
            cblk_ref[ti_, rs, ts] = jnp.where(left, 0.0, c2s).astype(BF16)
            cblk_ref[ti_, rs, ti] = jnp.where(left, 0.0, -c2).astype(BF16)


def _ssm_params(lam_re, lam_im, log_delta, b_re, b_im, c_re, c_im, d_skip):
    g = SSM_GROUPS
    nrow = len(POW_TAUS)
    tau = jnp.asarray(np.broadcast_to(np.asarray(POW_TAUS, np.float32)[:, None], (nrow, SW)))
    brt, bit = jnp.swapaxes(b_re, 1, 2), jnp.swapaxes(b_im, 1, 2)
    bt = jnp.concatenate([brt, bit], axis=-1)
    bts = jnp.concatenate([bit, brt], axis=-1)
    c2 = jnp.concatenate([c_re, c_im], axis=-1)
    c2s = jnp.concatenate([c_im, c_re], axis=-1)
    dtile = jnp.tile(d_skip.reshape(g, 1, SSM_GROUP), (1, 1, CHUNK))

    gstep = PREP_TILES * GT
    gspec = lambda r, w: pl.BlockSpec((gstep, r, w), lambda a: (a, 0, 0))
    rspec = lambda w: pl.BlockSpec((gstep, w), lambda a: (a, 0))
    tile_spec = lambda w: pl.BlockSpec((PREP_TILES, GT * SSM_GROUP, w), lambda a: (a, 0, 0))
    lhs, bpow, cpow, bre, bim, cblk, p1, p2 = pl.pallas_call(
        _ssm_prep_body,
        grid=(g // gstep,),
        in_specs=[rspec(SSM_STATE), rspec(SSM_STATE), rspec(1), pl.BlockSpec((nrow, SW), lambda a: (0, 0)),
                  gspec(SSM_GROUP, SW), gspec(SSM_GROUP, SW), gspec(SSM_GROUP, SW), gspec(SSM_GROUP, SW),
                  gspec(1, CW)],
        out_specs=[gspec(CW, CW), gspec(CW, SW), gspec(CW, SW),
                   tile_spec(GT * SSM_STATE), tile_spec(GT * SSM_STATE), tile_spec(2 * GT * SSM_STATE),
                   gspec(nrow, SW), gspec(nrow, SW)],
        out_shape=[jax.ShapeDtypeStruct((g, CW, CW), BF16), jax.ShapeDtypeStruct((g, CW, SW), BF16),
                   jax.ShapeDtypeStruct((g, CW, SW), BF16),
                   jax.ShapeDtypeStruct((GT, GT * SSM_GROUP, GT * SSM_STATE), BF16),
                   jax.ShapeDtypeStruct((GT, GT * SSM_GROUP, GT * SSM_STATE), BF16),
                   jax.ShapeDtypeStruct((GT, GT * SSM_GROUP, 2 * GT * SSM_STATE), BF16),
                   jax.ShapeDtypeStruct((g, nrow, SW), F32), jax.ShapeDtypeStruct((g, nrow, SW), F32)],
        name="ssm_prep",
    )(lam_re.astype(F32), lam_im.astype(F32), log_delta.astype(F32).reshape(g, 1), tau, bt, bts, c2, c2s, dtile)
    ar_row = p1[:, 1, :SSM_STATE].reshape(1, -1)
    ai_row = p2[:, 1, SSM_STATE:].reshape(1, -1)
    return lhs, bpow, cpow, p1, p2, ar_row, ai_row, bre, bim, cblk


def _attn_params(rel_bias, sinks):
    hp = lax.Precision.HIGHEST
    rb = rel_bias.astype(F32)
    onehot = jnp.asarray(_bucket_table()[:, None] == np.arange(N_BUCKETS)[None, :], F32)
    tbl = jnp.dot(onehot, rb, precision=hp)
    sink_rows = jnp.repeat(sinks.astype(F32).reshape(N_KV_HEADS, 1, KV_GROUP), WINDOW, axis=2)
    ds = np.clip(WINDOW - np.arange(KROWS), 0, WINDOW)
    bs = jnp.dot(jnp.asarray(ds[:, None] == np.arange(WINDOW + 1)[None, :], F32), tbl, precision=hp)
    bs = jnp.transpose(bs.reshape(KROWS, N_KV_HEADS, KV_GROUP), (0, 2, 1)).reshape(KROWS, N_HEADS)
    bias_s = jnp.pad(bs, ((0, 0), (0, 128 - N_HEADS)))
    sk = jnp.transpose(sinks.astype(F32).reshape(N_KV_HEADS, KV_GROUP), (1, 0)).reshape(1, N_HEADS)
    sink_row = jnp.pad(sk, ((0, 0), (0, 128 - N_HEADS)))
    seg_np = np.zeros((D_MODEL, 128), np.float32)
    for j in range(KV_GROUP):
        for kh in range(N_KV_HEADS):
            r0 = j * KV_WIDTH + kh * HEAD_DIM
            seg_np[r0:r0 + HEAD_DIM, j * N_KV_HEADS + kh] = 1.0
    seg = jnp.asarray(seg_np, BF16)
    segt = jnp.asarray(seg_np.T, BF16)
    return sink_rows, bias_s, sink_row, seg, segt


WP_ROWS = 256


def _regroup_tiles(w):
    left = lax.broadcasted_iota(jnp.int32, (w.shape[0], 128), 1) < HEAD_DIM
    tile = lambda t: w[:, t * 128:(t + 1) * 128]
    out = []
    for j in range(KV_GROUP):
        for pair in range(N_KV_HEADS // 2):
            t1 = tile((2 * pair) * 2 + j // 2)
            t2 = tile((2 * pair + 1) * 2 + j // 2)
            if j % 2 == 0:
                out.append(jnp.where(left, t1, pltpu.roll(t2, HEAD_DIM, 1)))
            else:
                out.append(jnp.where(left, pltpu.roll(t1, HEAD_DIM, 1), t2))
    return jnp.concatenate(out, axis=1)


def _weights_body(win_ref, wglu_ref, wbs_ref, wba_ref, wout_ref,
                  winb_ref, wut_ref, wglub_ref, wbsb_ref, wbab_ref, woutb_ref):
    i = pl.program_id(0)
    seg = lambda a: win_ref[:, a:a + D_MODEL]
    u = seg(0)
    winb_ref[:, 0:D_MODEL] = u.astype(BF16)
    winb_ref[:, D_MODEL:2 * D_MODEL] = seg(D_MODEL).astype(BF16)
    winb_ref[:, 2 * D_MODEL:3 * D_MODEL] = (_regroup_tiles(seg(2 * D_MODEL)) * (HEAD_DIM ** -0.5)).astype(BF16)
    winb_ref[:, 3 * D_MODEL:4 * D_MODEL] = _regroup_tiles(seg(3 * D_MODEL + 2 * KV_WIDTH)).astype(BF16)
    winb_ref[:, 4 * D_MODEL:5 * D_MODEL] = seg(4 * D_MODEL + 2 * KV_WIDTH).astype(BF16)
    winb_ref[:, 5 * D_MODEL:6 * D_MODEL] = seg(5 * D_MODEL + 2 * KV_WIDTH).astype(BF16)
    winb_ref[:, 6 * D_MODEL:6 * D_MODEL + 2 * KV_WIDTH] = \
        win_ref[:, 3 * D_MODEL:3 * D_MODEL + 2 * KV_WIDTH].astype(BF16)
    wut_ref[...] = u.astype(BF16).T
    wglub_ref[...] = wglu_ref[...].astype(BF16)
    wbsb_ref[...] = wbs_ref[...].astype(BF16)
    woutb_ref[...] = wout_ref[...].astype(BF16)
    wba = wba_ref[...].astype(BF16)
    for j in range(KV_GROUP):
        r0 = pl.multiple_of(j * KV_WIDTH + i * HEAD_DIM, HEAD_DIM)
        wbab_ref[pl.ds(r0, HEAD_DIM), :] = wba[j * HEAD_DIM:(j + 1) * HEAD_DIM, :]


def _weights_prep(w_in0, w_glu0, w_bs0, w_ba0, w_out0):
    assert WP_ROWS == KV_GROUP * HEAD_DIM
    rows = lambda w: pl.BlockSpec((WP_ROWS, w), lambda i: (i, 0))
    sq = jax.ShapeDtypeStruct((D_MODEL, D_MODEL), BF16)
    return pl.pallas_call(
        _weights_body,
        grid=(D_MODEL // WP_ROWS,),
        in_specs=[rows(D_IN), rows(D_MODEL), rows(D_MODEL), rows(D_MODEL), rows(D_MODEL)],
        out_specs=[rows(D_IN), pl.BlockSpec((D_MODEL, WP_ROWS), lambda i: (0, i)), rows(D_MODEL), rows(D_MODEL),
                   pl.BlockSpec((D_MODEL, D_MODEL), lambda i: (0, 0)), rows(D_MODEL)],
        out_shape=[jax.ShapeDtypeStruct((D_MODEL, D_IN), BF16), sq, sq, sq, sq, sq],
        compiler_params=pltpu.CompilerParams(dimension_semantics=("arbitrary",), vmem_limit_bytes=VMEM_LIMIT),
        name="weights_prep",
    )(w_in0, w_glu0, w_bs0, w_ba0, w_out0)


def kernel(x_prompt, x_sample, c_prompt, c_sample, state_ssm_re, state_ssm_im, cache_swa_k, cache_swa_v,
           w_ada, b_ada, w_in, ssm_lambda_re, ssm_lambda_im, ssm_log_delta, ssm_b_re, ssm_b_im,
           ssm_c_re, ssm_c_im, ssm_d, w_glu, b_glu, attn_sinks, rel_bias, w_branch_s, w_branch_a,
           w_out, ln_g, ln_b):
    assert w_ada.shape[0] == 1, "single-layer trunk"
    n_batch, seq, _ = x_prompt.shape
    n_dec = x_sample.shape[0]

    w_in_b, wut, wglu, wbs, wba, wout = _weights_prep(w_in[0], w_glu[0], w_branch_s[0], w_branch_a[0], w_out[0])
    bglu = b_glu[0].reshape(1, -1).astype(F32)
    lng = ln_g[0].reshape(1, -1).astype(F32)
    lnb = ln_b[0].reshape(1, -1).astype(F32)
    lhs, bpow, cpow, p1, p2, ar_row, ai_row, bre, bim, cblk = _ssm_params(
        ssm_lambda_re[0], ssm_lambda_im[0], ssm_log_delta[0], ssm_b_re[0], ssm_b_im[0],
        ssm_c_re[0], ssm_c_im[0], ssm_d[0])
    sink_rows, bias_s, sink_row, seg, segt = _attn_params(rel_bias, attn_sinks[0])

    c_all = jnp.concatenate([c_prompt, jnp.zeros((8 - n_batch, D_MODEL), F32), c_sample], axis=0)
    mod = _ada(c_all, w_ada[0], b_ada[0])
    modp = mod[:n_batch].reshape(n_batch * 3, 1, D_MODEL)
    mods = mod[8:8 + n_dec]

    xp = x_prompt.reshape(n_batch * seq, D_MODEL)
    yssm, hfin = _ssm_prompt(x_prompt, modp, wut, lhs, bpow, cpow, p1, p2)
    yp, ktail, vtail = _main_prompt(rel_bias.astype(F32).reshape(-1), xp, yssm, modp, w_in_b, sink_rows,
                                    wglu, bglu, wbs, wba, wout, lng, lnb, n_batch, seq)
    yp = yp.reshape(n_batch, seq, D_MODEL)
    p_hr = hfin[:, :, 0, :SSM_STATE][None]
    p_hi = hfin[:, :, 0, SSM_STATE:][None]
    p_k = ktail.reshape(1, n_batch, WINDOW, N_KV_HEADS, HEAD_DIM)
    p_v = vtail.reshape(1, n_batch, WINDOW, N_KV_HEADS, HEAD_DIM)

    xs = x_sample.reshape(n_dec, D_MODEL)
    proj_s = _proj_sample(xs, mods, w_in_b)
    ns = SSM_GROUPS * SSM_STATE
    ys_s, s_hr, s_hi = _sample_ssm(proj_s, state_ssm_re[0].reshape(n_dec, ns), state_ssm_im[0].reshape(n_dec, ns),
                                   ar_row, ai_row, bre, bim, cblk, ssm_d[0].reshape(1, -1).astype(F32))
    o_s, nk, nv = _sample_attn(proj_s, cache_swa_k[0].reshape(n_dec, WINDOW, KV_WIDTH),
                               cache_swa_v[0].reshape(n_dec, WINDOW, KV_WIDTH), bias_s, sink_row, seg, segt)
    ysmp = _final_sample(xs, ys_s, o_s, proj_s, mods, wglu, bglu, wbs, wba, wout, lng, lnb)

    return (yp, ysmp.reshape(n_dec, 1, D_MODEL),
            p_hr, p_hi, p_k, p_v,
            s_hr.reshape(1, n_dec, SSM_GROUPS, SSM_STATE), s_hi.reshape(1, n_dec, SSM_GROUPS, SSM_STATE),
            nk.reshape(1, n_dec, WINDOW, N_KV_HEADS, HEAD_DIM), nv.reshape(1, n_dec, WINDOW, N_KV_HEADS, HEAD_DIM))
```

```python
import math

import numpy as np
import jax
import jax.numpy as jnp
from jax import lax
from jax.experimental import pallas as pl
from jax.experimental.pallas import tpu as pltpu

F32 = jnp.float32
BF16 = jnp.bfloat16

D_MODEL = 1024
SSM_GROUPS = 64
SSM_GROUP = 16
SSM_STATE = 64
N_HEADS = 16
HEAD_DIM = 64
N_KV_HEADS = 4
KV_GROUP = 4
KV_WIDTH = N_KV_HEADS * HEAD_DIM
WINDOW = 128
N_BUCKETS = 32
NEG_INF = -1e30
LN_EPS = 1e-5
DEPTH = 1
DEEPNORM_ALPHA = (2 * DEPTH) ** 0.25
D_IN = 6656
CHUNK = 16
CW = CHUNK * SSM_GROUP
SW = 2 * SSM_STATE

COL_U, COL_ZS, COL_Q, COL_ZA, COL_GS, COL_GA = 0, 1, 2, 3, 4, 5
COL_K, COL_V = 24, 25

VMEM_LIMIT = 56 * 1024 * 1024
VMEM_LIMIT_SSM = 60 * 1024 * 1024


def _sigmoid(x):
    return 0.5 * jnp.tanh(0.5 * x) + 0.5


def _silu(x):
    return x * _sigmoid(x)


def _gelu_tanh(x):
    c = math.sqrt(2.0 / math.pi)
    hx = 0.5 * x
    return hx + hx * jnp.tanh(x * (c + (c * 0.044715) * (x * x)))


def _dot(a, b):
    return jnp.dot(a, b, preferred_element_type=F32)


def _dot_nt(a, b):
    return lax.dot_general(a, b, (((1,), (1,)), ((), ())), preferred_element_type=F32)


def _ada_body(c_ref, w_ref, b_ref, o_ref):
    c = c_ref[...]
    sc = _silu(c).astype(BF16)
    o_ref[...] = _dot(sc, w_ref[...].astype(BF16)) + b_ref[...]


def _ada(c_all, w_ada, b_ada):
    rows = c_all.shape[0]
    bn = 1536
    return pl.pallas_call(
        _ada_body,
        grid=(3 * D_MODEL // bn,),
        in_specs=[pl.BlockSpec((rows, D_MODEL), lambda j: (0, 0)),
                  pl.BlockSpec((D_MODEL, bn), lambda j: (0, j)),
                  pl.BlockSpec((1, bn), lambda j: (0, j))],
        out_specs=pl.BlockSpec((rows, bn), lambda j: (0, j)),
        out_shape=jax.ShapeDtypeStruct((rows, 3 * D_MODEL), F32),
        name="ada",
    )(c_all, w_ada, b_ada.reshape(1, -1))


PROJ_CB = 512


def _proj_body(x_ref, shift_ref, scale_ref, w_ref, o_ref):
    h = x_ref[...] * (1.0 + scale_ref[...]) + shift_ref[...]
    hb = h.astype(BF16)
    for j in range(D_IN // PROJ_CB):
        sl = slice(j * PROJ_CB, (j + 1) * PROJ_CB)
        o_ref[:, sl] = _dot(hb, w_ref[:, sl]).astype(BF16)


def _proj_sample(xs, mods, w_in_b):
    rows = xs.shape[0]
    return pl.pallas_call(
        _proj_body,
        grid=(1,),
        in_specs=[pl.BlockSpec((rows, D_MODEL), lambda i: (0, 0)),
                  pl.BlockSpec((rows, D_MODEL), lambda i: (0, 0)),
                  pl.BlockSpec((rows, D_MODEL), lambda i: (0, 1)),
                  pl.BlockSpec((D_MODEL, D_IN), lambda i: (0, 0))],
        out_specs=pl.BlockSpec((rows, D_IN), lambda i: (0, 0)),
        out_shape=jax.ShapeDtypeStruct((rows, D_IN), BF16),
        compiler_params=pltpu.CompilerParams(vmem_limit_bytes=VMEM_LIMIT),
        name="proj",
    )(xs, mods, mods, w_in_b)


NKS = 128
GC = 16


def _cmul(c1, c2, x):
    return c1 * x + c2 * pltpu.roll(x, SSM_STATE, x.ndim - 1)


SSM_TOK = NKS * CHUNK
SUB = 8
NPH = CHUNK // SUB


def _dot_tn(a, b):
    return lax.dot_general(a, b, (((0,), (0,)), ((), ())), preferred_element_type=F32)


SEG = 8
SEG_LEN = NKS // SEG
POW_TAUS = (tuple(range(CHUNK + 1)) + tuple(CHUNK * m for m in range(2, SEG_LEN))
            + tuple(CHUNK * SEG_LEN * m for m in range(1, SEG + 1)) + (0,))
ROW_CHUNK1 = CHUNK
ROW_SEG1 = CHUNK + SEG_LEN - 1


def _ssm_body(x_ref, shift_ref, scale_ref, wut_ref, lhs_ref, bpow_ref, cpow_ref, p1_ref, p2_ref,
              y_ref, hfin_ref, r_scr, yt_scr, s_scr, carry_scr):
    i = pl.program_id(1)
    j = pl.program_id(2)
    n_blocks = pl.num_programs(1) - 1

    @pl.when((i == 0) & (j == 0))
    def _():
        carry_scr[...] = jnp.zeros_like(carry_scr)

    def load_pair(sp):
        h = jnp.concatenate([x_ref[:, b, s, :] for s in (2 * sp, 2 * sp + 1) for b in range(SEG_LEN)],
                            axis=0)
        hb = (h * (1.0 + scale_ref[...]) + shift_ref[...]).astype(BF16)
        ut = _dot_nt(wut_ref[...], hb).astype(BF16)
        ut = ut.reshape(SSM_GROUPS, SSM_GROUP, 2 * NKS)
        pair = jnp.concatenate([ut[:, :, :NKS], ut[:, :, NKS:]], axis=1)
        r0 = pl.multiple_of(j * SUB * SSM_GROUP + sp * 2 * SSM_GROUP, 2 * SSM_GROUP)
        r_scr[:, pl.ds(r0, 2 * SSM_GROUP), :] = pair

    def emit_subset(sp):
        t0 = pl.multiple_of(j * SUB * SSM_GROUP + sp * SSM_GROUP, SSM_GROUP)
        yt = yt_scr[:, pl.ds(t0, SSM_GROUP), :].reshape(D_MODEL, NKS)
        yr = yt.T
        for b in range(SEG_LEN):
            y_ref[:, b, sp, :] = yr[b * SEG:(b + 1) * SEG, :]

    @pl.when(i == 0)
    def _():
        y_ref[...] = jnp.zeros_like(y_ref)
        for sp in range(SUB // 2):
            load_pair(sp)

    @pl.when((i > 0) & (i < n_blocks))
    def _():
        for sp in range(SUB // 2):
            load_pair(sp)
            emit_subset(2 * sp)
            emit_subset(2 * sp + 1)

    @pl.when(i == n_blocks)
    def _():
        for sp in range(SUB):
            emit_subset(sp)

    @pl.when((i < n_blocks) & (j == NPH - 1))
    def _():
        def local(g):
            rg = r_scr[g]
            yt_scr[g] = _dot(lhs_ref[g], rg)
            s_scr[g] = _dot_tn(rg, bpow_ref[g])

        def carried(g):
            yt_scr[g] += _dot_nt(cpow_ref[g], s_scr[g].astype(BF16))

        row = lax.broadcasted_iota(jnp.int32, (GC, SEG, SW), 1)
        seg_rows = lambda b: slice(b * SEG, (b + 1) * SEG)

        def scan(gc):
            gs = slice(gc * GC, (gc + 1) * GC)
            p1, p2 = p1_ref[gs], p2_ref[gs]
            c1, c2 = p1[:, ROW_CHUNK1:ROW_CHUNK1 + 1, :], p2[:, ROW_CHUNK1:ROW_CHUNK1 + 1, :]
            h = jnp.zeros((GC, SEG, SW), F32)
            hs = h
            for b in range(SEG_LEN):
                sb = s_scr[gs, seg_rows(b), :]
                h, hs = sb + c1 * h + c2 * hs, pltpu.roll(sb, SSM_STATE, 2) + c1 * hs - c2 * h
                s_scr[gs, seg_rows(b), :] = h
                yield
            e = h
            for d in (1, 2, 4):
                rd = ROW_SEG1 + d - 1
                sh = jnp.where(row >= d, pltpu.roll(e, d, 1), 0.0)
                e = e + _cmul(p1[:, rd:rd + 1, :], p2[:, rd:rd + 1, :], sh)
            carry = carry_scr[gs]
            e = e + _cmul(p1[:, ROW_SEG1:ROW_SEG1 + SEG, :], p2[:, ROW_SEG1:ROW_SEG1 + SEG, :], carry)
            ein = jnp.where(row == 0, carry, pltpu.roll(e, 1, 1))
            eins = pltpu.roll(ein, SSM_STATE, 2)
            carry_scr[gs] = jnp.broadcast_to(e[:, SEG - 1:SEG, :], (GC, SEG, SW))
            yield
            for b in range(SEG_LEN - 1, 0, -1):
                rb = ROW_CHUNK1 + b - 1
                s_scr[gs, seg_rows(b), :] = (s_scr[gs, seg_rows(b - 1), :]
                                             + p1[:, rb:rb + 1, :] * ein + p2[:, rb:rb + 1, :] * eins)
                yield
            s_scr[gs, seg_rows(0), :] = ein

        def interleave(fn, groups, gen):
            for g in groups:
                fn(g)
                for _ in range(2):
                    next(gen, None)
            for _ in gen:
                pass

        n_sets = SSM_GROUPS // GC
        sets = [range(q * GC, (q + 1) * GC) for q in range(n_sets)]
        for g in sets[0]:
            local(g)
        for q in range(1, n_sets):
            interleave(local, sets[q], scan(q - 1))
        interleave(carried, sets[0], scan(n_sets - 1))
        for q in range(1, n_sets):
            for g in sets[q]:
                carried(g)
        hfin_ref[...] = carry_scr[...]


def _ssm_prompt(x_prompt, modp, wut, lhs, bpow, cpow, p1, p2):
    n_batch, seq, _ = x_prompt.shape
    steps = seq // SSM_TOK
    xv = x_prompt.reshape(n_batch * steps * SEG, SEG_LEN, CHUNK, D_MODEL)
    mod_spec = lambda k: pl.BlockSpec((None, 1, D_MODEL), lambda b, i, j, k=k: (b * 3 + k, 0, 0))
    consts = (wut, lhs, bpow, cpow, p1, p2)
    y, hfin = pl.pallas_call(
        _ssm_body,
        grid=(n_batch, steps + 1, NPH),
        in_specs=[pl.BlockSpec((SEG, SEG_LEN, SUB, D_MODEL),
                               lambda b, i, j: (b * steps + jnp.minimum(i, steps - 1), 0, j, 0)),
                  mod_spec(0), mod_spec(1)] + [_const_spec(c.shape) for c in consts],
        out_specs=[pl.BlockSpec((SEG, SEG_LEN, SUB, D_MODEL),
                                lambda b, i, j: (b * (steps + 1) + (i + steps) % (steps + 1), 0, j, 0)),
                   pl.BlockSpec((None, SSM_GROUPS, 8, SW), lambda b, i, j: (b, 0, 0, 0))],
        out_shape=[jax.ShapeDtypeStruct((n_batch * (steps + 1) * SEG, SEG_LEN, CHUNK, D_MODEL), F32),
                   jax.ShapeDtypeStruct((n_batch, SSM_GROUPS, 8, SW), F32)],
        scratch_shapes=[pltpu.VMEM((SSM_GROUPS, CW, NKS), BF16), pltpu.VMEM((SSM_GROUPS, CW, NKS), F32),
                        pltpu.VMEM((SSM_GROUPS, NKS, SW), F32), pltpu.VMEM((SSM_GROUPS, 8, SW), F32)],
        compiler_params=pltpu.CompilerParams(dimension_semantics=("arbitrary", "arbitrary", "arbitrary"),
                                             vmem_limit_bytes=VMEM_LIMIT_SSM),
        name="ssm_prompt",
    )(xv, modp, modp, *consts)
    return y.reshape(n_batch * (seq + SSM_TOK), D_MODEL), hfin


def _post(chunks, gate, wglu_ref, bglu_ref, wbs_ref, wba_ref, wout_ref, lng_ref, lnb_ref):
    data = [load() for load in chunks]
    ys = [_gelu_tanh(d[0]) for d in data]
    glu = [y * _sigmoid(_dot(y.astype(BF16), wglu_ref[...]) + bglu_ref[...]) for y in ys]
    b_s = [_dot(g.astype(BF16) * _silu(d[1]), wbs_ref[...]) for g, d in zip(glu, data)]
    b_a = [_dot(d[2].astype(BF16) * _silu(d[3]), wba_ref[...]) for d in data]
    m = [_sigmoid(d[4]).astype(F32) * s + _sigmoid(d[5]).astype(F32) * a for d, s, a in zip(data, b_s, b_a)]
    out = [_dot(v.astype(BF16), wout_ref[...]) for v in m]
    res = []
    for d, o in zip(data, out):
        r = DEEPNORM_ALPHA * d[6] + gate * o
        mu = jnp.mean(r, axis=-1, keepdims=True)
        rc = r - mu
        var = jnp.mean(rc * rc, axis=-1, keepdims=True)
        res.append(rc * lax.rsqrt(var + LN_EPS) * lng_ref[...] + lnb_ref[...])
    return res


TB = 512
NQB = TB // WINDOW
W_OFF = D_MODEL
ONES_ROWS = 16
POST_ROWS = 256
BLK_GROUP = 1
PROJ_W = 256


def _const_spec(shape):
    nd = len(shape)
    return pl.BlockSpec(shape, lambda *_: (0,) * nd, pipeline_mode=pl.Buffered(1))


def _bucket_thresholds():
    tbl = _bucket_table()
    return [int(np.argmax(tbl >= k)) for k in range(1, N_BUCKETS)]


def _fill_bias_table(rb_ref, bias_s):
    si = lax.broadcasted_iota(jnp.int32, (2 * WINDOW, WINDOW), 0)
    qi = lax.broadcasted_iota(jnp.int32, (2 * WINDOW, WINDOW), 1)
    dist = qi + WINDOW - si
    bid = jnp.zeros((2 * WINDOW, WINDOW), jnp.int32)
    for thr in _bucket_thresholds():
        bid = bid + (dist >= thr).astype(jnp.int32)
    valid = (dist >= 0) & (dist <= WINDOW)
    own = si >= WINDOW

    def per_head(h, c):
        t = jnp.zeros((2 * WINDOW, WINDOW), F32)
        for b in range(N_BUCKETS):
            t = jnp.where(bid == b, rb_ref[b * N_HEADS + h], t)
        t = jnp.where(valid, t, NEG_INF)
        kh, j = h // KV_GROUP, h % KV_GROUP
        bias_s[0, kh, j] = t
        bias_s[1, kh, j] = jnp.where(own, t, NEG_INF)
        return c

    lax.fori_loop(0, N_HEADS, per_head, 0)


def _main_body(rb_ref, x_ref, yssm_ref, shift_ref, scale_ref, gate_ref, wmain_ref, sink_ref,
               wglu_ref, bglu_ref, wbs_ref, wba_ref, wout_ref, lng_ref, lnb_ref,
               y_ref, ktail_ref, vtail_ref,
               zs_s, q_s, za_s, gs_s, ga_s, k_s, vt_s, ot_s, o_s, bias_s):
    i = pl.program_id(1)

    @pl.when((pl.program_id(0) == 0) & (i == 0))
    def _():
        _fill_bias_table(rb_ref, bias_s)

    @pl.when(i == 0)
    def _():
        k_s[0:WINDOW, :] = jnp.zeros((WINDOW, KV_WIDTH), BF16)
        vt_s[:, 0:WINDOW] = jnp.zeros((KV_WIDTH, WINDOW), BF16)

    x = x_ref[...]
    hb = (x * (1.0 + scale_ref[...]) + shift_ref[...]).astype(BF16)

    def project(dst, c0, c, width=PROJ_W):
        lo = W_OFF + c0 + c * width
        dst[:, c * width:(c + 1) * width] = _dot(hb, wmain_ref[:, lo:lo + width]).astype(BF16)

    for c in range(D_MODEL // 512):
        project(q_s, D_MODEL, c, 512)
    deferred = [(dst, c0, c) for dst, c0 in ((zs_s, 0), (za_s, 2 * D_MODEL), (gs_s, 3 * D_MODEL), (ga_s, 4 * D_MODEL))
                for c in range(D_MODEL // PROJ_W)]

    kv = _dot(hb, wmain_ref[:, W_OFF + 5 * D_MODEL:W_OFF + 5 * D_MODEL + 2 * KV_WIDTH]).astype(BF16)
    k_s[WINDOW:WINDOW + TB, :] = kv[:, :KV_WIDTH]
    vt_s[:, WINDOW:WINDOW + TB] = kv[:, KV_WIDTH:].T

    first = jnp.where(i == 0, 1, 0)
    lane = lax.broadcasted_iota(jnp.int32, (2 * WINDOW, 128), 1)
    ones = jnp.ones((ONES_ROWS, 2 * WINDOW), BF16)
    n_def = len(deferred) * BLK_GROUP // NQB
    for grp in range(NQB // BLK_GROUP):
        units = [(blk, kh) for blk in range(grp * BLK_GROUP, (grp + 1) * BLK_GROUP) for kh in range(N_KV_HEADS)]
        mine = deferred[grp * n_def:(grp + 1) * n_def]
        sts = []
        for blk, kh in units:
            rows = slice(blk * WINDOW, (blk + 1) * WINDOW)
            kcat = k_s[blk * WINDOW:blk * WINDOW + 2 * WINDOW, :]
            pair, odd = kh // 2, kh % 2
            kcm = jnp.where((lane >= HEAD_DIM) == bool(odd), kcat[:, pair * 128:(pair + 1) * 128], jnp.zeros((), BF16))
            qp = jnp.concatenate(
                [q_s[rows, j * KV_WIDTH + pair * 128:j * KV_WIDTH + (pair + 1) * 128] for j in range(KV_GROUP)],
                axis=0)
            sel = first if blk == 0 else 0
            bias = jnp.concatenate([bias_s[sel, kh, j] for j in range(KV_GROUP)], axis=1)
            sts.append(_dot_nt(kcm, qp) + bias)
        share = [mine[k * len(mine) // 3:(k + 1) * len(mine) // 3] for k in range(3)]
        ms = [jnp.maximum(jnp.max(st, axis=0, keepdims=True), sink_ref[kh]) for st, (_, kh) in zip(sts, units)]
        for spec in share[0]:
            project(*spec)
        es = [jnp.exp(st - m).astype(BF16) for st, m in zip(sts, ms)]
        for spec in share[1]:
            project(*spec)
        rs = []
        for e, (blk, kh) in zip(es, units):
            vtc = vt_s[kh * HEAD_DIM:(kh + 1) * HEAD_DIM, blk * WINDOW:blk * WINDOW + 2 * WINDOW]
            rs.append(_dot(jnp.concatenate([vtc, ones], axis=0), e))
        for spec in share[2]:
            project(*spec)
        for r, m, (blk, kh) in zip(rs, ms, units):
            den = r[HEAD_DIM:HEAD_DIM + 1, :] + jnp.exp(sink_ref[kh] - m)
            ot = (r[:HEAD_DIM, :] * (1.0 / den)).astype(BF16)
            for j in range(KV_GROUP):
                ot_s[blk, j, kh * HEAD_DIM:(kh + 1) * HEAD_DIM, :] = ot[:, j * WINDOW:(j + 1) * WINDOW]
        for blk in range(grp * BLK_GROUP, (grp + 1) * BLK_GROUP):
            for j in range(KV_GROUP):
                o_s[blk * WINDOW:(blk + 1) * WINDOW, j * KV_WIDTH:(j + 1) * KV_WIDTH] = ot_s[blk, j].T

    k_s[0:WINDOW, :] = k_s[TB:TB + WINDOW, :]
    vt_s[:, 0:WINDOW] = vt_s[:, TB:TB + WINDOW]

    def loader(c):
        rs = slice(c * POST_ROWS, (c + 1) * POST_ROWS)
        return lambda: (yssm_ref[rs, :], zs_s[rs, :], o_s[rs, :], za_s[rs, :], gs_s[rs, :], ga_s[rs, :], x_ref[rs, :])

    ys = _post([loader(c) for c in range(TB // POST_ROWS)], gate_ref[...],
               wglu_ref, bglu_ref, wbs_ref, wba_ref, wout_ref, lng_ref, lnb_ref)
    for c, yc in enumerate(ys):
        y_ref[c * POST_ROWS:(c + 1) * POST_ROWS, :] = yc

    @pl.when(i == pl.num_programs(1) - 1)
    def _():
        ktail_ref[...] = k_s[0:WINDOW, :].astype(F32)
        vtail_ref[...] = vt_s[:, 0:WINDOW].T.astype(F32)


def _main_prompt(rel_bias_flat, x2d, yssm, modp, wmain, sink_rows, wglu, bglu, wbs, wba, wout, lng, lnb,
                 n_batch, seq):
    steps = seq // TB
    row_spec = pl.BlockSpec((TB, D_MODEL), lambda b, i: (b * steps + i, 0))
    yssm_spec = pl.BlockSpec((TB, D_MODEL), lambda b, i: (b * (steps + SSM_TOK // TB) + i, 0))
    mod_spec = lambda j: pl.BlockSpec((None, 1, D_MODEL), lambda b, i, j=j: (b * 3 + j, 0, 0))
    tail_spec = pl.BlockSpec((None, WINDOW, KV_WIDTH), lambda b, i: (b, 0, 0))
    consts = (wmain, sink_rows, wglu, bglu, wbs, wba, wout, lng, lnb)
    return pl.pallas_call(
        _main_body,
        grid=(n_batch, steps),
        in_specs=[pl.BlockSpec(memory_space=pltpu.SMEM), row_spec, yssm_spec, mod_spec(0), mod_spec(1), mod_spec(2)]
        + [_const_spec(c.shape) for c in consts],
        out_specs=[row_spec, tail_spec, tail_spec],
        out_shape=[jax.ShapeDtypeStruct((n_batch * seq, D_MODEL), F32),
                   jax.ShapeDtypeStruct((n_batch, WINDOW, KV_WIDTH), F32),
                   jax.ShapeDtypeStruct((n_batch, WINDOW, KV_WIDTH), F32)],
        scratch_shapes=[pltpu.VMEM((TB, D_MODEL), BF16)] * 5
        + [pltpu.VMEM((TB + WINDOW, KV_WIDTH), BF16), pltpu.VMEM((KV_WIDTH, TB + WINDOW), BF16),
           pltpu.VMEM((NQB, KV_GROUP, KV_WIDTH, WINDOW), BF16), pltpu.VMEM((TB, D_MODEL), BF16),
           pltpu.VMEM((2, N_KV_HEADS, KV_GROUP, 2 * WINDOW, WINDOW), F32)],
        compiler_params=pltpu.CompilerParams(dimension_semantics=("arbitrary", "arbitrary"),
                                             vmem_limit_bytes=VMEM_LIMIT),
        name="main_prompt",
    )(rel_bias_flat, x2d, yssm, modp, modp, modp, *consts)


GT = 8
PREP_TILES = 2


def _sample_ssm_body(u_ref, h0r_ref, h0i_ref, ar_ref, ai_ref, bre_ref, bim_ref, cblk_ref, d_ref,
                     y_ref, hr_ref, hi_ref):
    ub = u_ref[...]
    ys = []
    for a in range(GT):
        ua = ub[:, a * 128:(a + 1) * 128]
        st = slice(a * 512, (a + 1) * 512)
        xr = _dot(ua, bre_ref[a])
        xi = _dot(ua, bim_ref[a])
        h0r = h0r_ref[:, st]
        h0i = h0i_ref[:, st]
        ar = ar_ref[:, st]
        ai = ai_ref[:, st]
        hr = ar * h0r - ai * h0i + xr
        hi = ar * h0i + ai * h0r + xi
        hr_ref[:, st] = hr
        hi_ref[:, st] = hi
        hcat = jnp.concatenate([hr, hi], axis=1).astype(BF16)
        ys.append(_dot_nt(hcat, cblk_ref[a]))
    y_ref[...] = jnp.concatenate(ys, axis=1) + d_ref[...] * ub.astype(F32)


def _sample_ssm(proj_s, h0r, h0i, ar, ai, bre, bim, cblk, d_row):
    rows = proj_s.shape[0]
    ns = SSM_GROUPS * SSM_STATE
    full = lambda shape: pl.BlockSpec(shape, lambda i: (0,) * len(shape))
    return pl.pallas_call(
        _sample_ssm_body,
        grid=(1,),
        in_specs=[pl.BlockSpec((rows, D_MODEL), lambda i: (0, COL_U)),
                  full((rows, ns)), full((rows, ns)), full((1, ns)), full((1, ns)),
                  full(bre.shape), full(bim.shape), full(cblk.shape), full((1, D_MODEL))],
        out_specs=[full((rows, D_MODEL)), full((rows, ns)), full((rows, ns))],
        out_shape=[jax.ShapeDtypeStruct((rows, D_MODEL), F32),
                   jax.ShapeDtypeStruct((rows, ns), F32),
                   jax.ShapeDtypeStruct((rows, ns), F32)],
        compiler_params=pltpu.CompilerParams(vmem_limit_bytes=VMEM_LIMIT),
        name="sample_ssm",
    )(proj_s, h0r, h0i, ar, ai, bre, bim, cblk, d_row)


SB = 32
KROWS = WINDOW + 8
SEQ_GROUP = 4


def _sample_attn_body(q_ref, kn_ref, vn_ref, ck_ref, cv_ref, bias_ref, sink_ref, seg_ref, segt_ref,
                      o_ref, nk_ref, nv_ref):
    qf = q_ref[...].astype(F32)
    knf = kn_ref[...].astype(F32)
    vnf = vn_ref[...].astype(F32)
    rowk = lax.broadcasted_iota(jnp.int32, (KROWS, 128), 0)
    roww = lax.broadcasted_iota(jnp.int32, (WINDOW, KV_WIDTH), 0)
    sink = sink_ref[...]
    last = roww == WINDOW - 1
    for g0 in range(0, SB, SEQ_GROUP):
        seqs = range(g0, g0 + SEQ_GROUP)
        scores = []
        for b in seqs:
            kall = jnp.concatenate([ck_ref[b], jnp.broadcast_to(knf[b:b + 1], (8, KV_WIDTH))], axis=0)
            qrow = qf[b:b + 1]
            prod = jnp.concatenate([kall * qrow[:, j * KV_WIDTH:(j + 1) * KV_WIDTH] for j in range(KV_GROUP)], axis=1)
            s = _dot(prod.astype(BF16), seg_ref[...]) + bias_ref[...]
            scores.append(jnp.where(rowk <= WINDOW, s, NEG_INF))
        probs = []
        for s in scores:
            m = jnp.maximum(jnp.max(s, axis=0, keepdims=True), sink)
            e = jnp.exp(s - m)
            den = jnp.sum(e, axis=0, keepdims=True) + jnp.exp(sink - m)
            probs.append((e / den).astype(BF16))
        pexps = [_dot(p, segt_ref[...]) for p in probs]
        for b, pexp in zip(seqs, pexps):
            vb = cv_ref[b]
            vall = jnp.concatenate([vb, jnp.broadcast_to(vnf[b:b + 1], (8, KV_WIDTH))], axis=0)
            v4 = jnp.concatenate([vall] * KV_GROUP, axis=1)
            o_ref[b:b + 1, :] = jnp.sum(pexp * v4, axis=0, keepdims=True)
            nv_ref[b] = jnp.where(last, jnp.broadcast_to(vnf[b:b + 1], (WINDOW, KV_WIDTH)), pltpu.roll(vb, WINDOW - 1, 0))
            nk_ref[b] = jnp.where(last, jnp.broadcast_to(knf[b:b + 1], (WINDOW, KV_WIDTH)),
                                  pltpu.roll(ck_ref[b], WINDOW - 1, 0))


def _sample_attn(proj_s, ck, cv, bias_s, sink_row, seg, segt):
    rows = proj_s.shape[0]
    cache_spec = pl.BlockSpec((SB, WINDOW, KV_WIDTH), lambda i: (i, 0, 0))
    return pl.pallas_call(
        _sample_attn_body,
        grid=(rows // SB,),
        in_specs=[pl.BlockSpec((SB, D_MODEL), lambda i: (i, COL_Q)),
                  pl.BlockSpec((SB, KV_WIDTH), lambda i: (i, COL_K)),
                  pl.BlockSpec((SB, KV_WIDTH), lambda i: (i, COL_V)),
                  cache_spec, cache_spec,
                  pl.BlockSpec(bias_s.shape, lambda i: (0, 0)),
                  pl.BlockSpec(sink_row.shape, lambda i: (0, 0)),
                  pl.BlockSpec(seg.shape, lambda i: (0, 0)),
                  pl.BlockSpec(segt.shape, lambda i: (0, 0))],
        out_specs=[pl.BlockSpec((SB, D_MODEL), lambda i: (i, 0)), cache_spec, cache_spec],
        out_shape=[jax.ShapeDtypeStruct((rows, D_MODEL), F32),
                   jax.ShapeDtypeStruct(ck.shape, F32),
                   jax.ShapeDtypeStruct(cv.shape, F32)],
        compiler_params=pltpu.CompilerParams(vmem_limit_bytes=VMEM_LIMIT),
        name="sample_attn",
    )(proj_s, proj_s, proj_s, ck, cv, bias_s, sink_row, seg, segt)


def _sample_final_body(x_ref, yssm_ref, zs_ref, o_ref, za_ref, gs_ref, ga_ref, gate_ref,
                       wglu_ref, bglu_ref, wbs_ref, wba_ref, wout_ref, lng_ref, lnb_ref, y_ref):
    load = lambda: (yssm_ref[...], zs_ref[...], o_ref[...], za_ref[...], gs_ref[...], ga_ref[...], x_ref[...])
    y_ref[...] = _post([load], gate_ref[...], wglu_ref, bglu_ref, wbs_ref, wba_ref, wout_ref, lng_ref, lnb_ref)[0]


def _final_sample(x2d, yssm, o_a, proj_s, mods, wglu, bglu, wbs, wba, wout, lng, lnb):
    rows = x2d.shape[0]
    full = lambda shape: pl.BlockSpec(shape, lambda i: (0,) * len(shape))
    pcol = lambda c: pl.BlockSpec((rows, D_MODEL), lambda i, c=c: (0, c))
    return pl.pallas_call(
        _sample_final_body,
        grid=(1,),
        in_specs=[full((rows, D_MODEL)), full((rows, D_MODEL)), pcol(COL_ZS), full((rows, D_MODEL)),
                  pcol(COL_ZA), pcol(COL_GS), pcol(COL_GA),
                  pl.BlockSpec((rows, D_MODEL), lambda i: (0, 2)),
                  full(wglu.shape), full(bglu.shape), full(wbs.shape), full(wba.shape), full(wout.shape),
                  full(lng.shape), full(lnb.shape)],
        out_specs=full((rows, D_MODEL)),
        out_shape=jax.ShapeDtypeStruct((rows, D_MODEL), F32),
        compiler_params=pltpu.CompilerParams(vmem_limit_bytes=VMEM_LIMIT),
        name="final_sample",
    )(x2d, yssm, proj_s, o_a, proj_s, proj_s, proj_s, mods, wglu, bglu, wbs, wba, wout, lng, lnb)


def _bucket_table():
    max_exact = N_BUCKETS // 2
    dist = np.arange(WINDOW + 1)
    df = np.maximum(dist, 1).astype(np.float32)
    large = max_exact + (np.log(df / np.float32(max_exact)) / np.float32(math.log(WINDOW / max_exact))
                         * np.float32(N_BUCKETS - max_exact)).astype(np.int32)
    large = np.minimum(large, N_BUCKETS - 1)
    return np.where(dist < max_exact, dist, large)


def _dot3_nt(a, b):
    ah = a.astype(BF16)
    al = (a - ah.astype(F32)).astype(BF16)
    bh = b.astype(BF16)
    bl = (b - bh.astype(F32)).astype(BF16)
    return _dot_nt(ah, bh) + _dot_nt(ah, bl) + _dot_nt(al, bh)


def _ssm_prep_body(lre_ref, lim_ref, ld_ref, tau_ref, bt_ref, bts_ref, c2_ref, c2s_ref, d_ref,
                   lhs_ref, bpow_ref, cpow_ref, bre_ref, bim_ref, cblk_ref, p1_ref, p2_ref):
    lane = lax.broadcasted_iota(jnp.int32, (SSM_GROUP, SW), 1)
    left = lane < SSM_STATE
    sgn = jnp.where(left, 1.0, -1.0)
    row_t = lax.broadcasted_iota(jnp.int32, (CW, 128), 0)
    lane_t = lax.broadcasted_iota(jnp.int32, (CW, 128), 1)
    bre_ref[...] = jnp.zeros_like(bre_ref)
    bim_ref[...] = jnp.zeros_like(bim_ref)
    cblk_ref[...] = jnp.zeros_like(cblk_ref)
    half_sign = jnp.where(lax.broadcasted_iota(jnp.int32, (1, SW), 1) < SSM_STATE, -1.0, 1.0)
    tau = tau_ref[...]
    for gi in range(PREP_TILES * GT):
        ti_, gl = divmod(gi, GT)
        lr = jnp.concatenate([lre_ref[gi:gi + 1, :]] * 2, axis=1)
        li = jnp.concatenate([lim_ref[gi:gi + 1, :]] * 2, axis=1)
        dt = jnp.exp(ld_ref[gi:gi + 1, :])
        mag = jnp.exp((lr * dt) * tau)
        ang = (li * dt) * tau
        p1 = mag * jnp.cos(ang)
        pim = mag * jnp.sin(ang)
        p2 = pim * half_sign
        p1_ref[gi] = p1
        p2_ref[gi] = p2
        ar, ai = p1[1:2, :], pim[1:2, :]
        den = lr * lr + li * li
        nr = ar - 1.0
        k1 = (nr * lr + ai * li) / den
        k2 = (ai * lr - nr * li) / den * half_sign
        bt, bts = bt_ref[gi], bts_ref[gi]
        c2, c2s = c2_ref[gi], c2s_ref[gi]
        bb = k1 * bt + k2 * bts
        bbs = k1 * bts - k2 * bt
        ca = [sgn * (c2 * p1[t:t + 1, :] + c2s * p2[t:t + 1, :]) for t in range(CHUNK + 1)]
        cpow_ref[gi] = jnp.concatenate(ca[1:], axis=0).astype(BF16)
        bpow_ref[gi] = jnp.concatenate(
            [p1[CHUNK - 1 - s:CHUNK - s, :] * bb + p2[CHUNK - 1 - s:CHUNK - s, :] * bbs for s in range(CHUNK)],
            axis=0).astype(BF16)
        e = _dot3_nt(jnp.concatenate(ca[:CHUNK], axis=0), jnp.concatenate([bb] * CHUNK, axis=0))
        tiles = []
        for lt in range(CW // 128):
            cols = slice(lt * 128, (lt + 1) * 128)
            et = e[:, cols]
            blk = (lane_t + lt * 128) // SSM_GROUP
            at = jnp.where(row_t == lane_t + lt * 128, d_ref[gi, :, cols], 0.0)
            for s in range(lt * 128 // SSM_GROUP, (lt + 1) * 128 // SSM_GROUP):
                sh = et if s == 0 else jnp.concatenate(
                    [jnp.zeros((s * SSM_GROUP, 128), F32), et[:CW - s * SSM_GROUP, :]], axis=0)
                at = at + jnp.where(blk == s, sh, 0.0)
            tiles.append(at)
        lhs_ref[gi] = jnp.concatenate(tiles, axis=1).astype(BF16)
        rs = slice(gl * SSM_GROUP, (gl + 1) * SSM_GROUP)
        ts = slice((gl // 2) * SW, (gl // 2 + 1) * SW)
        ti = slice(GT * SSM_STATE + (gl // 2) * SW, GT * SSM_STATE + (gl // 2 + 1) * SW)
        if gl % 2 == 0:
            bre_ref[ti_, rs, ts] = jnp.where(left, bb, 0.0).astype(BF16)
            bim_ref[ti_, rs, ts] = jnp.where(left, bbs, 0.0).astype(BF16)
            cblk_ref[ti_, rs, ts] = jnp.where(left, c2, 0.0).astype(BF16)
            cblk_ref[ti_, rs, ti] = jnp.where(left, -c2s, 0.0).astype(BF16)
        else:
            bre_ref[ti_, rs, ts] = jnp.where(left, 0.0, bbs).astype(BF16)
            bim_ref[ti_, rs, ts] = jnp.where(left, 0.0, bb).astype(BF16)
            cblk_ref[ti_, rs, ts] = jnp.where(left, 0.0, c2s).astype(BF16)
            cblk_ref[ti_, rs, ti] = jnp.where(left, 0.0, -c2).astype(BF16)


def _ssm_params(lam_re, lam_im, log_delta, b_re, b_im, c_re, c_im, d_skip):
    g = SSM_GROUPS
    nrow = len(POW_TAUS)
    tau = jnp.asarray(np.broadcast_to(np.asarray(POW_TAUS, np.float32)[:, None], (nrow, SW)))
    brt, bit = jnp.swapaxes(b_re, 1, 2), jnp.swapaxes(b_im, 1, 2)
    bt = jnp.concatenate([brt, bit], axis=-1)
    bts = jnp.concatenate([bit, brt], axis=-1)
    c2 = jnp.concatenate([c_re, c_im], axis=-1)
    c2s = jnp.concatenate([c_im, c_re], axis=-1)
    dtile = jnp.tile(d_skip.reshape(g, 1, SSM_GROUP), (1, 1, CHUNK))

    gstep = PREP_TILES * GT
    gspec = lambda r, w: pl.BlockSpec((gstep, r, w), lambda a: (a, 0, 0))
    rspec = lambda w: pl.BlockSpec((gstep, w), lambda a: (a, 0))
    tile_spec = lambda w: pl.BlockSpec((PREP_TILES, GT * SSM_GROUP, w), lambda a: (a, 0, 0))
    lhs, bpow, cpow, bre, bim, cblk, p1, p2 = pl.pallas_call(
        _ssm_prep_body,
        grid=(g // gstep,),
        in_specs=[rspec(SSM_STATE), rspec(SSM_STATE), rspec(1), pl.BlockSpec((nrow, SW), lambda a: (0, 0)),
                  gspec(SSM_GROUP, SW), gspec(SSM_GROUP, SW), gspec(SSM_GROUP, SW), gspec(SSM_GROUP, SW),
                  gspec(1, CW)],
        out_specs=[gspec(CW, CW), gspec(CW, SW), gspec(CW, SW),
                   tile_spec(GT * SSM_STATE), tile_spec(GT * SSM_STATE), tile_spec(2 * GT * SSM_STATE),
                   gspec(nrow, SW), gspec(nrow, SW)],
        out_shape=[jax.ShapeDtypeStruct((g, CW, CW), BF16), jax.ShapeDtypeStruct((g, CW, SW), BF16),
                   jax.ShapeDtypeStruct((g, CW, SW), BF16),
                   jax.ShapeDtypeStruct((GT, GT * SSM_GROUP, GT * SSM_STATE), BF16),
                   jax.ShapeDtypeStruct((GT, GT * SSM_GROUP, GT * SSM_STATE), BF16),
                   jax.ShapeDtypeStruct((GT, GT * SSM_GROUP, 2 * GT * SSM_STATE), BF16),
                   jax.ShapeDtypeStruct((g, nrow, SW), F32), jax.ShapeDtypeStruct((g, nrow, SW), F32)],
        name="ssm_prep",
    )(lam_re.astype(F32), lam_im.astype(F32), log_delta.astype(F32).reshape(g, 1), tau, bt, bts, c2, c2s, dtile)
    ar_row = p1[:, 1, :SSM_STATE].reshape(1, -1)
    ai_row = p2[:, 1, SSM_STATE:].reshape(1, -1)
    return lhs, bpow, cpow, p1, p2, ar_row, ai_row, bre, bim, cblk


def _attn_params(rel_bias, sinks):
    hp = lax.Precision.HIGHEST
    rb = rel_bias.astype(F32)
    onehot = jnp.asarray(_bucket_table()[:, None] == np.arange(N_BUCKETS)[None, :], F32)
    tbl = jnp.dot(onehot, rb, precision=hp)
    sink_rows = jnp.repeat(sinks.astype(F32).reshape(N_KV_HEADS, 1, KV_GROUP), WINDOW, axis=2)
    ds = np.clip(WINDOW - np.arange(KROWS), 0, WINDOW)
    bs = jnp.dot(jnp.asarray(ds[:, None] == np.arange(WINDOW + 1)[None, :], F32), tbl, precision=hp)
    bs = jnp.transpose(bs.reshape(KROWS, N_KV_HEADS, KV_GROUP), (0, 2, 1)).reshape(KROWS, N_HEADS)
    bias_s = jnp.pad(bs, ((0, 0), (0, 128 - N_HEADS)))
    sk = jnp.transpose(sinks.astype(F32).reshape(N_KV_HEADS, KV_GROUP), (1, 0)).reshape(1, N_HEADS)
    sink_row = jnp.pad(sk, ((0, 0), (0, 128 - N_HEADS)))
    seg_np = np.zeros((D_MODEL, 128), np.float32)
    for j in range(KV_GROUP):
        for kh in range(N_KV_HEADS):
            r0 = j * KV_WIDTH + kh * HEAD_DIM
            seg_np[r0:r0 + HEAD_DIM, j * N_KV_HEADS + kh] = 1.0
    seg = jnp.asarray(seg_np, BF16)
    segt = jnp.asarray(seg_np.T, BF16)
    return sink_rows, bias_s, sink_row, seg, segt


WP_ROWS = 256


def _regroup_tiles(w):
    left = lax.broadcasted_iota(jnp.int32, (w.shape[0], 128), 1) < HEAD_DIM
    tile = lambda t: w[:, t * 128:(t + 1) * 128]
    out = []
    for j in range(KV_GROUP):
        for pair in range(N_KV_HEADS // 2):
            t1 = tile((2 * pair) * 2 + j // 2)
            t2 = tile((2 * pair + 1) * 2 + j // 2)
            if j % 2 == 0:
                out.append(jnp.where(left, t1, pltpu.roll(t2, HEAD_DIM, 1)))
            else:
                out.append(jnp.where(left, pltpu.roll(t1, HEAD_DIM, 1), t2))
    return jnp.concatenate(out, axis=1)


def _weights_body(win_ref, wglu_ref, wbs_ref, wba_ref, wout_ref,
                  winb_ref, wut_ref, wglub_ref, wbsb_ref, wbab_ref, woutb_ref):
    i = pl.program_id(0)
    seg = lambda a: win_ref[:, a:a + D_MODEL]
    u = seg(0)
    winb_ref[:, 0:D_MODEL] = u.astype(BF16)
    winb_ref[:, D_MODEL:2 * D_MODEL] = seg(D_MODEL).astype(BF16)
    winb_ref[:, 2 * D_MODEL:3 * D_MODEL] = (_regroup_tiles(seg(2 * D_MODEL)) * (HEAD_DIM ** -0.5)).astype(BF16)
    winb_ref[:, 3 * D_MODEL:4 * D_MODEL] = _regroup_tiles(seg(3 * D_MODEL + 2 * KV_WIDTH)).astype(BF16)
    winb_ref[:, 4 * D_MODEL:5 * D_MODEL] = seg(4 * D_MODEL + 2 * KV_WIDTH).astype(BF16)
    winb_ref[:, 5 * D_MODEL:6 * D_MODEL] = seg(5 * D_MODEL + 2 * KV_WIDTH).astype(BF16)
    winb_ref[:, 6 * D_MODEL:6 * D_MODEL + 2 * KV_WIDTH] = \
        win_ref[:, 3 * D_MODEL:3 * D_MODEL + 2 * KV_WIDTH].astype(BF16)
    wut_ref[...] = u.astype(BF16).T
    wglub_ref[...] = wglu_ref[...].astype(BF16)
    wbsb_ref[...] = wbs_ref[...].astype(BF16)
    woutb_ref[...] = wout_ref[...].astype(BF16)
    wba = wba_ref[...].astype(BF16)
    for j in range(KV_GROUP):
        r0 = pl.multiple_of(j * KV_WIDTH + i * HEAD_DIM, HEAD_DIM)
        wbab_ref[pl.ds(r0, HEAD_DIM), :] = wba[j * HEAD_DIM:(j + 1) * HEAD_DIM, :]


def _weights_prep(w_in0, w_glu0, w_bs0, w_ba0, w_out0):
    assert WP_ROWS == KV_GROUP * HEAD_DIM
    rows = lambda w: pl.BlockSpec((WP_ROWS, w), lambda i: (i, 0))
    sq = jax.ShapeDtypeStruct((D_MODEL, D_MODEL), BF16)
    return pl.pallas_call(
        _weights_body,
        grid=(D_MODEL // WP_ROWS,),
        in_specs=[rows(D_IN), rows(D_MODEL), rows(D_MODEL), rows(D_MODEL), rows(D_MODEL)],
        out_specs=[rows(D_IN), pl.BlockSpec((D_MODEL, WP_ROWS), lambda i: (0, i)), rows(D_MODEL), rows(D_MODEL),
                   pl.BlockSpec((D_MODEL, D_MODEL), lambda i: (0, 0)), rows(D_MODEL)],
        out_shape=[jax.ShapeDtypeStruct((D_MODEL, D_IN), BF16), sq, sq, sq, sq, sq],
        compiler_params=pltpu.CompilerParams(dimension_semantics=("arbitrary",), vmem_limit_bytes=VMEM_LIMIT),
        name="weights_prep",
    )(w_in0, w_glu0, w_bs0, w_ba0, w_out0)


def kernel(x_prompt, x_sample, c_prompt, c_sample, state_ssm_re, state_ssm_im, cache_swa_k, cache_swa_v,
           w_ada, b_ada, w_in, ssm_lambda_re, ssm_lambda_im, ssm_log_delta, ssm_b_re, ssm_b_im,
           ssm_c_re, ssm_c_im, ssm_d, w_glu, b_glu, attn_sinks, rel_bias, w_branch_s, w_branch_a,
           w_out, ln_g, ln_b):
    assert w_ada.shape[0] == 1, "single-layer trunk"
    n_batch, seq, _ = x_prompt.shape
    n_dec = x_sample.shape[0]

    w_in_b, wut, wglu, wbs, wba, wout = _weights_prep(w_in[0], w_glu[0], w_branch_s[0], w_branch_a[0], w_out[0])
    bglu = b_glu[0].reshape(1, -1).astype(F32)
    lng = ln_g[0].reshape(1, -1).astype(F32)
    lnb = ln_b[0].reshape(1, -1).astype(F32)
    lhs, bpow, cpow, p1, p2, ar_row, ai_row, bre, bim, cblk = _ssm_params(
        ssm_lambda_re[0], ssm_lambda_im[0], ssm_log_delta[0], ssm_b_re[0], ssm_b_im[0],
        ssm_c_re[0], ssm_c_im[0], ssm_d[0])
    sink_rows, bias_s, sink_row, seg, segt = _attn_params(rel_bias, attn_sinks[0])

    c_all = jnp.concatenate([c_prompt, jnp.zeros((8 - n_batch, D_MODEL), F32), c_sample], axis=0)
    mod = _ada(c_all, w_ada[0], b_ada[0])
    modp = mod[:n_batch].reshape(n_batch * 3, 1, D_MODEL)
    mods = mod[8:8 + n_dec]

    xp = x_prompt.reshape(n_batch * seq, D_MODEL)
    yssm, hfin = _ssm_prompt(x_prompt, modp, wut, lhs, bpow, cpow, p1, p2)
    yp, ktail, vtail = _main_prompt(rel_bias.astype(F32).reshape(-1), xp, yssm, modp, w_in_b, sink_rows,
                                    wglu, bglu, wbs, wba, wout, lng, lnb, n_batch, seq)
    yp = yp.reshape(n_batch, seq, D_MODEL)
    p_hr = hfin[:, :, 0, :SSM_STATE][None]
    p_hi = hfin[:, :, 0, SSM_STATE:][None]
    p_k = ktail.reshape(1, n_batch, WINDOW, N_KV_HEADS, HEAD_DIM)
    p_v = vtail.reshape(1, n_batch, WINDOW, N_KV_HEADS, HEAD_DIM)

    xs = x_sample.reshape(n_dec, D_MODEL)
    proj_s = _proj_sample(xs, mods, w_in_b)
    ns = SSM_GROUPS * SSM_STATE
    ys_s, s_hr, s_hi = _sample_ssm(proj_s, state_ssm_re[0].reshape(n_dec, ns), state_ssm_im[0].reshape(n_dec, ns),
                                   ar_row, ai_row, bre, bim, cblk, ssm_d[0].reshape(1, -1).astype(F32))
    o_s, nk, nv = _sample_attn(proj_s, cache_swa_k[0].reshape(n_dec, WINDOW, KV_WIDTH),
                               cache_swa_v[0].reshape(n_dec, WINDOW, KV_WIDTH), bias_s, sink_row, seg, segt)
    ysmp = _final_sample(xs, ys_s, o_s, proj_s, mods, wglu, bglu, wbs, wba, wout, lng, lnb)

    return (yp, ysmp.reshape(n_dec, 1, D_MODEL),
            p_hr, p_hi, p_k, p_v,
            s_hr.reshape(1, n_dec, SSM_GROUPS, SSM_STATE), s_hi.reshape(1, n_dec, SSM_GROUPS, SSM_STATE),
            nk.reshape(1, n_dec, WINDOW, N_KV_HEADS, HEAD_DIM), nv.reshape(1, n_dec, WINDOW, N_KV_HEADS, HEAD_DIM))
```

```python
import math

import numpy as np
import jax
import jax.numpy as jnp
from jax import lax
from jax.experimental import pallas as pl
from jax.experimental.pallas import tpu as pltpu

F32 = jnp.float32
BF16 = jnp.bfloat16

D_MODEL = 1024
SSM_GROUPS = 64
SSM_GROUP = 16
SSM_STATE = 64
N_HEADS = 16
HEAD_DIM = 64
N_KV_HEADS = 4
KV_GROUP = 4
KV_WIDTH = N_KV_HEADS * HEAD_DIM
WINDOW = 128
N_BUCKETS = 32
NEG_INF = -1e30
LN_EPS = 1e-5
DEPTH = 1
DEEPNORM_ALPHA = (2 * DEPTH) ** 0.25
D_IN = 6656
CHUNK = 16
CW = CHUNK * SSM_GROUP
SW = 2 * SSM_STATE

COL_U, COL_ZS, COL_Q, COL_ZA, COL_GS, COL_GA = 0, 1, 2, 3, 4, 5
COL_K, COL_V = 24, 25

VMEM_LIMIT = 56 * 1024 * 1024
VMEM_LIMIT_SSM = 60 * 1024 * 1024


def _sigmoid(x):
    return 0.5 * jnp.tanh(0.5 * x) + 0.5


def _silu(x):
    return x * _sigmoid(x)


def _gelu_tanh(x):
    c = math.sqrt(2.0 / math.pi)
    hx = 0.5 * x
    return hx + hx * jnp.tanh(x * (c + (c * 0.044715) * (x * x)))


def _dot(a, b):
    return jnp.dot(a, b, preferred_element_type=F32)


def _dot_nt(a, b):
    return lax.dot_general(a, b, (((1,), (1,)), ((), ())), preferred_element_type=F32)


def _ada_body(c_ref, w_ref, b_ref, o_ref):
    c = c_ref[...]
    sc = _silu(c).astype(BF16)
    o_ref[...] = _dot(sc, w_ref[...].astype(BF16)) + b_ref[...]


def _ada(c_all, w_ada, b_ada):
    rows = c_all.shape[0]
    bn = 1536
    return pl.pallas_call(
        _ada_body,
        grid=(3 * D_MODEL // bn,),
        in_specs=[pl.BlockSpec((rows, D_MODEL), lambda j: (0, 0)),
                  pl.BlockSpec((D_MODEL, bn), lambda j: (0, j)),
                  pl.BlockSpec((1, bn), lambda j: (0, j))],
        out_specs=pl.BlockSpec((rows, bn), lambda j: (0, j)),
        out_shape=jax.ShapeDtypeStruct((rows, 3 * D_MODEL), F32),
        name="ada",
    )(c_all, w_ada, b_ada.reshape(1, -1))


PROJ_CB = 512


def _proj_body(x_ref, shift_ref, scale_ref, w_ref, o_ref):
    h = x_ref[...] * (1.0 + scale_ref[...]) + shift_ref[...]
    hb = h.astype(BF16)
    for j in range(D_IN // PROJ_CB):
        sl = slice(j * PROJ_CB, (j + 1) * PROJ_CB)
        o_ref[:, sl] = _dot(hb, w_ref[:, sl]).astype(BF16)


def _proj_sample(xs, mods, w_in_b):
    rows = xs.shape[0]
    return pl.pallas_call(
        _proj_body,
        grid=(1,),
        in_specs=[pl.BlockSpec((rows, D_MODEL), lambda i: (0, 0)),
                  pl.BlockSpec((rows, D_MODEL), lambda i: (0, 0)),
                  pl.BlockSpec((rows, D_MODEL), lambda i: (0, 1)),
                  pl.BlockSpec((D_MODEL, D_IN), lambda i: (0, 0))],
        out_specs=pl.BlockSpec((rows, D_IN), lambda i: (0, 0)),
        out_shape=jax.ShapeDtypeStruct((rows, D_IN), BF16),
        compiler_params=pltpu.CompilerParams(vmem_limit_bytes=VMEM_LIMIT),
        name="proj",
    )(xs, mods, mods, w_in_b)


NKS = 128
GC = 16


def _cmul(c1, c2, x):
    return c1 * x + c2 * pltpu.roll(x, SSM_STATE, x.ndim - 1)


SSM_TOK = NKS * CHUNK
SUB = 8
NPH = CHUNK // SUB


def _dot_tn(a, b):
    return lax.dot_general(a, b, (((0,), (0,)), ((), ())), preferred_element_type=F32)


SEG = 8
SEG_LEN = NKS // SEG
POW_TAUS = (tuple(range(CHUNK + 1)) + tuple(CHUNK * m for m in range(2, SEG_LEN))
            + tuple(CHUNK * SEG_LEN * m for m in range(1, SEG + 1)) + (0,))
ROW_CHUNK1 = CHUNK
ROW_SEG1 = CHUNK + SEG_LEN - 1


def _ssm_body(x_ref, shift_ref, scale_ref, wut_ref, lhs_ref, bpow_ref, cpow_ref, p1_ref, p2_ref,
              y_ref, hfin_ref, r_scr, yt_scr, s_scr, carry_scr):
    i = pl.program_id(1)
    j = pl.program_id(2)
    n_blocks = pl.num_programs(1) - 1

    @pl.when((i == 0) & (j == 0))
    def _():
        carry_scr[...] = jnp.zeros_like(carry_scr)

    def load_pair(sp):
        h = jnp.concatenate([x_ref[:, b, s, :] for s in (2 * sp, 2 * sp + 1) for b in range(SEG_LEN)],
                            axis=0)
        hb = (h * (1.0 + scale_ref[...]) + shift_ref[...]).astype(BF16)
        ut = _dot_nt(wut_ref[...], hb).astype(BF16)
        ut = ut.reshape(SSM_GROUPS, SSM_GROUP, 2 * NKS)
        pair = jnp.concatenate([ut[:, :, :NKS], ut[:, :, NKS:]], axis=1)
        r0 = pl.multiple_of(j * SUB * SSM_GROUP + sp * 2 * SSM_GROUP, 2 * SSM_GROUP)
        r_scr[:, pl.ds(r0, 2 * SSM_GROUP), :] = pair

    def emit_subset(sp):
        t0 = pl.multiple_of(j * SUB * SSM_GROUP + sp * SSM_GROUP, SSM_GROUP)
        yt = yt_scr[:, pl.ds(t0, SSM_GROUP), :].reshape(D_MODEL, NKS)
        yr = yt.T
        for b in range(SEG_LEN):
            y_ref[:, b, sp, :] = yr[b * SEG:(b + 1) * SEG, :]

    @pl.when(i == 0)
    def _():
        y_ref[...] = jnp.zeros_like(y_ref)
        for sp in range(SUB // 2):
            load_pair(sp)

    @pl.when((i > 0) & (i < n_blocks))
    def _():
        for sp in range(SUB // 2):
            emit_subset(2 * sp)
            load_pair(sp)
            emit_subset(2 * sp + 1)

    @pl.when(i == n_blocks)
    def _():
        for sp in range(SUB):
            emit_subset(sp)

    @pl.when((i < n_blocks) & (j == NPH - 1))
    def _():
        def local(g):
            rg = r_scr[g]
            yt_scr[g] = _dot(lhs_ref[g], rg)
            s_scr[g] = _dot_tn(rg, bpow_ref[g])

        def carried(g):
            yt_scr[g] += _dot_nt(cpow_ref[g], s_scr[g].astype(BF16))

        row = lax.broadcasted_iota(jnp.int32, (GC, SEG, SW), 1)
        seg_rows = lambda b: slice(b * SEG, (b + 1) * SEG)

        def scan(gc):
            gs = slice(gc * GC, (gc + 1) * GC)
            p1, p2 = p1_ref[gs], p2_ref[gs]
            c1, c2 = p1[:, ROW_CHUNK1:ROW_CHUNK1 + 1, :], p2[:, ROW_CHUNK1:ROW_CHUNK1 + 1, :]
            h = jnp.zeros((GC, SEG, SW), F32)
            hs = h
            for b in range(SEG_LEN):
                sb = s_scr[gs, seg_rows(b), :]
                h, hs = sb + c1 * h + c2 * hs, pltpu.roll(sb, SSM_STATE, 2) + c1 * hs - c2 * h
                s_scr[gs, seg_rows(b), :] = h
                yield
            e = h
            for d in (1, 2, 4):
                rd = ROW_SEG1 + d - 1
                sh = jnp.where(row >= d, pltpu.roll(e, d, 1), 0.0)
                e = e + _cmul(p1[:, rd:rd + 1, :], p2[:, rd:rd + 1, :], sh)
            carry = carry_scr[gs]
            e = e + _cmul(p1[:, ROW_SEG1:ROW_SEG1 + SEG, :], p2[:, ROW_SEG1:ROW_SEG1 + SEG, :], carry)
            ein = jnp.where(row == 0, carry, pltpu.roll(e, 1, 1))
            eins = pltpu.roll(ein, SSM_STATE, 2)
            carry_scr[gs] = jnp.broadcast_to(e[:, SEG - 1:SEG, :], (GC, SEG, SW))
            yield
            for b in range(SEG_LEN - 1, 0, -1):
                rb = ROW_CHUNK1 + b - 1
                s_scr[gs, seg_rows(b), :] = (s_scr[gs, seg_rows(b - 1), :]
                                             + p1[:, rb:rb + 1, :] * ein + p2[:, rb:rb + 1, :] * eins)
                yield
            s_scr[gs, seg_rows(0), :] = ein

        def interleave(fn, groups, gen):
            for g in groups:
                fn(g)
                for _ in range(2):
                    next(gen, None)
            for _ in gen:
                pass

        n_sets = SSM_GROUPS // GC
        sets = [range(q * GC, (q + 1) * GC) for q in range(n_sets)]
        for g in sets[0]:
            local(g)
        for q in range(1, n_sets):
            interleave(local, sets[q], scan(q - 1))
        interleave(carried, sets[0], scan(n_sets - 1))
        for q in range(1, n_sets):
            for g in sets[q]:
                carried(g)
        hfin_ref[...] = carry_scr[...]


def _ssm_prompt(x_prompt, modp, wut, lhs, bpow, cpow, p1, p2):
    n_batch, seq, _ = x_prompt.shape
    steps = seq // SSM_TOK
    xv = x_prompt.reshape(n_batch * steps * SEG, SEG_LEN, CHUNK, D_MODEL)
    mod_spec = lambda k: pl.BlockSpec((None, 1, D_MODEL), lambda b, i, j, k=k: (b * 3 + k, 0, 0))
    consts = (wut, lhs, bpow, cpow, p1, p2)
    y, hfin = pl.pallas_call(
        _ssm_body,
        grid=(n_batch, steps + 1, NPH),
        in_specs=[pl.BlockSpec((SEG, SEG_LEN, SUB, D_MODEL),
                               lambda b, i, j: (b * steps + jnp.minimum(i, steps - 1), 0, j, 0)),
                  mod_spec(0), mod_spec(1)] + [_const_spec(c.shape) for c in consts],
        out_specs=[pl.BlockSpec((SEG, SEG_LEN, SUB, D_MODEL),
                                lambda b, i, j: (b * (steps + 1) + (i + steps) % (steps + 1), 0, j, 0)),
                   pl.BlockSpec((None, SSM_GROUPS, 8, SW), lambda b, i, j: (b, 0, 0, 0))],
        out_shape=[jax.ShapeDtypeStruct((n_batch * (steps + 1) * SEG, SEG_LEN, CHUNK, D_MODEL), F32),
                   jax.ShapeDtypeStruct((n_batch, SSM_GROUPS, 8, SW), F32)],
        scratch_shapes=[pltpu.VMEM((SSM_GROUPS, CW, NKS), BF16), pltpu.VMEM((SSM_GROUPS, CW, NKS), F32),
                        pltpu.VMEM((SSM_GROUPS, NKS, SW), F32), pltpu.VMEM((SSM_GROUPS, 8, SW), F32)],
        compiler_params=pltpu.CompilerParams(dimension_semantics=("arbitrary", "arbitrary", "arbitrary"),
                                             vmem_limit_bytes=VMEM_LIMIT_SSM),
        name="ssm_prompt",
    )(xv, modp, modp, *consts)
    return y.reshape(n_batch * (seq + SSM_TOK), D_MODEL), hfin


def _post(chunks, gate, wglu_ref, bglu_ref, wbs_ref, wba_ref, wout_ref, lng_ref, lnb_ref):
    data = [load() for load in chunks]
    ys = [_gelu_tanh(d[0]) for d in data]
    glu = [y * _sigmoid(_dot(y.astype(BF16), wglu_ref[...]) + bglu_ref[...]) for y in ys]
    b_s = [_dot(g.astype(BF16) * _silu(d[1]), wbs_ref[...]) for g, d in zip(glu, data)]
    b_a = [_dot(d[2].astype(BF16) * _silu(d[3]), wba_ref[...]) for d in data]
    m = [_sigmoid(d[4]).astype(F32) * s + _sigmoid(d[5]).astype(F32) * a for d, s, a in zip(data, b_s, b_a)]
    out = [_dot(v.astype(BF16), wout_ref[...]) for v in m]
    res = []
    for d, o in zip(data, out):
        r = DEEPNORM_ALPHA * d[6] + gate * o
        mu = jnp.mean(r, axis=-1, keepdims=True)
        rc = r - mu
        var = jnp.mean(rc * rc, axis=-1, keepdims=True)
        res.append(rc * lax.rsqrt(var + LN_EPS) * lng_ref[...] + lnb_ref[...])
    return res


TB = 512
NQB = TB // WINDOW
W_OFF = D_MODEL
ONES_ROWS = 16
POST_ROWS = 256
BLK_GROUP = 1
PROJ_W = 256


def _const_spec(shape):
    nd = len(shape)
    return pl.BlockSpec(shape, lambda *_: (0,) * nd, pipeline_mode=pl.Buffered(1))


def _bucket_thresholds():
    tbl = _bucket_table()
    return [int(np.argmax(tbl >= k)) for k in range(1, N_BUCKETS)]


def _fill_bias_table(rb_ref, bias_s):
    si = lax.broadcasted_iota(jnp.int32, (2 * WINDOW, WINDOW), 0)
    qi = lax.broadcasted_iota(jnp.int32, (2 * WINDOW, WINDOW), 1)
    dist = qi + WINDOW - si
    bid = jnp.zeros((2 * WINDOW, WINDOW), jnp.int32)
    for thr in _bucket_thresholds():
        bid = bid + (dist >= thr).astype(jnp.int32)
    valid = (dist >= 0) & (dist <= WINDOW)
    own = si >= WINDOW

    def per_head(h, c):
        t = jnp.zeros((2 * WINDOW, WINDOW), F32)
        for b in range(N_BUCKETS):
            t = jnp.where(bid == b, rb_ref[b * N_HEADS + h], t)
        t = jnp.where(valid, t, NEG_INF)
        kh, j = h // KV_GROUP, h % KV_GROUP
        bias_s[0, kh, j] = t
        bias_s[1, kh, j] = jnp.where(own, t, NEG_INF)
        return c

    lax.fori_loop(0, N_HEADS, per_head, 0)


def _main_body(rb_ref, x_ref, yssm_ref, shift_ref, scale_ref, gate_ref, wmain_ref, sink_ref,
               wglu_ref, bglu_ref, wbs_ref, wba_ref, wout_ref, lng_ref, lnb_ref,
               y_ref, ktail_ref, vtail_ref,
               zs_s, q_s, za_s, gs_s, ga_s, k_s, vt_s, ot_s, o_s, bias_s):
    i = pl.program_id(1)

    @pl.when((pl.program_id(0) == 0) & (i == 0))
    def _():
        _fill_bias_table(rb_ref, bias_s)

    @pl.when(i == 0)
    def _():
        k_s[0:WINDOW, :] = jnp.zeros((WINDOW, KV_WIDTH), BF16)
        vt_s[:, 0:WINDOW] = jnp.zeros((KV_WIDTH, WINDOW), BF16)

    x = x_ref[...]
    hb = (x * (1.0 + scale_ref[...]) + shift_ref[...]).astype(BF16)

    def project(dst, c0, c, width=PROJ_W):
        lo = W_OFF + c0 + c * width
        dst[:, c * width:(c + 1) * width] = _dot(hb, wmain_ref[:, lo:lo + width]).astype(BF16)

    for c in range(D_MODEL // 512):
        project(q_s, D_MODEL, c, 512)
    deferred = [(dst, c0, c) for dst, c0 in ((zs_s, 0), (za_s, 2 * D_MODEL), (gs_s, 3 * D_MODEL), (ga_s, 4 * D_MODEL))
                for c in range(D_MODEL // PROJ_W)]

    kv = _dot(hb, wmain_ref[:, W_OFF + 5 * D_MODEL:W_OFF + 5 * D_MODEL + 2 * KV_WIDTH]).astype(BF16)
    k_s[WINDOW:WINDOW + TB, :] = kv[:, :KV_WIDTH]
    vt_s[:, WINDOW:WINDOW + TB] = kv[:, KV_WIDTH:].T

    first = jnp.where(i == 0, 1, 0)
    lane = lax.broadcasted_iota(jnp.int32, (2 * WINDOW, 128), 1)
    ones = jnp.ones((ONES_ROWS, 2 * WINDOW), BF16)
    n_def = len(deferred) * BLK_GROUP // NQB
    for grp in range(NQB // BLK_GROUP):
        units = [(blk, kh) for blk in range(grp * BLK_GROUP, (grp + 1) * BLK_GROUP) for kh in range(N_KV_HEADS)]
        mine = deferred[grp * n_def:(grp + 1) * n_def]
        sts = []
        for blk, kh in units:
            rows = slice(blk * WINDOW, (blk + 1) * WINDOW)
            kcat = k_s[blk * WINDOW:blk * WINDOW + 2 * WINDOW, :]
            pair, odd = kh // 2, kh % 2
            kcm = jnp.where((lane >= HEAD_DIM) == bool(odd), kcat[:, pair * 128:(pair + 1) * 128], jnp.zeros((), BF16))
            qp = jnp.concatenate(
                [q_s[rows, j * KV_WIDTH + pair * 128:j * KV_WIDTH + (pair + 1) * 128] for j in range(KV_GROUP)],
                axis=0)
            sel = first if blk == 0 else 0
            bias = jnp.concatenate([bias_s[sel, kh, j] for j in range(KV_GROUP)], axis=1)
            sts.append(_dot_nt(kcm, qp) + bias)
        share = [mine[k * len(mine) // 3:(k + 1) * len(mine) // 3] for k in range(3)]
        ms = [jnp.maximum(jnp.max(st, axis=0, keepdims=True), sink_ref[kh]) for st, (_, kh) in zip(sts, units)]
        for spec in share[0]:
            project(*spec)
        es = [jnp.exp(st - m).astype(BF16) for st, m in zip(sts, ms)]
        for spec in share[1]:
            project(*spec)
        rs = []
        for e, (blk, kh) in zip(es, units):
            vtc = vt_s[kh * HEAD_DIM:(kh + 1) * HEAD_DIM, blk * WINDOW:blk * WINDOW + 2 * WINDOW]
            rs.append(_dot(jnp.concatenate([vtc, ones], axis=0), e))
        for spec in share[2]:
            project(*spec)
        for r, m, (blk, kh) in zip(rs, ms, units):
            den = r[HEAD_DIM:HEAD_DIM + 1, :] + jnp.exp(sink_ref[kh] - m)
            ot = (r[:HEAD_DIM, :] / den).astype(BF16)
            for j in range(KV_GROUP):
                ot_s[blk, j, kh * HEAD_DIM:(kh + 1) * HEAD_DIM, :] = ot[:, j * WINDOW:(j + 1) * WINDOW]
        for blk in range(grp * BLK_GROUP, (grp + 1) * BLK_GROUP):
            for j in range(KV_GROUP):
                o_s[blk * WINDOW:(blk + 1) * WINDOW, j * KV_WIDTH:(j + 1) * KV_WIDTH] = ot_s[blk, j].T

    k_s[0:WINDOW, :] = k_s[TB:TB + WINDOW, :]
    vt_s[:, 0:WINDOW] = vt_s[:, TB:TB + WINDOW]

    def loader(c):
        rs = slice(c * POST_ROWS, (c + 1) * POST_ROWS)
        return lambda: (yssm_ref[rs, :], zs_s[rs, :], o_s[rs, :], za_s[rs, :], gs_s[rs, :], ga_s[rs, :], x_ref[rs, :])

    ys = _post([loader(c) for c in range(TB // POST_ROWS)], gate_ref[...],
               wglu_ref, bglu_ref, wbs_ref, wba_ref, wout_ref, lng_ref, lnb_ref)
    for c, yc in enumerate(ys):
        y_ref[c * POST_ROWS:(c + 1) * POST_ROWS, :] = yc

    @pl.when(i == pl.num_programs(1) - 1)
    def _():
        ktail_ref[...] = k_s[0:WINDOW, :].astype(F32)
        vtail_ref[...] = vt_s[:, 0:WINDOW].T.astype(F32)


def _main_prompt(rel_bias_flat, x2d, yssm, modp, wmain, sink_rows, wglu, bglu, wbs, wba, wout, lng, lnb,
                 n_batch, seq):
    steps = seq // TB
    row_spec = pl.BlockSpec((TB, D_MODEL), lambda b, i: (b * steps + i, 0))
    yssm_spec = pl.BlockSpec((TB, D_MODEL), lambda b, i: (b * (steps + SSM_TOK // TB) + i, 0))
    mod_spec = lambda j: pl.BlockSpec((None, 1, D_MODEL), lambda b, i, j=j: (b * 3 + j, 0, 0))
    tail_spec = pl.BlockSpec((None, WINDOW, KV_WIDTH), lambda b, i: (b, 0, 0))
    consts = (wmain, sink_rows, wglu, bglu, wbs, wba, wout, lng, lnb)
    return pl.pallas_call(
        _main_body,
        grid=(n_batch, steps),
        in_specs=[pl.BlockSpec(memory_space=pltpu.SMEM), row_spec, yssm_spec, mod_spec(0), mod_spec(1), mod_spec(2)]
        + [_const_spec(c.shape) for c in consts],
        out_specs=[row_spec, tail_spec, tail_spec],
        out_shape=[jax.ShapeDtypeStruct((n_batch * seq, D_MODEL), F32),
                   jax.ShapeDtypeStruct((n_batch, WINDOW, KV_WIDTH), F32),
                   jax.ShapeDtypeStruct((n_batch, WINDOW, KV_WIDTH), F32)],
        scratch_shapes=[pltpu.VMEM((TB, D_MODEL), BF16)] * 5
        + [pltpu.VMEM((TB + WINDOW, KV_WIDTH), BF16), pltpu.VMEM((KV_WIDTH, TB + WINDOW), BF16),
           pltpu.VMEM((NQB, KV_GROUP, KV_WIDTH, WINDOW), BF16), pltpu.VMEM((TB, D_MODEL), BF16),
           pltpu.VMEM((2, N_KV_HEADS, KV_GROUP, 2 * WINDOW, WINDOW), F32)],
        compiler_params=pltpu.CompilerParams(dimension_semantics=("arbitrary", "arbitrary"),
                                             vmem_limit_bytes=VMEM_LIMIT),
        name="main_prompt",
    )(rel_bias_flat, x2d, yssm, modp, modp, modp, *consts)


GT = 8
PREP_TILES = 2


def _sample_ssm_body(u_ref, h0r_ref, h0i_ref, ar_ref, ai_ref, bre_ref, bim_ref, cblk_ref, d_ref,
                     y_ref, hr_ref, hi_ref):
    ub = u_ref[...]
    ys = []
    for a in range(GT):
        ua = ub[:, a * 128:(a + 1) * 128]
        st = slice(a * 512, (a + 1) * 512)
        xr = _dot(ua, bre_ref[a])
        xi = _dot(ua, bim_ref[a])
        h0r = h0r_ref[:, st]
        h0i = h0i_ref[:, st]
        ar = ar_ref[:, st]
        ai = ai_ref[:, st]
        hr = ar * h0r - ai * h0i + xr
        hi = ar * h0i + ai * h0r + xi
        hr_ref[:, st] = hr
        hi_ref[:, st] = hi
        hcat = jnp.concatenate([hr, hi], axis=1).astype(BF16)
        ys.append(_dot_nt(hcat, cblk_ref[a]))
    y_ref[...] = jnp.concatenate(ys, axis=1) + d_ref[...] * ub.astype(F32)


def _sample_ssm(proj_s, h0r, h0i, ar, ai, bre, bim, cblk, d_row):
    rows = proj_s.shape[0]
    ns = SSM_GROUPS * SSM_STATE
    full = lambda shape: pl.BlockSpec(shape, lambda i: (0,) * len(shape))
    return pl.pallas_call(
        _sample_ssm_body,
        grid=(1,),
        in_specs=[pl.BlockSpec((rows, D_MODEL), lambda i: (0, COL_U)),
                  full((rows, ns)), full((rows, ns)), full((1, ns)), full((1, ns)),
                  full(bre.shape), full(bim.shape), full(cblk.shape), full((1, D_MODEL))],
        out_specs=[full((rows, D_MODEL)), full((rows, ns)), full((rows, ns))],
        out_shape=[jax.ShapeDtypeStruct((rows, D_MODEL), F32),
                   jax.ShapeDtypeStruct((rows, ns), F32),
                   jax.ShapeDtypeStruct((rows, ns), F32)],
        compiler_params=pltpu.CompilerParams(vmem_limit_bytes=VMEM_LIMIT),
        name="sample_ssm",
    )(proj_s, h0r, h0i, ar, ai, bre, bim, cblk, d_row)


SB = 32
KROWS = WINDOW + 8
SEQ_GROUP = 4


def _sample_attn_body(q_ref, kn_ref, vn_ref, ck_ref, cv_ref, bias_ref, sink_ref, seg_ref, segt_ref,
                      o_ref, nk_ref, nv_ref):
    qf = q_ref[...].astype(F32)
    knf = kn_ref[...].astype(F32)
    vnf = vn_ref[...].astype(F32)
    rowk = lax.broadcasted_iota(jnp.int32, (KROWS, 128), 0)
    roww = lax.broadcasted_iota(jnp.int32, (WINDOW, KV_WIDTH), 0)
    sink = sink_ref[...]
    last = roww == WINDOW - 1
    for g0 in range(0, SB, SEQ_GROUP):
        seqs = range(g0, g0 + SEQ_GROUP)
        scores = []
        for b in seqs:
            kall = jnp.concatenate([ck_ref[b], jnp.broadcast_to(knf[b:b + 1], (8, KV_WIDTH))], axis=0)
            qrow = qf[b:b + 1]
            prod = jnp.concatenate([kall * qrow[:, j * KV_WIDTH:(j + 1) * KV_WIDTH] for j in range(KV_GROUP)], axis=1)
            s = _dot(prod.astype(BF16), seg_ref[...]) + bias_ref[...]
            scores.append(jnp.where(rowk <= WINDOW, s, NEG_INF))
        probs = []
        for s in scores:
            m = jnp.maximum(jnp.max(s, axis=0, keepdims=True), sink)
            e = jnp.exp(s - m)
            den = jnp.sum(e, axis=0, keepdims=True) + jnp.exp(sink - m)
            probs.append((e / den).astype(BF16))
        pexps = [_dot(p, segt_ref[...]) for p in probs]
        for b, pexp in zip(seqs, pexps):
            vb = cv_ref[b]
            vall = jnp.concatenate([vb, jnp.broadcast_to(vnf[b:b + 1], (8, KV_WIDTH))], axis=0)
            v4 = jnp.concatenate([vall] * KV_GROUP, axis=1)
            o_ref[b:b + 1, :] = jnp.sum(pexp * v4, axis=0, keepdims=True)
            nv_ref[b] = jnp.where(last, jnp.broadcast_to(vnf[b:b + 1], (WINDOW, KV_WIDTH)), pltpu.roll(vb, WINDOW - 1, 0))
            nk_ref[b] = jnp.where(last, jnp.broadcast_to(knf[b:b + 1], (WINDOW, KV_WIDTH)),
                                  pltpu.roll(ck_ref[b], WINDOW - 1, 0))


def _sample_attn(proj_s, ck, cv, bias_s, sink_row, seg, segt):
    rows = proj_s.shape[0]
    cache_spec = pl.BlockSpec((SB, WINDOW, KV_WIDTH), lambda i: (i, 0, 0))
    return pl.pallas_call(
        _sample_attn_body,
        grid=(rows // SB,),
        in_specs=[pl.BlockSpec((SB, D_MODEL), lambda i: (i, COL_Q)),
                  pl.BlockSpec((SB, KV_WIDTH), lambda i: (i, COL_K)),
                  pl.BlockSpec((SB, KV_WIDTH), lambda i: (i, COL_V)),
                  cache_spec, cache_spec,
                  pl.BlockSpec(bias_s.shape, lambda i: (0, 0)),
                  pl.BlockSpec(sink_row.shape, lambda i: (0, 0)),
                  pl.BlockSpec(seg.shape, lambda i: (0, 0)),
                  pl.BlockSpec(segt.shape, lambda i: (0, 0))],
        out_specs=[pl.BlockSpec((SB, D_MODEL), lambda i: (i, 0)), cache_spec, cache_spec],
        out_shape=[jax.ShapeDtypeStruct((rows, D_MODEL), F32),
                   jax.ShapeDtypeStruct(ck.shape, F32),
                   jax.ShapeDtypeStruct(cv.shape, F32)],
        compiler_params=pltpu.CompilerParams(vmem_limit_bytes=VMEM_LIMIT),
        name="sample_attn",
    )(proj_s, proj_s, proj_s, ck, cv, bias_s, sink_row, seg, segt)


def _sample_final_body(x_ref, yssm_ref, zs_ref, o_ref, za_ref, gs_ref, ga_ref, gate_ref,
                       wglu_ref, bglu_ref, wbs_ref, wba_ref, wout_ref, lng_ref, lnb_ref, y_ref):
    load = lambda: (yssm_ref[...], zs_ref[...], o_ref[...], za_ref[...], gs_ref[...], ga_ref[...], x_ref[...])
    y_ref[...] = _post([load], gate_ref[...], wglu_ref, bglu_ref, wbs_ref, wba_ref, wout_ref, lng_ref, lnb_ref)[0]


def _final_sample(x2d, yssm, o_a, proj_s, mods, wglu, bglu, wbs, wba, wout, lng, lnb):
    rows = x2d.shape[0]
    full = lambda shape: pl.BlockSpec(shape, lambda i: (0,) * len(shape))
    pcol = lambda c: pl.BlockSpec((rows, D_MODEL), lambda i, c=c: (0, c))
    return pl.pallas_call(
        _sample_final_body,
        grid=(1,),
        in_specs=[full((rows, D_MODEL)), full((rows, D_MODEL)), pcol(COL_ZS), full((rows, D_MODEL)),
                  pcol(COL_ZA), pcol(COL_GS), pcol(COL_GA),
                  pl.BlockSpec((rows, D_MODEL), lambda i: (0, 2)),
                  full(wglu.shape), full(bglu.shape), full(wbs.shape), full(wba.shape), full(wout.shape),
                  full(lng.shape), full(lnb.shape)],
        out_specs=full((rows, D_MODEL)),
        out_shape=jax.ShapeDtypeStruct((rows, D_MODEL), F32),
        compiler_params=pltpu.CompilerParams(vmem_limit_bytes=VMEM_LIMIT),
        name="final_sample",
    )(x2d, yssm, proj_s, o_a, proj_s, proj_s, proj_s, mods, wglu, bglu, wbs, wba, wout, lng, lnb)


def _bucket_table():
    max_exact = N_BUCKETS // 2
    dist = np.arange(WINDOW + 1)
    df = np.maximum(dist, 1).astype(np.float32)
    large = max_exact + (np.log(df / np.float32(max_exact)) / np.float32(math.log(WINDOW / max_exact))
                         * np.float32(N_BUCKETS - max_exact)).astype(np.int32)
    large = np.minimum(large, N_BUCKETS - 1)
    return np.where(dist < max_exact, dist, large)


def _dot3_nt(a, b):
    ah = a.astype(BF16)
    al = (a - ah.astype(F32)).astype(BF16)
    bh = b.astype(BF16)
    bl = (b - bh.astype(F32)).astype(BF16)
    return _dot_nt(ah, bh) + _dot_nt(ah, bl) + _dot_nt(al, bh)


def _ssm_prep_body(lre_ref, lim_ref, ld_ref, tau_ref, bt_ref, bts_ref, c2_ref, c2s_ref, d_ref,
                   lhs_ref, bpow_ref, cpow_ref, bre_ref, bim_ref, cblk_ref, p1_ref, p2_ref):
    lane = lax.broadcasted_iota(jnp.int32, (SSM_GROUP, SW), 1)
    left = lane < SSM_STATE
    sgn = jnp.where(left, 1.0, -1.0)
    row_t = lax.broadcasted_iota(jnp.int32, (CW, 128), 0)
    lane_t = lax.broadcasted_iota(jnp.int32, (CW, 128), 1)
    bre_ref[...] = jnp.zeros_like(bre_ref)
    bim_ref[...] = jnp.zeros_like(bim_ref)
    cblk_ref[...] = jnp.zeros_like(cblk_ref)
    half_sign = jnp.where(lax.broadcasted_iota(jnp.int32, (1, SW), 1) < SSM_STATE, -1.0, 1.0)
    tau = tau_ref[...]
    for gi in range(PREP_TILES * GT):
        ti_, gl = divmod(gi, GT)
        lr = jnp.concatenate([lre_ref[gi:gi + 1, :]] * 2, axis=1)
        li = jnp.concatenate([lim_ref[gi:gi + 1, :]] * 2, axis=1)
        dt = jnp.exp(ld_ref[gi:gi + 1, :])
        mag = jnp.exp((lr * dt) * tau)
        ang = (li * dt) * tau
        p1 = mag * jnp.cos(ang)
        pim = mag * jnp.sin(ang)
        p2 = pim * half_sign
        p1_ref[gi] = p1
        p2_ref[gi] = p2
        ar, ai = p1[1:2, :], pim[1:2, :]
        den = lr * lr + li * li
        nr = ar - 1.0
        k1 = (nr * lr + ai * li) / den
        k2 = (ai * lr - nr * li) / den * half_sign
        bt, bts = bt_ref[gi], bts_ref[gi]
        c2, c2s = c2_ref[gi], c2s_ref[gi]
        bb = k1 * bt + k2 * bts
        bbs = k1 * bts - k2 * bt
        ca = [sgn * (c2 * p1[t:t + 1, :] + c2s * p2[t:t + 1, :]) for t in range(CHUNK + 1)]
        cpow_ref[gi] = jnp.concatenate(ca[1:], axis=0).astype(BF16)
        bpow_ref[gi] = jnp.concatenate(
            [p1[CHUNK - 1 - s:CHUNK - s, :] * bb + p2[CHUNK - 1 - s:CHUNK - s, :] * bbs for s in range(CHUNK)],
            axis=0).astype(BF16)
        e = _dot3_nt(jnp.concatenate(ca[:CHUNK], axis=0), jnp.concatenate([bb] * CHUNK, axis=0))
        tiles = []
        for lt in range(CW // 128):
            cols = slice(lt * 128, (lt + 1) * 128)
            et = e[:, cols]
            blk = (lane_t + lt * 128) // SSM_GROUP
            at = jnp.where(row_t == lane_t + lt * 128, d_ref[gi, :, cols], 0.0)
            for s in range(lt * 128 // SSM_GROUP, (lt + 1) * 128 // SSM_GROUP):
                sh = et if s == 0 else jnp.concatenate(
                    [jnp.zeros((s * SSM_GROUP, 128), F32), et[:CW - s * SSM_GROUP, :]], axis=0)
                at = at + jnp.where(blk == s, sh, 0.0)
            tiles.append(at)
        lhs_ref[gi] = jnp.concatenate(tiles, axis=1).astype(BF16)
        rs = slice(gl * SSM_GROUP, (gl + 1) * SSM_GROUP)
        ts = slice((gl // 2) * SW, (gl // 2 + 1) * SW)
        ti = slice(GT * SSM_STATE + (gl // 2) * SW, GT * SSM_STATE + (gl // 2 + 1) * SW)
        if gl % 2 == 0:
            bre_ref[ti_, rs, ts] = jnp.where(left, bb, 0.0).astype(BF16)
            bim_ref[ti_, rs, ts] = jnp.where(left, bbs, 0.0).astype(BF16)
            cblk_ref[ti_, rs, ts] = jnp.where(left, c2, 0.0).astype(BF16)
            cblk_ref[ti_, rs, ti] = jnp.where(left, -c2s, 0.0).astype(BF16)
        else:
            bre_ref[ti_, rs, ts] = jnp.where(left, 0.0, bbs).astype(BF16)
            bim_ref[ti_, rs, ts] = jnp.where(left, 0.0, bb).astype(BF16)
            cblk_ref[ti_, rs, ts] = jnp.where(left, 0.0, c2s).astype(BF16)
            cblk_ref[ti_, rs, ti] = jnp.where(left, 0.0, -c2).astype(BF16)


def _ssm_params(lam_re, lam_im, log_delta, b_re, b_im, c_re, c_im, d_skip):
    g = SSM_GROUPS
    nrow = len(POW_TAUS)
    tau = jnp.asarray(np.broadcast_to(np.asarray(POW_TAUS, np.float32)[:, None], (nrow, SW)))
    brt, bit = jnp.swapaxes(b_re, 1, 2), jnp.swapaxes(b_im, 1, 2)
    bt = jnp.concatenate([brt, bit], axis=-1)
    bts = jnp.concatenate([bit, brt], axis=-1)
    c2 = jnp.concatenate([c_re, c_im], axis=-1)
    c2s = jnp.concatenate([c_im, c_re], axis=-1)
    dtile = jnp.tile(d_skip.reshape(g, 1, SSM_GROUP), (1, 1, CHUNK))

    gstep = PREP_TILES * GT
    gspec = lambda r, w: pl.BlockSpec((gstep, r, w), lambda a: (a, 0, 0))
    rspec = lambda w: pl.BlockSpec((gstep, w), lambda a: (a, 0))
    tile_spec = lambda w: pl.BlockSpec((PREP_TILES, GT * SSM_GROUP, w), lambda a: (a, 0, 0))
    lhs, bpow, cpow, bre, bim, cblk, p1, p2 = pl.pallas_call(
        _ssm_prep_body,
        grid=(g // gstep,),
        in_specs=[rspec(SSM_STATE), rspec(SSM_STATE), rspec(1), pl.BlockSpec((nrow, SW), lambda a: (0, 0)),
                  gspec(SSM_GROUP, SW), gspec(SSM_GROUP, SW), gspec(SSM_GROUP, SW), gspec(SSM_GROUP, SW),
                  gspec(1, CW)],
        out_specs=[gspec(CW, CW), gspec(CW, SW), gspec(CW, SW),
                   tile_spec(GT * SSM_STATE), tile_spec(GT * SSM_STATE), tile_spec(2 * GT * SSM_STATE),
                   gspec(nrow, SW), gspec(nrow, SW)],
        out_shape=[jax.ShapeDtypeStruct((g, CW, CW), BF16), jax.ShapeDtypeStruct((g, CW, SW), BF16),
                   jax.ShapeDtypeStruct((g, CW, SW), BF16),
                   jax.ShapeDtypeStruct((GT, GT * SSM_GROUP, GT * SSM_STATE), BF16),
                   jax.ShapeDtypeStruct((GT, GT * SSM_GROUP, GT * SSM_STATE), BF16),
                   jax.ShapeDtypeStruct((GT, GT * SSM_GROUP, 2 * GT * SSM_STATE), BF16),
                   jax.ShapeDtypeStruct((g, nrow, SW), F32), jax.ShapeDtypeStruct((g, nrow, SW), F32)],
        name="ssm_prep",
    )(lam_re.astype(F32), lam_im.astype(F32), log_delta.astype(F32).reshape(g, 1), tau, bt, bts, c2, c2s, dtile)
    ar_row = p1[:, 1, :SSM_STATE].reshape(1, -1)
    ai_row = p2[:, 1, SSM_STATE:].reshape(1, -1)
    return lhs, bpow, cpow, p1, p2, ar_row, ai_row, bre, bim, cblk


def _attn_params(rel_bias, sinks):
    hp = lax.Precision.HIGHEST
    rb = rel_bias.astype(F32)
    onehot = jnp.asarray(_bucket_table()[:, None] == np.arange(N_BUCKETS)[None, :], F32)
    tbl = jnp.dot(onehot, rb, precision=hp)
    sink_rows = jnp.repeat(sinks.astype(F32).reshape(N_KV_HEADS, 1, KV_GROUP), WINDOW, axis=2)
    ds = np.clip(WINDOW - np.arange(KROWS), 0, WINDOW)
    bs = jnp.dot(jnp.asarray(ds[:, None] == np.arange(WINDOW + 1)[None, :], F32), tbl, precision=hp)
    bs = jnp.transpose(bs.reshape(KROWS, N_KV_HEADS, KV_GROUP), (0, 2, 1)).reshape(KROWS, N_HEADS)
    bias_s = jnp.pad(bs, ((0, 0), (0, 128 - N_HEADS)))
    sk = jnp.transpose(sinks.astype(F32).reshape(N_KV_HEADS, KV_GROUP), (1, 0)).reshape(1, N_HEADS)
    sink_row = jnp.pad(sk, ((0, 0), (0, 128 - N_HEADS)))
    seg_np = np.zeros((D_MODEL, 128), np.float32)
    for j in range(KV_GROUP):
        for kh in range(N_KV_HEADS):
            r0 = j * KV_WIDTH + kh * HEAD_DIM
            seg_np[r0:r0 + HEAD_DIM, j * N_KV_HEADS + kh] = 1.0
    seg = jnp.asarray(seg_np, BF16)
    segt = jnp.asarray(seg_np.T, BF16)
    return sink_rows, bias_s, sink_row, seg, segt


WP_ROWS = 256


def _regroup_tiles(w):
    left = lax.broadcasted_iota(jnp.int32, (w.shape[0], 128), 1) < HEAD_DIM
    tile = lambda t: w[:, t * 128:(t + 1) * 128]
    out = []
    for j in range(KV_GROUP):
        for pair in range(N_KV_HEADS // 2):
            t1 = tile((2 * pair) * 2 + j // 2)
            t2 = tile((2 * pair + 1) * 2 + j // 2)
            if j % 2 == 0:
                out.append(jnp.where(left, t1, pltpu.roll(t2, HEAD_DIM, 1)))
            else:
                out.append(jnp.where(left, pltpu.roll(t1, HEAD_DIM, 1), t2))
    return jnp.concatenate(out, axis=1)


def _weights_body(win_ref, wglu_ref, wbs_ref, wba_ref, wout_ref,
                  winb_ref, wut_ref, wglub_ref, wbsb_ref, wbab_ref, woutb_ref):
    i = pl.program_id(0)
    seg = lambda a: win_ref[:, a:a + D_MODEL]
    u = seg(0)
    winb_ref[:, 0:D_MODEL] = u.astype(BF16)
    winb_ref[:, D_MODEL:2 * D_MODEL] = seg(D_MODEL).astype(BF16)
    winb_ref[:, 2 * D_MODEL:3 * D_MODEL] = (_regroup_tiles(seg(2 * D_MODEL)) * (HEAD_DIM ** -0.5)).astype(BF16)
    winb_ref[:, 3 * D_MODEL:4 * D_MODEL] = _regroup_tiles(seg(3 * D_MODEL + 2 * KV_WIDTH)).astype(BF16)
    winb_ref[:, 4 * D_MODEL:5 * D_MODEL] = seg(4 * D_MODEL + 2 * KV_WIDTH).astype(BF16)
    winb_ref[:, 5 * D_MODEL:6 * D_MODEL] = seg(5 * D_MODEL + 2 * KV_WIDTH).astype(BF16)
    winb_ref[:, 6 * D_MODEL:6 * D_MODEL + 2 * KV_WIDTH] = \
        win_ref[:, 3 * D_MODEL:3 * D_MODEL + 2 * KV_WIDTH].astype(BF16)
    wut_ref[...] = u.astype(BF16).T
    wglub_ref[...] = wglu_ref[...].astype(BF16)
    wbsb_ref[...] = wbs_ref[...].astype(BF16)
    woutb_ref[...] = wout_ref[...].astype(BF16)
    wba = wba_ref[...].astype(BF16)
    for j in range(KV_GROUP):
        r0 = pl.multiple_of(j * KV_WIDTH + i * HEAD_DIM, HEAD_DIM)
        wbab_ref[pl.ds(r0, HEAD_DIM), :] = wba[j * HEAD_DIM:(j + 1) * HEAD_DIM, :]


def _weights_prep(w_in0, w_glu0, w_bs0, w_ba0, w_out0):
    assert WP_ROWS == KV_GROUP * HEAD_DIM
    rows = lambda w: pl.BlockSpec((WP_ROWS, w), lambda i: (i, 0))
    sq = jax.ShapeDtypeStruct((D_MODEL, D_MODEL), BF16)
    return pl.pallas_call(
        _weights_body,
        grid=(D_MODEL // WP_ROWS,),
        in_specs=[rows(D_IN), rows(D_MODEL), rows(D_MODEL), rows(D_MODEL), rows(D_MODEL)],
        out_specs=[rows(D_IN), pl.BlockSpec((D_MODEL, WP_ROWS), lambda i: (0, i)), rows(D_MODEL), rows(D_MODEL),
                   pl.BlockSpec((D_MODEL, D_MODEL), lambda i: (0, 0)), rows(D_MODEL)],
        out_shape=[jax.ShapeDtypeStruct((D_MODEL, D_IN), BF16), sq, sq, sq, sq, sq],
        compiler_params=pltpu.CompilerParams(dimension_semantics=("arbitrary",), vmem_limit_bytes=VMEM_LIMIT),
        name="weights_prep",
    )(w_in0, w_glu0, w_bs0, w_ba0, w_out0)


def kernel(x_prompt, x_sample, c_prompt, c_sample, state_ssm_re, state_ssm_im, cache_swa_k, cache_swa_v,
           w_ada, b_ada, w_in, ssm_lambda_re, ssm_lambda_im, ssm_log_delta, ssm_b_re, ssm_b_im,
           ssm_c_re, ssm_c_im, ssm_d, w_glu, b_glu, attn_sinks, rel_bias, w_branch_s, w_branch_a,
           w_out, ln_g, ln_b):
    assert w_ada.shape[0] == 1, "single-layer trunk"
    n_batch, seq, _ = x_prompt.shape
    n_dec = x_sample.shape[0]

    w_in_b, wut, wglu, wbs, wba, wout = _weights_prep(w_in[0], w_glu[0], w_branch_s[0], w_branch_a[0], w_out[0])
    bglu = b_glu[0].reshape(1, -1).astype(F32)
    lng = ln_g[0].reshape(1, -1).astype(F32)
    lnb = ln_b[0].reshape(1, -1).astype(F32)
    lhs, bpow, cpow, p1, p2, ar_row, ai_row, bre, bim, cblk = _ssm_params(
        ssm_lambda_re[0], ssm_lambda_im[0], ssm_log_delta[0], ssm_b_re[0], ssm_b_im[0],
        ssm_c_re[0], ssm_c_im[0], ssm_d[0])
    sink_rows, bias_s, sink_row, seg, segt = _attn_params(rel_bias, attn_sinks[0])

    c_all = jnp.concatenate([c_prompt, jnp.zeros((8 - n_batch, D_MODEL), F32), c_sample], axis=0)
    mod = _ada(c_all, w_ada[0], b_ada[0])
    modp = mod[:n_batch].reshape(n_batch * 3, 1, D_MODEL)
    mods = mod[8:8 + n_dec]

    xp = x_prompt.reshape(n_batch * seq, D_MODEL)
    yssm, hfin = _ssm_prompt(x_prompt, modp, wut, lhs, bpow, cpow, p1, p2)
    yp, ktail, vtail = _main_prompt(rel_bias.astype(F32).reshape(-1), xp, yssm, modp, w_in_b, sink_rows,
                                    wglu, bglu, wbs, wba, wout, lng, lnb, n_batch, seq)
    yp = yp.reshape(n_batch, seq, D_MODEL)
    p_hr = hfin[:, :, 0, :SSM_STATE][None]
    p_hi = hfin[:, :, 0, SSM_STATE:][None]
    p_k = ktail.reshape(1, n_batch, WINDOW, N_KV_HEADS, HEAD_DIM)
    p_v = vtail.reshape(1, n_batch, WINDOW, N_KV_HEADS, HEAD_DIM)

    xs = x_sample.reshape(n_dec, D_MODEL)
    proj_s = _proj_sample(xs, mods, w_in_b)
    ns = SSM_GROUPS * SSM_STATE
    ys_s, s_hr, s_hi = _sample_ssm(proj_s, state_ssm_re[0].reshape(n_dec, ns), state_ssm_im[0].reshape(n_dec, ns),
                                   ar_row, ai_row, bre, bim, cblk, ssm_d[0].reshape(1, -1).astype(F32))
    o_s, nk, nv = _sample_attn(proj_s, cache_swa_k[0].reshape(n_dec, WINDOW, KV_WIDTH),
                               cache_swa_v[0].reshape(n_dec, WINDOW, KV_WIDTH), bias_s, sink_row, seg, segt)
    ysmp = _final_sample(xs, ys_s, o_s, proj_s, mods, wglu, bglu, wbs, wba, wout, lng, lnb)

    return (yp, ysmp.reshape(n_dec, 1, D_MODEL),
            p_hr, p_hi, p_k, p_v,
            s_hr.reshape(1, n_dec, SSM_GROUPS, SSM_STATE), s_hi.reshape(1, n_dec, SSM_GROUPS, SSM_STATE),
            nk.reshape(1, n_dec, WINDOW, N_KV_HEADS, HEAD_DIM), nv.reshape(1, n_dec, WINDOW, N_KV_HEADS, HEAD_DIM))
```
